```python
import jax, jax.numpy as jnp
from jax import lax
import numpy as np

D_MODEL = 2048
BATCH = 8
SEQ = 2048
DEPTH = 1

DN_HEADS = 8
DN_HEAD_DIM = 128
DN_WIDTH = DN_HEADS * DN_HEAD_DIM
DN_CONV = 4
DN_CHUNK = 64
CF_WIDTH = 1024
CF_KERNEL = 31
FFN_DIM = 5632
FFN_CONV = 3
EPS = 1e-6

IN_SIZES = [3 * DN_WIDTH,
            DN_WIDTH,
            DN_HEADS,
            DN_HEADS,
            2 * CF_WIDTH,
            D_MODEL,
            D_MODEL]
N_IN = sum(IN_SIZES)
IN_SPLITS = [int(v) for v in np.cumsum(IN_SIZES)[:-1]]

kernel_name = "hybrid_deltanet_conformer_convffn_adaln"


def rmsnorm(x, g):
    xf = x.astype(jnp.float32)
    y = xf * lax.rsqrt(jnp.mean(xf * xf, axis=-1, keepdims=True) + EPS)
    return (y * g.astype(jnp.float32)).astype(x.dtype)


def layernorm(x, g, b):
    xf = x.astype(jnp.float32)
    mu = jnp.mean(xf, axis=-1, keepdims=True)
    xc = xf - mu
    y = xc * lax.rsqrt(jnp.mean(xc * xc, axis=-1, keepdims=True) + EPS)
    return (y * g.astype(jnp.float32) + b.astype(jnp.float32)).astype(x.dtype)


def l2norm(x):
    return x * lax.rsqrt(jnp.sum(x * x, axis=-1, keepdims=True) + EPS)


def causal_dwconv(x, w):
    k = w.shape[0]
    return lax.conv_general_dilated(
        x, w[:, None, :].astype(x.dtype), window_strides=(1,), padding=[(k - 1, 0)],
        dimension_numbers=("NWC", "WIO", "NWC"), feature_group_count=x.shape[-1])


def chunk_gated_delta_rule(q, k, v, g, beta):
    b, s, h, dk = q.shape
    dv = v.shape[-1]
    c = DN_CHUNK
    n = s // c

    def to_chunks(t):
        return jnp.moveaxis(t.reshape((b, n, c, h) + t.shape[3:]), 3, 1)

    q, k, v, g, beta = (to_chunks(t) for t in (q, k, v, g, beta))
    g = jnp.cumsum(g, axis=-1)
    causal = jnp.tril(jnp.ones((c, c), dtype=bool))
    strict = jnp.tril(jnp.ones((c, c), dtype=bool), -1)
    diff = g[..., :, None] - g[..., None, :]
    decay = jnp.where(causal, jnp.exp(jnp.where(causal, diff, 0.0)), 0.0)

    kb = k * beta[..., None]
    m = jnp.einsum("bhnid,bhnjd->bhnij", kb, k) * decay
    m = jnp.where(strict, m, 0.0) + jnp.eye(c, dtype=jnp.float32)
    rhs = jnp.concatenate([v * beta[..., None], kb * jnp.exp(g)[..., None]], axis=-1)
    sol = lax.linalg.triangular_solve(m, rhs, left_side=True, lower=True, unit_diagonal=True)
    u0, w = sol[..., :dv], sol[..., dv:]

    attn = jnp.einsum("bhnid,bhnjd->bhnij", q, k) * decay
    q_dec = q * jnp.exp(g)[..., None]
    g_last = g[..., -1]
    k_dec = k * jnp.exp(g_last[..., None] - g)[..., None]

    def step(state, inp):
        u0_i, w_i, attn_i, qd_i, kd_i, gl_i = inp
        u = u0_i - jnp.einsum("bhck,bhkv->bhcv", w_i, state)
        o = (jnp.einsum("bhck,bhkv->bhcv", qd_i, state)
             + jnp.einsum("bhij,bhjv->bhiv", attn_i, u))
        state = (state * jnp.exp(gl_i)[..., None, None]
                 + jnp.einsum("bhck,bhcv->bhkv", kd_i, u))
        return state, o

    xs = tuple(jnp.moveaxis(t, 2, 0) for t in (u0, w, attn, q_dec, k_dec, g_last))
    s0 = jnp.zeros((b, h, dk, dv), jnp.float32)
    _, o = lax.scan(step, s0, xs)
    o = jnp.moveaxis(o, 0, 2)
    return jnp.moveaxis(o, 1, 3).reshape(b, s, h, dv)


def hybrid_mixer(hn, w_in, dn_conv_w, dn_a_log, dn_dt_bias, dn_norm_g, dn_w_o,
                 cf_conv_w, cf_ln_g, cf_ln_b, cf_w_o, w_out):
    b, s, _ = hn.shape
    proj = hn @ w_in
    qkv, z, beta_logit, a_logit, glu_in, gate_a, gate_b = jnp.split(proj, IN_SPLITS, axis=-1)

    qkv = jax.nn.silu(causal_dwconv(qkv, dn_conv_w)).astype(jnp.float32)
    q, k, v = (t.reshape(b, s, DN_HEADS, DN_HEAD_DIM) for t in jnp.split(qkv, 3, axis=-1))
    q = l2norm(q) * (DN_HEAD_DIM ** -0.5)
    k = l2norm(k)
    beta = jax.nn.sigmoid(beta_logit.astype(jnp.float32))
    g = -jnp.exp(dn_a_log.astype(jnp.float32)) * jax.nn.softplus(
        a_logit.astype(jnp.float32) + dn_dt_bias.astype(jnp.float32))
    o = chunk_gated_delta_rule(q, k, v, g, beta)
    o = o * lax.rsqrt(jnp.mean(o * o, axis=-1, keepdims=True) + EPS) * dn_norm_g.astype(jnp.float32)
    o = o * jax.nn.silu(z.astype(jnp.float32).reshape(b, s, DN_HEADS, DN_HEAD_DIM))
    branch_a = o.reshape(b, s, DN_WIDTH).astype(hn.dtype) @ dn_w_o

    val, gl = jnp.split(glu_in, 2, axis=-1)
    u = val * jax.nn.sigmoid(gl)
    u = causal_dwconv(u, cf_conv_w)
    u = jax.nn.silu(layernorm(u, cf_ln_g, cf_ln_b))
    branch_b = u @ cf_w_o

    merged = jax.nn.sigmoid(gate_a) * branch_a + jax.nn.sigmoid(gate_b) * branch_b
    return merged @ w_out


def conv_glu_ffn(hn, w_up, conv_w, w_down):
    gate, up = jnp.split(hn @ w_up, 2, axis=-1)
    gate = causal_dwconv(gate, conv_w)
    return (jax.nn.silu(gate) * up) @ w_down


def _fwd_setup_inputs(seed: int = 0) -> dict:
    key = jax.random.key(seed)
    ks = jax.random.split(key, 24)
    L, D = DEPTH, D_MODEL
    nrm = lambda k, shape, s: jax.random.normal(k, shape, jnp.float32) * s
    dt = jnp.exp(jax.random.uniform(ks[5], (L, DN_HEADS), jnp.float32,
                                    float(np.log(1e-3)), float(np.log(1e-1))))
    return {
        "x": nrm(ks[0], (BATCH, SEQ, D), 1.0),
        "c": nrm(ks[1], (BATCH, D), 1.0),
        "w_ada": nrm(ks[2], (L, D, 6 * D), 0.5 * D ** -0.5),
        "b_ada": nrm(ks[3], (L, 6 * D), 0.01),
        "norm1_g": 1.0 + nrm(ks[4], (L, D), 0.02),
        "w_in": nrm(ks[6], (L, D, N_IN), D ** -0.5),
        "dn_conv_w": nrm(ks[7], (L, DN_CONV, 3 * DN_WIDTH), DN_CONV ** -0.5),
        "dn_a_log": jnp.log(jax.random.uniform(ks[8], (L, DN_HEADS), jnp.float32, 1.0, 16.0)),
        "dn_dt_bias": jnp.log(jnp.expm1(dt)),
        "dn_norm_g": 1.0 + nrm(ks[9], (L, DN_HEAD_DIM), 0.02),
        "dn_w_o": nrm(ks[10], (L, DN_WIDTH, D), DN_WIDTH ** -0.5),
        "cf_conv_w": nrm(ks[11], (L, CF_KERNEL, CF_WIDTH), CF_KERNEL ** -0.5),
        "cf_ln_g": 1.0 + nrm(ks[12], (L, CF_WIDTH), 0.02),
        "cf_ln_b": nrm(ks[13], (L, CF_WIDTH), 0.02),
        "cf_w_o": nrm(ks[14], (L, CF_WIDTH, D), CF_WIDTH ** -0.5),
        "w_out": nrm(ks[15], (L, D, D), D ** -0.5),
        "norm2_g": 1.0 + nrm(ks[16], (L, D), 0.02),
        "ffn_w_up": nrm(ks[17], (L, D, 2 * FFN_DIM), D ** -0.5),
        "ffn_conv_w": nrm(ks[18], (L, FFN_CONV, FFN_DIM), FFN_CONV ** -0.5),
        "ffn_w_down": nrm(ks[19], (L, FFN_DIM, D), FFN_DIM ** -0.5),
        "final_norm_g": 1.0 + nrm(ks[20], (D,), 0.02),
    }


def _fwd_reference(x, c, w_ada, b_ada, norm1_g, w_in, dn_conv_w, dn_a_log, dn_dt_bias, dn_norm_g,
              dn_w_o, cf_conv_w, cf_ln_g, cf_ln_b, cf_w_o, w_out, norm2_g, ffn_w_up, ffn_conv_w,
              ffn_w_down, final_norm_g):
    c_act = jax.nn.silu(c)
    for l in range(DEPTH):
        mod = c_act @ w_ada[l] + b_ada[l]
        sh1, sc1, gt1, sh2, sc2, gt2 = jnp.split(mod[:, None, :], 6, axis=-1)
        hn = rmsnorm(x, norm1_g[l]) * (1.0 + sc1) + sh1
        x = x + gt1 * hybrid_mixer(hn, w_in[l], dn_conv_w[l], dn_a_log[l], dn_dt_bias[l],
                                   dn_norm_g[l], dn_w_o[l], cf_conv_w[l], cf_ln_g[l],
                                   cf_ln_b[l], cf_w_o[l], w_out[l])
        hn = rmsnorm(x, norm2_g[l]) * (1.0 + sc2) + sh2
        x = x + gt2 * conv_glu_ffn(hn, ffn_w_up[l], ffn_conv_w[l], ffn_w_down[l])
    return rmsnorm(x, final_norm_g)


import jax as _jax
import jax.numpy as _jnp

TWIN_FORMAT = 'train_step'
FWD_PARAMS = ['x', 'c', 'w_ada', 'b_ada', 'norm1_g', 'w_in', 'dn_conv_w', 'dn_a_log', 'dn_dt_bias', 'dn_norm_g', 'dn_w_o', 'cf_conv_w', 'cf_ln_g', 'cf_ln_b', 'cf_w_o', 'w_out', 'norm2_g', 'ffn_w_up', 'ffn_conv_w', 'ffn_w_down', 'final_norm_g']
TWIN_WEIGHTS = ['w_ada', 'b_ada', 'norm1_g', 'w_in', 'dn_conv_w', 'dn_a_log', 'dn_dt_bias', 'dn_norm_g', 'dn_w_o', 'cf_conv_w', 'cf_ln_g', 'cf_ln_b', 'cf_w_o', 'w_out', 'norm2_g', 'ffn_w_up', 'ffn_conv_w', 'ffn_w_down', 'final_norm_g']
TWIN_DIFF_INPUT = 'x'
TWIN_INPUTS = ['x', 'c', 'w_ada', 'b_ada', 'norm1_g', 'w_in', 'dn_conv_w', 'dn_a_log', 'dn_dt_bias', 'dn_norm_g', 'dn_w_o', 'cf_conv_w', 'cf_ln_g', 'cf_ln_b', 'cf_w_o', 'w_out', 'norm2_g', 'ffn_w_up', 'ffn_conv_w', 'ffn_w_down', 'final_norm_g', 'loss_target', 'm_w_ada', 'm_b_ada', 'm_norm1_g', 'm_w_in', 'm_dn_conv_w', 'm_dn_a_log', 'm_dn_dt_bias', 'm_dn_norm_g', 'm_dn_w_o', 'm_cf_conv_w', 'm_cf_ln_g', 'm_cf_ln_b', 'm_cf_w_o', 'm_w_out', 'm_norm2_g', 'm_ffn_w_up', 'm_ffn_conv_w', 'm_ffn_w_down', 'm_final_norm_g', 'v_w_ada', 'v_b_ada', 'v_norm1_g', 'v_w_in', 'v_dn_conv_w', 'v_dn_a_log', 'v_dn_dt_bias', 'v_dn_norm_g', 'v_dn_w_o', 'v_cf_conv_w', 'v_cf_ln_g', 'v_cf_ln_b', 'v_cf_w_o', 'v_w_out', 'v_norm2_g', 'v_ffn_w_up', 'v_ffn_conv_w', 'v_ffn_w_down', 'v_final_norm_g']
TWIN_OUTPUTS = ['loss', 'grad_x', 'grad_w_ada', 'grad_b_ada', 'grad_norm1_g', 'grad_w_in', 'grad_dn_conv_w', 'grad_dn_a_log', 'grad_dn_dt_bias', 'grad_dn_norm_g', 'grad_dn_w_o', 'grad_cf_conv_w', 'grad_cf_ln_g', 'grad_cf_ln_b', 'grad_cf_w_o', 'grad_w_out', 'grad_norm2_g', 'grad_ffn_w_up', 'grad_ffn_conv_w', 'grad_ffn_w_down', 'grad_final_norm_g', 'delta_w_ada', 'delta_b_ada', 'delta_norm1_g', 'delta_w_in', 'delta_dn_conv_w', 'delta_dn_a_log', 'delta_dn_dt_bias', 'delta_dn_norm_g', 'delta_dn_w_o', 'delta_cf_conv_w', 'delta_cf_ln_g', 'delta_cf_ln_b', 'delta_cf_w_o', 'delta_w_out', 'delta_norm2_g', 'delta_ffn_w_up', 'delta_ffn_conv_w', 'delta_ffn_w_down', 'delta_final_norm_g', 'new_m_w_ada', 'new_m_b_ada', 'new_m_norm1_g', 'new_m_w_in', 'new_m_dn_conv_w', 'new_m_dn_a_log', 'new_m_dn_dt_bias', 'new_m_dn_norm_g', 'new_m_dn_w_o', 'new_m_cf_conv_w', 'new_m_cf_ln_g', 'new_m_cf_ln_b', 'new_m_cf_w_o', 'new_m_w_out', 'new_m_norm2_g', 'new_m_ffn_w_up', 'new_m_ffn_conv_w', 'new_m_ffn_w_down', 'new_m_final_norm_g', 'new_v_w_ada', 'new_v_b_ada', 'new_v_norm1_g', 'new_v_w_in', 'new_v_dn_conv_w', 'new_v_dn_a_log', 'new_v_dn_dt_bias', 'new_v_dn_norm_g', 'new_v_dn_w_o', 'new_v_cf_conv_w', 'new_v_cf_ln_g', 'new_v_cf_ln_b', 'new_v_cf_w_o', 'new_v_w_out', 'new_v_norm2_g', 'new_v_ffn_w_up', 'new_v_ffn_conv_w', 'new_v_ffn_w_down', 'new_v_final_norm_g']
TWIN_LEAF_KINDS = {'loss': 'loss', 'grad_x': 'grad_x', 'grad_w_ada': 'grad_w', 'grad_b_ada': 'grad_w', 'grad_norm1_g': 'grad_w', 'grad_w_in': 'grad_w', 'grad_dn_conv_w': 'grad_w', 'grad_dn_a_log': 'grad_w', 'grad_dn_dt_bias': 'grad_w', 'grad_dn_norm_g': 'grad_w', 'grad_dn_w_o': 'grad_w', 'grad_cf_conv_w': 'grad_w', 'grad_cf_ln_g': 'grad_w', 'grad_cf_ln_b': 'grad_w', 'grad_cf_w_o': 'grad_w', 'grad_w_out': 'grad_w', 'grad_norm2_g': 'grad_w', 'grad_ffn_w_up': 'grad_w', 'grad_ffn_conv_w': 'grad_w', 'grad_ffn_w_down': 'grad_w', 'grad_final_norm_g': 'grad_w', 'delta_w_ada': 'delta_w', 'delta_b_ada': 'delta_w', 'delta_norm1_g': 'delta_w', 'delta_w_in': 'delta_w', 'delta_dn_conv_w': 'delta_w', 'delta_dn_a_log': 'delta_w', 'delta_dn_dt_bias': 'delta_w', 'delta_dn_norm_g': 'delta_w', 'delta_dn_w_o': 'delta_w', 'delta_cf_conv_w': 'delta_w', 'delta_cf_ln_g': 'delta_w', 'delta_cf_ln_b': 'delta_w', 'delta_cf_w_o': 'delta_w', 'delta_w_out': 'delta_w', 'delta_norm2_g': 'delta_w', 'delta_ffn_w_up': 'delta_w', 'delta_ffn_conv_w': 'delta_w', 'delta_ffn_w_down': 'delta_w', 'delta_final_norm_g': 'delta_w', 'new_m_w_ada': 'new_m', 'new_m_b_ada': 'new_m', 'new_m_norm1_g': 'new_m', 'new_m_w_in': 'new_m', 'new_m_dn_conv_w': 'new_m', 'new_m_dn_a_log': 'new_m', 'new_m_dn_dt_bias': 'new_m', 'new_m_dn_norm_g': 'new_m', 'new_m_dn_w_o': 'new_m', 'new_m_cf_conv_w': 'new_m', 'new_m_cf_ln_g': 'new_m', 'new_m_cf_ln_b': 'new_m', 'new_m_cf_w_o': 'new_m', 'new_m_w_out': 'new_m', 'new_m_norm2_g': 'new_m', 'new_m_ffn_w_up': 'new_m', 'new_m_ffn_conv_w': 'new_m', 'new_m_ffn_w_down': 'new_m', 'new_m_final_norm_g': 'new_m', 'new_v_w_ada': 'new_v', 'new_v_b_ada': 'new_v', 'new_v_norm1_g': 'new_v', 'new_v_w_in': 'new_v', 'new_v_dn_conv_w': 'new_v', 'new_v_dn_a_log': 'new_v', 'new_v_dn_dt_bias': 'new_v', 'new_v_dn_norm_g': 'new_v', 'new_v_dn_w_o': 'new_v', 'new_v_cf_conv_w': 'new_v', 'new_v_cf_ln_g': 'new_v', 'new_v_cf_ln_b': 'new_v', 'new_v_cf_w_o': 'new_v', 'new_v_w_out': 'new_v', 'new_v_norm2_g': 'new_v', 'new_v_ffn_w_up': 'new_v', 'new_v_ffn_conv_w': 'new_v', 'new_v_ffn_w_down': 'new_v', 'new_v_final_norm_g': 'new_v'}


def _forward(args):
    return _fwd_reference(*[args[k] for k in FWD_PARAMS])


def _output_shape():
    out = _jax.eval_shape(lambda: _forward(_fwd_setup_inputs(0)))
    return out.shape, out.dtype

N_MICROBATCH = 1
ADAM_LR = 0.001
ADAM_B1 = 0.9
ADAM_B2 = 0.999
ADAM_EPS = 1e-08
ADAM_WD = 0.01
ADAM_STEP = 10
PER_EXAMPLE_BATCH_AXIS = {'x': 0, 'c': 0, 'loss_target': 0}
SHARED_INPUTS = []
_WEIGHT_DTYPES = {'w_ada': _jnp.float32, 'b_ada': _jnp.float32, 'norm1_g': _jnp.float32, 'w_in': _jnp.float32, 'dn_conv_w': _jnp.float32, 'dn_a_log': _jnp.float32, 'dn_dt_bias': _jnp.float32, 'dn_norm_g': _jnp.float32, 'dn_w_o': _jnp.float32, 'cf_conv_w': _jnp.float32, 'cf_ln_g': _jnp.float32, 'cf_ln_b': _jnp.float32, 'cf_w_o': _jnp.float32, 'w_out': _jnp.float32, 'norm2_g': _jnp.float32, 'ffn_w_up': _jnp.float32, 'ffn_conv_w': _jnp.float32, 'ffn_w_down': _jnp.float32, 'final_norm_g': _jnp.float32}
MOMENT_SCALE = {'w_ada': 1.580266e-02, 'b_ada': 2.606809e-02, 'norm1_g': 1.345361e-02, 'w_in': 6.170669e-03, 'dn_conv_w': 7.282626e-03, 'dn_a_log': 3.998296e-02, 'dn_dt_bias': 3.820562e-02, 'dn_norm_g': 2.864298e-02, 'dn_w_o': 6.792541e-03, 'cf_conv_w': 9.197162e-03, 'cf_ln_g': 1.087911e-02, 'cf_ln_b': 9.675557e-03, 'cf_w_o': 6.349247e-03, 'w_out': 9.253225e-03, 'norm2_g': 1.893984e-02, 'ffn_w_up': 8.174718e-03, 'ffn_conv_w': 8.427182e-03, 'ffn_w_down': 1.331898e-02, 'final_norm_g': 8.009666e+00}


def _to_microbatches(a, axis):
    t = _jnp.moveaxis(a, axis, 0)
    t = t.reshape((N_MICROBATCH, t.shape[0] // N_MICROBATCH) + t.shape[1:])
    return _jnp.moveaxis(t, 1, axis + 1)


def setup_inputs(seed: int = 0) -> dict:
    inp = _fwd_setup_inputs(seed)
    key = _jax.random.fold_in(_jax.random.key(seed), 7919)
    shape, _ = _output_shape()
    out = dict(inp)
    out["loss_target"] = _jax.random.normal(_jax.random.fold_in(key, 0), shape, _jnp.float32)
    for i, name in enumerate(TWIN_WEIGHTS):
        w = inp[name].astype(_jnp.float32)
        if MOMENT_SCALE is None:
            s = _jnp.sqrt(_jnp.mean(_jnp.square(w)) + 1e-30)
        else:
            s = MOMENT_SCALE[name]
        km, kv = _jax.random.split(_jax.random.fold_in(key, i + 1))
        out[name] = w
        out["m_" + name] = s * _jax.random.normal(km, w.shape, _jnp.float32)
        out["v_" + name] = (s * s) * _jax.random.uniform(kv, w.shape, _jnp.float32, 0.5, 1.5)
    if N_MICROBATCH > 1:
        for name, axis in PER_EXAMPLE_BATCH_AXIS.items():
            out[name] = _to_microbatches(out[name], axis)
    return {'x': out['x'], 'c': out['c'], 'w_ada': out['w_ada'], 'b_ada': out['b_ada'], 'norm1_g': out['norm1_g'], 'w_in': out['w_in'], 'dn_conv_w': out['dn_conv_w'], 'dn_a_log': out['dn_a_log'], 'dn_dt_bias': out['dn_dt_bias'], 'dn_norm_g': out['dn_norm_g'], 'dn_w_o': out['dn_w_o'], 'cf_conv_w': out['cf_conv_w'], 'cf_ln_g': out['cf_ln_g'], 'cf_ln_b': out['cf_ln_b'], 'cf_w_o': out['cf_w_o'], 'w_out': out['w_out'], 'norm2_g': out['norm2_g'], 'ffn_w_up': out['ffn_w_up'], 'ffn_conv_w': out['ffn_conv_w'], 'ffn_w_down': out['ffn_w_down'], 'final_norm_g': out['final_norm_g'], 'loss_target': out['loss_target'], 'm_w_ada': out['m_w_ada'], 'm_b_ada': out['m_b_ada'], 'm_norm1_g': out['m_norm1_g'], 'm_w_in': out['m_w_in'], 'm_dn_conv_w': out['m_dn_conv_w'], 'm_dn_a_log': out['m_dn_a_log'], 'm_dn_dt_bias': out['m_dn_dt_bias'], 'm_dn_norm_g': out['m_dn_norm_g'], 'm_dn_w_o': out['m_dn_w_o'], 'm_cf_conv_w': out['m_cf_conv_w'], 'm_cf_ln_g': out['m_cf_ln_g'], 'm_cf_ln_b': out['m_cf_ln_b'], 'm_cf_w_o': out['m_cf_w_o'], 'm_w_out': out['m_w_out'], 'm_norm2_g': out['m_norm2_g'], 'm_ffn_w_up': out['m_ffn_w_up'], 'm_ffn_conv_w': out['m_ffn_conv_w'], 'm_ffn_w_down': out['m_ffn_w_down'], 'm_final_norm_g': out['m_final_norm_g'], 'v_w_ada': out['v_w_ada'], 'v_b_ada': out['v_b_ada'], 'v_norm1_g': out['v_norm1_g'], 'v_w_in': out['v_w_in'], 'v_dn_conv_w': out['v_dn_conv_w'], 'v_dn_a_log': out['v_dn_a_log'], 'v_dn_dt_bias': out['v_dn_dt_bias'], 'v_dn_norm_g': out['v_dn_norm_g'], 'v_dn_w_o': out['v_dn_w_o'], 'v_cf_conv_w': out['v_cf_conv_w'], 'v_cf_ln_g': out['v_cf_ln_g'], 'v_cf_ln_b': out['v_cf_ln_b'], 'v_cf_w_o': out['v_cf_w_o'], 'v_w_out': out['v_w_out'], 'v_norm2_g': out['v_norm2_g'], 'v_ffn_w_up': out['v_ffn_w_up'], 'v_ffn_conv_w': out['v_ffn_conv_w'], 'v_ffn_w_down': out['v_ffn_w_down'], 'v_final_norm_g': out['v_final_norm_g']}


def _loss(weights, diff, rest, loss_target):
    with _jax.named_scope("forward"):
        args = {**rest, TWIN_DIFF_INPUT: diff, **{k: w.astype(_WEIGHT_DTYPES[k]) for k, w in weights.items()}}
        y = _forward(args)
    with _jax.named_scope("loss_head"):
        err = _jnp.square(y.astype(_jnp.float32) - loss_target)
        return 0.5 * _jnp.sum(_jnp.mean(err, axis=-1)) if err.ndim else 0.5 * err


def _adamw(w, g, m, v):
    m = ADAM_B1 * m + (1.0 - ADAM_B1) * g
    v = ADAM_B2 * v + (1.0 - ADAM_B2) * _jnp.square(g)
    m_hat = m / (1.0 - ADAM_B1 ** ADAM_STEP)
    v_hat = v / (1.0 - ADAM_B2 ** ADAM_STEP)
    delta = -ADAM_LR * (m_hat / (_jnp.sqrt(v_hat) + ADAM_EPS) + ADAM_WD * w)
    return delta, m, v


def reference(x, c, w_ada, b_ada, norm1_g, w_in, dn_conv_w, dn_a_log, dn_dt_bias, dn_norm_g, dn_w_o, cf_conv_w, cf_ln_g, cf_ln_b, cf_w_o, w_out, norm2_g, ffn_w_up, ffn_conv_w, ffn_w_down, final_norm_g, loss_target, m_w_ada, m_b_ada, m_norm1_g, m_w_in, m_dn_conv_w, m_dn_a_log, m_dn_dt_bias, m_dn_norm_g, m_dn_w_o, m_cf_conv_w, m_cf_ln_g, m_cf_ln_b, m_cf_w_o, m_w_out, m_norm2_g, m_ffn_w_up, m_ffn_conv_w, m_ffn_w_down, m_final_norm_g, v_w_ada, v_b_ada, v_norm1_g, v_w_in, v_dn_conv_w, v_dn_a_log, v_dn_dt_bias, v_dn_norm_g, v_dn_w_o, v_cf_conv_w, v_cf_ln_g, v_cf_ln_b, v_cf_w_o, v_w_out, v_norm2_g, v_ffn_w_up, v_ffn_conv_w, v_ffn_w_down, v_final_norm_g):
    given = dict(x=x, c=c, w_ada=w_ada, b_ada=b_ada, norm1_g=norm1_g, w_in=w_in, dn_conv_w=dn_conv_w, dn_a_log=dn_a_log, dn_dt_bias=dn_dt_bias, dn_norm_g=dn_norm_g, dn_w_o=dn_w_o, cf_conv_w=cf_conv_w, cf_ln_g=cf_ln_g, cf_ln_b=cf_ln_b, cf_w_o=cf_w_o, w_out=w_out, norm2_g=norm2_g, ffn_w_up=ffn_w_up, ffn_conv_w=ffn_conv_w, ffn_w_down=ffn_w_down, final_norm_g=final_norm_g, loss_target=loss_target, m_w_ada=m_w_ada, m_b_ada=m_b_ada, m_norm1_g=m_norm1_g, m_w_in=m_w_in, m_dn_conv_w=m_dn_conv_w, m_dn_a_log=m_dn_a_log, m_dn_dt_bias=m_dn_dt_bias, m_dn_norm_g=m_dn_norm_g, m_dn_w_o=m_dn_w_o, m_cf_conv_w=m_cf_conv_w, m_cf_ln_g=m_cf_ln_g, m_cf_ln_b=m_cf_ln_b, m_cf_w_o=m_cf_w_o, m_w_out=m_w_out, m_norm2_g=m_norm2_g, m_ffn_w_up=m_ffn_w_up, m_ffn_conv_w=m_ffn_conv_w, m_ffn_w_down=m_ffn_w_down, m_final_norm_g=m_final_norm_g, v_w_ada=v_w_ada, v_b_ada=v_b_ada, v_norm1_g=v_norm1_g, v_w_in=v_w_in, v_dn_conv_w=v_dn_conv_w, v_dn_a_log=v_dn_a_log, v_dn_dt_bias=v_dn_dt_bias, v_dn_norm_g=v_dn_norm_g, v_dn_w_o=v_dn_w_o, v_cf_conv_w=v_cf_conv_w, v_cf_ln_g=v_cf_ln_g, v_cf_ln_b=v_cf_ln_b, v_cf_w_o=v_cf_w_o, v_w_out=v_w_out, v_norm2_g=v_norm2_g, v_ffn_w_up=v_ffn_w_up, v_ffn_conv_w=v_ffn_conv_w, v_ffn_w_down=v_ffn_w_down, v_final_norm_g=v_final_norm_g)
    weights = {n: given[n] for n in TWIN_WEIGHTS}
    shared = {n: given[n] for n in SHARED_INPUTS}
    per_example = {n: given[n] for n in ['x', 'c']}
    grad_fn = _jax.value_and_grad(_loss, argnums=(0, 1))

    def one_microbatch(ex, loss_target):
        ex = dict(ex)
        diff = ex.pop(TWIN_DIFF_INPUT)
        return grad_fn(weights, diff, {**shared, **ex}, loss_target)

    if N_MICROBATCH == 1:
        loss, (grad_w, grad_x) = one_microbatch(per_example, given["loss_target"])
    else:
        def body(carry, xs):
            loss_sum, grad_sum = carry
            l_k, (gw_k, gx_k) = one_microbatch(xs[0], xs[1])
            with _jax.named_scope("update"):
                return (loss_sum + l_k, _jax.tree.map(_jnp.add, grad_sum, gw_k)), gx_k

        init = (_jnp.zeros((), _jnp.float32), _jax.tree.map(_jnp.zeros_like, weights))
        (loss, grad_w), grad_x = _jax.lax.scan(body, init, (per_example, given["loss_target"]))
    with _jax.named_scope("update"):
        delta_w, new_m, new_v = {}, {}, {}
        for n in TWIN_WEIGHTS:
            delta_w[n], new_m[n], new_v[n] = _adamw(weights[n], grad_w[n], given["m_" + n], given["v_" + n])
    return (loss, grad_x, *[grad_w[n] for n in TWIN_WEIGHTS], *[delta_w[n] for n in TWIN_WEIGHTS],
            *[new_m[n] for n in TWIN_WEIGHTS], *[new_v[n] for n in TWIN_WEIGHTS])
```

```python
import functools

import jax
import jax.numpy as jnp
from jax import lax
from jax.experimental import pallas as pl
from jax.experimental.pallas import tpu as pltpu

F32 = jnp.float32
BF16 = jnp.bfloat16
EPS = 1e-6
LANES = 128
VMEM_LIMIT = 48 * 1024 * 1024
DN_CHUNK = 128
ADAM_LR, ADAM_B1, ADAM_B2, ADAM_EPS, ADAM_WD, ADAM_STEP = 0.001, 0.9, 0.999, 1e-08, 0.01, 10
MESH = pl.DeviceIdType.MESH

NN = (((1,), (0,)), ((), ()))
NT = (((1,), (1,)), ((), ()))
TN = (((0,), (0,)), ((), ()))
_DIMS = {"nn": NN, "nt": NT, "tn": TN}


def _mm(a, b, dims):
    return lax.dot_general(a.astype(BF16), b.astype(BF16), dims, preferred_element_type=F32)


def _mmx(a, b, dims):
    return lax.dot_general(a, b, dims, precision=lax.Precision.HIGHEST, preferred_element_type=F32)


def _div_tile(n, target, mult):
    best = None
    t = mult
    while t <= min(n, target):
        if n % t == 0:
            best = t
        t += mult
    return best if best is not None else n


def _params(sem=None):
    kw = dict(vmem_limit_bytes=VMEM_LIMIT)
    if sem is not None:
        kw["dimension_semantics"] = sem
    return pltpu.CompilerParams(**kw)


def _sigmoid(x):
    return jax.nn.sigmoid(x)


def _silu(x):
    return x * jax.nn.sigmoid(x)


def _softplus(x):
    return jnp.maximum(x, 0.0) + jnp.log(1.0 + jnp.exp(-jnp.abs(x)))


def _matmul(a, b, mode, out_dtype, name):
    if mode == "nn":
        (M, K), (_, N) = a.shape, b.shape
    elif mode == "nt":
        (M, K), (N, _) = a.shape, b.shape
    else:
        (K, M), (_, N) = a.shape, b.shape
    tm = _div_tile(M, 1024, LANES)
    tn = _div_tile(N, 1536, LANES)
    tk = _div_tile(K, 512, LANES)
    nk = K // tk
    dims = _DIMS[mode]
    a_spec = {"nn": pl.BlockSpec((tm, tk), lambda i, j, k: (i, k)),
              "nt": pl.BlockSpec((tm, tk), lambda i, j, k: (i, k)),
              "tn": pl.BlockSpec((tk, tm), lambda i, j, k: (k, i))}[mode]
    b_spec = {"nn": pl.BlockSpec((tk, tn), lambda i, j, k: (k, j)),
              "nt": pl.BlockSpec((tn, tk), lambda i, j, k: (j, k)),
              "tn": pl.BlockSpec((tk, tn), lambda i, j, k: (k, j))}[mode]

    def body(a_ref, b_ref, o_ref, acc_ref):
        k = pl.program_id(2)

        @pl.when(k == 0)
        def _():
            acc_ref[...] = jnp.zeros_like(acc_ref)

        acc_ref[...] += lax.dot_general(a_ref[...], b_ref[...], dims, preferred_element_type=F32)

        @pl.when(k == nk - 1)
        def _():
            o_ref[...] = acc_ref[...].astype(o_ref.dtype)

    return pl.pallas_call(
        body, name=name, grid=(M // tm, N // tn, nk),
        in_specs=[a_spec, b_spec], out_specs=pl.BlockSpec((tm, tn), lambda i, j, k: (i, j)),
        out_shape=jax.ShapeDtypeStruct((M, N), out_dtype),
        scratch_shapes=[pltpu.VMEM((tm, tn), F32)],
        compiler_params=_params(("parallel", "parallel", "arbitrary")),
    )(a, b)


def _ew(fn, name, grid, ins, in_specs, outs, acc=()):
    n_in = len(ins)
    n_ax = len(grid)

    def body(*refs):
        in_refs, out_refs = refs[:n_in], refs[n_in:]
        ids = [pl.program_id(a) for a in range(n_ax)]
        res = fn(*[r[...] for r in in_refs])
        first = ids[0] == 0
        for t in ids[1:]:
            first = jnp.logical_and(first, t == 0)
        for idx, (r, val) in enumerate(zip(out_refs, res)):
            if idx in acc:
                @pl.when(first)
                def _(r=r, val=val):
                    r[...] = val.astype(r.dtype)

                @pl.when(jnp.logical_not(first))
                def _(r=r, val=val):
                    r[...] += val.astype(r.dtype)
            else:
                r[...] = val.astype(r.dtype)

    return pl.pallas_call(
        body, name=name, grid=grid, in_specs=list(in_specs),
        out_specs=[o[2] for o in outs],
        out_shape=[jax.ShapeDtypeStruct(o[0], o[1]) for o in outs],
        compiler_params=_params(("arbitrary",) * n_ax),
    )(*ins)


def _row(bs, w, col=0):
    return pl.BlockSpec((bs, w), lambda i, col=col: (i, col))


def _full(shape):
    nd = len(shape)
    return pl.BlockSpec(tuple(shape), lambda *_: (0,) * nd)


def _rms(x, g):
    return x * lax.rsqrt(jnp.mean(x * x, axis=-1, keepdims=True) + EPS) * g


def _f_normmod(x, g, sc, sh):
    return _rms(x, g) * (1.0 + sc) + sh


def _f_res_normmod(x, mix, gt, g, sc, sh):
    x1 = x + gt * mix
    return x1, _f_normmod(x1, g, sc, sh)


def _f_loss(x1, f, gt, gf, tgt):
    y = _rms(x1 + gt * f, gf)
    return 0.5 * jnp.sum(jnp.mean(jnp.square(y - tgt), axis=-1))


def _f_dn_gate(nh, ba, alp, dtb):
    lane = lax.broadcasted_iota(jnp.int32, ba.shape, 1)
    m = (lane < nh).astype(F32)
    beta = _sigmoid(ba)
    g = -jnp.exp(alp) * _softplus(ba + dtb)
    return m * beta + (1.0 - m) * g


def _f_dn_post(o, z, g):
    return o * lax.rsqrt(jnp.mean(o * o, axis=-1, keepdims=True) + EPS) * g * _silu(z)


def _f_cf_ln(u, g, b):
    mu = jnp.mean(u, axis=-1, keepdims=True)
    xc = u - mu
    y = xc * lax.rsqrt(jnp.mean(xc * xc, axis=-1, keepdims=True) + EPS)
    return _silu(y * g + b)


def _f_merge(a, b, ga, gb):
    return _sigmoid(ga) * a + _sigmoid(gb) * b


def _shift_down(u, d, rows):
    if d == 0:
        return u
    return jnp.where(rows >= d, pltpu.roll(u, d, 0), 0.0)


def _shift_up(u, d, rows):
    if d == 0:
        return u
    s = u.shape[0]
    return jnp.where(rows < s - d, pltpu.roll(u, s - d, 0), 0.0)


def _conv(u, w_ref, kw, rows):
    acc = None
    for k in range(kw):
        t = w_ref[k:k + 1, :] * _shift_down(u, kw - 1 - k, rows)
        acc = t if acc is None else acc + t
    return acc


def _conv_t(dc, w_ref, kw, rows):
    acc = None
    for k in range(kw):
        t = w_ref[k:k + 1, :] * _shift_up(dc, kw - 1 - k, rows)
        acc = t if acc is None else acc + t
    return acc


def _conv_fwd_call(name, kw, ncol, ins, in_specs, extras, extra_specs, w, w_spec, pre, post, out_shape, out_spec):
    n_in, n_ex = len(ins), len(extras)

    def body(*refs):
        in_refs, ex_refs = refs[:n_in], refs[n_in:n_in + n_ex]
        w_ref, out_ref = refs[n_in + n_ex], refs[n_in + n_ex + 1]
        j = pl.program_id(0)
        u = pre(*[r[...] for r in in_refs])
        rows = lax.broadcasted_iota(jnp.int32, u.shape, 0)
        cv = _conv(u, w_ref, kw, rows)
        out_ref[...] = post(j, cv, *[r[...] for r in ex_refs]).astype(out_ref.dtype)

    return pl.pallas_call(
        body, name=name, grid=(ncol,), in_specs=list(in_specs) + list(extra_specs) + [w_spec],
        out_specs=out_spec, out_shape=out_shape, compiler_params=_params(("arbitrary",)),
    )(*ins, *extras, w)


def _conv_bwd_call(name, kw, ncol, ins, in_specs, extras, extra_specs, w, w_spec, pre, post, dout, dout_spec,
                   din_shapes, din_specs, dex_shapes, dex_specs, dw_shape, dw_spec):
    n_in, n_ex = len(ins), len(extras)

    def body(*refs):
        in_refs, ex_refs = refs[:n_in], refs[n_in:n_in + n_ex]
        w_ref, dout_ref = refs[n_in + n_ex], refs[n_in + n_ex + 1]
        outs = refs[n_in + n_ex + 2:]
        din_refs, dex_refs, dw_ref = outs[:n_in], outs[n_in:n_in + n_ex], outs[n_in + n_ex]
        j = pl.program_id(0)
        u, pre_vjp = jax.vjp(pre, *[r[...] for r in in_refs])
        rows = lax.broadcasted_iota(jnp.int32, u.shape, 0)
        cv = _conv(u, w_ref, kw, rows)
        _, post_vjp = jax.vjp(lambda cc, *ex: post(j, cc, *ex), cv, *[r[...] for r in ex_refs])
        g = post_vjp(dout_ref[...].astype(F32))
        dc = g[0]
        for r, val in zip(dex_refs, g[1:]):
            r[...] = val.astype(r.dtype)
        for k in range(kw):
            dw_ref[k:k + 1, :] = jnp.sum(dc * _shift_down(u, kw - 1 - k, rows), axis=0, keepdims=True)
        du = _conv_t(dc, w_ref, kw, rows)
        for r, val in zip(din_refs, pre_vjp(du)):
            r[...] = val.astype(r.dtype)

    return pl.pallas_call(
        body, name=name, grid=(ncol,),
        in_specs=list(in_specs) + list(extra_specs) + [w_spec, dout_spec],
        out_specs=list(din_specs) + list(dex_specs) + [dw_spec],
        out_shape=list(din_shapes) + list(dex_shapes) + [dw_shape],
        compiler_params=_params(("arbitrary",)),
    )(*ins, *extras, w, dout)


def _tri_inverse(a):
    c = a.shape[0]
    ii = lax.broadcasted_iota(jnp.int32, (c, c), 0)
    jj = lax.broadcasted_iota(jnp.int32, (c, c), 1)
    y = -a
    t = (ii == jj).astype(F32) + y
    p = 2
    while p < c:
        y = _mmx(y, y, NN)
        t = t + _mmx(t, y, NN)
        p *= 2
    return t


def _dn_common(q, k, v, gb, bb):
    c = q.shape[0]
    ii = lax.broadcasted_iota(jnp.int32, (c, c), 0)
    jj = lax.broadcasted_iota(jnp.int32, (c, c), 1)
    causal = jj <= ii
    strict = jj < ii
    low = causal.astype(F32)
    ones = jnp.ones((c, LANES), F32)
    gc = _mmx(low, gb, NN)
    diff = (_mmx(gc, ones, NT) - _mmx(ones, gc, NT)) * (1.0 / LANES)
    decay = jnp.where(causal, jnp.exp(jnp.where(causal, diff, 0.0)), 0.0)
    gl = jnp.sum(gb, axis=0, keepdims=True)
    eg = jnp.exp(gc)
    egm = jnp.exp(gl - gc)
    egl = jnp.exp(gl)
    kb = k * bb
    vb = v * bb
    kbg = kb * eg
    kk = _mm(kb, k, NT)
    t = _tri_inverse(jnp.where(strict, kk * decay, 0.0))
    qk = _mm(q, k, NT)
    attn = qk * decay
    return dict(causal=causal, strict=strict, low=low, ones=ones, decay=decay, eg=eg, egm=egm, egl=egl,
                kb=kb, vb=vb, kbg=kbg, kk=kk, t=t, qk=qk, attn=attn, qd=q * eg, kd=k * egm)


def _dn_fwd_chunk(q, k, v, gb, bb, s):
    m = _dn_common(q, k, v, gb, bb)
    u = _mmx(m["t"], m["vb"] - _mm(m["kbg"], s, NN), NN)
    o = _mm(m["qd"], s, NN) + _mm(m["attn"], u, NN)
    s2 = s * m["egl"] + _mm(m["kd"], u, TN)
    return o, s2


def _dn_bwd_chunk(q, k, v, gb, bb, s, do, dsp):
    m = _dn_common(q, k, v, gb, bb)
    c = q.shape[0]
    t, decay, eg, egm, egl = m["t"], m["decay"], m["eg"], m["egm"], m["egl"]
    u = _mmx(t, m["vb"] - _mm(m["kbg"], s, NN), NN)
    du = _mm(m["attn"], do, TN) + _mm(m["kd"], dsp, NN)
    dattn = jnp.where(m["causal"], _mm(do, u, NT), 0.0)
    dqd = _mm(do, s, NT)
    dkd = _mm(u, dsp, NT)
    dr = _mmx(t, du, TN)
    da = jnp.where(m["strict"], -_mm(dr, u, NT), 0.0)
    dkbg = -_mm(dr, s, NT)
    ds = dsp * egl + _mm(m["qd"], do, TN) - _mm(m["kbg"], dr, TN)
    degl = jnp.sum(jnp.sum(dsp * s, axis=1, keepdims=True), axis=0, keepdims=True)
    dkk = da * decay
    dqk = dattn * decay
    ddiff = (da * m["kk"] + dattn * m["qk"]) * decay
    dgc = _mmx(ddiff, m["ones"], NN) - _mmx(ddiff, m["ones"], TN)
    dkb = _mm(dkk, k, NN) + dkbg * eg
    dk = _mm(dkk, m["kb"], TN) + _mm(dqk, q, TN) + dkd * egm + dkb * bb
    dq = _mm(dqk, k, NN) + dqd * eg
    dgc = dgc + jnp.sum(dqd * q + dkbg * m["kb"], axis=-1, keepdims=True) * eg
    tt = jnp.sum(dkd * k, axis=-1, keepdims=True) * egm
    dgc = dgc - tt
    dgl = jnp.sum(tt, axis=0, keepdims=True) + degl * egl
    dbb = jnp.sum(dkb * k + dr * v, axis=-1, keepdims=True) + jnp.zeros((c, LANES), F32)
    dv = dr * bb
    dgb = _mmx(m["low"], dgc, TN) + dgl
    return dq, dk, dv, dgb, dbb, ds


def _dn_fwd_call(qkvn, gb, bb):
    _, nh, s, dh = qkvn.shape
    c = min(DN_CHUNK, s)
    n = s // c
    hb = nh

    def body(q_ref, k_ref, v_ref, g_ref, b_ref, o_ref, st_ref, s_ref):
        @pl.when(pl.program_id(1) == 0)
        def _():
            s_ref[...] = jnp.zeros_like(s_ref)

        for h in range(hb):
            st = s_ref[h]
            st_ref[h] = st
            o, s2 = _dn_fwd_chunk(q_ref[h], k_ref[h], v_ref[h], g_ref[h], b_ref[h], st)
            o_ref[h] = o
            s_ref[h] = s2

    def qspec(t):
        return pl.BlockSpec((None, hb, c, dh), lambda i, j, t=t: (t, i, j, 0))

    hs = pl.BlockSpec((hb, c, dh), lambda i, j: (i, j, 0))
    return pl.pallas_call(
        body, name="dn_fwd", grid=(nh // hb, n),
        in_specs=[qspec(0), qspec(1), qspec(2), hs, hs],
        out_specs=[hs, pl.BlockSpec((hb, None, dh, dh), lambda i, j: (i, j, 0, 0))],
        out_shape=[jax.ShapeDtypeStruct((nh, s, dh), F32), jax.ShapeDtypeStruct((nh, n, dh, dh), F32)],
        scratch_shapes=[pltpu.VMEM((hb, dh, dh), F32)],
        compiler_params=_params(("arbitrary", "arbitrary")),
    )(qkvn, qkvn, qkvn, gb, bb)


def _dn_bwd_call(qkvn, gb, bb, states, do):
    _, nh, s, dh = qkvn.shape
    c = min(DN_CHUNK, s)
    n = s // c
    hb = nh

    def body(q_ref, k_ref, v_ref, g_ref, b_ref, st_ref, do_ref, dq_ref, dk_ref, dv_ref, dg_ref, db_ref, ds_ref):
        @pl.when(pl.program_id(1) == 0)
        def _():
            ds_ref[...] = jnp.zeros_like(ds_ref)

        for h in range(hb):
            dq, dk, dv, dg, db, ds = _dn_bwd_chunk(q_ref[h], k_ref[h], v_ref[h], g_ref[h], b_ref[h], st_ref[h],
                                                   do_ref[h], ds_ref[h])
            dq_ref[h] = dq
            dk_ref[h] = dk
            dv_ref[h] = dv
            dg_ref[h] = dg
            db_ref[h] = db
            ds_ref[h] = ds

    def qspec(t):
        return pl.BlockSpec((None, hb, c, dh), lambda i, j, t=t: (t, i, n - 1 - j, 0))

    hs = pl.BlockSpec((hb, c, dh), lambda i, j: (i, n - 1 - j, 0))
    sh = jax.ShapeDtypeStruct((nh, s, dh), F32)
    return pl.pallas_call(
        body, name="dn_bwd", grid=(nh // hb, n),
        in_specs=[qspec(0), qspec(1), qspec(2), hs, hs,
                  pl.BlockSpec((hb, None, dh, dh), lambda i, j: (i, n - 1 - j, 0, 0)), hs],
        out_specs=[hs] * 5, out_shape=[sh] * 5,
        scratch_shapes=[pltpu.VMEM((hb, dh, dh), F32)],
        compiler_params=_params(("arbitrary", "arbitrary")),
    )(qkvn, qkvn, qkvn, gb, bb, states, do)


def _adamw(w, g, m, v, name):
    r, c = w.shape
    br = _div_tile(r, max(8, (1 << 18) // max(c, 1)), 8)

    def fn(w, g, m, v):
        m = ADAM_B1 * m + (1.0 - ADAM_B1) * g
        v = ADAM_B2 * v + (1.0 - ADAM_B2) * jnp.square(g)
        m_hat = m / (1.0 - ADAM_B1 ** ADAM_STEP)
        v_hat = v / (1.0 - ADAM_B2 ** ADAM_STEP)
        delta = -ADAM_LR * (m_hat / (jnp.sqrt(v_hat) + ADAM_EPS) + ADAM_WD * w)
        return delta, m, v

    spec = pl.BlockSpec((br, c), lambda i: (i, 0))
    return _ew(fn, name, (r // br,), [w, g, m, v], [spec] * 4, [((r, c), F32, spec)] * 3)


def _coords():
    return lax.axis_index("x"), lax.axis_index("y"), lax.axis_index("c")


def _all_gather_small(v):
    r, w = v.shape

    def body(v_ref, out_ref, send_sems, recv_sems, local_sem):
        x, y, c = _coords()
        me = 4 * x + 2 * y + c
        mine = pltpu.make_async_copy(v_ref, out_ref.at[me], local_sem)
        mine.start()
        peers = []
        for k in range(1, 8):
            px = 1 - x if k & 4 else x
            py = 1 - y if k & 2 else y
            pc = 1 - c if k & 1 else c
            peers.append((px, py, pc))
        sends = []
        for k, peer in enumerate(peers):
            cp = pltpu.make_async_remote_copy(src_ref=v_ref, dst_ref=out_ref.at[me], send_sem=send_sems.at[k],
                                              recv_sem=recv_sems.at[k], device_id=peer, device_id_type=MESH)
            cp.start()
            sends.append(cp)
        for k, (px, py, pc) in enumerate(peers):
            pltpu.make_async_remote_copy(src_ref=v_ref, dst_ref=out_ref.at[4 * px + 2 * py + pc],
                                         send_sem=send_sems.at[k], recv_sem=recv_sems.at[k],
                                         device_id=(px, py, pc), device_id_type=MESH).wait_recv()
        for cp in sends:
            cp.wait_send()
        mine.wait()

    return pl.pallas_call(
        body, name="ag_small", out_shape=jax.ShapeDtypeStruct((8, r, w), v.dtype),
        in_specs=[pl.BlockSpec(memory_space=pltpu.VMEM)], out_specs=pl.BlockSpec(memory_space=pltpu.VMEM),
        scratch_shapes=[pltpu.SemaphoreType.DMA((7,)), pltpu.SemaphoreType.DMA((7,)), pltpu.SemaphoreType.DMA],
        compiler_params=pltpu.CompilerParams(vmem_limit_bytes=VMEM_LIMIT),
    )(v)


def _ag_chips(p):
    rt, w = p.shape
    h = rt // 2

    def body(p_ref, out_ref, send_sems, recv_sems, local_sem):
        x, y, c = _coords()
        mine = pl.ds(pl.multiple_of(c * h, 16), h)
        sib = pl.ds(pl.multiple_of((1 - c) * h, 16), h)
        local = pltpu.make_async_copy(p_ref, out_ref.at[2 * x + y], local_sem)
        local.start()
        chips = [(1 - x, y), (x, 1 - y), (1 - x, 1 - y)]

        def cp(k, chip, rows, to, src=None):
            dst = out_ref.at[2 * chip[0] + chip[1], rows]
            return pltpu.make_async_remote_copy(src_ref=dst if src is None else src, dst_ref=dst,
                                                send_sem=send_sems.at[k], recv_sem=recv_sems.at[k],
                                                device_id=to, device_id_type=MESH)

        first = [cp(k, (x, y), mine, (chip[0], chip[1], c), src=p_ref.at[mine]) for k, chip in enumerate(chips)]
        for f in first:
            f.start()
        passed = [cp(3 + k, chip, mine, (x, y, 1 - c)) for k, chip in enumerate(chips)]
        for k, chip in enumerate(chips):
            cp(k, chip, mine, (x, y, c)).wait_recv()
            passed[k].start()
        for k, chip in enumerate(chips):
            cp(3 + k, chip, sib, (x, y, c)).wait_recv()
        for f in first + passed:
            f.wait_send()
        local.wait()

    hbm = pl.BlockSpec(memory_space=pltpu.HBM)
    return pl.pallas_call(
        body, name="ag_weights", out_shape=jax.ShapeDtypeStruct((4, rt, w), p.dtype),
        in_specs=[hbm], out_specs=hbm,
        scratch_shapes=[pltpu.SemaphoreType.DMA((6,)), pltpu.SemaphoreType.DMA((6,)), pltpu.SemaphoreType.DMA],
    )(p)


def _pair_swap_half(p):
    n, rt, w = p.shape
    h = rt // 2

    def body(p_ref, out_ref, send_sem, recv_sem):
        x, y, c = _coords()
        other = pl.ds(pl.multiple_of((1 - c) * h, 8), h)
        cp = pltpu.make_async_remote_copy(src_ref=p_ref.at[:, other], dst_ref=out_ref, send_sem=send_sem,
                                          recv_sem=recv_sem, device_id=(x, y, 1 - c), device_id_type=MESH)
        cp.start()
        cp.wait()

    hbm = pl.BlockSpec(memory_space=pltpu.HBM)
    return pl.pallas_call(
        body, name="rs_pair", out_shape=jax.ShapeDtypeStruct((n, h, w), p.dtype), in_specs=[hbm], out_specs=hbm,
        scratch_shapes=[pltpu.SemaphoreType.DMA, pltpu.SemaphoreType.DMA],
    )(p)


def _chip_scatter(q):
    n, h, w = q.shape

    def body(q_ref, out_ref, send_sems, recv_sems, local_sem):
        x, y, c = _coords()
        j = 2 * x + y
        local = pltpu.make_async_copy(q_ref.at[j], out_ref.at[j], local_sem)
        local.start()
        chips = [(1 - x, y), (x, 1 - y), (1 - x, 1 - y)]
        sends = []
        for k, (px, py) in enumerate(chips):
            cp = pltpu.make_async_remote_copy(src_ref=q_ref.at[2 * px + py], dst_ref=out_ref.at[j],
                                              send_sem=send_sems.at[k], recv_sem=recv_sems.at[k],
                                              device_id=(px, py, c), device_id_type=MESH)
            cp.start()
            sends.append(cp)
        for k, (px, py) in enumerate(chips):
            pltpu.make_async_remote_copy(src_ref=q_ref.at[j], dst_ref=out_ref.at[2 * px + py],
                                         send_sem=send_sems.at[k], recv_sem=recv_sems.at[k],
                                         device_id=(px, py, c), device_id_type=MESH).wait_recv()
        for cp in sends:
            cp.wait_send()
        local.wait()

    hbm = pl.BlockSpec(memory_space=pltpu.HBM)
    return pl.pallas_call(
        body, name="rs_chips", out_shape=jax.ShapeDtypeStruct((n, h, w), q.dtype), in_specs=[hbm], out_specs=hbm,
        scratch_shapes=[pltpu.SemaphoreType.DMA((3,)), pltpu.SemaphoreType.DMA((3,)), pltpu.SemaphoreType.DMA],
    )(q)


def _pair_gather(hf):
    h, w = hf.shape

    def body(h_ref, out_ref, send_sem, recv_sem, local_sem):
        x, y, c = _coords()
        local = pltpu.make_async_copy(h_ref, out_ref.at[c], local_sem)
        local.start()
        cp = pltpu.make_async_remote_copy(src_ref=h_ref, dst_ref=out_ref.at[c], send_sem=send_sem,
                                          recv_sem=recv_sem, device_id=(x, y, 1 - c), device_id_type=MESH)
        cp.start()
        pltpu.make_async_remote_copy(src_ref=h_ref, dst_ref=out_ref.at[1 - c], send_sem=send_sem,
                                     recv_sem=recv_sem, device_id=(x, y, 1 - c), device_id_type=MESH).wait_recv()
        cp.wait_send()
        local.wait()

    hbm = pl.BlockSpec(memory_space=pltpu.HBM)
    return pl.pallas_call(
        body, name="rs_pair_gather", out_shape=jax.ShapeDtypeStruct((2, h, w), hf.dtype), in_specs=[hbm],
        out_specs=hbm,
        scratch_shapes=[pltpu.SemaphoreType.DMA, pltpu.SemaphoreType.DMA, pltpu.SemaphoreType.DMA],
    )(hf)


def _pair_add(p, rb, core):
    n, rt, w = p.shape
    h = rt // 2
    br = _div_tile(h, 2048, 16)
    nb = h // br

    def body(c_ref, p_ref, r_ref, o_ref):
        o_ref[...] = (p_ref[...] + r_ref[...]).astype(o_ref.dtype)

    return pl.pallas_call(
        body, name="rs_pair_add",
        grid_spec=pltpu.PrefetchScalarGridSpec(
            num_scalar_prefetch=1, grid=(n, nb),
            in_specs=[pl.BlockSpec((None, br, w), lambda s, i, c_ref: (s, c_ref[0] * nb + i, 0)),
                      pl.BlockSpec((None, br, w), lambda s, i, c_ref: (s, i, 0))],
            out_specs=pl.BlockSpec((None, br, w), lambda s, i, c_ref: (s, i, 0))),
        out_shape=jax.ShapeDtypeStruct((n, h, w), BF16),
        compiler_params=_params(("arbitrary", "arbitrary")),
    )(core, p, rb)


def _sum_slots(r, name):
    n, h, w = r.shape
    br = _div_tile(h, 2048, 16)

    def fn(blk):
        acc = blk[0].astype(F32)
        for s in range(1, n):
            acc = acc + blk[s].astype(F32)
        return (acc,)

    return _ew(fn, name, (h // br,), [r], [pl.BlockSpec((n, br, w), lambda i: (0, i, 0))],
               [((h, w), F32, pl.BlockSpec((br, w), lambda i: (i, 0)))])[0]


def _pack_small(arrs):
    flat = jnp.concatenate([a.reshape(-1).astype(F32) for a in arrs])
    n = flat.shape[0]
    rows = -(-n // LANES)
    rows = -(-rows // 8) * 8
    return jnp.pad(flat, (0, rows * LANES - n)).reshape(rows, LANES)


def _unpack_small(p, shapes):
    lead = p.shape[:-2]
    flat = p.reshape(lead + (-1,))
    out, off = [], 0
    for sh in shapes:
        n = 1
        for d in sh:
            n *= d
        out.append(flat[..., off:off + n].reshape(lead + tuple(sh)))
        off += n
    return out


def _pack_rows128(a):
    c = a.shape[-1]
    cp = -(-c // LANES) * LANES
    if cp != c:
        a = jnp.pad(a, [(0, 0)] * (a.ndim - 1) + [(0, cp - c)])
    return a.reshape(a.shape[:-2] + (a.shape[-2] * cp // LANES, LANES))


def _unpack_rows128(p, r, c):
    cp = -(-c // LANES) * LANES
    a = p.reshape(p.shape[:-2] + (r, cp))
    return a[..., :c]


def kernel(x, c, w_ada, b_ada, norm1_g, w_in, dn_conv_w, dn_a_log, dn_dt_bias, dn_norm_g, dn_w_o, cf_conv_w, cf_ln_g, cf_ln_b, cf_w_o, w_out, norm2_g, ffn_w_up, ffn_conv_w, ffn_w_down, final_norm_g, loss_target, m_w_ada, m_b_ada, m_norm1_g, m_w_in, m_dn_conv_w, m_dn_a_log, m_dn_dt_bias, m_dn_norm_g, m_dn_w_o, m_cf_conv_w, m_cf_ln_g, m_cf_ln_b, m_cf_w_o, m_w_out, m_norm2_g, m_ffn_w_up, m_ffn_conv_w, m_ffn_w_down, m_final_norm_g, v_w_ada, v_b_ada, v_norm1_g, v_w_in, v_dn_conv_w, v_dn_a_log, v_dn_dt_bias, v_dn_norm_g, v_dn_w_o, v_cf_conv_w, v_cf_ln_g, v_cf_ln_b, v_cf_w_o, v_w_out, v_norm2_g, v_ffn_w_up, v_ffn_conv_w, v_ffn_w_down, v_final_norm_g):
    xi, yi, ci = _coords()
    chip = 2 * xi + yi
    me = 4 * xi + 2 * yi + ci
    core = jnp.reshape(ci, (1,)).astype(jnp.int32)

    S, D = x.shape[1], x.shape[2]
    NH = dn_a_log.shape[1]
    DH = dn_norm_g.shape[1]
    DNW = NH * DH
    CFW = cf_ln_g.shape[1]
    FFN = ffn_w_down.shape[1] * 4
    KDN, KCF, KFF = dn_conv_w.shape[1], cf_conv_w.shape[1], ffn_conv_w.shape[1]
    NIN = w_in.shape[2] * 4
    assert NIN == 4 * DNW + 2 * NH + 2 * CFW + 2 * D and DH == LANES and 2 * NH <= LANES
    x2, tgt = x[0], loss_target[0]

    sm_shapes = [(D,), (KDN, 3 * DNW // 4), (KCF, CFW // 4), (KFF, FFN // 4)]
    g1 = _all_gather_small(_pack_small([c[0], dn_conv_w[0], cf_conv_w[0], ffn_conv_w[0]]))
    c_all, dcw_s, ccw_s, fcw_s = _unpack_small(g1, sm_shapes)

    def chips_cols(t):
        t = t[0::2]
        return jnp.transpose(t, (1, 0, 2)).reshape(t.shape[1], -1)

    dn_cw, cf_cw, ff_cw = chips_cols(dcw_s), chips_cols(ccw_s), chips_cols(fcw_s)

    col_sharded = [w_in[0], dn_w_o[0], cf_w_o[0], ffn_w_up[0]]
    row_sharded = [w_out[0], ffn_w_down[0]]
    big = col_sharded + row_sharded
    packed = [_pack_rows128(w.astype(BF16)) for w in big]
    sizes = [p.shape[0] for p in packed]
    rt = sum(sizes)
    rt_pad = -(-rt // 256) * 256
    pk = jnp.concatenate(packed + ([jnp.zeros((rt_pad - rt, LANES), BF16)] if rt_pad > rt else []))
    wall = _ag_chips(pk)
    offs = [sum(sizes[:i]) for i in range(len(sizes))]
    full = []
    for i, w in enumerate(big):
        r, cs = w.shape
        t = _unpack_rows128(wall[:, offs[i]:offs[i] + sizes[i]], r, cs)
        if i < len(col_sharded):
            full.append(jnp.transpose(t, (1, 0, 2)).reshape(r, 4 * cs))
        else:
            full.append(t.reshape(4 * r, cs))
    w_in_f, w_do_f, w_co_f, w_up_f, w_out_f, w_dn_f = full

    o_b = 4 * DNW
    o_glu = o_b + 2 * NH
    NA = NIN - 2 * NH
    w_aug = jnp.concatenate([w_in_f[:, :o_b], w_in_f[:, o_glu:], w_in_f[:, o_b:o_glu],
                             jnp.zeros((D, LANES - 2 * NH), BF16)], axis=1)
    a_z, a_val, a_gl, a_ga, a_gb, a_ba = 3 * DNW, 4 * DNW, 4 * DNW + CFW, 4 * DNW + 2 * CFW, 4 * DNW + 2 * CFW + D, NA

    CA = w_ada.shape[2]
    b_sh = lax.dynamic_slice(b_ada, (0, chip * CA), (1, CA))
    tn_a = _div_tile(CA, 512, LANES)

    def mod_fn(cc, w, b):
        return (_mm(_silu(cc), w, NN) + b,)

    mod_sh = _ew(mod_fn, "ada_mod", (CA // tn_a,), [c_all, w_ada[0], b_sh],
                 [_full((8, D)), pl.BlockSpec((D, tn_a), lambda j: (0, j)), pl.BlockSpec((1, tn_a), lambda j: (0, j))],
                 [((8, CA), F32, pl.BlockSpec((8, tn_a), lambda j: (0, j)))])[0]
    g2 = _all_gather_small(_pack_small([mod_sh]))
    mod_all = _unpack_small(g2, [(8, CA)])[0][0::2]
    mod_all = jnp.transpose(mod_all, (1, 0, 2)).reshape(8, 4 * CA)
    mod_me = lax.dynamic_slice(mod_all, (me, 0), (1, 6 * D))
    sh1, sc1, gt1, sh2, sc2, gt2 = [mod_me[:, i * D:(i + 1) * D] for i in range(6)]

    bs = _div_tile(S, 128, 8)
    nb = S // bs
    vecD = _full((1, D))
    rowD = _row(bs, D)

    hn1 = _ew(lambda a, g, sc, sh: (_f_normmod(a, g, sc, sh),), "norm1_fwd", (nb,),
              [x2, norm1_g, sc1, sh1], [rowD, vecD, vecD, vecD], [((S, D), BF16, rowD)])[0]
    proj = _matmul(hn1, w_aug, "nn", F32, "mm_in")

    def dn_post(j, cv):
        s = _silu(cv)
        nrm = s * lax.rsqrt(jnp.sum(s * s, axis=-1, keepdims=True) + EPS)
        fq = (j < NH).astype(F32)
        fk = (j < 2 * NH).astype(F32)
        scale = fq * (DH ** -0.5) + (1.0 - fq)
        return fk * (nrm * scale) + (1.0 - fk) * s

    def ident(a):
        return a

    def colS(w, off=0):
        return pl.BlockSpec((S, w), lambda j, off=off: (0, j + off))

    def wS(kw, w):
        return pl.BlockSpec((kw, w), lambda j: (0, j))

    qkv_spec = pl.BlockSpec((None, None, S, DH), lambda j: (j // NH, j % NH, 0, 0))
    dn_args = dict(kw=KDN, ncol=3 * NH, ins=[proj], in_specs=[colS(DH)], extras=[], extra_specs=[],
                   w=dn_cw, w_spec=wS(KDN, DH), pre=ident, post=dn_post)
    qkvn = _conv_fwd_call("dn_conv_fwd", out_shape=jax.ShapeDtypeStruct((3, NH, S, DH), F32), out_spec=qkv_spec,
                          **dn_args)

    alp = jnp.pad(dn_a_log, ((0, 0), (NH, LANES - 2 * NH)))
    dtb = jnp.pad(dn_dt_bias, ((0, 0), (NH, LANES - 2 * NH)))
    vecL = _full((1, LANES))
    ba_spec = _row(bs, LANES, a_ba // LANES)
    rowL = _row(bs, LANES)
    gate_fn = functools.partial(_f_dn_gate, NH)
    bg = _ew(lambda a, p, q: (gate_fn(a, p, q),), "dn_gate_fwd", (nb,), [proj, alp, dtb], [ba_spec, vecL, vecL],
             [((S, LANES), F32, rowL)])[0]

    def lanes_bcast(t):
        return jnp.broadcast_to(jnp.transpose(t)[:, :, None], (NH, S, DH))

    bb_b, gb_b = lanes_bcast(bg[:, :NH]), lanes_bcast(bg[:, NH:2 * NH])
    o_dn, states = _dn_fwd_call(qkvn, gb_b, bb_b)

    bsh = _div_tile(S, 512, 8)
    nbh = S // bsh
    o_spec = pl.BlockSpec((None, bsh, DH), lambda i, h: (h, i, 0))
    z_spec = pl.BlockSpec((bsh, DH), lambda i, h: (i, a_z // DH + h))
    oh_spec = pl.BlockSpec((bsh, DH), lambda i, h: (i, h))
    ng_spec = pl.BlockSpec((1, DH), lambda i, h: (0, 0))
    on = _ew(lambda o, z, g: (_f_dn_post(o, z, g),), "dn_post_fwd", (nbh, NH), [o_dn, proj, dn_norm_g],
             [o_spec, z_spec, ng_spec], [((S, DNW), BF16, oh_spec)])[0]
    br_a = _matmul(on, w_do_f, "nn", F32, "mm_dn_o")

    cb_cf = _div_tile(CFW, 256, LANES)

    def glu_pre(val, gl):
        return val * _sigmoid(gl)

    cf_args = dict(kw=KCF, ncol=CFW // cb_cf, ins=[proj, proj],
                   in_specs=[colS(cb_cf, a_val // cb_cf), colS(cb_cf, a_gl // cb_cf)], extras=[], extra_specs=[],
                   w=cf_cw, w_spec=wS(KCF, cb_cf), pre=glu_pre, post=lambda j, cv: cv)
    uc = _conv_fwd_call("cf_conv_fwd", out_shape=jax.ShapeDtypeStruct((S, CFW), F32), out_spec=colS(cb_cf), **cf_args)
    rowC = _row(bs, CFW)
    vecC = _full((1, CFW))
    ub = _ew(lambda u, g, b: (_f_cf_ln(u, g, b),), "cf_ln_fwd", (nb,), [uc, cf_ln_g, cf_ln_b], [rowC, vecC, vecC],
             [((S, CFW), BF16, rowC)])[0]
    br_b = _matmul(ub, w_co_f, "nn", F32, "mm_cf_o")

    ga_spec, gb_spec = _row(bs, D, a_ga // D), _row(bs, D, a_gb // D)
    merged = _ew(lambda a, b, ga, gb: (_f_merge(a, b, ga, gb),), "merge_fwd", (nb,), [br_a, br_b, proj, proj],
                 [rowD, rowD, ga_spec, gb_spec], [((S, D), BF16, rowD)])[0]
    mix = _matmul(merged, w_out_f, "nn", F32, "mm_out")

    x1, hn2 = _ew(_f_res_normmod, "norm2_fwd", (nb,), [x2, mix, gt1, norm2_g, sc2, sh2],
                  [rowD, rowD, vecD, vecD, vecD, vecD], [((S, D), F32, rowD), ((S, D), BF16, rowD)])
    up_all = _matmul(hn2, w_up_f, "nn", F32, "mm_up")

    cb_ff = _div_tile(FFN, 256, LANES)
    ff_args = dict(kw=KFF, ncol=FFN // cb_ff, ins=[up_all], in_specs=[colS(cb_ff)], extras=[up_all],
                   extra_specs=[colS(cb_ff, FFN // cb_ff)], w=ff_cw, w_spec=wS(KFF, cb_ff),
                   pre=ident, post=lambda j, cv, up: _silu(cv) * up)
    hff = _conv_fwd_call("ffn_conv_fwd", out_shape=jax.ShapeDtypeStruct((S, FFN), BF16), out_spec=colS(cb_ff), **ff_args)
    ffo = _matmul(hff, w_dn_f, "nn", F32, "mm_down")

    gf2 = final_norm_g.reshape(1, D)

    def loss_bwd(a, f, gt, gf, t):
        val, vjp = jax.vjp(_f_loss, a, f, gt, gf, t)
        da, df, dgt, dgf, _ = vjp(jnp.ones((), F32))
        return da, df, dgt, dgf, jnp.zeros((1, LANES), F32) + val

    dx1_l, dffo, dgt2, dgf, loss_v = _ew(
        loss_bwd, "loss_bwd", (nb,), [x1, ffo, gt2, gf2, tgt], [rowD, rowD, vecD, vecD, rowD],
        [((S, D), F32, rowD), ((S, D), BF16, rowD), ((1, D), F32, vecD), ((1, D), F32, vecD),
         ((1, LANES), F32, vecL)], acc=(2, 3, 4))

    dhff = _matmul(dffo, w_dn_f, "nt", F32, "mm_down_dx")
    g_w_dn = _matmul(hff, dffo, "tn", F32, "mm_down_dw")

    d_gate, d_up, g_ffcw = _conv_bwd_call(
        "ffn_conv_bwd", dout=dhff, dout_spec=colS(cb_ff),
        din_shapes=[jax.ShapeDtypeStruct((S, FFN), BF16)], din_specs=[colS(cb_ff)],
        dex_shapes=[jax.ShapeDtypeStruct((S, FFN), BF16)], dex_specs=[colS(cb_ff)],
        dw_shape=jax.ShapeDtypeStruct((KFF, FFN), F32), dw_spec=wS(KFF, cb_ff), **ff_args)
    d_upall = jnp.concatenate([d_gate, d_up], axis=1)
    dhn2 = _matmul(d_upall, w_up_f, "nt", F32, "mm_up_dx")
    g_w_up = _matmul(hn2, d_upall, "tn", F32, "mm_up_dw")

    def res2_bwd(a, mx, gt, g, sc, sh, dx1, dhn):
        _, vjp = jax.vjp(_f_res_normmod, a, mx, gt, g, sc, sh)
        return vjp((dx1, dhn))

    dx_r, dmix, dgt1, dg2, dsc2, dsh2 = _ew(
        res2_bwd, "norm2_bwd", (nb,), [x2, mix, gt1, norm2_g, sc2, sh2, dx1_l, dhn2],
        [rowD, rowD, vecD, vecD, vecD, vecD, rowD, rowD],
        [((S, D), F32, rowD), ((S, D), BF16, rowD)] + [((1, D), F32, vecD)] * 4, acc=(2, 3, 4, 5))

    dmerged = _matmul(dmix, w_out_f, "nt", F32, "mm_out_dx")
    g_w_out = _matmul(merged, dmix, "tn", F32, "mm_out_dw")

    def merge_bwd(a, b, ga, gb, dm):
        _, vjp = jax.vjp(_f_merge, a, b, ga, gb)
        return vjp(dm)

    d_bra, d_brb, d_ga, d_gb = _ew(merge_bwd, "merge_bwd", (nb,), [br_a, br_b, proj, proj, dmerged],
                                   [rowD, rowD, ga_spec, gb_spec, rowD], [((S, D), BF16, rowD)] * 4)

    d_on = _matmul(d_bra, w_do_f, "nt", F32, "mm_dn_o_dx")
    g_w_do = _matmul(on, d_bra, "tn", F32, "mm_dn_o_dw")
    d_ub = _matmul(d_brb, w_co_f, "nt", F32, "mm_cf_o_dx")
    g_w_co = _matmul(ub, d_brb, "tn", F32, "mm_cf_o_dw")

    def cf_ln_bwd(u, g, b, du):
        _, vjp = jax.vjp(_f_cf_ln, u, g, b)
        return vjp(du)

    d_uc, g_cflg, g_cflb = _ew(cf_ln_bwd, "cf_ln_bwd", (nb,), [uc, cf_ln_g, cf_ln_b, d_ub], [rowC, vecC, vecC, rowC],
                               [((S, CFW), F32, rowC), ((1, CFW), F32, vecC), ((1, CFW), F32, vecC)], acc=(1, 2))
    d_val, d_gl, g_cfcw = _conv_bwd_call(
        "cf_conv_bwd", dout=d_uc, dout_spec=colS(cb_cf),
        din_shapes=[jax.ShapeDtypeStruct((S, CFW), BF16)] * 2, din_specs=[colS(cb_cf)] * 2,
        dex_shapes=[], dex_specs=[],
        dw_shape=jax.ShapeDtypeStruct((KCF, CFW), F32), dw_spec=wS(KCF, cb_cf), **cf_args)

    def dn_post_bwd(o, z, g, d):
        _, vjp = jax.vjp(_f_dn_post, o, z, g)
        return vjp(d)

    d_o, d_z, g_dnng = _ew(dn_post_bwd, "dn_post_bwd", (nbh, NH), [o_dn, proj, dn_norm_g, d_on],
                           [o_spec, z_spec, ng_spec, oh_spec],
                           [((NH, S, DH), F32, o_spec), ((S, DNW), BF16, oh_spec), ((1, DH), F32, ng_spec)], acc=(2,))

    dq, dk, dv, dgb_b, dbb_b = _dn_bwd_call(qkvn, gb_b, bb_b, states, d_o)
    dqkvn = jnp.stack([dq, dk, dv])
    d_qkv, g_dncw = _conv_bwd_call(
        "dn_conv_bwd", dout=dqkvn, dout_spec=qkv_spec,
        din_shapes=[jax.ShapeDtypeStruct((S, 3 * DNW), BF16)], din_specs=[colS(DH)], dex_shapes=[], dex_specs=[],
        dw_shape=jax.ShapeDtypeStruct((KDN, 3 * DNW), F32), dw_spec=wS(KDN, DH), **dn_args)

    dbg = jnp.concatenate([jnp.transpose(dbb_b[:, :, 0]), jnp.transpose(dgb_b[:, :, 0]),
                           jnp.zeros((S, LANES - 2 * NH), F32)], axis=1)

    def gate_bwd(a, p, q, d):
        _, vjp = jax.vjp(gate_fn, a, p, q)
        return vjp(d)

    d_ba, g_alp, g_dtb = _ew(gate_bwd, "dn_gate_bwd", (nb,), [proj, alp, dtb, dbg], [ba_spec, vecL, vecL, rowL],
                             [((S, LANES), BF16, rowL), ((1, LANES), F32, vecL), ((1, LANES), F32, vecL)], acc=(1, 2))

    dproj = jnp.concatenate([d_qkv, d_z, d_val, d_gl, d_ga, d_gb, d_ba], axis=1)
    dhn1 = _matmul(dproj, w_aug, "nt", F32, "mm_in_dx")
    g_w_aug = _matmul(hn1, dproj, "tn", F32, "mm_in_dw")

    def norm1_bwd(a, g, sc, sh, dhn, dxr):
        _, vjp = jax.vjp(_f_normmod, a, g, sc, sh)
        da, dg, dsc, dsh = vjp(dhn)
        return da + dxr, dg, dsc, dsh

    grad_x, dg1, dsc1, dsh1 = _ew(norm1_bwd, "norm1_bwd", (nb,), [x2, norm1_g, sc1, sh1, dhn1, dx_r],
                                  [rowD, vecD, vecD, vecD, rowD, rowD],
                                  [((S, D), F32, rowD)] + [((1, D), F32, vecD)] * 3, acc=(1, 2, 3))

    g_w_in = jnp.concatenate([g_w_aug[:, :o_b], g_w_aug[:, NA:NA + 2 * NH], g_w_aug[:, o_b:NA]], axis=1)
    grads_col = [g_w_in, g_w_do, g_w_co, g_w_up]
    grads_row = [g_w_out, g_w_dn]
    gp = []
    for gfull in grads_col:
        r, cc = gfull.shape
        gp.append(_pack_rows128(jnp.transpose(gfull.reshape(r, 4, cc // 4), (1, 0, 2))))
    for gfull in grads_row:
        r, cc = gfull.shape
        gp.append(_pack_rows128(gfull.reshape(4, r // 4, cc)))
    if rt_pad > rt:
        gp.append(jnp.zeros((4, rt_pad - rt, LANES), F32))
    gpk = jnp.concatenate(gp, axis=1)
    rb = _pair_swap_half(gpk)
    q16 = _pair_add(gpk, rb, core)
    r2 = _chip_scatter(q16)
    half = _sum_slots(r2, "rs_chip_sum")
    gall = _pair_gather(half).reshape(rt_pad, LANES)
    g_big = [_unpack_rows128(gall[offs[i]:offs[i] + sizes[i]], *big[i].shape) for i in range(len(big))]
    gs_w_in, gs_w_do, gs_w_co, gs_w_up, gs_w_out, gs_w_dn = g_big

    dmod = jnp.concatenate([dsh1, dsc1, dgt1, dsh2, dsc2, dgt2], axis=1)
    sm2 = [dmod, dg1, dg2, dgf, g_alp, g_dtb, g_dnng, g_cflg, g_cflb, g_dncw, g_cfcw, g_ffcw]
    sm2_shapes = [tuple(a.shape) for a in sm2]
    g3 = _all_gather_small(_pack_small(sm2))
    ssum = _sum_slots(g3, "small_sum")
    dmod_all = _unpack_small(g3, sm2_shapes[:1])[0].reshape(8, 6 * D)
    (g_b_ada, gs_n1, gs_n2, gs_fn, gs_alp, gs_dtb, gs_dnng, gs_cflg, gs_cflb, gs_dncw, gs_cfcw,
     gs_ffcw) = _unpack_small(ssum, sm2_shapes)
    gs_alog, gs_dtbias = gs_alp[:, NH:2 * NH], gs_dtb[:, NH:2 * NH]
    gs_dncw = lax.dynamic_slice(gs_dncw, (0, chip * (3 * DNW // 4)), (KDN, 3 * DNW // 4))
    gs_cfcw = lax.dynamic_slice(gs_cfcw, (0, chip * (CFW // 4)), (KCF, CFW // 4))
    gs_ffcw = lax.dynamic_slice(gs_ffcw, (0, chip * (FFN // 4)), (KFF, FFN // 4))

    dmod_sh = lax.dynamic_slice(dmod_all, (0, chip * CA), (8, CA))

    def wada_fn(cc, dm):
        return (_mm(_silu(cc), dm, TN),)

    g_w_ada = _ew(wada_fn, "ada_dw", (CA // tn_a,), [c_all, dmod_sh],
                  [_full((8, D)), pl.BlockSpec((8, tn_a), lambda j: (0, j))],
                  [((D, CA), F32, pl.BlockSpec((D, tn_a), lambda j: (0, j)))])[0]

    loss = lax.psum(loss_v[0, 0], ("x", "y", "c"))

    names = ["w_ada", "b_ada", "norm1_g", "w_in", "dn_conv_w", "dn_a_log", "dn_dt_bias", "dn_norm_g", "dn_w_o",
             "cf_conv_w", "cf_ln_g", "cf_ln_b", "cf_w_o", "w_out", "norm2_g", "ffn_w_up", "ffn_conv_w", "ffn_w_down",
             "final_norm_g"]
    wts = dict(zip(names, [w_ada, b_ada, norm1_g, w_in, dn_conv_w, dn_a_log, dn_dt_bias, dn_norm_g, dn_w_o, cf_conv_w,
                           cf_ln_g, cf_ln_b, cf_w_o, w_out, norm2_g, ffn_w_up, ffn_conv_w, ffn_w_down, final_norm_g]))
    ms = dict(zip(names, [m_w_ada, m_b_ada, m_norm1_g, m_w_in, m_dn_conv_w, m_dn_a_log, m_dn_dt_bias, m_dn_norm_g,
                          m_dn_w_o, m_cf_conv_w, m_cf_ln_g, m_cf_ln_b, m_cf_w_o, m_w_out, m_norm2_g, m_ffn_w_up,
                          m_ffn_conv_w, m_ffn_w_down, m_final_norm_g]))
    vs = dict(zip(names, [v_w_ada, v_b_ada, v_norm1_g, v_w_in, v_dn_conv_w, v_dn_a_log, v_dn_dt_bias, v_dn_norm_g,
                          v_dn_w_o, v_cf_conv_w, v_cf_ln_g, v_cf_ln_b, v_cf_w_o, v_w_out, v_norm2_g, v_ffn_w_up,
                          v_ffn_conv_w, v_ffn_w_down, v_final_norm_g]))
    grads = {"w_ada": g_w_ada, "b_ada": g_b_ada, "norm1_g": gs_n1, "w_in": gs_w_in, "dn_conv_w": gs_dncw,
             "dn_a_log": gs_alog, "dn_dt_bias": gs_dtbias, "dn_norm_g": gs_dnng, "dn_w_o": gs_w_do,
             "cf_conv_w": gs_cfcw, "cf_ln_g": gs_cflg, "cf_ln_b": gs_cflb, "cf_w_o": gs_w_co, "w_out": gs_w_out,
             "norm2_g": gs_n2, "ffn_w_up": gs_w_up, "ffn_conv_w": gs_ffcw, "ffn_w_down": gs_w_dn,
             "final_norm_g": gs_fn}
    grads = {n: grads[n].reshape(wts[n].shape) for n in names}

    large = ["w_ada", "w_in", "dn_w_o", "cf_w_o", "w_out", "ffn_w_up", "ffn_w_down"]
    small = [n for n in names if n not in large]
    delta, new_m, new_v = {}, {}, {}
    for n in large:
        d_, m_, v_ = _adamw(wts[n][0], grads[n][0], ms[n][0], vs[n][0], "adamw_" + n)
        delta[n], new_m[n], new_v[n] = d_[None], m_[None], v_[None]
    sm_sh = [tuple(wts[n].shape) for n in small]
    d_, m_, v_ = _adamw(_pack_small([wts[n] for n in small]), _pack_small([grads[n] for n in small]),
                        _pack_small([ms[n] for n in small]), _pack_small([vs[n] for n in small]), "adamw_small")
    for n, a, b_, c_ in zip(small, _unpack_small(d_, sm_sh), _unpack_small(m_, sm_sh), _unpack_small(v_, sm_sh)):
        delta[n], new_m[n], new_v[n] = a, b_, c_

    return (loss, grad_x[None], *[grads[n] for n in names], *[delta[n] for n in names],
            *[new_m[n] for n in names], *[new_v[n] for n in names])
```

```python
import functools

import jax
import jax.numpy as jnp
from jax import lax
from jax.experimental import pallas as pl
from jax.experimental.pallas import tpu as pltpu

F32 = jnp.float32
BF16 = jnp.bfloat16
EPS = 1e-6
LANES = 128
VMEM_LIMIT = 48 * 1024 * 1024
DN_CHUNK = 128
ADAM_LR, ADAM_B1, ADAM_B2, ADAM_EPS, ADAM_WD, ADAM_STEP = 0.001, 0.9, 0.999, 1e-08, 0.01, 10
MESH = pl.DeviceIdType.MESH

NN = (((1,), (0,)), ((), ()))
NT = (((1,), (1,)), ((), ()))
TN = (((0,), (0,)), ((), ()))
_DIMS = {"nn": NN, "nt": NT, "tn": TN}


def _mm(a, b, dims):
    return lax.dot_general(a.astype(BF16), b.astype(BF16), dims, preferred_element_type=F32)


def _mmx(a, b, dims):
    return lax.dot_general(a, b, dims, precision=lax.Precision.HIGHEST, preferred_element_type=F32)


def _div_tile(n, target, mult):
    best = None
    t = mult
    while t <= min(n, target):
        if n % t == 0:
            best = t
        t += mult
    return best if best is not None else n


def _params(sem=None):
    kw = dict(vmem_limit_bytes=VMEM_LIMIT)
    if sem is not None:
        kw["dimension_semantics"] = sem
    return pltpu.CompilerParams(**kw)


def _sigmoid(x):
    return jax.nn.sigmoid(x)


def _silu(x):
    return x * jax.nn.sigmoid(x)


def _softplus(x):
    return jnp.maximum(x, 0.0) + jnp.log(1.0 + jnp.exp(-jnp.abs(x)))


def _view(arr):
    if arr.ndim == 3:
        return arr.shape[1], arr.shape[0] * arr.shape[2], arr.shape[0]
    return arr.shape[0], arr.shape[1], 1


def _tile_spec(groups, cols, tr, tc, rsel, csel):
    if groups > 1:
        per = cols // groups // tc
        return pl.BlockSpec((None, tr, tc), lambda i, j, k: (csel(i, j, k) // per, rsel(i, j, k), csel(i, j, k) % per))
    return pl.BlockSpec((tr, tc), lambda i, j, k: (rsel(i, j, k), csel(i, j, k)))


def _matmul(a, b, mode, out_dtype, name, out_groups=1):
    ar, ac, ag = _view(a)
    br, bc, bg = _view(b)
    if mode == "nn":
        M, K, N = ar, ac, bc
        kdiv, mdiv, ndiv = ac // ag, M, min(bc // bg, N // out_groups)
    elif mode == "nt":
        M, K, N = ar, ac, br
        kdiv, mdiv, ndiv = min(ac // ag, bc // bg), M, N // out_groups
    else:
        K, M, N = ar, ac, bc
        kdiv, mdiv, ndiv = K, ac // ag, min(bc // bg, N // out_groups)
    tm = _div_tile(mdiv, 1024, LANES)
    tn = _div_tile(ndiv, 1536, LANES)
    tk = _div_tile(kdiv, 1536, LANES)
    nk = K // tk
    dims = _DIMS[mode]
    si, sj, sk = (lambda i, j, k: i), (lambda i, j, k: j), (lambda i, j, k: k)
    a_spec = {"nn": _tile_spec(ag, ac, tm, tk, si, sk), "nt": _tile_spec(ag, ac, tm, tk, si, sk),
              "tn": _tile_spec(ag, ac, tk, tm, sk, si)}[mode]
    b_spec = {"nn": _tile_spec(bg, bc, tk, tn, sk, sj), "nt": _tile_spec(bg, bc, tn, tk, sj, sk),
              "tn": _tile_spec(bg, bc, tk, tn, sk, sj)}[mode]
    out_shape = (M, N) if out_groups == 1 else (out_groups, M, N // out_groups)

    def body(a_ref, b_ref, o_ref, acc_ref):
        k = pl.program_id(2)

        @pl.when(k == 0)
        def _():
            acc_ref[...] = jnp.zeros_like(acc_ref)

        acc_ref[...] += lax.dot_general(a_ref[...], b_ref[...], dims, preferred_element_type=F32)

        @pl.when(k == nk - 1)
        def _():
            o_ref[...] = acc_ref[...].astype(o_ref.dtype)

    return pl.pallas_call(
        body, name=name, grid=(M // tm, N // tn, nk),
        in_specs=[a_spec, b_spec], out_specs=_tile_spec(out_groups, N, tm, tn, si, sj),
        out_shape=jax.ShapeDtypeStruct(out_shape, out_dtype),
        scratch_shapes=[pltpu.VMEM((tm, tn), F32)],
        compiler_params=_params(("parallel", "parallel", "arbitrary")),
    )(a, b)


def _ew(fn, name, grid, ins, in_specs, outs, acc=(), alias=None):
    n_in = len(ins)
    n_ax = len(grid)
    extra, extra_specs, aliases = [], [], {}
    if alias is not None:
        extra, extra_specs, aliases = [alias[0]], [pl.BlockSpec(memory_space=pl.ANY)], {n_in: alias[1]}

    def body(*refs):
        in_refs, out_refs = refs[:n_in], refs[n_in + len(extra):]
        ids = [pl.program_id(a) for a in range(n_ax)]
        res = fn(*[r[...] for r in in_refs])
        first = ids[0] == 0
        for t in ids[1:]:
            first = jnp.logical_and(first, t == 0)
        for idx, (r, val) in enumerate(zip(out_refs, res)):
            if idx in acc:
                @pl.when(first)
                def _(r=r, val=val):
                    r[...] = val.astype(r.dtype)

                @pl.when(jnp.logical_not(first))
                def _(r=r, val=val):
                    r[...] += val.astype(r.dtype)
            else:
                r[...] = val.astype(r.dtype)

    return pl.pallas_call(
        body, name=name, grid=grid, in_specs=list(in_specs) + extra_specs,
        out_specs=[o[2] for o in outs],
        out_shape=[jax.ShapeDtypeStruct(o[0], o[1]) for o in outs],
        input_output_aliases=aliases,
        compiler_params=_params(("arbitrary",) * n_ax),
    )(*ins, *extra)


def _ew_slot(fn, name, grid, slots, ins, in_specs, out_shape, out_dtype, out_spec):
    def body(s_ref, *refs):
        refs[-1][...] = fn(*[r[...] for r in refs[:-1]]).astype(refs[-1].dtype)

    return pl.pallas_call(
        body, name=name,
        grid_spec=pltpu.PrefetchScalarGridSpec(num_scalar_prefetch=1, grid=grid, in_specs=list(in_specs),
                                               out_specs=out_spec),
        out_shape=jax.ShapeDtypeStruct(out_shape, out_dtype),
        compiler_params=_params(("arbitrary",) * len(grid)),
    )(slots, *ins)


def _row(bs, w, col=0):
    return pl.BlockSpec((bs, w), lambda i, col=col: (i, col))


def _full(shape):
    nd = len(shape)
    return pl.BlockSpec(tuple(shape), lambda *_: (0,) * nd)


def _rms(x, g):
    return x * lax.rsqrt(jnp.mean(x * x, axis=-1, keepdims=True) + EPS) * g


def _f_normmod(x, g, sc, sh):
    return _rms(x, g) * (1.0 + sc) + sh


def _f_res_normmod(x, mix, gt, g, sc, sh):
    x1 = x + gt * mix
    return x1, _f_normmod(x1, g, sc, sh)


def _f_loss(x1, f, gt, gf, tgt):
    y = _rms(x1 + gt * f, gf)
    return 0.5 * jnp.sum(jnp.mean(jnp.square(y - tgt), axis=-1))


def _f_dn_gate(nh, ba, alp, dtb):
    lane = lax.broadcasted_iota(jnp.int32, ba.shape, 1)
    m = (lane < nh).astype(F32)
    beta = _sigmoid(ba)
    g = -jnp.exp(alp) * _softplus(ba + dtb)
    return m * beta + (1.0 - m) * g


def _f_dn_post(o, z, g):
    return o * lax.rsqrt(jnp.mean(o * o, axis=-1, keepdims=True) + EPS) * g * _silu(z)


def _f_cf_ln(u, g, b):
    mu = jnp.mean(u, axis=-1, keepdims=True)
    xc = u - mu
    y = xc * lax.rsqrt(jnp.mean(xc * xc, axis=-1, keepdims=True) + EPS)
    return _silu(y * g + b)


def _f_merge(a, b, ga, gb):
    return _sigmoid(ga) * a + _sigmoid(gb) * b


def _shift_down(u, d, rows):
    if d == 0:
        return u
    return jnp.where(rows >= d, pltpu.roll(u, d, 0), 0.0)


def _shift_up(u, d, rows):
    if d == 0:
        return u
    s = u.shape[0]
    return jnp.where(rows < s - d, pltpu.roll(u, s - d, 0), 0.0)


def _conv(u, w_ref, kw, rows):
    acc = None
    for k in range(kw):
        t = w_ref[k:k + 1, :] * _shift_down(u, kw - 1 - k, rows)
        acc = t if acc is None else acc + t
    return acc


def _conv_t(dc, w_ref, kw, rows):
    acc = None
    for k in range(kw):
        t = w_ref[k:k + 1, :] * _shift_up(dc, kw - 1 - k, rows)
        acc = t if acc is None else acc + t
    return acc


def _conv_fwd_call(name, kw, ncol, ins, in_specs, extras, extra_specs, w, w_spec, pre, post, out_shape, out_spec):
    n_in, n_ex = len(ins), len(extras)

    def body(*refs):
        in_refs, ex_refs = refs[:n_in], refs[n_in:n_in + n_ex]
        w_ref, out_ref = refs[n_in + n_ex], refs[n_in + n_ex + 1]
        j = pl.program_id(0)
        u = pre(*[r[...] for r in in_refs])
        rows = lax.broadcasted_iota(jnp.int32, u.shape, 0)
        cv = _conv(u, w_ref, kw, rows)
        out_ref[...] = post(j, cv, *[r[...] for r in ex_refs]).astype(out_ref.dtype)

    return pl.pallas_call(
        body, name=name, grid=(ncol,), in_specs=list(in_specs) + list(extra_specs) + [w_spec],
        out_specs=out_spec, out_shape=out_shape, compiler_params=_params(("arbitrary",)),
    )(*ins, *extras, w)


def _conv_bwd_call(name, kw, ncol, ins, in_specs, extras, extra_specs, w, w_spec, pre, post, dout, dout_spec,
                   dio_shapes, dio_specs, dio_pack, dw_shape, dw_spec, alias=None):
    n_in, n_ex, n_io = len(ins), len(extras), len(dio_shapes)
    al, al_specs, aliases = [], [], {}
    if alias is not None:
        al, al_specs, aliases = [alias], [pl.BlockSpec(memory_space=pl.ANY)], {n_in + n_ex + 2: 0}

    def body(*refs):
        in_refs, ex_refs = refs[:n_in], refs[n_in:n_in + n_ex]
        w_ref, dout_ref = refs[n_in + n_ex], refs[n_in + n_ex + 1]
        outs = refs[n_in + n_ex + 2 + len(al):]
        dio_refs, dw_ref = outs[:n_io], outs[n_io]
        j = pl.program_id(0)
        u, pre_vjp = jax.vjp(pre, *[r[...] for r in in_refs])
        rows = lax.broadcasted_iota(jnp.int32, u.shape, 0)
        cv = _conv(u, w_ref, kw, rows)
        _, post_vjp = jax.vjp(lambda cc, *ex: post(j, cc, *ex), cv, *[r[...] for r in ex_refs])
        g = post_vjp(dout_ref[...].astype(F32))
        dc = g[0]
        for k in range(kw):
            dw_ref[k:k + 1, :] = jnp.sum(dc * _shift_down(u, kw - 1 - k, rows), axis=0, keepdims=True)
        du = _conv_t(dc, w_ref, kw, rows)
        for r, val in zip(dio_refs, dio_pack(*pre_vjp(du), *g[1:])):
            if isinstance(val, tuple):
                for t, part in enumerate(val):
                    r[t] = part.astype(r.dtype)
            else:
                r[...] = val.astype(r.dtype)

    return pl.pallas_call(
        body, name=name, grid=(ncol,),
        in_specs=list(in_specs) + list(extra_specs) + [w_spec, dout_spec] + al_specs,
        out_specs=list(dio_specs) + [dw_spec],
        out_shape=list(dio_shapes) + [dw_shape],
        input_output_aliases=aliases,
        compiler_params=_params(("arbitrary",)),
    )(*ins, *extras, w, dout, *al)


def _tri_inverse(a):
    c = a.shape[0]
    ii = lax.broadcasted_iota(jnp.int32, (c, c), 0)
    jj = lax.broadcasted_iota(jnp.int32, (c, c), 1)
    y = -a
    t = (ii == jj).astype(F32) + y
    p = 2
    while p < c:
        y = _mmx(y, y, NN)
        t = t + _mmx(t, y, NN)
        p *= 2
    return t


def _dn_common(q, k, v, gb, bb):
    c = q.shape[0]
    ii = lax.broadcasted_iota(jnp.int32, (c, c), 0)
    jj = lax.broadcasted_iota(jnp.int32, (c, c), 1)
    causal = jj <= ii
    strict = jj < ii
    low = causal.astype(F32)
    ones = jnp.ones((c, LANES), F32)
    gc = _mmx(low, gb, NN)
    diff = (_mmx(gc, ones, NT) - _mmx(ones, gc, NT)) * (1.0 / LANES)
    decay = jnp.where(causal, jnp.exp(jnp.where(causal, diff, 0.0)), 0.0)
    gl = jnp.sum(gb, axis=0, keepdims=True)
    eg = jnp.exp(gc)
    egm = jnp.exp(gl - gc)
    egl = jnp.exp(gl)
    kb = k * bb
    vb = v * bb
    kbg = kb * eg
    kk = _mm(kb, k, NT)
    t = _tri_inverse(jnp.where(strict, kk * decay, 0.0))
    qk = _mm(q, k, NT)
    attn = qk * decay
    return dict(causal=causal, strict=strict, low=low, ones=ones, decay=decay, eg=eg, egm=egm, egl=egl,
                kb=kb, vb=vb, kbg=kbg, kk=kk, t=t, qk=qk, attn=attn, qd=q * eg, kd=k * egm)


def _dn_fwd_chunk(q, k, v, gb, bb, s):
    m = _dn_common(q, k, v, gb, bb)
    u = _mmx(m["t"], m["vb"] - _mm(m["kbg"], s, NN), NN)
    o = _mm(m["qd"], s, NN) + _mm(m["attn"], u, NN)
    s2 = s * m["egl"] + _mm(m["kd"], u, TN)
    return o, s2


def _dn_bwd_chunk(q, k, v, gb, bb, s, do, dsp):
    m = _dn_common(q, k, v, gb, bb)
    c = q.shape[0]
    t, decay, eg, egm, egl = m["t"], m["decay"], m["eg"], m["egm"], m["egl"]
    u = _mmx(t, m["vb"] - _mm(m["kbg"], s, NN), NN)
    du = _mm(m["attn"], do, TN) + _mm(m["kd"], dsp, NN)
    dattn = jnp.where(m["causal"], _mm(do, u, NT), 0.0)
    dqd = _mm(do, s, NT)
    dkd = _mm(u, dsp, NT)
    dr = _mmx(t, du, TN)
    da = jnp.where(m["strict"], -_mm(dr, u, NT), 0.0)
    dkbg = -_mm(dr, s, NT)
    ds = dsp * egl + _mm(m["qd"], do, TN) - _mm(m["kbg"], dr, TN)
    degl = jnp.sum(jnp.sum(dsp * s, axis=1, keepdims=True), axis=0, keepdims=True)
    dkk = da * decay
    dqk = dattn * decay
    ddiff = (da * m["kk"] + dattn * m["qk"]) * decay
    dgc = _mmx(ddiff, m["ones"], NN) - _mmx(ddiff, m["ones"], TN)
    dkb = _mm(dkk, k, NN) + dkbg * eg
    dk = _mm(dkk, m["kb"], TN) + _mm(dqk, q, TN) + dkd * egm + dkb * bb
    dq = _mm(dqk, k, NN) + dqd * eg
    dgc = dgc + jnp.sum(dqd * q + dkbg * m["kb"], axis=-1, keepdims=True) * eg
    tt = jnp.sum(dkd * k, axis=-1, keepdims=True) * egm
    dgc = dgc - tt
    dgl = jnp.sum(tt, axis=0, keepdims=True) + degl * egl
    dbb = jnp.sum(dkb * k + dr * v, axis=-1, keepdims=True) + jnp.zeros((c, LANES), F32)
    dv = dr * bb
    dgb = _mmx(m["low"], dgc, TN) + dgl
    return dq, dk, dv, dgb, dbb, ds


def _dn_fwd_call(qkvn, gb, bb):
    _, nh, s, dh = qkvn.shape
    c = min(DN_CHUNK, s)
    n = s // c
    hb = nh

    def body(q_ref, k_ref, v_ref, g_ref, b_ref, o_ref, st_ref, s_ref):
        @pl.when(pl.program_id(1) == 0)
        def _():
            s_ref[...] = jnp.zeros_like(s_ref)

        for h in range(hb):
            st = s_ref[h]
            st_ref[h] = st
            o, s2 = _dn_fwd_chunk(q_ref[h], k_ref[h], v_ref[h], g_ref[h], b_ref[h], st)
            o_ref[h] = o
            s_ref[h] = s2

    def qspec(t):
        return pl.BlockSpec((None, hb, c, dh), lambda i, j, t=t: (t, i, j, 0))

    hs = pl.BlockSpec((hb, c, dh), lambda i, j: (i, j, 0))
    return pl.pallas_call(
        body, name="dn_fwd", grid=(nh // hb, n),
        in_specs=[qspec(0), qspec(1), qspec(2), hs, hs],
        out_specs=[hs, pl.BlockSpec((hb, None, dh, dh), lambda i, j: (i, j, 0, 0))],
        out_shape=[jax.ShapeDtypeStruct((nh, s, dh), F32), jax.ShapeDtypeStruct((nh, n, dh, dh), F32)],
        scratch_shapes=[pltpu.VMEM((hb, dh, dh), F32)],
        compiler_params=_params(("arbitrary", "arbitrary")),
    )(qkvn, qkvn, qkvn, gb, bb)


def _dn_bwd_call(qkvn, gb, bb, states, do):
    _, nh, s, dh = qkvn.shape
    c = min(DN_CHUNK, s)
    n = s // c
    hb = nh

    def body(q_ref, k_ref, v_ref, g_ref, b_ref, st_ref, do_ref, dqkv_ref, dg_ref, db_ref, ds_ref):
        @pl.when(pl.program_id(1) == 0)
        def _():
            ds_ref[...] = jnp.zeros_like(ds_ref)

        for h in range(hb):
            dq, dk, dv, dg, db, ds = _dn_bwd_chunk(q_ref[h], k_ref[h], v_ref[h], g_ref[h], b_ref[h], st_ref[h],
                                                   do_ref[h], ds_ref[h])
            dqkv_ref[0, h] = dq
            dqkv_ref[1, h] = dk
            dqkv_ref[2, h] = dv
            dg_ref[h] = dg
            db_ref[h] = db
            ds_ref[h] = ds

    def qspec(t):
        return pl.BlockSpec((None, hb, c, dh), lambda i, j, t=t: (t, i, n - 1 - j, 0))

    hs = pl.BlockSpec((hb, c, dh), lambda i, j: (i, n - 1 - j, 0))
    sh = jax.ShapeDtypeStruct((nh, s, dh), F32)
    return pl.pallas_call(
        body, name="dn_bwd", grid=(nh // hb, n),
        in_specs=[qspec(0), qspec(1), qspec(2), hs, hs,
                  pl.BlockSpec((hb, None, dh, dh), lambda i, j: (i, n - 1 - j, 0, 0)), hs],
        out_specs=[pl.BlockSpec((3, hb, c, dh), lambda i, j: (0, i, n - 1 - j, 0)), hs, hs],
        out_shape=[jax.ShapeDtypeStruct((3, nh, s, dh), F32), sh, sh],
        scratch_shapes=[pltpu.VMEM((hb, dh, dh), F32)],
        compiler_params=_params(("arbitrary", "arbitrary")),
    )(qkvn, qkvn, qkvn, gb, bb, states, do)


def _adamw(w, g, m, v, name):
    r, c = w.shape
    br = _div_tile(r, max(8, (1 << 18) // max(c, 1)), 8)

    def fn(w, g, m, v):
        m = ADAM_B1 * m + (1.0 - ADAM_B1) * g
        v = ADAM_B2 * v + (1.0 - ADAM_B2) * jnp.square(g)
        m_hat = m / (1.0 - ADAM_B1 ** ADAM_STEP)
        v_hat = v / (1.0 - ADAM_B2 ** ADAM_STEP)
        delta = -ADAM_LR * (m_hat / (jnp.sqrt(v_hat) + ADAM_EPS) + ADAM_WD * w)
        return delta, m, v

    spec = pl.BlockSpec((br, c), lambda i: (i, 0))
    return _ew(fn, name, (r // br,), [w, g, m, v], [spec] * 4, [((r, c), F32, spec)] * 3)


def _coords():
    return lax.axis_index("x"), lax.axis_index("y"), lax.axis_index("c")


def _all_gather_small(v):
    r, w = v.shape

    def body(v_ref, out_ref, send_sems, recv_sems, local_sem):
        x, y, c = _coords()
        me = 4 * x + 2 * y + c
        mine = pltpu.make_async_copy(v_ref, out_ref.at[me], local_sem)
        mine.start()
        peers = []
        for k in range(1, 8):
            px = 1 - x if k & 4 else x
            py = 1 - y if k & 2 else y
            pc = 1 - c if k & 1 else c
            peers.append((px, py, pc))
        sends = []
        for k, peer in enumerate(peers):
            cp = pltpu.make_async_remote_copy(src_ref=v_ref, dst_ref=out_ref.at[me], send_sem=send_sems.at[k],
                                              recv_sem=recv_sems.at[k], device_id=peer, device_id_type=MESH)
            cp.start()
            sends.append(cp)
        for k, (px, py, pc) in enumerate(peers):
            pltpu.make_async_remote_copy(src_ref=v_ref, dst_ref=out_ref.at[4 * px + 2 * py + pc],
                                         send_sem=send_sems.at[k], recv_sem=recv_sems.at[k],
                                         device_id=(px, py, pc), device_id_type=MESH).wait_recv()
        for cp in sends:
            cp.wait_send()
        mine.wait()

    return pl.pallas_call(
        body, name="ag_small", out_shape=jax.ShapeDtypeStruct((8, r, w), v.dtype),
        in_specs=[pl.BlockSpec(memory_space=pltpu.VMEM)], out_specs=pl.BlockSpec(memory_space=pltpu.VMEM),
        scratch_shapes=[pltpu.SemaphoreType.DMA((7,)), pltpu.SemaphoreType.DMA((7,)), pltpu.SemaphoreType.DMA],
        compiler_params=pltpu.CompilerParams(vmem_limit_bytes=VMEM_LIMIT),
    )(v)


def _hbm_call(body, name, arrays, out_shapes, n_sems, aliases=None):
    hbm = pl.BlockSpec(memory_space=pltpu.HBM)
    return pl.pallas_call(
        body, name=name, out_shape=list(out_shapes), in_specs=[hbm] * len(arrays), out_specs=[hbm] * len(out_shapes),
        input_output_aliases=aliases or {},
        scratch_shapes=[pltpu.SemaphoreType.DMA((n_sems,)), pltpu.SemaphoreType.DMA((n_sems,))],
    )(*arrays)


def _half_rows(ref_shape, c):
    h = ref_shape[1] // 2
    return pl.ds(pl.multiple_of(c * h, 16), h), pl.ds(pl.multiple_of((1 - c) * h, 16), h)


def _ag_chips(bufs):
    n = len(bufs)

    def body(*refs):
        ins, outs, send_sems, recv_sems = refs[:n], refs[n:2 * n], refs[2 * n], refs[2 * n + 1]
        x, y, c = _coords()
        j = 2 * x + y
        chips = [(1 - x, y), (x, 1 - y), (1 - x, 1 - y)]

        def cp(w, k, slot, rows, to, src=None):
            dst = outs[w].at[slot, rows]
            return pltpu.make_async_remote_copy(src_ref=dst if src is None else src, dst_ref=dst,
                                                send_sem=send_sems.at[6 * w + k], recv_sem=recv_sems.at[6 * w + k],
                                                device_id=to, device_id_type=MESH)

        started = []
        for w in range(n):
            mine, _ = _half_rows(outs[w].shape, c)
            for k, (px, py) in enumerate(chips):
                f = cp(w, k, j, mine, (px, py, c), src=ins[w].at[j, mine])
                f.start()
                started.append(f)
        for w in range(n):
            mine, _ = _half_rows(outs[w].shape, c)
            for k, (px, py) in enumerate(chips):
                cp(w, k, 2 * px + py, mine, (x, y, c)).wait_recv()
                f = cp(w, 3 + k, 2 * px + py, mine, (x, y, 1 - c))
                f.start()
                started.append(f)
        for w in range(n):
            _, sib = _half_rows(outs[w].shape, c)
            for k, (px, py) in enumerate(chips):
                cp(w, 3 + k, 2 * px + py, sib, (x, y, c)).wait_recv()
        for f in started:
            f.wait_send()

    shapes = [jax.ShapeDtypeStruct(b.shape, b.dtype) for b in bufs]
    return _hbm_call(body, "ag_weights", bufs, shapes, 6 * n, aliases={i: i for i in range(n)})


def _pair_swap_half(ps):
    n = len(ps)

    def body(*refs):
        ins, outs, send_sems, recv_sems = refs[:n], refs[n:2 * n], refs[2 * n], refs[2 * n + 1]
        x, y, c = _coords()
        cps = []
        for w in range(n):
            _, other = _half_rows(ins[w].shape, c)
            cp = pltpu.make_async_remote_copy(src_ref=ins[w].at[:, other], dst_ref=outs[w], send_sem=send_sems.at[w],
                                              recv_sem=recv_sems.at[w], device_id=(x, y, 1 - c), device_id_type=MESH)
            cp.start()
            cps.append(cp)
        for cp in cps:
            cp.wait()

    shapes = [jax.ShapeDtypeStruct((p.shape[0], p.shape[1] // 2, p.shape[2]), p.dtype) for p in ps]
    return _hbm_call(body, "rs_pair", ps, shapes, n)


def _chip_scatter(qs):
    n = len(qs)

    def body(*refs):
        ins, outs, send_sems, recv_sems = refs[:n], refs[n:2 * n], refs[2 * n], refs[2 * n + 1]
        x, y, c = _coords()
        chips = [(1 - x, y), (x, 1 - y), (1 - x, 1 - y)]
        cps = []
        for w in range(n):
            for k, (px, py) in enumerate(chips):
                cp = pltpu.make_async_remote_copy(src_ref=ins[w].at[2 * px + py], dst_ref=outs[w].at[k],
                                                  send_sem=send_sems.at[3 * w + k], recv_sem=recv_sems.at[3 * w + k],
                                                  device_id=(px, py, c), device_id_type=MESH)
                cp.start()
                cps.append(cp)
        for cp in cps:
            cp.wait()

    shapes = [jax.ShapeDtypeStruct((3,) + q.shape[1:], q.dtype) for q in qs]
    return _hbm_call(body, "rs_chips", qs, shapes, 3 * n)


def _pair_gather(gs):
    n = len(gs)

    def body(*refs):
        ins, outs, send_sems, recv_sems = refs[:n], refs[n:2 * n], refs[2 * n], refs[2 * n + 1]
        x, y, c = _coords()
        cps = []
        for w in range(n):
            cp = pltpu.make_async_remote_copy(src_ref=ins[w].at[c], dst_ref=outs[w].at[c], send_sem=send_sems.at[w],
                                              recv_sem=recv_sems.at[w], device_id=(x, y, 1 - c), device_id_type=MESH)
            cp.start()
            cps.append(cp)
        for w, cp in enumerate(cps):
            pltpu.make_async_remote_copy(src_ref=ins[w].at[c], dst_ref=outs[w].at[1 - c], send_sem=send_sems.at[w],
                                         recv_sem=recv_sems.at[w], device_id=(x, y, 1 - c),
                                         device_id_type=MESH).wait_recv()
            cp.wait_send()

    shapes = [jax.ShapeDtypeStruct(g.shape, g.dtype) for g in gs]
    return _hbm_call(body, "rs_pair_gather", gs, shapes, n, aliases={i: i for i in range(n)})


def _slot_rows(h, cs):
    return _div_tile(h, max(16, (1 << 19) // cs), 16)


def _cast_into_slot(w, slot, name):
    r, cs = w.shape
    br = _slot_rows(r, cs)
    return _ew_slot(lambda a: a, name, (r // br,), slot, [w], [pl.BlockSpec((br, cs), lambda i, s: (i, 0))],
                    (4, r, cs), BF16, pl.BlockSpec((None, br, cs), lambda i, s: (s[0], i, 0)))


def _pair_add(p, rb, core, name):
    n, r, cs = p.shape
    h = r // 2
    br = _slot_rows(h, cs)
    nb = h // br
    return _ew_slot(lambda a, b: a + b, name, (n, nb), core, [p, rb],
                    [pl.BlockSpec((None, br, cs), lambda s, i, c: (s, c[0] * nb + i, 0)),
                     pl.BlockSpec((None, br, cs), lambda s, i, c: (s, i, 0))],
                    (n, h, cs), BF16, pl.BlockSpec((None, br, cs), lambda s, i, c: (s, i, 0)))


def _chip_sum(q, r3, slots, name):
    _, h, cs = q.shape
    br = _slot_rows(h, cs)

    def fn(a, b):
        acc = a.astype(F32)
        for k in range(3):
            acc = acc + b[k].astype(F32)
        return acc

    return _ew_slot(fn, name, (h // br,), slots, [q, r3],
                    [pl.BlockSpec((None, br, cs), lambda i, s: (s[0], i, 0)),
                     pl.BlockSpec((3, br, cs), lambda i, s: (0, i, 0))],
                    (2, h, cs), F32, pl.BlockSpec((None, br, cs), lambda i, s: (s[1], i, 0)))


def _sum_slots(r, name):
    n, h, w = r.shape
    br = _div_tile(h, 2048, 16)

    def fn(blk):
        acc = blk[0].astype(F32)
        for s in range(1, n):
            acc = acc + blk[s].astype(F32)
        return (acc,)

    return _ew(fn, name, (h // br,), [r], [pl.BlockSpec((n, br, w), lambda i: (0, i, 0))],
               [((h, w), F32, pl.BlockSpec((br, w), lambda i: (i, 0)))])[0]


def _pack_small(arrs):
    flat = jnp.concatenate([a.reshape(-1).astype(F32) for a in arrs])
    n = flat.shape[0]
    rows = -(-n // LANES)
    rows = -(-rows // 8) * 8
    return jnp.pad(flat, (0, rows * LANES - n)).reshape(rows, LANES)


def _unpack_small(p, shapes):
    lead = p.shape[:-2]
    flat = p.reshape(lead + (-1,))
    out, off = [], 0
    for sh in shapes:
        n = 1
        for d in sh:
            n *= d
        out.append(flat[..., off:off + n].reshape(lead + tuple(sh)))
        off += n
    return out


def kernel(x, c, w_ada, b_ada, norm1_g, w_in, dn_conv_w, dn_a_log, dn_dt_bias, dn_norm_g, dn_w_o, cf_conv_w, cf_ln_g, cf_ln_b, cf_w_o, w_out, norm2_g, ffn_w_up, ffn_conv_w, ffn_w_down, final_norm_g, loss_target, m_w_ada, m_b_ada, m_norm1_g, m_w_in, m_dn_conv_w, m_dn_a_log, m_dn_dt_bias, m_dn_norm_g, m_dn_w_o, m_cf_conv_w, m_cf_ln_g, m_cf_ln_b, m_cf_w_o, m_w_out, m_norm2_g, m_ffn_w_up, m_ffn_conv_w, m_ffn_w_down, m_final_norm_g, v_w_ada, v_b_ada, v_norm1_g, v_w_in, v_dn_conv_w, v_dn_a_log, v_dn_dt_bias, v_dn_norm_g, v_dn_w_o, v_cf_conv_w, v_cf_ln_g, v_cf_ln_b, v_cf_w_o, v_w_out, v_norm2_g, v_ffn_w_up, v_ffn_conv_w, v_ffn_w_down, v_final_norm_g):
    xi, yi, ci = _coords()
    chip = 2 * xi + yi
    me = 4 * xi + 2 * yi + ci
    core = jnp.reshape(ci, (1,)).astype(jnp.int32)

    S, D = x.shape[1], x.shape[2]
    NH = dn_a_log.shape[1]
    DH = dn_norm_g.shape[1]
    DNW = NH * DH
    CFW = cf_ln_g.shape[1]
    FFN = ffn_w_down.shape[1] * 4
    KDN, KCF, KFF = dn_conv_w.shape[1], cf_conv_w.shape[1], ffn_conv_w.shape[1]
    NIN = w_in.shape[2] * 4
    assert NIN == 4 * DNW + 2 * NH + 2 * CFW + 2 * D and DH == LANES and 2 * NH <= LANES
    x2, tgt = x[0], loss_target[0]

    sm_shapes = [(D,), (KDN, 3 * DNW // 4), (KCF, CFW // 4), (KFF, FFN // 4)]
    g1 = _all_gather_small(_pack_small([c[0], dn_conv_w[0], cf_conv_w[0], ffn_conv_w[0]]))
    c_all, dcw_s, ccw_s, fcw_s = _unpack_small(g1, sm_shapes)

    def chips_cols(t):
        t = t[0::2]
        return jnp.transpose(t, (1, 0, 2)).reshape(t.shape[1], -1)

    dn_cw, cf_cw, ff_cw = chips_cols(dcw_s), chips_cols(ccw_s), chips_cols(fcw_s)

    slot_chip = jnp.reshape(chip, (1,)).astype(jnp.int32)
    big = [w_in[0], dn_w_o[0], cf_w_o[0], ffn_w_up[0], w_out[0], ffn_w_down[0]]
    names_big = ["w_in", "dn_w_o", "cf_w_o", "ffn_w_up", "w_out", "ffn_w_down"]
    gathered = _ag_chips([_cast_into_slot(w, slot_chip, "cast_" + nm) for w, nm in zip(big, names_big)])
    w_in_g, w_do_f, w_co_f, w_up_f, w_out_g, w_dn_g = gathered
    w_out_f = w_out_g.reshape(-1, w_out_g.shape[2])
    w_dn_f = w_dn_g.reshape(-1, w_dn_g.shape[2])
    w_in_f = jnp.transpose(w_in_g, (1, 0, 2)).reshape(D, NIN)

    cb_cf = _div_tile(CFW, 256, LANES)
    o_b = 4 * DNW
    o_glu = o_b + 2 * NH
    o_ga = o_glu + 2 * CFW
    NA = NIN - 2 * NH
    a_z, a_ga, a_gb, a_glu, a_ba = 3 * DNW, 4 * DNW, 4 * DNW + D, 4 * DNW + 2 * D, NA

    def glu_pairs(t):
        return jnp.transpose(t.reshape(D, 2, CFW // cb_cf, cb_cf), (0, 2, 1, 3)).reshape(D, 2 * CFW)

    def glu_unpairs(t):
        return jnp.transpose(t.reshape(D, CFW // cb_cf, 2, cb_cf), (0, 2, 1, 3)).reshape(D, 2 * CFW)

    w_aug = jnp.concatenate([w_in_f[:, :o_b], w_in_f[:, o_ga:], glu_pairs(w_in_f[:, o_glu:o_ga]), w_in_f[:, o_b:o_glu],
                             jnp.zeros((D, LANES - 2 * NH), BF16)], axis=1)

    CA = w_ada.shape[2]
    b_sh = lax.dynamic_slice(b_ada, (0, chip * CA), (1, CA))
    tn_a = _div_tile(CA, 512, LANES)

    def mod_fn(cc, w, b):
        return (_mm(_silu(cc), w, NN) + b,)

    mod_sh = _ew(mod_fn, "ada_mod", (CA // tn_a,), [c_all, w_ada[0], b_sh],
                 [_full((8, D)), pl.BlockSpec((D, tn_a), lambda j: (0, j)), pl.BlockSpec((1, tn_a), lambda j: (0, j))],
                 [((8, CA), F32, pl.BlockSpec((8, tn_a), lambda j: (0, j)))])[0]
    g2 = _all_gather_small(_pack_small([mod_sh]))
    mod_all = _unpack_small(g2, [(8, CA)])[0][0::2]
    mod_all = jnp.transpose(mod_all, (1, 0, 2)).reshape(8, 4 * CA)
    mod_me = lax.dynamic_slice(mod_all, (me, 0), (1, 6 * D))
    sh1, sc1, gt1, sh2, sc2, gt2 = [mod_me[:, i * D:(i + 1) * D] for i in range(6)]

    bs = _div_tile(S, 128, 8)
    nb = S // bs
    vecD = _full((1, D))
    rowD = _row(bs, D)

    hn1 = _ew(lambda a, g, sc, sh: (_f_normmod(a, g, sc, sh),), "norm1_fwd", (nb,),
              [x2, norm1_g, sc1, sh1], [rowD, vecD, vecD, vecD], [((S, D), BF16, rowD)])[0]
    proj = _matmul(hn1, w_aug, "nn", F32, "mm_in")

    def dn_post(j, cv):
        s = _silu(cv)
        nrm = s * lax.rsqrt(jnp.sum(s * s, axis=-1, keepdims=True) + EPS)
        fq = (j < NH).astype(F32)
        fk = (j < 2 * NH).astype(F32)
        scale = fq * (DH ** -0.5) + (1.0 - fq)
        return fk * (nrm * scale) + (1.0 - fk) * s

    def ident(a):
        return a

    def colS(w, off=0):
        return pl.BlockSpec((S, w), lambda j, off=off: (0, j + off))

    def wS(kw, w):
        return pl.BlockSpec((kw, w), lambda j: (0, j))

    qkv_spec = pl.BlockSpec((None, None, S, DH), lambda j: (j // NH, j % NH, 0, 0))
    dn_args = dict(kw=KDN, ncol=3 * NH, ins=[proj], in_specs=[colS(DH)], extras=[], extra_specs=[],
                   w=dn_cw, w_spec=wS(KDN, DH), pre=ident, post=dn_post)
    qkvn = _conv_fwd_call("dn_conv_fwd", out_shape=jax.ShapeDtypeStruct((3, NH, S, DH), F32), out_spec=qkv_spec,
                          **dn_args)

    alp = jnp.pad(dn_a_log, ((0, 0), (NH, LANES - 2 * NH)))
    dtb = jnp.pad(dn_dt_bias, ((0, 0), (NH, LANES - 2 * NH)))
    vecL = _full((1, LANES))
    ba_spec = _row(bs, LANES, a_ba // LANES)
    rowL = _row(bs, LANES)
    gate_fn = functools.partial(_f_dn_gate, NH)
    bg = _ew(lambda a, p, q: (gate_fn(a, p, q),), "dn_gate_fwd", (nb,), [proj, alp, dtb], [ba_spec, vecL, vecL],
             [((S, LANES), F32, rowL)])[0]

    def lanes_bcast(t):
        return jnp.broadcast_to(jnp.transpose(t)[:, :, None], (NH, S, DH))

    bb_b, gb_b = lanes_bcast(bg[:, :NH]), lanes_bcast(bg[:, NH:2 * NH])
    o_dn, states = _dn_fwd_call(qkvn, gb_b, bb_b)

    bsh = _div_tile(S, 512, 8)
    nbh = S // bsh
    o_spec = pl.BlockSpec((None, bsh, DH), lambda i, h: (h, i, 0))
    z_spec = pl.BlockSpec((bsh, DH), lambda i, h: (i, a_z // DH + h))
    oh_spec = pl.BlockSpec((bsh, DH), lambda i, h: (i, h))
    ng_spec = pl.BlockSpec((1, DH), lambda i, h: (0, 0))
    on = _ew(lambda o, z, g: (_f_dn_post(o, z, g),), "dn_post_fwd", (nbh, NH), [o_dn, proj, dn_norm_g],
             [o_spec, z_spec, ng_spec], [((S, DNW), BF16, oh_spec)])[0]
    br_a = _matmul(on, w_do_f, "nn", F32, "mm_dn_o")

    def glu_pre(val, gl):
        return val * _sigmoid(gl)

    def glu_spec(t):
        return pl.BlockSpec((S, cb_cf), lambda j, t=t: (0, a_glu // cb_cf + 2 * j + t))

    cf_args = dict(kw=KCF, ncol=CFW // cb_cf, ins=[proj, proj], in_specs=[glu_spec(0), glu_spec(1)],
                   extras=[], extra_specs=[], w=cf_cw, w_spec=wS(KCF, cb_cf), pre=glu_pre, post=lambda j, cv: cv)
    uc = _conv_fwd_call("cf_conv_fwd", out_shape=jax.ShapeDtypeStruct((S, CFW), F32), out_spec=colS(cb_cf), **cf_args)
    rowC = _row(bs, CFW)
    vecC = _full((1, CFW))
    ub = _ew(lambda u, g, b: (_f_cf_ln(u, g, b),), "cf_ln_fwd", (nb,), [uc, cf_ln_g, cf_ln_b], [rowC, vecC, vecC],
             [((S, CFW), BF16, rowC)])[0]
    br_b = _matmul(ub, w_co_f, "nn", F32, "mm_cf_o")

    ga_spec, gb_spec = _row(bs, D, a_ga // D), _row(bs, D, a_gb // D)
    merged = _ew(lambda a, b, ga, gb: (_f_merge(a, b, ga, gb),), "merge_fwd", (nb,), [br_a, br_b, proj, proj],
                 [rowD, rowD, ga_spec, gb_spec], [((S, D), BF16, rowD)])[0]
    mix = _matmul(merged, w_out_f, "nn", F32, "mm_out")

    x1, hn2 = _ew(_f_res_normmod, "norm2_fwd", (nb,), [x2, mix, gt1, norm2_g, sc2, sh2],
                  [rowD, rowD, vecD, vecD, vecD, vecD], [((S, D), F32, rowD), ((S, D), BF16, rowD)])
    up_all = _matmul(hn2, w_up_f, "nn", F32, "mm_up")

    cb_ff = _div_tile(FFN, 256, LANES)
    ff_args = dict(kw=KFF, ncol=FFN // cb_ff, ins=[up_all], in_specs=[colS(cb_ff)], extras=[up_all],
                   extra_specs=[colS(cb_ff, FFN // cb_ff)], w=ff_cw, w_spec=wS(KFF, cb_ff),
                   pre=ident, post=lambda j, cv, up: _silu(cv) * up)
    hff = _conv_fwd_call("ffn_conv_fwd", out_shape=jax.ShapeDtypeStruct((S, FFN), BF16), out_spec=colS(cb_ff), **ff_args)
    ffo = _matmul(hff, w_dn_f, "nn", F32, "mm_down")

    gf2 = final_norm_g.reshape(1, D)

    def loss_bwd(a, f, gt, gf, t):
        val, vjp = jax.vjp(_f_loss, a, f, gt, gf, t)
        da, df, dgt, dgf, _ = vjp(jnp.ones((), F32))
        return da, df, dgt, dgf, jnp.zeros((1, LANES), F32) + val

    dx1_l, dffo, dgt2, dgf, loss_v = _ew(
        loss_bwd, "loss_bwd", (nb,), [x1, ffo, gt2, gf2, tgt], [rowD, rowD, vecD, vecD, rowD],
        [((S, D), F32, rowD), ((S, D), BF16, rowD), ((1, D), F32, vecD), ((1, D), F32, vecD),
         ((1, LANES), F32, vecL)], acc=(2, 3, 4))

    dhff = _matmul(dffo, w_dn_f, "nt", F32, "mm_down_dx")
    g_w_dn = _matmul(hff, dffo, "tn", F32, "mm_down_dw")

    d_upall, g_ffcw = _conv_bwd_call(
        "ffn_conv_bwd", dout=dhff, dout_spec=colS(cb_ff),
        dio_shapes=[jax.ShapeDtypeStruct((2, S, FFN), BF16)],
        dio_specs=[pl.BlockSpec((2, S, cb_ff), lambda j: (0, 0, j))], dio_pack=lambda dg, du: [(dg, du)],
        dw_shape=jax.ShapeDtypeStruct((KFF, FFN), F32), dw_spec=wS(KFF, cb_ff), **ff_args)
    dhn2 = _matmul(d_upall, w_up_f, "nt", F32, "mm_up_dx")
    g_w_up = _matmul(hn2, d_upall, "tn", F32, "mm_up_dw", out_groups=4)

    def res2_bwd(a, mx, gt, g, sc, sh, dx1, dhn):
        _, vjp = jax.vjp(_f_res_normmod, a, mx, gt, g, sc, sh)
        return vjp((dx1, dhn))

    dx_r, dmix, dgt1, dg2, dsc2, dsh2 = _ew(
        res2_bwd, "norm2_bwd", (nb,), [x2, mix, gt1, norm2_g, sc2, sh2, dx1_l, dhn2],
        [rowD, rowD, vecD, vecD, vecD, vecD, rowD, rowD],
        [((S, D), F32, rowD), ((S, D), BF16, rowD)] + [((1, D), F32, vecD)] * 4, acc=(2, 3, 4, 5))

    dmerged = _matmul(dmix, w_out_f, "nt", F32, "mm_out_dx")
    g_w_out = _matmul(merged, dmix, "tn", F32, "mm_out_dw")

    def merge_bwd(a, b, ga, gb, dm):
        _, vjp = jax.vjp(_f_merge, a, b, ga, gb)
        da, db, dga, dgb = vjp(dm)
        return da, db, jnp.concatenate([dga, dgb], axis=1)

    d_bra, d_brb, dproj = _ew(merge_bwd, "merge_bwd", (nb,), [br_a, br_b, proj, proj, dmerged],
                              [rowD, rowD, ga_spec, gb_spec, rowD],
                              [((S, D), BF16, rowD), ((S, D), BF16, rowD),
                               ((S, NA + LANES), BF16, _row(bs, 2 * D, a_ga // (2 * D)))])

    d_on = _matmul(d_bra, w_do_f, "nt", F32, "mm_dn_o_dx")
    g_w_do = _matmul(on, d_bra, "tn", F32, "mm_dn_o_dw", out_groups=4)
    d_ub = _matmul(d_brb, w_co_f, "nt", F32, "mm_cf_o_dx")
    g_w_co = _matmul(ub, d_brb, "tn", F32, "mm_cf_o_dw", out_groups=4)

    def cf_ln_bwd(u, g, b, du):
        _, vjp = jax.vjp(_f_cf_ln, u, g, b)
        return vjp(du)

    d_uc, g_cflg, g_cflb = _ew(cf_ln_bwd, "cf_ln_bwd", (nb,), [uc, cf_ln_g, cf_ln_b, d_ub], [rowC, vecC, vecC, rowC],
                               [((S, CFW), F32, rowC), ((1, CFW), F32, vecC), ((1, CFW), F32, vecC)], acc=(1, 2))
    dproj_sds = jax.ShapeDtypeStruct((S, NA + LANES), BF16)
    dproj, g_cfcw = _conv_bwd_call(
        "cf_conv_bwd", dout=d_uc, dout_spec=colS(cb_cf), dio_shapes=[dproj_sds],
        dio_specs=[colS(2 * cb_cf, a_glu // (2 * cb_cf))], dio_pack=lambda dv, dg: [jnp.concatenate([dv, dg], axis=1)],
        dw_shape=jax.ShapeDtypeStruct((KCF, CFW), F32), dw_spec=wS(KCF, cb_cf), alias=dproj, **cf_args)

    def dn_post_bwd(o, z, g, d):
        _, vjp = jax.vjp(_f_dn_post, o, z, g)
        return vjp(d)

    d_o, dproj, g_dnng = _ew(dn_post_bwd, "dn_post_bwd", (nbh, NH), [o_dn, proj, dn_norm_g, d_on],
                             [o_spec, z_spec, ng_spec, oh_spec],
                             [((NH, S, DH), F32, o_spec), ((S, NA + LANES), BF16, z_spec), ((1, DH), F32, ng_spec)],
                             acc=(2,), alias=(dproj, 1))

    dqkvn, dgb_b, dbb_b = _dn_bwd_call(qkvn, gb_b, bb_b, states, d_o)
    dproj, g_dncw = _conv_bwd_call(
        "dn_conv_bwd", dout=dqkvn, dout_spec=qkv_spec, dio_shapes=[dproj_sds], dio_specs=[colS(DH)],
        dio_pack=lambda d: [d], dw_shape=jax.ShapeDtypeStruct((KDN, 3 * DNW), F32), dw_spec=wS(KDN, DH),
        alias=dproj, **dn_args)

    dbg = jnp.concatenate([jnp.transpose(dbb_b[:, :, 0]), jnp.transpose(dgb_b[:, :, 0]),
                           jnp.zeros((S, LANES - 2 * NH), F32)], axis=1)

    def gate_bwd(a, p, q, d):
        _, vjp = jax.vjp(gate_fn, a, p, q)
        return vjp(d)

    dproj, g_alp, g_dtb = _ew(gate_bwd, "dn_gate_bwd", (nb,), [proj, alp, dtb, dbg], [ba_spec, vecL, vecL, rowL],
                              [((S, NA + LANES), BF16, ba_spec), ((1, LANES), F32, vecL), ((1, LANES), F32, vecL)],
                              acc=(1, 2), alias=(dproj, 0))

    dhn1 = _matmul(dproj, w_aug, "nt", F32, "mm_in_dx")
    g_w_aug = _matmul(hn1, dproj, "tn", F32, "mm_in_dw")

    def norm1_bwd(a, g, sc, sh, dhn, dxr):
        _, vjp = jax.vjp(_f_normmod, a, g, sc, sh)
        da, dg, dsc, dsh = vjp(dhn)
        return da + dxr, dg, dsc, dsh

    grad_x, dg1, dsc1, dsh1 = _ew(norm1_bwd, "norm1_bwd", (nb,), [x2, norm1_g, sc1, sh1, dhn1, dx_r],
                                  [rowD, vecD, vecD, vecD, rowD, rowD],
                                  [((S, D), F32, rowD)] + [((1, D), F32, vecD)] * 3, acc=(1, 2, 3))

    g_w_in = jnp.concatenate([g_w_aug[:, :o_b], g_w_aug[:, NA:NA + 2 * NH], glu_unpairs(g_w_aug[:, a_glu:NA]),
                              g_w_aug[:, a_ga:a_glu]], axis=1)
    g_w_in = jnp.transpose(g_w_in.reshape(D, 4, NIN // 4), (1, 0, 2))
    parts = [g_w_in, g_w_do, g_w_co, g_w_up, g_w_out.reshape(4, D // 4, D), g_w_dn.reshape(4, FFN // 4, D)]
    slots = jnp.stack([chip, ci]).astype(jnp.int32)
    rbs = _pair_swap_half(parts)
    q16 = [_pair_add(p, rb, core, "rs_pair_add_" + nm) for p, rb, nm in zip(parts, rbs, names_big)]
    r3s = _chip_scatter(q16)
    halves = [_chip_sum(q, r3, slots, "rs_chip_sum_" + nm) for q, r3, nm in zip(q16, r3s, names_big)]
    g_big = [g.reshape(w.shape) for g, w in zip(_pair_gather(halves), big)]
    gs_w_in, gs_w_do, gs_w_co, gs_w_up, gs_w_out, gs_w_dn = g_big

    dmod = jnp.concatenate([dsh1, dsc1, dgt1, dsh2, dsc2, dgt2], axis=1)
    sm2 = [dmod, dg1, dg2, dgf, g_alp, g_dtb, g_dnng, g_cflg, g_cflb, g_dncw, g_cfcw, g_ffcw]
    sm2_shapes = [tuple(a.shape) for a in sm2]
    g3 = _all_gather_small(_pack_small(sm2))
    ssum = _sum_slots(g3, "small_sum")
    dmod_all = _unpack_small(g3, sm2_shapes[:1])[0].reshape(8, 6 * D)
    (g_b_ada, gs_n1, gs_n2, gs_fn, gs_alp, gs_dtb, gs_dnng, gs_cflg, gs_cflb, gs_dncw, gs_cfcw,
     gs_ffcw) = _unpack_small(ssum, sm2_shapes)
    gs_alog, gs_dtbias = gs_alp[:, NH:2 * NH], gs_dtb[:, NH:2 * NH]
    gs_dncw = lax.dynamic_slice(gs_dncw, (0, chip * (3 * DNW // 4)), (KDN, 3 * DNW // 4))
    gs_cfcw = lax.dynamic_slice(gs_cfcw, (0, chip * (CFW // 4)), (KCF, CFW // 4))
    gs_ffcw = lax.dynamic_slice(gs_ffcw, (0, chip * (FFN // 4)), (KFF, FFN // 4))

    dmod_sh = lax.dynamic_slice(dmod_all, (0, chip * CA), (8, CA))

    def wada_fn(cc, dm):
        return (_mm(_silu(cc), dm, TN),)

    g_w_ada = _ew(wada_fn, "ada_dw", (CA // tn_a,), [c_all, dmod_sh],
                  [_full((8, D)), pl.BlockSpec((8, tn_a), lambda j: (0, j))],
                  [((D, CA), F32, pl.BlockSpec((D, tn_a), lambda j: (0, j)))])[0]

    loss = lax.psum(loss_v[0, 0], ("x", "y", "c"))

    names = ["w_ada", "b_ada", "norm1_g", "w_in", "dn_conv_w", "dn_a_log", "dn_dt_bias", "dn_norm_g", "dn_w_o",
             "cf_conv_w", "cf_ln_g", "cf_ln_b", "cf_w_o", "w_out", "norm2_g", "ffn_w_up", "ffn_conv_w", "ffn_w_down",
             "final_norm_g"]
    wts = dict(zip(names, [w_ada, b_ada, norm1_g, w_in, dn_conv_w, dn_a_log, dn_dt_bias, dn_norm_g, dn_w_o, cf_conv_w,
                           cf_ln_g, cf_ln_b, cf_w_o, w_out, norm2_g, ffn_w_up, ffn_conv_w, ffn_w_down, final_norm_g]))
    ms = dict(zip(names, [m_w_ada, m_b_ada, m_norm1_g, m_w_in, m_dn_conv_w, m_dn_a_log, m_dn_dt_bias, m_dn_norm_g,
                          m_dn_w_o, m_cf_conv_w, m_cf_ln_g, m_cf_ln_b, m_cf_w_o, m_w_out, m_norm2_g, m_ffn_w_up,
                          m_ffn_conv_w, m_ffn_w_down, m_final_norm_g]))
    vs = dict(zip(names, [v_w_ada, v_b_ada, v_norm1_g, v_w_in, v_dn_conv_w, v_dn_a_log, v_dn_dt_bias, v_dn_norm_g,
                          v_dn_w_o, v_cf_conv_w, v_cf_ln_g, v_cf_ln_b, v_cf_w_o, v_w_out, v_norm2_g, v_ffn_w_up,
                          v_ffn_conv_w, v_ffn_w_down, v_final_norm_g]))
    grads = {"w_ada": g_w_ada, "b_ada": g_b_ada, "norm1_g": gs_n1, "w_in": gs_w_in, "dn_conv_w": gs_dncw,
             "dn_a_log": gs_alog, "dn_dt_bias": gs_dtbias, "dn_norm_g": gs_dnng, "dn_w_o": gs_w_do,
             "cf_conv_w": gs_cfcw, "cf_ln_g": gs_cflg, "cf_ln_b": gs_cflb, "cf_w_o": gs_w_co, "w_out": gs_w_out,
             "norm2_g": gs_n2, "ffn_w_up": gs_w_up, "ffn_conv_w": gs_ffcw, "ffn_w_down": gs_w_dn,
             "final_norm_g": gs_fn}
    grads = {n: grads[n].reshape(wts[n].shape) for n in names}

    large = ["w_ada", "w_in", "dn_w_o", "cf_w_o", "w_out", "ffn_w_up", "ffn_w_down"]
    small = [n for n in names if n not in large]
    delta, new_m, new_v = {}, {}, {}
    for n in large:
        d_, m_, v_ = _adamw(wts[n][0], grads[n][0], ms[n][0], vs[n][0], "adamw_" + n)
        delta[n], new_m[n], new_v[n] = d_[None], m_[None], v_[None]
    sm_sh = [tuple(wts[n].shape) for n in small]
    d_, m_, v_ = _adamw(_pack_small([wts[n] for n in small]), _pack_small([grads[n] for n in small]),
                        _pack_small([ms[n] for n in small]), _pack_small([vs[n] for n in small]), "adamw_small")
    for n, a, b_, c_ in zip(small, _unpack_small(d_, sm_sh), _unpack_small(m_, sm_sh), _unpack_small(v_, sm_sh)):
        delta[n], new_m[n], new_v[n] = a, b_, c_

    return (loss, grad_x[None], *[grads[n] for n in names], *[delta[n] for n in names],
            *[new_m[n] for n in names], *[new_v[n] for n in names])
```

```python
import functools

import jax
import jax.numpy as jnp
from jax import lax
from jax.experimental import pallas as pl
from jax.experimental.pallas import tpu as pltpu

F32 = jnp.float32
BF16 = jnp.bfloat16
EPS = 1e-6
LANES = 128
VMEM_LIMIT = 48 * 1024 * 1024
DN_CHUNK = 128
ADAM_LR, ADAM_B1, ADAM_B2, ADAM_EPS, ADAM_WD, ADAM_STEP = 0.001, 0.9, 0.999, 1e-08, 0.01, 10
MESH = pl.DeviceIdType.MESH

NN = (((1,), (0,)), ((), ()))
NT = (((1,), (1,)), ((), ()))
TN = (((0,), (0,)), ((), ()))
_DIMS = {"nn": NN, "nt": NT, "tn": TN}


def _mm(a, b, dims):
    return lax.dot_general(a.astype(BF16), b.astype(BF16), dims, preferred_element_type=F32)


def _mmx(a, b, dims):
    return lax.dot_general(a, b, dims, precision=lax.Precision.HIGHEST, preferred_element_type=F32)


def _div_tile(n, target, mult):
    best = None
    t = mult
    while t <= min(n, target):
        if n % t == 0:
            best = t
        t += mult
    return best if best is not None else n


def _params(sem=None):
    kw = dict(vmem_limit_bytes=VMEM_LIMIT)
    if sem is not None:
        kw["dimension_semantics"] = sem
    return pltpu.CompilerParams(**kw)


def _sigmoid(x):
    return jax.nn.sigmoid(x)


def _silu(x):
    return x * jax.nn.sigmoid(x)


def _softplus(x):
    return jnp.maximum(x, 0.0) + jnp.log(1.0 + jnp.exp(-jnp.abs(x)))


def _view(arr):
    if arr.ndim == 3:
        return arr.shape[1], arr.shape[0] * arr.shape[2], arr.shape[0]
    return arr.shape[0], arr.shape[1], 1


def _tile_spec(groups, cols, tr, tc, rsel, csel):
    if groups > 1:
        per = cols // groups // tc
        return pl.BlockSpec((None, tr, tc), lambda i, j, k: (csel(i, j, k) // per, rsel(i, j, k), csel(i, j, k) % per))
    return pl.BlockSpec((tr, tc), lambda i, j, k: (rsel(i, j, k), csel(i, j, k)))


def _matmul(a, b, mode, out_dtype, name, out_groups=1, after=()):
    ar, ac, ag = _view(a)
    br, bc, bg = _view(b)
    if mode == "nn":
        M, K, N = ar, ac, bc
        kdiv, mdiv, ndiv = ac // ag, M, min(bc // bg, N // out_groups)
    elif mode == "nt":
        M, K, N = ar, ac, br
        kdiv, mdiv, ndiv = min(ac // ag, bc // bg), M, N // out_groups
    else:
        K, M, N = ar, ac, bc
        kdiv, mdiv, ndiv = K, ac // ag, min(bc // bg, N // out_groups)
    tm = _div_tile(mdiv, 1024, LANES)
    tn = _div_tile(ndiv, 1536, LANES)
    tk = _div_tile(kdiv, 2048, LANES)
    nk = K // tk
    dims = _DIMS[mode]
    si, sj, sk = (lambda i, j, k: i), (lambda i, j, k: j), (lambda i, j, k: k)
    a_spec = {"nn": _tile_spec(ag, ac, tm, tk, si, sk), "nt": _tile_spec(ag, ac, tm, tk, si, sk),
              "tn": _tile_spec(ag, ac, tk, tm, sk, si)}[mode]
    b_spec = {"nn": _tile_spec(bg, bc, tk, tn, sk, sj), "nt": _tile_spec(bg, bc, tn, tk, sj, sk),
              "tn": _tile_spec(bg, bc, tk, tn, sk, sj)}[mode]
    out_shape = (M, N) if out_groups == 1 else (out_groups, M, N // out_groups)

    n_after = len(after)

    def body(*refs):
        a_ref, b_ref, o_ref = refs[0], refs[1], refs[2 + n_after]
        if nk == 1:
            o_ref[...] = lax.dot_general(a_ref[...], b_ref[...], dims, preferred_element_type=F32).astype(o_ref.dtype)
            return
        acc_ref = refs[3 + n_after]
        k = pl.program_id(2)

        @pl.when(k == 0)
        def _():
            acc_ref[...] = jnp.zeros_like(acc_ref)

        acc_ref[...] += lax.dot_general(a_ref[...], b_ref[...], dims, preferred_element_type=F32)

        @pl.when(k == nk - 1)
        def _():
            o_ref[...] = acc_ref[...].astype(o_ref.dtype)

    return pl.pallas_call(
        body, name=name, grid=(M // tm, N // tn, nk),
        in_specs=[a_spec, b_spec] + [pl.BlockSpec(memory_space=pl.ANY)] * n_after,
        out_specs=_tile_spec(out_groups, N, tm, tn, si, sj),
        out_shape=jax.ShapeDtypeStruct(out_shape, out_dtype),
        scratch_shapes=[pltpu.VMEM((tm, tn), F32)] if nk > 1 else [],
        compiler_params=_params(("parallel", "parallel", "arbitrary")),
    )(a, b, *after)


def _ew(fn, name, grid, ins, in_specs, outs, acc=(), alias=None, after=()):
    n_in = len(ins)
    n_ax = len(grid)
    extra, aliases = list(after), {}
    if alias is not None:
        extra, aliases = extra + [alias[0]], {n_in + len(after): alias[1]}
    extra_specs = [pl.BlockSpec(memory_space=pl.ANY)] * len(extra)

    def body(*refs):
        in_refs, out_refs = refs[:n_in], refs[n_in + len(extra):]
        ids = [pl.program_id(a) for a in range(n_ax)]
        res = fn(*[r[...] for r in in_refs])
        first = ids[0] == 0
        for t in ids[1:]:
            first = jnp.logical_and(first, t == 0)
        for idx, (r, val) in enumerate(zip(out_refs, res)):
            if idx in acc:
                @pl.when(first)
                def _(r=r, val=val):
                    r[...] = val.astype(r.dtype)

                @pl.when(jnp.logical_not(first))
                def _(r=r, val=val):
                    r[...] += val.astype(r.dtype)
            else:
                r[...] = val.astype(r.dtype)

    return pl.pallas_call(
        body, name=name, grid=grid, in_specs=list(in_specs) + extra_specs,
        out_specs=[o[2] for o in outs],
        out_shape=[jax.ShapeDtypeStruct(o[0], o[1]) for o in outs],
        input_output_aliases=aliases,
        compiler_params=_params(("arbitrary",) * n_ax),
    )(*ins, *extra)


def _ew_slot(fn, name, grid, slots, ins, in_specs, out_shape, out_dtype, out_spec):
    def body(s_ref, *refs):
        refs[-1][...] = fn(*[r[...] for r in refs[:-1]]).astype(refs[-1].dtype)

    return pl.pallas_call(
        body, name=name,
        grid_spec=pltpu.PrefetchScalarGridSpec(num_scalar_prefetch=1, grid=grid, in_specs=list(in_specs),
                                               out_specs=out_spec),
        out_shape=jax.ShapeDtypeStruct(out_shape, out_dtype),
        compiler_params=_params(("arbitrary",) * len(grid)),
    )(slots, *ins)


def _row(bs, w, col=0):
    return pl.BlockSpec((bs, w), lambda i, col=col: (i, col))


def _full(shape):
    nd = len(shape)
    return pl.BlockSpec(tuple(shape), lambda *_: (0,) * nd)


def _rms(x, g):
    return x * lax.rsqrt(jnp.mean(x * x, axis=-1, keepdims=True) + EPS) * g


def _f_normmod(x, g, sc, sh):
    return _rms(x, g) * (1.0 + sc) + sh


def _f_res_normmod(x, mix, gt, g, sc, sh):
    x1 = x + gt * mix
    return x1, _f_normmod(x1, g, sc, sh)


def _f_loss(x1, f, gt, gf, tgt):
    y = _rms(x1 + gt * f, gf)
    return 0.5 * jnp.sum(jnp.mean(jnp.square(y - tgt), axis=-1))


def _f_dn_gate(nh, ba, alp, dtb):
    lane = lax.broadcasted_iota(jnp.int32, ba.shape, 1)
    m = (lane < nh).astype(F32)
    beta = _sigmoid(ba)
    g = -jnp.exp(alp) * _softplus(ba + dtb)
    return m * beta + (1.0 - m) * g


def _f_dn_post(o, z, g):
    return o * lax.rsqrt(jnp.mean(o * o, axis=-1, keepdims=True) + EPS) * g * _silu(z)


def _f_cf_ln(u, g, b):
    mu = jnp.mean(u, axis=-1, keepdims=True)
    xc = u - mu
    y = xc * lax.rsqrt(jnp.mean(xc * xc, axis=-1, keepdims=True) + EPS)
    return _silu(y * g + b)


def _f_merge(a, b, ga, gb):
    return _sigmoid(ga) * a + _sigmoid(gb) * b


def _shift_down(u, d, rows):
    if d == 0:
        return u
    return jnp.where(rows >= d, pltpu.roll(u, d, 0), 0.0)


def _shift_up(u, d, rows):
    if d == 0:
        return u
    s = u.shape[0]
    return jnp.where(rows < s - d, pltpu.roll(u, s - d, 0), 0.0)


def _conv(u, w_ref, kw, rows):
    acc = None
    for k in range(kw):
        t = w_ref[k:k + 1, :] * _shift_down(u, kw - 1 - k, rows)
        acc = t if acc is None else acc + t
    return acc


def _conv_t(dc, w_ref, kw, rows):
    acc = None
    for k in range(kw):
        t = w_ref[k:k + 1, :] * _shift_up(dc, kw - 1 - k, rows)
        acc = t if acc is None else acc + t
    return acc


def _conv_fwd_call(name, kw, ncol, ins, in_specs, extras, extra_specs, w, w_spec, pre, post, out_shape, out_spec):
    n_in, n_ex = len(ins), len(extras)

    def body(*refs):
        in_refs, ex_refs = refs[:n_in], refs[n_in:n_in + n_ex]
        w_ref, out_ref = refs[n_in + n_ex], refs[n_in + n_ex + 1]
        j = pl.program_id(0)
        u = pre(*[r[...] for r in in_refs])
        rows = lax.broadcasted_iota(jnp.int32, u.shape, 0)
        cv = _conv(u, w_ref, kw, rows)
        out_ref[...] = post(j, cv, *[r[...] for r in ex_refs]).astype(out_ref.dtype)

    return pl.pallas_call(
        body, name=name, grid=(ncol,), in_specs=list(in_specs) + list(extra_specs) + [w_spec],
        out_specs=out_spec, out_shape=out_shape, compiler_params=_params(("arbitrary",)),
    )(*ins, *extras, w)


def _conv_bwd_call(name, kw, ncol, ins, in_specs, extras, extra_specs, w, w_spec, pre, post, dout, dout_spec,
                   dio_shapes, dio_specs, dio_pack, dw_shape, dw_spec, alias=None, after=()):
    n_in, n_ex, n_io = len(ins), len(extras), len(dio_shapes)
    al, aliases = list(after), {}
    if alias is not None:
        al, aliases = al + [alias], {n_in + n_ex + 2 + len(after): 0}
    al_specs = [pl.BlockSpec(memory_space=pl.ANY)] * len(al)

    def body(*refs):
        in_refs, ex_refs = refs[:n_in], refs[n_in:n_in + n_ex]
        w_ref, dout_ref = refs[n_in + n_ex], refs[n_in + n_ex + 1]
        outs = refs[n_in + n_ex + 2 + len(al):]
        dio_refs, dw_ref = outs[:n_io], outs[n_io]
        j = pl.program_id(0)
        u, pre_vjp = jax.vjp(pre, *[r[...] for r in in_refs])
        rows = lax.broadcasted_iota(jnp.int32, u.shape, 0)
        cv = _conv(u, w_ref, kw, rows)
        _, post_vjp = jax.vjp(lambda cc, *ex: post(j, cc, *ex), cv, *[r[...] for r in ex_refs])
        g = post_vjp(dout_ref[...].astype(F32))
        dc = g[0]
        for k in range(kw):
            dw_ref[k:k + 1, :] = jnp.sum(dc * _shift_down(u, kw - 1 - k, rows), axis=0, keepdims=True)
        du = _conv_t(dc, w_ref, kw, rows)
        for r, val in zip(dio_refs, dio_pack(*pre_vjp(du), *g[1:])):
            if isinstance(val, tuple):
                for t, part in enumerate(val):
                    r[t] = part.astype(r.dtype)
            else:
                r[...] = val.astype(r.dtype)

    return pl.pallas_call(
        body, name=name, grid=(ncol,),
        in_specs=list(in_specs) + list(extra_specs) + [w_spec, dout_spec] + al_specs,
        out_specs=list(dio_specs) + [dw_spec],
        out_shape=list(dio_shapes) + [dw_shape],
        input_output_aliases=aliases,
        compiler_params=_params(("arbitrary",)),
    )(*ins, *extras, w, dout, *al)


def _tri_inverse(a):
    c = a.shape[0]
    ii = lax.broadcasted_iota(jnp.int32, (c, c), 0)
    jj = lax.broadcasted_iota(jnp.int32, (c, c), 1)
    y = -a
    t = (ii == jj).astype(F32) + y
    p = 2
    while p < c:
        y = _mmx(y, y, NN)
        t = t + _mmx(t, y, NN)
        p *= 2
    return t


def _dn_common(q, k, v, gb, bb):
    c = q.shape[0]
    ii = lax.broadcasted_iota(jnp.int32, (c, c), 0)
    jj = lax.broadcasted_iota(jnp.int32, (c, c), 1)
    causal = jj <= ii
    strict = jj < ii
    low = causal.astype(F32)
    ones = jnp.ones((c, LANES), F32)
    gc = _mmx(low, gb, NN)
    diff = (_mmx(gc, ones, NT) - _mmx(ones, gc, NT)) * (1.0 / LANES)
    decay = jnp.where(causal, jnp.exp(jnp.where(causal, diff, 0.0)), 0.0)
    gl = jnp.sum(gb, axis=0, keepdims=True)
    eg = jnp.exp(gc)
    egm = jnp.exp(gl - gc)
    egl = jnp.exp(gl)
    kb = k * bb
    vb = v * bb
    kbg = kb * eg
    kk = _mm(kb, k, NT)
    t = _tri_inverse(jnp.where(strict, kk * decay, 0.0))
    qk = _mm(q, k, NT)
    attn = qk * decay
    return dict(causal=causal, strict=strict, low=low, ones=ones, decay=decay, eg=eg, egm=egm, egl=egl,
                kb=kb, vb=vb, kbg=kbg, kk=kk, t=t, qk=qk, attn=attn, qd=q * eg, kd=k * egm)


def _dn_fwd_chunk(q, k, v, gb, bb, s):
    m = _dn_common(q, k, v, gb, bb)
    u = _mmx(m["t"], m["vb"] - _mm(m["kbg"], s, NN), NN)
    o = _mm(m["qd"], s, NN) + _mm(m["attn"], u, NN)
    s2 = s * m["egl"] + _mm(m["kd"], u, TN)
    return o, s2


def _dn_bwd_chunk(q, k, v, gb, bb, s, do, dsp):
    m = _dn_common(q, k, v, gb, bb)
    c = q.shape[0]
    t, decay, eg, egm, egl = m["t"], m["decay"], m["eg"], m["egm"], m["egl"]
    u = _mmx(t, m["vb"] - _mm(m["kbg"], s, NN), NN)
    du = _mm(m["attn"], do, TN) + _mm(m["kd"], dsp, NN)
    dattn = jnp.where(m["causal"], _mm(do, u, NT), 0.0)
    dqd = _mm(do, s, NT)
    dkd = _mm(u, dsp, NT)
    dr = _mmx(t, du, TN)
    da = jnp.where(m["strict"], -_mm(dr, u, NT), 0.0)
    dkbg = -_mm(dr, s, NT)
    ds = dsp * egl + _mm(m["qd"], do, TN) - _mm(m["kbg"], dr, TN)
    degl = jnp.sum(jnp.sum(dsp * s, axis=1, keepdims=True), axis=0, keepdims=True)
    dkk = da * decay
    dqk = dattn * decay
    ddiff = (da * m["kk"] + dattn * m["qk"]) * decay
    dgc = _mmx(ddiff, m["ones"], NN) - _mmx(ddiff, m["ones"], TN)
    dkb = _mm(dkk, k, NN) + dkbg * eg
    dk = _mm(dkk, m["kb"], TN) + _mm(dqk, q, TN) + dkd * egm + dkb * bb
    dq = _mm(dqk, k, NN) + dqd * eg
    dgc = dgc + jnp.sum(dqd * q + dkbg * m["kb"], axis=-1, keepdims=True) * eg
    tt = jnp.sum(dkd * k, axis=-1, keepdims=True) * egm
    dgc = dgc - tt
    dgl = jnp.sum(tt, axis=0, keepdims=True) + degl * egl
    dbb = jnp.sum(dkb * k + dr * v, axis=-1, keepdims=True) + jnp.zeros((c, LANES), F32)
    dv = dr * bb
    dgb = _mmx(m["low"], dgc, TN) + dgl
    return dq, dk, dv, dgb, dbb, ds


def _dn_fwd_call(qkvn, gb, bb):
    _, nh, s, dh = qkvn.shape
    c = min(DN_CHUNK, s)
    n = s // c
    hb = nh

    def body(q_ref, k_ref, v_ref, g_ref, b_ref, o_ref, st_ref, s_ref):
        @pl.when(pl.program_id(1) == 0)
        def _():
            s_ref[...] = jnp.zeros_like(s_ref)

        for h in range(hb):
            st = s_ref[h]
            st_ref[h] = st
            o, s2 = _dn_fwd_chunk(q_ref[h], k_ref[h], v_ref[h], g_ref[h], b_ref[h], st)
            o_ref[h] = o
            s_ref[h] = s2

    def qspec(t):
        return pl.BlockSpec((None, hb, c, dh), lambda i, j, t=t: (t, i, j, 0))

    hs = pl.BlockSpec((hb, c, dh), lambda i, j: (i, j, 0))
    return pl.pallas_call(
        body, name="dn_fwd", grid=(nh // hb, n),
        in_specs=[qspec(0), qspec(1), qspec(2), hs, hs],
        out_specs=[hs, pl.BlockSpec((hb, None, dh, dh), lambda i, j: (i, j, 0, 0))],
        out_shape=[jax.ShapeDtypeStruct((nh, s, dh), F32), jax.ShapeDtypeStruct((nh, n, dh, dh), F32)],
        scratch_shapes=[pltpu.VMEM((hb, dh, dh), F32)],
        compiler_params=_params(("arbitrary", "arbitrary")),
    )(qkvn, qkvn, qkvn, gb, bb)


def _dn_bwd_call(qkvn, gb, bb, states, do):
    _, nh, s, dh = qkvn.shape
    c = min(DN_CHUNK, s)
    n = s // c
    hb = nh

    def body(q_ref, k_ref, v_ref, g_ref, b_ref, st_ref, do_ref, dqkv_ref, dg_ref, db_ref, ds_ref):
        @pl.when(pl.program_id(1) == 0)
        def _():
            ds_ref[...] = jnp.zeros_like(ds_ref)

        for h in range(hb):
            dq, dk, dv, dg, db, ds = _dn_bwd_chunk(q_ref[h], k_ref[h], v_ref[h], g_ref[h], b_ref[h], st_ref[h],
                                                   do_ref[h], ds_ref[h])
            dqkv_ref[0, h] = dq
            dqkv_ref[1, h] = dk
            dqkv_ref[2, h] = dv
            dg_ref[h] = dg
            db_ref[h] = db
            ds_ref[h] = ds

    def qspec(t):
        return pl.BlockSpec((None, hb, c, dh), lambda i, j, t=t: (t, i, n - 1 - j, 0))

    hs = pl.BlockSpec((hb, c, dh), lambda i, j: (i, n - 1 - j, 0))
    sh = jax.ShapeDtypeStruct((nh, s, dh), F32)
    return pl.pallas_call(
        body, name="dn_bwd", grid=(nh // hb, n),
        in_specs=[qspec(0), qspec(1), qspec(2), hs, hs,
                  pl.BlockSpec((hb, None, dh, dh), lambda i, j: (i, n - 1 - j, 0, 0)), hs],
        out_specs=[pl.BlockSpec((3, hb, c, dh), lambda i, j: (0, i, n - 1 - j, 0)), hs, hs],
        out_shape=[jax.ShapeDtypeStruct((3, nh, s, dh), F32), sh, sh],
        scratch_shapes=[pltpu.VMEM((hb, dh, dh), F32)],
        compiler_params=_params(("arbitrary", "arbitrary")),
    )(qkvn, qkvn, qkvn, gb, bb, states, do)


def _adamw(w, g, m, v, name):
    r, c = w.shape
    br = _div_tile(r, max(8, (1 << 18) // max(c, 1)), 8)

    def fn(w, g, m, v):
        m = ADAM_B1 * m + (1.0 - ADAM_B1) * g
        v = ADAM_B2 * v + (1.0 - ADAM_B2) * jnp.square(g)
        m_hat = m / (1.0 - ADAM_B1 ** ADAM_STEP)
        v_hat = v / (1.0 - ADAM_B2 ** ADAM_STEP)
        delta = -ADAM_LR * (m_hat / (jnp.sqrt(v_hat) + ADAM_EPS) + ADAM_WD * w)
        return delta, m, v

    spec = pl.BlockSpec((br, c), lambda i: (i, 0))
    return _ew(fn, name, (r // br,), [w, g, m, v], [spec] * 4, [((r, c), F32, spec)] * 3)


def _coords():
    return lax.axis_index("x"), lax.axis_index("y"), lax.axis_index("c")


def _all_gather_small(v):
    r, w = v.shape

    def body(v_ref, out_ref, send_sems, recv_sems, local_sem):
        x, y, c = _coords()
        me = 4 * x + 2 * y + c
        mine = pltpu.make_async_copy(v_ref, out_ref.at[me], local_sem)
        mine.start()
        peers = []
        for k in range(1, 8):
            px = 1 - x if k & 4 else x
            py = 1 - y if k & 2 else y
            pc = 1 - c if k & 1 else c
            peers.append((px, py, pc))
        sends = []
        for k, peer in enumerate(peers):
            cp = pltpu.make_async_remote_copy(src_ref=v_ref, dst_ref=out_ref.at[me], send_sem=send_sems.at[k],
                                              recv_sem=recv_sems.at[k], device_id=peer, device_id_type=MESH)
            cp.start()
            sends.append(cp)
        for k, (px, py, pc) in enumerate(peers):
            pltpu.make_async_remote_copy(src_ref=v_ref, dst_ref=out_ref.at[4 * px + 2 * py + pc],
                                         send_sem=send_sems.at[k], recv_sem=recv_sems.at[k],
                                         device_id=(px, py, pc), device_id_type=MESH).wait_recv()
        for cp in sends:
            cp.wait_send()
        mine.wait()

    return pl.pallas_call(
        body, name="ag_small", out_shape=jax.ShapeDtypeStruct((8, r, w), v.dtype),
        in_specs=[pl.BlockSpec(memory_space=pltpu.VMEM)], out_specs=pl.BlockSpec(memory_space=pltpu.VMEM),
        scratch_shapes=[pltpu.SemaphoreType.DMA((7,)), pltpu.SemaphoreType.DMA((7,)), pltpu.SemaphoreType.DMA],
        compiler_params=pltpu.CompilerParams(vmem_limit_bytes=VMEM_LIMIT),
    )(v)


def _hbm_call(body, name, arrays, out_shapes, n_sems, aliases=None):
    hbm = pl.BlockSpec(memory_space=pltpu.HBM)
    return pl.pallas_call(
        body, name=name, out_shape=list(out_shapes), in_specs=[hbm] * len(arrays), out_specs=[hbm] * len(out_shapes),
        input_output_aliases=aliases or {},
        scratch_shapes=[pltpu.SemaphoreType.DMA((n_sems,)), pltpu.SemaphoreType.DMA((n_sems,))],
    )(*arrays)


def _half_rows(ref_shape, c):
    h = ref_shape[1] // 2
    return pl.ds(pl.multiple_of(c * h, 16), h), pl.ds(pl.multiple_of((1 - c) * h, 16), h)


def _ag_chips(bufs):
    n = len(bufs)

    def body(*refs):
        ins, outs, send_sems, recv_sems = refs[:n], refs[n:2 * n], refs[2 * n], refs[2 * n + 1]
        x, y, c = _coords()
        j = 2 * x + y
        chips = [(1 - x, y), (x, 1 - y), (1 - x, 1 - y)]

        def cp(w, k, slot, rows, to, src=None):
            dst = outs[w].at[slot, rows]
            return pltpu.make_async_remote_copy(src_ref=dst if src is None else src, dst_ref=dst,
                                                send_sem=send_sems.at[6 * w + k], recv_sem=recv_sems.at[6 * w + k],
                                                device_id=to, device_id_type=MESH)

        started = []
        for w in range(n):
            mine, _ = _half_rows(outs[w].shape, c)
            for k, (px, py) in enumerate(chips):
                f = cp(w, k, j, mine, (px, py, c), src=ins[w].at[j, mine])
                f.start()
                started.append(f)
        for w in range(n):
            mine, _ = _half_rows(outs[w].shape, c)
            for k, (px, py) in enumerate(chips):
                cp(w, k, 2 * px + py, mine, (x, y, c)).wait_recv()
                f = cp(w, 3 + k, 2 * px + py, mine, (x, y, 1 - c))
                f.start()
                started.append(f)
        for w in range(n):
            _, sib = _half_rows(outs[w].shape, c)
            for k, (px, py) in enumerate(chips):
                cp(w, 3 + k, 2 * px + py, sib, (x, y, c)).wait_recv()
        for f in started:
            f.wait_send()

    shapes = [jax.ShapeDtypeStruct(b.shape, b.dtype) for b in bufs]
    return _hbm_call(body, "ag_weights", bufs, shapes, 6 * n, aliases={i: i for i in range(n)})


def _pair_swap_half(ps, name):
    n = len(ps)

    def body(*refs):
        ins, outs, send_sems, recv_sems = refs[:n], refs[n:2 * n], refs[2 * n], refs[2 * n + 1]
        x, y, c = _coords()
        cps = []
        for w in range(n):
            _, other = _half_rows(ins[w].shape, c)
            cp = pltpu.make_async_remote_copy(src_ref=ins[w].at[:, other], dst_ref=outs[w], send_sem=send_sems.at[w],
                                              recv_sem=recv_sems.at[w], device_id=(x, y, 1 - c), device_id_type=MESH)
            cp.start()
            cps.append(cp)
        for cp in cps:
            cp.wait()

    shapes = [jax.ShapeDtypeStruct((p.shape[0], p.shape[1] // 2, p.shape[2]), p.dtype) for p in ps]
    return _hbm_call(body, name, ps, shapes, n)


def _pair_gather(gs):
    n = len(gs)

    def body(*refs):
        ins, outs, send_sems, recv_sems = refs[:n], refs[n:2 * n], refs[2 * n], refs[2 * n + 1]
        x, y, c = _coords()
        cps = []
        for w in range(n):
            cp = pltpu.make_async_remote_copy(src_ref=ins[w].at[c], dst_ref=outs[w].at[c], send_sem=send_sems.at[w],
                                              recv_sem=recv_sems.at[w], device_id=(x, y, 1 - c), device_id_type=MESH)
            cp.start()
            cps.append(cp)
        for w, cp in enumerate(cps):
            pltpu.make_async_remote_copy(src_ref=ins[w].at[c], dst_ref=outs[w].at[1 - c], send_sem=send_sems.at[w],
                                         recv_sem=recv_sems.at[w], device_id=(x, y, 1 - c),
                                         device_id_type=MESH).wait_recv()
            cp.wait_send()

    shapes = [jax.ShapeDtypeStruct(g.shape, g.dtype) for g in gs]
    return _hbm_call(body, "rs_pair_gather", gs, shapes, n, aliases={i: i for i in range(n)})


_HBM = pl.BlockSpec(memory_space=pltpu.HBM)
_SEM = pl.BlockSpec(memory_space=pltpu.SEMAPHORE)
_EFFECT = pltpu.SideEffectType.DATAFLOW_SIDE_EFFECTING


def _split_start(name, srcs, lands, after, n_copies, make_copies):
    n, m = len(srcs), len(lands)
    arrays = [pltpu.with_memory_space_constraint(a, pltpu.HBM) for a in list(srcs) + list(lands)]

    def body(*refs):
        src_refs, land_refs = refs[:n], refs[n:n + m]
        send_sems, recv_sems = refs[n + m + 1], refs[n + m + 2]
        for cp in make_copies(src_refs, land_refs, send_sems, recv_sems):
            cp.start()
        refs[-1][...] = jnp.zeros_like(refs[-1])

    outs = pl.pallas_call(
        body, name=name,
        out_shape=(pltpu.SemaphoreType.DMA((n_copies,)), pltpu.SemaphoreType.DMA((n_copies,)),
                   *[pltpu.HBM(a.shape, a.dtype) for a in arrays], jax.ShapeDtypeStruct((8, LANES), F32)),
        in_specs=[_HBM] * (n + m) + [pl.BlockSpec(memory_space=pl.ANY)],
        out_specs=(_SEM, _SEM, *[_HBM] * (n + m), pl.BlockSpec(memory_space=pltpu.VMEM)),
        input_output_aliases={i: 2 + i for i in range(n + m)},
        compiler_params=pltpu.CompilerParams(has_side_effects=_EFFECT),
    )(*arrays, after)
    return (outs[0], outs[1], list(outs[2:2 + n]), list(outs[2 + n:2 + n + m])), outs[-1]


def _split_wait(name, state, after, make_copies):
    send_sems, recv_sems, srcs, lands = state
    n, m = len(srcs), len(lands)

    def body(*refs):
        src_refs, land_refs = refs[:n], refs[n:n + m]
        for cp in make_copies(src_refs, land_refs, refs[n + m], refs[n + m + 1]):
            cp.wait_send()
            cp.wait_recv()

    outs = pl.pallas_call(
        body, name=name, out_shape=tuple(pltpu.HBM(a.shape, a.dtype) for a in srcs + lands),
        in_specs=[_HBM] * (n + m) + [_SEM, _SEM, pl.BlockSpec(memory_space=pl.ANY)], out_specs=tuple([_HBM] * (n + m)),
        input_output_aliases={i: i for i in range(n + m)},
        compiler_params=pltpu.CompilerParams(has_side_effects=_EFFECT),
    )(*srcs, *lands, send_sems, recv_sems, after)
    return list(outs[:n]), list(outs[n:])


def _scatter_copies(q_refs, land_refs, send_sems, recv_sems):
    x, y, c = _coords()
    cps = []
    for w, (q, land) in enumerate(zip(q_refs, land_refs)):
        for k, (px, py) in enumerate([(1 - x, y), (x, 1 - y), (1 - x, 1 - y)]):
            cps.append(pltpu.make_async_remote_copy(src_ref=q.at[2 * px + py], dst_ref=land.at[k],
                                                    send_sem=send_sems.at[3 * w + k], recv_sem=recv_sems.at[3 * w + k],
                                                    device_id=(px, py, c), device_id_type=MESH))
    return cps


def _chip_scatter_start(qs, name):
    lands = [lax.empty((3,) + q.shape[1:], q.dtype) for q in qs]
    return _split_start(name, qs, lands, qs[0], 3 * len(qs), _scatter_copies)


def _chip_scatter_wait(state, after, name):
    return _split_wait(name, state, after, _scatter_copies)


def _gather_copies(src_refs, buf_refs, send_sems, recv_sems):
    x, y, c = _coords()
    j = 2 * x + y
    cps = []
    for w, buf in enumerate(buf_refs):
        mine, _ = _half_rows(buf.shape, c)
        for k, (px, py) in enumerate([(1 - x, y), (x, 1 - y), (1 - x, 1 - y)]):
            cps.append(pltpu.make_async_remote_copy(src_ref=buf.at[j, mine], dst_ref=buf.at[j, mine],
                                                    send_sem=send_sems.at[3 * w + k], recv_sem=recv_sems.at[3 * w + k],
                                                    device_id=(px, py, c), device_id_type=MESH))
    return cps


def _gather_wait_copies(src_refs, buf_refs, send_sems, recv_sems):
    x, y, c = _coords()
    j = 2 * x + y
    cps = []
    for w, buf in enumerate(buf_refs):
        mine, _ = _half_rows(buf.shape, c)
        for k, (px, py) in enumerate([(1 - x, y), (x, 1 - y), (1 - x, 1 - y)]):
            cps.append(pltpu.make_async_remote_copy(src_ref=buf.at[j, mine], dst_ref=buf.at[2 * px + py, mine],
                                                    send_sem=send_sems.at[3 * w + k], recv_sem=recv_sems.at[3 * w + k],
                                                    device_id=(px, py, c), device_id_type=MESH))
    return cps


def _pair_forward(bufs):
    n = len(bufs)

    def body(*refs):
        ins, outs, send_sems, recv_sems = refs[:n], refs[n:2 * n], refs[2 * n], refs[2 * n + 1]
        x, y, c = _coords()
        chips = [(1 - x, y), (x, 1 - y), (1 - x, 1 - y)]
        cps = []
        for w in range(n):
            mine, _ = _half_rows(outs[w].shape, c)
            for k, (px, py) in enumerate(chips):
                cp = pltpu.make_async_remote_copy(src_ref=ins[w].at[2 * px + py, mine],
                                                  dst_ref=outs[w].at[2 * px + py, mine],
                                                  send_sem=send_sems.at[3 * w + k], recv_sem=recv_sems.at[3 * w + k],
                                                  device_id=(x, y, 1 - c), device_id_type=MESH)
                cp.start()
                cps.append(cp)
        for w in range(n):
            _, sib = _half_rows(outs[w].shape, c)
            for k, (px, py) in enumerate(chips):
                pltpu.make_async_remote_copy(src_ref=ins[w].at[2 * px + py, sib], dst_ref=outs[w].at[2 * px + py, sib],
                                             send_sem=send_sems.at[3 * w + k], recv_sem=recv_sems.at[3 * w + k],
                                             device_id=(x, y, 1 - c), device_id_type=MESH).wait_recv()
        for cp in cps:
            cp.wait_send()

    shapes = [jax.ShapeDtypeStruct(b.shape, b.dtype) for b in bufs]
    return _hbm_call(body, "ag_pair_forward", bufs, shapes, 3 * n, aliases={i: i for i in range(n)})


def _slot_rows(h, cs):
    return _div_tile(h, max(16, (1 << 19) // cs), 16)


def _cast_into_slot(w, slot, name):
    r, cs = w.shape
    br = _slot_rows(r, cs)
    return _ew_slot(lambda a: a, name, (r // br,), slot, [w], [pl.BlockSpec((br, cs), lambda i, s: (i, 0))],
                    (4, r, cs), BF16, pl.BlockSpec((None, br, cs), lambda i, s: (s[0], i, 0)))


def _pair_add(p, rb, core, name):
    n, r, cs = p.shape
    h = r // 2
    br = _slot_rows(h, cs)
    nb = h // br
    return _ew_slot(lambda a, b: a + b, name, (n, nb), core, [p, rb],
                    [pl.BlockSpec((None, br, cs), lambda s, i, c: (s, c[0] * nb + i, 0)),
                     pl.BlockSpec((None, br, cs), lambda s, i, c: (s, i, 0))],
                    (n, h, cs), BF16, pl.BlockSpec((None, br, cs), lambda s, i, c: (s, i, 0)))


def _chip_sum(q, r3, slots, name):
    _, h, cs = q.shape
    br = _slot_rows(h, cs)

    def fn(a, b):
        acc = a.astype(F32)
        for k in range(3):
            acc = acc + b[k].astype(F32)
        return acc

    return _ew_slot(fn, name, (h // br,), slots, [q, r3],
                    [pl.BlockSpec((None, br, cs), lambda i, s: (s[0], i, 0)),
                     pl.BlockSpec((3, br, cs), lambda i, s: (0, i, 0))],
                    (2, h, cs), F32, pl.BlockSpec((None, br, cs), lambda i, s: (s[1], i, 0)))


def _sum_slots(r, name):
    n, h, w = r.shape
    br = _div_tile(h, 2048, 16)

    def fn(blk):
        acc = blk[0].astype(F32)
        for s in range(1, n):
            acc = acc + blk[s].astype(F32)
        return (acc,)

    return _ew(fn, name, (h // br,), [r], [pl.BlockSpec((n, br, w), lambda i: (0, i, 0))],
               [((h, w), F32, pl.BlockSpec((br, w), lambda i: (i, 0)))])[0]


def _pack_small(arrs):
    flat = jnp.concatenate([a.reshape(-1).astype(F32) for a in arrs])
    n = flat.shape[0]
    rows = -(-n // LANES)
    rows = -(-rows // 8) * 8
    return jnp.pad(flat, (0, rows * LANES - n)).reshape(rows, LANES)


def _unpack_small(p, shapes):
    lead = p.shape[:-2]
    flat = p.reshape(lead + (-1,))
    out, off = [], 0
    for sh in shapes:
        n = 1
        for d in sh:
            n *= d
        out.append(flat[..., off:off + n].reshape(lead + tuple(sh)))
        off += n
    return out


def kernel(x, c, w_ada, b_ada, norm1_g, w_in, dn_conv_w, dn_a_log, dn_dt_bias, dn_norm_g, dn_w_o, cf_conv_w, cf_ln_g, cf_ln_b, cf_w_o, w_out, norm2_g, ffn_w_up, ffn_conv_w, ffn_w_down, final_norm_g, loss_target, m_w_ada, m_b_ada, m_norm1_g, m_w_in, m_dn_conv_w, m_dn_a_log, m_dn_dt_bias, m_dn_norm_g, m_dn_w_o, m_cf_conv_w, m_cf_ln_g, m_cf_ln_b, m_cf_w_o, m_w_out, m_norm2_g, m_ffn_w_up, m_ffn_conv_w, m_ffn_w_down, m_final_norm_g, v_w_ada, v_b_ada, v_norm1_g, v_w_in, v_dn_conv_w, v_dn_a_log, v_dn_dt_bias, v_dn_norm_g, v_dn_w_o, v_cf_conv_w, v_cf_ln_g, v_cf_ln_b, v_cf_w_o, v_w_out, v_norm2_g, v_ffn_w_up, v_ffn_conv_w, v_ffn_w_down, v_final_norm_g):
    xi, yi, ci = _coords()
    chip = 2 * xi + yi
    me = 4 * xi + 2 * yi + ci
    core = jnp.reshape(ci, (1,)).astype(jnp.int32)

    S, D = x.shape[1], x.shape[2]
    NH = dn_a_log.shape[1]
    DH = dn_norm_g.shape[1]
    DNW = NH * DH
    CFW = cf_ln_g.shape[1]
    FFN = ffn_w_down.shape[1] * 4
    KDN, KCF, KFF = dn_conv_w.shape[1], cf_conv_w.shape[1], ffn_conv_w.shape[1]
    NIN = w_in.shape[2] * 4
    assert NIN == 4 * DNW + 2 * NH + 2 * CFW + 2 * D and DH == LANES and 2 * NH <= LANES
    x2, tgt = x[0], loss_target[0]

    sm_shapes = [(D,), (KDN, 3 * DNW // 4), (KCF, CFW // 4), (KFF, FFN // 4)]
    g1 = _all_gather_small(_pack_small([c[0], dn_conv_w[0], cf_conv_w[0], ffn_conv_w[0]]))
    c_all, dcw_s, ccw_s, fcw_s = _unpack_small(g1, sm_shapes)

    def chips_cols(t):
        t = t[0::2]
        return jnp.transpose(t, (1, 0, 2)).reshape(t.shape[1], -1)

    dn_cw, cf_cw, ff_cw = chips_cols(dcw_s), chips_cols(ccw_s), chips_cols(fcw_s)

    slot_chip = jnp.reshape(chip, (1,)).astype(jnp.int32)
    big = [w_in[0], dn_w_o[0], cf_w_o[0], w_out[0], ffn_w_up[0], ffn_w_down[0]]
    names_big = ["w_in", "dn_w_o", "cf_w_o", "w_out", "ffn_w_up", "ffn_w_down"]
    bufs = [_cast_into_slot(w, slot_chip, "cast_" + nm) for w, nm in zip(big, names_big)]
    (w_in_g,) = _ag_chips(bufs[:1])
    ag_state, ag_token = _split_start("ag_rest_start", [], bufs[1:], w_in_g, 3 * len(bufs[1:]), _gather_copies)
    w_in_f = jnp.transpose(w_in_g, (1, 0, 2)).reshape(D, NIN)

    cb_cf = _div_tile(CFW, 256, LANES)
    o_b = 4 * DNW
    o_glu = o_b + 2 * NH
    o_ga = o_glu + 2 * CFW
    NA = NIN - 2 * NH
    a_z, a_ga, a_gb, a_glu, a_ba = 3 * DNW, 4 * DNW, 4 * DNW + D, 4 * DNW + 2 * D, NA

    def glu_pairs(t):
        return jnp.transpose(t.reshape(D, 2, CFW // cb_cf, cb_cf), (0, 2, 1, 3)).reshape(D, 2 * CFW)

    def glu_unpairs(t):
        return jnp.transpose(t.reshape(D, CFW // cb_cf, 2, cb_cf), (0, 2, 1, 3)).reshape(D, 2 * CFW)

    w_aug = jnp.concatenate([w_in_f[:, :o_b], w_in_f[:, o_ga:], glu_pairs(w_in_f[:, o_glu:o_ga]), w_in_f[:, o_b:o_glu],
                             jnp.zeros((D, LANES - 2 * NH), BF16)], axis=1)

    CA = w_ada.shape[2]
    b_sh = lax.dynamic_slice(b_ada, (0, chip * CA), (1, CA))
    tn_a = _div_tile(CA, 512, LANES)

    def mod_fn(cc, w, b):
        return (_mm(_silu(cc), w, NN) + b,)

    mod_sh = _ew(mod_fn, "ada_mod", (CA // tn_a,), [c_all, w_ada[0], b_sh],
                 [_full((8, D)), pl.BlockSpec((D, tn_a), lambda j: (0, j)), pl.BlockSpec((1, tn_a), lambda j: (0, j))],
                 [((8, CA), F32, pl.BlockSpec((8, tn_a), lambda j: (0, j)))])[0]
    g2 = _all_gather_small(_pack_small([mod_sh]))
    mod_all = _unpack_small(g2, [(8, CA)])[0][0::2]
    mod_all = jnp.transpose(mod_all, (1, 0, 2)).reshape(8, 4 * CA)
    mod_me = lax.dynamic_slice(mod_all, (me, 0), (1, 6 * D))
    sh1, sc1, gt1, sh2, sc2, gt2 = [mod_me[:, i * D:(i + 1) * D] for i in range(6)]

    bs = _div_tile(S, 128, 8)
    nb = S // bs
    vecD = _full((1, D))
    rowD = _row(bs, D)

    hn1 = _ew(lambda a, g, sc, sh: (_f_normmod(a, g, sc, sh),), "norm1_fwd", (nb,),
              [x2, norm1_g, sc1, sh1], [rowD, vecD, vecD, vecD], [((S, D), BF16, rowD)], after=[ag_token])[0]
    proj = _matmul(hn1, w_aug, "nn", F32, "mm_in")

    def dn_post(j, cv):
        s = _silu(cv)
        nrm = s * lax.rsqrt(jnp.sum(s * s, axis=-1, keepdims=True) + EPS)
        fq = (j < NH).astype(F32)
        fk = (j < 2 * NH).astype(F32)
        scale = fq * (DH ** -0.5) + (1.0 - fq)
        return fk * (nrm * scale) + (1.0 - fk) * s

    def ident(a):
        return a

    def colS(w, off=0):
        return pl.BlockSpec((S, w), lambda j, off=off: (0, j + off))

    def wS(kw, w):
        return pl.BlockSpec((kw, w), lambda j: (0, j))

    qkv_spec = pl.BlockSpec((None, None, S, DH), lambda j: (j // NH, j % NH, 0, 0))
    dn_args = dict(kw=KDN, ncol=3 * NH, ins=[proj], in_specs=[colS(DH)], extras=[], extra_specs=[],
                   w=dn_cw, w_spec=wS(KDN, DH), pre=ident, post=dn_post)
    qkvn = _conv_fwd_call("dn_conv_fwd", out_shape=jax.ShapeDtypeStruct((3, NH, S, DH), F32), out_spec=qkv_spec,
                          **dn_args)

    alp = jnp.pad(dn_a_log, ((0, 0), (NH, LANES - 2 * NH)))
    dtb = jnp.pad(dn_dt_bias, ((0, 0), (NH, LANES - 2 * NH)))
    vecL = _full((1, LANES))
    ba_spec = _row(bs, LANES, a_ba // LANES)
    rowL = _row(bs, LANES)
    gate_fn = functools.partial(_f_dn_gate, NH)
    bg = _ew(lambda a, p, q: (gate_fn(a, p, q),), "dn_gate_fwd", (nb,), [proj, alp, dtb], [ba_spec, vecL, vecL],
             [((S, LANES), F32, rowL)])[0]

    def lanes_bcast(t):
        return jnp.broadcast_to(jnp.transpose(t)[:, :, None], (NH, S, DH))

    bb_b, gb_b = lanes_bcast(bg[:, :NH]), lanes_bcast(bg[:, NH:2 * NH])
    o_dn, states = _dn_fwd_call(qkvn, gb_b, bb_b)

    bsh = _div_tile(S, 512, 8)
    nbh = S // bsh
    o_spec = pl.BlockSpec((None, bsh, DH), lambda i, h: (h, i, 0))
    z_spec = pl.BlockSpec((bsh, DH), lambda i, h: (i, a_z // DH + h))
    oh_spec = pl.BlockSpec((bsh, DH), lambda i, h: (i, h))
    ng_spec = pl.BlockSpec((1, DH), lambda i, h: (0, 0))
    on = _ew(lambda o, z, g: (_f_dn_post(o, z, g),), "dn_post_fwd", (nbh, NH), [o_dn, proj, dn_norm_g],
             [o_spec, z_spec, ng_spec], [((S, DNW), BF16, oh_spec)])[0]
    _, landed = _split_wait("ag_rest_wait", ag_state, on, _gather_wait_copies)
    w_do_f, w_co_f, w_out_g, w_up_f, w_dn_g = _pair_forward(landed)
    w_out_f = w_out_g.reshape(-1, w_out_g.shape[2])
    w_dn_f = w_dn_g.reshape(-1, w_dn_g.shape[2])
    br_a = _matmul(on, w_do_f, "nn", F32, "mm_dn_o")

    def glu_pre(val, gl):
        return val * _sigmoid(gl)

    def glu_spec(t):
        return pl.BlockSpec((S, cb_cf), lambda j, t=t: (0, a_glu // cb_cf + 2 * j + t))

    cf_args = dict(kw=KCF, ncol=CFW // cb_cf, ins=[proj, proj], in_specs=[glu_spec(0), glu_spec(1)],
                   extras=[], extra_specs=[], w=cf_cw, w_spec=wS(KCF, cb_cf), pre=glu_pre, post=lambda j, cv: cv)
    uc = _conv_fwd_call("cf_conv_fwd", out_shape=jax.ShapeDtypeStruct((S, CFW), F32), out_spec=colS(cb_cf), **cf_args)
    rowC = _row(bs, CFW)
    vecC = _full((1, CFW))
    ub = _ew(lambda u, g, b: (_f_cf_ln(u, g, b),), "cf_ln_fwd", (nb,), [uc, cf_ln_g, cf_ln_b], [rowC, vecC, vecC],
             [((S, CFW), BF16, rowC)])[0]
    br_b = _matmul(ub, w_co_f, "nn", F32, "mm_cf_o")

    ga_spec, gb_spec = _row(bs, D, a_ga // D), _row(bs, D, a_gb // D)
    merged = _ew(lambda a, b, ga, gb: (_f_merge(a, b, ga, gb),), "merge_fwd", (nb,), [br_a, br_b, proj, proj],
                 [rowD, rowD, ga_spec, gb_spec], [((S, D), BF16, rowD)])[0]
    mix = _matmul(merged, w_out_f, "nn", F32, "mm_out")

    x1, hn2 = _ew(_f_res_normmod, "norm2_fwd", (nb,), [x2, mix, gt1, norm2_g, sc2, sh2],
                  [rowD, rowD, vecD, vecD, vecD, vecD], [((S, D), F32, rowD), ((S, D), BF16, rowD)])
    up_all = _matmul(hn2, w_up_f, "nn", F32, "mm_up")

    cb_ff = _div_tile(FFN, 256, LANES)
    ff_args = dict(kw=KFF, ncol=FFN // cb_ff, ins=[up_all], in_specs=[colS(cb_ff)], extras=[up_all],
                   extra_specs=[colS(cb_ff, FFN // cb_ff)], w=ff_cw, w_spec=wS(KFF, cb_ff),
                   pre=ident, post=lambda j, cv, up: _silu(cv) * up)
    hff = _conv_fwd_call("ffn_conv_fwd", out_shape=jax.ShapeDtypeStruct((S, FFN), BF16), out_spec=colS(cb_ff), **ff_args)
    ffo = _matmul(hff, w_dn_f, "nn", F32, "mm_down")

    gf2 = final_norm_g.reshape(1, D)

    def loss_bwd(a, f, gt, gf, t):
        val, vjp = jax.vjp(_f_loss, a, f, gt, gf, t)
        da, df, dgt, dgf, _ = vjp(jnp.ones((), F32))
        return da, df, dgt, dgf, jnp.zeros((1, LANES), F32) + val

    dx1_l, dffo, dgt2, dgf, loss_v = _ew(
        loss_bwd, "loss_bwd", (nb,), [x1, ffo, gt2, gf2, tgt], [rowD, rowD, vecD, vecD, rowD],
        [((S, D), F32, rowD), ((S, D), BF16, rowD), ((1, D), F32, vecD), ((1, D), F32, vecD),
         ((1, LANES), F32, vecL)], acc=(2, 3, 4))

    dhff = _matmul(dffo, w_dn_f, "nt", F32, "mm_down_dx")
    g_w_dn = _matmul(hff, dffo, "tn", F32, "mm_down_dw")

    slots = jnp.stack([chip, ci]).astype(jnp.int32)
    rs_groups = []

    def rs_begin(parts, nms, tag):
        rbs = _pair_swap_half(parts, "rs_pair_" + tag)
        q16 = [_pair_add(p, rb, core, "rs_pair_add_" + nm) for p, rb, nm in zip(parts, rbs, nms)]
        state, token = _chip_scatter_start(q16, "rs_chips_start_" + tag)
        rs_groups.append((state, nms, tag))
        return token

    tok_a = rs_begin([g_w_dn.reshape(4, FFN // 4, D)], ["ffn_w_down"], "a")

    d_upall, g_ffcw = _conv_bwd_call(
        "ffn_conv_bwd", dout=dhff, dout_spec=colS(cb_ff),
        dio_shapes=[jax.ShapeDtypeStruct((2, S, FFN), BF16)],
        dio_specs=[pl.BlockSpec((2, S, cb_ff), lambda j: (0, 0, j))], dio_pack=lambda dg, du: [(dg, du)],
        dw_shape=jax.ShapeDtypeStruct((KFF, FFN), F32), dw_spec=wS(KFF, cb_ff), after=[tok_a], **ff_args)
    dhn2 = _matmul(d_upall, w_up_f, "nt", F32, "mm_up_dx")
    g_w_up = _matmul(hn2, d_upall, "tn", F32, "mm_up_dw", out_groups=4)
    tok_b = rs_begin([g_w_up], ["ffn_w_up"], "b")

    def res2_bwd(a, mx, gt, g, sc, sh, dx1, dhn):
        _, vjp = jax.vjp(_f_res_normmod, a, mx, gt, g, sc, sh)
        return vjp((dx1, dhn))

    dx_r, dmix, dgt1, dg2, dsc2, dsh2 = _ew(
        res2_bwd, "norm2_bwd", (nb,), [x2, mix, gt1, norm2_g, sc2, sh2, dx1_l, dhn2],
        [rowD, rowD, vecD, vecD, vecD, vecD, rowD, rowD],
        [((S, D), F32, rowD), ((S, D), BF16, rowD)] + [((1, D), F32, vecD)] * 4, acc=(2, 3, 4, 5), after=[tok_b])

    dmerged = _matmul(dmix, w_out_f, "nt", F32, "mm_out_dx")
    g_w_out = _matmul(merged, dmix, "tn", F32, "mm_out_dw")

    def merge_bwd(a, b, ga, gb, dm):
        _, vjp = jax.vjp(_f_merge, a, b, ga, gb)
        da, db, dga, dgb = vjp(dm)
        return da, db, jnp.concatenate([dga, dgb], axis=1)

    d_bra, d_brb, dproj = _ew(merge_bwd, "merge_bwd", (nb,), [br_a, br_b, proj, proj, dmerged],
                              [rowD, rowD, ga_spec, gb_spec, rowD],
                              [((S, D), BF16, rowD), ((S, D), BF16, rowD),
                               ((S, NA + LANES), BF16, _row(bs, 2 * D, a_ga // (2 * D)))])

    d_on = _matmul(d_bra, w_do_f, "nt", F32, "mm_dn_o_dx")
    g_w_do = _matmul(on, d_bra, "tn", F32, "mm_dn_o_dw", out_groups=4)
    d_ub = _matmul(d_brb, w_co_f, "nt", F32, "mm_cf_o_dx")
    g_w_co = _matmul(ub, d_brb, "tn", F32, "mm_cf_o_dw", out_groups=4)
    tok_c = rs_begin([g_w_do, g_w_co, g_w_out.reshape(4, D // 4, D)], ["dn_w_o", "cf_w_o", "w_out"], "c")

    def cf_ln_bwd(u, g, b, du):
        _, vjp = jax.vjp(_f_cf_ln, u, g, b)
        return vjp(du)

    d_uc, g_cflg, g_cflb = _ew(cf_ln_bwd, "cf_ln_bwd", (nb,), [uc, cf_ln_g, cf_ln_b, d_ub], [rowC, vecC, vecC, rowC],
                               [((S, CFW), F32, rowC), ((1, CFW), F32, vecC), ((1, CFW), F32, vecC)], acc=(1, 2),
                               after=[tok_c])
    dproj_sds = jax.ShapeDtypeStruct((S, NA + LANES), BF16)
    dproj, g_cfcw = _conv_bwd_call(
        "cf_conv_bwd", dout=d_uc, dout_spec=colS(cb_cf), dio_shapes=[dproj_sds],
        dio_specs=[colS(2 * cb_cf, a_glu // (2 * cb_cf))], dio_pack=lambda dv, dg: [jnp.concatenate([dv, dg], axis=1)],
        dw_shape=jax.ShapeDtypeStruct((KCF, CFW), F32), dw_spec=wS(KCF, cb_cf), alias=dproj, **cf_args)

    def dn_post_bwd(o, z, g, d):
        _, vjp = jax.vjp(_f_dn_post, o, z, g)
        return vjp(d)

    d_o, dproj, g_dnng = _ew(dn_post_bwd, "dn_post_bwd", (nbh, NH), [o_dn, proj, dn_norm_g, d_on],
                             [o_spec, z_spec, ng_spec, oh_spec],
                             [((NH, S, DH), F32, o_spec), ((S, NA + LANES), BF16, z_spec), ((1, DH), F32, ng_spec)],
                             acc=(2,), alias=(dproj, 1))

    dqkvn, dgb_b, dbb_b = _dn_bwd_call(qkvn, gb_b, bb_b, states, d_o)
    dproj, g_dncw = _conv_bwd_call(
        "dn_conv_bwd", dout=dqkvn, dout_spec=qkv_spec, dio_shapes=[dproj_sds], dio_specs=[colS(DH)],
        dio_pack=lambda d: [d], dw_shape=jax.ShapeDtypeStruct((KDN, 3 * DNW), F32), dw_spec=wS(KDN, DH),
        alias=dproj, **dn_args)

    dbg = jnp.concatenate([jnp.transpose(dbb_b[:, :, 0]), jnp.transpose(dgb_b[:, :, 0]),
                           jnp.zeros((S, LANES - 2 * NH), F32)], axis=1)

    def gate_bwd(a, p, q, d):
        _, vjp = jax.vjp(gate_fn, a, p, q)
        return vjp(d)

    dproj, g_alp, g_dtb = _ew(gate_bwd, "dn_gate_bwd", (nb,), [proj, alp, dtb, dbg], [ba_spec, vecL, vecL, rowL],
                              [((S, NA + LANES), BF16, ba_spec), ((1, LANES), F32, vecL), ((1, LANES), F32, vecL)],
                              acc=(1, 2), alias=(dproj, 0))

    g_w_aug = _matmul(hn1, dproj, "tn", F32, "mm_in_dw")
    g_w_in = jnp.concatenate([g_w_aug[:, :o_b], g_w_aug[:, NA:NA + 2 * NH], glu_unpairs(g_w_aug[:, a_glu:NA]),
                              g_w_aug[:, a_ga:a_glu]], axis=1)
    tok_d = rs_begin([jnp.transpose(g_w_in.reshape(D, 4, NIN // 4), (1, 0, 2))], ["w_in"], "d")
    dhn1 = _matmul(dproj, w_aug, "nt", F32, "mm_in_dx", after=[tok_d])

    def norm1_bwd(a, g, sc, sh, dhn, dxr):
        _, vjp = jax.vjp(_f_normmod, a, g, sc, sh)
        da, dg, dsc, dsh = vjp(dhn)
        return da + dxr, dg, dsc, dsh

    grad_x, dg1, dsc1, dsh1 = _ew(norm1_bwd, "norm1_bwd", (nb,), [x2, norm1_g, sc1, sh1, dhn1, dx_r],
                                  [rowD, vecD, vecD, vecD, rowD, rowD],
                                  [((S, D), F32, rowD)] + [((1, D), F32, vecD)] * 3, acc=(1, 2, 3))

    halves = {}
    for state, nms, tag in rs_groups:
        q16, r3s = _chip_scatter_wait(state, grad_x, "rs_chips_wait_" + tag)
        for q, r3, nm in zip(q16, r3s, nms):
            halves[nm] = _chip_sum(q, r3, slots, "rs_chip_sum_" + nm)
    g_big = [g.reshape(w.shape) for g, w in zip(_pair_gather([halves[nm] for nm in names_big]), big)]
    gs_w_in, gs_w_do, gs_w_co, gs_w_out, gs_w_up, gs_w_dn = g_big

    dmod = jnp.concatenate([dsh1, dsc1, dgt1, dsh2, dsc2, dgt2], axis=1)
    sm2 = [dmod, dg1, dg2, dgf, g_alp, g_dtb, g_dnng, g_cflg, g_cflb, g_dncw, g_cfcw, g_ffcw]
    sm2_shapes = [tuple(a.shape) for a in sm2]
    g3 = _all_gather_small(_pack_small(sm2))
    ssum = _sum_slots(g3, "small_sum")
    dmod_all = _unpack_small(g3, sm2_shapes[:1])[0].reshape(8, 6 * D)
    (g_b_ada, gs_n1, gs_n2, gs_fn, gs_alp, gs_dtb, gs_dnng, gs_cflg, gs_cflb, gs_dncw, gs_cfcw,
     gs_ffcw) = _unpack_small(ssum, sm2_shapes)
    gs_alog, gs_dtbias = gs_alp[:, NH:2 * NH], gs_dtb[:, NH:2 * NH]
    gs_dncw = lax.dynamic_slice(gs_dncw, (0, chip * (3 * DNW // 4)), (KDN, 3 * DNW // 4))
    gs_cfcw = lax.dynamic_slice(gs_cfcw, (0, chip * (CFW // 4)), (KCF, CFW // 4))
    gs_ffcw = lax.dynamic_slice(gs_ffcw, (0, chip * (FFN // 4)), (KFF, FFN // 4))

    dmod_sh = lax.dynamic_slice(dmod_all, (0, chip * CA), (8, CA))

    def wada_fn(cc, dm):
        return (_mm(_silu(cc), dm, TN),)

    g_w_ada = _ew(wada_fn, "ada_dw", (CA // tn_a,), [c_all, dmod_sh],
                  [_full((8, D)), pl.BlockSpec((8, tn_a), lambda j: (0, j))],
                  [((D, CA), F32, pl.BlockSpec((D, tn_a), lambda j: (0, j)))])[0]

    loss = lax.psum(loss_v[0, 0], ("x", "y", "c"))

    names = ["w_ada", "b_ada", "norm1_g", "w_in", "dn_conv_w", "dn_a_log", "dn_dt_bias", "dn_norm_g", "dn_w_o",
             "cf_conv_w", "cf_ln_g", "cf_ln_b", "cf_w_o", "w_out", "norm2_g", "ffn_w_up", "ffn_conv_w", "ffn_w_down",
             "final_norm_g"]
    wts = dict(zip(names, [w_ada, b_ada, norm1_g, w_in, dn_conv_w, dn_a_log, dn_dt_bias, dn_norm_g, dn_w_o, cf_conv_w,
                           cf_ln_g, cf_ln_b, cf_w_o, w_out, norm2_g, ffn_w_up, ffn_conv_w, ffn_w_down, final_norm_g]))
    ms = dict(zip(names, [m_w_ada, m_b_ada, m_norm1_g, m_w_in, m_dn_conv_w, m_dn_a_log, m_dn_dt_bias, m_dn_norm_g,
                          m_dn_w_o, m_cf_conv_w, m_cf_ln_g, m_cf_ln_b, m_cf_w_o, m_w_out, m_norm2_g, m_ffn_w_up,
                          m_ffn_conv_w, m_ffn_w_down, m_final_norm_g]))
    vs = dict(zip(names, [v_w_ada, v_b_ada, v_norm1_g, v_w_in, v_dn_conv_w, v_dn_a_log, v_dn_dt_bias, v_dn_norm_g,
                          v_dn_w_o, v_cf_conv_w, v_cf_ln_g, v_cf_ln_b, v_cf_w_o, v_w_out, v_norm2_g, v_ffn_w_up,
                          v_ffn_conv_w, v_ffn_w_down, v_final_norm_g]))
    grads = {"w_ada": g_w_ada, "b_ada": g_b_ada, "norm1_g": gs_n1, "w_in": gs_w_in, "dn_conv_w": gs_dncw,
             "dn_a_log": gs_alog, "dn_dt_bias": gs_dtbias, "dn_norm_g": gs_dnng, "dn_w_o": gs_w_do,
             "cf_conv_w": gs_cfcw, "cf_ln_g": gs_cflg, "cf_ln_b": gs_cflb, "cf_w_o": gs_w_co, "w_out": gs_w_out,
             "norm2_g": gs_n2, "ffn_w_up": gs_w_up, "ffn_conv_w": gs_ffcw, "ffn_w_down": gs_w_dn,
             "final_norm_g": gs_fn}
    grads = {n: grads[n].reshape(wts[n].shape) for n in names}

    large = ["w_ada", "w_in", "dn_w_o", "cf_w_o", "w_out", "ffn_w_up", "ffn_w_down"]
    small = [n for n in names if n not in large]
    delta, new_m, new_v = {}, {}, {}
    for n in large:
        d_, m_, v_ = _adamw(wts[n][0], grads[n][0], ms[n][0], vs[n][0], "adamw_" + n)
        delta[n], new_m[n], new_v[n] = d_[None], m_[None], v_[None]
    sm_sh = [tuple(wts[n].shape) for n in small]
    d_, m_, v_ = _adamw(_pack_small([wts[n] for n in small]), _pack_small([grads[n] for n in small]),
                        _pack_small([ms[n] for n in small]), _pack_small([vs[n] for n in small]), "adamw_small")
    for n, a, b_, c_ in zip(small, _unpack_small(d_, sm_sh), _unpack_small(m_, sm_sh), _unpack_small(v_, sm_sh)):
        delta[n], new_m[n], new_v[n] = a, b_, c_

    return (loss, grad_x[None], *[grads[n] for n in names], *[delta[n] for n in names],
            *[new_m[n] for n in names], *[new_v[n] for n in names])
```

```python
import functools

import jax
import jax.numpy as jnp
from jax import lax
from jax.experimental import pallas as pl
from jax.experimental.pallas import tpu as pltpu

F32 = jnp.float32
BF16 = jnp.bfloat16
EPS = 1e-6
LANES = 128
VMEM_LIMIT = 48 * 1024 * 1024
DN_CHUNK = 128
ADAM_LR, ADAM_B1, ADAM_B2, ADAM_EPS, ADAM_WD, ADAM_STEP = 0.001, 0.9, 0.999, 1e-08, 0.01, 10
MESH = pl.DeviceIdType.MESH

NN = (((1,), (0,)), ((), ()))
NT = (((1,), (1,)), ((), ()))
TN = (((0,), (0,)), ((), ()))
_DIMS = {"nn": NN, "nt": NT, "tn": TN}


def _mm(a, b, dims):
    return lax.dot_general(a.astype(BF16), b.astype(BF16), dims, preferred_element_type=F32)


def _mmx(a, b, dims):
    return lax.dot_general(a, b, dims, precision=lax.Precision.HIGHEST, preferred_element_type=F32)


def _div_tile(n, target, mult):
    best = None
    t = mult
    while t <= min(n, target):
        if n % t == 0:
            best = t
        t += mult
    return best if best is not None else n


def _params(sem=None):
    kw = dict(vmem_limit_bytes=VMEM_LIMIT)
    if sem is not None:
        kw["dimension_semantics"] = sem
    return pltpu.CompilerParams(**kw)


def _sigmoid(x):
    return jax.nn.sigmoid(x)


def _silu(x):
    return x * jax.nn.sigmoid(x)


def _softplus(x):
    return jnp.maximum(x, 0.0) + jnp.log(1.0 + jnp.exp(-jnp.abs(x)))


def _view(arr):
    if arr.ndim == 3:
        return arr.shape[1], arr.shape[0] * arr.shape[2], arr.shape[0]
    return arr.shape[0], arr.shape[1], 1


def _tile_spec(groups, cols, tr, tc, rsel, csel):
    if groups > 1:
        per = cols // groups // tc
        return pl.BlockSpec((None, tr, tc), lambda i, j, k: (csel(i, j, k) // per, rsel(i, j, k), csel(i, j, k) % per))
    return pl.BlockSpec((tr, tc), lambda i, j, k: (rsel(i, j, k), csel(i, j, k)))


def _matmul(a, b, mode, out_dtype, name, out_groups=1, after=()):
    ar, ac, ag = _view(a)
    br, bc, bg = _view(b)
    if mode == "nn":
        M, K, N = ar, ac, bc
        kdiv, mdiv, ndiv = ac // ag, M, min(bc // bg, N // out_groups)
    elif mode == "nt":
        M, K, N = ar, ac, br
        kdiv, mdiv, ndiv = min(ac // ag, bc // bg), M, N // out_groups
    else:
        K, M, N = ar, ac, bc
        kdiv, mdiv, ndiv = K, ac // ag, min(bc // bg, N // out_groups)
    tm = _div_tile(mdiv, 1024, LANES)
    tn = _div_tile(ndiv, 1536, LANES)
    tk = _div_tile(kdiv, 2048, LANES)
    nk = K // tk
    dims = _DIMS[mode]
    si, sj, sk = (lambda i, j, k: i), (lambda i, j, k: j), (lambda i, j, k: k)
    a_spec = {"nn": _tile_spec(ag, ac, tm, tk, si, sk), "nt": _tile_spec(ag, ac, tm, tk, si, sk),
              "tn": _tile_spec(ag, ac, tk, tm, sk, si)}[mode]
    b_spec = {"nn": _tile_spec(bg, bc, tk, tn, sk, sj), "nt": _tile_spec(bg, bc, tn, tk, sj, sk),
              "tn": _tile_spec(bg, bc, tk, tn, sk, sj)}[mode]
    out_shape = (M, N) if out_groups == 1 else (out_groups, M, N // out_groups)

    n_after = len(after)

    def body(*refs):
        a_ref, b_ref, o_ref = refs[0], refs[1], refs[2 + n_after]
        if nk == 1:
            o_ref[...] = lax.dot_general(a_ref[...], b_ref[...], dims, preferred_element_type=F32).astype(o_ref.dtype)
            return
        acc_ref = refs[3 + n_after]
        k = pl.program_id(2)

        @pl.when(k == 0)
        def _():
            acc_ref[...] = jnp.zeros_like(acc_ref)

        acc_ref[...] += lax.dot_general(a_ref[...], b_ref[...], dims, preferred_element_type=F32)

        @pl.when(k == nk - 1)
        def _():
            o_ref[...] = acc_ref[...].astype(o_ref.dtype)

    return pl.pallas_call(
        body, name=name, grid=(M // tm, N // tn, nk),
        in_specs=[a_spec, b_spec] + [pl.BlockSpec(memory_space=pl.ANY)] * n_after,
        out_specs=_tile_spec(out_groups, N, tm, tn, si, sj),
        out_shape=jax.ShapeDtypeStruct(out_shape, out_dtype),
        scratch_shapes=[pltpu.VMEM((tm, tn), F32)] if nk > 1 else [],
        compiler_params=_params(("parallel", "parallel", "arbitrary")),
    )(a, b, *after)


def _ew(fn, name, grid, ins, in_specs, outs, acc=(), alias=None, after=()):
    n_in = len(ins)
    n_ax = len(grid)
    extra, aliases = list(after), {}
    if alias is not None:
        extra, aliases = extra + [alias[0]], {n_in + len(after): alias[1]}
    extra_specs = [pl.BlockSpec(memory_space=pl.ANY)] * len(extra)

    def body(*refs):
        in_refs, out_refs = refs[:n_in], refs[n_in + len(extra):]
        ids = [pl.program_id(a) for a in range(n_ax)]
        res = fn(*[r[...] for r in in_refs])
        first = ids[0] == 0
        for t in ids[1:]:
            first = jnp.logical_and(first, t == 0)
        for idx, (r, val) in enumerate(zip(out_refs, res)):
            if idx in acc:
                @pl.when(first)
                def _(r=r, val=val):
                    r[...] = val.astype(r.dtype)

                @pl.when(jnp.logical_not(first))
                def _(r=r, val=val):
                    r[...] += val.astype(r.dtype)
            elif isinstance(val, tuple):
                for t, part in enumerate(val):
                    r[t] = part.astype(r.dtype)
            else:
                r[...] = val.astype(r.dtype)

    return pl.pallas_call(
        body, name=name, grid=grid, in_specs=list(in_specs) + extra_specs,
        out_specs=[o[2] for o in outs],
        out_shape=[jax.ShapeDtypeStruct(o[0], o[1]) for o in outs],
        input_output_aliases=aliases,
        compiler_params=_params(("arbitrary",) * n_ax),
    )(*ins, *extra)


def _ew_slot(fn, name, grid, slots, ins, in_specs, out_shape, out_dtype, out_spec):
    def body(s_ref, *refs):
        refs[-1][...] = fn(*[r[...] for r in refs[:-1]]).astype(refs[-1].dtype)

    return pl.pallas_call(
        body, name=name,
        grid_spec=pltpu.PrefetchScalarGridSpec(num_scalar_prefetch=1, grid=grid, in_specs=list(in_specs),
                                               out_specs=out_spec),
        out_shape=jax.ShapeDtypeStruct(out_shape, out_dtype),
        compiler_params=_params(("arbitrary",) * len(grid)),
    )(slots, *ins)


def _row(bs, w, col=0):
    return pl.BlockSpec((bs, w), lambda i, col=col: (i, col))


def _full(shape):
    nd = len(shape)
    return pl.BlockSpec(tuple(shape), lambda *_: (0,) * nd)


def _rms(x, g):
    return x * lax.rsqrt(jnp.mean(x * x, axis=-1, keepdims=True) + EPS) * g


def _f_normmod(x, g, sc, sh):
    return _rms(x, g) * (1.0 + sc) + sh


def _f_res_normmod(x, mix, gt, g, sc, sh):
    x1 = x + gt * mix
    return x1, _f_normmod(x1, g, sc, sh)


def _f_loss(x1, f, gt, gf, tgt):
    y = _rms(x1 + gt * f, gf)
    return 0.5 * jnp.sum(jnp.mean(jnp.square(y - tgt), axis=-1))


def _f_dn_gate(nh, ba, alp, dtb):
    lane = lax.broadcasted_iota(jnp.int32, ba.shape, 1)
    m = (lane < nh).astype(F32)
    beta = _sigmoid(ba)
    g = -jnp.exp(alp) * _softplus(ba + dtb)
    return m * beta + (1.0 - m) * g


def _f_dn_post(o, z, g):
    return o * lax.rsqrt(jnp.mean(o * o, axis=-1, keepdims=True) + EPS) * g * _silu(z)


def _f_cf_ln(u, g, b):
    mu = jnp.mean(u, axis=-1, keepdims=True)
    xc = u - mu
    y = xc * lax.rsqrt(jnp.mean(xc * xc, axis=-1, keepdims=True) + EPS)
    return _silu(y * g + b)


def _f_merge(a, b, ga, gb):
    return _sigmoid(ga) * a + _sigmoid(gb) * b


def _shift_down(u, d, rows):
    if d == 0:
        return u
    return jnp.where(rows >= d, pltpu.roll(u, d, 0), 0.0)


def _shift_up(u, d, rows):
    if d == 0:
        return u
    s = u.shape[0]
    return jnp.where(rows < s - d, pltpu.roll(u, s - d, 0), 0.0)


def _conv(u, w_ref, kw, rows):
    acc = None
    for k in range(kw):
        t = w_ref[k:k + 1, :] * _shift_down(u, kw - 1 - k, rows)
        acc = t if acc is None else acc + t
    return acc


def _conv_t(dc, w_ref, kw, rows):
    acc = None
    for k in range(kw):
        t = w_ref[k:k + 1, :] * _shift_up(dc, kw - 1 - k, rows)
        acc = t if acc is None else acc + t
    return acc


def _conv_fwd_call(name, kw, ncol, ins, in_specs, extras, extra_specs, w, w_spec, pre, post, out_shape, out_spec):
    n_in, n_ex = len(ins), len(extras)

    def body(*refs):
        in_refs, ex_refs = refs[:n_in], refs[n_in:n_in + n_ex]
        w_ref, out_ref = refs[n_in + n_ex], refs[n_in + n_ex + 1]
        j = pl.program_id(0)
        u = pre(*[r[...] for r in in_refs])
        rows = lax.broadcasted_iota(jnp.int32, u.shape, 0)
        cv = _conv(u, w_ref, kw, rows)
        out_ref[...] = post(j, cv, *[r[...] for r in ex_refs]).astype(out_ref.dtype)

    return pl.pallas_call(
        body, name=name, grid=(ncol,), in_specs=list(in_specs) + list(extra_specs) + [w_spec],
        out_specs=out_spec, out_shape=out_shape, compiler_params=_params(("arbitrary",)),
    )(*ins, *extras, w)


def _conv_bwd_call(name, kw, ncol, ins, in_specs, extras, extra_specs, w, w_spec, pre, post, dout, dout_spec,
                   dio_shapes, dio_specs, dio_pack, dw_shape, dw_spec, alias=None, after=()):
    n_in, n_ex, n_io = len(ins), len(extras), len(dio_shapes)
    al, aliases = list(after), {}
    if alias is not None:
        al, aliases = al + [alias], {n_in + n_ex + 2 + len(after): 0}
    al_specs = [pl.BlockSpec(memory_space=pl.ANY)] * len(al)

    def body(*refs):
        in_refs, ex_refs = refs[:n_in], refs[n_in:n_in + n_ex]
        w_ref, dout_ref = refs[n_in + n_ex], refs[n_in + n_ex + 1]
        outs = refs[n_in + n_ex + 2 + len(al):]
        dio_refs, dw_ref = outs[:n_io], outs[n_io]
        j = pl.program_id(0)
        u, pre_vjp = jax.vjp(pre, *[r[...] for r in in_refs])
        rows = lax.broadcasted_iota(jnp.int32, u.shape, 0)
        cv = _conv(u, w_ref, kw, rows)
        _, post_vjp = jax.vjp(lambda cc, *ex: post(j, cc, *ex), cv, *[r[...] for r in ex_refs])
        g = post_vjp(dout_ref[...].astype(F32))
        dc = g[0]
        for k in range(kw):
            dw_ref[k:k + 1, :] = jnp.sum(dc * _shift_down(u, kw - 1 - k, rows), axis=0, keepdims=True)
        du = _conv_t(dc, w_ref, kw, rows)
        for r, val in zip(dio_refs, dio_pack(*pre_vjp(du), *g[1:])):
            if isinstance(val, tuple):
                for t, part in enumerate(val):
                    r[t] = part.astype(r.dtype)
            else:
                r[...] = val.astype(r.dtype)

    return pl.pallas_call(
        body, name=name, grid=(ncol,),
        in_specs=list(in_specs) + list(extra_specs) + [w_spec, dout_spec] + al_specs,
        out_specs=list(dio_specs) + [dw_spec],
        out_shape=list(dio_shapes) + [dw_shape],
        input_output_aliases=aliases,
        compiler_params=_params(("arbitrary",)),
    )(*ins, *extras, w, dout, *al)


def _tri_inverse(a):
    c = a.shape[0]
    ii = lax.broadcasted_iota(jnp.int32, (c, c), 0)
    jj = lax.broadcasted_iota(jnp.int32, (c, c), 1)
    y = -a
    t = (ii == jj).astype(F32) + y
    p = 2
    while p < c:
        y = _mmx(y, y, NN)
        t = t + _mmx(t, y, NN)
        p *= 2
    return t


def _dn_common(q, k, v, gb, bb):
    c = q.shape[0]
    ii = lax.broadcasted_iota(jnp.int32, (c, c), 0)
    jj = lax.broadcasted_iota(jnp.int32, (c, c), 1)
    causal = jj <= ii
    strict = jj < ii
    low = causal.astype(F32)
    ones = jnp.ones((c, LANES), F32)
    gc = _mmx(low, gb, NN)
    diff = (_mmx(gc, ones, NT) - _mmx(ones, gc, NT)) * (1.0 / LANES)
    decay = jnp.where(causal, jnp.exp(jnp.where(causal, diff, 0.0)), 0.0)
    gl = jnp.sum(gb, axis=0, keepdims=True)
    eg = jnp.exp(gc)
    egm = jnp.exp(gl - gc)
    egl = jnp.exp(gl)
    kb = k * bb
    vb = v * bb
    kbg = kb * eg
    kk = _mm(kb, k, NT)
    t = _tri_inverse(jnp.where(strict, kk * decay, 0.0))
    qk = _mm(q, k, NT)
    attn = qk * decay
    return dict(causal=causal, strict=strict, low=low, ones=ones, decay=decay, eg=eg, egm=egm, egl=egl,
                kb=kb, vb=vb, kbg=kbg, kk=kk, t=t, qk=qk, attn=attn, qd=q * eg, kd=k * egm)


def _dn_fwd_chunk(q, k, v, gb, bb, s):
    m = _dn_common(q, k, v, gb, bb)
    u = _mmx(m["t"], m["vb"] - _mm(m["kbg"], s, NN), NN)
    o = _mm(m["qd"], s, NN) + _mm(m["attn"], u, NN)
    s2 = s * m["egl"] + _mm(m["kd"], u, TN)
    return o, s2


def _dn_bwd_chunk(q, k, v, gb, bb, s, do, dsp):
    m = _dn_common(q, k, v, gb, bb)
    c = q.shape[0]
    t, decay, eg, egm, egl = m["t"], m["decay"], m["eg"], m["egm"], m["egl"]
    u = _mmx(t, m["vb"] - _mm(m["kbg"], s, NN), NN)
    du = _mm(m["attn"], do, TN) + _mm(m["kd"], dsp, NN)
    dattn = jnp.where(m["causal"], _mm(do, u, NT), 0.0)
    dqd = _mm(do, s, NT)
    dkd = _mm(u, dsp, NT)
    dr = _mmx(t, du, TN)
    da = jnp.where(m["strict"], -_mm(dr, u, NT), 0.0)
    dkbg = -_mm(dr, s, NT)
    ds = dsp * egl + _mm(m["qd"], do, TN) - _mm(m["kbg"], dr, TN)
    degl = jnp.sum(jnp.sum(dsp * s, axis=1, keepdims=True), axis=0, keepdims=True)
    dkk = da * decay
    dqk = dattn * decay
    ddiff = (da * m["kk"] + dattn * m["qk"]) * decay
    dgc = _mmx(ddiff, m["ones"], NN) - _mmx(ddiff, m["ones"], TN)
    dkb = _mm(dkk, k, NN) + dkbg * eg
    dk = _mm(dkk, m["kb"], TN) + _mm(dqk, q, TN) + dkd * egm + dkb * bb
    dq = _mm(dqk, k, NN) + dqd * eg
    dgc = dgc + jnp.sum(dqd * q + dkbg * m["kb"], axis=-1, keepdims=True) * eg
    tt = jnp.sum(dkd * k, axis=-1, keepdims=True) * egm
    dgc = dgc - tt
    dgl = jnp.sum(tt, axis=0, keepdims=True) + degl * egl
    dbb = jnp.sum(dkb * k + dr * v, axis=-1, keepdims=True) + jnp.zeros((c, LANES), F32)
    dv = dr * bb
    dgb = _mmx(m["low"], dgc, TN) + dgl
    return dq, dk, dv, dgb, dbb, ds


def _dn_fwd_call(qkvn, gb, bb):
    _, nh, s, dh = qkvn.shape
    c = min(DN_CHUNK, s)
    n = s // c
    hb = nh

    def body(q_ref, k_ref, v_ref, g_ref, b_ref, o_ref, st_ref, s_ref):
        @pl.when(pl.program_id(1) == 0)
        def _():
            s_ref[...] = jnp.zeros_like(s_ref)

        for h in range(hb):
            st = s_ref[h]
            st_ref[h] = st
            o, s2 = _dn_fwd_chunk(q_ref[h], k_ref[h], v_ref[h], g_ref[h], b_ref[h], st)
            o_ref[h] = o
            s_ref[h] = s2

    def qspec(t):
        return pl.BlockSpec((None, hb, c, dh), lambda i, j, t=t: (t, i, j, 0))

    hs = pl.BlockSpec((hb, c, dh), lambda i, j: (i, j, 0))
    return pl.pallas_call(
        body, name="dn_fwd", grid=(nh // hb, n),
        in_specs=[qspec(0), qspec(1), qspec(2), hs, hs],
        out_specs=[hs, pl.BlockSpec((hb, None, dh, dh), lambda i, j: (i, j, 0, 0))],
        out_shape=[jax.ShapeDtypeStruct((nh, s, dh), F32), jax.ShapeDtypeStruct((nh, n, dh, dh), F32)],
        scratch_shapes=[pltpu.VMEM((hb, dh, dh), F32)],
        compiler_params=_params(("arbitrary", "arbitrary")),
    )(qkvn, qkvn, qkvn, gb, bb)


def _dn_bwd_call(qkvn, gb, bb, states, do):
    _, nh, s, dh = qkvn.shape
    c = min(DN_CHUNK, s)
    n = s // c
    hb = nh

    def body(q_ref, k_ref, v_ref, g_ref, b_ref, st_ref, do_ref, dqkv_ref, dg_ref, db_ref, ds_ref):
        @pl.when(pl.program_id(1) == 0)
        def _():
            ds_ref[...] = jnp.zeros_like(ds_ref)

        for h in range(hb):
            dq, dk, dv, dg, db, ds = _dn_bwd_chunk(q_ref[h], k_ref[h], v_ref[h], g_ref[h], b_ref[h], st_ref[h],
                                                   do_ref[h], ds_ref[h])
            dqkv_ref[0, h] = dq
            dqkv_ref[1, h] = dk
            dqkv_ref[2, h] = dv
            dg_ref[h] = dg
            db_ref[h] = db
            ds_ref[h] = ds

    def qspec(t):
        return pl.BlockSpec((None, hb, c, dh), lambda i, j, t=t: (t, i, n - 1 - j, 0))

    hs = pl.BlockSpec((hb, c, dh), lambda i, j: (i, n - 1 - j, 0))
    sh = jax.ShapeDtypeStruct((nh, s, dh), F32)
    return pl.pallas_call(
        body, name="dn_bwd", grid=(nh // hb, n),
        in_specs=[qspec(0), qspec(1), qspec(2), hs, hs,
                  pl.BlockSpec((hb, None, dh, dh), lambda i, j: (i, n - 1 - j, 0, 0)), hs],
        out_specs=[pl.BlockSpec((3, hb, c, dh), lambda i, j: (0, i, n - 1 - j, 0)), hs, hs],
        out_shape=[jax.ShapeDtypeStruct((3, nh, s, dh), F32), sh, sh],
        scratch_shapes=[pltpu.VMEM((hb, dh, dh), F32)],
        compiler_params=_params(("arbitrary", "arbitrary")),
    )(qkvn, qkvn, qkvn, gb, bb, states, do)


def _adamw(w, g, m, v, name):
    r, c = w.shape
    br = _div_tile(r, max(8, (1 << 18) // max(c, 1)), 8)

    def fn(w, g, m, v):
        m = ADAM_B1 * m + (1.0 - ADAM_B1) * g
        v = ADAM_B2 * v + (1.0 - ADAM_B2) * jnp.square(g)
        m_hat = m / (1.0 - ADAM_B1 ** ADAM_STEP)
        v_hat = v / (1.0 - ADAM_B2 ** ADAM_STEP)
        delta = -ADAM_LR * (m_hat / (jnp.sqrt(v_hat) + ADAM_EPS) + ADAM_WD * w)
        return delta, m, v

    spec = pl.BlockSpec((br, c), lambda i: (i, 0))
    return _ew(fn, name, (r // br,), [w, g, m, v], [spec] * 4, [((r, c), F32, spec)] * 3)


def _coords():
    return lax.axis_index("x"), lax.axis_index("y"), lax.axis_index("c")


def _all_gather_small(v):
    r, w = v.shape

    def body(v_ref, out_ref, send_sems, recv_sems, local_sem):
        x, y, c = _coords()
        me = 4 * x + 2 * y + c
        mine = pltpu.make_async_copy(v_ref, out_ref.at[me], local_sem)
        mine.start()
        peers = []
        for k in range(1, 8):
            px = 1 - x if k & 4 else x
            py = 1 - y if k & 2 else y
            pc = 1 - c if k & 1 else c
            peers.append((px, py, pc))
        sends = []
        for k, peer in enumerate(peers):
            cp = pltpu.make_async_remote_copy(src_ref=v_ref, dst_ref=out_ref.at[me], send_sem=send_sems.at[k],
                                              recv_sem=recv_sems.at[k], device_id=peer, device_id_type=MESH)
            cp.start()
            sends.append(cp)
        for k, (px, py, pc) in enumerate(peers):
            pltpu.make_async_remote_copy(src_ref=v_ref, dst_ref=out_ref.at[4 * px + 2 * py + pc],
                                         send_sem=send_sems.at[k], recv_sem=recv_sems.at[k],
                                         device_id=(px, py, pc), device_id_type=MESH).wait_recv()
        for cp in sends:
            cp.wait_send()
        mine.wait()

    return pl.pallas_call(
        body, name="ag_small", out_shape=jax.ShapeDtypeStruct((8, r, w), v.dtype),
        in_specs=[pl.BlockSpec(memory_space=pltpu.VMEM)], out_specs=pl.BlockSpec(memory_space=pltpu.VMEM),
        scratch_shapes=[pltpu.SemaphoreType.DMA((7,)), pltpu.SemaphoreType.DMA((7,)), pltpu.SemaphoreType.DMA],
        compiler_params=pltpu.CompilerParams(vmem_limit_bytes=VMEM_LIMIT),
    )(v)


def _hbm_call(body, name, arrays, out_shapes, n_sems, aliases=None):
    hbm = pl.BlockSpec(memory_space=pltpu.HBM)
    return pl.pallas_call(
        body, name=name, out_shape=list(out_shapes), in_specs=[hbm] * len(arrays), out_specs=[hbm] * len(out_shapes),
        input_output_aliases=aliases or {},
        scratch_shapes=[pltpu.SemaphoreType.DMA((n_sems,)), pltpu.SemaphoreType.DMA((n_sems,))],
    )(*arrays)


def _half_rows(ref_shape, c):
    h = ref_shape[1] // 2
    return pl.ds(pl.multiple_of(c * h, 16), h), pl.ds(pl.multiple_of((1 - c) * h, 16), h)


def _pair_swap_half(ps, name):
    n = len(ps)

    def body(*refs):
        ins, outs, send_sems, recv_sems = refs[:n], refs[n:2 * n], refs[2 * n], refs[2 * n + 1]
        x, y, c = _coords()
        cps = []
        for w in range(n):
            _, other = _half_rows(ins[w].shape, c)
            cp = pltpu.make_async_remote_copy(src_ref=ins[w].at[:, other], dst_ref=outs[w], send_sem=send_sems.at[w],
                                              recv_sem=recv_sems.at[w], device_id=(x, y, 1 - c), device_id_type=MESH)
            cp.start()
            cps.append(cp)
        for cp in cps:
            cp.wait()

    shapes = [jax.ShapeDtypeStruct((p.shape[0], p.shape[1] // 2, p.shape[2]), p.dtype) for p in ps]
    return _hbm_call(body, name, ps, shapes, n)


def _pair_gather(gs, name):
    n = len(gs)

    def body(*refs):
        ins, outs, send_sems, recv_sems = refs[:n], refs[n:2 * n], refs[2 * n], refs[2 * n + 1]
        x, y, c = _coords()
        cps = []
        for w in range(n):
            cp = pltpu.make_async_remote_copy(src_ref=ins[w].at[c], dst_ref=outs[w].at[c], send_sem=send_sems.at[w],
                                              recv_sem=recv_sems.at[w], device_id=(x, y, 1 - c), device_id_type=MESH)
            cp.start()
            cps.append(cp)
        for w, cp in enumerate(cps):
            pltpu.make_async_remote_copy(src_ref=ins[w].at[c], dst_ref=outs[w].at[1 - c], send_sem=send_sems.at[w],
                                         recv_sem=recv_sems.at[w], device_id=(x, y, 1 - c),
                                         device_id_type=MESH).wait_recv()
            cp.wait_send()

    shapes = [jax.ShapeDtypeStruct(g.shape, g.dtype) for g in gs]
    return _hbm_call(body, name, gs, shapes, n, aliases={i: i for i in range(n)})


_HBM = pl.BlockSpec(memory_space=pltpu.HBM)
_SEM = pl.BlockSpec(memory_space=pltpu.SEMAPHORE)
_EFFECT = pltpu.SideEffectType.DATAFLOW_SIDE_EFFECTING


def _split_start(name, srcs, lands, after, n_copies, make_copies):
    n, m = len(srcs), len(lands)
    arrays = [pltpu.with_memory_space_constraint(a, pltpu.HBM) for a in list(srcs) + list(lands)]

    def body(*refs):
        src_refs, land_refs = refs[:n], refs[n:n + m]
        send_sems, recv_sems = refs[n + m + 1], refs[n + m + 2]
        for cp in make_copies(src_refs, land_refs, send_sems, recv_sems):
            cp.start()
        refs[-1][...] = jnp.zeros_like(refs[-1])

    outs = pl.pallas_call(
        body, name=name,
        out_shape=(pltpu.SemaphoreType.DMA((n_copies,)), pltpu.SemaphoreType.DMA((n_copies,)),
                   *[pltpu.HBM(a.shape, a.dtype) for a in arrays], jax.ShapeDtypeStruct((8, LANES), F32)),
        in_specs=[_HBM] * (n + m) + [pl.BlockSpec(memory_space=pl.ANY)],
        out_specs=(_SEM, _SEM, *[_HBM] * (n + m), pl.BlockSpec(memory_space=pltpu.VMEM)),
        input_output_aliases={i: 2 + i for i in range(n + m)},
        compiler_params=pltpu.CompilerParams(has_side_effects=_EFFECT),
    )(*arrays, after)
    return (outs[0], outs[1], list(outs[2:2 + n]), list(outs[2 + n:2 + n + m])), outs[-1]


def _split_wait(name, state, after, make_copies):
    send_sems, recv_sems, srcs, lands = state
    n, m = len(srcs), len(lands)

    def body(*refs):
        src_refs, land_refs = refs[:n], refs[n:n + m]
        for cp in make_copies(src_refs, land_refs, refs[n + m], refs[n + m + 1]):
            cp.wait_send()
            cp.wait_recv()

    outs = pl.pallas_call(
        body, name=name, out_shape=tuple(pltpu.HBM(a.shape, a.dtype) for a in srcs + lands),
        in_specs=[_HBM] * (n + m) + [_SEM, _SEM, pl.BlockSpec(memory_space=pl.ANY)], out_specs=tuple([_HBM] * (n + m)),
        input_output_aliases={i: i for i in range(n + m)},
        compiler_params=pltpu.CompilerParams(has_side_effects=_EFFECT),
    )(*srcs, *lands, send_sems, recv_sems, after)
    return list(outs[:n]), list(outs[n:])


def _scatter_copies(q_refs, land_refs, send_sems, recv_sems):
    x, y, c = _coords()
    cps = []
    for w, (q, land) in enumerate(zip(q_refs, land_refs)):
        for k, (px, py) in enumerate([(1 - x, y), (x, 1 - y), (1 - x, 1 - y)]):
            cps.append(pltpu.make_async_remote_copy(src_ref=q.at[2 * px + py], dst_ref=land.at[k],
                                                    send_sem=send_sems.at[3 * w + k], recv_sem=recv_sems.at[3 * w + k],
                                                    device_id=(px, py, c), device_id_type=MESH))
    return cps


def _chip_scatter_start(qs, name):
    lands = [lax.empty((3,) + q.shape[1:], q.dtype) for q in qs]
    return _split_start(name, qs, lands, qs[0], 3 * len(qs), _scatter_copies)


def _chip_scatter_wait(state, after, name):
    return _split_wait(name, state, after, _scatter_copies)


def _gather_copies(src_refs, buf_refs, send_sems, recv_sems):
    x, y, c = _coords()
    j = 2 * x + y
    cps = []
    for w, buf in enumerate(buf_refs):
        mine, _ = _half_rows(buf.shape, c)
        for k, (px, py) in enumerate([(1 - x, y), (x, 1 - y), (1 - x, 1 - y)]):
            cps.append(pltpu.make_async_remote_copy(src_ref=buf.at[j, mine], dst_ref=buf.at[j, mine],
                                                    send_sem=send_sems.at[3 * w + k], recv_sem=recv_sems.at[3 * w + k],
                                                    device_id=(px, py, c), device_id_type=MESH))
    return cps


def _gather_wait_copies(src_refs, buf_refs, send_sems, recv_sems):
    x, y, c = _coords()
    j = 2 * x + y
    cps = []
    for w, buf in enumerate(buf_refs):
        mine, _ = _half_rows(buf.shape, c)
        for k, (px, py) in enumerate([(1 - x, y), (x, 1 - y), (1 - x, 1 - y)]):
            cps.append(pltpu.make_async_remote_copy(src_ref=buf.at[j, mine], dst_ref=buf.at[2 * px + py, mine],
                                                    send_sem=send_sems.at[3 * w + k], recv_sem=recv_sems.at[3 * w + k],
                                                    device_id=(px, py, c), device_id_type=MESH))
    return cps


def _pair_forward(bufs, name):
    n = len(bufs)

    def body(*refs):
        ins, outs, send_sems, recv_sems = refs[:n], refs[n:2 * n], refs[2 * n], refs[2 * n + 1]
        x, y, c = _coords()
        chips = [(1 - x, y), (x, 1 - y), (1 - x, 1 - y)]
        cps = []
        for w in range(n):
            mine, _ = _half_rows(outs[w].shape, c)
            for k, (px, py) in enumerate(chips):
                cp = pltpu.make_async_remote_copy(src_ref=ins[w].at[2 * px + py, mine],
                                                  dst_ref=outs[w].at[2 * px + py, mine],
                                                  send_sem=send_sems.at[3 * w + k], recv_sem=recv_sems.at[3 * w + k],
                                                  device_id=(x, y, 1 - c), device_id_type=MESH)
                cp.start()
                cps.append(cp)
        for w in range(n):
            _, sib = _half_rows(outs[w].shape, c)
            for k, (px, py) in enumerate(chips):
                pltpu.make_async_remote_copy(src_ref=ins[w].at[2 * px + py, sib], dst_ref=outs[w].at[2 * px + py, sib],
                                             send_sem=send_sems.at[3 * w + k], recv_sem=recv_sems.at[3 * w + k],
                                             device_id=(x, y, 1 - c), device_id_type=MESH).wait_recv()
        for cp in cps:
            cp.wait_send()

    shapes = [jax.ShapeDtypeStruct(b.shape, b.dtype) for b in bufs]
    return _hbm_call(body, name, bufs, shapes, 3 * n, aliases={i: i for i in range(n)})


def _slot_rows(h, cs):
    return _div_tile(h, max(16, (1 << 19) // cs), 16)


def _cast_into_slot(w, slot, name):
    r, cs = w.shape
    br = _slot_rows(r, cs)
    return _ew_slot(lambda a: a, name, (r // br,), slot, [w], [pl.BlockSpec((br, cs), lambda i, s: (i, 0))],
                    (4, r, cs), BF16, pl.BlockSpec((None, br, cs), lambda i, s: (s[0], i, 0)))


def _pair_add(p, rb, core, name):
    n, r, cs = p.shape
    h = r // 2
    br = _slot_rows(h, cs)
    nb = h // br
    return _ew_slot(lambda a, b: a + b, name, (n, nb), core, [p, rb],
                    [pl.BlockSpec((None, br, cs), lambda s, i, c: (s, c[0] * nb + i, 0)),
                     pl.BlockSpec((None, br, cs), lambda s, i, c: (s, i, 0))],
                    (n, h, cs), BF16, pl.BlockSpec((None, br, cs), lambda s, i, c: (s, i, 0)))


def _chip_sum(q, r3, slots, name):
    _, h, cs = q.shape
    br = _slot_rows(h, cs)

    def fn(a, b):
        acc = a.astype(F32)
        for k in range(3):
            acc = acc + b[k].astype(F32)
        return acc

    return _ew_slot(fn, name, (h // br,), slots, [q, r3],
                    [pl.BlockSpec((None, br, cs), lambda i, s: (s[0], i, 0)),
                     pl.BlockSpec((3, br, cs), lambda i, s: (0, i, 0))],
                    (2, h, cs), F32, pl.BlockSpec((None, br, cs), lambda i, s: (s[1], i, 0)))


def _sum_slots(r, name):
    n, h, w = r.shape
    br = _div_tile(h, 2048, 16)

    def fn(blk):
        acc = blk[0].astype(F32)
        for s in range(1, n):
            acc = acc + blk[s].astype(F32)
        return (acc,)

    return _ew(fn, name, (h // br,), [r], [pl.BlockSpec((n, br, w), lambda i: (0, i, 0))],
               [((h, w), F32, pl.BlockSpec((br, w), lambda i: (i, 0)))])[0]


def _pack_small(arrs):
    flat = jnp.concatenate([a.reshape(-1).astype(F32) for a in arrs])
    n = flat.shape[0]
    rows = -(-n // LANES)
    rows = -(-rows // 8) * 8
    return jnp.pad(flat, (0, rows * LANES - n)).reshape(rows, LANES)


def _unpack_small(p, shapes):
    lead = p.shape[:-2]
    flat = p.reshape(lead + (-1,))
    out, off = [], 0
    for sh in shapes:
        n = 1
        for d in sh:
            n *= d
        out.append(flat[..., off:off + n].reshape(lead + tuple(sh)))
        off += n
    return out


def kernel(x, c, w_ada, b_ada, norm1_g, w_in, dn_conv_w, dn_a_log, dn_dt_bias, dn_norm_g, dn_w_o, cf_conv_w, cf_ln_g, cf_ln_b, cf_w_o, w_out, norm2_g, ffn_w_up, ffn_conv_w, ffn_w_down, final_norm_g, loss_target, m_w_ada, m_b_ada, m_norm1_g, m_w_in, m_dn_conv_w, m_dn_a_log, m_dn_dt_bias, m_dn_norm_g, m_dn_w_o, m_cf_conv_w, m_cf_ln_g, m_cf_ln_b, m_cf_w_o, m_w_out, m_norm2_g, m_ffn_w_up, m_ffn_conv_w, m_ffn_w_down, m_final_norm_g, v_w_ada, v_b_ada, v_norm1_g, v_w_in, v_dn_conv_w, v_dn_a_log, v_dn_dt_bias, v_dn_norm_g, v_dn_w_o, v_cf_conv_w, v_cf_ln_g, v_cf_ln_b, v_cf_w_o, v_w_out, v_norm2_g, v_ffn_w_up, v_ffn_conv_w, v_ffn_w_down, v_final_norm_g):
    xi, yi, ci = _coords()
    chip = 2 * xi + yi
    me = 4 * xi + 2 * yi + ci
    core = jnp.reshape(ci, (1,)).astype(jnp.int32)

    S, D = x.shape[1], x.shape[2]
    NH = dn_a_log.shape[1]
    DH = dn_norm_g.shape[1]
    DNW = NH * DH
    CFW = cf_ln_g.shape[1]
    FFN = ffn_w_down.shape[1] * 4
    KDN, KCF, KFF = dn_conv_w.shape[1], cf_conv_w.shape[1], ffn_conv_w.shape[1]
    NIN = w_in.shape[2] * 4
    assert NIN == 4 * DNW + 2 * NH + 2 * CFW + 2 * D and DH == LANES and 2 * NH <= LANES
    x2, tgt = x[0], loss_target[0]

    sm_shapes = [(D,), (KDN, 3 * DNW // 4), (KCF, CFW // 4), (KFF, FFN // 4)]
    g1 = _all_gather_small(_pack_small([c[0], dn_conv_w[0], cf_conv_w[0], ffn_conv_w[0]]))
    c_all, dcw_s, ccw_s, fcw_s = _unpack_small(g1, sm_shapes)

    def chips_cols(t):
        t = t[0::2]
        return jnp.transpose(t, (1, 0, 2)).reshape(t.shape[1], -1)

    dn_cw, cf_cw, ff_cw = chips_cols(dcw_s), chips_cols(ccw_s), chips_cols(fcw_s)

    slot_chip = jnp.reshape(chip, (1,)).astype(jnp.int32)
    big = [w_in[0], dn_w_o[0], cf_w_o[0], w_out[0], ffn_w_up[0], ffn_w_down[0]]
    names_big = ["w_in", "dn_w_o", "cf_w_o", "w_out", "ffn_w_up", "ffn_w_down"]
    bufs = [_cast_into_slot(w, slot_chip, "cast_" + nm) for w, nm in zip(big, names_big)]
    in_state, in_token = _split_start("ag_in_start", [], bufs[:1], c, 3, _gather_copies)
    ag_state, ag_token = _split_start("ag_rest_start", [], bufs[1:], in_token, 3 * len(bufs[1:]), _gather_copies)

    cb_cf = _div_tile(CFW, 256, LANES)
    o_b = 4 * DNW
    o_glu = o_b + 2 * NH
    o_ga = o_glu + 2 * CFW
    NA = NIN - 2 * NH
    a_z, a_ga, a_gb, a_glu, a_ba = 3 * DNW, 4 * DNW, 4 * DNW + D, 4 * DNW + 2 * D, NA

    segs = [(0, 0, o_b), (o_b, NA, 2 * NH), (o_ga, a_ga, 2 * D)]
    for t in range(CFW // cb_cf):
        segs += [(o_glu + t * cb_cf, a_glu + 2 * t * cb_cf, cb_cf),
                 (o_glu + CFW + t * cb_cf, a_glu + (2 * t + 1) * cb_cf, cb_cf)]
    CS = NIN // 4

    def shard_slices(lo, n):
        out = []
        while n > 0:
            j, off = lo // CS, lo % CS
            m = min(n, CS - off)
            out.append(w_in_g[j][:, off:off + m])
            lo, n = lo + m, n - m
        return out

    CA = w_ada.shape[2]
    b_sh = lax.dynamic_slice(b_ada, (0, chip * CA), (1, CA))
    tn_a = _div_tile(CA, 512, LANES)

    def mod_fn(cc, w, b):
        return (_mm(_silu(cc), w, NN) + b,)

    mod_sh = _ew(mod_fn, "ada_mod", (CA // tn_a,), [c_all, w_ada[0], b_sh],
                 [_full((8, D)), pl.BlockSpec((D, tn_a), lambda j: (0, j)), pl.BlockSpec((1, tn_a), lambda j: (0, j))],
                 [((8, CA), F32, pl.BlockSpec((8, tn_a), lambda j: (0, j)))])[0]
    g2 = _all_gather_small(_pack_small([mod_sh]))
    mod_all = _unpack_small(g2, [(8, CA)])[0][0::2]
    mod_all = jnp.transpose(mod_all, (1, 0, 2)).reshape(8, 4 * CA)
    mod_me = lax.dynamic_slice(mod_all, (me, 0), (1, 6 * D))
    sh1, sc1, gt1, sh2, sc2, gt2 = [mod_me[:, i * D:(i + 1) * D] for i in range(6)]

    bs = _div_tile(S, 128, 8)
    nb = S // bs
    vecD = _full((1, D))
    rowD = _row(bs, D)

    hn1 = _ew(lambda a, g, sc, sh: (_f_normmod(a, g, sc, sh),), "norm1_fwd", (nb,),
              [x2, norm1_g, sc1, sh1], [rowD, vecD, vecD, vecD], [((S, D), BF16, rowD)], after=[ag_token])[0]
    _, in_landed = _split_wait("ag_in_wait", in_state, hn1, _gather_wait_copies)
    (w_in_g,) = _pair_forward(in_landed, "ag_in_pair")
    w_aug = jnp.concatenate([p for o, a, n in sorted(segs, key=lambda s: s[1]) for p in shard_slices(o, n)]
                            + [jnp.zeros((D, LANES - 2 * NH), BF16)], axis=1)
    proj = _matmul(hn1, w_aug, "nn", F32, "mm_in")

    def dn_post(j, cv):
        s = _silu(cv)
        nrm = s * lax.rsqrt(jnp.sum(s * s, axis=-1, keepdims=True) + EPS)
        fq = (j < NH).astype(F32)
        fk = (j < 2 * NH).astype(F32)
        scale = fq * (DH ** -0.5) + (1.0 - fq)
        return fk * (nrm * scale) + (1.0 - fk) * s

    def ident(a):
        return a

    def colS(w, off=0):
        return pl.BlockSpec((S, w), lambda j, off=off: (0, j + off))

    def wS(kw, w):
        return pl.BlockSpec((kw, w), lambda j: (0, j))

    qkv_spec = pl.BlockSpec((None, None, S, DH), lambda j: (j // NH, j % NH, 0, 0))
    dn_args = dict(kw=KDN, ncol=3 * NH, ins=[proj], in_specs=[colS(DH)], extras=[], extra_specs=[],
                   w=dn_cw, w_spec=wS(KDN, DH), pre=ident, post=dn_post)
    qkvn = _conv_fwd_call("dn_conv_fwd", out_shape=jax.ShapeDtypeStruct((3, NH, S, DH), F32), out_spec=qkv_spec,
                          **dn_args)

    alp = jnp.pad(dn_a_log, ((0, 0), (NH, LANES - 2 * NH)))
    dtb = jnp.pad(dn_dt_bias, ((0, 0), (NH, LANES - 2 * NH)))
    vecL = _full((1, LANES))
    ba_spec = _row(bs, LANES, a_ba // LANES)
    rowL = _row(bs, LANES)
    gate_fn = functools.partial(_f_dn_gate, NH)
    def gate_fwd(a, p, q):
        val = gate_fn(a, p, q)
        lane = lax.broadcasted_iota(jnp.int32, val.shape, 1)

        def spread(col):
            return jnp.sum(jnp.where(lane == col, val, 0.0), axis=-1, keepdims=True) + jnp.zeros(val.shape, F32)

        return tuple(spread(h) for h in range(NH)), tuple(spread(NH + h) for h in range(NH))

    hrow = pl.BlockSpec((NH, bs, DH), lambda i: (0, i, 0))
    bb_b, gb_b = _ew(gate_fwd, "dn_gate_fwd", (nb,), [proj, alp, dtb], [ba_spec, vecL, vecL],
                     [((NH, S, DH), F32, hrow), ((NH, S, DH), F32, hrow)])
    o_dn, states = _dn_fwd_call(qkvn, gb_b, bb_b)

    bsh = _div_tile(S, 512, 8)
    nbh = S // bsh
    o_spec = pl.BlockSpec((None, bsh, DH), lambda i, h: (h, i, 0))
    z_spec = pl.BlockSpec((bsh, DH), lambda i, h: (i, a_z // DH + h))
    oh_spec = pl.BlockSpec((bsh, DH), lambda i, h: (i, h))
    ng_spec = pl.BlockSpec((1, DH), lambda i, h: (0, 0))
    on = _ew(lambda o, z, g: (_f_dn_post(o, z, g),), "dn_post_fwd", (nbh, NH), [o_dn, proj, dn_norm_g],
             [o_spec, z_spec, ng_spec], [((S, DNW), BF16, oh_spec)])[0]
    _, landed = _split_wait("ag_rest_wait", ag_state, on, _gather_wait_copies)
    w_do_f, w_co_f, w_out_g, w_up_f, w_dn_g = _pair_forward(landed, "ag_rest_pair")
    w_out_f = w_out_g.reshape(-1, w_out_g.shape[2])
    w_dn_f = w_dn_g.reshape(-1, w_dn_g.shape[2])
    br_a = _matmul(on, w_do_f, "nn", F32, "mm_dn_o")

    def glu_pre(val, gl):
        return val * _sigmoid(gl)

    def glu_spec(t):
        return pl.BlockSpec((S, cb_cf), lambda j, t=t: (0, a_glu // cb_cf + 2 * j + t))

    cf_args = dict(kw=KCF, ncol=CFW // cb_cf, ins=[proj, proj], in_specs=[glu_spec(0), glu_spec(1)],
                   extras=[], extra_specs=[], w=cf_cw, w_spec=wS(KCF, cb_cf), pre=glu_pre, post=lambda j, cv: cv)
    uc = _conv_fwd_call("cf_conv_fwd", out_shape=jax.ShapeDtypeStruct((S, CFW), F32), out_spec=colS(cb_cf), **cf_args)
    rowC = _row(bs, CFW)
    vecC = _full((1, CFW))
    ub = _ew(lambda u, g, b: (_f_cf_ln(u, g, b),), "cf_ln_fwd", (nb,), [uc, cf_ln_g, cf_ln_b], [rowC, vecC, vecC],
             [((S, CFW), BF16, rowC)])[0]
    br_b = _matmul(ub, w_co_f, "nn", F32, "mm_cf_o")

    ga_spec, gb_spec = _row(bs, D, a_ga // D), _row(bs, D, a_gb // D)
    merged = _ew(lambda a, b, ga, gb: (_f_merge(a, b, ga, gb),), "merge_fwd", (nb,), [br_a, br_b, proj, proj],
                 [rowD, rowD, ga_spec, gb_spec], [((S, D), BF16, rowD)])[0]
    mix = _matmul(merged, w_out_f, "nn", F32, "mm_out")

    x1, hn2 = _ew(_f_res_normmod, "norm2_fwd", (nb,), [x2, mix, gt1, norm2_g, sc2, sh2],
                  [rowD, rowD, vecD, vecD, vecD, vecD], [((S, D), F32, rowD), ((S, D), BF16, rowD)])
    up_all = _matmul(hn2, w_up_f, "nn", F32, "mm_up")

    cb_ff = _div_tile(FFN, 256, LANES)
    ff_args = dict(kw=KFF, ncol=FFN // cb_ff, ins=[up_all], in_specs=[colS(cb_ff)], extras=[up_all],
                   extra_specs=[colS(cb_ff, FFN // cb_ff)], w=ff_cw, w_spec=wS(KFF, cb_ff),
                   pre=ident, post=lambda j, cv, up: _silu(cv) * up)
    hff = _conv_fwd_call("ffn_conv_fwd", out_shape=jax.ShapeDtypeStruct((S, FFN), BF16), out_spec=colS(cb_ff), **ff_args)
    ffo = _matmul(hff, w_dn_f, "nn", F32, "mm_down")

    gf2 = final_norm_g.reshape(1, D)

    def loss_bwd(a, f, gt, gf, t):
        val, vjp = jax.vjp(_f_loss, a, f, gt, gf, t)
        da, df, dgt, dgf, _ = vjp(jnp.ones((), F32))
        return da, df, dgt, dgf, jnp.zeros((1, LANES), F32) + val

    dx1_l, dffo, dgt2, dgf, loss_v = _ew(
        loss_bwd, "loss_bwd", (nb,), [x1, ffo, gt2, gf2, tgt], [rowD, rowD, vecD, vecD, rowD],
        [((S, D), F32, rowD), ((S, D), BF16, rowD), ((1, D), F32, vecD), ((1, D), F32, vecD),
         ((1, LANES), F32, vecL)], acc=(2, 3, 4))

    dhff = _matmul(dffo, w_dn_f, "nt", F32, "mm_down_dx")
    g_w_dn = _matmul(hff, dffo, "tn", F32, "mm_down_dw")

    slots = jnp.stack([chip, ci]).astype(jnp.int32)
    rs_groups = []

    def rs_begin(parts, nms, tag):
        rbs = _pair_swap_half(parts, "rs_pair_" + tag)
        q16 = [_pair_add(p, rb, core, "rs_pair_add_" + nm) for p, rb, nm in zip(parts, rbs, nms)]
        state, token = _chip_scatter_start(q16, "rs_chips_start_" + tag)
        rs_groups.append((state, nms, tag))
        return token

    tok_a = rs_begin([g_w_dn.reshape(4, FFN // 4, D)], ["ffn_w_down"], "a")

    d_upall, g_ffcw = _conv_bwd_call(
        "ffn_conv_bwd", dout=dhff, dout_spec=colS(cb_ff),
        dio_shapes=[jax.ShapeDtypeStruct((2, S, FFN), BF16)],
        dio_specs=[pl.BlockSpec((2, S, cb_ff), lambda j: (0, 0, j))], dio_pack=lambda dg, du: [(dg, du)],
        dw_shape=jax.ShapeDtypeStruct((KFF, FFN), F32), dw_spec=wS(KFF, cb_ff), after=[tok_a], **ff_args)
    dhn2 = _matmul(d_upall, w_up_f, "nt", F32, "mm_up_dx")
    g_w_up = _matmul(hn2, d_upall, "tn", F32, "mm_up_dw", out_groups=4)
    tok_b = rs_begin([g_w_up], ["ffn_w_up"], "b")

    def res2_bwd(a, mx, gt, g, sc, sh, dx1, dhn):
        _, vjp = jax.vjp(_f_res_normmod, a, mx, gt, g, sc, sh)
        return vjp((dx1, dhn))

    dx_r, dmix, dgt1, dg2, dsc2, dsh2 = _ew(
        res2_bwd, "norm2_bwd", (nb,), [x2, mix, gt1, norm2_g, sc2, sh2, dx1_l, dhn2],
        [rowD, rowD, vecD, vecD, vecD, vecD, rowD, rowD],
        [((S, D), F32, rowD), ((S, D), BF16, rowD)] + [((1, D), F32, vecD)] * 4, acc=(2, 3, 4, 5), after=[tok_b])

    dmerged = _matmul(dmix, w_out_f, "nt", F32, "mm_out_dx")
    g_w_out = _matmul(merged, dmix, "tn", F32, "mm_out_dw")

    def merge_bwd(a, b, ga, gb, dm):
        _, vjp = jax.vjp(_f_merge, a, b, ga, gb)
        da, db, dga, dgb = vjp(dm)
        return da, db, jnp.concatenate([dga, dgb], axis=1)

    d_bra, d_brb, dproj = _ew(merge_bwd, "merge_bwd", (nb,), [br_a, br_b, proj, proj, dmerged],
                              [rowD, rowD, ga_spec, gb_spec, rowD],
                              [((S, D), BF16, rowD), ((S, D), BF16, rowD),
                               ((S, NA + LANES), BF16, _row(bs, 2 * D, a_ga // (2 * D)))])

    d_on = _matmul(d_bra, w_do_f, "nt", F32, "mm_dn_o_dx")
    g_w_do = _matmul(on, d_bra, "tn", F32, "mm_dn_o_dw", out_groups=4)
    d_ub = _matmul(d_brb, w_co_f, "nt", F32, "mm_cf_o_dx")
    g_w_co = _matmul(ub, d_brb, "tn", F32, "mm_cf_o_dw", out_groups=4)
    tok_c = rs_begin([g_w_do, g_w_co, g_w_out.reshape(4, D // 4, D)], ["dn_w_o", "cf_w_o", "w_out"], "c")

    def cf_ln_bwd(u, g, b, du):
        _, vjp = jax.vjp(_f_cf_ln, u, g, b)
        return vjp(du)

    d_uc, g_cflg, g_cflb = _ew(cf_ln_bwd, "cf_ln_bwd", (nb,), [uc, cf_ln_g, cf_ln_b, d_ub], [rowC, vecC, vecC, rowC],
                               [((S, CFW), F32, rowC), ((1, CFW), F32, vecC), ((1, CFW), F32, vecC)], acc=(1, 2),
                               after=[tok_c])
    dproj_sds = jax.ShapeDtypeStruct((S, NA + LANES), BF16)
    dproj, g_cfcw = _conv_bwd_call(
        "cf_conv_bwd", dout=d_uc, dout_spec=colS(cb_cf), dio_shapes=[dproj_sds],
        dio_specs=[colS(2 * cb_cf, a_glu // (2 * cb_cf))], dio_pack=lambda dv, dg: [jnp.concatenate([dv, dg], axis=1)],
        dw_shape=jax.ShapeDtypeStruct((KCF, CFW), F32), dw_spec=wS(KCF, cb_cf), alias=dproj, **cf_args)

    def dn_post_bwd(o, z, g, d):
        _, vjp = jax.vjp(_f_dn_post, o, z, g)
        return vjp(d)

    d_o, dproj, g_dnng = _ew(dn_post_bwd, "dn_post_bwd", (nbh, NH), [o_dn, proj, dn_norm_g, d_on],
                             [o_spec, z_spec, ng_spec, oh_spec],
                             [((NH, S, DH), F32, o_spec), ((S, NA + LANES), BF16, z_spec), ((1, DH), F32, ng_spec)],
                             acc=(2,), alias=(dproj, 1))

    dqkvn, dgb_b, dbb_b = _dn_bwd_call(qkvn, gb_b, bb_b, states, d_o)
    dproj, g_dncw = _conv_bwd_call(
        "dn_conv_bwd", dout=dqkvn, dout_spec=qkv_spec, dio_shapes=[dproj_sds], dio_specs=[colS(DH)],
        dio_pack=lambda d: [d], dw_shape=jax.ShapeDtypeStruct((KDN, 3 * DNW), F32), dw_spec=wS(KDN, DH),
        alias=dproj, **dn_args)

    def gate_bwd(a, p, q, db, dg):
        lane = lax.broadcasted_iota(jnp.int32, a.shape, 1)
        d = jnp.zeros(a.shape, F32)
        for h in range(NH):
            d = d + jnp.where(lane == h, db[h], 0.0) + jnp.where(lane == NH + h, dg[h], 0.0)
        _, vjp = jax.vjp(gate_fn, a, p, q)
        return vjp(d)

    dproj, g_alp, g_dtb = _ew(gate_bwd, "dn_gate_bwd", (nb,), [proj, alp, dtb, dbb_b, dgb_b],
                              [ba_spec, vecL, vecL, hrow, hrow],
                              [((S, NA + LANES), BF16, ba_spec), ((1, LANES), F32, vecL), ((1, LANES), F32, vecL)],
                              acc=(1, 2), alias=(dproj, 0))

    g_w_aug = _matmul(hn1, dproj, "tn", F32, "mm_in_dw")
    def aug_slices(lo, n):
        out = []
        for o, a, m in sorted(segs):
            s, e = max(lo, o), min(lo + n, o + m)
            if s < e:
                out.append(g_w_aug[:, a + s - o:a + e - o])
        return out

    g_w_in = jnp.stack([jnp.concatenate(aug_slices(j * CS, CS), axis=1) for j in range(4)])
    tok_d = rs_begin([g_w_in], ["w_in"], "d")
    dhn1 = _matmul(dproj, w_aug, "nt", F32, "mm_in_dx", after=[tok_d])

    def norm1_bwd(a, g, sc, sh, dhn, dxr):
        _, vjp = jax.vjp(_f_normmod, a, g, sc, sh)
        da, dg, dsc, dsh = vjp(dhn)
        return da + dxr, dg, dsc, dsh

    grad_x, dg1, dsc1, dsh1 = _ew(norm1_bwd, "norm1_bwd", (nb,), [x2, norm1_g, sc1, sh1, dhn1, dx_r],
                                  [rowD, vecD, vecD, vecD, rowD, rowD],
                                  [((S, D), F32, rowD)] + [((1, D), F32, vecD)] * 3, acc=(1, 2, 3))

    names = ["w_ada", "b_ada", "norm1_g", "w_in", "dn_conv_w", "dn_a_log", "dn_dt_bias", "dn_norm_g", "dn_w_o",
             "cf_conv_w", "cf_ln_g", "cf_ln_b", "cf_w_o", "w_out", "norm2_g", "ffn_w_up", "ffn_conv_w", "ffn_w_down",
             "final_norm_g"]
    wts = dict(zip(names, [w_ada, b_ada, norm1_g, w_in, dn_conv_w, dn_a_log, dn_dt_bias, dn_norm_g, dn_w_o, cf_conv_w,
                           cf_ln_g, cf_ln_b, cf_w_o, w_out, norm2_g, ffn_w_up, ffn_conv_w, ffn_w_down, final_norm_g]))
    ms = dict(zip(names, [m_w_ada, m_b_ada, m_norm1_g, m_w_in, m_dn_conv_w, m_dn_a_log, m_dn_dt_bias, m_dn_norm_g,
                          m_dn_w_o, m_cf_conv_w, m_cf_ln_g, m_cf_ln_b, m_cf_w_o, m_w_out, m_norm2_g, m_ffn_w_up,
                          m_ffn_conv_w, m_ffn_w_down, m_final_norm_g]))
    vs = dict(zip(names, [v_w_ada, v_b_ada, v_norm1_g, v_w_in, v_dn_conv_w, v_dn_a_log, v_dn_dt_bias, v_dn_norm_g,
                          v_dn_w_o, v_cf_conv_w, v_cf_ln_g, v_cf_ln_b, v_cf_w_o, v_w_out, v_norm2_g, v_ffn_w_up,
                          v_ffn_conv_w, v_ffn_w_down, v_final_norm_g]))
    grads, delta, new_m, new_v = {}, {}, {}, {}

    def adam_large(n, g):
        grads[n] = g.reshape(wts[n].shape)
        d_, m_, v_ = _adamw(wts[n][0], grads[n][0], ms[n][0], vs[n][0], "adamw_" + n)
        delta[n], new_m[n], new_v[n] = d_[None], m_[None], v_[None]

    def rs_finish(groups, after, tag):
        hs, nms_all = [], []
        for state, nms, t in groups:
            q16, r3s = _chip_scatter_wait(state, after, "rs_chips_wait_" + t)
            for q, r3, nm in zip(q16, r3s, nms):
                hs.append(_chip_sum(q, r3, slots, "rs_chip_sum_" + nm))
                nms_all.append(nm)
        for nm, g in zip(nms_all, _pair_gather(hs, "rs_pair_gather_" + tag)):
            adam_large(nm, g)

    rs_finish(rs_groups[:3], grad_x, "abc")
    rs_finish(rs_groups[3:], delta["ffn_w_up"], "d")

    dmod = jnp.concatenate([dsh1, dsc1, dgt1, dsh2, dsc2, dgt2], axis=1)
    sm2 = [dmod, dg1, dg2, dgf, g_alp, g_dtb, g_dnng, g_cflg, g_cflb, g_dncw, g_cfcw, g_ffcw]
    sm2_shapes = [tuple(a.shape) for a in sm2]
    g3 = _all_gather_small(_pack_small(sm2))
    ssum = _sum_slots(g3, "small_sum")
    dmod_all = _unpack_small(g3, sm2_shapes[:1])[0].reshape(8, 6 * D)
    (g_b_ada, gs_n1, gs_n2, gs_fn, gs_alp, gs_dtb, gs_dnng, gs_cflg, gs_cflb, gs_dncw, gs_cfcw,
     gs_ffcw) = _unpack_small(ssum, sm2_shapes)
    gs_alog, gs_dtbias = gs_alp[:, NH:2 * NH], gs_dtb[:, NH:2 * NH]
    gs_dncw = lax.dynamic_slice(gs_dncw, (0, chip * (3 * DNW // 4)), (KDN, 3 * DNW // 4))
    gs_cfcw = lax.dynamic_slice(gs_cfcw, (0, chip * (CFW // 4)), (KCF, CFW // 4))
    gs_ffcw = lax.dynamic_slice(gs_ffcw, (0, chip * (FFN // 4)), (KFF, FFN // 4))

    dmod_sh = lax.dynamic_slice(dmod_all, (0, chip * CA), (8, CA))

    def wada_fn(cc, dm):
        return (_mm(_silu(cc), dm, TN),)

    g_w_ada = _ew(wada_fn, "ada_dw", (CA // tn_a,), [c_all, dmod_sh],
                  [_full((8, D)), pl.BlockSpec((8, tn_a), lambda j: (0, j))],
                  [((D, CA), F32, pl.BlockSpec((D, tn_a), lambda j: (0, j)))])[0]

    loss = lax.psum(loss_v[0, 0], ("x", "y", "c"))

    adam_large("w_ada", g_w_ada)
    small_grads = {"b_ada": g_b_ada, "norm1_g": gs_n1, "dn_conv_w": gs_dncw, "dn_a_log": gs_alog,
                   "dn_dt_bias": gs_dtbias, "dn_norm_g": gs_dnng, "cf_conv_w": gs_cfcw, "cf_ln_g": gs_cflg,
                   "cf_ln_b": gs_cflb, "norm2_g": gs_n2, "ffn_conv_w": gs_ffcw, "final_norm_g": gs_fn}
    small = [n for n in names if n in small_grads]
    for n in small:
        grads[n] = small_grads[n].reshape(wts[n].shape)
    sm_sh = [tuple(wts[n].shape) for n in small]
    d_, m_, v_ = _adamw(_pack_small([wts[n] for n in small]), _pack_small([grads[n] for n in small]),
                        _pack_small([ms[n] for n in small]), _pack_small([vs[n] for n in small]), "adamw_small")
    for n, a, b_, c_ in zip(small, _unpack_small(d_, sm_sh), _unpack_small(m_, sm_sh), _unpack_small(v_, sm_sh)):
        delta[n], new_m[n], new_v[n] = a, b_, c_

    return (loss, grad_x[None], *[grads[n] for n in names], *[delta[n] for n in names],
            *[new_m[n] for n in names], *[new_v[n] for n in names])
```

```python
import functools

import jax
import jax.numpy as jnp
from jax import lax
from jax.experimental import pallas as pl
from jax.experimental.pallas import tpu as pltpu

F32 = jnp.float32
BF16 = jnp.bfloat16
EPS = 1e-6
LANES = 128
VMEM_LIMIT = 48 * 1024 * 1024
DN_CHUNK = 128
ADAM_LR, ADAM_B1, ADAM_B2, ADAM_EPS, ADAM_WD, ADAM_STEP = 0.001, 0.9, 0.999, 1e-08, 0.01, 10
MESH = pl.DeviceIdType.MESH

NN = (((1,), (0,)), ((), ()))
NT = (((1,), (1,)), ((), ()))
TN = (((0,), (0,)), ((), ()))
_DIMS = {"nn": NN, "nt": NT, "tn": TN}


def _mm(a, b, dims):
    return lax.dot_general(a.astype(BF16), b.astype(BF16), dims, preferred_element_type=F32)


def _mmx(a, b, dims):
    return lax.dot_general(a, b, dims, precision=lax.Precision.HIGHEST, preferred_element_type=F32)


def _div_tile(n, target, mult):
    best = None
    t = mult
    while t <= min(n, target):
        if n % t == 0:
            best = t
        t += mult
    return best if best is not None else n


def _params(sem=None):
    kw = dict(vmem_limit_bytes=VMEM_LIMIT)
    if sem is not None:
        kw["dimension_semantics"] = sem
    return pltpu.CompilerParams(**kw)


def _sigmoid(x):
    return jax.nn.sigmoid(x)


def _silu(x):
    return x * jax.nn.sigmoid(x)


def _softplus(x):
    return jnp.maximum(x, 0.0) + jnp.log(1.0 + jnp.exp(-jnp.abs(x)))


def _view(arr):
    if arr.ndim == 3:
        return arr.shape[1], arr.shape[0] * arr.shape[2], arr.shape[0]
    return arr.shape[0], arr.shape[1], 1


def _tile_spec(groups, cols, tr, tc, rsel, csel):
    if groups > 1:
        per = cols // groups // tc
        return pl.BlockSpec((None, tr, tc), lambda i, j, k: (csel(i, j, k) // per, rsel(i, j, k), csel(i, j, k) % per))
    return pl.BlockSpec((tr, tc), lambda i, j, k: (rsel(i, j, k), csel(i, j, k)))


def _matmul(a, b, mode, out_dtype, name, out_groups=1, after=()):
    ar, ac, ag = _view(a)
    br, bc, bg = _view(b)
    if mode == "nn":
        M, K, N = ar, ac, bc
        kdiv, mdiv, ndiv = ac // ag, M, min(bc // bg, N // out_groups)
    elif mode == "nt":
        M, K, N = ar, ac, br
        kdiv, mdiv, ndiv = min(ac // ag, bc // bg), M, N // out_groups
    else:
        K, M, N = ar, ac, bc
        kdiv, mdiv, ndiv = K, ac // ag, min(bc // bg, N // out_groups)
    tm = _div_tile(mdiv, 1024, LANES)
    tn = _div_tile(ndiv, 1536, LANES)
    tk = _div_tile(kdiv, 2048, LANES)
    nk = K // tk
    dims = _DIMS[mode]
    si, sj, sk = (lambda i, j, k: i), (lambda i, j, k: j), (lambda i, j, k: k)
    a_spec = {"nn": _tile_spec(ag, ac, tm, tk, si, sk), "nt": _tile_spec(ag, ac, tm, tk, si, sk),
              "tn": _tile_spec(ag, ac, tk, tm, sk, si)}[mode]
    b_spec = {"nn": _tile_spec(bg, bc, tk, tn, sk, sj), "nt": _tile_spec(bg, bc, tn, tk, sj, sk),
              "tn": _tile_spec(bg, bc, tk, tn, sk, sj)}[mode]
    out_shape = (M, N) if out_groups == 1 else (out_groups, M, N // out_groups)

    n_after = len(after)

    def body(*refs):
        a_ref, b_ref, o_ref = refs[0], refs[1], refs[2 + n_after]
        if nk == 1:
            o_ref[...] = lax.dot_general(a_ref[...], b_ref[...], dims, preferred_element_type=F32).astype(o_ref.dtype)
            return
        acc_ref = refs[3 + n_after]
        k = pl.program_id(2)

        @pl.when(k == 0)
        def _():
            acc_ref[...] = jnp.zeros_like(acc_ref)

        acc_ref[...] += lax.dot_general(a_ref[...], b_ref[...], dims, preferred_element_type=F32)

        @pl.when(k == nk - 1)
        def _():
            o_ref[...] = acc_ref[...].astype(o_ref.dtype)

    return pl.pallas_call(
        body, name=name, grid=(M // tm, N // tn, nk),
        in_specs=[a_spec, b_spec] + [pl.BlockSpec(memory_space=pl.ANY)] * n_after,
        out_specs=_tile_spec(out_groups, N, tm, tn, si, sj),
        out_shape=jax.ShapeDtypeStruct(out_shape, out_dtype),
        scratch_shapes=[pltpu.VMEM((tm, tn), F32)] if nk > 1 else [],
        compiler_params=_params(("parallel", "parallel", "arbitrary")),
    )(a, b, *after)


def _ew(fn, name, grid, ins, in_specs, outs, acc=(), alias=None, after=()):
    n_in = len(ins)
    n_ax = len(grid)
    extra, aliases = list(after), {}
    if alias is not None:
        extra, aliases = extra + [alias[0]], {n_in + len(after): alias[1]}
    extra_specs = [pl.BlockSpec(memory_space=pl.ANY)] * len(extra)

    def body(*refs):
        in_refs, out_refs = refs[:n_in], refs[n_in + len(extra):]
        ids = [pl.program_id(a) for a in range(n_ax)]
        res = fn(*[r[...] for r in in_refs])
        first = ids[0] == 0
        for t in ids[1:]:
            first = jnp.logical_and(first, t == 0)
        for idx, (r, val) in enumerate(zip(out_refs, res)):
            if idx in acc:
                @pl.when(first)
                def _(r=r, val=val):
                    r[...] = val.astype(r.dtype)

                @pl.when(jnp.logical_not(first))
                def _(r=r, val=val):
                    r[...] += val.astype(r.dtype)
            elif isinstance(val, tuple):
                for t, part in enumerate(val):
                    r[t] = part.astype(r.dtype)
            else:
                r[...] = val.astype(r.dtype)

    return pl.pallas_call(
        body, name=name, grid=grid, in_specs=list(in_specs) + extra_specs,
        out_specs=[o[2] for o in outs],
        out_shape=[jax.ShapeDtypeStruct(o[0], o[1]) for o in outs],
        input_output_aliases=aliases,
        compiler_params=_params(("arbitrary",) * n_ax),
    )(*ins, *extra)


def _ew_slot(fn, name, grid, slots, ins, in_specs, out_shape, out_dtype, out_spec):
    def body(s_ref, *refs):
        refs[-1][...] = fn(*[r[...] for r in refs[:-1]]).astype(refs[-1].dtype)

    return pl.pallas_call(
        body, name=name,
        grid_spec=pltpu.PrefetchScalarGridSpec(num_scalar_prefetch=1, grid=grid, in_specs=list(in_specs),
                                               out_specs=out_spec),
        out_shape=jax.ShapeDtypeStruct(out_shape, out_dtype),
        compiler_params=_params(("arbitrary",) * len(grid)),
    )(slots, *ins)


def _row(bs, w, col=0):
    return pl.BlockSpec((bs, w), lambda i, col=col: (i, col))


def _full(shape):
    nd = len(shape)
    return pl.BlockSpec(tuple(shape), lambda *_: (0,) * nd)


def _rms(x, g):
    return x * lax.rsqrt(jnp.mean(x * x, axis=-1, keepdims=True) + EPS) * g


def _f_normmod(x, g, sc, sh):
    return _rms(x, g) * (1.0 + sc) + sh


def _f_res_normmod(x, mix, gt, g, sc, sh):
    x1 = x + gt * mix
    return x1, _f_normmod(x1, g, sc, sh)


def _f_loss(x1, f, gt, gf, tgt):
    y = _rms(x1 + gt * f, gf)
    return 0.5 * jnp.sum(jnp.mean(jnp.square(y - tgt), axis=-1))


def _f_dn_gate(nh, ba, alp, dtb):
    lane = lax.broadcasted_iota(jnp.int32, ba.shape, 1)
    m = (lane < nh).astype(F32)
    beta = _sigmoid(ba)
    g = -jnp.exp(alp) * _softplus(ba + dtb)
    return m * beta + (1.0 - m) * g


def _f_dn_post(o, z, g):
    return o * lax.rsqrt(jnp.mean(o * o, axis=-1, keepdims=True) + EPS) * g * _silu(z)


def _f_cf_ln(u, g, b):
    mu = jnp.mean(u, axis=-1, keepdims=True)
    xc = u - mu
    y = xc * lax.rsqrt(jnp.mean(xc * xc, axis=-1, keepdims=True) + EPS)
    return _silu(y * g + b)


def _f_merge(a, b, ga, gb):
    return _sigmoid(ga) * a + _sigmoid(gb) * b


def _shift_down(u, d, rows):
    if d == 0:
        return u
    return jnp.where(rows >= d, pltpu.roll(u, d, 0), 0.0)


def _shift_up(u, d, rows):
    if d == 0:
        return u
    s = u.shape[0]
    return jnp.where(rows < s - d, pltpu.roll(u, s - d, 0), 0.0)


def _conv(u, w_ref, kw, rows):
    acc = None
    for k in range(kw):
        t = w_ref[k:k + 1, :] * _shift_down(u, kw - 1 - k, rows)
        acc = t if acc is None else acc + t
    return acc


def _conv_t(dc, w_ref, kw, rows):
    acc = None
    for k in range(kw):
        t = w_ref[k:k + 1, :] * _shift_up(dc, kw - 1 - k, rows)
        acc = t if acc is None else acc + t
    return acc


def _conv_fwd_call(name, kw, ncol, ins, in_specs, extras, extra_specs, w, w_spec, pre, post, out_shape, out_spec):
    n_in, n_ex = len(ins), len(extras)

    def body(*refs):
        in_refs, ex_refs = refs[:n_in], refs[n_in:n_in + n_ex]
        w_ref, out_ref = refs[n_in + n_ex], refs[n_in + n_ex + 1]
        j = pl.program_id(0)
        u = pre(*[r[...] for r in in_refs])
        rows = lax.broadcasted_iota(jnp.int32, u.shape, 0)
        cv = _conv(u, w_ref, kw, rows)
        out_ref[...] = post(j, cv, *[r[...] for r in ex_refs]).astype(out_ref.dtype)

    return pl.pallas_call(
        body, name=name, grid=(ncol,), in_specs=list(in_specs) + list(extra_specs) + [w_spec],
        out_specs=out_spec, out_shape=out_shape, compiler_params=_params(("arbitrary",)),
    )(*ins, *extras, w)


def _conv_bwd_call(name, kw, ncol, ins, in_specs, extras, extra_specs, w, w_spec, pre, post, dout, dout_spec,
                   dio_shapes, dio_specs, dio_pack, dw_shape, dw_spec, alias=None, after=()):
    n_in, n_ex, n_io = len(ins), len(extras), len(dio_shapes)
    al, aliases = list(after), {}
    if alias is not None:
        al, aliases = al + [alias], {n_in + n_ex + 2 + len(after): 0}
    al_specs = [pl.BlockSpec(memory_space=pl.ANY)] * len(al)

    def body(*refs):
        in_refs, ex_refs = refs[:n_in], refs[n_in:n_in + n_ex]
        w_ref, dout_ref = refs[n_in + n_ex], refs[n_in + n_ex + 1]
        outs = refs[n_in + n_ex + 2 + len(al):]
        dio_refs, dw_ref = outs[:n_io], outs[n_io]
        j = pl.program_id(0)
        u, pre_vjp = jax.vjp(pre, *[r[...] for r in in_refs])
        rows = lax.broadcasted_iota(jnp.int32, u.shape, 0)
        cv = _conv(u, w_ref, kw, rows)
        _, post_vjp = jax.vjp(lambda cc, *ex: post(j, cc, *ex), cv, *[r[...] for r in ex_refs])
        g = post_vjp(dout_ref[...].astype(F32))
        dc = g[0]
        for k in range(kw):
            dw_ref[k:k + 1, :] = jnp.sum(dc * _shift_down(u, kw - 1 - k, rows), axis=0, keepdims=True)
        du = _conv_t(dc, w_ref, kw, rows)
        for r, val in zip(dio_refs, dio_pack(*pre_vjp(du), *g[1:])):
            if isinstance(val, tuple):
                for t, part in enumerate(val):
                    r[t] = part.astype(r.dtype)
            else:
                r[...] = val.astype(r.dtype)

    return pl.pallas_call(
        body, name=name, grid=(ncol,),
        in_specs=list(in_specs) + list(extra_specs) + [w_spec, dout_spec] + al_specs,
        out_specs=list(dio_specs) + [dw_spec],
        out_shape=list(dio_shapes) + [dw_shape],
        input_output_aliases=aliases,
        compiler_params=_params(("arbitrary",)),
    )(*ins, *extras, w, dout, *al)


def _tri_inverse(a):
    c = a.shape[0]
    ii = lax.broadcasted_iota(jnp.int32, (c, c), 0)
    jj = lax.broadcasted_iota(jnp.int32, (c, c), 1)
    eye = (ii == jj).astype(F32)

    def same_block(bits):
        return jnp.right_shift(ii, bits) == jnp.right_shift(jj, bits)

    d = jnp.where(same_block(3), a, 0.0)
    d2 = _mmx(d, d, NN)
    t = (eye - d) + _mmx(eye - d, d2, NN)
    t = t + _mmx(t, _mmx(d2, d2, NN), NN)
    bits = 3
    while (1 << bits) < c:
        low = jnp.where(jnp.logical_and(same_block(bits + 1), jnp.logical_not(same_block(bits))), a, 0.0)
        t = t - _mmx(_mmx(t, low, NN), t, NN)
        bits += 1
    return t


def _dn_common(q, k, v, gb, bb):
    c = q.shape[0]
    ii = lax.broadcasted_iota(jnp.int32, (c, c), 0)
    jj = lax.broadcasted_iota(jnp.int32, (c, c), 1)
    causal = jj <= ii
    strict = jj < ii
    low = causal.astype(F32)
    ones = jnp.ones((c, LANES), F32)
    gc = _mmx(low, gb, NN)
    diff = (_mmx(gc, ones, NT) - _mmx(ones, gc, NT)) * (1.0 / LANES)
    decay = jnp.where(causal, jnp.exp(jnp.where(causal, diff, 0.0)), 0.0)
    gl = jnp.sum(gb, axis=0, keepdims=True)
    eg = jnp.exp(gc)
    egm = jnp.exp(gl - gc)
    egl = jnp.exp(gl)
    kb = k * bb
    vb = v * bb
    kbg = kb * eg
    kk = _mm(kb, k, NT)
    t = _tri_inverse(jnp.where(strict, kk * decay, 0.0))
    qk = _mm(q, k, NT)
    attn = qk * decay
    return dict(causal=causal, strict=strict, low=low, ones=ones, decay=decay, eg=eg, egm=egm, egl=egl,
                kb=kb, vb=vb, kbg=kbg, kk=kk, t=t, qk=qk, attn=attn, qd=q * eg, kd=k * egm)


def _dn_fwd_chunk(q, k, v, gb, bb, s):
    m = _dn_common(q, k, v, gb, bb)
    u = _mmx(m["t"], m["vb"] - _mm(m["kbg"], s, NN), NN)
    o = _mm(m["qd"], s, NN) + _mm(m["attn"], u, NN)
    s2 = s * m["egl"] + _mm(m["kd"], u, TN)
    return o, s2


def _dn_bwd_chunk(q, k, v, gb, bb, s, do, dsp):
    m = _dn_common(q, k, v, gb, bb)
    c = q.shape[0]
    t, decay, eg, egm, egl = m["t"], m["decay"], m["eg"], m["egm"], m["egl"]
    u = _mmx(t, m["vb"] - _mm(m["kbg"], s, NN), NN)
    du = _mm(m["attn"], do, TN) + _mm(m["kd"], dsp, NN)
    dattn = jnp.where(m["causal"], _mm(do, u, NT), 0.0)
    dqd = _mm(do, s, NT)
    dkd = _mm(u, dsp, NT)
    dr = _mmx(t, du, TN)
    da = jnp.where(m["strict"], -_mm(dr, u, NT), 0.0)
    dkbg = -_mm(dr, s, NT)
    ds = dsp * egl + _mm(m["qd"], do, TN) - _mm(m["kbg"], dr, TN)
    degl = jnp.sum(jnp.sum(dsp * s, axis=1, keepdims=True), axis=0, keepdims=True)
    dkk = da * decay
    dqk = dattn * decay
    ddiff = (da * m["kk"] + dattn * m["qk"]) * decay
    dgc = _mmx(ddiff, m["ones"], NN) - _mmx(ddiff, m["ones"], TN)
    dkb = _mm(dkk, k, NN) + dkbg * eg
    dk = _mm(dkk, m["kb"], TN) + _mm(dqk, q, TN) + dkd * egm + dkb * bb
    dq = _mm(dqk, k, NN) + dqd * eg
    dgc = dgc + jnp.sum(dqd * q + dkbg * m["kb"], axis=-1, keepdims=True) * eg
    tt = jnp.sum(dkd * k, axis=-1, keepdims=True) * egm
    dgc = dgc - tt
    dgl = jnp.sum(tt, axis=0, keepdims=True) + degl * egl
    dbb = jnp.sum(dkb * k + dr * v, axis=-1, keepdims=True) + jnp.zeros((c, LANES), F32)
    dv = dr * bb
    dgb = _mmx(m["low"], dgc, TN) + dgl
    return dq, dk, dv, dgb, dbb, ds


def _dn_fwd_call(qkvn, gb, bb):
    _, nh, s, dh = qkvn.shape
    c = min(DN_CHUNK, s)
    n = s // c
    hb = nh

    def body(q_ref, k_ref, v_ref, g_ref, b_ref, o_ref, st_ref, s_ref):
        @pl.when(pl.program_id(1) == 0)
        def _():
            s_ref[...] = jnp.zeros_like(s_ref)

        for h in range(hb):
            st = s_ref[h]
            st_ref[h] = st
            o, s2 = _dn_fwd_chunk(q_ref[h], k_ref[h], v_ref[h], g_ref[h], b_ref[h], st)
            o_ref[h] = o
            s_ref[h] = s2

    def qspec(t):
        return pl.BlockSpec((None, hb, c, dh), lambda i, j, t=t: (t, i, j, 0))

    hs = pl.BlockSpec((hb, c, dh), lambda i, j: (i, j, 0))
    return pl.pallas_call(
        body, name="dn_fwd", grid=(nh // hb, n),
        in_specs=[qspec(0), qspec(1), qspec(2), hs, hs],
        out_specs=[hs, pl.BlockSpec((hb, None, dh, dh), lambda i, j: (i, j, 0, 0))],
        out_shape=[jax.ShapeDtypeStruct((nh, s, dh), F32), jax.ShapeDtypeStruct((nh, n, dh, dh), F32)],
        scratch_shapes=[pltpu.VMEM((hb, dh, dh), F32)],
        compiler_params=_params(("arbitrary", "arbitrary")),
    )(qkvn, qkvn, qkvn, gb, bb)


def _dn_bwd_call(qkvn, gb, bb, states, do):
    _, nh, s, dh = qkvn.shape
    c = min(DN_CHUNK, s)
    n = s // c
    hb = nh

    def body(q_ref, k_ref, v_ref, g_ref, b_ref, st_ref, do_ref, dqkv_ref, dg_ref, db_ref, ds_ref):
        @pl.when(pl.program_id(1) == 0)
        def _():
            ds_ref[...] = jnp.zeros_like(ds_ref)

        for h in range(hb):
            dq, dk, dv, dg, db, ds = _dn_bwd_chunk(q_ref[h], k_ref[h], v_ref[h], g_ref[h], b_ref[h], st_ref[h],
                                                   do_ref[h], ds_ref[h])
            dqkv_ref[0, h] = dq
            dqkv_ref[1, h] = dk
            dqkv_ref[2, h] = dv
            dg_ref[h] = dg
            db_ref[h] = db
            ds_ref[h] = ds

    def qspec(t):
        return pl.BlockSpec((None, hb, c, dh), lambda i, j, t=t: (t, i, n - 1 - j, 0))

    hs = pl.BlockSpec((hb, c, dh), lambda i, j: (i, n - 1 - j, 0))
    sh = jax.ShapeDtypeStruct((nh, s, dh), F32)
    return pl.pallas_call(
        body, name="dn_bwd", grid=(nh // hb, n),
        in_specs=[qspec(0), qspec(1), qspec(2), hs, hs,
                  pl.BlockSpec((hb, None, dh, dh), lambda i, j: (i, n - 1 - j, 0, 0)), hs],
        out_specs=[pl.BlockSpec((3, hb, c, dh), lambda i, j: (0, i, n - 1 - j, 0)), hs, hs],
        out_shape=[jax.ShapeDtypeStruct((3, nh, s, dh), F32), sh, sh],
        scratch_shapes=[pltpu.VMEM((hb, dh, dh), F32)],
        compiler_params=_params(("arbitrary", "arbitrary")),
    )(qkvn, qkvn, qkvn, gb, bb, states, do)


def _adamw(w, g, m, v, name):
    r, c = w.shape
    br = _div_tile(r, max(8, (1 << 18) // max(c, 1)), 8)

    def fn(w, g, m, v):
        m = ADAM_B1 * m + (1.0 - ADAM_B1) * g
        v = ADAM_B2 * v + (1.0 - ADAM_B2) * jnp.square(g)
        m_hat = m / (1.0 - ADAM_B1 ** ADAM_STEP)
        v_hat = v / (1.0 - ADAM_B2 ** ADAM_STEP)
        delta = -ADAM_LR * (m_hat / (jnp.sqrt(v_hat) + ADAM_EPS) + ADAM_WD * w)
        return delta, m, v

    spec = pl.BlockSpec((br, c), lambda i: (i, 0))
    return _ew(fn, name, (r // br,), [w, g, m, v], [spec] * 4, [((r, c), F32, spec)] * 3)


def _coords():
    return lax.axis_index("x"), lax.axis_index("y"), lax.axis_index("c")


def _all_gather_small(v):
    r, w = v.shape

    def body(v_ref, out_ref, send_sems, recv_sems, local_sem):
        x, y, c = _coords()
        me = 4 * x + 2 * y + c
        mine = pltpu.make_async_copy(v_ref, out_ref.at[me], local_sem)
        mine.start()
        peers = []
        for k in range(1, 8):
            px = 1 - x if k & 4 else x
            py = 1 - y if k & 2 else y
            pc = 1 - c if k & 1 else c
            peers.append((px, py, pc))
        sends = []
        for k, peer in enumerate(peers):
            cp = pltpu.make_async_remote_copy(src_ref=v_ref, dst_ref=out_ref.at[me], send_sem=send_sems.at[k],
                                              recv_sem=recv_sems.at[k], device_id=peer, device_id_type=MESH)
            cp.start()
            sends.append(cp)
        for k, (px, py, pc) in enumerate(peers):
            pltpu.make_async_remote_copy(src_ref=v_ref, dst_ref=out_ref.at[4 * px + 2 * py + pc],
                                         send_sem=send_sems.at[k], recv_sem=recv_sems.at[k],
                                         device_id=(px, py, pc), device_id_type=MESH).wait_recv()
        for cp in sends:
            cp.wait_send()
        mine.wait()

    return pl.pallas_call(
        body, name="ag_small", out_shape=jax.ShapeDtypeStruct((8, r, w), v.dtype),
        in_specs=[pl.BlockSpec(memory_space=pltpu.VMEM)], out_specs=pl.BlockSpec(memory_space=pltpu.VMEM),
        scratch_shapes=[pltpu.SemaphoreType.DMA((7,)), pltpu.SemaphoreType.DMA((7,)), pltpu.SemaphoreType.DMA],
        compiler_params=pltpu.CompilerParams(vmem_limit_bytes=VMEM_LIMIT),
    )(v)


def _hbm_call(body, name, arrays, out_shapes, n_sems, aliases=None):
    hbm = pl.BlockSpec(memory_space=pltpu.HBM)
    return pl.pallas_call(
        body, name=name, out_shape=list(out_shapes), in_specs=[hbm] * len(arrays), out_specs=[hbm] * len(out_shapes),
        input_output_aliases=aliases or {},
        scratch_shapes=[pltpu.SemaphoreType.DMA((n_sems,)), pltpu.SemaphoreType.DMA((n_sems,))],
    )(*arrays)


def _half_rows(ref_shape, c):
    h = ref_shape[1] // 2
    return pl.ds(pl.multiple_of(c * h, 16), h), pl.ds(pl.multiple_of((1 - c) * h, 16), h)


def _pair_swap_half(ps, name):
    n = len(ps)

    def body(*refs):
        ins, outs, send_sems, recv_sems = refs[:n], refs[n:2 * n], refs[2 * n], refs[2 * n + 1]
        x, y, c = _coords()
        cps = []
        for w in range(n):
            _, other = _half_rows(ins[w].shape, c)
            cp = pltpu.make_async_remote_copy(src_ref=ins[w].at[:, other], dst_ref=outs[w], send_sem=send_sems.at[w],
                                              recv_sem=recv_sems.at[w], device_id=(x, y, 1 - c), device_id_type=MESH)
            cp.start()
            cps.append(cp)
        for cp in cps:
            cp.wait()

    shapes = [jax.ShapeDtypeStruct((p.shape[0], p.shape[1] // 2, p.shape[2]), p.dtype) for p in ps]
    return _hbm_call(body, name, ps, shapes, n)


def _pair_gather(gs, name):
    n = len(gs)

    def body(*refs):
        ins, outs, send_sems, recv_sems = refs[:n], refs[n:2 * n], refs[2 * n], refs[2 * n + 1]
        x, y, c = _coords()
        cps = []
        for w in range(n):
            cp = pltpu.make_async_remote_copy(src_ref=ins[w].at[c], dst_ref=outs[w].at[c], send_sem=send_sems.at[w],
                                              recv_sem=recv_sems.at[w], device_id=(x, y, 1 - c), device_id_type=MESH)
            cp.start()
            cps.append(cp)
        for w, cp in enumerate(cps):
            pltpu.make_async_remote_copy(src_ref=ins[w].at[c], dst_ref=outs[w].at[1 - c], send_sem=send_sems.at[w],
                                         recv_sem=recv_sems.at[w], device_id=(x, y, 1 - c),
                                         device_id_type=MESH).wait_recv()
            cp.wait_send()

    shapes = [jax.ShapeDtypeStruct(g.shape, g.dtype) for g in gs]
    return _hbm_call(body, name, gs, shapes, n, aliases={i: i for i in range(n)})


_HBM = pl.BlockSpec(memory_space=pltpu.HBM)
_SEM = pl.BlockSpec(memory_space=pltpu.SEMAPHORE)
_EFFECT = pltpu.SideEffectType.DATAFLOW_SIDE_EFFECTING


def _split_start(name, srcs, lands, after, n_copies, make_copies):
    n, m = len(srcs), len(lands)
    arrays = [pltpu.with_memory_space_constraint(a, pltpu.HBM) for a in list(srcs) + list(lands)]

    def body(*refs):
        src_refs, land_refs = refs[:n], refs[n:n + m]
        send_sems, recv_sems = refs[n + m + 1], refs[n + m + 2]
        for cp in make_copies(src_refs, land_refs, send_sems, recv_sems):
            cp.start()
        refs[-1][...] = jnp.zeros_like(refs[-1])

    outs = pl.pallas_call(
        body, name=name,
        out_shape=(pltpu.SemaphoreType.DMA((n_copies,)), pltpu.SemaphoreType.DMA((n_copies,)),
                   *[pltpu.HBM(a.shape, a.dtype) for a in arrays], jax.ShapeDtypeStruct((8, LANES), F32)),
        in_specs=[_HBM] * (n + m) + [pl.BlockSpec(memory_space=pl.ANY)],
        out_specs=(_SEM, _SEM, *[_HBM] * (n + m), pl.BlockSpec(memory_space=pltpu.VMEM)),
        input_output_aliases={i: 2 + i for i in range(n + m)},
        compiler_params=pltpu.CompilerParams(has_side_effects=_EFFECT),
    )(*arrays, after)
    return (outs[0], outs[1], list(outs[2:2 + n]), list(outs[2 + n:2 + n + m])), outs[-1]


def _split_wait(name, state, after, make_copies):
    send_sems, recv_sems, srcs, lands = state
    n, m = len(srcs), len(lands)

    def body(*refs):
        src_refs, land_refs = refs[:n], refs[n:n + m]
        for cp in make_copies(src_refs, land_refs, refs[n + m], refs[n + m + 1]):
            cp.wait_send()
            cp.wait_recv()

    outs = pl.pallas_call(
        body, name=name, out_shape=tuple(pltpu.HBM(a.shape, a.dtype) for a in srcs + lands),
        in_specs=[_HBM] * (n + m) + [_SEM, _SEM, pl.BlockSpec(memory_space=pl.ANY)], out_specs=tuple([_HBM] * (n + m)),
        input_output_aliases={i: i for i in range(n + m)},
        compiler_params=pltpu.CompilerParams(has_side_effects=_EFFECT),
    )(*srcs, *lands, send_sems, recv_sems, after)
    return list(outs[:n]), list(outs[n:])


def _scatter_copies(q_refs, land_refs, send_sems, recv_sems):
    x, y, c = _coords()
    cps = []
    for w, (q, land) in enumerate(zip(q_refs, land_refs)):
        for k, (px, py) in enumerate([(1 - x, y), (x, 1 - y), (1 - x, 1 - y)]):
            cps.append(pltpu.make_async_remote_copy(src_ref=q.at[2 * px + py], dst_ref=land.at[k],
                                                    send_sem=send_sems.at[3 * w + k], recv_sem=recv_sems.at[3 * w + k],
                                                    device_id=(px, py, c), device_id_type=MESH))
    return cps


def _chip_scatter_start(qs, name):
    lands = [lax.empty((3,) + q.shape[1:], q.dtype) for q in qs]
    return _split_start(name, qs, lands, qs[0], 3 * len(qs), _scatter_copies)


def _chip_scatter_wait(state, after, name):
    return _split_wait(name, state, after, _scatter_copies)


def _gather_copies(src_refs, buf_refs, send_sems, recv_sems):
    x, y, c = _coords()
    j = 2 * x + y
    cps = []
    for w, buf in enumerate(buf_refs):
        mine, _ = _half_rows(buf.shape, c)
        for k, (px, py) in enumerate([(1 - x, y), (x, 1 - y), (1 - x, 1 - y)]):
            cps.append(pltpu.make_async_remote_copy(src_ref=buf.at[j, mine], dst_ref=buf.at[j, mine],
                                                    send_sem=send_sems.at[3 * w + k], recv_sem=recv_sems.at[3 * w + k],
                                                    device_id=(px, py, c), device_id_type=MESH))
    return cps


def _gather_wait_copies(src_refs, buf_refs, send_sems, recv_sems):
    x, y, c = _coords()
    j = 2 * x + y
    cps = []
    for w, buf in enumerate(buf_refs):
        mine, _ = _half_rows(buf.shape, c)
        for k, (px, py) in enumerate([(1 - x, y), (x, 1 - y), (1 - x, 1 - y)]):
            cps.append(pltpu.make_async_remote_copy(src_ref=buf.at[j, mine], dst_ref=buf.at[2 * px + py, mine],
                                                    send_sem=send_sems.at[3 * w + k], recv_sem=recv_sems.at[3 * w + k],
                                                    device_id=(px, py, c), device_id_type=MESH))
    return cps


def _pair_forward(bufs, name):
    n = len(bufs)

    def body(*refs):
        ins, outs, send_sems, recv_sems = refs[:n], refs[n:2 * n], refs[2 * n], refs[2 * n + 1]
        x, y, c = _coords()
        chips = [(1 - x, y), (x, 1 - y), (1 - x, 1 - y)]
        cps = []
        for w in range(n):
            mine, _ = _half_rows(outs[w].shape, c)
            for k, (px, py) in enumerate(chips):
                cp = pltpu.make_async_remote_copy(src_ref=ins[w].at[2 * px + py, mine],
                                                  dst_ref=outs[w].at[2 * px + py, mine],
                                                  send_sem=send_sems.at[3 * w + k], recv_sem=recv_sems.at[3 * w + k],
                                                  device_id=(x, y, 1 - c), device_id_type=MESH)
                cp.start()
                cps.append(cp)
        for w in range(n):
            _, sib = _half_rows(outs[w].shape, c)
            for k, (px, py) in enumerate(chips):
                pltpu.make_async_remote_copy(src_ref=ins[w].at[2 * px + py, sib], dst_ref=outs[w].at[2 * px + py, sib],
                                             send_sem=send_sems.at[3 * w + k], recv_sem=recv_sems.at[3 * w + k],
                                             device_id=(x, y, 1 - c), device_id_type=MESH).wait_recv()
        for cp in cps:
            cp.wait_send()

    shapes = [jax.ShapeDtypeStruct(b.shape, b.dtype) for b in bufs]
    return _hbm_call(body, name, bufs, shapes, 3 * n, aliases={i: i for i in range(n)})


def _slot_rows(h, cs):
    return _div_tile(h, max(16, (1 << 19) // cs), 16)


def _cast_into_slot(w, slot, name):
    r, cs = w.shape
    br = _slot_rows(r, cs)
    return _ew_slot(lambda a: a, name, (r // br,), slot, [w], [pl.BlockSpec((br, cs), lambda i, s: (i, 0))],
                    (4, r, cs), BF16, pl.BlockSpec((None, br, cs), lambda i, s: (s[0], i, 0)))


def _pair_add(p, rb, core, name):
    n, r, cs = p.shape
    h = r // 2
    br = _slot_rows(h, cs)
    nb = h // br
    return _ew_slot(lambda a, b: a + b, name, (n, nb), core, [p, rb],
                    [pl.BlockSpec((None, br, cs), lambda s, i, c: (s, c[0] * nb + i, 0)),
                     pl.BlockSpec((None, br, cs), lambda s, i, c: (s, i, 0))],
                    (n, h, cs), BF16, pl.BlockSpec((None, br, cs), lambda s, i, c: (s, i, 0)))


def _chip_sum(q, r3, slots, name):
    _, h, cs = q.shape
    br = _slot_rows(h, cs)

    def fn(a, b):
        acc = a.astype(F32)
        for k in range(3):
            acc = acc + b[k].astype(F32)
        return acc

    return _ew_slot(fn, name, (h // br,), slots, [q, r3],
                    [pl.BlockSpec((None, br, cs), lambda i, s: (s[0], i, 0)),
                     pl.BlockSpec((3, br, cs), lambda i, s: (0, i, 0))],
                    (2, h, cs), F32, pl.BlockSpec((None, br, cs), lambda i, s: (s[1], i, 0)))


def _sum_slots(r, name):
    n, h, w = r.shape
    br = _div_tile(h, 2048, 16)

    def fn(blk):
        acc = blk[0].astype(F32)
        for s in range(1, n):
            acc = acc + blk[s].astype(F32)
        return (acc,)

    return _ew(fn, name, (h // br,), [r], [pl.BlockSpec((n, br, w), lambda i: (0, i, 0))],
               [((h, w), F32, pl.BlockSpec((br, w), lambda i: (i, 0)))])[0]


def _pack_small(arrs):
    flat = jnp.concatenate([a.reshape(-1).astype(F32) for a in arrs])
    n = flat.shape[0]
    rows = -(-n // LANES)
    rows = -(-rows // 8) * 8
    return jnp.pad(flat, (0, rows * LANES - n)).reshape(rows, LANES)


def _unpack_small(p, shapes):
    lead = p.shape[:-2]
    flat = p.reshape(lead + (-1,))
    out, off = [], 0
    for sh in shapes:
        n = 1
        for d in sh:
            n *= d
        out.append(flat[..., off:off + n].reshape(lead + tuple(sh)))
        off += n
    return out


def kernel(x, c, w_ada, b_ada, norm1_g, w_in, dn_conv_w, dn_a_log, dn_dt_bias, dn_norm_g, dn_w_o, cf_conv_w, cf_ln_g, cf_ln_b, cf_w_o, w_out, norm2_g, ffn_w_up, ffn_conv_w, ffn_w_down, final_norm_g, loss_target, m_w_ada, m_b_ada, m_norm1_g, m_w_in, m_dn_conv_w, m_dn_a_log, m_dn_dt_bias, m_dn_norm_g, m_dn_w_o, m_cf_conv_w, m_cf_ln_g, m_cf_ln_b, m_cf_w_o, m_w_out, m_norm2_g, m_ffn_w_up, m_ffn_conv_w, m_ffn_w_down, m_final_norm_g, v_w_ada, v_b_ada, v_norm1_g, v_w_in, v_dn_conv_w, v_dn_a_log, v_dn_dt_bias, v_dn_norm_g, v_dn_w_o, v_cf_conv_w, v_cf_ln_g, v_cf_ln_b, v_cf_w_o, v_w_out, v_norm2_g, v_ffn_w_up, v_ffn_conv_w, v_ffn_w_down, v_final_norm_g):
    xi, yi, ci = _coords()
    chip = 2 * xi + yi
    me = 4 * xi + 2 * yi + ci
    core = jnp.reshape(ci, (1,)).astype(jnp.int32)

    S, D = x.shape[1], x.shape[2]
    NH = dn_a_log.shape[1]
    DH = dn_norm_g.shape[1]
    DNW = NH * DH
    CFW = cf_ln_g.shape[1]
    FFN = ffn_w_down.shape[1] * 4
    KDN, KCF, KFF = dn_conv_w.shape[1], cf_conv_w.shape[1], ffn_conv_w.shape[1]
    NIN = w_in.shape[2] * 4
    assert NIN == 4 * DNW + 2 * NH + 2 * CFW + 2 * D and DH == LANES and 2 * NH <= LANES
    x2, tgt = x[0], loss_target[0]

    sm_shapes = [(D,), (KDN, 3 * DNW // 4), (KCF, CFW // 4), (KFF, FFN // 4)]
    g1 = _all_gather_small(_pack_small([c[0], dn_conv_w[0], cf_conv_w[0], ffn_conv_w[0]]))
    c_all, dcw_s, ccw_s, fcw_s = _unpack_small(g1, sm_shapes)

    def chips_cols(t):
        t = t[0::2]
        return jnp.transpose(t, (1, 0, 2)).reshape(t.shape[1], -1)

    dn_cw, cf_cw, ff_cw = chips_cols(dcw_s), chips_cols(ccw_s), chips_cols(fcw_s)

    slot_chip = jnp.reshape(chip, (1,)).astype(jnp.int32)
    big = [w_in[0], dn_w_o[0], cf_w_o[0], w_out[0], ffn_w_up[0], ffn_w_down[0]]
    names_big = ["w_in", "dn_w_o", "cf_w_o", "w_out", "ffn_w_up", "ffn_w_down"]
    in_state, in_token = _split_start("ag_in_start", [], [_cast_into_slot(big[0], slot_chip, "cast_w_in")], c, 3,
                                      _gather_copies)
    bufs = [_cast_into_slot(w, slot_chip, "cast_" + nm) for w, nm in zip(big[1:], names_big[1:])]
    ag_state, ag_token = _split_start("ag_rest_start", [], bufs, in_token, 3 * len(bufs), _gather_copies)

    cb_cf = _div_tile(CFW, 256, LANES)
    o_b = 4 * DNW
    o_glu = o_b + 2 * NH
    o_ga = o_glu + 2 * CFW
    NA = NIN - 2 * NH
    a_z, a_ga, a_gb, a_glu, a_ba = 3 * DNW, 4 * DNW, 4 * DNW + D, 4 * DNW + 2 * D, NA

    segs = [(0, 0, o_b), (o_b, NA, 2 * NH), (o_ga, a_ga, 2 * D)]
    for t in range(CFW // cb_cf):
        segs += [(o_glu + t * cb_cf, a_glu + 2 * t * cb_cf, cb_cf),
                 (o_glu + CFW + t * cb_cf, a_glu + (2 * t + 1) * cb_cf, cb_cf)]
    CS = NIN // 4

    def shard_slices(lo, n):
        out = []
        while n > 0:
            j, off = lo // CS, lo % CS
            m = min(n, CS - off)
            out.append(w_in_g[j][:, off:off + m])
            lo, n = lo + m, n - m
        return out

    CA = w_ada.shape[2]
    b_sh = lax.dynamic_slice(b_ada, (0, chip * CA), (1, CA))
    tn_a = _div_tile(CA, 512, LANES)

    def mod_fn(cc, w, b):
        return (_mm(_silu(cc), w, NN) + b,)

    mod_sh = _ew(mod_fn, "ada_mod", (CA // tn_a,), [c_all, w_ada[0], b_sh],
                 [_full((8, D)), pl.BlockSpec((D, tn_a), lambda j: (0, j)), pl.BlockSpec((1, tn_a), lambda j: (0, j))],
                 [((8, CA), F32, pl.BlockSpec((8, tn_a), lambda j: (0, j)))])[0]
    g2 = _all_gather_small(_pack_small([mod_sh]))
    mod_all = _unpack_small(g2, [(8, CA)])[0][0::2]
    mod_all = jnp.transpose(mod_all, (1, 0, 2)).reshape(8, 4 * CA)
    mod_me = lax.dynamic_slice(mod_all, (me, 0), (1, 6 * D))
    sh1, sc1, gt1, sh2, sc2, gt2 = [mod_me[:, i * D:(i + 1) * D] for i in range(6)]

    bs = _div_tile(S, 128, 8)
    nb = S // bs
    vecD = _full((1, D))
    rowD = _row(bs, D)

    hn1 = _ew(lambda a, g, sc, sh: (_f_normmod(a, g, sc, sh),), "norm1_fwd", (nb,),
              [x2, norm1_g, sc1, sh1], [rowD, vecD, vecD, vecD], [((S, D), BF16, rowD)],
              after=[ag_token, m_w_in[0], v_w_in[0]])[0]
    _, in_landed = _split_wait("ag_in_wait", in_state, hn1, _gather_wait_copies)
    (w_in_g,) = _pair_forward(in_landed, "ag_in_pair")
    w_aug = jnp.concatenate([p for o, a, n in sorted(segs, key=lambda s: s[1]) for p in shard_slices(o, n)]
                            + [jnp.zeros((D, LANES - 2 * NH), BF16)], axis=1)
    proj = _matmul(hn1, w_aug, "nn", F32, "mm_in")

    def dn_post(j, cv):
        s = _silu(cv)
        nrm = s * lax.rsqrt(jnp.sum(s * s, axis=-1, keepdims=True) + EPS)
        fq = (j < NH).astype(F32)
        fk = (j < 2 * NH).astype(F32)
        scale = fq * (DH ** -0.5) + (1.0 - fq)
        return fk * (nrm * scale) + (1.0 - fk) * s

    def ident(a):
        return a

    def colS(w, off=0):
        return pl.BlockSpec((S, w), lambda j, off=off: (0, j + off))

    def wS(kw, w):
        return pl.BlockSpec((kw, w), lambda j: (0, j))

    qkv_spec = pl.BlockSpec((None, None, S, DH), lambda j: (j // NH, j % NH, 0, 0))
    dn_args = dict(kw=KDN, ncol=3 * NH, ins=[proj], in_specs=[colS(DH)], extras=[], extra_specs=[],
                   w=dn_cw, w_spec=wS(KDN, DH), pre=ident, post=dn_post)
    qkvn = _conv_fwd_call("dn_conv_fwd", out_shape=jax.ShapeDtypeStruct((3, NH, S, DH), F32), out_spec=qkv_spec,
                          **dn_args)

    alp = jnp.pad(dn_a_log, ((0, 0), (NH, LANES - 2 * NH)))
    dtb = jnp.pad(dn_dt_bias, ((0, 0), (NH, LANES - 2 * NH)))
    vecL = _full((1, LANES))
    ba_spec = _row(bs, LANES, a_ba // LANES)
    rowL = _row(bs, LANES)
    gate_fn = functools.partial(_f_dn_gate, NH)
    def gate_fwd(a, p, q):
        val = gate_fn(a, p, q)
        lane = lax.broadcasted_iota(jnp.int32, val.shape, 1)

        def spread(col):
            return jnp.sum(jnp.where(lane == col, val, 0.0), axis=-1, keepdims=True) + jnp.zeros(val.shape, F32)

        return tuple(spread(h) for h in range(NH)), tuple(spread(NH + h) for h in range(NH))

    hrow = pl.BlockSpec((NH, bs, DH), lambda i: (0, i, 0))
    bb_b, gb_b = _ew(gate_fwd, "dn_gate_fwd", (nb,), [proj, alp, dtb], [ba_spec, vecL, vecL],
                     [((NH, S, DH), F32, hrow), ((NH, S, DH), F32, hrow)])
    o_dn, states = _dn_fwd_call(qkvn, gb_b, bb_b)

    bsh = _div_tile(S, 512, 8)
    nbh = S // bsh
    o_spec = pl.BlockSpec((None, bsh, DH), lambda i, h: (h, i, 0))
    z_spec = pl.BlockSpec((bsh, DH), lambda i, h: (i, a_z // DH + h))
    oh_spec = pl.BlockSpec((bsh, DH), lambda i, h: (i, h))
    ng_spec = pl.BlockSpec((1, DH), lambda i, h: (0, 0))
    on = _ew(lambda o, z, g: (_f_dn_post(o, z, g),), "dn_post_fwd", (nbh, NH), [o_dn, proj, dn_norm_g],
             [o_spec, z_spec, ng_spec], [((S, DNW), BF16, oh_spec)])[0]
    _, landed = _split_wait("ag_rest_wait", ag_state, on, _gather_wait_copies)
    w_do_f, w_co_f, w_out_g, w_up_f, w_dn_g = _pair_forward(landed, "ag_rest_pair")
    w_out_f = w_out_g.reshape(-1, w_out_g.shape[2])
    w_dn_f = w_dn_g.reshape(-1, w_dn_g.shape[2])
    br_a = _matmul(on, w_do_f, "nn", F32, "mm_dn_o")

    def glu_pre(val, gl):
        return val * _sigmoid(gl)

    def glu_spec(t):
        return pl.BlockSpec((S, cb_cf), lambda j, t=t: (0, a_glu // cb_cf + 2 * j + t))

    cf_args = dict(kw=KCF, ncol=CFW // cb_cf, ins=[proj, proj], in_specs=[glu_spec(0), glu_spec(1)],
                   extras=[], extra_specs=[], w=cf_cw, w_spec=wS(KCF, cb_cf), pre=glu_pre, post=lambda j, cv: cv)
    uc = _conv_fwd_call("cf_conv_fwd", out_shape=jax.ShapeDtypeStruct((S, CFW), F32), out_spec=colS(cb_cf), **cf_args)
    rowC = _row(bs, CFW)
    vecC = _full((1, CFW))
    ub = _ew(lambda u, g, b: (_f_cf_ln(u, g, b),), "cf_ln_fwd", (nb,), [uc, cf_ln_g, cf_ln_b], [rowC, vecC, vecC],
             [((S, CFW), BF16, rowC)])[0]
    br_b = _matmul(ub, w_co_f, "nn", F32, "mm_cf_o")

    ga_spec, gb_spec = _row(bs, D, a_ga // D), _row(bs, D, a_gb // D)
    merged = _ew(lambda a, b, ga, gb: (_f_merge(a, b, ga, gb),), "merge_fwd", (nb,), [br_a, br_b, proj, proj],
                 [rowD, rowD, ga_spec, gb_spec], [((S, D), BF16, rowD)])[0]
    mix = _matmul(merged, w_out_f, "nn", F32, "mm_out")

    x1, hn2 = _ew(_f_res_normmod, "norm2_fwd", (nb,), [x2, mix, gt1, norm2_g, sc2, sh2],
                  [rowD, rowD, vecD, vecD, vecD, vecD], [((S, D), F32, rowD), ((S, D), BF16, rowD)])
    up_all = _matmul(hn2, w_up_f, "nn", F32, "mm_up")

    cb_ff = _div_tile(FFN, 256, LANES)
    ff_args = dict(kw=KFF, ncol=FFN // cb_ff, ins=[up_all], in_specs=[colS(cb_ff)], extras=[up_all],
                   extra_specs=[colS(cb_ff, FFN // cb_ff)], w=ff_cw, w_spec=wS(KFF, cb_ff),
                   pre=ident, post=lambda j, cv, up: _silu(cv) * up)
    hff = _conv_fwd_call("ffn_conv_fwd", out_shape=jax.ShapeDtypeStruct((S, FFN), BF16), out_spec=colS(cb_ff), **ff_args)
    ffo = _matmul(hff, w_dn_f, "nn", F32, "mm_down")

    gf2 = final_norm_g.reshape(1, D)

    def loss_bwd(a, f, gt, gf, t):
        val, vjp = jax.vjp(_f_loss, a, f, gt, gf, t)
        da, df, dgt, dgf, _ = vjp(jnp.ones((), F32))
        return da, df, dgt, dgf, jnp.zeros((1, LANES), F32) + val

    dx1_l, dffo, dgt2, dgf, loss_v = _ew(
        loss_bwd, "loss_bwd", (nb,), [x1, ffo, gt2, gf2, tgt], [rowD, rowD, vecD, vecD, rowD],
        [((S, D), F32, rowD), ((S, D), BF16, rowD), ((1, D), F32, vecD), ((1, D), F32, vecD),
         ((1, LANES), F32, vecL)], acc=(2, 3, 4))

    dhff = _matmul(dffo, w_dn_f, "nt", F32, "mm_down_dx")
    g_w_dn = _matmul(hff, dffo, "tn", F32, "mm_down_dw")

    slots = jnp.stack([chip, ci]).astype(jnp.int32)
    rs_groups = []

    def rs_begin(parts, nms, tag):
        rbs = _pair_swap_half(parts, "rs_pair_" + tag)
        q16 = [_pair_add(p, rb, core, "rs_pair_add_" + nm) for p, rb, nm in zip(parts, rbs, nms)]
        state, token = _chip_scatter_start(q16, "rs_chips_start_" + tag)
        rs_groups.append((state, nms, tag))
        return token

    tok_a = rs_begin([g_w_dn.reshape(4, FFN // 4, D)], ["ffn_w_down"], "a")

    d_upall, g_ffcw = _conv_bwd_call(
        "ffn_conv_bwd", dout=dhff, dout_spec=colS(cb_ff),
        dio_shapes=[jax.ShapeDtypeStruct((2, S, FFN), BF16)],
        dio_specs=[pl.BlockSpec((2, S, cb_ff), lambda j: (0, 0, j))], dio_pack=lambda dg, du: [(dg, du)],
        dw_shape=jax.ShapeDtypeStruct((KFF, FFN), F32), dw_spec=wS(KFF, cb_ff), after=[tok_a], **ff_args)
    dhn2 = _matmul(d_upall, w_up_f, "nt", F32, "mm_up_dx")
    g_w_up = _matmul(hn2, d_upall, "tn", F32, "mm_up_dw", out_groups=4)
    tok_b = rs_begin([g_w_up], ["ffn_w_up"], "b")

    def res2_bwd(a, mx, gt, g, sc, sh, dx1, dhn):
        _, vjp = jax.vjp(_f_res_normmod, a, mx, gt, g, sc, sh)
        return vjp((dx1, dhn))

    dx_r, dmix, dgt1, dg2, dsc2, dsh2 = _ew(
        res2_bwd, "norm2_bwd", (nb,), [x2, mix, gt1, norm2_g, sc2, sh2, dx1_l, dhn2],
        [rowD, rowD, vecD, vecD, vecD, vecD, rowD, rowD],
        [((S, D), F32, rowD), ((S, D), BF16, rowD)] + [((1, D), F32, vecD)] * 4, acc=(2, 3, 4, 5), after=[tok_b])

    dmerged = _matmul(dmix, w_out_f, "nt", F32, "mm_out_dx")
    g_w_out = _matmul(merged, dmix, "tn", F32, "mm_out_dw")

    def merge_bwd(a, b, ga, gb, dm):
        _, vjp = jax.vjp(_f_merge, a, b, ga, gb)
        da, db, dga, dgb = vjp(dm)
        return da, db, jnp.concatenate([dga, dgb], axis=1)

    d_bra, d_brb, dproj = _ew(merge_bwd, "merge_bwd", (nb,), [br_a, br_b, proj, proj, dmerged],
                              [rowD, rowD, ga_spec, gb_spec, rowD],
                              [((S, D), BF16, rowD), ((S, D), BF16, rowD),
                               ((S, NA + LANES), BF16, _row(bs, 2 * D, a_ga // (2 * D)))])

    d_on = _matmul(d_bra, w_do_f, "nt", F32, "mm_dn_o_dx")
    g_w_do = _matmul(on, d_bra, "tn", F32, "mm_dn_o_dw", out_groups=4)
    d_ub = _matmul(d_brb, w_co_f, "nt", F32, "mm_cf_o_dx")
    g_w_co = _matmul(ub, d_brb, "tn", F32, "mm_cf_o_dw", out_groups=4)
    tok_c = rs_begin([g_w_do, g_w_co, g_w_out.reshape(4, D // 4, D)], ["dn_w_o", "cf_w_o", "w_out"], "c")

    def cf_ln_bwd(u, g, b, du):
        _, vjp = jax.vjp(_f_cf_ln, u, g, b)
        return vjp(du)

    d_uc, g_cflg, g_cflb = _ew(cf_ln_bwd, "cf_ln_bwd", (nb,), [uc, cf_ln_g, cf_ln_b, d_ub], [rowC, vecC, vecC, rowC],
                               [((S, CFW), F32, rowC), ((1, CFW), F32, vecC), ((1, CFW), F32, vecC)], acc=(1, 2),
                               after=[tok_c])
    dproj_sds = jax.ShapeDtypeStruct((S, NA + LANES), BF16)
    dproj, g_cfcw = _conv_bwd_call(
        "cf_conv_bwd", dout=d_uc, dout_spec=colS(cb_cf), dio_shapes=[dproj_sds],
        dio_specs=[colS(2 * cb_cf, a_glu // (2 * cb_cf))], dio_pack=lambda dv, dg: [jnp.concatenate([dv, dg], axis=1)],
        dw_shape=jax.ShapeDtypeStruct((KCF, CFW), F32), dw_spec=wS(KCF, cb_cf), alias=dproj, **cf_args)

    def dn_post_bwd(o, z, g, d):
        _, vjp = jax.vjp(_f_dn_post, o, z, g)
        return vjp(d)

    d_o, dproj, g_dnng = _ew(dn_post_bwd, "dn_post_bwd", (nbh, NH), [o_dn, proj, dn_norm_g, d_on],
                             [o_spec, z_spec, ng_spec, oh_spec],
                             [((NH, S, DH), F32, o_spec), ((S, NA + LANES), BF16, z_spec), ((1, DH), F32, ng_spec)],
                             acc=(2,), alias=(dproj, 1))

    dqkvn, dgb_b, dbb_b = _dn_bwd_call(qkvn, gb_b, bb_b, states, d_o)
    dproj, g_dncw = _conv_bwd_call(
        "dn_conv_bwd", dout=dqkvn, dout_spec=qkv_spec, dio_shapes=[dproj_sds], dio_specs=[colS(DH)],
        dio_pack=lambda d: [d], dw_shape=jax.ShapeDtypeStruct((KDN, 3 * DNW), F32), dw_spec=wS(KDN, DH),
        alias=dproj, **dn_args)

    def gate_bwd(a, p, q, db, dg):
        lane = lax.broadcasted_iota(jnp.int32, a.shape, 1)
        d = jnp.zeros(a.shape, F32)
        for h in range(NH):
            d = d + jnp.where(lane == h, db[h], 0.0) + jnp.where(lane == NH + h, dg[h], 0.0)
        _, vjp = jax.vjp(gate_fn, a, p, q)
        return vjp(d)

    dproj, g_alp, g_dtb = _ew(gate_bwd, "dn_gate_bwd", (nb,), [proj, alp, dtb, dbb_b, dgb_b],
                              [ba_spec, vecL, vecL, hrow, hrow],
                              [((S, NA + LANES), BF16, ba_spec), ((1, LANES), F32, vecL), ((1, LANES), F32, vecL)],
                              acc=(1, 2), alias=(dproj, 0))

    g_w_aug = _matmul(hn1, dproj, "tn", F32, "mm_in_dw")
    def aug_slices(lo, n):
        out = []
        for o, a, m in sorted(segs):
            s, e = max(lo, o), min(lo + n, o + m)
            if s < e:
                out.append(g_w_aug[:, a + s - o:a + e - o])
        return out

    g_w_in = jnp.stack([jnp.concatenate(aug_slices(j * CS, CS), axis=1) for j in range(4)])
    tok_d = rs_begin([g_w_in], ["w_in"], "d")
    dhn1 = _matmul(dproj, w_aug, "nt", F32, "mm_in_dx", after=[tok_d])

    def norm1_bwd(a, g, sc, sh, dhn, dxr):
        _, vjp = jax.vjp(_f_normmod, a, g, sc, sh)
        da, dg, dsc, dsh = vjp(dhn)
        return da + dxr, dg, dsc, dsh

    grad_x, dg1, dsc1, dsh1 = _ew(norm1_bwd, "norm1_bwd", (nb,), [x2, norm1_g, sc1, sh1, dhn1, dx_r],
                                  [rowD, vecD, vecD, vecD, rowD, rowD],
                                  [((S, D), F32, rowD)] + [((1, D), F32, vecD)] * 3, acc=(1, 2, 3))

    names = ["w_ada", "b_ada", "norm1_g", "w_in", "dn_conv_w", "dn_a_log", "dn_dt_bias", "dn_norm_g", "dn_w_o",
             "cf_conv_w", "cf_ln_g", "cf_ln_b", "cf_w_o", "w_out", "norm2_g", "ffn_w_up", "ffn_conv_w", "ffn_w_down",
             "final_norm_g"]
    wts = dict(zip(names, [w_ada, b_ada, norm1_g, w_in, dn_conv_w, dn_a_log, dn_dt_bias, dn_norm_g, dn_w_o, cf_conv_w,
                           cf_ln_g, cf_ln_b, cf_w_o, w_out, norm2_g, ffn_w_up, ffn_conv_w, ffn_w_down, final_norm_g]))
    ms = dict(zip(names, [m_w_ada, m_b_ada, m_norm1_g, m_w_in, m_dn_conv_w, m_dn_a_log, m_dn_dt_bias, m_dn_norm_g,
                          m_dn_w_o, m_cf_conv_w, m_cf_ln_g, m_cf_ln_b, m_cf_w_o, m_w_out, m_norm2_g, m_ffn_w_up,
                          m_ffn_conv_w, m_ffn_w_down, m_final_norm_g]))
    vs = dict(zip(names, [v_w_ada, v_b_ada, v_norm1_g, v_w_in, v_dn_conv_w, v_dn_a_log, v_dn_dt_bias, v_dn_norm_g,
                          v_dn_w_o, v_cf_conv_w, v_cf_ln_g, v_cf_ln_b, v_cf_w_o, v_w_out, v_norm2_g, v_ffn_w_up,
                          v_ffn_conv_w, v_ffn_w_down, v_final_norm_g]))
    grads, delta, new_m, new_v = {}, {}, {}, {}

    def adam_large(n, g):
        grads[n] = g.reshape(wts[n].shape)
        d_, m_, v_ = _adamw(wts[n][0], grads[n][0], ms[n][0], vs[n][0], "adamw_" + n)
        delta[n], new_m[n], new_v[n] = d_[None], m_[None], v_[None]

    def rs_finish(groups, after, tag):
        hs, nms_all = [], []
        for state, nms, t in groups:
            q16, r3s = _chip_scatter_wait(state, after, "rs_chips_wait_" + t)
            for q, r3, nm in zip(q16, r3s, nms):
                hs.append(_chip_sum(q, r3, slots, "rs_chip_sum_" + nm))
                nms_all.append(nm)
        for nm, g in zip(nms_all, _pair_gather(hs, "rs_pair_gather_" + tag)):
            adam_large(nm, g)

    rs_finish(rs_groups[:3], grad_x, "abc")
    rs_finish(rs_groups[3:], delta["ffn_w_up"], "d")

    dmod = jnp.concatenate([dsh1, dsc1, dgt1, dsh2, dsc2, dgt2], axis=1)
    sm2 = [dmod, dg1, dg2, dgf, g_alp, g_dtb, g_dnng, g_cflg, g_cflb, g_dncw, g_cfcw, g_ffcw]
    sm2_shapes = [tuple(a.shape) for a in sm2]
    g3 = _all_gather_small(_pack_small(sm2))
    ssum = _sum_slots(g3, "small_sum")
    dmod_all = _unpack_small(g3, sm2_shapes[:1])[0].reshape(8, 6 * D)
    (g_b_ada, gs_n1, gs_n2, gs_fn, gs_alp, gs_dtb, gs_dnng, gs_cflg, gs_cflb, gs_dncw, gs_cfcw,
     gs_ffcw) = _unpack_small(ssum, sm2_shapes)
    gs_alog, gs_dtbias = gs_alp[:, NH:2 * NH], gs_dtb[:, NH:2 * NH]
    gs_dncw = lax.dynamic_slice(gs_dncw, (0, chip * (3 * DNW // 4)), (KDN, 3 * DNW // 4))
    gs_cfcw = lax.dynamic_slice(gs_cfcw, (0, chip * (CFW // 4)), (KCF, CFW // 4))
    gs_ffcw = lax.dynamic_slice(gs_ffcw, (0, chip * (FFN // 4)), (KFF, FFN // 4))

    dmod_sh = lax.dynamic_slice(dmod_all, (0, chip * CA), (8, CA))

    def wada_fn(cc, dm):
        return (_mm(_silu(cc), dm, TN),)

    g_w_ada = _ew(wada_fn, "ada_dw", (CA // tn_a,), [c_all, dmod_sh],
                  [_full((8, D)), pl.BlockSpec((8, tn_a), lambda j: (0, j))],
                  [((D, CA), F32, pl.BlockSpec((D, tn_a), lambda j: (0, j)))])[0]

    loss = lax.psum(loss_v[0, 0], ("x", "y", "c"))

    adam_large("w_ada", g_w_ada)
    small_grads = {"b_ada": g_b_ada, "norm1_g": gs_n1, "dn_conv_w": gs_dncw, "dn_a_log": gs_alog,
                   "dn_dt_bias": gs_dtbias, "dn_norm_g": gs_dnng, "cf_conv_w": gs_cfcw, "cf_ln_g": gs_cflg,
                   "cf_ln_b": gs_cflb, "norm2_g": gs_n2, "ffn_conv_w": gs_ffcw, "final_norm_g": gs_fn}
    small = [n for n in names if n in small_grads]
    for n in small:
        grads[n] = small_grads[n].reshape(wts[n].shape)
    sm_sh = [tuple(wts[n].shape) for n in small]
    d_, m_, v_ = _adamw(_pack_small([wts[n] for n in small]), _pack_small([grads[n] for n in small]),
                        _pack_small([ms[n] for n in small]), _pack_small([vs[n] for n in small]), "adamw_small")
    for n, a, b_, c_ in zip(small, _unpack_small(d_, sm_sh), _unpack_small(m_, sm_sh), _unpack_small(v_, sm_sh)):
        delta[n], new_m[n], new_v[n] = a, b_, c_

    return (loss, grad_x[None], *[grads[n] for n in names], *[delta[n] for n in names],
            *[new_m[n] for n in names], *[new_v[n] for n in names])
```

```python
import functools

import jax
import jax.numpy as jnp
from jax import lax
from jax.experimental import pallas as pl
from jax.experimental.pallas import tpu as pltpu

F32 = jnp.float32
BF16 = jnp.bfloat16
EPS = 1e-6
LANES = 128
VMEM_LIMIT = 48 * 1024 * 1024
DN_CHUNK = 128
ADAM_LR, ADAM_B1, ADAM_B2, ADAM_EPS, ADAM_WD, ADAM_STEP = 0.001, 0.9, 0.999, 1e-08, 0.01, 10
MESH = pl.DeviceIdType.MESH

NN = (((1,), (0,)), ((), ()))
NT = (((1,), (1,)), ((), ()))
TN = (((0,), (0,)), ((), ()))
_DIMS = {"nn": NN, "nt": NT, "tn": TN}


def _mm(a, b, dims):
    return lax.dot_general(a.astype(BF16), b.astype(BF16), dims, preferred_element_type=F32)


def _mmx(a, b, dims):
    return lax.dot_general(a, b, dims, precision=lax.Precision.HIGH, preferred_element_type=F32)


def _div_tile(n, target, mult):
    best = None
    t = mult
    while t <= min(n, target):
        if n % t == 0:
            best = t
        t += mult
    return best if best is not None else n


def _params(sem=None):
    kw = dict(vmem_limit_bytes=VMEM_LIMIT)
    if sem is not None:
        kw["dimension_semantics"] = sem
    return pltpu.CompilerParams(**kw)


def _sigmoid(x):
    return jax.nn.sigmoid(x)


def _silu(x):
    return x * jax.nn.sigmoid(x)


def _softplus(x):
    return jnp.maximum(x, 0.0) + jnp.log(1.0 + jnp.exp(-jnp.abs(x)))


def _view(arr):
    if arr.ndim == 3:
        return arr.shape[1], arr.shape[0] * arr.shape[2], arr.shape[0]
    return arr.shape[0], arr.shape[1], 1


def _tile_spec(groups, cols, tr, tc, rsel, csel):
    if groups > 1:
        per = cols // groups // tc
        return pl.BlockSpec((None, tr, tc), lambda i, j, k: (csel(i, j, k) // per, rsel(i, j, k), csel(i, j, k) % per))
    return pl.BlockSpec((tr, tc), lambda i, j, k: (rsel(i, j, k), csel(i, j, k)))


def _matmul(a, b, mode, out_dtype, name, out_groups=1, after=()):
    ar, ac, ag = _view(a)
    br, bc, bg = _view(b)
    if mode == "nn":
        M, K, N = ar, ac, bc
        kdiv, mdiv, ndiv = ac // ag, M, min(bc // bg, N // out_groups)
    elif mode == "nt":
        M, K, N = ar, ac, br
        kdiv, mdiv, ndiv = min(ac // ag, bc // bg), M, N // out_groups
    else:
        K, M, N = ar, ac, bc
        kdiv, mdiv, ndiv = K, ac // ag, min(bc // bg, N // out_groups)
    tm = _div_tile(mdiv, 1024, LANES)
    tn = _div_tile(ndiv, 1536, LANES)
    tk = _div_tile(kdiv, 2048, LANES)
    nk = K // tk
    dims = _DIMS[mode]
    si, sj, sk = (lambda i, j, k: i), (lambda i, j, k: j), (lambda i, j, k: k)
    a_spec = {"nn": _tile_spec(ag, ac, tm, tk, si, sk), "nt": _tile_spec(ag, ac, tm, tk, si, sk),
              "tn": _tile_spec(ag, ac, tk, tm, sk, si)}[mode]
    b_spec = {"nn": _tile_spec(bg, bc, tk, tn, sk, sj), "nt": _tile_spec(bg, bc, tn, tk, sj, sk),
              "tn": _tile_spec(bg, bc, tk, tn, sk, sj)}[mode]
    out_shape = (M, N) if out_groups == 1 else (out_groups, M, N // out_groups)

    n_after = len(after)

    def body(*refs):
        a_ref, b_ref, o_ref = refs[0], refs[1], refs[2 + n_after]
        if nk == 1:
            o_ref[...] = lax.dot_general(a_ref[...], b_ref[...], dims, preferred_element_type=F32).astype(o_ref.dtype)
            return
        acc_ref = refs[3 + n_after]
        k = pl.program_id(2)

        @pl.when(k == 0)
        def _():
            acc_ref[...] = jnp.zeros_like(acc_ref)

        acc_ref[...] += lax.dot_general(a_ref[...], b_ref[...], dims, preferred_element_type=F32)

        @pl.when(k == nk - 1)
        def _():
            o_ref[...] = acc_ref[...].astype(o_ref.dtype)

    return pl.pallas_call(
        body, name=name, grid=(M // tm, N // tn, nk),
        in_specs=[a_spec, b_spec] + [pl.BlockSpec(memory_space=pl.ANY)] * n_after,
        out_specs=_tile_spec(out_groups, N, tm, tn, si, sj),
        out_shape=jax.ShapeDtypeStruct(out_shape, out_dtype),
        scratch_shapes=[pltpu.VMEM((tm, tn), F32)] if nk > 1 else [],
        compiler_params=_params(("parallel", "parallel", "arbitrary")),
    )(a, b, *after)


def _ew(fn, name, grid, ins, in_specs, outs, acc=(), alias=None, after=()):
    n_in = len(ins)
    n_ax = len(grid)
    extra, aliases = list(after), {}
    if alias is not None:
        extra, aliases = extra + [alias[0]], {n_in + len(after): alias[1]}
    extra_specs = [pl.BlockSpec(memory_space=pl.ANY)] * len(extra)

    def body(*refs):
        in_refs, out_refs = refs[:n_in], refs[n_in + len(extra):]
        ids = [pl.program_id(a) for a in range(n_ax)]
        res = fn(*[r[...] for r in in_refs])
        first = ids[0] == 0
        for t in ids[1:]:
            first = jnp.logical_and(first, t == 0)
        for idx, (r, val) in enumerate(zip(out_refs, res)):
            if idx in acc:
                @pl.when(first)
                def _(r=r, val=val):
                    r[...] = val.astype(r.dtype)

                @pl.when(jnp.logical_not(first))
                def _(r=r, val=val):
                    r[...] += val.astype(r.dtype)
            elif isinstance(val, tuple):
                for t, part in enumerate(val):
                    r[t] = part.astype(r.dtype)
            else:
                r[...] = val.astype(r.dtype)

    return pl.pallas_call(
        body, name=name, grid=grid, in_specs=list(in_specs) + extra_specs,
        out_specs=[o[2] for o in outs],
        out_shape=[jax.ShapeDtypeStruct(o[0], o[1]) for o in outs],
        input_output_aliases=aliases,
        compiler_params=_params(("arbitrary",) * n_ax),
    )(*ins, *extra)


def _ew_slot(fn, name, grid, slots, ins, in_specs, out_shape, out_dtype, out_spec):
    def body(s_ref, *refs):
        refs[-1][...] = fn(*[r[...] for r in refs[:-1]]).astype(refs[-1].dtype)

    return pl.pallas_call(
        body, name=name,
        grid_spec=pltpu.PrefetchScalarGridSpec(num_scalar_prefetch=1, grid=grid, in_specs=list(in_specs),
                                               out_specs=out_spec),
        out_shape=jax.ShapeDtypeStruct(out_shape, out_dtype),
        compiler_params=_params(("arbitrary",) * len(grid)),
    )(slots, *ins)


def _row(bs, w, col=0):
    return pl.BlockSpec((bs, w), lambda i, col=col: (i, col))


def _full(shape):
    nd = len(shape)
    return pl.BlockSpec(tuple(shape), lambda *_: (0,) * nd)


def _rms(x, g):
    return x * lax.rsqrt(jnp.mean(x * x, axis=-1, keepdims=True) + EPS) * g


def _f_normmod(x, g, sc, sh):
    return _rms(x, g) * (1.0 + sc) + sh


def _f_res_normmod(x, mix, gt, g, sc, sh):
    x1 = x + gt * mix
    return x1, _f_normmod(x1, g, sc, sh)


def _f_loss(x1, f, gt, gf, tgt):
    y = _rms(x1 + gt * f, gf)
    return 0.5 * jnp.sum(jnp.mean(jnp.square(y - tgt), axis=-1))


def _f_dn_gate(nh, ba, alp, dtb):
    lane = lax.broadcasted_iota(jnp.int32, ba.shape, 1)
    m = (lane < nh).astype(F32)
    beta = _sigmoid(ba)
    g = -jnp.exp(alp) * _softplus(ba + dtb)
    return m * beta + (1.0 - m) * g


def _f_dn_post(o, z, g):
    return o * lax.rsqrt(jnp.mean(o * o, axis=-1, keepdims=True) + EPS) * g * _silu(z)


def _f_cf_ln(u, g, b):
    mu = jnp.mean(u, axis=-1, keepdims=True)
    xc = u - mu
    y = xc * lax.rsqrt(jnp.mean(xc * xc, axis=-1, keepdims=True) + EPS)
    return _silu(y * g + b)


def _f_merge(a, b, ga, gb):
    return _sigmoid(ga) * a + _sigmoid(gb) * b


def _shift_down(u, d, rows):
    if d == 0:
        return u
    return jnp.where(rows >= d, pltpu.roll(u, d, 0), 0.0)


def _shift_up(u, d, rows):
    if d == 0:
        return u
    s = u.shape[0]
    return jnp.where(rows < s - d, pltpu.roll(u, s - d, 0), 0.0)


def _conv(u, w_ref, kw, rows):
    acc = None
    for k in range(kw):
        t = w_ref[k:k + 1, :] * _shift_down(u, kw - 1 - k, rows)
        acc = t if acc is None else acc + t
    return acc


def _conv_t(dc, w_ref, kw, rows):
    acc = None
    for k in range(kw):
        t = w_ref[k:k + 1, :] * _shift_up(dc, kw - 1 - k, rows)
        acc = t if acc is None else acc + t
    return acc


def _conv_fwd_call(name, kw, ncol, ins, in_specs, extras, extra_specs, w, w_spec, pre, post, out_shape, out_spec):
    n_in, n_ex = len(ins), len(extras)

    def body(*refs):
        in_refs, ex_refs = refs[:n_in], refs[n_in:n_in + n_ex]
        w_ref, out_ref = refs[n_in + n_ex], refs[n_in + n_ex + 1]
        j = pl.program_id(0)
        u = pre(*[r[...] for r in in_refs])
        rows = lax.broadcasted_iota(jnp.int32, u.shape, 0)
        cv = _conv(u, w_ref, kw, rows)
        out_ref[...] = post(j, cv, *[r[...] for r in ex_refs]).astype(out_ref.dtype)

    return pl.pallas_call(
        body, name=name, grid=(ncol,), in_specs=list(in_specs) + list(extra_specs) + [w_spec],
        out_specs=out_spec, out_shape=out_shape, compiler_params=_params(("arbitrary",)),
    )(*ins, *extras, w)


def _conv_bwd_call(name, kw, ncol, ins, in_specs, extras, extra_specs, w, w_spec, pre, post, dout, dout_spec,
                   dio_shapes, dio_specs, dio_pack, dw_shape, dw_spec, alias=None, after=()):
    n_in, n_ex, n_io = len(ins), len(extras), len(dio_shapes)
    al, aliases = list(after), {}
    if alias is not None:
        al, aliases = al + [alias], {n_in + n_ex + 2 + len(after): 0}
    al_specs = [pl.BlockSpec(memory_space=pl.ANY)] * len(al)

    def body(*refs):
        in_refs, ex_refs = refs[:n_in], refs[n_in:n_in + n_ex]
        w_ref, dout_ref = refs[n_in + n_ex], refs[n_in + n_ex + 1]
        outs = refs[n_in + n_ex + 2 + len(al):]
        dio_refs, dw_ref = outs[:n_io], outs[n_io]
        j = pl.program_id(0)
        u, pre_vjp = jax.vjp(pre, *[r[...] for r in in_refs])
        rows = lax.broadcasted_iota(jnp.int32, u.shape, 0)
        cv = _conv(u, w_ref, kw, rows)
        _, post_vjp = jax.vjp(lambda cc, *ex: post(j, cc, *ex), cv, *[r[...] for r in ex_refs])
        g = post_vjp(dout_ref[...].astype(F32))
        dc = g[0]
        for k in range(kw):
            dw_ref[k:k + 1, :] = jnp.sum(dc * _shift_down(u, kw - 1 - k, rows), axis=0, keepdims=True)
        du = _conv_t(dc, w_ref, kw, rows)
        for r, val in zip(dio_refs, dio_pack(*pre_vjp(du), *g[1:])):
            if isinstance(val, tuple):
                for t, part in enumerate(val):
                    r[t] = part.astype(r.dtype)
            else:
                r[...] = val.astype(r.dtype)

    return pl.pallas_call(
        body, name=name, grid=(ncol,),
        in_specs=list(in_specs) + list(extra_specs) + [w_spec, dout_spec] + al_specs,
        out_specs=list(dio_specs) + [dw_spec],
        out_shape=list(dio_shapes) + [dw_shape],
        input_output_aliases=aliases,
        compiler_params=_params(("arbitrary",)),
    )(*ins, *extras, w, dout, *al)


def _tri_inverse(a):
    c = a.shape[0]
    ii = lax.broadcasted_iota(jnp.int32, (c, c), 0)
    jj = lax.broadcasted_iota(jnp.int32, (c, c), 1)
    eye = (ii == jj).astype(F32)

    def same_block(bits):
        return jnp.right_shift(ii, bits) == jnp.right_shift(jj, bits)

    d = jnp.where(same_block(3), a, 0.0)
    d2 = _mmx(d, d, NN)
    t = (eye - d) + _mmx(eye - d, d2, NN)
    t = t + _mmx(t, _mmx(d2, d2, NN), NN)
    bits = 3
    while (1 << bits) < c:
        low = jnp.where(jnp.logical_and(same_block(bits + 1), jnp.logical_not(same_block(bits))), a, 0.0)
        t = t - _mmx(_mmx(t, low, NN), t, NN)
        bits += 1
    return t


def _dn_common(q, k, v, gb, bb):
    c = q.shape[0]
    ii = lax.broadcasted_iota(jnp.int32, (c, c), 0)
    jj = lax.broadcasted_iota(jnp.int32, (c, c), 1)
    causal = jj <= ii
    strict = jj < ii
    low = causal.astype(F32)
    ones = jnp.ones((c, LANES), F32)
    gc = _mmx(low, gb, NN)
    diff = (_mmx(gc, ones, NT) - _mmx(ones, gc, NT)) * (1.0 / LANES)
    decay = jnp.where(causal, jnp.exp(jnp.where(causal, diff, 0.0)), 0.0)
    gl = jnp.sum(gb, axis=0, keepdims=True)
    eg = jnp.exp(gc)
    egm = jnp.exp(gl - gc)
    egl = jnp.exp(gl)
    kb = k * bb
    vb = v * bb
    kbg = kb * eg
    kk = _mm(kb, k, NT)
    t = _tri_inverse(jnp.where(strict, kk * decay, 0.0))
    qk = _mm(q, k, NT)
    attn = qk * decay
    return dict(causal=causal, strict=strict, low=low, ones=ones, decay=decay, eg=eg, egm=egm, egl=egl,
                kb=kb, vb=vb, kbg=kbg, kk=kk, t=t, qk=qk, attn=attn, qd=q * eg, kd=k * egm)


def _dn_fwd_chunk(q, k, v, gb, bb, s):
    m = _dn_common(q, k, v, gb, bb)
    u = _mmx(m["t"], m["vb"] - _mm(m["kbg"], s, NN), NN)
    o = _mm(m["qd"], s, NN) + _mm(m["attn"], u, NN)
    s2 = s * m["egl"] + _mm(m["kd"], u, TN)
    return o, s2


def _dn_bwd_chunk(q, k, v, gb, bb, s, do, dsp):
    m = _dn_common(q, k, v, gb, bb)
    c = q.shape[0]
    t, decay, eg, egm, egl = m["t"], m["decay"], m["eg"], m["egm"], m["egl"]
    u = _mmx(t, m["vb"] - _mm(m["kbg"], s, NN), NN)
    du = _mm(m["attn"], do, TN) + _mm(m["kd"], dsp, NN)
    dattn = jnp.where(m["causal"], _mm(do, u, NT), 0.0)
    dqd = _mm(do, s, NT)
    dkd = _mm(u, dsp, NT)
    dr = _mmx(t, du, TN)
    da = jnp.where(m["strict"], -_mm(dr, u, NT), 0.0)
    dkbg = -_mm(dr, s, NT)
    ds = dsp * egl + _mm(m["qd"], do, TN) - _mm(m["kbg"], dr, TN)
    degl = jnp.sum(jnp.sum(dsp * s, axis=1, keepdims=True), axis=0, keepdims=True)
    dkk = da * decay
    dqk = dattn * decay
    ddiff = (da * m["kk"] + dattn * m["qk"]) * decay
    dgc = _mmx(ddiff, m["ones"], NN) - _mmx(ddiff, m["ones"], TN)
    dkb = _mm(dkk, k, NN) + dkbg * eg
    dk = _mm(dkk, m["kb"], TN) + _mm(dqk, q, TN) + dkd * egm + dkb * bb
    dq = _mm(dqk, k, NN) + dqd * eg
    dgc = dgc + jnp.sum(dqd * q + dkbg * m["kb"], axis=-1, keepdims=True) * eg
    tt = jnp.sum(dkd * k, axis=-1, keepdims=True) * egm
    dgc = dgc - tt
    dgl = jnp.sum(tt, axis=0, keepdims=True) + degl * egl
    dbb = jnp.sum(dkb * k + dr * v, axis=-1, keepdims=True) + jnp.zeros((c, LANES), F32)
    dv = dr * bb
    dgb = _mmx(m["low"], dgc, TN) + dgl
    return dq, dk, dv, dgb, dbb, ds


def _dn_fwd_call(qkvn, gb, bb):
    _, nh, s, dh = qkvn.shape
    c = min(DN_CHUNK, s)
    n = s // c
    hb = nh

    def body(q_ref, k_ref, v_ref, g_ref, b_ref, o_ref, st_ref, s_ref):
        @pl.when(pl.program_id(1) == 0)
        def _():
            s_ref[...] = jnp.zeros_like(s_ref)

        for h in range(hb):
            st = s_ref[h]
            st_ref[h] = st
            o, s2 = _dn_fwd_chunk(q_ref[h], k_ref[h], v_ref[h], g_ref[h], b_ref[h], st)
            o_ref[h] = o
            s_ref[h] = s2

    def qspec(t):
        return pl.BlockSpec((None, hb, c, dh), lambda i, j, t=t: (t, i, j, 0))

    hs = pl.BlockSpec((hb, c, dh), lambda i, j: (i, j, 0))
    return pl.pallas_call(
        body, name="dn_fwd", grid=(nh // hb, n),
        in_specs=[qspec(0), qspec(1), qspec(2), hs, hs],
        out_specs=[hs, pl.BlockSpec((hb, None, dh, dh), lambda i, j: (i, j, 0, 0))],
        out_shape=[jax.ShapeDtypeStruct((nh, s, dh), F32), jax.ShapeDtypeStruct((nh, n, dh, dh), F32)],
        scratch_shapes=[pltpu.VMEM((hb, dh, dh), F32)],
        compiler_params=_params(("arbitrary", "arbitrary")),
    )(qkvn, qkvn, qkvn, gb, bb)


def _dn_bwd_call(qkvn, gb, bb, states, do):
    _, nh, s, dh = qkvn.shape
    c = min(DN_CHUNK, s)
    n = s // c
    hb = nh

    def body(q_ref, k_ref, v_ref, g_ref, b_ref, st_ref, do_ref, dqkv_ref, dg_ref, db_ref, ds_ref):
        @pl.when(pl.program_id(1) == 0)
        def _():
            ds_ref[...] = jnp.zeros_like(ds_ref)

        for h in range(hb):
            dq, dk, dv, dg, db, ds = _dn_bwd_chunk(q_ref[h], k_ref[h], v_ref[h], g_ref[h], b_ref[h], st_ref[h],
                                                   do_ref[h], ds_ref[h])
            dqkv_ref[0, h] = dq
            dqkv_ref[1, h] = dk
            dqkv_ref[2, h] = dv
            dg_ref[h] = dg
            db_ref[h] = db
            ds_ref[h] = ds

    def qspec(t):
        return pl.BlockSpec((None, hb, c, dh), lambda i, j, t=t: (t, i, n - 1 - j, 0))

    hs = pl.BlockSpec((hb, c, dh), lambda i, j: (i, n - 1 - j, 0))
    sh = jax.ShapeDtypeStruct((nh, s, dh), F32)
    return pl.pallas_call(
        body, name="dn_bwd", grid=(nh // hb, n),
        in_specs=[qspec(0), qspec(1), qspec(2), hs, hs,
                  pl.BlockSpec((hb, None, dh, dh), lambda i, j: (i, n - 1 - j, 0, 0)), hs],
        out_specs=[pl.BlockSpec((3, hb, c, dh), lambda i, j: (0, i, n - 1 - j, 0)), hs, hs],
        out_shape=[jax.ShapeDtypeStruct((3, nh, s, dh), F32), sh, sh],
        scratch_shapes=[pltpu.VMEM((hb, dh, dh), F32)],
        compiler_params=_params(("arbitrary", "arbitrary")),
    )(qkvn, qkvn, qkvn, gb, bb, states, do)


def _adamw(w, g, m, v, name):
    r, c = w.shape
    br = _div_tile(r, max(8, (1 << 18) // max(c, 1)), 8)

    def fn(w, g, m, v):
        m = ADAM_B1 * m + (1.0 - ADAM_B1) * g
        v = ADAM_B2 * v + (1.0 - ADAM_B2) * jnp.square(g)
        m_hat = m / (1.0 - ADAM_B1 ** ADAM_STEP)
        v_hat = v / (1.0 - ADAM_B2 ** ADAM_STEP)
        delta = -ADAM_LR * (m_hat / (jnp.sqrt(v_hat) + ADAM_EPS) + ADAM_WD * w)
        return delta, m, v

    spec = pl.BlockSpec((br, c), lambda i: (i, 0))
    return _ew(fn, name, (r // br,), [w, g, m, v], [spec] * 4, [((r, c), F32, spec)] * 3)


def _coords():
    return lax.axis_index("x"), lax.axis_index("y"), lax.axis_index("c")


def _all_gather_small(v):
    r, w = v.shape

    def body(v_ref, out_ref, send_sems, recv_sems, local_sem):
        x, y, c = _coords()
        me = 4 * x + 2 * y + c
        mine = pltpu.make_async_copy(v_ref, out_ref.at[me], local_sem)
        mine.start()
        peers = []
        for k in range(1, 8):
            px = 1 - x if k & 4 else x
            py = 1 - y if k & 2 else y
            pc = 1 - c if k & 1 else c
            peers.append((px, py, pc))
        sends = []
        for k, peer in enumerate(peers):
            cp = pltpu.make_async_remote_copy(src_ref=v_ref, dst_ref=out_ref.at[me], send_sem=send_sems.at[k],
                                              recv_sem=recv_sems.at[k], device_id=peer, device_id_type=MESH)
            cp.start()
            sends.append(cp)
        for k, (px, py, pc) in enumerate(peers):
            pltpu.make_async_remote_copy(src_ref=v_ref, dst_ref=out_ref.at[4 * px + 2 * py + pc],
                                         send_sem=send_sems.at[k], recv_sem=recv_sems.at[k],
                                         device_id=(px, py, pc), device_id_type=MESH).wait_recv()
        for cp in sends:
            cp.wait_send()
        mine.wait()

    return pl.pallas_call(
        body, name="ag_small", out_shape=jax.ShapeDtypeStruct((8, r, w), v.dtype),
        in_specs=[pl.BlockSpec(memory_space=pltpu.VMEM)], out_specs=pl.BlockSpec(memory_space=pltpu.VMEM),
        scratch_shapes=[pltpu.SemaphoreType.DMA((7,)), pltpu.SemaphoreType.DMA((7,)), pltpu.SemaphoreType.DMA],
        compiler_params=pltpu.CompilerParams(vmem_limit_bytes=VMEM_LIMIT),
    )(v)


def _hbm_call(body, name, arrays, out_shapes, n_sems, aliases=None):
    hbm = pl.BlockSpec(memory_space=pltpu.HBM)
    return pl.pallas_call(
        body, name=name, out_shape=list(out_shapes), in_specs=[hbm] * len(arrays), out_specs=[hbm] * len(out_shapes),
        input_output_aliases=aliases or {},
        scratch_shapes=[pltpu.SemaphoreType.DMA((n_sems,)), pltpu.SemaphoreType.DMA((n_sems,))],
    )(*arrays)


def _half_rows(ref_shape, c):
    h = ref_shape[1] // 2
    return pl.ds(pl.multiple_of(c * h, 16), h), pl.ds(pl.multiple_of((1 - c) * h, 16), h)


def _pair_swap_half(ps, name):
    n = len(ps)

    def body(*refs):
        ins, outs, send_sems, recv_sems = refs[:n], refs[n:2 * n], refs[2 * n], refs[2 * n + 1]
        x, y, c = _coords()
        cps = []
        for w in range(n):
            _, other = _half_rows(ins[w].shape, c)
            cp = pltpu.make_async_remote_copy(src_ref=ins[w].at[:, other], dst_ref=outs[w], send_sem=send_sems.at[w],
                                              recv_sem=recv_sems.at[w], device_id=(x, y, 1 - c), device_id_type=MESH)
            cp.start()
            cps.append(cp)
        for cp in cps:
            cp.wait()

    shapes = [jax.ShapeDtypeStruct((p.shape[0], p.shape[1] // 2, p.shape[2]), p.dtype) for p in ps]
    return _hbm_call(body, name, ps, shapes, n)


def _pair_gather(gs, name):
    n = len(gs)

    def body(*refs):
        ins, outs, send_sems, recv_sems = refs[:n], refs[n:2 * n], refs[2 * n], refs[2 * n + 1]
        x, y, c = _coords()
        cps = []
        for w in range(n):
            cp = pltpu.make_async_remote_copy(src_ref=ins[w].at[c], dst_ref=outs[w].at[c], send_sem=send_sems.at[w],
                                              recv_sem=recv_sems.at[w], device_id=(x, y, 1 - c), device_id_type=MESH)
            cp.start()
            cps.append(cp)
        for w, cp in enumerate(cps):
            pltpu.make_async_remote_copy(src_ref=ins[w].at[c], dst_ref=outs[w].at[1 - c], send_sem=send_sems.at[w],
                                         recv_sem=recv_sems.at[w], device_id=(x, y, 1 - c),
                                         device_id_type=MESH).wait_recv()
            cp.wait_send()

    shapes = [jax.ShapeDtypeStruct(g.shape, g.dtype) for g in gs]
    return _hbm_call(body, name, gs, shapes, n, aliases={i: i for i in range(n)})


_HBM = pl.BlockSpec(memory_space=pltpu.HBM)
_SEM = pl.BlockSpec(memory_space=pltpu.SEMAPHORE)
_EFFECT = pltpu.SideEffectType.DATAFLOW_SIDE_EFFECTING


def _split_start(name, srcs, lands, after, n_copies, make_copies):
    n, m = len(srcs), len(lands)
    arrays = [pltpu.with_memory_space_constraint(a, pltpu.HBM) for a in list(srcs) + list(lands)]

    def body(*refs):
        src_refs, land_refs = refs[:n], refs[n:n + m]
        send_sems, recv_sems = refs[n + m + 1], refs[n + m + 2]
        for cp in make_copies(src_refs, land_refs, send_sems, recv_sems):
            cp.start()
        refs[-1][...] = jnp.zeros_like(refs[-1])

    outs = pl.pallas_call(
        body, name=name,
        out_shape=(pltpu.SemaphoreType.DMA((n_copies,)), pltpu.SemaphoreType.DMA((n_copies,)),
                   *[pltpu.HBM(a.shape, a.dtype) for a in arrays], jax.ShapeDtypeStruct((8, LANES), F32)),
        in_specs=[_HBM] * (n + m) + [pl.BlockSpec(memory_space=pl.ANY)],
        out_specs=(_SEM, _SEM, *[_HBM] * (n + m), pl.BlockSpec(memory_space=pltpu.VMEM)),
        input_output_aliases={i: 2 + i for i in range(n + m)},
        compiler_params=pltpu.CompilerParams(has_side_effects=_EFFECT),
    )(*arrays, after)
    return (outs[0], outs[1], list(outs[2:2 + n]), list(outs[2 + n:2 + n + m])), outs[-1]


def _split_wait(name, state, after, make_copies):
    send_sems, recv_sems, srcs, lands = state
    n, m = len(srcs), len(lands)

    def body(*refs):
        src_refs, land_refs = refs[:n], refs[n:n + m]
        for cp in make_copies(src_refs, land_refs, refs[n + m], refs[n + m + 1]):
            cp.wait_send()
            cp.wait_recv()

    outs = pl.pallas_call(
        body, name=name, out_shape=tuple(pltpu.HBM(a.shape, a.dtype) for a in srcs + lands),
        in_specs=[_HBM] * (n + m) + [_SEM, _SEM, pl.BlockSpec(memory_space=pl.ANY)], out_specs=tuple([_HBM] * (n + m)),
        input_output_aliases={i: i for i in range(n + m)},
        compiler_params=pltpu.CompilerParams(has_side_effects=_EFFECT),
    )(*srcs, *lands, send_sems, recv_sems, after)
    return list(outs[:n]), list(outs[n:])


def _scatter_copies(q_refs, land_refs, send_sems, recv_sems):
    x, y, c = _coords()
    cps = []
    for w, (q, land) in enumerate(zip(q_refs, land_refs)):
        for k, (px, py) in enumerate([(1 - x, y), (x, 1 - y), (1 - x, 1 - y)]):
            cps.append(pltpu.make_async_remote_copy(src_ref=q.at[2 * px + py], dst_ref=land.at[k],
                                                    send_sem=send_sems.at[3 * w + k], recv_sem=recv_sems.at[3 * w + k],
                                                    device_id=(px, py, c), device_id_type=MESH))
    return cps


def _chip_scatter_start(qs, name):
    lands = [lax.empty((3,) + q.shape[1:], q.dtype) for q in qs]
    return _split_start(name, qs, lands, qs[0], 3 * len(qs), _scatter_copies)


def _chip_scatter_wait(state, after, name):
    return _split_wait(name, state, after, _scatter_copies)


def _gather_copies(src_refs, buf_refs, send_sems, recv_sems):
    x, y, c = _coords()
    j = 2 * x + y
    cps = []
    for w, buf in enumerate(buf_refs):
        mine, _ = _half_rows(buf.shape, c)
        for k, (px, py) in enumerate([(1 - x, y), (x, 1 - y), (1 - x, 1 - y)]):
            cps.append(pltpu.make_async_remote_copy(src_ref=buf.at[j, mine], dst_ref=buf.at[j, mine],
                                                    send_sem=send_sems.at[3 * w + k], recv_sem=recv_sems.at[3 * w + k],
                                                    device_id=(px, py, c), device_id_type=MESH))
    return cps


def _gather_wait_copies(src_refs, buf_refs, send_sems, recv_sems):
    x, y, c = _coords()
    j = 2 * x + y
    cps = []
    for w, buf in enumerate(buf_refs):
        mine, _ = _half_rows(buf.shape, c)
        for k, (px, py) in enumerate([(1 - x, y), (x, 1 - y), (1 - x, 1 - y)]):
            cps.append(pltpu.make_async_remote_copy(src_ref=buf.at[j, mine], dst_ref=buf.at[2 * px + py, mine],
                                                    send_sem=send_sems.at[3 * w + k], recv_sem=recv_sems.at[3 * w + k],
                                                    device_id=(px, py, c), device_id_type=MESH))
    return cps


def _pair_forward(bufs, name):
    n = len(bufs)

    def body(*refs):
        ins, outs, send_sems, recv_sems = refs[:n], refs[n:2 * n], refs[2 * n], refs[2 * n + 1]
        x, y, c = _coords()
        chips = [(1 - x, y), (x, 1 - y), (1 - x, 1 - y)]
        cps = []
        for w in range(n):
            mine, _ = _half_rows(outs[w].shape, c)
            for k, (px, py) in enumerate(chips):
                cp = pltpu.make_async_remote_copy(src_ref=ins[w].at[2 * px + py, mine],
                                                  dst_ref=outs[w].at[2 * px + py, mine],
                                                  send_sem=send_sems.at[3 * w + k], recv_sem=recv_sems.at[3 * w + k],
                                                  device_id=(x, y, 1 - c), device_id_type=MESH)
                cp.start()
                cps.append(cp)
        for w in range(n):
            _, sib = _half_rows(outs[w].shape, c)
            for k, (px, py) in enumerate(chips):
                pltpu.make_async_remote_copy(src_ref=ins[w].at[2 * px + py, sib], dst_ref=outs[w].at[2 * px + py, sib],
                                             send_sem=send_sems.at[3 * w + k], recv_sem=recv_sems.at[3 * w + k],
                                             device_id=(x, y, 1 - c), device_id_type=MESH).wait_recv()
        for cp in cps:
            cp.wait_send()

    shapes = [jax.ShapeDtypeStruct(b.shape, b.dtype) for b in bufs]
    return _hbm_call(body, name, bufs, shapes, 3 * n, aliases={i: i for i in range(n)})


def _slot_rows(h, cs):
    return _div_tile(h, max(16, (1 << 19) // cs), 16)


def _cast_into_slot(w, slot, name):
    r, cs = w.shape
    br = _slot_rows(r, cs)
    return _ew_slot(lambda a: a, name, (r // br,), slot, [w], [pl.BlockSpec((br, cs), lambda i, s: (i, 0))],
                    (4, r, cs), BF16, pl.BlockSpec((None, br, cs), lambda i, s: (s[0], i, 0)))


def _pair_add(p, rb, core, name):
    n, r, cs = p.shape
    h = r // 2
    br = _slot_rows(h, cs)
    nb = h // br
    return _ew_slot(lambda a, b: a + b, name, (n, nb), core, [p, rb],
                    [pl.BlockSpec((None, br, cs), lambda s, i, c: (s, c[0] * nb + i, 0)),
                     pl.BlockSpec((None, br, cs), lambda s, i, c: (s, i, 0))],
                    (n, h, cs), BF16, pl.BlockSpec((None, br, cs), lambda s, i, c: (s, i, 0)))


def _chip_sum(q, r3, slots, name):
    _, h, cs = q.shape
    br = _slot_rows(h, cs)

    def fn(a, b):
        acc = a.astype(F32)
        for k in range(3):
            acc = acc + b[k].astype(F32)
        return acc

    return _ew_slot(fn, name, (h // br,), slots, [q, r3],
                    [pl.BlockSpec((None, br, cs), lambda i, s: (s[0], i, 0)),
                     pl.BlockSpec((3, br, cs), lambda i, s: (0, i, 0))],
                    (2, h, cs), F32, pl.BlockSpec((None, br, cs), lambda i, s: (s[1], i, 0)))


def _sum_slots(r, name):
    n, h, w = r.shape
    br = _div_tile(h, 2048, 16)

    def fn(blk):
        acc = blk[0].astype(F32)
        for s in range(1, n):
            acc = acc + blk[s].astype(F32)
        return (acc,)

    return _ew(fn, name, (h // br,), [r], [pl.BlockSpec((n, br, w), lambda i: (0, i, 0))],
               [((h, w), F32, pl.BlockSpec((br, w), lambda i: (i, 0)))])[0]


def _pack_small(arrs):
    flat = jnp.concatenate([a.reshape(-1).astype(F32) for a in arrs])
    n = flat.shape[0]
    rows = -(-n // LANES)
    rows = -(-rows // 8) * 8
    return jnp.pad(flat, (0, rows * LANES - n)).reshape(rows, LANES)


def _unpack_small(p, shapes):
    lead = p.shape[:-2]
    flat = p.reshape(lead + (-1,))
    out, off = [], 0
    for sh in shapes:
        n = 1
        for d in sh:
            n *= d
        out.append(flat[..., off:off + n].reshape(lead + tuple(sh)))
        off += n
    return out


def kernel(x, c, w_ada, b_ada, norm1_g, w_in, dn_conv_w, dn_a_log, dn_dt_bias, dn_norm_g, dn_w_o, cf_conv_w, cf_ln_g, cf_ln_b, cf_w_o, w_out, norm2_g, ffn_w_up, ffn_conv_w, ffn_w_down, final_norm_g, loss_target, m_w_ada, m_b_ada, m_norm1_g, m_w_in, m_dn_conv_w, m_dn_a_log, m_dn_dt_bias, m_dn_norm_g, m_dn_w_o, m_cf_conv_w, m_cf_ln_g, m_cf_ln_b, m_cf_w_o, m_w_out, m_norm2_g, m_ffn_w_up, m_ffn_conv_w, m_ffn_w_down, m_final_norm_g, v_w_ada, v_b_ada, v_norm1_g, v_w_in, v_dn_conv_w, v_dn_a_log, v_dn_dt_bias, v_dn_norm_g, v_dn_w_o, v_cf_conv_w, v_cf_ln_g, v_cf_ln_b, v_cf_w_o, v_w_out, v_norm2_g, v_ffn_w_up, v_ffn_conv_w, v_ffn_w_down, v_final_norm_g):
    xi, yi, ci = _coords()
    chip = 2 * xi + yi
    me = 4 * xi + 2 * yi + ci
    core = jnp.reshape(ci, (1,)).astype(jnp.int32)

    S, D = x.shape[1], x.shape[2]
    NH = dn_a_log.shape[1]
    DH = dn_norm_g.shape[1]
    DNW = NH * DH
    CFW = cf_ln_g.shape[1]
    FFN = ffn_w_down.shape[1] * 4
    KDN, KCF, KFF = dn_conv_w.shape[1], cf_conv_w.shape[1], ffn_conv_w.shape[1]
    NIN = w_in.shape[2] * 4
    assert NIN == 4 * DNW + 2 * NH + 2 * CFW + 2 * D and DH == LANES and 2 * NH <= LANES
    x2, tgt = x[0], loss_target[0]

    sm_shapes = [(D,), (KDN, 3 * DNW // 4), (KCF, CFW // 4), (KFF, FFN // 4)]
    g1 = _all_gather_small(_pack_small([c[0], dn_conv_w[0], cf_conv_w[0], ffn_conv_w[0]]))
    c_all, dcw_s, ccw_s, fcw_s = _unpack_small(g1, sm_shapes)

    def chips_cols(t):
        t = t[0::2]
        return jnp.transpose(t, (1, 0, 2)).reshape(t.shape[1], -1)

    dn_cw, cf_cw, ff_cw = chips_cols(dcw_s), chips_cols(ccw_s), chips_cols(fcw_s)

    slot_chip = jnp.reshape(chip, (1,)).astype(jnp.int32)
    big = [w_in[0], dn_w_o[0], cf_w_o[0], w_out[0], ffn_w_up[0], ffn_w_down[0]]
    names_big = ["w_in", "dn_w_o", "cf_w_o", "w_out", "ffn_w_up", "ffn_w_down"]
    in_state, in_token = _split_start("ag_in_start", [], [_cast_into_slot(big[0], slot_chip, "cast_w_in")], c, 3,
                                      _gather_copies)
    bufs = [_cast_into_slot(w, slot_chip, "cast_" + nm) for w, nm in zip(big[1:], names_big[1:])]
    ag_state, ag_token = _split_start("ag_rest_start", [], bufs, in_token, 3 * len(bufs), _gather_copies)

    cb_cf = _div_tile(CFW, 256, LANES)
    o_b = 4 * DNW
    o_glu = o_b + 2 * NH
    o_ga = o_glu + 2 * CFW
    NA = NIN - 2 * NH
    a_z, a_ga, a_gb, a_glu, a_ba = 3 * DNW, 4 * DNW, 4 * DNW + D, 4 * DNW + 2 * D, NA

    segs = [(0, 0, o_b), (o_b, NA, 2 * NH), (o_ga, a_ga, 2 * D)]
    for t in range(CFW // cb_cf):
        segs += [(o_glu + t * cb_cf, a_glu + 2 * t * cb_cf, cb_cf),
                 (o_glu + CFW + t * cb_cf, a_glu + (2 * t + 1) * cb_cf, cb_cf)]
    CS = NIN // 4

    def shard_slices(lo, n):
        out = []
        while n > 0:
            j, off = lo // CS, lo % CS
            m = min(n, CS - off)
            out.append(w_in_g[j][:, off:off + m])
            lo, n = lo + m, n - m
        return out

    CA = w_ada.shape[2]
    b_sh = lax.dynamic_slice(b_ada, (0, chip * CA), (1, CA))
    tn_a = _div_tile(CA, 512, LANES)

    def mod_fn(cc, w, b):
        return (_mm(_silu(cc), w, NN) + b,)

    mod_sh = _ew(mod_fn, "ada_mod", (CA // tn_a,), [c_all, w_ada[0], b_sh],
                 [_full((8, D)), pl.BlockSpec((D, tn_a), lambda j: (0, j)), pl.BlockSpec((1, tn_a), lambda j: (0, j))],
                 [((8, CA), F32, pl.BlockSpec((8, tn_a), lambda j: (0, j)))])[0]
    g2 = _all_gather_small(_pack_small([mod_sh]))
    mod_all = _unpack_small(g2, [(8, CA)])[0][0::2]
    mod_all = jnp.transpose(mod_all, (1, 0, 2)).reshape(8, 4 * CA)
    mod_me = lax.dynamic_slice(mod_all, (me, 0), (1, 6 * D))
    sh1, sc1, gt1, sh2, sc2, gt2 = [mod_me[:, i * D:(i + 1) * D] for i in range(6)]

    bs = _div_tile(S, 128, 8)
    nb = S // bs
    vecD = _full((1, D))
    rowD = _row(bs, D)

    hn1 = _ew(lambda a, g, sc, sh: (_f_normmod(a, g, sc, sh),), "norm1_fwd", (nb,),
              [x2, norm1_g, sc1, sh1], [rowD, vecD, vecD, vecD], [((S, D), BF16, rowD)],
              after=[ag_token, m_w_in[0], v_w_in[0]])[0]
    _, in_landed = _split_wait("ag_in_wait", in_state, hn1, _gather_wait_copies)
    (w_in_g,) = _pair_forward(in_landed, "ag_in_pair")
    w_aug = jnp.concatenate([p for o, a, n in sorted(segs, key=lambda s: s[1]) for p in shard_slices(o, n)]
                            + [jnp.zeros((D, LANES - 2 * NH), BF16)], axis=1)
    proj = _matmul(hn1, w_aug, "nn", F32, "mm_in")

    def dn_post(j, cv):
        s = _silu(cv)
        nrm = s * lax.rsqrt(jnp.sum(s * s, axis=-1, keepdims=True) + EPS)
        fq = (j < NH).astype(F32)
        fk = (j < 2 * NH).astype(F32)
        scale = fq * (DH ** -0.5) + (1.0 - fq)
        return fk * (nrm * scale) + (1.0 - fk) * s

    def ident(a):
        return a

    def colS(w, off=0):
        return pl.BlockSpec((S, w), lambda j, off=off: (0, j + off))

    def wS(kw, w):
        return pl.BlockSpec((kw, w), lambda j: (0, j))

    qkv_spec = pl.BlockSpec((None, None, S, DH), lambda j: (j // NH, j % NH, 0, 0))
    dn_args = dict(kw=KDN, ncol=3 * NH, ins=[proj], in_specs=[colS(DH)], extras=[], extra_specs=[],
                   w=dn_cw, w_spec=wS(KDN, DH), pre=ident, post=dn_post)
    qkvn = _conv_fwd_call("dn_conv_fwd", out_shape=jax.ShapeDtypeStruct((3, NH, S, DH), F32), out_spec=qkv_spec,
                          **dn_args)

    alp = jnp.pad(dn_a_log, ((0, 0), (NH, LANES - 2 * NH)))
    dtb = jnp.pad(dn_dt_bias, ((0, 0), (NH, LANES - 2 * NH)))
    vecL = _full((1, LANES))
    ba_spec = _row(bs, LANES, a_ba // LANES)
    rowL = _row(bs, LANES)
    gate_fn = functools.partial(_f_dn_gate, NH)
    def gate_fwd(a, p, q):
        val = gate_fn(a, p, q)
        lane = lax.broadcasted_iota(jnp.int32, val.shape, 1)

        def spread(col):
            return jnp.sum(jnp.where(lane == col, val, 0.0), axis=-1, keepdims=True) + jnp.zeros(val.shape, F32)

        return tuple(spread(h) for h in range(NH)), tuple(spread(NH + h) for h in range(NH))

    hrow = pl.BlockSpec((NH, bs, DH), lambda i: (0, i, 0))
    bb_b, gb_b = _ew(gate_fwd, "dn_gate_fwd", (nb,), [proj, alp, dtb], [ba_spec, vecL, vecL],
                     [((NH, S, DH), F32, hrow), ((NH, S, DH), F32, hrow)])
    o_dn, states = _dn_fwd_call(qkvn, gb_b, bb_b)

    bsh = _div_tile(S, 512, 8)
    nbh = S // bsh
    o_spec = pl.BlockSpec((None, bsh, DH), lambda i, h: (h, i, 0))
    z_spec = pl.BlockSpec((bsh, DH), lambda i, h: (i, a_z // DH + h))
    oh_spec = pl.BlockSpec((bsh, DH), lambda i, h: (i, h))
    ng_spec = pl.BlockSpec((1, DH), lambda i, h: (0, 0))
    on = _ew(lambda o, z, g: (_f_dn_post(o, z, g),), "dn_post_fwd", (nbh, NH), [o_dn, proj, dn_norm_g],
             [o_spec, z_spec, ng_spec], [((S, DNW), BF16, oh_spec)])[0]
    _, landed = _split_wait("ag_rest_wait", ag_state, on, _gather_wait_copies)
    w_do_f, w_co_f, w_out_g, w_up_f, w_dn_g = _pair_forward(landed, "ag_rest_pair")
    w_out_f = w_out_g.reshape(-1, w_out_g.shape[2])
    w_dn_f = w_dn_g.reshape(-1, w_dn_g.shape[2])
    br_a = _matmul(on, w_do_f, "nn", F32, "mm_dn_o")

    def glu_pre(val, gl):
        return val * _sigmoid(gl)

    def glu_spec(t):
        return pl.BlockSpec((S, cb_cf), lambda j, t=t: (0, a_glu // cb_cf + 2 * j + t))

    cf_args = dict(kw=KCF, ncol=CFW // cb_cf, ins=[proj, proj], in_specs=[glu_spec(0), glu_spec(1)],
                   extras=[], extra_specs=[], w=cf_cw, w_spec=wS(KCF, cb_cf), pre=glu_pre, post=lambda j, cv: cv)
    uc = _conv_fwd_call("cf_conv_fwd", out_shape=jax.ShapeDtypeStruct((S, CFW), F32), out_spec=colS(cb_cf), **cf_args)
    rowC = _row(bs, CFW)
    vecC = _full((1, CFW))
    ub = _ew(lambda u, g, b: (_f_cf_ln(u, g, b),), "cf_ln_fwd", (nb,), [uc, cf_ln_g, cf_ln_b], [rowC, vecC, vecC],
             [((S, CFW), BF16, rowC)])[0]
    br_b = _matmul(ub, w_co_f, "nn", F32, "mm_cf_o")

    ga_spec, gb_spec = _row(bs, D, a_ga // D), _row(bs, D, a_gb // D)
    merged = _ew(lambda a, b, ga, gb: (_f_merge(a, b, ga, gb),), "merge_fwd", (nb,), [br_a, br_b, proj, proj],
                 [rowD, rowD, ga_spec, gb_spec], [((S, D), BF16, rowD)])[0]
    mix = _matmul(merged, w_out_f, "nn", F32, "mm_out")

    x1, hn2 = _ew(_f_res_normmod, "norm2_fwd", (nb,), [x2, mix, gt1, norm2_g, sc2, sh2],
                  [rowD, rowD, vecD, vecD, vecD, vecD], [((S, D), F32, rowD), ((S, D), BF16, rowD)])
    up_all = _matmul(hn2, w_up_f, "nn", F32, "mm_up")

    cb_ff = _div_tile(FFN, 256, LANES)
    ff_args = dict(kw=KFF, ncol=FFN // cb_ff, ins=[up_all], in_specs=[colS(cb_ff)], extras=[up_all],
                   extra_specs=[colS(cb_ff, FFN // cb_ff)], w=ff_cw, w_spec=wS(KFF, cb_ff),
                   pre=ident, post=lambda j, cv, up: _silu(cv) * up)
    hff = _conv_fwd_call("ffn_conv_fwd", out_shape=jax.ShapeDtypeStruct((S, FFN), BF16), out_spec=colS(cb_ff), **ff_args)
    ffo = _matmul(hff, w_dn_f, "nn", F32, "mm_down")

    gf2 = final_norm_g.reshape(1, D)

    def loss_bwd(a, f, gt, gf, t):
        val, vjp = jax.vjp(_f_loss, a, f, gt, gf, t)
        da, df, dgt, dgf, _ = vjp(jnp.ones((), F32))
        return da, df, dgt, dgf, jnp.zeros((1, LANES), F32) + val

    dx1_l, dffo, dgt2, dgf, loss_v = _ew(
        loss_bwd, "loss_bwd", (nb,), [x1, ffo, gt2, gf2, tgt], [rowD, rowD, vecD, vecD, rowD],
        [((S, D), F32, rowD), ((S, D), BF16, rowD), ((1, D), F32, vecD), ((1, D), F32, vecD),
         ((1, LANES), F32, vecL)], acc=(2, 3, 4))

    dhff = _matmul(dffo, w_dn_f, "nt", F32, "mm_down_dx")
    g_w_dn = _matmul(hff, dffo, "tn", F32, "mm_down_dw")

    slots = jnp.stack([chip, ci]).astype(jnp.int32)
    rs_groups = []

    def rs_begin(parts, nms, tag):
        rbs = _pair_swap_half(parts, "rs_pair_" + tag)
        q16 = [_pair_add(p, rb, core, "rs_pair_add_" + nm) for p, rb, nm in zip(parts, rbs, nms)]
        state, token = _chip_scatter_start(q16, "rs_chips_start_" + tag)
        rs_groups.append((state, nms, tag))
        return token

    tok_a = rs_begin([g_w_dn.reshape(4, FFN // 4, D)], ["ffn_w_down"], "a")

    d_upall, g_ffcw = _conv_bwd_call(
        "ffn_conv_bwd", dout=dhff, dout_spec=colS(cb_ff),
        dio_shapes=[jax.ShapeDtypeStruct((2, S, FFN), BF16)],
        dio_specs=[pl.BlockSpec((2, S, cb_ff), lambda j: (0, 0, j))], dio_pack=lambda dg, du: [(dg, du)],
        dw_shape=jax.ShapeDtypeStruct((KFF, FFN), F32), dw_spec=wS(KFF, cb_ff), after=[tok_a], **ff_args)
    dhn2 = _matmul(d_upall, w_up_f, "nt", F32, "mm_up_dx")
    g_w_up = _matmul(hn2, d_upall, "tn", F32, "mm_up_dw", out_groups=4)
    tok_b = rs_begin([g_w_up], ["ffn_w_up"], "b")

    def res2_bwd(a, mx, gt, g, sc, sh, dx1, dhn):
        _, vjp = jax.vjp(_f_res_normmod, a, mx, gt, g, sc, sh)
        return vjp((dx1, dhn))

    dx_r, dmix, dgt1, dg2, dsc2, dsh2 = _ew(
        res2_bwd, "norm2_bwd", (nb,), [x2, mix, gt1, norm2_g, sc2, sh2, dx1_l, dhn2],
        [rowD, rowD, vecD, vecD, vecD, vecD, rowD, rowD],
        [((S, D), F32, rowD), ((S, D), BF16, rowD)] + [((1, D), F32, vecD)] * 4, acc=(2, 3, 4, 5), after=[tok_b])

    dmerged = _matmul(dmix, w_out_f, "nt", F32, "mm_out_dx")
    g_w_out = _matmul(merged, dmix, "tn", F32, "mm_out_dw")

    def merge_bwd(a, b, ga, gb, dm):
        _, vjp = jax.vjp(_f_merge, a, b, ga, gb)
        da, db, dga, dgb = vjp(dm)
        return da, db, jnp.concatenate([dga, dgb], axis=1)

    d_bra, d_brb, dproj = _ew(merge_bwd, "merge_bwd", (nb,), [br_a, br_b, proj, proj, dmerged],
                              [rowD, rowD, ga_spec, gb_spec, rowD],
                              [((S, D), BF16, rowD), ((S, D), BF16, rowD),
                               ((S, NA + LANES), BF16, _row(bs, 2 * D, a_ga // (2 * D)))])

    d_on = _matmul(d_bra, w_do_f, "nt", F32, "mm_dn_o_dx")
    g_w_do = _matmul(on, d_bra, "tn", F32, "mm_dn_o_dw", out_groups=4)
    d_ub = _matmul(d_brb, w_co_f, "nt", F32, "mm_cf_o_dx")
    g_w_co = _matmul(ub, d_brb, "tn", F32, "mm_cf_o_dw", out_groups=4)
    tok_c = rs_begin([g_w_do, g_w_co, g_w_out.reshape(4, D // 4, D)], ["dn_w_o", "cf_w_o", "w_out"], "c")

    def cf_ln_bwd(u, g, b, du):
        _, vjp = jax.vjp(_f_cf_ln, u, g, b)
        return vjp(du)

    d_uc, g_cflg, g_cflb = _ew(cf_ln_bwd, "cf_ln_bwd", (nb,), [uc, cf_ln_g, cf_ln_b, d_ub], [rowC, vecC, vecC, rowC],
                               [((S, CFW), F32, rowC), ((1, CFW), F32, vecC), ((1, CFW), F32, vecC)], acc=(1, 2),
                               after=[tok_c])
    dproj_sds = jax.ShapeDtypeStruct((S, NA + LANES), BF16)
    dproj, g_cfcw = _conv_bwd_call(
        "cf_conv_bwd", dout=d_uc, dout_spec=colS(cb_cf), dio_shapes=[dproj_sds],
        dio_specs=[colS(2 * cb_cf, a_glu // (2 * cb_cf))], dio_pack=lambda dv, dg: [jnp.concatenate([dv, dg], axis=1)],
        dw_shape=jax.ShapeDtypeStruct((KCF, CFW), F32), dw_spec=wS(KCF, cb_cf), alias=dproj, **cf_args)

    def dn_post_bwd(o, z, g, d):
        _, vjp = jax.vjp(_f_dn_post, o, z, g)
        return vjp(d)

    d_o, dproj, g_dnng = _ew(dn_post_bwd, "dn_post_bwd", (nbh, NH), [o_dn, proj, dn_norm_g, d_on],
                             [o_spec, z_spec, ng_spec, oh_spec],
                             [((NH, S, DH), F32, o_spec), ((S, NA + LANES), BF16, z_spec), ((1, DH), F32, ng_spec)],
                             acc=(2,), alias=(dproj, 1))

    dqkvn, dgb_b, dbb_b = _dn_bwd_call(qkvn, gb_b, bb_b, states, d_o)
    dproj, g_dncw = _conv_bwd_call(
        "dn_conv_bwd", dout=dqkvn, dout_spec=qkv_spec, dio_shapes=[dproj_sds], dio_specs=[colS(DH)],
        dio_pack=lambda d: [d], dw_shape=jax.ShapeDtypeStruct((KDN, 3 * DNW), F32), dw_spec=wS(KDN, DH),
        alias=dproj, **dn_args)

    def gate_bwd(a, p, q, db, dg):
        lane = lax.broadcasted_iota(jnp.int32, a.shape, 1)
        d = jnp.zeros(a.shape, F32)
        for h in range(NH):
            d = d + jnp.where(lane == h, db[h], 0.0) + jnp.where(lane == NH + h, dg[h], 0.0)
        _, vjp = jax.vjp(gate_fn, a, p, q)
        return vjp(d)

    dproj, g_alp, g_dtb = _ew(gate_bwd, "dn_gate_bwd", (nb,), [proj, alp, dtb, dbb_b, dgb_b],
                              [ba_spec, vecL, vecL, hrow, hrow],
                              [((S, NA + LANES), BF16, ba_spec), ((1, LANES), F32, vecL), ((1, LANES), F32, vecL)],
                              acc=(1, 2), alias=(dproj, 0))

    g_w_aug = _matmul(hn1, dproj, "tn", F32, "mm_in_dw")
    def aug_slices(lo, n):
        out = []
        for o, a, m in sorted(segs):
            s, e = max(lo, o), min(lo + n, o + m)
            if s < e:
                out.append(g_w_aug[:, a + s - o:a + e - o])
        return out

    g_w_in = jnp.stack([jnp.concatenate(aug_slices(j * CS, CS), axis=1) for j in range(4)])
    tok_d = rs_begin([g_w_in], ["w_in"], "d")
    dhn1 = _matmul(dproj, w_aug, "nt", F32, "mm_in_dx", after=[tok_d])

    def norm1_bwd(a, g, sc, sh, dhn, dxr):
        _, vjp = jax.vjp(_f_normmod, a, g, sc, sh)
        da, dg, dsc, dsh = vjp(dhn)
        return da + dxr, dg, dsc, dsh

    grad_x, dg1, dsc1, dsh1 = _ew(norm1_bwd, "norm1_bwd", (nb,), [x2, norm1_g, sc1, sh1, dhn1, dx_r],
                                  [rowD, vecD, vecD, vecD, rowD, rowD],
                                  [((S, D), F32, rowD)] + [((1, D), F32, vecD)] * 3, acc=(1, 2, 3))

    names = ["w_ada", "b_ada", "norm1_g", "w_in", "dn_conv_w", "dn_a_log", "dn_dt_bias", "dn_norm_g", "dn_w_o",
             "cf_conv_w", "cf_ln_g", "cf_ln_b", "cf_w_o", "w_out", "norm2_g", "ffn_w_up", "ffn_conv_w", "ffn_w_down",
             "final_norm_g"]
    wts = dict(zip(names, [w_ada, b_ada, norm1_g, w_in, dn_conv_w, dn_a_log, dn_dt_bias, dn_norm_g, dn_w_o, cf_conv_w,
                           cf_ln_g, cf_ln_b, cf_w_o, w_out, norm2_g, ffn_w_up, ffn_conv_w, ffn_w_down, final_norm_g]))
    ms = dict(zip(names, [m_w_ada, m_b_ada, m_norm1_g, m_w_in, m_dn_conv_w, m_dn_a_log, m_dn_dt_bias, m_dn_norm_g,
                          m_dn_w_o, m_cf_conv_w, m_cf_ln_g, m_cf_ln_b, m_cf_w_o, m_w_out, m_norm2_g, m_ffn_w_up,
                          m_ffn_conv_w, m_ffn_w_down, m_final_norm_g]))
    vs = dict(zip(names, [v_w_ada, v_b_ada, v_norm1_g, v_w_in, v_dn_conv_w, v_dn_a_log, v_dn_dt_bias, v_dn_norm_g,
                          v_dn_w_o, v_cf_conv_w, v_cf_ln_g, v_cf_ln_b, v_cf_w_o, v_w_out, v_norm2_g, v_ffn_w_up,
                          v_ffn_conv_w, v_ffn_w_down, v_final_norm_g]))
    grads, delta, new_m, new_v = {}, {}, {}, {}

    def adam_large(n, g):
        grads[n] = g.reshape(wts[n].shape)
        d_, m_, v_ = _adamw(wts[n][0], grads[n][0], ms[n][0], vs[n][0], "adamw_" + n)
        delta[n], new_m[n], new_v[n] = d_[None], m_[None], v_[None]

    def rs_finish(groups, after, tag):
        hs, nms_all = [], []
        for state, nms, t in groups:
            q16, r3s = _chip_scatter_wait(state, after, "rs_chips_wait_" + t)
            for q, r3, nm in zip(q16, r3s, nms):
                hs.append(_chip_sum(q, r3, slots, "rs_chip_sum_" + nm))
                nms_all.append(nm)
        for nm, g in zip(nms_all, _pair_gather(hs, "rs_pair_gather_" + tag)):
            adam_large(nm, g)

    rs_finish(rs_groups[:3], grad_x, "abc")
    rs_finish(rs_groups[3:], delta["ffn_w_up"], "d")

    dmod = jnp.concatenate([dsh1, dsc1, dgt1, dsh2, dsc2, dgt2], axis=1)
    sm2 = [dmod, dg1, dg2, dgf, g_alp, g_dtb, g_dnng, g_cflg, g_cflb, g_dncw, g_cfcw, g_ffcw]
    sm2_shapes = [tuple(a.shape) for a in sm2]
    g3 = _all_gather_small(_pack_small(sm2))
    ssum = _sum_slots(g3, "small_sum")
    dmod_all = _unpack_small(g3, sm2_shapes[:1])[0].reshape(8, 6 * D)
    (g_b_ada, gs_n1, gs_n2, gs_fn, gs_alp, gs_dtb, gs_dnng, gs_cflg, gs_cflb, gs_dncw, gs_cfcw,
     gs_ffcw) = _unpack_small(ssum, sm2_shapes)
    gs_alog, gs_dtbias = gs_alp[:, NH:2 * NH], gs_dtb[:, NH:2 * NH]
    gs_dncw = lax.dynamic_slice(gs_dncw, (0, chip * (3 * DNW // 4)), (KDN, 3 * DNW // 4))
    gs_cfcw = lax.dynamic_slice(gs_cfcw, (0, chip * (CFW // 4)), (KCF, CFW // 4))
    gs_ffcw = lax.dynamic_slice(gs_ffcw, (0, chip * (FFN // 4)), (KFF, FFN // 4))

    dmod_sh = lax.dynamic_slice(dmod_all, (0, chip * CA), (8, CA))

    def wada_fn(cc, dm):
        return (_mm(_silu(cc), dm, TN),)

    g_w_ada = _ew(wada_fn, "ada_dw", (CA // tn_a,), [c_all, dmod_sh],
                  [_full((8, D)), pl.BlockSpec((8, tn_a), lambda j: (0, j))],
                  [((D, CA), F32, pl.BlockSpec((D, tn_a), lambda j: (0, j)))])[0]

    loss = lax.psum(loss_v[0, 0], ("x", "y", "c"))

    adam_large("w_ada", g_w_ada)
    small_grads = {"b_ada": g_b_ada, "norm1_g": gs_n1, "dn_conv_w": gs_dncw, "dn_a_log": gs_alog,
                   "dn_dt_bias": gs_dtbias, "dn_norm_g": gs_dnng, "cf_conv_w": gs_cfcw, "cf_ln_g": gs_cflg,
                   "cf_ln_b": gs_cflb, "norm2_g": gs_n2, "ffn_conv_w": gs_ffcw, "final_norm_g": gs_fn}
    small = [n for n in names if n in small_grads]
    for n in small:
        grads[n] = small_grads[n].reshape(wts[n].shape)
    sm_sh = [tuple(wts[n].shape) for n in small]
    d_, m_, v_ = _adamw(_pack_small([wts[n] for n in small]), _pack_small([grads[n] for n in small]),
                        _pack_small([ms[n] for n in small]), _pack_small([vs[n] for n in small]), "adamw_small")
    for n, a, b_, c_ in zip(small, _unpack_small(d_, sm_sh), _unpack_small(m_, sm_sh), _unpack_small(v_, sm_sh)):
        delta[n], new_m[n], new_v[n] = a, b_, c_

    return (loss, grad_x[None], *[grads[n] for n in names], *[delta[n] for n in names],
            *[new_m[n] for n in names], *[new_v[n] for n in names])
```

```python
import functools

import jax
import jax.numpy as jnp
from jax import lax
from jax.experimental import pallas as pl
from jax.experimental.pallas import tpu as pltpu

F32 = jnp.float32
BF16 = jnp.bfloat16
EPS = 1e-6
LANES = 128
VMEM_LIMIT = 48 * 1024 * 1024
DN_CHUNK = 128
ADAM_LR, ADAM_B1, ADAM_B2, ADAM_EPS, ADAM_WD, ADAM_STEP = 0.001, 0.9, 0.999, 1e-08, 0.01, 10
MESH = pl.DeviceIdType.MESH

NN = (((1,), (0,)), ((), ()))
NT = (((1,), (1,)), ((), ()))
TN = (((0,), (0,)), ((), ()))
_DIMS = {"nn": NN, "nt": NT, "tn": TN}


def _mm(a, b, dims):
    return lax.dot_general(a.astype(BF16), b.astype(BF16), dims, preferred_element_type=F32)


def _mmx(a, b, dims):
    return lax.dot_general(a, b, dims, precision=lax.Precision.HIGH, preferred_element_type=F32)


def _div_tile(n, target, mult):
    best = None
    t = mult
    while t <= min(n, target):
        if n % t == 0:
            best = t
        t += mult
    return best if best is not None else n


def _params(sem=None):
    kw = dict(vmem_limit_bytes=VMEM_LIMIT)
    if sem is not None:
        kw["dimension_semantics"] = sem
    return pltpu.CompilerParams(**kw)


def _sigmoid(x):
    return jax.nn.sigmoid(x)


def _silu(x):
    return x * jax.nn.sigmoid(x)


def _softplus(x):
    return jnp.maximum(x, 0.0) + jnp.log(1.0 + jnp.exp(-jnp.abs(x)))


def _view(arr):
    if arr.ndim == 3:
        return arr.shape[1], arr.shape[0] * arr.shape[2], arr.shape[0]
    return arr.shape[0], arr.shape[1], 1


def _tile_spec(groups, cols, tr, tc, rsel, csel):
    if groups > 1:
        per = cols // groups // tc
        return pl.BlockSpec((None, tr, tc), lambda i, j, k: (csel(i, j, k) // per, rsel(i, j, k), csel(i, j, k) % per))
    return pl.BlockSpec((tr, tc), lambda i, j, k: (rsel(i, j, k), csel(i, j, k)))


def _matmul(a, b, mode, out_dtype, name, out_groups=1, after=()):
    ar, ac, ag = _view(a)
    br, bc, bg = _view(b)
    if mode == "nn":
        M, K, N = ar, ac, bc
        kdiv, mdiv, ndiv = ac // ag, M, min(bc // bg, N // out_groups)
    elif mode == "nt":
        M, K, N = ar, ac, br
        kdiv, mdiv, ndiv = min(ac // ag, bc // bg), M, N // out_groups
    else:
        K, M, N = ar, ac, bc
        kdiv, mdiv, ndiv = K, ac // ag, min(bc // bg, N // out_groups)
    tm = _div_tile(mdiv, 1024, LANES)
    tn = _div_tile(ndiv, 1536, LANES)
    tk = _div_tile(kdiv, 2048, LANES)
    nk = K // tk
    dims = _DIMS[mode]
    si, sj, sk = (lambda i, j, k: i), (lambda i, j, k: j), (lambda i, j, k: k)
    a_spec = {"nn": _tile_spec(ag, ac, tm, tk, si, sk), "nt": _tile_spec(ag, ac, tm, tk, si, sk),
              "tn": _tile_spec(ag, ac, tk, tm, sk, si)}[mode]
    b_spec = {"nn": _tile_spec(bg, bc, tk, tn, sk, sj), "nt": _tile_spec(bg, bc, tn, tk, sj, sk),
              "tn": _tile_spec(bg, bc, tk, tn, sk, sj)}[mode]
    out_shape = (M, N) if out_groups == 1 else (out_groups, M, N // out_groups)

    n_after = len(after)

    def body(*refs):
        a_ref, b_ref, o_ref = refs[0], refs[1], refs[2 + n_after]
        if nk == 1:
            o_ref[...] = lax.dot_general(a_ref[...], b_ref[...], dims, preferred_element_type=F32).astype(o_ref.dtype)
            return
        acc_ref = refs[3 + n_after]
        k = pl.program_id(2)

        @pl.when(k == 0)
        def _():
            acc_ref[...] = jnp.zeros_like(acc_ref)

        acc_ref[...] += lax.dot_general(a_ref[...], b_ref[...], dims, preferred_element_type=F32)

        @pl.when(k == nk - 1)
        def _():
            o_ref[...] = acc_ref[...].astype(o_ref.dtype)

    return pl.pallas_call(
        body, name=name, grid=(M // tm, N // tn, nk),
        in_specs=[a_spec, b_spec] + [pl.BlockSpec(memory_space=pl.ANY)] * n_after,
        out_specs=_tile_spec(out_groups, N, tm, tn, si, sj),
        out_shape=jax.ShapeDtypeStruct(out_shape, out_dtype),
        scratch_shapes=[pltpu.VMEM((tm, tn), F32)] if nk > 1 else [],
        compiler_params=_params(("parallel", "parallel", "arbitrary")),
    )(a, b, *after)


def _ew(fn, name, grid, ins, in_specs, outs, acc=(), alias=None, after=()):
    n_in = len(ins)
    n_ax = len(grid)
    extra, aliases = list(after), {}
    if alias is not None:
        extra, aliases = extra + [alias[0]], {n_in + len(after): alias[1]}
    extra_specs = [pl.BlockSpec(memory_space=pl.ANY)] * len(extra)

    def body(*refs):
        in_refs, out_refs = refs[:n_in], refs[n_in + len(extra):]
        ids = [pl.program_id(a) for a in range(n_ax)]
        res = fn(*[r[...] for r in in_refs])
        first = ids[0] == 0
        for t in ids[1:]:
            first = jnp.logical_and(first, t == 0)
        for idx, (r, val) in enumerate(zip(out_refs, res)):
            if idx in acc:
                @pl.when(first)
                def _(r=r, val=val):
                    r[...] = val.astype(r.dtype)

                @pl.when(jnp.logical_not(first))
                def _(r=r, val=val):
                    r[...] += val.astype(r.dtype)
            elif isinstance(val, tuple):
                for t, part in enumerate(val):
                    r[t] = part.astype(r.dtype)
            else:
                r[...] = val.astype(r.dtype)

    return pl.pallas_call(
        body, name=name, grid=grid, in_specs=list(in_specs) + extra_specs,
        out_specs=[o[2] for o in outs],
        out_shape=[jax.ShapeDtypeStruct(o[0], o[1]) for o in outs],
        input_output_aliases=aliases,
        compiler_params=_params(("arbitrary",) * n_ax),
    )(*ins, *extra)


def _ew_slot(fn, name, grid, slots, ins, in_specs, out_shape, out_dtype, out_spec):
    def body(s_ref, *refs):
        refs[-1][...] = fn(*[r[...] for r in refs[:-1]]).astype(refs[-1].dtype)

    return pl.pallas_call(
        body, name=name,
        grid_spec=pltpu.PrefetchScalarGridSpec(num_scalar_prefetch=1, grid=grid, in_specs=list(in_specs),
                                               out_specs=out_spec),
        out_shape=jax.ShapeDtypeStruct(out_shape, out_dtype),
        compiler_params=_params(("arbitrary",) * len(grid)),
    )(slots, *ins)


def _row(bs, w, col=0):
    return pl.BlockSpec((bs, w), lambda i, col=col: (i, col))


def _full(shape):
    nd = len(shape)
    return pl.BlockSpec(tuple(shape), lambda *_: (0,) * nd)


def _rms(x, g):
    return x * lax.rsqrt(jnp.mean(x * x, axis=-1, keepdims=True) + EPS) * g


def _f_normmod(x, g, sc, sh):
    return _rms(x, g) * (1.0 + sc) + sh


def _f_res_normmod(x, mix, gt, g, sc, sh):
    x1 = x + gt * mix
    return x1, _f_normmod(x1, g, sc, sh)


def _f_loss(x1, f, gt, gf, tgt):
    y = _rms(x1 + gt * f, gf)
    return 0.5 * jnp.sum(jnp.mean(jnp.square(y - tgt), axis=-1))


def _f_dn_gate(nh, ba, alp, dtb):
    lane = lax.broadcasted_iota(jnp.int32, ba.shape, 1)
    m = (lane < nh).astype(F32)
    beta = _sigmoid(ba)
    g = -jnp.exp(alp) * _softplus(ba + dtb)
    return m * beta + (1.0 - m) * g


def _f_dn_post(o, z, g):
    return o * lax.rsqrt(jnp.mean(o * o, axis=-1, keepdims=True) + EPS) * g * _silu(z)


def _f_cf_ln(u, g, b):
    mu = jnp.mean(u, axis=-1, keepdims=True)
    xc = u - mu
    y = xc * lax.rsqrt(jnp.mean(xc * xc, axis=-1, keepdims=True) + EPS)
    return _silu(y * g + b)


def _f_merge(a, b, ga, gb):
    return _sigmoid(ga) * a + _sigmoid(gb) * b


def _shift_down(u, d, rows):
    if d == 0:
        return u
    return jnp.where(rows >= d, pltpu.roll(u, d, 0), 0.0)


def _shift_up(u, d, rows):
    if d == 0:
        return u
    s = u.shape[0]
    return jnp.where(rows < s - d, pltpu.roll(u, s - d, 0), 0.0)


def _conv(u, w_ref, kw, rows):
    acc = None
    for k in range(kw):
        t = w_ref[k:k + 1, :] * _shift_down(u, kw - 1 - k, rows)
        acc = t if acc is None else acc + t
    return acc


def _conv_t(dc, w_ref, kw, rows):
    acc = None
    for k in range(kw):
        t = w_ref[k:k + 1, :] * _shift_up(dc, kw - 1 - k, rows)
        acc = t if acc is None else acc + t
    return acc


def _conv_fwd_call(name, kw, ncol, ins, in_specs, extras, extra_specs, w, w_spec, pre, post, out_shape, out_spec):
    n_in, n_ex = len(ins), len(extras)

    def body(*refs):
        in_refs, ex_refs = refs[:n_in], refs[n_in:n_in + n_ex]
        w_ref, out_ref = refs[n_in + n_ex], refs[n_in + n_ex + 1]
        j = pl.program_id(0)
        u = pre(*[r[...] for r in in_refs])
        rows = lax.broadcasted_iota(jnp.int32, u.shape, 0)
        cv = _conv(u, w_ref, kw, rows)
        out_ref[...] = post(j, cv, *[r[...] for r in ex_refs]).astype(out_ref.dtype)

    return pl.pallas_call(
        body, name=name, grid=(ncol,), in_specs=list(in_specs) + list(extra_specs) + [w_spec],
        out_specs=out_spec, out_shape=out_shape, compiler_params=_params(("arbitrary",)),
    )(*ins, *extras, w)


def _conv_bwd_call(name, kw, ncol, ins, in_specs, extras, extra_specs, w, w_spec, pre, post, dout, dout_spec,
                   dio_shapes, dio_specs, dio_pack, dw_shape, dw_spec, alias=None, after=()):
    n_in, n_ex, n_io = len(ins), len(extras), len(dio_shapes)
    al, aliases = list(after), {}
    if alias is not None:
        al, aliases = al + [alias], {n_in + n_ex + 2 + len(after): 0}
    al_specs = [pl.BlockSpec(memory_space=pl.ANY)] * len(al)

    def body(*refs):
        in_refs, ex_refs = refs[:n_in], refs[n_in:n_in + n_ex]
        w_ref, dout_ref = refs[n_in + n_ex], refs[n_in + n_ex + 1]
        outs = refs[n_in + n_ex + 2 + len(al):]
        dio_refs, dw_ref = outs[:n_io], outs[n_io]
        j = pl.program_id(0)
        u, pre_vjp = jax.vjp(pre, *[r[...] for r in in_refs])
        rows = lax.broadcasted_iota(jnp.int32, u.shape, 0)
        cv = _conv(u, w_ref, kw, rows)
        _, post_vjp = jax.vjp(lambda cc, *ex: post(j, cc, *ex), cv, *[r[...] for r in ex_refs])
        g = post_vjp(dout_ref[...].astype(F32))
        dc = g[0]
        for k in range(kw):
            dw_ref[k:k + 1, :] = jnp.sum(dc * _shift_down(u, kw - 1 - k, rows), axis=0, keepdims=True)
        du = _conv_t(dc, w_ref, kw, rows)
        for r, val in zip(dio_refs, dio_pack(*pre_vjp(du), *g[1:])):
            if isinstance(val, tuple):
                for t, part in enumerate(val):
                    r[t] = part.astype(r.dtype)
            else:
                r[...] = val.astype(r.dtype)

    return pl.pallas_call(
        body, name=name, grid=(ncol,),
        in_specs=list(in_specs) + list(extra_specs) + [w_spec, dout_spec] + al_specs,
        out_specs=list(dio_specs) + [dw_spec],
        out_shape=list(dio_shapes) + [dw_shape],
        input_output_aliases=aliases,
        compiler_params=_params(("arbitrary",)),
    )(*ins, *extras, w, dout, *al)


def _tri_inverse(a):
    c = a.shape[0]
    ii = lax.broadcasted_iota(jnp.int32, (c, c), 0)
    jj = lax.broadcasted_iota(jnp.int32, (c, c), 1)
    eye = (ii == jj).astype(F32)

    def same_block(bits):
        return jnp.right_shift(ii, bits) == jnp.right_shift(jj, bits)

    d = jnp.where(same_block(3), a, 0.0)
    d2 = _mmx(d, d, NN)
    yield
    t = (eye - d) + _mmx(eye - d, d2, NN)
    d4 = _mmx(d2, d2, NN)
    yield
    t = t + _mmx(t, d4, NN)
    yield
    bits = 3
    while (1 << bits) < c:
        low = jnp.where(jnp.logical_and(same_block(bits + 1), jnp.logical_not(same_block(bits))), a, 0.0)
        tl = _mmx(t, low, NN)
        yield
        t = t - _mmx(tl, t, NN)
        yield
        bits += 1
    return t


def _lockstep(gens):
    out = [None] * len(gens)
    live = list(range(len(gens)))
    while live:
        nxt = []
        for i in live:
            try:
                next(gens[i])
                nxt.append(i)
            except StopIteration as e:
                out[i] = e.value
        live = nxt
    return out


def _dn_common(q, k, v, gb, bb):
    c = q.shape[0]
    ii = lax.broadcasted_iota(jnp.int32, (c, c), 0)
    jj = lax.broadcasted_iota(jnp.int32, (c, c), 1)
    causal = jj <= ii
    strict = jj < ii
    low = causal.astype(F32)
    ones = jnp.ones((c, LANES), F32)
    gc = _mmx(low, gb, NN)
    kb = k * bb
    vb = v * bb
    kk = _mm(kb, k, NT)
    qk = _mm(q, k, NT)
    yield
    diff = (_mmx(gc, ones, NT) - _mmx(ones, gc, NT)) * (1.0 / LANES)
    gl = jnp.sum(gb, axis=0, keepdims=True)
    eg = jnp.exp(gc)
    egm = jnp.exp(gl - gc)
    egl = jnp.exp(gl)
    kbg = kb * eg
    yield
    decay = jnp.where(causal, jnp.exp(jnp.where(causal, diff, 0.0)), 0.0)
    t = yield from _tri_inverse(jnp.where(strict, kk * decay, 0.0))
    attn = qk * decay
    return dict(causal=causal, strict=strict, low=low, ones=ones, decay=decay, eg=eg, egm=egm, egl=egl,
                kb=kb, vb=vb, kbg=kbg, kk=kk, t=t, qk=qk, attn=attn, qd=q * eg, kd=k * egm)


def _dn_fwd_stages(q, k, v, gb, bb, s):
    m = yield from _dn_common(q, k, v, gb, bb)
    r = m["vb"] - _mm(m["kbg"], s, NN)
    qs = _mm(m["qd"], s, NN)
    yield
    u = _mmx(m["t"], r, NN)
    yield
    o = qs + _mm(m["attn"], u, NN)
    s2 = s * m["egl"] + _mm(m["kd"], u, TN)
    return o, s2


def _dn_bwd_stages(q, k, v, gb, bb, s, do, dsp):
    m = yield from _dn_common(q, k, v, gb, bb)
    c = q.shape[0]
    t, decay, eg, egm, egl = m["t"], m["decay"], m["eg"], m["egm"], m["egl"]
    r = m["vb"] - _mm(m["kbg"], s, NN)
    du = _mm(m["attn"], do, TN) + _mm(m["kd"], dsp, NN)
    dqd = _mm(do, s, NT)
    ds = dsp * egl + _mm(m["qd"], do, TN)
    degl = jnp.sum(jnp.sum(dsp * s, axis=1, keepdims=True), axis=0, keepdims=True)
    yield
    u = _mmx(t, r, NN)
    dr = _mmx(t, du, TN)
    yield
    dattn = jnp.where(m["causal"], _mm(do, u, NT), 0.0)
    dkd = _mm(u, dsp, NT)
    da = jnp.where(m["strict"], -_mm(dr, u, NT), 0.0)
    dkbg = -_mm(dr, s, NT)
    ds = ds - _mm(m["kbg"], dr, TN)
    yield
    dkk = da * decay
    dqk = dattn * decay
    ddiff = (da * m["kk"] + dattn * m["qk"]) * decay
    dgc = _mmx(ddiff, m["ones"], NN) - _mmx(ddiff, m["ones"], TN)
    dkb = _mm(dkk, k, NN) + dkbg * eg
    dk = _mm(dkk, m["kb"], TN) + _mm(dqk, q, TN) + dkd * egm + dkb * bb
    dq = _mm(dqk, k, NN) + dqd * eg
    yield
    dgc = dgc + jnp.sum(dqd * q + dkbg * m["kb"], axis=-1, keepdims=True) * eg
    tt = jnp.sum(dkd * k, axis=-1, keepdims=True) * egm
    dgc = dgc - tt
    dgl = jnp.sum(tt, axis=0, keepdims=True) + degl * egl
    dbb = jnp.sum(dkb * k + dr * v, axis=-1, keepdims=True) + jnp.zeros((c, LANES), F32)
    dv = dr * bb
    dgb = _mmx(m["low"], dgc, TN) + dgl
    return dq, dk, dv, dgb, dbb, ds


def _dn_fwd_chunk(q, k, v, gb, bb, s):
    return _lockstep([_dn_fwd_stages(q, k, v, gb, bb, s)])[0]


def _dn_bwd_chunk(q, k, v, gb, bb, s, do, dsp):
    return _lockstep([_dn_bwd_stages(q, k, v, gb, bb, s, do, dsp)])[0]


def _dn_fwd_call(qkvn, gb, bb):
    _, nh, s, dh = qkvn.shape
    c = min(DN_CHUNK, s)
    n = s // c
    hb = nh

    def body(q_ref, k_ref, v_ref, g_ref, b_ref, o_ref, st_ref, s_ref):
        @pl.when(pl.program_id(1) == 0)
        def _():
            s_ref[...] = jnp.zeros_like(s_ref)

        st_ref[...] = s_ref[...]
        heads = [_dn_fwd_stages(q_ref[h], k_ref[h], v_ref[h], g_ref[h], b_ref[h], s_ref[h]) for h in range(hb)]
        for h, (o, s2) in enumerate(_lockstep(heads)):
            o_ref[h] = o
            s_ref[h] = s2

    def qspec(t):
        return pl.BlockSpec((None, hb, c, dh), lambda i, j, t=t: (t, i, j, 0))

    hs = pl.BlockSpec((hb, c, dh), lambda i, j: (i, j, 0))
    return pl.pallas_call(
        body, name="dn_fwd", grid=(nh // hb, n),
        in_specs=[qspec(0), qspec(1), qspec(2), hs, hs],
        out_specs=[hs, pl.BlockSpec((hb, None, dh, dh), lambda i, j: (i, j, 0, 0))],
        out_shape=[jax.ShapeDtypeStruct((nh, s, dh), F32), jax.ShapeDtypeStruct((nh, n, dh, dh), F32)],
        scratch_shapes=[pltpu.VMEM((hb, dh, dh), F32)],
        compiler_params=_params(("arbitrary", "arbitrary")),
    )(qkvn, qkvn, qkvn, gb, bb)


def _dn_bwd_call(qkvn, gb, bb, states, do):
    _, nh, s, dh = qkvn.shape
    c = min(DN_CHUNK, s)
    n = s // c
    hb = nh

    def body(q_ref, k_ref, v_ref, g_ref, b_ref, st_ref, do_ref, dqkv_ref, dg_ref, db_ref, ds_ref):
        @pl.when(pl.program_id(1) == 0)
        def _():
            ds_ref[...] = jnp.zeros_like(ds_ref)

        heads = [_dn_bwd_stages(q_ref[h], k_ref[h], v_ref[h], g_ref[h], b_ref[h], st_ref[h], do_ref[h], ds_ref[h])
                 for h in range(hb)]
        for h, (dq, dk, dv, dg, db, ds) in enumerate(_lockstep(heads)):
            dqkv_ref[0, h] = dq
            dqkv_ref[1, h] = dk
            dqkv_ref[2, h] = dv
            dg_ref[h] = dg
            db_ref[h] = db
            ds_ref[h] = ds

    def qspec(t):
        return pl.BlockSpec((None, hb, c, dh), lambda i, j, t=t: (t, i, n - 1 - j, 0))

    hs = pl.BlockSpec((hb, c, dh), lambda i, j: (i, n - 1 - j, 0))
    sh = jax.ShapeDtypeStruct((nh, s, dh), F32)
    return pl.pallas_call(
        body, name="dn_bwd", grid=(nh // hb, n),
        in_specs=[qspec(0), qspec(1), qspec(2), hs, hs,
                  pl.BlockSpec((hb, None, dh, dh), lambda i, j: (i, n - 1 - j, 0, 0)), hs],
        out_specs=[pl.BlockSpec((3, hb, c, dh), lambda i, j: (0, i, n - 1 - j, 0)), hs, hs],
        out_shape=[jax.ShapeDtypeStruct((3, nh, s, dh), F32), sh, sh],
        scratch_shapes=[pltpu.VMEM((hb, dh, dh), F32)],
        compiler_params=_params(("arbitrary", "arbitrary")),
    )(qkvn, qkvn, qkvn, gb, bb, states, do)


def _adamw(w, g, m, v, name):
    r, c = w.shape
    br = _div_tile(r, max(8, (1 << 18) // max(c, 1)), 8)

    def fn(w, g, m, v):
        m = ADAM_B1 * m + (1.0 - ADAM_B1) * g
        v = ADAM_B2 * v + (1.0 - ADAM_B2) * jnp.square(g)
        m_hat = m / (1.0 - ADAM_B1 ** ADAM_STEP)
        v_hat = v / (1.0 - ADAM_B2 ** ADAM_STEP)
        delta = -ADAM_LR * (m_hat / (jnp.sqrt(v_hat) + ADAM_EPS) + ADAM_WD * w)
        return delta, m, v

    spec = pl.BlockSpec((br, c), lambda i: (i, 0))
    return _ew(fn, name, (r // br,), [w, g, m, v], [spec] * 4, [((r, c), F32, spec)] * 3)


def _coords():
    return lax.axis_index("x"), lax.axis_index("y"), lax.axis_index("c")


def _all_gather_small(v):
    r, w = v.shape

    def body(v_ref, out_ref, send_sems, recv_sems, local_sem):
        x, y, c = _coords()
        me = 4 * x + 2 * y + c
        mine = pltpu.make_async_copy(v_ref, out_ref.at[me], local_sem)
        mine.start()
        peers = []
        for k in range(1, 8):
            px = 1 - x if k & 4 else x
            py = 1 - y if k & 2 else y
            pc = 1 - c if k & 1 else c
            peers.append((px, py, pc))
        sends = []
        for k, peer in enumerate(peers):
            cp = pltpu.make_async_remote_copy(src_ref=v_ref, dst_ref=out_ref.at[me], send_sem=send_sems.at[k],
                                              recv_sem=recv_sems.at[k], device_id=peer, device_id_type=MESH)
            cp.start()
            sends.append(cp)
        for k, (px, py, pc) in enumerate(peers):
            pltpu.make_async_remote_copy(src_ref=v_ref, dst_ref=out_ref.at[4 * px + 2 * py + pc],
                                         send_sem=send_sems.at[k], recv_sem=recv_sems.at[k],
                                         device_id=(px, py, pc), device_id_type=MESH).wait_recv()
        for cp in sends:
            cp.wait_send()
        mine.wait()

    return pl.pallas_call(
        body, name="ag_small", out_shape=jax.ShapeDtypeStruct((8, r, w), v.dtype),
        in_specs=[pl.BlockSpec(memory_space=pltpu.VMEM)], out_specs=pl.BlockSpec(memory_space=pltpu.VMEM),
        scratch_shapes=[pltpu.SemaphoreType.DMA((7,)), pltpu.SemaphoreType.DMA((7,)), pltpu.SemaphoreType.DMA],
        compiler_params=pltpu.CompilerParams(vmem_limit_bytes=VMEM_LIMIT),
    )(v)


def _hbm_call(body, name, arrays, out_shapes, n_sems, aliases=None):
    hbm = pl.BlockSpec(memory_space=pltpu.HBM)
    return pl.pallas_call(
        body, name=name, out_shape=list(out_shapes), in_specs=[hbm] * len(arrays), out_specs=[hbm] * len(out_shapes),
        input_output_aliases=aliases or {},
        scratch_shapes=[pltpu.SemaphoreType.DMA((n_sems,)), pltpu.SemaphoreType.DMA((n_sems,))],
    )(*arrays)


def _half_rows(ref_shape, c):
    h = ref_shape[1] // 2
    return pl.ds(pl.multiple_of(c * h, 16), h), pl.ds(pl.multiple_of((1 - c) * h, 16), h)


def _pair_swap_half(ps, name):
    n = len(ps)

    def body(*refs):
        ins, outs, send_sems, recv_sems = refs[:n], refs[n:2 * n], refs[2 * n], refs[2 * n + 1]
        x, y, c = _coords()
        cps = []
        for w in range(n):
            _, other = _half_rows(ins[w].shape, c)
            cp = pltpu.make_async_remote_copy(src_ref=ins[w].at[:, other], dst_ref=outs[w], send_sem=send_sems.at[w],
                                              recv_sem=recv_sems.at[w], device_id=(x, y, 1 - c), device_id_type=MESH)
            cp.start()
            cps.append(cp)
        for cp in cps:
            cp.wait()

    shapes = [jax.ShapeDtypeStruct((p.shape[0], p.shape[1] // 2, p.shape[2]), p.dtype) for p in ps]
    return _hbm_call(body, name, ps, shapes, n)


def _pair_gather(gs, name):
    n = len(gs)

    def body(*refs):
        ins, outs, send_sems, recv_sems = refs[:n], refs[n:2 * n], refs[2 * n], refs[2 * n + 1]
        x, y, c = _coords()
        cps = []
        for w in range(n):
            cp = pltpu.make_async_remote_copy(src_ref=ins[w].at[c], dst_ref=outs[w].at[c], send_sem=send_sems.at[w],
                                              recv_sem=recv_sems.at[w], device_id=(x, y, 1 - c), device_id_type=MESH)
            cp.start()
            cps.append(cp)
        for w, cp in enumerate(cps):
            pltpu.make_async_remote_copy(src_ref=ins[w].at[c], dst_ref=outs[w].at[1 - c], send_sem=send_sems.at[w],
                                         recv_sem=recv_sems.at[w], device_id=(x, y, 1 - c),
                                         device_id_type=MESH).wait_recv()
            cp.wait_send()

    shapes = [jax.ShapeDtypeStruct(g.shape, g.dtype) for g in gs]
    return _hbm_call(body, name, gs, shapes, n, aliases={i: i for i in range(n)})


_HBM = pl.BlockSpec(memory_space=pltpu.HBM)
_SEM = pl.BlockSpec(memory_space=pltpu.SEMAPHORE)
_EFFECT = pltpu.SideEffectType.DATAFLOW_SIDE_EFFECTING


def _split_start(name, srcs, lands, after, n_copies, make_copies):
    n, m = len(srcs), len(lands)
    arrays = [pltpu.with_memory_space_constraint(a, pltpu.HBM) for a in list(srcs) + list(lands)]

    def body(*refs):
        src_refs, land_refs = refs[:n], refs[n:n + m]
        send_sems, recv_sems = refs[n + m + 1], refs[n + m + 2]
        for cp in make_copies(src_refs, land_refs, send_sems, recv_sems):
            cp.start()
        refs[-1][...] = jnp.zeros_like(refs[-1])

    outs = pl.pallas_call(
        body, name=name,
        out_shape=(pltpu.SemaphoreType.DMA((n_copies,)), pltpu.SemaphoreType.DMA((n_copies,)),
                   *[pltpu.HBM(a.shape, a.dtype) for a in arrays], jax.ShapeDtypeStruct((8, LANES), F32)),
        in_specs=[_HBM] * (n + m) + [pl.BlockSpec(memory_space=pl.ANY)],
        out_specs=(_SEM, _SEM, *[_HBM] * (n + m), pl.BlockSpec(memory_space=pltpu.VMEM)),
        input_output_aliases={i: 2 + i for i in range(n + m)},
        compiler_params=pltpu.CompilerParams(has_side_effects=_EFFECT),
    )(*arrays, after)
    return (outs[0], outs[1], list(outs[2:2 + n]), list(outs[2 + n:2 + n + m])), outs[-1]


def _split_wait(name, state, after, make_copies):
    send_sems, recv_sems, srcs, lands = state
    n, m = len(srcs), len(lands)

    def body(*refs):
        src_refs, land_refs = refs[:n], refs[n:n + m]
        for cp in make_copies(src_refs, land_refs, refs[n + m], refs[n + m + 1]):
            cp.wait_send()
            cp.wait_recv()

    outs = pl.pallas_call(
        body, name=name, out_shape=tuple(pltpu.HBM(a.shape, a.dtype) for a in srcs + lands),
        in_specs=[_HBM] * (n + m) + [_SEM, _SEM, pl.BlockSpec(memory_space=pl.ANY)], out_specs=tuple([_HBM] * (n + m)),
        input_output_aliases={i: i for i in range(n + m)},
        compiler_params=pltpu.CompilerParams(has_side_effects=_EFFECT),
    )(*srcs, *lands, send_sems, recv_sems, after)
    return list(outs[:n]), list(outs[n:])


def _scatter_copies(q_refs, land_refs, send_sems, recv_sems):
    x, y, c = _coords()
    cps = []
    for w, (q, land) in enumerate(zip(q_refs, land_refs)):
        for k, (px, py) in enumerate([(1 - x, y), (x, 1 - y), (1 - x, 1 - y)]):
            cps.append(pltpu.make_async_remote_copy(src_ref=q.at[2 * px + py], dst_ref=land.at[k],
                                                    send_sem=send_sems.at[3 * w + k], recv_sem=recv_sems.at[3 * w + k],
                                                    device_id=(px, py, c), device_id_type=MESH))
    return cps


def _chip_scatter_start(qs, name):
    lands = [lax.empty((3,) + q.shape[1:], q.dtype) for q in qs]
    return _split_start(name, qs, lands, qs[0], 3 * len(qs), _scatter_copies)


def _chip_scatter_wait(state, after, name):
    return _split_wait(name, state, after, _scatter_copies)


def _gather_copies(src_refs, buf_refs, send_sems, recv_sems):
    x, y, c = _coords()
    j = 2 * x + y
    cps = []
    for w, buf in enumerate(buf_refs):
        mine, _ = _half_rows(buf.shape, c)
        for k, (px, py) in enumerate([(1 - x, y), (x, 1 - y), (1 - x, 1 - y)]):
            cps.append(pltpu.make_async_remote_copy(src_ref=buf.at[j, mine], dst_ref=buf.at[j, mine],
                                                    send_sem=send_sems.at[3 * w + k], recv_sem=recv_sems.at[3 * w + k],
                                                    device_id=(px, py, c), device_id_type=MESH))
    return cps


def _gather_wait_copies(src_refs, buf_refs, send_sems, recv_sems):
    x, y, c = _coords()
    j = 2 * x + y
    cps = []
    for w, buf in enumerate(buf_refs):
        mine, _ = _half_rows(buf.shape, c)
        for k, (px, py) in enumerate([(1 - x, y), (x, 1 - y), (1 - x, 1 - y)]):
            cps.append(pltpu.make_async_remote_copy(src_ref=buf.at[j, mine], dst_ref=buf.at[2 * px + py, mine],
                                                    send_sem=send_sems.at[3 * w + k], recv_sem=recv_sems.at[3 * w + k],
                                                    device_id=(px, py, c), device_id_type=MESH))
    return cps


def _pair_forward(bufs, name):
    n = len(bufs)

    def body(*refs):
        ins, outs, send_sems, recv_sems = refs[:n], refs[n:2 * n], refs[2 * n], refs[2 * n + 1]
        x, y, c = _coords()
        chips = [(1 - x, y), (x, 1 - y), (1 - x, 1 - y)]
        cps = []
        for w in range(n):
            mine, _ = _half_rows(outs[w].shape, c)
            for k, (px, py) in enumerate(chips):
                cp = pltpu.make_async_remote_copy(src_ref=ins[w].at[2 * px + py, mine],
                                                  dst_ref=outs[w].at[2 * px + py, mine],
                                                  send_sem=send_sems.at[3 * w + k], recv_sem=recv_sems.at[3 * w + k],
                                                  device_id=(x, y, 1 - c), device_id_type=MESH)
                cp.start()
                cps.append(cp)
        for w in range(n):
            _, sib = _half_rows(outs[w].shape, c)
            for k, (px, py) in enumerate(chips):
                pltpu.make_async_remote_copy(src_ref=ins[w].at[2 * px + py, sib], dst_ref=outs[w].at[2 * px + py, sib],
                                             send_sem=send_sems.at[3 * w + k], recv_sem=recv_sems.at[3 * w + k],
                                             device_id=(x, y, 1 - c), device_id_type=MESH).wait_recv()
        for cp in cps:
            cp.wait_send()

    shapes = [jax.ShapeDtypeStruct(b.shape, b.dtype) for b in bufs]
    return _hbm_call(body, name, bufs, shapes, 3 * n, aliases={i: i for i in range(n)})


def _slot_rows(h, cs):
    return _div_tile(h, max(16, (1 << 19) // cs), 16)


def _cast_into_slot(w, slot, name):
    r, cs = w.shape
    br = _slot_rows(r, cs)
    return _ew_slot(lambda a: a, name, (r // br,), slot, [w], [pl.BlockSpec((br, cs), lambda i, s: (i, 0))],
                    (4, r, cs), BF16, pl.BlockSpec((None, br, cs), lambda i, s: (s[0], i, 0)))


def _pair_add(p, rb, core, name):
    n, r, cs = p.shape
    h = r // 2
    br = _slot_rows(h, cs)
    nb = h // br
    return _ew_slot(lambda a, b: a + b, name, (n, nb), core, [p, rb],
                    [pl.BlockSpec((None, br, cs), lambda s, i, c: (s, c[0] * nb + i, 0)),
                     pl.BlockSpec((None, br, cs), lambda s, i, c: (s, i, 0))],
                    (n, h, cs), BF16, pl.BlockSpec((None, br, cs), lambda s, i, c: (s, i, 0)))


def _chip_sum(q, r3, slots, name):
    _, h, cs = q.shape
    br = _slot_rows(h, cs)

    def fn(a, b):
        acc = a.astype(F32)
        for k in range(3):
            acc = acc + b[k].astype(F32)
        return acc

    return _ew_slot(fn, name, (h // br,), slots, [q, r3],
                    [pl.BlockSpec((None, br, cs), lambda i, s: (s[0], i, 0)),
                     pl.BlockSpec((3, br, cs), lambda i, s: (0, i, 0))],
                    (2, h, cs), F32, pl.BlockSpec((None, br, cs), lambda i, s: (s[1], i, 0)))


def _sum_slots(r, name):
    n, h, w = r.shape
    br = _div_tile(h, 2048, 16)

    def fn(blk):
        acc = blk[0].astype(F32)
        for s in range(1, n):
            acc = acc + blk[s].astype(F32)
        return (acc,)

    return _ew(fn, name, (h // br,), [r], [pl.BlockSpec((n, br, w), lambda i: (0, i, 0))],
               [((h, w), F32, pl.BlockSpec((br, w), lambda i: (i, 0)))])[0]


def _pack_small(arrs):
    flat = jnp.concatenate([a.reshape(-1).astype(F32) for a in arrs])
    n = flat.shape[0]
    rows = -(-n // LANES)
    rows = -(-rows // 8) * 8
    return jnp.pad(flat, (0, rows * LANES - n)).reshape(rows, LANES)


def _unpack_small(p, shapes):
    lead = p.shape[:-2]
    flat = p.reshape(lead + (-1,))
    out, off = [], 0
    for sh in shapes:
        n = 1
        for d in sh:
            n *= d
        out.append(flat[..., off:off + n].reshape(lead + tuple(sh)))
        off += n
    return out


def kernel(x, c, w_ada, b_ada, norm1_g, w_in, dn_conv_w, dn_a_log, dn_dt_bias, dn_norm_g, dn_w_o, cf_conv_w, cf_ln_g, cf_ln_b, cf_w_o, w_out, norm2_g, ffn_w_up, ffn_conv_w, ffn_w_down, final_norm_g, loss_target, m_w_ada, m_b_ada, m_norm1_g, m_w_in, m_dn_conv_w, m_dn_a_log, m_dn_dt_bias, m_dn_norm_g, m_dn_w_o, m_cf_conv_w, m_cf_ln_g, m_cf_ln_b, m_cf_w_o, m_w_out, m_norm2_g, m_ffn_w_up, m_ffn_conv_w, m_ffn_w_down, m_final_norm_g, v_w_ada, v_b_ada, v_norm1_g, v_w_in, v_dn_conv_w, v_dn_a_log, v_dn_dt_bias, v_dn_norm_g, v_dn_w_o, v_cf_conv_w, v_cf_ln_g, v_cf_ln_b, v_cf_w_o, v_w_out, v_norm2_g, v_ffn_w_up, v_ffn_conv_w, v_ffn_w_down, v_final_norm_g):
    xi, yi, ci = _coords()
    chip = 2 * xi + yi
    me = 4 * xi + 2 * yi + ci
    core = jnp.reshape(ci, (1,)).astype(jnp.int32)

    S, D = x.shape[1], x.shape[2]
    NH = dn_a_log.shape[1]
    DH = dn_norm_g.shape[1]
    DNW = NH * DH
    CFW = cf_ln_g.shape[1]
    FFN = ffn_w_down.shape[1] * 4
    KDN, KCF, KFF = dn_conv_w.shape[1], cf_conv_w.shape[1], ffn_conv_w.shape[1]
    NIN = w_in.shape[2] * 4
    assert NIN == 4 * DNW + 2 * NH + 2 * CFW + 2 * D and DH == LANES and 2 * NH <= LANES
    x2, tgt = x[0], loss_target[0]

    sm_shapes = [(D,), (KDN, 3 * DNW // 4), (KCF, CFW // 4), (KFF, FFN // 4)]
    g1 = _all_gather_small(_pack_small([c[0], dn_conv_w[0], cf_conv_w[0], ffn_conv_w[0]]))
    c_all, dcw_s, ccw_s, fcw_s = _unpack_small(g1, sm_shapes)

    def chips_cols(t):
        t = t[0::2]
        return jnp.transpose(t, (1, 0, 2)).reshape(t.shape[1], -1)

    dn_cw, cf_cw, ff_cw = chips_cols(dcw_s), chips_cols(ccw_s), chips_cols(fcw_s)

    slot_chip = jnp.reshape(chip, (1,)).astype(jnp.int32)
    big = [w_in[0], dn_w_o[0], cf_w_o[0], w_out[0], ffn_w_up[0], ffn_w_down[0]]
    names_big = ["w_in", "dn_w_o", "cf_w_o", "w_out", "ffn_w_up", "ffn_w_down"]
    in_state, in_token = _split_start("ag_in_start", [], [_cast_into_slot(big[0], slot_chip, "cast_w_in")], c, 3,
                                      _gather_copies)
    bufs = [_cast_into_slot(w, slot_chip, "cast_" + nm) for w, nm in zip(big[1:], names_big[1:])]
    ag_state, ag_token = _split_start("ag_rest_start", [], bufs, in_token, 3 * len(bufs), _gather_copies)

    cb_cf = _div_tile(CFW, 256, LANES)
    o_b = 4 * DNW
    o_glu = o_b + 2 * NH
    o_ga = o_glu + 2 * CFW
    NA = NIN - 2 * NH
    a_z, a_ga, a_gb, a_glu, a_ba = 3 * DNW, 4 * DNW, 4 * DNW + D, 4 * DNW + 2 * D, NA

    segs = [(0, 0, o_b), (o_b, NA, 2 * NH), (o_ga, a_ga, 2 * D)]
    for t in range(CFW // cb_cf):
        segs += [(o_glu + t * cb_cf, a_glu + 2 * t * cb_cf, cb_cf),
                 (o_glu + CFW + t * cb_cf, a_glu + (2 * t + 1) * cb_cf, cb_cf)]
    CS = NIN // 4

    def shard_slices(lo, n):
        out = []
        while n > 0:
            j, off = lo // CS, lo % CS
            m = min(n, CS - off)
            out.append(w_in_g[j][:, off:off + m])
            lo, n = lo + m, n - m
        return out

    CA = w_ada.shape[2]
    b_sh = lax.dynamic_slice(b_ada, (0, chip * CA), (1, CA))
    tn_a = _div_tile(CA, 512, LANES)

    def mod_fn(cc, w, b):
        return (_mm(_silu(cc), w, NN) + b,)

    mod_sh = _ew(mod_fn, "ada_mod", (CA // tn_a,), [c_all, w_ada[0], b_sh],
                 [_full((8, D)), pl.BlockSpec((D, tn_a), lambda j: (0, j)), pl.BlockSpec((1, tn_a), lambda j: (0, j))],
                 [((8, CA), F32, pl.BlockSpec((8, tn_a), lambda j: (0, j)))])[0]
    g2 = _all_gather_small(_pack_small([mod_sh]))
    mod_all = _unpack_small(g2, [(8, CA)])[0][0::2]
    mod_all = jnp.transpose(mod_all, (1, 0, 2)).reshape(8, 4 * CA)
    mod_me = lax.dynamic_slice(mod_all, (me, 0), (1, 6 * D))
    sh1, sc1, gt1, sh2, sc2, gt2 = [mod_me[:, i * D:(i + 1) * D] for i in range(6)]

    bs = _div_tile(S, 128, 8)
    nb = S // bs
    vecD = _full((1, D))
    rowD = _row(bs, D)

    hn1 = _ew(lambda a, g, sc, sh: (_f_normmod(a, g, sc, sh),), "norm1_fwd", (nb,),
              [x2, norm1_g, sc1, sh1], [rowD, vecD, vecD, vecD], [((S, D), BF16, rowD)],
              after=[ag_token, m_w_in[0], v_w_in[0]])[0]
    _, in_landed = _split_wait("ag_in_wait", in_state, hn1, _gather_wait_copies)
    (w_in_g,) = _pair_forward(in_landed, "ag_in_pair")
    w_aug = jnp.concatenate([p for o, a, n in sorted(segs, key=lambda s: s[1]) for p in shard_slices(o, n)]
                            + [jnp.zeros((D, LANES - 2 * NH), BF16)], axis=1)
    proj = _matmul(hn1, w_aug, "nn", F32, "mm_in")

    def dn_post(j, cv):
        s = _silu(cv)
        nrm = s * lax.rsqrt(jnp.sum(s * s, axis=-1, keepdims=True) + EPS)
        fq = (j < NH).astype(F32)
        fk = (j < 2 * NH).astype(F32)
        scale = fq * (DH ** -0.5) + (1.0 - fq)
        return fk * (nrm * scale) + (1.0 - fk) * s

    def ident(a):
        return a

    def colS(w, off=0):
        return pl.BlockSpec((S, w), lambda j, off=off: (0, j + off))

    def wS(kw, w):
        return pl.BlockSpec((kw, w), lambda j: (0, j))

    qkv_spec = pl.BlockSpec((None, None, S, DH), lambda j: (j // NH, j % NH, 0, 0))
    dn_args = dict(kw=KDN, ncol=3 * NH, ins=[proj], in_specs=[colS(DH)], extras=[], extra_specs=[],
                   w=dn_cw, w_spec=wS(KDN, DH), pre=ident, post=dn_post)
    qkvn = _conv_fwd_call("dn_conv_fwd", out_shape=jax.ShapeDtypeStruct((3, NH, S, DH), F32), out_spec=qkv_spec,
                          **dn_args)

    alp = jnp.pad(dn_a_log, ((0, 0), (NH, LANES - 2 * NH)))
    dtb = jnp.pad(dn_dt_bias, ((0, 0), (NH, LANES - 2 * NH)))
    vecL = _full((1, LANES))
    ba_spec = _row(bs, LANES, a_ba // LANES)
    rowL = _row(bs, LANES)
    gate_fn = functools.partial(_f_dn_gate, NH)
    def gate_fwd(a, p, q):
        val = gate_fn(a, p, q)
        lane = lax.broadcasted_iota(jnp.int32, val.shape, 1)

        def spread(col):
            return jnp.sum(jnp.where(lane == col, val, 0.0), axis=-1, keepdims=True) + jnp.zeros(val.shape, F32)

        return tuple(spread(h) for h in range(NH)), tuple(spread(NH + h) for h in range(NH))

    hrow = pl.BlockSpec((NH, bs, DH), lambda i: (0, i, 0))
    bb_b, gb_b = _ew(gate_fwd, "dn_gate_fwd", (nb,), [proj, alp, dtb], [ba_spec, vecL, vecL],
                     [((NH, S, DH), F32, hrow), ((NH, S, DH), F32, hrow)])
    o_dn, states = _dn_fwd_call(qkvn, gb_b, bb_b)

    bsh = _div_tile(S, 512, 8)
    nbh = S // bsh
    o_spec = pl.BlockSpec((None, bsh, DH), lambda i, h: (h, i, 0))
    z_spec = pl.BlockSpec((bsh, DH), lambda i, h: (i, a_z // DH + h))
    oh_spec = pl.BlockSpec((bsh, DH), lambda i, h: (i, h))
    ng_spec = pl.BlockSpec((1, DH), lambda i, h: (0, 0))
    on = _ew(lambda o, z, g: (_f_dn_post(o, z, g),), "dn_post_fwd", (nbh, NH), [o_dn, proj, dn_norm_g],
             [o_spec, z_spec, ng_spec], [((S, DNW), BF16, oh_spec)])[0]
    _, landed = _split_wait("ag_rest_wait", ag_state, on, _gather_wait_copies)
    w_do_f, w_co_f, w_out_g, w_up_f, w_dn_g = _pair_forward(landed, "ag_rest_pair")
    w_out_f = w_out_g.reshape(-1, w_out_g.shape[2])
    w_dn_f = w_dn_g.reshape(-1, w_dn_g.shape[2])
    br_a = _matmul(on, w_do_f, "nn", F32, "mm_dn_o")

    def glu_pre(val, gl):
        return val * _sigmoid(gl)

    def glu_spec(t):
        return pl.BlockSpec((S, cb_cf), lambda j, t=t: (0, a_glu // cb_cf + 2 * j + t))

    cf_args = dict(kw=KCF, ncol=CFW // cb_cf, ins=[proj, proj], in_specs=[glu_spec(0), glu_spec(1)],
                   extras=[], extra_specs=[], w=cf_cw, w_spec=wS(KCF, cb_cf), pre=glu_pre, post=lambda j, cv: cv)
    uc = _conv_fwd_call("cf_conv_fwd", out_shape=jax.ShapeDtypeStruct((S, CFW), F32), out_spec=colS(cb_cf), **cf_args)
    rowC = _row(bs, CFW)
    vecC = _full((1, CFW))
    ub = _ew(lambda u, g, b: (_f_cf_ln(u, g, b),), "cf_ln_fwd", (nb,), [uc, cf_ln_g, cf_ln_b], [rowC, vecC, vecC],
             [((S, CFW), BF16, rowC)])[0]
    br_b = _matmul(ub, w_co_f, "nn", F32, "mm_cf_o")

    ga_spec, gb_spec = _row(bs, D, a_ga // D), _row(bs, D, a_gb // D)
    merged = _ew(lambda a, b, ga, gb: (_f_merge(a, b, ga, gb),), "merge_fwd", (nb,), [br_a, br_b, proj, proj],
                 [rowD, rowD, ga_spec, gb_spec], [((S, D), BF16, rowD)])[0]
    mix = _matmul(merged, w_out_f, "nn", F32, "mm_out")

    x1, hn2 = _ew(_f_res_normmod, "norm2_fwd", (nb,), [x2, mix, gt1, norm2_g, sc2, sh2],
                  [rowD, rowD, vecD, vecD, vecD, vecD], [((S, D), F32, rowD), ((S, D), BF16, rowD)])
    up_all = _matmul(hn2, w_up_f, "nn", F32, "mm_up")

    cb_ff = _div_tile(FFN, 256, LANES)
    ff_args = dict(kw=KFF, ncol=FFN // cb_ff, ins=[up_all], in_specs=[colS(cb_ff)], extras=[up_all],
                   extra_specs=[colS(cb_ff, FFN // cb_ff)], w=ff_cw, w_spec=wS(KFF, cb_ff),
                   pre=ident, post=lambda j, cv, up: _silu(cv) * up)
    hff = _conv_fwd_call("ffn_conv_fwd", out_shape=jax.ShapeDtypeStruct((S, FFN), BF16), out_spec=colS(cb_ff), **ff_args)
    ffo = _matmul(hff, w_dn_f, "nn", F32, "mm_down")

    gf2 = final_norm_g.reshape(1, D)

    def loss_bwd(a, f, gt, gf, t):
        val, vjp = jax.vjp(_f_loss, a, f, gt, gf, t)
        da, df, dgt, dgf, _ = vjp(jnp.ones((), F32))
        return da, df, dgt, dgf, jnp.zeros((1, LANES), F32) + val

    dx1_l, dffo, dgt2, dgf, loss_v = _ew(
        loss_bwd, "loss_bwd", (nb,), [x1, ffo, gt2, gf2, tgt], [rowD, rowD, vecD, vecD, rowD],
        [((S, D), F32, rowD), ((S, D), BF16, rowD), ((1, D), F32, vecD), ((1, D), F32, vecD),
         ((1, LANES), F32, vecL)], acc=(2, 3, 4))

    dhff = _matmul(dffo, w_dn_f, "nt", F32, "mm_down_dx")
    g_w_dn = _matmul(hff, dffo, "tn", F32, "mm_down_dw")

    slots = jnp.stack([chip, ci]).astype(jnp.int32)
    rs_groups = []

    def rs_begin(parts, nms, tag):
        rbs = _pair_swap_half(parts, "rs_pair_" + tag)
        q16 = [_pair_add(p, rb, core, "rs_pair_add_" + nm) for p, rb, nm in zip(parts, rbs, nms)]
        state, token = _chip_scatter_start(q16, "rs_chips_start_" + tag)
        rs_groups.append((state, nms, tag))
        return token

    tok_a = rs_begin([g_w_dn.reshape(4, FFN // 4, D)], ["ffn_w_down"], "a")

    d_upall, g_ffcw = _conv_bwd_call(
        "ffn_conv_bwd", dout=dhff, dout_spec=colS(cb_ff),
        dio_shapes=[jax.ShapeDtypeStruct((2, S, FFN), BF16)],
        dio_specs=[pl.BlockSpec((2, S, cb_ff), lambda j: (0, 0, j))], dio_pack=lambda dg, du: [(dg, du)],
        dw_shape=jax.ShapeDtypeStruct((KFF, FFN), F32), dw_spec=wS(KFF, cb_ff), after=[tok_a], **ff_args)
    dhn2 = _matmul(d_upall, w_up_f, "nt", F32, "mm_up_dx")
    g_w_up = _matmul(hn2, d_upall, "tn", F32, "mm_up_dw", out_groups=4)
    tok_b = rs_begin([g_w_up], ["ffn_w_up"], "b")

    def res2_bwd(a, mx, gt, g, sc, sh, dx1, dhn):
        _, vjp = jax.vjp(_f_res_normmod, a, mx, gt, g, sc, sh)
        return vjp((dx1, dhn))

    dx_r, dmix, dgt1, dg2, dsc2, dsh2 = _ew(
        res2_bwd, "norm2_bwd", (nb,), [x2, mix, gt1, norm2_g, sc2, sh2, dx1_l, dhn2],
        [rowD, rowD, vecD, vecD, vecD, vecD, rowD, rowD],
        [((S, D), F32, rowD), ((S, D), BF16, rowD)] + [((1, D), F32, vecD)] * 4, acc=(2, 3, 4, 5), after=[tok_b])

    dmerged = _matmul(dmix, w_out_f, "nt", F32, "mm_out_dx")
    g_w_out = _matmul(merged, dmix, "tn", F32, "mm_out_dw")

    def merge_bwd(a, b, ga, gb, dm):
        _, vjp = jax.vjp(_f_merge, a, b, ga, gb)
        da, db, dga, dgb = vjp(dm)
        return da, db, jnp.concatenate([dga, dgb], axis=1)

    d_bra, d_brb, dproj = _ew(merge_bwd, "merge_bwd", (nb,), [br_a, br_b, proj, proj, dmerged],
                              [rowD, rowD, ga_spec, gb_spec, rowD],
                              [((S, D), BF16, rowD), ((S, D), BF16, rowD),
                               ((S, NA + LANES), BF16, _row(bs, 2 * D, a_ga // (2 * D)))])

    d_on = _matmul(d_bra, w_do_f, "nt", F32, "mm_dn_o_dx")
    g_w_do = _matmul(on, d_bra, "tn", F32, "mm_dn_o_dw", out_groups=4)
    d_ub = _matmul(d_brb, w_co_f, "nt", F32, "mm_cf_o_dx")
    g_w_co = _matmul(ub, d_brb, "tn", F32, "mm_cf_o_dw", out_groups=4)
    tok_c = rs_begin([g_w_do, g_w_co, g_w_out.reshape(4, D // 4, D)], ["dn_w_o", "cf_w_o", "w_out"], "c")

    def cf_ln_bwd(u, g, b, du):
        _, vjp = jax.vjp(_f_cf_ln, u, g, b)
        return vjp(du)

    d_uc, g_cflg, g_cflb = _ew(cf_ln_bwd, "cf_ln_bwd", (nb,), [uc, cf_ln_g, cf_ln_b, d_ub], [rowC, vecC, vecC, rowC],
                               [((S, CFW), F32, rowC), ((1, CFW), F32, vecC), ((1, CFW), F32, vecC)], acc=(1, 2),
                               after=[tok_c])
    dproj_sds = jax.ShapeDtypeStruct((S, NA + LANES), BF16)
    dproj, g_cfcw = _conv_bwd_call(
        "cf_conv_bwd", dout=d_uc, dout_spec=colS(cb_cf), dio_shapes=[dproj_sds],
        dio_specs=[colS(2 * cb_cf, a_glu // (2 * cb_cf))], dio_pack=lambda dv, dg: [jnp.concatenate([dv, dg], axis=1)],
        dw_shape=jax.ShapeDtypeStruct((KCF, CFW), F32), dw_spec=wS(KCF, cb_cf), alias=dproj, **cf_args)

    def dn_post_bwd(o, z, g, d):
        _, vjp = jax.vjp(_f_dn_post, o, z, g)
        return vjp(d)

    d_o, dproj, g_dnng = _ew(dn_post_bwd, "dn_post_bwd", (nbh, NH), [o_dn, proj, dn_norm_g, d_on],
                             [o_spec, z_spec, ng_spec, oh_spec],
                             [((NH, S, DH), F32, o_spec), ((S, NA + LANES), BF16, z_spec), ((1, DH), F32, ng_spec)],
                             acc=(2,), alias=(dproj, 1))

    dqkvn, dgb_b, dbb_b = _dn_bwd_call(qkvn, gb_b, bb_b, states, d_o)
    dproj, g_dncw = _conv_bwd_call(
        "dn_conv_bwd", dout=dqkvn, dout_spec=qkv_spec, dio_shapes=[dproj_sds], dio_specs=[colS(DH)],
        dio_pack=lambda d: [d], dw_shape=jax.ShapeDtypeStruct((KDN, 3 * DNW), F32), dw_spec=wS(KDN, DH),
        alias=dproj, **dn_args)

    def gate_bwd(a, p, q, db, dg):
        lane = lax.broadcasted_iota(jnp.int32, a.shape, 1)
        d = jnp.zeros(a.shape, F32)
        for h in range(NH):
            d = d + jnp.where(lane == h, db[h], 0.0) + jnp.where(lane == NH + h, dg[h], 0.0)
        _, vjp = jax.vjp(gate_fn, a, p, q)
        return vjp(d)

    dproj, g_alp, g_dtb = _ew(gate_bwd, "dn_gate_bwd", (nb,), [proj, alp, dtb, dbb_b, dgb_b],
                              [ba_spec, vecL, vecL, hrow, hrow],
                              [((S, NA + LANES), BF16, ba_spec), ((1, LANES), F32, vecL), ((1, LANES), F32, vecL)],
                              acc=(1, 2), alias=(dproj, 0))

    g_w_aug = _matmul(hn1, dproj, "tn", F32, "mm_in_dw")
    def aug_slices(lo, n):
        out = []
        for o, a, m in sorted(segs):
            s, e = max(lo, o), min(lo + n, o + m)
            if s < e:
                out.append(g_w_aug[:, a + s - o:a + e - o])
        return out

    g_w_in = jnp.stack([jnp.concatenate(aug_slices(j * CS, CS), axis=1) for j in range(4)])
    tok_d = rs_begin([g_w_in], ["w_in"], "d")
    dhn1 = _matmul(dproj, w_aug, "nt", F32, "mm_in_dx", after=[tok_d])

    def norm1_bwd(a, g, sc, sh, dhn, dxr):
        _, vjp = jax.vjp(_f_normmod, a, g, sc, sh)
        da, dg, dsc, dsh = vjp(dhn)
        return da + dxr, dg, dsc, dsh

    grad_x, dg1, dsc1, dsh1 = _ew(norm1_bwd, "norm1_bwd", (nb,), [x2, norm1_g, sc1, sh1, dhn1, dx_r],
                                  [rowD, vecD, vecD, vecD, rowD, rowD],
                                  [((S, D), F32, rowD)] + [((1, D), F32, vecD)] * 3, acc=(1, 2, 3))

    names = ["w_ada", "b_ada", "norm1_g", "w_in", "dn_conv_w", "dn_a_log", "dn_dt_bias", "dn_norm_g", "dn_w_o",
             "cf_conv_w", "cf_ln_g", "cf_ln_b", "cf_w_o", "w_out", "norm2_g", "ffn_w_up", "ffn_conv_w", "ffn_w_down",
             "final_norm_g"]
    wts = dict(zip(names, [w_ada, b_ada, norm1_g, w_in, dn_conv_w, dn_a_log, dn_dt_bias, dn_norm_g, dn_w_o, cf_conv_w,
                           cf_ln_g, cf_ln_b, cf_w_o, w_out, norm2_g, ffn_w_up, ffn_conv_w, ffn_w_down, final_norm_g]))
    ms = dict(zip(names, [m_w_ada, m_b_ada, m_norm1_g, m_w_in, m_dn_conv_w, m_dn_a_log, m_dn_dt_bias, m_dn_norm_g,
                          m_dn_w_o, m_cf_conv_w, m_cf_ln_g, m_cf_ln_b, m_cf_w_o, m_w_out, m_norm2_g, m_ffn_w_up,
                          m_ffn_conv_w, m_ffn_w_down, m_final_norm_g]))
    vs = dict(zip(names, [v_w_ada, v_b_ada, v_norm1_g, v_w_in, v_dn_conv_w, v_dn_a_log, v_dn_dt_bias, v_dn_norm_g,
                          v_dn_w_o, v_cf_conv_w, v_cf_ln_g, v_cf_ln_b, v_cf_w_o, v_w_out, v_norm2_g, v_ffn_w_up,
                          v_ffn_conv_w, v_ffn_w_down, v_final_norm_g]))
    grads, delta, new_m, new_v = {}, {}, {}, {}

    def adam_large(n, g):
        grads[n] = g.reshape(wts[n].shape)
        d_, m_, v_ = _adamw(wts[n][0], grads[n][0], ms[n][0], vs[n][0], "adamw_" + n)
        delta[n], new_m[n], new_v[n] = d_[None], m_[None], v_[None]

    def rs_finish(groups, after, tag):
        hs, nms_all = [], []
        for state, nms, t in groups:
            q16, r3s = _chip_scatter_wait(state, after, "rs_chips_wait_" + t)
            for q, r3, nm in zip(q16, r3s, nms):
                hs.append(_chip_sum(q, r3, slots, "rs_chip_sum_" + nm))
                nms_all.append(nm)
        for nm, g in zip(nms_all, _pair_gather(hs, "rs_pair_gather_" + tag)):
            adam_large(nm, g)

    rs_finish(rs_groups[:3], grad_x, "abc")
    rs_finish(rs_groups[3:], delta["ffn_w_up"], "d")

    dmod = jnp.concatenate([dsh1, dsc1, dgt1, dsh2, dsc2, dgt2], axis=1)
    sm2 = [dmod, dg1, dg2, dgf, g_alp, g_dtb, g_dnng, g_cflg, g_cflb, g_dncw, g_cfcw, g_ffcw]
    sm2_shapes = [tuple(a.shape) for a in sm2]
    g3 = _all_gather_small(_pack_small(sm2))
    ssum = _sum_slots(g3, "small_sum")
    dmod_all = _unpack_small(g3, sm2_shapes[:1])[0].reshape(8, 6 * D)
    (g_b_ada, gs_n1, gs_n2, gs_fn, gs_alp, gs_dtb, gs_dnng, gs_cflg, gs_cflb, gs_dncw, gs_cfcw,
     gs_ffcw) = _unpack_small(ssum, sm2_shapes)
    gs_alog, gs_dtbias = gs_alp[:, NH:2 * NH], gs_dtb[:, NH:2 * NH]
    gs_dncw = lax.dynamic_slice(gs_dncw, (0, chip * (3 * DNW // 4)), (KDN, 3 * DNW // 4))
    gs_cfcw = lax.dynamic_slice(gs_cfcw, (0, chip * (CFW // 4)), (KCF, CFW // 4))
    gs_ffcw = lax.dynamic_slice(gs_ffcw, (0, chip * (FFN // 4)), (KFF, FFN // 4))

    dmod_sh = lax.dynamic_slice(dmod_all, (0, chip * CA), (8, CA))

    def wada_fn(cc, dm):
        return (_mm(_silu(cc), dm, TN),)

    g_w_ada = _ew(wada_fn, "ada_dw", (CA // tn_a,), [c_all, dmod_sh],
                  [_full((8, D)), pl.BlockSpec((8, tn_a), lambda j: (0, j))],
                  [((D, CA), F32, pl.BlockSpec((D, tn_a), lambda j: (0, j)))])[0]

    loss = lax.psum(loss_v[0, 0], ("x", "y", "c"))

    adam_large("w_ada", g_w_ada)
    small_grads = {"b_ada": g_b_ada, "norm1_g": gs_n1, "dn_conv_w": gs_dncw, "dn_a_log": gs_alog,
                   "dn_dt_bias": gs_dtbias, "dn_norm_g": gs_dnng, "cf_conv_w": gs_cfcw, "cf_ln_g": gs_cflg,
                   "cf_ln_b": gs_cflb, "norm2_g": gs_n2, "ffn_conv_w": gs_ffcw, "final_norm_g": gs_fn}
    small = [n for n in names if n in small_grads]
    for n in small:
        grads[n] = small_grads[n].reshape(wts[n].shape)
    sm_sh = [tuple(wts[n].shape) for n in small]
    d_, m_, v_ = _adamw(_pack_small([wts[n] for n in small]), _pack_small([grads[n] for n in small]),
                        _pack_small([ms[n] for n in small]), _pack_small([vs[n] for n in small]), "adamw_small")
    for n, a, b_, c_ in zip(small, _unpack_small(d_, sm_sh), _unpack_small(m_, sm_sh), _unpack_small(v_, sm_sh)):
        delta[n], new_m[n], new_v[n] = a, b_, c_

    return (loss, grad_x[None], *[grads[n] for n in names], *[delta[n] for n in names],
            *[new_m[n] for n in names], *[new_v[n] for n in names])
```

```python
import functools

import jax
import jax.numpy as jnp
from jax import lax
from jax.experimental import pallas as pl
from jax.experimental.pallas import tpu as pltpu

F32 = jnp.float32
BF16 = jnp.bfloat16
EPS = 1e-6
LANES = 128
VMEM_LIMIT = 48 * 1024 * 1024
DN_CHUNK = 128
ADAM_LR, ADAM_B1, ADAM_B2, ADAM_EPS, ADAM_WD, ADAM_STEP = 0.001, 0.9, 0.999, 1e-08, 0.01, 10
MESH = pl.DeviceIdType.MESH

NN = (((1,), (0,)), ((), ()))
NT = (((1,), (1,)), ((), ()))
TN = (((0,), (0,)), ((), ()))
_DIMS = {"nn": NN, "nt": NT, "tn": TN}


def _mm(a, b, dims):
    return lax.dot_general(a.astype(BF16), b.astype(BF16), dims, preferred_element_type=F32)


def _mmx(a, b, dims):
    return lax.dot_general(a, b, dims, precision=lax.Precision.HIGH, preferred_element_type=F32)


def _div_tile(n, target, mult):
    best = None
    t = mult
    while t <= min(n, target):
        if n % t == 0:
            best = t
        t += mult
    return best if best is not None else n


def _params(sem=None):
    kw = dict(vmem_limit_bytes=VMEM_LIMIT)
    if sem is not None:
        kw["dimension_semantics"] = sem
    return pltpu.CompilerParams(**kw)


def _sigmoid(x):
    return jax.nn.sigmoid(x)


def _silu(x):
    return x * jax.nn.sigmoid(x)


def _softplus(x):
    return jnp.maximum(x, 0.0) + jnp.log(1.0 + jnp.exp(-jnp.abs(x)))


def _view(arr):
    if arr.ndim == 3:
        return arr.shape[1], arr.shape[0] * arr.shape[2], arr.shape[0]
    return arr.shape[0], arr.shape[1], 1


def _tile_spec(groups, cols, tr, tc, rsel, csel):
    if groups > 1:
        per = cols // groups // tc
        return pl.BlockSpec((None, tr, tc), lambda i, j, k: (csel(i, j, k) // per, rsel(i, j, k), csel(i, j, k) % per))
    return pl.BlockSpec((tr, tc), lambda i, j, k: (rsel(i, j, k), csel(i, j, k)))


def _matmul(a, b, mode, out_dtype, name, out_groups=1, after=()):
    ar, ac, ag = _view(a)
    br, bc, bg = _view(b)
    if mode == "nn":
        M, K, N = ar, ac, bc
        kdiv, mdiv, ndiv = ac // ag, M, min(bc // bg, N // out_groups)
    elif mode == "nt":
        M, K, N = ar, ac, br
        kdiv, mdiv, ndiv = min(ac // ag, bc // bg), M, N // out_groups
    else:
        K, M, N = ar, ac, bc
        kdiv, mdiv, ndiv = K, ac // ag, min(bc // bg, N // out_groups)
    tm = _div_tile(mdiv, 1024, LANES)
    tn = _div_tile(ndiv, 1536, LANES)
    tk = _div_tile(kdiv, 2048, LANES)
    nk = K // tk
    dims = _DIMS[mode]
    si, sj, sk = (lambda i, j, k: i), (lambda i, j, k: j), (lambda i, j, k: k)
    a_spec = {"nn": _tile_spec(ag, ac, tm, tk, si, sk), "nt": _tile_spec(ag, ac, tm, tk, si, sk),
              "tn": _tile_spec(ag, ac, tk, tm, sk, si)}[mode]
    b_spec = {"nn": _tile_spec(bg, bc, tk, tn, sk, sj), "nt": _tile_spec(bg, bc, tn, tk, sj, sk),
              "tn": _tile_spec(bg, bc, tk, tn, sk, sj)}[mode]
    out_shape = (M, N) if out_groups == 1 else (out_groups, M, N // out_groups)

    n_after = len(after)

    def body(*refs):
        a_ref, b_ref, o_ref = refs[0], refs[1], refs[2 + n_after]
        if nk == 1:
            o_ref[...] = lax.dot_general(a_ref[...], b_ref[...], dims, preferred_element_type=F32).astype(o_ref.dtype)
            return
        acc_ref = refs[3 + n_after]
        k = pl.program_id(2)

        @pl.when(k == 0)
        def _():
            acc_ref[...] = jnp.zeros_like(acc_ref)

        acc_ref[...] += lax.dot_general(a_ref[...], b_ref[...], dims, preferred_element_type=F32)

        @pl.when(k == nk - 1)
        def _():
            o_ref[...] = acc_ref[...].astype(o_ref.dtype)

    return pl.pallas_call(
        body, name=name, grid=(M // tm, N // tn, nk),
        in_specs=[a_spec, b_spec] + [pl.BlockSpec(memory_space=pl.ANY)] * n_after,
        out_specs=_tile_spec(out_groups, N, tm, tn, si, sj),
        out_shape=jax.ShapeDtypeStruct(out_shape, out_dtype),
        scratch_shapes=[pltpu.VMEM((tm, tn), F32)] if nk > 1 else [],
        compiler_params=_params(("parallel", "parallel", "arbitrary")),
    )(a, b, *after)


def _ew(fn, name, grid, ins, in_specs, outs, acc=(), alias=None, after=()):
    n_in = len(ins)
    n_ax = len(grid)
    extra, aliases = list(after), {}
    if alias is not None:
        extra, aliases = extra + [alias[0]], {n_in + len(after): alias[1]}
    extra_specs = [pl.BlockSpec(memory_space=pl.ANY)] * len(extra)

    def body(*refs):
        in_refs, out_refs = refs[:n_in], refs[n_in + len(extra):]
        ids = [pl.program_id(a) for a in range(n_ax)]
        res = fn(*[r[...] for r in in_refs])
        first = ids[0] == 0
        for t in ids[1:]:
            first = jnp.logical_and(first, t == 0)
        for idx, (r, val) in enumerate(zip(out_refs, res)):
            if idx in acc:
                @pl.when(first)
                def _(r=r, val=val):
                    r[...] = val.astype(r.dtype)

                @pl.when(jnp.logical_not(first))
                def _(r=r, val=val):
                    r[...] += val.astype(r.dtype)
            elif isinstance(val, tuple):
                for t, part in enumerate(val):
                    r[t] = part.astype(r.dtype)
            else:
                r[...] = val.astype(r.dtype)

    return pl.pallas_call(
        body, name=name, grid=grid, in_specs=list(in_specs) + extra_specs,
        out_specs=[o[2] for o in outs],
        out_shape=[jax.ShapeDtypeStruct(o[0], o[1]) for o in outs],
        input_output_aliases=aliases,
        compiler_params=_params(("arbitrary",) * n_ax),
    )(*ins, *extra)


def _ew_slot(fn, name, grid, slots, ins, in_specs, out_shape, out_dtype, out_spec):
    def body(s_ref, *refs):
        refs[-1][...] = fn(*[r[...] for r in refs[:-1]]).astype(refs[-1].dtype)

    return pl.pallas_call(
        body, name=name,
        grid_spec=pltpu.PrefetchScalarGridSpec(num_scalar_prefetch=1, grid=grid, in_specs=list(in_specs),
                                               out_specs=out_spec),
        out_shape=jax.ShapeDtypeStruct(out_shape, out_dtype),
        compiler_params=_params(("arbitrary",) * len(grid)),
    )(slots, *ins)


def _row(bs, w, col=0):
    return pl.BlockSpec((bs, w), lambda i, col=col: (i, col))


def _full(shape):
    nd = len(shape)
    return pl.BlockSpec(tuple(shape), lambda *_: (0,) * nd)


def _rms(x, g):
    return x * lax.rsqrt(jnp.mean(x * x, axis=-1, keepdims=True) + EPS) * g


def _f_normmod(x, g, sc, sh):
    return _rms(x, g) * (1.0 + sc) + sh


def _f_res_normmod(x, mix, gt, g, sc, sh):
    x1 = x + gt * mix
    return x1, _f_normmod(x1, g, sc, sh)


def _f_loss(x1, f, gt, gf, tgt):
    y = _rms(x1 + gt * f, gf)
    return 0.5 * jnp.sum(jnp.mean(jnp.square(y - tgt), axis=-1))


def _f_dn_gate(nh, ba, alp, dtb):
    lane = lax.broadcasted_iota(jnp.int32, ba.shape, 1)
    m = (lane < nh).astype(F32)
    beta = _sigmoid(ba)
    g = -jnp.exp(alp) * _softplus(ba + dtb)
    return m * beta + (1.0 - m) * g


def _f_dn_post(o, z, g):
    return o * lax.rsqrt(jnp.mean(o * o, axis=-1, keepdims=True) + EPS) * g * _silu(z)


def _f_cf_ln(u, g, b):
    mu = jnp.mean(u, axis=-1, keepdims=True)
    xc = u - mu
    y = xc * lax.rsqrt(jnp.mean(xc * xc, axis=-1, keepdims=True) + EPS)
    return _silu(y * g + b)


def _f_merge(a, b, ga, gb):
    return _sigmoid(ga) * a + _sigmoid(gb) * b


def _shift_down(u, d, rows):
    if d == 0:
        return u
    return jnp.where(rows >= d, pltpu.roll(u, d, 0), 0.0)


def _shift_up(u, d, rows):
    if d == 0:
        return u
    s = u.shape[0]
    return jnp.where(rows < s - d, pltpu.roll(u, s - d, 0), 0.0)


def _conv(u, w_ref, kw, rows):
    acc = None
    for k in range(kw):
        t = w_ref[k:k + 1, :] * _shift_down(u, kw - 1 - k, rows)
        acc = t if acc is None else acc + t
    return acc


def _conv_t(dc, w_ref, kw, rows):
    acc = None
    for k in range(kw):
        t = w_ref[k:k + 1, :] * _shift_up(dc, kw - 1 - k, rows)
        acc = t if acc is None else acc + t
    return acc


def _conv_fwd_call(name, kw, ncol, ins, in_specs, extras, extra_specs, w, w_spec, pre, post, out_shape, out_spec):
    n_in, n_ex = len(ins), len(extras)

    def body(*refs):
        in_refs, ex_refs = refs[:n_in], refs[n_in:n_in + n_ex]
        w_ref, out_ref = refs[n_in + n_ex], refs[n_in + n_ex + 1]
        j = pl.program_id(0)
        u = pre(*[r[...] for r in in_refs])
        rows = lax.broadcasted_iota(jnp.int32, u.shape, 0)
        cv = _conv(u, w_ref, kw, rows)
        out_ref[...] = post(j, cv, *[r[...] for r in ex_refs]).astype(out_ref.dtype)

    return pl.pallas_call(
        body, name=name, grid=(ncol,), in_specs=list(in_specs) + list(extra_specs) + [w_spec],
        out_specs=out_spec, out_shape=out_shape, compiler_params=_params(("arbitrary",)),
    )(*ins, *extras, w)


def _conv_bwd_call(name, kw, ncol, ins, in_specs, extras, extra_specs, w, w_spec, pre, post, dout, dout_spec,
                   dio_shapes, dio_specs, dio_pack, dw_shape, dw_spec, alias=None, after=()):
    n_in, n_ex, n_io = len(ins), len(extras), len(dio_shapes)
    al, aliases = list(after), {}
    if alias is not None:
        al, aliases = al + [alias], {n_in + n_ex + 2 + len(after): 0}
    al_specs = [pl.BlockSpec(memory_space=pl.ANY)] * len(al)

    def body(*refs):
        in_refs, ex_refs = refs[:n_in], refs[n_in:n_in + n_ex]
        w_ref, dout_ref = refs[n_in + n_ex], refs[n_in + n_ex + 1]
        outs = refs[n_in + n_ex + 2 + len(al):]
        dio_refs, dw_ref = outs[:n_io], outs[n_io]
        j = pl.program_id(0)
        u, pre_vjp = jax.vjp(pre, *[r[...] for r in in_refs])
        rows = lax.broadcasted_iota(jnp.int32, u.shape, 0)
        cv = _conv(u, w_ref, kw, rows)
        _, post_vjp = jax.vjp(lambda cc, *ex: post(j, cc, *ex), cv, *[r[...] for r in ex_refs])
        g = post_vjp(dout_ref[...].astype(F32))
        dc = g[0]
        for k in range(kw):
            dw_ref[k:k + 1, :] = jnp.sum(dc * _shift_down(u, kw - 1 - k, rows), axis=0, keepdims=True)
        du = _conv_t(dc, w_ref, kw, rows)
        for r, val in zip(dio_refs, dio_pack(*pre_vjp(du), *g[1:])):
            if isinstance(val, tuple):
                for t, part in enumerate(val):
                    r[t] = part.astype(r.dtype)
            else:
                r[...] = val.astype(r.dtype)

    return pl.pallas_call(
        body, name=name, grid=(ncol,),
        in_specs=list(in_specs) + list(extra_specs) + [w_spec, dout_spec] + al_specs,
        out_specs=list(dio_specs) + [dw_spec],
        out_shape=list(dio_shapes) + [dw_shape],
        input_output_aliases=aliases,
        compiler_params=_params(("arbitrary",)),
    )(*ins, *extras, w, dout, *al)


def _tri_inverse(a):
    c = a.shape[0]
    ii = lax.broadcasted_iota(jnp.int32, (c, c), 0)
    jj = lax.broadcasted_iota(jnp.int32, (c, c), 1)
    eye = (ii == jj).astype(F32)

    def same_block(bits):
        return jnp.right_shift(ii, bits) == jnp.right_shift(jj, bits)

    d = jnp.where(same_block(3), a, 0.0)
    d2 = _mmx(d, d, NN)
    yield
    t = (eye - d) + _mmx(eye - d, d2, NN)
    d4 = _mmx(d2, d2, NN)
    yield
    t = t + _mmx(t, d4, NN)
    yield
    bits = 3
    while (1 << bits) < c:
        low = jnp.where(jnp.logical_and(same_block(bits + 1), jnp.logical_not(same_block(bits))), a, 0.0)
        tl = _mmx(t, low, NN)
        yield
        t = t - _mmx(tl, t, NN)
        yield
        bits += 1
    return t


def _lockstep(gens):
    out = [None] * len(gens)
    live = list(range(len(gens)))
    while live:
        nxt = []
        for i in live:
            try:
                next(gens[i])
                nxt.append(i)
            except StopIteration as e:
                out[i] = e.value
        live = nxt
    return out


def _dn_common(q, k, v, gb, bb):
    c = q.shape[0]
    ii = lax.broadcasted_iota(jnp.int32, (c, c), 0)
    jj = lax.broadcasted_iota(jnp.int32, (c, c), 1)
    causal = jj <= ii
    strict = jj < ii
    low = causal.astype(F32)
    ones = jnp.ones((c, LANES), F32)
    gc = _mmx(low, gb, NN)
    kb = k * bb
    vb = v * bb
    kk = _mm(kb, k, NT)
    qk = _mm(q, k, NT)
    yield
    diff = (_mmx(gc, ones, NT) - _mmx(ones, gc, NT)) * (1.0 / LANES)
    gl = jnp.sum(gb, axis=0, keepdims=True)
    eg = jnp.exp(gc)
    egm = jnp.exp(gl - gc)
    egl = jnp.exp(gl)
    kbg = kb * eg
    yield
    decay = jnp.where(causal, jnp.exp(jnp.where(causal, diff, 0.0)), 0.0)
    t = yield from _tri_inverse(jnp.where(strict, kk * decay, 0.0))
    attn = qk * decay
    return dict(causal=causal, strict=strict, low=low, ones=ones, decay=decay, eg=eg, egm=egm, egl=egl,
                kb=kb, vb=vb, kbg=kbg, kk=kk, t=t, qk=qk, attn=attn, qd=q * eg, kd=k * egm)


def _dn_fwd_stages(q, k, v, gb, bb, s):
    m = yield from _dn_common(q, k, v, gb, bb)
    r = m["vb"] - _mm(m["kbg"], s, NN)
    qs = _mm(m["qd"], s, NN)
    yield
    u = _mmx(m["t"], r, NN)
    yield
    o = qs + _mm(m["attn"], u, NN)
    s2 = s * m["egl"] + _mm(m["kd"], u, TN)
    return o, s2


def _dn_bwd_stages(q, k, v, gb, bb, s, do, dsp):
    m = yield from _dn_common(q, k, v, gb, bb)
    c = q.shape[0]
    t, decay, eg, egm, egl = m["t"], m["decay"], m["eg"], m["egm"], m["egl"]
    r = m["vb"] - _mm(m["kbg"], s, NN)
    du = _mm(m["attn"], do, TN) + _mm(m["kd"], dsp, NN)
    dqd = _mm(do, s, NT)
    ds = dsp * egl + _mm(m["qd"], do, TN)
    degl = jnp.sum(jnp.sum(dsp * s, axis=1, keepdims=True), axis=0, keepdims=True)
    yield
    u = _mmx(t, r, NN)
    dr = _mmx(t, du, TN)
    yield
    dattn = jnp.where(m["causal"], _mm(do, u, NT), 0.0)
    dkd = _mm(u, dsp, NT)
    da = jnp.where(m["strict"], -_mm(dr, u, NT), 0.0)
    dkbg = -_mm(dr, s, NT)
    ds = ds - _mm(m["kbg"], dr, TN)
    yield
    dkk = da * decay
    dqk = dattn * decay
    ddiff = (da * m["kk"] + dattn * m["qk"]) * decay
    dgc = _mmx(ddiff, m["ones"], NN) - _mmx(ddiff, m["ones"], TN)
    dkb = _mm(dkk, k, NN) + dkbg * eg
    dk = _mm(dkk, m["kb"], TN) + _mm(dqk, q, TN) + dkd * egm + dkb * bb
    dq = _mm(dqk, k, NN) + dqd * eg
    yield
    dgc = dgc + jnp.sum(dqd * q + dkbg * m["kb"], axis=-1, keepdims=True) * eg
    tt = jnp.sum(dkd * k, axis=-1, keepdims=True) * egm
    dgc = dgc - tt
    dgl = jnp.sum(tt, axis=0, keepdims=True) + degl * egl
    dbb = jnp.sum(dkb * k + dr * v, axis=-1, keepdims=True) + jnp.zeros((c, LANES), F32)
    dv = dr * bb
    dgb = _mmx(m["low"], dgc, TN) + dgl
    return dq, dk, dv, dgb, dbb, ds


def _dn_fwd_chunk(q, k, v, gb, bb, s):
    return _lockstep([_dn_fwd_stages(q, k, v, gb, bb, s)])[0]


def _dn_bwd_chunk(q, k, v, gb, bb, s, do, dsp):
    return _lockstep([_dn_bwd_stages(q, k, v, gb, bb, s, do, dsp)])[0]


def _dn_fwd_call(qkvn, gb, bb):
    _, nh, s, dh = qkvn.shape
    c = min(DN_CHUNK, s)
    n = s // c
    hb = nh

    def body(q_ref, k_ref, v_ref, g_ref, b_ref, o_ref, st_ref, s_ref):
        @pl.when(pl.program_id(1) == 0)
        def _():
            s_ref[...] = jnp.zeros_like(s_ref)

        st_ref[...] = s_ref[...]
        heads = [_dn_fwd_stages(q_ref[h], k_ref[h], v_ref[h], g_ref[h], b_ref[h], s_ref[h]) for h in range(hb)]
        for h, (o, s2) in enumerate(_lockstep(heads)):
            o_ref[h] = o
            s_ref[h] = s2

    def qspec(t):
        return pl.BlockSpec((None, hb, c, dh), lambda i, j, t=t: (t, i, j, 0))

    hs = pl.BlockSpec((hb, c, dh), lambda i, j: (i, j, 0))
    return pl.pallas_call(
        body, name="dn_fwd", grid=(nh // hb, n),
        in_specs=[qspec(0), qspec(1), qspec(2), hs, hs],
        out_specs=[hs, pl.BlockSpec((hb, None, dh, dh), lambda i, j: (i, j, 0, 0))],
        out_shape=[jax.ShapeDtypeStruct((nh, s, dh), F32), jax.ShapeDtypeStruct((nh, n, dh, dh), F32)],
        scratch_shapes=[pltpu.VMEM((hb, dh, dh), F32)],
        compiler_params=_params(("arbitrary", "arbitrary")),
    )(qkvn, qkvn, qkvn, gb, bb)


def _dn_bwd_call(qkvn, gb, bb, states, do):
    _, nh, s, dh = qkvn.shape
    c = min(DN_CHUNK, s)
    n = s // c
    hb = nh

    def body(q_ref, k_ref, v_ref, g_ref, b_ref, st_ref, do_ref, dqkv_ref, dg_ref, db_ref, ds_ref):
        @pl.when(pl.program_id(1) == 0)
        def _():
            ds_ref[...] = jnp.zeros_like(ds_ref)

        heads = [_dn_bwd_stages(q_ref[h], k_ref[h], v_ref[h], g_ref[h], b_ref[h], st_ref[h], do_ref[h], ds_ref[h])
                 for h in range(hb)]
        for h, (dq, dk, dv, dg, db, ds) in enumerate(_lockstep(heads)):
            dqkv_ref[0, h] = dq
            dqkv_ref[1, h] = dk
            dqkv_ref[2, h] = dv
            dg_ref[h] = dg
            db_ref[h] = db
            ds_ref[h] = ds

    def qspec(t):
        return pl.BlockSpec((None, hb, c, dh), lambda i, j, t=t: (t, i, n - 1 - j, 0))

    hs = pl.BlockSpec((hb, c, dh), lambda i, j: (i, n - 1 - j, 0))
    sh = jax.ShapeDtypeStruct((nh, s, dh), F32)
    return pl.pallas_call(
        body, name="dn_bwd", grid=(nh // hb, n),
        in_specs=[qspec(0), qspec(1), qspec(2), hs, hs,
                  pl.BlockSpec((hb, None, dh, dh), lambda i, j: (i, n - 1 - j, 0, 0)), hs],
        out_specs=[pl.BlockSpec((3, hb, c, dh), lambda i, j: (0, i, n - 1 - j, 0)), hs, hs],
        out_shape=[jax.ShapeDtypeStruct((3, nh, s, dh), F32), sh, sh],
        scratch_shapes=[pltpu.VMEM((hb, dh, dh), F32)],
        compiler_params=_params(("arbitrary", "arbitrary")),
    )(qkvn, qkvn, qkvn, gb, bb, states, do)


def _adamw(w, g, m, v, name):
    r, c = w.shape
    br = _div_tile(r, max(8, (1 << 18) // max(c, 1)), 8)

    def fn(w, g, m, v):
        m = ADAM_B1 * m + (1.0 - ADAM_B1) * g
        v = ADAM_B2 * v + (1.0 - ADAM_B2) * jnp.square(g)
        m_hat = m / (1.0 - ADAM_B1 ** ADAM_STEP)
        v_hat = v / (1.0 - ADAM_B2 ** ADAM_STEP)
        delta = -ADAM_LR * (m_hat / (jnp.sqrt(v_hat) + ADAM_EPS) + ADAM_WD * w)
        return delta, m, v

    spec = pl.BlockSpec((br, c), lambda i: (i, 0))
    return _ew(fn, name, (r // br,), [w, g, m, v], [spec] * 4, [((r, c), F32, spec)] * 3)


def _coords():
    return lax.axis_index("x"), lax.axis_index("y"), lax.axis_index("c")


def _all_gather_small(v):
    r, w = v.shape

    def body(v_ref, out_ref, send_sems, recv_sems, local_sem):
        x, y, c = _coords()
        me = 4 * x + 2 * y + c
        mine = pltpu.make_async_copy(v_ref, out_ref.at[me], local_sem)
        mine.start()
        peers = []
        for k in range(1, 8):
            px = 1 - x if k & 4 else x
            py = 1 - y if k & 2 else y
            pc = 1 - c if k & 1 else c
            peers.append((px, py, pc))
        sends = []
        for k, peer in enumerate(peers):
            cp = pltpu.make_async_remote_copy(src_ref=v_ref, dst_ref=out_ref.at[me], send_sem=send_sems.at[k],
                                              recv_sem=recv_sems.at[k], device_id=peer, device_id_type=MESH)
            cp.start()
            sends.append(cp)
        for k, (px, py, pc) in enumerate(peers):
            pltpu.make_async_remote_copy(src_ref=v_ref, dst_ref=out_ref.at[4 * px + 2 * py + pc],
                                         send_sem=send_sems.at[k], recv_sem=recv_sems.at[k],
                                         device_id=(px, py, pc), device_id_type=MESH).wait_recv()
        for cp in sends:
            cp.wait_send()
        mine.wait()

    return pl.pallas_call(
        body, name="ag_small", out_shape=jax.ShapeDtypeStruct((8, r, w), v.dtype),
        in_specs=[pl.BlockSpec(memory_space=pltpu.VMEM)], out_specs=pl.BlockSpec(memory_space=pltpu.VMEM),
        scratch_shapes=[pltpu.SemaphoreType.DMA((7,)), pltpu.SemaphoreType.DMA((7,)), pltpu.SemaphoreType.DMA],
        compiler_params=pltpu.CompilerParams(vmem_limit_bytes=VMEM_LIMIT),
    )(v)


def _hbm_call(body, name, arrays, out_shapes, n_sems, aliases=None):
    hbm = pl.BlockSpec(memory_space=pltpu.HBM)
    return pl.pallas_call(
        body, name=name, out_shape=list(out_shapes), in_specs=[hbm] * len(arrays), out_specs=[hbm] * len(out_shapes),
        input_output_aliases=aliases or {},
        scratch_shapes=[pltpu.SemaphoreType.DMA((n_sems,)), pltpu.SemaphoreType.DMA((n_sems,))],
    )(*arrays)


def _half_rows(ref_shape, c):
    h = ref_shape[1] // 2
    return pl.ds(pl.multiple_of(c * h, 16), h), pl.ds(pl.multiple_of((1 - c) * h, 16), h)


def _pair_copies(p_refs, land_refs, send_sems, recv_sems):
    x, y, c = _coords()
    cps = []
    for w, (p, land) in enumerate(zip(p_refs, land_refs)):
        _, other = _half_rows(p.shape, c)
        cps.append(pltpu.make_async_remote_copy(src_ref=p.at[:, other], dst_ref=land, send_sem=send_sems.at[w],
                                                recv_sem=recv_sems.at[w], device_id=(x, y, 1 - c),
                                                device_id_type=MESH))
    return cps


def _pair_gather(gs, name):
    n = len(gs)

    def body(*refs):
        ins, outs, send_sems, recv_sems = refs[:n], refs[n:2 * n], refs[2 * n], refs[2 * n + 1]
        x, y, c = _coords()
        cps = []
        for w in range(n):
            cp = pltpu.make_async_remote_copy(src_ref=ins[w].at[c], dst_ref=outs[w].at[c], send_sem=send_sems.at[w],
                                              recv_sem=recv_sems.at[w], device_id=(x, y, 1 - c), device_id_type=MESH)
            cp.start()
            cps.append(cp)
        for w, cp in enumerate(cps):
            pltpu.make_async_remote_copy(src_ref=ins[w].at[c], dst_ref=outs[w].at[1 - c], send_sem=send_sems.at[w],
                                         recv_sem=recv_sems.at[w], device_id=(x, y, 1 - c),
                                         device_id_type=MESH).wait_recv()
            cp.wait_send()

    shapes = [jax.ShapeDtypeStruct(g.shape, g.dtype) for g in gs]
    return _hbm_call(body, name, gs, shapes, n, aliases={i: i for i in range(n)})


_HBM = pl.BlockSpec(memory_space=pltpu.HBM)
_SEM = pl.BlockSpec(memory_space=pltpu.SEMAPHORE)
_EFFECT = pltpu.SideEffectType.DATAFLOW_SIDE_EFFECTING


def _split_start(name, srcs, lands, after, n_copies, make_copies):
    n, m = len(srcs), len(lands)
    arrays = [pltpu.with_memory_space_constraint(a, pltpu.HBM) for a in list(srcs) + list(lands)]

    def body(*refs):
        src_refs, land_refs = refs[:n], refs[n:n + m]
        send_sems, recv_sems = refs[n + m + 1], refs[n + m + 2]
        for cp in make_copies(src_refs, land_refs, send_sems, recv_sems):
            cp.start()
        refs[-1][...] = jnp.zeros_like(refs[-1])

    outs = pl.pallas_call(
        body, name=name,
        out_shape=(pltpu.SemaphoreType.DMA((n_copies,)), pltpu.SemaphoreType.DMA((n_copies,)),
                   *[pltpu.HBM(a.shape, a.dtype) for a in arrays], jax.ShapeDtypeStruct((8, LANES), F32)),
        in_specs=[_HBM] * (n + m) + [pl.BlockSpec(memory_space=pl.ANY)],
        out_specs=(_SEM, _SEM, *[_HBM] * (n + m), pl.BlockSpec(memory_space=pltpu.VMEM)),
        input_output_aliases={i: 2 + i for i in range(n + m)},
        compiler_params=pltpu.CompilerParams(has_side_effects=_EFFECT),
    )(*arrays, after)
    return (outs[0], outs[1], list(outs[2:2 + n]), list(outs[2 + n:2 + n + m])), outs[-1]


def _split_wait(name, state, after, make_copies):
    send_sems, recv_sems, srcs, lands = state
    n, m = len(srcs), len(lands)

    def body(*refs):
        src_refs, land_refs = refs[:n], refs[n:n + m]
        for cp in make_copies(src_refs, land_refs, refs[n + m], refs[n + m + 1]):
            cp.wait_send()
            cp.wait_recv()

    outs = pl.pallas_call(
        body, name=name, out_shape=tuple(pltpu.HBM(a.shape, a.dtype) for a in srcs + lands),
        in_specs=[_HBM] * (n + m) + [_SEM, _SEM, pl.BlockSpec(memory_space=pl.ANY)], out_specs=tuple([_HBM] * (n + m)),
        input_output_aliases={i: i for i in range(n + m)},
        compiler_params=pltpu.CompilerParams(has_side_effects=_EFFECT),
    )(*srcs, *lands, send_sems, recv_sems, after)
    return list(outs[:n]), list(outs[n:])


def _scatter_copies(q_refs, land_refs, send_sems, recv_sems):
    x, y, c = _coords()
    cps = []
    for w, (q, land) in enumerate(zip(q_refs, land_refs)):
        for k, (px, py) in enumerate([(1 - x, y), (x, 1 - y), (1 - x, 1 - y)]):
            cps.append(pltpu.make_async_remote_copy(src_ref=q.at[2 * px + py], dst_ref=land.at[k],
                                                    send_sem=send_sems.at[3 * w + k], recv_sem=recv_sems.at[3 * w + k],
                                                    device_id=(px, py, c), device_id_type=MESH))
    return cps


def _chip_scatter_start(qs, name):
    lands = [lax.empty((3,) + q.shape[1:], q.dtype) for q in qs]
    return _split_start(name, qs, lands, qs[0], 3 * len(qs), _scatter_copies)


def _chip_scatter_wait(state, after, name):
    return _split_wait(name, state, after, _scatter_copies)


def _gather_copies(src_refs, buf_refs, send_sems, recv_sems):
    x, y, c = _coords()
    j = 2 * x + y
    cps = []
    for w, buf in enumerate(buf_refs):
        mine, _ = _half_rows(buf.shape, c)
        for k, (px, py) in enumerate([(1 - x, y), (x, 1 - y), (1 - x, 1 - y)]):
            cps.append(pltpu.make_async_remote_copy(src_ref=buf.at[j, mine], dst_ref=buf.at[j, mine],
                                                    send_sem=send_sems.at[3 * w + k], recv_sem=recv_sems.at[3 * w + k],
                                                    device_id=(px, py, c), device_id_type=MESH))
    return cps


def _gather_wait_copies(src_refs, buf_refs, send_sems, recv_sems):
    x, y, c = _coords()
    j = 2 * x + y
    cps = []
    for w, buf in enumerate(buf_refs):
        mine, _ = _half_rows(buf.shape, c)
        for k, (px, py) in enumerate([(1 - x, y), (x, 1 - y), (1 - x, 1 - y)]):
            cps.append(pltpu.make_async_remote_copy(src_ref=buf.at[j, mine], dst_ref=buf.at[2 * px + py, mine],
                                                    send_sem=send_sems.at[3 * w + k], recv_sem=recv_sems.at[3 * w + k],
                                                    device_id=(px, py, c), device_id_type=MESH))
    return cps


def _pair_forward(bufs, name):
    n = len(bufs)

    def body(*refs):
        ins, outs, send_sems, recv_sems = refs[:n], refs[n:2 * n], refs[2 * n], refs[2 * n + 1]
        x, y, c = _coords()
        chips = [(1 - x, y), (x, 1 - y), (1 - x, 1 - y)]
        cps = []
        for w in range(n):
            mine, _ = _half_rows(outs[w].shape, c)
            for k, (px, py) in enumerate(chips):
                cp = pltpu.make_async_remote_copy(src_ref=ins[w].at[2 * px + py, mine],
                                                  dst_ref=outs[w].at[2 * px + py, mine],
                                                  send_sem=send_sems.at[3 * w + k], recv_sem=recv_sems.at[3 * w + k],
                                                  device_id=(x, y, 1 - c), device_id_type=MESH)
                cp.start()
                cps.append(cp)
        for w in range(n):
            _, sib = _half_rows(outs[w].shape, c)
            for k, (px, py) in enumerate(chips):
                pltpu.make_async_remote_copy(src_ref=ins[w].at[2 * px + py, sib], dst_ref=outs[w].at[2 * px + py, sib],
                                             send_sem=send_sems.at[3 * w + k], recv_sem=recv_sems.at[3 * w + k],
                                             device_id=(x, y, 1 - c), device_id_type=MESH).wait_recv()
        for cp in cps:
            cp.wait_send()

    shapes = [jax.ShapeDtypeStruct(b.shape, b.dtype) for b in bufs]
    return _hbm_call(body, name, bufs, shapes, 3 * n, aliases={i: i for i in range(n)})


def _slot_rows(h, cs):
    return _div_tile(h, max(16, (1 << 19) // cs), 16)


def _cast_into_slot(w, slot, name):
    r, cs = w.shape
    br = _slot_rows(r, cs)
    return _ew_slot(lambda a: a, name, (r // br,), slot, [w], [pl.BlockSpec((br, cs), lambda i, s: (i, 0))],
                    (4, r, cs), BF16, pl.BlockSpec((None, br, cs), lambda i, s: (s[0], i, 0)))


def _pair_add(p, rb, core, name):
    n, r, cs = p.shape
    h = r // 2
    br = _slot_rows(h, cs)
    nb = h // br
    return _ew_slot(lambda a, b: a + b, name, (n, nb), core, [p, rb],
                    [pl.BlockSpec((None, br, cs), lambda s, i, c: (s, c[0] * nb + i, 0)),
                     pl.BlockSpec((None, br, cs), lambda s, i, c: (s, i, 0))],
                    (n, h, cs), BF16, pl.BlockSpec((None, br, cs), lambda s, i, c: (s, i, 0)))


def _chip_sum(q, r3, slots, name):
    _, h, cs = q.shape
    br = _slot_rows(h, cs)

    def fn(a, b):
        acc = a.astype(F32)
        for k in range(3):
            acc = acc + b[k].astype(F32)
        return acc

    return _ew_slot(fn, name, (h // br,), slots, [q, r3],
                    [pl.BlockSpec((None, br, cs), lambda i, s: (s[0], i, 0)),
                     pl.BlockSpec((3, br, cs), lambda i, s: (0, i, 0))],
                    (2, h, cs), F32, pl.BlockSpec((None, br, cs), lambda i, s: (s[1], i, 0)))


def _sum_slots(r, name):
    n, h, w = r.shape
    br = _div_tile(h, 2048, 16)

    def fn(blk):
        acc = blk[0].astype(F32)
        for s in range(1, n):
            acc = acc + blk[s].astype(F32)
        return (acc,)

    return _ew(fn, name, (h // br,), [r], [pl.BlockSpec((n, br, w), lambda i: (0, i, 0))],
               [((h, w), F32, pl.BlockSpec((br, w), lambda i: (i, 0)))])[0]


def _pack_small(arrs):
    flat = jnp.concatenate([a.reshape(-1).astype(F32) for a in arrs])
    n = flat.shape[0]
    rows = -(-n // LANES)
    rows = -(-rows // 8) * 8
    return jnp.pad(flat, (0, rows * LANES - n)).reshape(rows, LANES)


def _unpack_small(p, shapes):
    lead = p.shape[:-2]
    flat = p.reshape(lead + (-1,))
    out, off = [], 0
    for sh in shapes:
        n = 1
        for d in sh:
            n *= d
        out.append(flat[..., off:off + n].reshape(lead + tuple(sh)))
        off += n
    return out


def kernel(x, c, w_ada, b_ada, norm1_g, w_in, dn_conv_w, dn_a_log, dn_dt_bias, dn_norm_g, dn_w_o, cf_conv_w, cf_ln_g, cf_ln_b, cf_w_o, w_out, norm2_g, ffn_w_up, ffn_conv_w, ffn_w_down, final_norm_g, loss_target, m_w_ada, m_b_ada, m_norm1_g, m_w_in, m_dn_conv_w, m_dn_a_log, m_dn_dt_bias, m_dn_norm_g, m_dn_w_o, m_cf_conv_w, m_cf_ln_g, m_cf_ln_b, m_cf_w_o, m_w_out, m_norm2_g, m_ffn_w_up, m_ffn_conv_w, m_ffn_w_down, m_final_norm_g, v_w_ada, v_b_ada, v_norm1_g, v_w_in, v_dn_conv_w, v_dn_a_log, v_dn_dt_bias, v_dn_norm_g, v_dn_w_o, v_cf_conv_w, v_cf_ln_g, v_cf_ln_b, v_cf_w_o, v_w_out, v_norm2_g, v_ffn_w_up, v_ffn_conv_w, v_ffn_w_down, v_final_norm_g):
    xi, yi, ci = _coords()
    chip = 2 * xi + yi
    me = 4 * xi + 2 * yi + ci
    core = jnp.reshape(ci, (1,)).astype(jnp.int32)

    S, D = x.shape[1], x.shape[2]
    NH = dn_a_log.shape[1]
    DH = dn_norm_g.shape[1]
    DNW = NH * DH
    CFW = cf_ln_g.shape[1]
    FFN = ffn_w_down.shape[1] * 4
    KDN, KCF, KFF = dn_conv_w.shape[1], cf_conv_w.shape[1], ffn_conv_w.shape[1]
    NIN = w_in.shape[2] * 4
    assert NIN == 4 * DNW + 2 * NH + 2 * CFW + 2 * D and DH == LANES and 2 * NH <= LANES
    x2, tgt = x[0], loss_target[0]

    sm_shapes = [(D,), (KDN, 3 * DNW // 4), (KCF, CFW // 4), (KFF, FFN // 4)]
    g1 = _all_gather_small(_pack_small([c[0], dn_conv_w[0], cf_conv_w[0], ffn_conv_w[0]]))
    c_all, dcw_s, ccw_s, fcw_s = _unpack_small(g1, sm_shapes)

    def chips_cols(t):
        t = t[0::2]
        return jnp.transpose(t, (1, 0, 2)).reshape(t.shape[1], -1)

    dn_cw, cf_cw, ff_cw = chips_cols(dcw_s), chips_cols(ccw_s), chips_cols(fcw_s)

    slot_chip = jnp.reshape(chip, (1,)).astype(jnp.int32)
    big = [w_in[0], dn_w_o[0], cf_w_o[0], w_out[0], ffn_w_up[0], ffn_w_down[0]]
    names_big = ["w_in", "dn_w_o", "cf_w_o", "w_out", "ffn_w_up", "ffn_w_down"]
    in_state, in_token = _split_start("ag_in_start", [], [_cast_into_slot(big[0], slot_chip, "cast_w_in")], c, 3,
                                      _gather_copies)
    bufs = [_cast_into_slot(w, slot_chip, "cast_" + nm) for w, nm in zip(big[1:], names_big[1:])]
    mid_state, mid_token = _split_start("ag_mid_start", [], bufs[:3], in_token, 9, _gather_copies)
    ag_state, ag_token = _split_start("ag_late_start", [], bufs[3:], mid_token, 6, _gather_copies)

    cb_cf = _div_tile(CFW, 256, LANES)
    o_b = 4 * DNW
    o_glu = o_b + 2 * NH
    o_ga = o_glu + 2 * CFW
    NA = NIN - 2 * NH
    a_z, a_ga, a_gb, a_glu, a_ba = 3 * DNW, 4 * DNW, 4 * DNW + D, 4 * DNW + 2 * D, NA

    segs = [(0, 0, o_b), (o_b, NA, 2 * NH), (o_ga, a_ga, 2 * D)]
    for t in range(CFW // cb_cf):
        segs += [(o_glu + t * cb_cf, a_glu + 2 * t * cb_cf, cb_cf),
                 (o_glu + CFW + t * cb_cf, a_glu + (2 * t + 1) * cb_cf, cb_cf)]
    CS = NIN // 4

    def shard_slices(lo, n):
        out = []
        while n > 0:
            j, off = lo // CS, lo % CS
            m = min(n, CS - off)
            out.append(w_in_g[j][:, off:off + m])
            lo, n = lo + m, n - m
        return out

    CA = w_ada.shape[2]
    b_sh = lax.dynamic_slice(b_ada, (0, chip * CA), (1, CA))
    tn_a = _div_tile(CA, 512, LANES)

    def mod_fn(cc, w, b):
        return (_mm(_silu(cc), w, NN) + b,)

    mod_sh = _ew(mod_fn, "ada_mod", (CA // tn_a,), [c_all, w_ada[0], b_sh],
                 [_full((8, D)), pl.BlockSpec((D, tn_a), lambda j: (0, j)), pl.BlockSpec((1, tn_a), lambda j: (0, j))],
                 [((8, CA), F32, pl.BlockSpec((8, tn_a), lambda j: (0, j)))])[0]
    g2 = _all_gather_small(_pack_small([mod_sh]))
    mod_all = _unpack_small(g2, [(8, CA)])[0][0::2]
    mod_all = jnp.transpose(mod_all, (1, 0, 2)).reshape(8, 4 * CA)
    mod_me = lax.dynamic_slice(mod_all, (me, 0), (1, 6 * D))
    sh1, sc1, gt1, sh2, sc2, gt2 = [mod_me[:, i * D:(i + 1) * D] for i in range(6)]

    bs = _div_tile(S, 128, 8)
    nb = S // bs
    vecD = _full((1, D))
    rowD = _row(bs, D)

    hn1 = _ew(lambda a, g, sc, sh: (_f_normmod(a, g, sc, sh),), "norm1_fwd", (nb,),
              [x2, norm1_g, sc1, sh1], [rowD, vecD, vecD, vecD], [((S, D), BF16, rowD)],
              after=[ag_token, m_w_in[0], v_w_in[0]])[0]
    _, in_landed = _split_wait("ag_in_wait", in_state, hn1, _gather_wait_copies)
    (w_in_g,) = _pair_forward(in_landed, "ag_in_pair")
    w_aug = jnp.concatenate([p for o, a, n in sorted(segs, key=lambda s: s[1]) for p in shard_slices(o, n)]
                            + [jnp.zeros((D, LANES - 2 * NH), BF16)], axis=1)
    proj = _matmul(hn1, w_aug, "nn", F32, "mm_in")

    def dn_post(j, cv):
        s = _silu(cv)
        nrm = s * lax.rsqrt(jnp.sum(s * s, axis=-1, keepdims=True) + EPS)
        fq = (j < NH).astype(F32)
        fk = (j < 2 * NH).astype(F32)
        scale = fq * (DH ** -0.5) + (1.0 - fq)
        return fk * (nrm * scale) + (1.0 - fk) * s

    def ident(a):
        return a

    def colS(w, off=0):
        return pl.BlockSpec((S, w), lambda j, off=off: (0, j + off))

    def wS(kw, w):
        return pl.BlockSpec((kw, w), lambda j: (0, j))

    qkv_spec = pl.BlockSpec((None, None, S, DH), lambda j: (j // NH, j % NH, 0, 0))
    dn_args = dict(kw=KDN, ncol=3 * NH, ins=[proj], in_specs=[colS(DH)], extras=[], extra_specs=[],
                   w=dn_cw, w_spec=wS(KDN, DH), pre=ident, post=dn_post)
    qkvn = _conv_fwd_call("dn_conv_fwd", out_shape=jax.ShapeDtypeStruct((3, NH, S, DH), F32), out_spec=qkv_spec,
                          **dn_args)

    alp = jnp.pad(dn_a_log, ((0, 0), (NH, LANES - 2 * NH)))
    dtb = jnp.pad(dn_dt_bias, ((0, 0), (NH, LANES - 2 * NH)))
    vecL = _full((1, LANES))
    ba_spec = _row(bs, LANES, a_ba // LANES)
    rowL = _row(bs, LANES)
    gate_fn = functools.partial(_f_dn_gate, NH)
    def gate_fwd(a, p, q):
        val = gate_fn(a, p, q)
        lane = lax.broadcasted_iota(jnp.int32, val.shape, 1)

        def spread(col):
            return jnp.sum(jnp.where(lane == col, val, 0.0), axis=-1, keepdims=True) + jnp.zeros(val.shape, F32)

        return tuple(spread(h) for h in range(NH)), tuple(spread(NH + h) for h in range(NH))

    hrow = pl.BlockSpec((NH, bs, DH), lambda i: (0, i, 0))
    bb_b, gb_b = _ew(gate_fwd, "dn_gate_fwd", (nb,), [proj, alp, dtb], [ba_spec, vecL, vecL],
                     [((NH, S, DH), F32, hrow), ((NH, S, DH), F32, hrow)])
    o_dn, states = _dn_fwd_call(qkvn, gb_b, bb_b)

    bsh = _div_tile(S, 512, 8)
    nbh = S // bsh
    o_spec = pl.BlockSpec((None, bsh, DH), lambda i, h: (h, i, 0))
    z_spec = pl.BlockSpec((bsh, DH), lambda i, h: (i, a_z // DH + h))
    oh_spec = pl.BlockSpec((bsh, DH), lambda i, h: (i, h))
    ng_spec = pl.BlockSpec((1, DH), lambda i, h: (0, 0))
    on = _ew(lambda o, z, g: (_f_dn_post(o, z, g),), "dn_post_fwd", (nbh, NH), [o_dn, proj, dn_norm_g],
             [o_spec, z_spec, ng_spec], [((S, DNW), BF16, oh_spec)])[0]
    _, landed = _split_wait("ag_mid_wait", mid_state, on, _gather_wait_copies)
    w_do_f, w_co_f, w_out_g = _pair_forward(landed, "ag_mid_pair")
    w_out_f = w_out_g.reshape(-1, w_out_g.shape[2])
    br_a = _matmul(on, w_do_f, "nn", F32, "mm_dn_o")

    def glu_pre(val, gl):
        return val * _sigmoid(gl)

    def glu_spec(t):
        return pl.BlockSpec((S, cb_cf), lambda j, t=t: (0, a_glu // cb_cf + 2 * j + t))

    cf_args = dict(kw=KCF, ncol=CFW // cb_cf, ins=[proj, proj], in_specs=[glu_spec(0), glu_spec(1)],
                   extras=[], extra_specs=[], w=cf_cw, w_spec=wS(KCF, cb_cf), pre=glu_pre, post=lambda j, cv: cv)
    uc = _conv_fwd_call("cf_conv_fwd", out_shape=jax.ShapeDtypeStruct((S, CFW), F32), out_spec=colS(cb_cf), **cf_args)
    rowC = _row(bs, CFW)
    vecC = _full((1, CFW))
    ub = _ew(lambda u, g, b: (_f_cf_ln(u, g, b),), "cf_ln_fwd", (nb,), [uc, cf_ln_g, cf_ln_b], [rowC, vecC, vecC],
             [((S, CFW), BF16, rowC)])[0]
    br_b = _matmul(ub, w_co_f, "nn", F32, "mm_cf_o")

    ga_spec, gb_spec = _row(bs, D, a_ga // D), _row(bs, D, a_gb // D)
    merged = _ew(lambda a, b, ga, gb: (_f_merge(a, b, ga, gb),), "merge_fwd", (nb,), [br_a, br_b, proj, proj],
                 [rowD, rowD, ga_spec, gb_spec], [((S, D), BF16, rowD)])[0]
    mix = _matmul(merged, w_out_f, "nn", F32, "mm_out")

    x1, hn2 = _ew(_f_res_normmod, "norm2_fwd", (nb,), [x2, mix, gt1, norm2_g, sc2, sh2],
                  [rowD, rowD, vecD, vecD, vecD, vecD], [((S, D), F32, rowD), ((S, D), BF16, rowD)])
    _, landed = _split_wait("ag_late_wait", ag_state, hn2, _gather_wait_copies)
    w_up_f, w_dn_g = _pair_forward(landed, "ag_late_pair")
    w_dn_f = w_dn_g.reshape(-1, w_dn_g.shape[2])
    up_all = _matmul(hn2, w_up_f, "nn", F32, "mm_up")

    cb_ff = _div_tile(FFN, 256, LANES)
    ff_args = dict(kw=KFF, ncol=FFN // cb_ff, ins=[up_all], in_specs=[colS(cb_ff)], extras=[up_all],
                   extra_specs=[colS(cb_ff, FFN // cb_ff)], w=ff_cw, w_spec=wS(KFF, cb_ff),
                   pre=ident, post=lambda j, cv, up: _silu(cv) * up)
    hff = _conv_fwd_call("ffn_conv_fwd", out_shape=jax.ShapeDtypeStruct((S, FFN), BF16), out_spec=colS(cb_ff), **ff_args)
    ffo = _matmul(hff, w_dn_f, "nn", F32, "mm_down")

    gf2 = final_norm_g.reshape(1, D)

    def loss_bwd(a, f, gt, gf, t):
        val, vjp = jax.vjp(_f_loss, a, f, gt, gf, t)
        da, df, dgt, dgf, _ = vjp(jnp.ones((), F32))
        return da, df, dgt, dgf, jnp.zeros((1, LANES), F32) + val

    dx1_l, dffo, dgt2, dgf, loss_v = _ew(
        loss_bwd, "loss_bwd", (nb,), [x1, ffo, gt2, gf2, tgt], [rowD, rowD, vecD, vecD, rowD],
        [((S, D), F32, rowD), ((S, D), BF16, rowD), ((1, D), F32, vecD), ((1, D), F32, vecD),
         ((1, LANES), F32, vecL)], acc=(2, 3, 4))

    dhff = _matmul(dffo, w_dn_f, "nt", F32, "mm_down_dx")
    g_w_dn = _matmul(hff, dffo, "tn", F32, "mm_down_dw")

    slots = jnp.stack([chip, ci]).astype(jnp.int32)
    rs_groups = []

    def rs_pair_begin(parts, tag):
        lands = [lax.empty((4, p.shape[1] // 2, p.shape[2]), F32) for p in parts]
        return _split_start("rs_pair_start_" + tag, parts, lands, parts[0], len(parts), _pair_copies)

    def rs_chips_begin(pair_state, nms, tag, after):
        parts, rbs = _split_wait("rs_pair_wait_" + tag, pair_state, after, _pair_copies)
        q16 = [_pair_add(p, rb, core, "rs_pair_add_" + nm) for p, rb, nm in zip(parts, rbs, nms)]
        state, token = _chip_scatter_start(q16, "rs_chips_start_" + tag)
        rs_groups.append((state, nms, tag))
        return token

    pair_a, ptok_a = rs_pair_begin([g_w_dn.reshape(4, FFN // 4, D)], "a")

    d_upall, g_ffcw = _conv_bwd_call(
        "ffn_conv_bwd", dout=dhff, dout_spec=colS(cb_ff),
        dio_shapes=[jax.ShapeDtypeStruct((2, S, FFN), BF16)],
        dio_specs=[pl.BlockSpec((2, S, cb_ff), lambda j: (0, 0, j))], dio_pack=lambda dg, du: [(dg, du)],
        dw_shape=jax.ShapeDtypeStruct((KFF, FFN), F32), dw_spec=wS(KFF, cb_ff), after=[ptok_a], **ff_args)
    tok_a = rs_chips_begin(pair_a, ["ffn_w_down"], "a", d_upall)
    dhn2 = _matmul(d_upall, w_up_f, "nt", F32, "mm_up_dx", after=[tok_a])
    g_w_up = _matmul(hn2, d_upall, "tn", F32, "mm_up_dw", out_groups=4)
    pair_b, ptok_b = rs_pair_begin([g_w_up], "b")

    def res2_bwd(a, mx, gt, g, sc, sh, dx1, dhn):
        _, vjp = jax.vjp(_f_res_normmod, a, mx, gt, g, sc, sh)
        return vjp((dx1, dhn))

    dx_r, dmix, dgt1, dg2, dsc2, dsh2 = _ew(
        res2_bwd, "norm2_bwd", (nb,), [x2, mix, gt1, norm2_g, sc2, sh2, dx1_l, dhn2],
        [rowD, rowD, vecD, vecD, vecD, vecD, rowD, rowD],
        [((S, D), F32, rowD), ((S, D), BF16, rowD)] + [((1, D), F32, vecD)] * 4, acc=(2, 3, 4, 5), after=[ptok_b])

    dmerged = _matmul(dmix, w_out_f, "nt", F32, "mm_out_dx")
    g_w_out = _matmul(merged, dmix, "tn", F32, "mm_out_dw")

    def merge_bwd(a, b, ga, gb, dm):
        _, vjp = jax.vjp(_f_merge, a, b, ga, gb)
        da, db, dga, dgb = vjp(dm)
        return da, db, jnp.concatenate([dga, dgb], axis=1)

    d_bra, d_brb, dproj = _ew(merge_bwd, "merge_bwd", (nb,), [br_a, br_b, proj, proj, dmerged],
                              [rowD, rowD, ga_spec, gb_spec, rowD],
                              [((S, D), BF16, rowD), ((S, D), BF16, rowD),
                               ((S, NA + LANES), BF16, _row(bs, 2 * D, a_ga // (2 * D)))])

    tok_b = rs_chips_begin(pair_b, ["ffn_w_up"], "b", d_brb)
    d_on = _matmul(d_bra, w_do_f, "nt", F32, "mm_dn_o_dx", after=[tok_b])
    g_w_do = _matmul(on, d_bra, "tn", F32, "mm_dn_o_dw", out_groups=4)
    d_ub = _matmul(d_brb, w_co_f, "nt", F32, "mm_cf_o_dx")
    g_w_co = _matmul(ub, d_brb, "tn", F32, "mm_cf_o_dw", out_groups=4)
    pair_c, ptok_c = rs_pair_begin([g_w_do, g_w_co, g_w_out.reshape(4, D // 4, D)], "c")

    def cf_ln_bwd(u, g, b, du):
        _, vjp = jax.vjp(_f_cf_ln, u, g, b)
        return vjp(du)

    d_uc, g_cflg, g_cflb = _ew(cf_ln_bwd, "cf_ln_bwd", (nb,), [uc, cf_ln_g, cf_ln_b, d_ub], [rowC, vecC, vecC, rowC],
                               [((S, CFW), F32, rowC), ((1, CFW), F32, vecC), ((1, CFW), F32, vecC)], acc=(1, 2),
                               after=[ptok_c])
    dproj_sds = jax.ShapeDtypeStruct((S, NA + LANES), BF16)
    dproj, g_cfcw = _conv_bwd_call(
        "cf_conv_bwd", dout=d_uc, dout_spec=colS(cb_cf), dio_shapes=[dproj_sds],
        dio_specs=[colS(2 * cb_cf, a_glu // (2 * cb_cf))], dio_pack=lambda dv, dg: [jnp.concatenate([dv, dg], axis=1)],
        dw_shape=jax.ShapeDtypeStruct((KCF, CFW), F32), dw_spec=wS(KCF, cb_cf), alias=dproj, **cf_args)

    def dn_post_bwd(o, z, g, d):
        _, vjp = jax.vjp(_f_dn_post, o, z, g)
        return vjp(d)

    tok_c = rs_chips_begin(pair_c, ["dn_w_o", "cf_w_o", "w_out"], "c", g_cfcw)
    d_o, dproj, g_dnng = _ew(dn_post_bwd, "dn_post_bwd", (nbh, NH), [o_dn, proj, dn_norm_g, d_on],
                             [o_spec, z_spec, ng_spec, oh_spec],
                             [((NH, S, DH), F32, o_spec), ((S, NA + LANES), BF16, z_spec), ((1, DH), F32, ng_spec)],
                             acc=(2,), alias=(dproj, 1), after=[tok_c])

    dqkvn, dgb_b, dbb_b = _dn_bwd_call(qkvn, gb_b, bb_b, states, d_o)
    dproj, g_dncw = _conv_bwd_call(
        "dn_conv_bwd", dout=dqkvn, dout_spec=qkv_spec, dio_shapes=[dproj_sds], dio_specs=[colS(DH)],
        dio_pack=lambda d: [d], dw_shape=jax.ShapeDtypeStruct((KDN, 3 * DNW), F32), dw_spec=wS(KDN, DH),
        alias=dproj, **dn_args)

    def gate_bwd(a, p, q, db, dg):
        lane = lax.broadcasted_iota(jnp.int32, a.shape, 1)
        d = jnp.zeros(a.shape, F32)
        for h in range(NH):
            d = d + jnp.where(lane == h, db[h], 0.0) + jnp.where(lane == NH + h, dg[h], 0.0)
        _, vjp = jax.vjp(gate_fn, a, p, q)
        return vjp(d)

    dproj, g_alp, g_dtb = _ew(gate_bwd, "dn_gate_bwd", (nb,), [proj, alp, dtb, dbb_b, dgb_b],
                              [ba_spec, vecL, vecL, hrow, hrow],
                              [((S, NA + LANES), BF16, ba_spec), ((1, LANES), F32, vecL), ((1, LANES), F32, vecL)],
                              acc=(1, 2), alias=(dproj, 0))

    g_w_aug = _matmul(hn1, dproj, "tn", F32, "mm_in_dw")
    def aug_slices(lo, n):
        out = []
        for o, a, m in sorted(segs):
            s, e = max(lo, o), min(lo + n, o + m)
            if s < e:
                out.append(g_w_aug[:, a + s - o:a + e - o])
        return out

    g_w_in = jnp.stack([jnp.concatenate(aug_slices(j * CS, CS), axis=1) for j in range(4)])
    pair_d, ptok_d = rs_pair_begin([g_w_in], "d")
    dhn1 = _matmul(dproj, w_aug, "nt", F32, "mm_in_dx", after=[ptok_d])
    tok_d = rs_chips_begin(pair_d, ["w_in"], "d", dhn1)

    def norm1_bwd(a, g, sc, sh, dhn, dxr):
        _, vjp = jax.vjp(_f_normmod, a, g, sc, sh)
        da, dg, dsc, dsh = vjp(dhn)
        return da + dxr, dg, dsc, dsh

    grad_x, dg1, dsc1, dsh1 = _ew(norm1_bwd, "norm1_bwd", (nb,), [x2, norm1_g, sc1, sh1, dhn1, dx_r],
                                  [rowD, vecD, vecD, vecD, rowD, rowD],
                                  [((S, D), F32, rowD)] + [((1, D), F32, vecD)] * 3, acc=(1, 2, 3), after=[tok_d])

    names = ["w_ada", "b_ada", "norm1_g", "w_in", "dn_conv_w", "dn_a_log", "dn_dt_bias", "dn_norm_g", "dn_w_o",
             "cf_conv_w", "cf_ln_g", "cf_ln_b", "cf_w_o", "w_out", "norm2_g", "ffn_w_up", "ffn_conv_w", "ffn_w_down",
             "final_norm_g"]
    wts = dict(zip(names, [w_ada, b_ada, norm1_g, w_in, dn_conv_w, dn_a_log, dn_dt_bias, dn_norm_g, dn_w_o, cf_conv_w,
                           cf_ln_g, cf_ln_b, cf_w_o, w_out, norm2_g, ffn_w_up, ffn_conv_w, ffn_w_down, final_norm_g]))
    ms = dict(zip(names, [m_w_ada, m_b_ada, m_norm1_g, m_w_in, m_dn_conv_w, m_dn_a_log, m_dn_dt_bias, m_dn_norm_g,
                          m_dn_w_o, m_cf_conv_w, m_cf_ln_g, m_cf_ln_b, m_cf_w_o, m_w_out, m_norm2_g, m_ffn_w_up,
                          m_ffn_conv_w, m_ffn_w_down, m_final_norm_g]))
    vs = dict(zip(names, [v_w_ada, v_b_ada, v_norm1_g, v_w_in, v_dn_conv_w, v_dn_a_log, v_dn_dt_bias, v_dn_norm_g,
                          v_dn_w_o, v_cf_conv_w, v_cf_ln_g, v_cf_ln_b, v_cf_w_o, v_w_out, v_norm2_g, v_ffn_w_up,
                          v_ffn_conv_w, v_ffn_w_down, v_final_norm_g]))
    grads, delta, new_m, new_v = {}, {}, {}, {}

    def adam_large(n, g):
        grads[n] = g.reshape(wts[n].shape)
        d_, m_, v_ = _adamw(wts[n][0], grads[n][0], ms[n][0], vs[n][0], "adamw_" + n)
        delta[n], new_m[n], new_v[n] = d_[None], m_[None], v_[None]

    def rs_finish(groups, after, tag):
        hs, nms_all = [], []
        for state, nms, t in groups:
            q16, r3s = _chip_scatter_wait(state, after, "rs_chips_wait_" + t)
            for q, r3, nm in zip(q16, r3s, nms):
                hs.append(_chip_sum(q, r3, slots, "rs_chip_sum_" + nm))
                nms_all.append(nm)
        for nm, g in zip(nms_all, _pair_gather(hs, "rs_pair_gather_" + tag)):
            adam_large(nm, g)

    rs_finish(rs_groups[:3], grad_x, "abc")
    rs_finish(rs_groups[3:], delta["ffn_w_up"], "d")

    dmod = jnp.concatenate([dsh1, dsc1, dgt1, dsh2, dsc2, dgt2], axis=1)
    sm2 = [dmod, dg1, dg2, dgf, g_alp, g_dtb, g_dnng, g_cflg, g_cflb, g_dncw, g_cfcw, g_ffcw]
    sm2_shapes = [tuple(a.shape) for a in sm2]
    g3 = _all_gather_small(_pack_small(sm2))
    ssum = _sum_slots(g3, "small_sum")
    dmod_all = _unpack_small(g3, sm2_shapes[:1])[0].reshape(8, 6 * D)
    (g_b_ada, gs_n1, gs_n2, gs_fn, gs_alp, gs_dtb, gs_dnng, gs_cflg, gs_cflb, gs_dncw, gs_cfcw,
     gs_ffcw) = _unpack_small(ssum, sm2_shapes)
    gs_alog, gs_dtbias = gs_alp[:, NH:2 * NH], gs_dtb[:, NH:2 * NH]
    gs_dncw = lax.dynamic_slice(gs_dncw, (0, chip * (3 * DNW // 4)), (KDN, 3 * DNW // 4))
    gs_cfcw = lax.dynamic_slice(gs_cfcw, (0, chip * (CFW // 4)), (KCF, CFW // 4))
    gs_ffcw = lax.dynamic_slice(gs_ffcw, (0, chip * (FFN // 4)), (KFF, FFN // 4))

    dmod_sh = lax.dynamic_slice(dmod_all, (0, chip * CA), (8, CA))

    def wada_fn(cc, dm):
        return (_mm(_silu(cc), dm, TN),)

    g_w_ada = _ew(wada_fn, "ada_dw", (CA // tn_a,), [c_all, dmod_sh],
                  [_full((8, D)), pl.BlockSpec((8, tn_a), lambda j: (0, j))],
                  [((D, CA), F32, pl.BlockSpec((D, tn_a), lambda j: (0, j)))])[0]

    loss = lax.psum(loss_v[0, 0], ("x", "y", "c"))

    adam_large("w_ada", g_w_ada)
    small_grads = {"b_ada": g_b_ada, "norm1_g": gs_n1, "dn_conv_w": gs_dncw, "dn_a_log": gs_alog,
                   "dn_dt_bias": gs_dtbias, "dn_norm_g": gs_dnng, "cf_conv_w": gs_cfcw, "cf_ln_g": gs_cflg,
                   "cf_ln_b": gs_cflb, "norm2_g": gs_n2, "ffn_conv_w": gs_ffcw, "final_norm_g": gs_fn}
    small = [n for n in names if n in small_grads]
    for n in small:
        grads[n] = small_grads[n].reshape(wts[n].shape)
    sm_sh = [tuple(wts[n].shape) for n in small]
    d_, m_, v_ = _adamw(_pack_small([wts[n] for n in small]), _pack_small([grads[n] for n in small]),
                        _pack_small([ms[n] for n in small]), _pack_small([vs[n] for n in small]), "adamw_small")
    for n, a, b_, c_ in zip(small, _unpack_small(d_, sm_sh), _unpack_small(m_, sm_sh), _unpack_small(v_, sm_sh)):
        delta[n], new_m[n], new_v[n] = a, b_, c_

    return (loss, grad_x[None], *[grads[n] for n in names], *[delta[n] for n in names],
            *[new_m[n] for n in names], *[new_v[n] for n in names])
```

```python
import functools

import jax
import jax.numpy as jnp
from jax import lax
from jax.experimental import pallas as pl
from jax.experimental.pallas import tpu as pltpu

F32 = jnp.float32
BF16 = jnp.bfloat16
EPS = 1e-6
LANES = 128
VMEM_LIMIT = 48 * 1024 * 1024
DN_CHUNK = 128
ADAM_LR, ADAM_B1, ADAM_B2, ADAM_EPS, ADAM_WD, ADAM_STEP = 0.001, 0.9, 0.999, 1e-08, 0.01, 10
MESH = pl.DeviceIdType.MESH

NN = (((1,), (0,)), ((), ()))
NT = (((1,), (1,)), ((), ()))
TN = (((0,), (0,)), ((), ()))
_DIMS = {"nn": NN, "nt": NT, "tn": TN}


def _mm(a, b, dims):
    return lax.dot_general(a.astype(BF16), b.astype(BF16), dims, preferred_element_type=F32)


def _mmx(a, b, dims):
    return lax.dot_general(a, b, dims, precision=lax.Precision.HIGH, preferred_element_type=F32)


def _div_tile(n, target, mult):
    best = None
    t = mult
    while t <= min(n, target):
        if n % t == 0:
            best = t
        t += mult
    return best if best is not None else n


def _params(sem=None):
    kw = dict(vmem_limit_bytes=VMEM_LIMIT)
    if sem is not None:
        kw["dimension_semantics"] = sem
    return pltpu.CompilerParams(**kw)


def _sigmoid(x):
    return jax.nn.sigmoid(x)


def _silu(x):
    return x * jax.nn.sigmoid(x)


def _softplus(x):
    return jnp.maximum(x, 0.0) + jnp.log(1.0 + jnp.exp(-jnp.abs(x)))


def _view(arr):
    if arr.ndim == 3:
        return arr.shape[1], arr.shape[0] * arr.shape[2], arr.shape[0]
    return arr.shape[0], arr.shape[1], 1


def _tile_spec(groups, cols, tr, tc, rsel, csel):
    if groups > 1:
        per = cols // groups // tc
        return pl.BlockSpec((None, tr, tc), lambda i, j, k: (csel(i, j, k) // per, rsel(i, j, k), csel(i, j, k) % per))
    return pl.BlockSpec((tr, tc), lambda i, j, k: (rsel(i, j, k), csel(i, j, k)))


def _matmul(a, b, mode, out_dtype, name, out_groups=1, after=()):
    ar, ac, ag = _view(a)
    br, bc, bg = _view(b)
    if mode == "nn":
        M, K, N = ar, ac, bc
        kdiv, mdiv, ndiv = ac // ag, M, min(bc // bg, N // out_groups)
    elif mode == "nt":
        M, K, N = ar, ac, br
        kdiv, mdiv, ndiv = min(ac // ag, bc // bg), M, N // out_groups
    else:
        K, M, N = ar, ac, bc
        kdiv, mdiv, ndiv = K, ac // ag, min(bc // bg, N // out_groups)
    tm = _div_tile(mdiv, 1024, LANES)
    tn = _div_tile(ndiv, 1536, LANES)
    tk = _div_tile(kdiv, 2048, LANES)
    nk = K // tk
    dims = _DIMS[mode]
    si, sj, sk = (lambda i, j, k: i), (lambda i, j, k: j), (lambda i, j, k: k)
    a_spec = {"nn": _tile_spec(ag, ac, tm, tk, si, sk), "nt": _tile_spec(ag, ac, tm, tk, si, sk),
              "tn": _tile_spec(ag, ac, tk, tm, sk, si)}[mode]
    b_spec = {"nn": _tile_spec(bg, bc, tk, tn, sk, sj), "nt": _tile_spec(bg, bc, tn, tk, sj, sk),
              "tn": _tile_spec(bg, bc, tk, tn, sk, sj)}[mode]
    out_shape = (M, N) if out_groups == 1 else (out_groups, M, N // out_groups)

    n_after = len(after)

    def body(*refs):
        a_ref, b_ref, o_ref = refs[0], refs[1], refs[2 + n_after]
        if nk == 1:
            o_ref[...] = lax.dot_general(a_ref[...], b_ref[...], dims, preferred_element_type=F32).astype(o_ref.dtype)
            return
        acc_ref = refs[3 + n_after]
        k = pl.program_id(2)

        @pl.when(k == 0)
        def _():
            acc_ref[...] = jnp.zeros_like(acc_ref)

        acc_ref[...] += lax.dot_general(a_ref[...], b_ref[...], dims, preferred_element_type=F32)

        @pl.when(k == nk - 1)
        def _():
            o_ref[...] = acc_ref[...].astype(o_ref.dtype)

    return pl.pallas_call(
        body, name=name, grid=(M // tm, N // tn, nk),
        in_specs=[a_spec, b_spec] + [pl.BlockSpec(memory_space=pl.ANY)] * n_after,
        out_specs=_tile_spec(out_groups, N, tm, tn, si, sj),
        out_shape=jax.ShapeDtypeStruct(out_shape, out_dtype),
        scratch_shapes=[pltpu.VMEM((tm, tn), F32)] if nk > 1 else [],
        compiler_params=_params(("parallel", "parallel", "arbitrary")),
    )(a, b, *after)


def _ew(fn, name, grid, ins, in_specs, outs, acc=(), alias=None, after=()):
    n_in = len(ins)
    n_ax = len(grid)
    extra, aliases = list(after), {}
    if alias is not None:
        extra, aliases = extra + [alias[0]], {n_in + len(after): alias[1]}
    extra_specs = [pl.BlockSpec(memory_space=pl.ANY)] * len(extra)

    def body(*refs):
        in_refs, out_refs = refs[:n_in], refs[n_in + len(extra):]
        ids = [pl.program_id(a) for a in range(n_ax)]
        res = fn(*[r[...] for r in in_refs])
        first = ids[0] == 0
        for t in ids[1:]:
            first = jnp.logical_and(first, t == 0)
        for idx, (r, val) in enumerate(zip(out_refs, res)):
            if idx in acc:
                @pl.when(first)
                def _(r=r, val=val):
                    r[...] = val.astype(r.dtype)

                @pl.when(jnp.logical_not(first))
                def _(r=r, val=val):
                    r[...] += val.astype(r.dtype)
            elif isinstance(val, tuple):
                for t, part in enumerate(val):
                    r[t] = part.astype(r.dtype)
            else:
                r[...] = val.astype(r.dtype)

    return pl.pallas_call(
        body, name=name, grid=grid, in_specs=list(in_specs) + extra_specs,
        out_specs=[o[2] for o in outs],
        out_shape=[jax.ShapeDtypeStruct(o[0], o[1]) for o in outs],
        input_output_aliases=aliases,
        compiler_params=_params(("arbitrary",) * n_ax),
    )(*ins, *extra)


def _ew_slot(fn, name, grid, slots, ins, in_specs, out_shape, out_dtype, out_spec):
    def body(s_ref, *refs):
        refs[-1][...] = fn(*[r[...] for r in refs[:-1]]).astype(refs[-1].dtype)

    return pl.pallas_call(
        body, name=name,
        grid_spec=pltpu.PrefetchScalarGridSpec(num_scalar_prefetch=1, grid=grid, in_specs=list(in_specs),
                                               out_specs=out_spec),
        out_shape=jax.ShapeDtypeStruct(out_shape, out_dtype),
        compiler_params=_params(("arbitrary",) * len(grid)),
    )(slots, *ins)


def _row(bs, w, col=0):
    return pl.BlockSpec((bs, w), lambda i, col=col: (i, col))


def _full(shape):
    nd = len(shape)
    return pl.BlockSpec(tuple(shape), lambda *_: (0,) * nd)


def _rms(x, g):
    return x * lax.rsqrt(jnp.mean(x * x, axis=-1, keepdims=True) + EPS) * g


def _f_normmod(x, g, sc, sh):
    return _rms(x, g) * (1.0 + sc) + sh


def _f_res_normmod(x, mix, gt, g, sc, sh):
    x1 = x + gt * mix
    return x1, _f_normmod(x1, g, sc, sh)


def _f_loss(x1, f, gt, gf, tgt):
    y = _rms(x1 + gt * f, gf)
    return 0.5 * jnp.sum(jnp.mean(jnp.square(y - tgt), axis=-1))


def _f_dn_gate(nh, ba, alp, dtb):
    lane = lax.broadcasted_iota(jnp.int32, ba.shape, 1)
    m = (lane < nh).astype(F32)
    beta = _sigmoid(ba)
    g = -jnp.exp(alp) * _softplus(ba + dtb)
    return m * beta + (1.0 - m) * g


def _f_dn_post(o, z, g):
    return o * lax.rsqrt(jnp.mean(o * o, axis=-1, keepdims=True) + EPS) * g * _silu(z)


def _f_cf_ln(u, g, b):
    mu = jnp.mean(u, axis=-1, keepdims=True)
    xc = u - mu
    y = xc * lax.rsqrt(jnp.mean(xc * xc, axis=-1, keepdims=True) + EPS)
    return _silu(y * g + b)


def _f_merge(a, b, ga, gb):
    return _sigmoid(ga) * a + _sigmoid(gb) * b


def _shift_down(u, d, rows):
    if d == 0:
        return u
    return jnp.where(rows >= d, pltpu.roll(u, d, 0), 0.0)


def _shift_up(u, d, rows):
    if d == 0:
        return u
    s = u.shape[0]
    return jnp.where(rows < s - d, pltpu.roll(u, s - d, 0), 0.0)


def _conv(u, w_ref, kw, rows):
    acc = None
    for k in range(kw):
        t = w_ref[k:k + 1, :] * _shift_down(u, kw - 1 - k, rows)
        acc = t if acc is None else acc + t
    return acc


def _conv_t(dc, w_ref, kw, rows):
    acc = None
    for k in range(kw):
        t = w_ref[k:k + 1, :] * _shift_up(dc, kw - 1 - k, rows)
        acc = t if acc is None else acc + t
    return acc


def _conv_fwd_call(name, kw, ncol, ins, in_specs, extras, extra_specs, w, w_spec, pre, post, out_shape, out_spec):
    n_in, n_ex = len(ins), len(extras)

    def body(*refs):
        in_refs, ex_refs = refs[:n_in], refs[n_in:n_in + n_ex]
        w_ref, out_ref = refs[n_in + n_ex], refs[n_in + n_ex + 1]
        j = pl.program_id(0)
        u = pre(*[r[...] for r in in_refs])
        rows = lax.broadcasted_iota(jnp.int32, u.shape, 0)
        cv = _conv(u, w_ref, kw, rows)
        out_ref[...] = post(j, cv, *[r[...] for r in ex_refs]).astype(out_ref.dtype)

    return pl.pallas_call(
        body, name=name, grid=(ncol,), in_specs=list(in_specs) + list(extra_specs) + [w_spec],
        out_specs=out_spec, out_shape=out_shape, compiler_params=_params(("arbitrary",)),
    )(*ins, *extras, w)


def _conv_bwd_call(name, kw, ncol, ins, in_specs, extras, extra_specs, w, w_spec, pre, post, dout, dout_spec,
                   dio_shapes, dio_specs, dio_pack, dw_shape, dw_spec, alias=None, after=()):
    n_in, n_ex, n_io = len(ins), len(extras), len(dio_shapes)
    al, aliases = list(after), {}
    if alias is not None:
        al, aliases = al + [alias], {n_in + n_ex + 2 + len(after): 0}
    al_specs = [pl.BlockSpec(memory_space=pl.ANY)] * len(al)

    def body(*refs):
        in_refs, ex_refs = refs[:n_in], refs[n_in:n_in + n_ex]
        w_ref, dout_ref = refs[n_in + n_ex], refs[n_in + n_ex + 1]
        outs = refs[n_in + n_ex + 2 + len(al):]
        dio_refs, dw_ref = outs[:n_io], outs[n_io]
        j = pl.program_id(0)
        u, pre_vjp = jax.vjp(pre, *[r[...] for r in in_refs])
        rows = lax.broadcasted_iota(jnp.int32, u.shape, 0)
        cv = _conv(u, w_ref, kw, rows)
        _, post_vjp = jax.vjp(lambda cc, *ex: post(j, cc, *ex), cv, *[r[...] for r in ex_refs])
        g = post_vjp(dout_ref[...].astype(F32))
        dc = g[0]
        for k in range(kw):
            dw_ref[k:k + 1, :] = jnp.sum(dc * _shift_down(u, kw - 1 - k, rows), axis=0, keepdims=True)
        du = _conv_t(dc, w_ref, kw, rows)
        for r, val in zip(dio_refs, dio_pack(*pre_vjp(du), *g[1:])):
            if isinstance(val, tuple):
                for t, part in enumerate(val):
                    r[t] = part.astype(r.dtype)
            else:
                r[...] = val.astype(r.dtype)

    return pl.pallas_call(
        body, name=name, grid=(ncol,),
        in_specs=list(in_specs) + list(extra_specs) + [w_spec, dout_spec] + al_specs,
        out_specs=list(dio_specs) + [dw_spec],
        out_shape=list(dio_shapes) + [dw_shape],
        input_output_aliases=aliases,
        compiler_params=_params(("arbitrary",)),
    )(*ins, *extras, w, dout, *al)


def _tri_inverse(a):
    c = a.shape[0]
    ii = lax.broadcasted_iota(jnp.int32, (c, c), 0)
    jj = lax.broadcasted_iota(jnp.int32, (c, c), 1)
    eye = (ii == jj).astype(F32)

    def same_block(bits):
        return jnp.right_shift(ii, bits) == jnp.right_shift(jj, bits)

    d = jnp.where(same_block(3), a, 0.0)
    d2 = _mmx(d, d, NN)
    yield
    t = (eye - d) + _mmx(eye - d, d2, NN)
    d4 = _mmx(d2, d2, NN)
    yield
    t = t + _mmx(t, d4, NN)
    yield
    bits = 3
    while (1 << bits) < c:
        low = jnp.where(jnp.logical_and(same_block(bits + 1), jnp.logical_not(same_block(bits))), a, 0.0)
        tl = _mmx(t, low, NN)
        yield
        t = t - _mmx(tl, t, NN)
        yield
        bits += 1
    return t


def _lockstep(gens):
    out = [None] * len(gens)
    live = list(range(len(gens)))
    while live:
        nxt = []
        for i in live:
            try:
                next(gens[i])
                nxt.append(i)
            except StopIteration as e:
                out[i] = e.value
        live = nxt
    return out


def _dn_common(q, k, v, gb, bb):
    c = q.shape[0]
    ii = lax.broadcasted_iota(jnp.int32, (c, c), 0)
    jj = lax.broadcasted_iota(jnp.int32, (c, c), 1)
    causal = jj <= ii
    strict = jj < ii
    low = causal.astype(F32)
    ones = jnp.ones((c, LANES), F32)
    gc = _mmx(low, gb, NN)
    kb = k * bb
    vb = v * bb
    kk = _mm(kb, k, NT)
    qk = _mm(q, k, NT)
    yield
    diff = (_mmx(gc, ones, NT) - _mmx(ones, gc, NT)) * (1.0 / LANES)
    gl = jnp.sum(gb, axis=0, keepdims=True)
    eg = jnp.exp(gc)
    egm = jnp.exp(gl - gc)
    egl = jnp.exp(gl)
    kbg = kb * eg
    yield
    decay = jnp.where(causal, jnp.exp(jnp.where(causal, diff, 0.0)), 0.0)
    t = yield from _tri_inverse(jnp.where(strict, kk * decay, 0.0))
    attn = qk * decay
    return dict(causal=causal, strict=strict, low=low, ones=ones, decay=decay, eg=eg, egm=egm, egl=egl,
                kb=kb, vb=vb, kbg=kbg, kk=kk, t=t, qk=qk, attn=attn, qd=q * eg, kd=k * egm)


def _dn_fwd_stages(q, k, v, gb, bb, s):
    m = yield from _dn_common(q, k, v, gb, bb)
    r = m["vb"] - _mm(m["kbg"], s, NN)
    qs = _mm(m["qd"], s, NN)
    yield
    u = _mmx(m["t"], r, NN)
    yield
    o = qs + _mm(m["attn"], u, NN)
    s2 = s * m["egl"] + _mm(m["kd"], u, TN)
    return o, s2


def _dn_bwd_stages(q, k, v, gb, bb, s, do, dsp):
    m = yield from _dn_common(q, k, v, gb, bb)
    c = q.shape[0]
    t, decay, eg, egm, egl = m["t"], m["decay"], m["eg"], m["egm"], m["egl"]
    r = m["vb"] - _mm(m["kbg"], s, NN)
    du = _mm(m["attn"], do, TN) + _mm(m["kd"], dsp, NN)
    dqd = _mm(do, s, NT)
    ds = dsp * egl + _mm(m["qd"], do, TN)
    degl = jnp.sum(jnp.sum(dsp * s, axis=1, keepdims=True), axis=0, keepdims=True)
    yield
    u = _mmx(t, r, NN)
    dr = _mmx(t, du, TN)
    yield
    dattn = jnp.where(m["causal"], _mm(do, u, NT), 0.0)
    dkd = _mm(u, dsp, NT)
    da = jnp.where(m["strict"], -_mm(dr, u, NT), 0.0)
    dkbg = -_mm(dr, s, NT)
    ds = ds - _mm(m["kbg"], dr, TN)
    yield
    dkk = da * decay
    dqk = dattn * decay
    ddiff = (da * m["kk"] + dattn * m["qk"]) * decay
    dgc = _mmx(ddiff, m["ones"], NN) - _mmx(ddiff, m["ones"], TN)
    dkb = _mm(dkk, k, NN) + dkbg * eg
    dk = _mm(dkk, m["kb"], TN) + _mm(dqk, q, TN) + dkd * egm + dkb * bb
    dq = _mm(dqk, k, NN) + dqd * eg
    yield
    dgc = dgc + jnp.sum(dqd * q + dkbg * m["kb"], axis=-1, keepdims=True) * eg
    tt = jnp.sum(dkd * k, axis=-1, keepdims=True) * egm
    dgc = dgc - tt
    dgl = jnp.sum(tt, axis=0, keepdims=True) + degl * egl
    dbb = jnp.sum(dkb * k + dr * v, axis=-1, keepdims=True) + jnp.zeros((c, LANES), F32)
    dv = dr * bb
    dgb = _mmx(m["low"], dgc, TN) + dgl
    return dq, dk, dv, dgb, dbb, ds


def _dn_fwd_chunk(q, k, v, gb, bb, s):
    return _lockstep([_dn_fwd_stages(q, k, v, gb, bb, s)])[0]


def _dn_bwd_chunk(q, k, v, gb, bb, s, do, dsp):
    return _lockstep([_dn_bwd_stages(q, k, v, gb, bb, s, do, dsp)])[0]


def _dn_fwd_call(qkvn, gb, bb):
    _, nh, s, dh = qkvn.shape
    c = min(DN_CHUNK, s)
    n = s // c
    hb = nh

    def body(q_ref, k_ref, v_ref, g_ref, b_ref, o_ref, st_ref, s_ref):
        @pl.when(pl.program_id(1) == 0)
        def _():
            s_ref[...] = jnp.zeros_like(s_ref)

        st_ref[...] = s_ref[...]
        heads = [_dn_fwd_stages(q_ref[h], k_ref[h], v_ref[h], g_ref[h], b_ref[h], s_ref[h]) for h in range(hb)]
        for h, (o, s2) in enumerate(_lockstep(heads)):
            o_ref[h] = o
            s_ref[h] = s2

    def qspec(t):
        return pl.BlockSpec((None, hb, c, dh), lambda i, j, t=t: (t, i, j, 0))

    hs = pl.BlockSpec((hb, c, dh), lambda i, j: (i, j, 0))
    return pl.pallas_call(
        body, name="dn_fwd", grid=(nh // hb, n),
        in_specs=[qspec(0), qspec(1), qspec(2), hs, hs],
        out_specs=[hs, pl.BlockSpec((hb, None, dh, dh), lambda i, j: (i, j, 0, 0))],
        out_shape=[jax.ShapeDtypeStruct((nh, s, dh), F32), jax.ShapeDtypeStruct((nh, n, dh, dh), F32)],
        scratch_shapes=[pltpu.VMEM((hb, dh, dh), F32)],
        compiler_params=_params(("arbitrary", "arbitrary")),
    )(qkvn, qkvn, qkvn, gb, bb)


def _dn_bwd_call(qkvn, gb, bb, states, do):
    _, nh, s, dh = qkvn.shape
    c = min(DN_CHUNK, s)
    n = s // c
    hb = nh

    def body(q_ref, k_ref, v_ref, g_ref, b_ref, st_ref, do_ref, dqkv_ref, dg_ref, db_ref, ds_ref):
        @pl.when(pl.program_id(1) == 0)
        def _():
            ds_ref[...] = jnp.zeros_like(ds_ref)

        heads = [_dn_bwd_stages(q_ref[h], k_ref[h], v_ref[h], g_ref[h], b_ref[h], st_ref[h], do_ref[h], ds_ref[h])
                 for h in range(hb)]
        for h, (dq, dk, dv, dg, db, ds) in enumerate(_lockstep(heads)):
            dqkv_ref[0, h] = dq
            dqkv_ref[1, h] = dk
            dqkv_ref[2, h] = dv
            dg_ref[h] = dg
            db_ref[h] = db
            ds_ref[h] = ds

    def qspec(t):
        return pl.BlockSpec((None, hb, c, dh), lambda i, j, t=t: (t, i, n - 1 - j, 0))

    hs = pl.BlockSpec((hb, c, dh), lambda i, j: (i, n - 1 - j, 0))
    sh = jax.ShapeDtypeStruct((nh, s, dh), F32)
    return pl.pallas_call(
        body, name="dn_bwd", grid=(nh // hb, n),
        in_specs=[qspec(0), qspec(1), qspec(2), hs, hs,
                  pl.BlockSpec((hb, None, dh, dh), lambda i, j: (i, n - 1 - j, 0, 0)), hs],
        out_specs=[pl.BlockSpec((3, hb, c, dh), lambda i, j: (0, i, n - 1 - j, 0)), hs, hs],
        out_shape=[jax.ShapeDtypeStruct((3, nh, s, dh), F32), sh, sh],
        scratch_shapes=[pltpu.VMEM((hb, dh, dh), F32)],
        compiler_params=_params(("arbitrary", "arbitrary")),
    )(qkvn, qkvn, qkvn, gb, bb, states, do)


def _adamw(w, g, m, v, name):
    r, c = w.shape
    br = _div_tile(r, max(8, (1 << 18) // max(c, 1)), 8)

    def fn(w, g, m, v):
        m = ADAM_B1 * m + (1.0 - ADAM_B1) * g
        v = ADAM_B2 * v + (1.0 - ADAM_B2) * jnp.square(g)
        m_hat = m / (1.0 - ADAM_B1 ** ADAM_STEP)
        v_hat = v / (1.0 - ADAM_B2 ** ADAM_STEP)
        delta = -ADAM_LR * (m_hat / (jnp.sqrt(v_hat) + ADAM_EPS) + ADAM_WD * w)
        return delta, m, v

    spec = pl.BlockSpec((br, c), lambda i: (i, 0))
    return _ew(fn, name, (r // br,), [w, g, m, v], [spec] * 4, [((r, c), F32, spec)] * 3)


def _coords():
    return lax.axis_index("x"), lax.axis_index("y"), lax.axis_index("c")


def _all_gather_small(v):
    r, w = v.shape

    def body(v_ref, out_ref, send_sems, recv_sems, local_sem):
        x, y, c = _coords()
        me = 4 * x + 2 * y + c
        mine = pltpu.make_async_copy(v_ref, out_ref.at[me], local_sem)
        mine.start()
        peers = []
        for k in range(1, 8):
            px = 1 - x if k & 4 else x
            py = 1 - y if k & 2 else y
            pc = 1 - c if k & 1 else c
            peers.append((px, py, pc))
        sends = []
        for k, peer in enumerate(peers):
            cp = pltpu.make_async_remote_copy(src_ref=v_ref, dst_ref=out_ref.at[me], send_sem=send_sems.at[k],
                                              recv_sem=recv_sems.at[k], device_id=peer, device_id_type=MESH)
            cp.start()
            sends.append(cp)
        for k, (px, py, pc) in enumerate(peers):
            pltpu.make_async_remote_copy(src_ref=v_ref, dst_ref=out_ref.at[4 * px + 2 * py + pc],
                                         send_sem=send_sems.at[k], recv_sem=recv_sems.at[k],
                                         device_id=(px, py, pc), device_id_type=MESH).wait_recv()
        for cp in sends:
            cp.wait_send()
        mine.wait()

    return pl.pallas_call(
        body, name="ag_small", out_shape=jax.ShapeDtypeStruct((8, r, w), v.dtype),
        in_specs=[pl.BlockSpec(memory_space=pltpu.VMEM)], out_specs=pl.BlockSpec(memory_space=pltpu.VMEM),
        scratch_shapes=[pltpu.SemaphoreType.DMA((7,)), pltpu.SemaphoreType.DMA((7,)), pltpu.SemaphoreType.DMA],
        compiler_params=pltpu.CompilerParams(vmem_limit_bytes=VMEM_LIMIT),
    )(v)


def _hbm_call(body, name, arrays, out_shapes, n_sems, aliases=None):
    hbm = pl.BlockSpec(memory_space=pltpu.HBM)
    return pl.pallas_call(
        body, name=name, out_shape=list(out_shapes), in_specs=[hbm] * len(arrays), out_specs=[hbm] * len(out_shapes),
        input_output_aliases=aliases or {},
        scratch_shapes=[pltpu.SemaphoreType.DMA((n_sems,)), pltpu.SemaphoreType.DMA((n_sems,))],
    )(*arrays)


def _half_rows(ref_shape, c):
    h = ref_shape[1] // 2
    return pl.ds(pl.multiple_of(c * h, 16), h), pl.ds(pl.multiple_of((1 - c) * h, 16), h)


def _pair_copies(p_refs, land_refs, send_sems, recv_sems):
    x, y, c = _coords()
    cps = []
    for w, (p, land) in enumerate(zip(p_refs, land_refs)):
        _, other = _half_rows(p.shape, c)
        cps.append(pltpu.make_async_remote_copy(src_ref=p.at[:, other], dst_ref=land, send_sem=send_sems.at[w],
                                                recv_sem=recv_sems.at[w], device_id=(x, y, 1 - c),
                                                device_id_type=MESH))
    return cps


def _pair_gather(gs, name):
    n = len(gs)

    def body(*refs):
        ins, outs, send_sems, recv_sems = refs[:n], refs[n:2 * n], refs[2 * n], refs[2 * n + 1]
        x, y, c = _coords()
        cps = []
        for w in range(n):
            cp = pltpu.make_async_remote_copy(src_ref=ins[w].at[c], dst_ref=outs[w].at[c], send_sem=send_sems.at[w],
                                              recv_sem=recv_sems.at[w], device_id=(x, y, 1 - c), device_id_type=MESH)
            cp.start()
            cps.append(cp)
        for w, cp in enumerate(cps):
            pltpu.make_async_remote_copy(src_ref=ins[w].at[c], dst_ref=outs[w].at[1 - c], send_sem=send_sems.at[w],
                                         recv_sem=recv_sems.at[w], device_id=(x, y, 1 - c),
                                         device_id_type=MESH).wait_recv()
            cp.wait_send()

    shapes = [jax.ShapeDtypeStruct(g.shape, g.dtype) for g in gs]
    return _hbm_call(body, name, gs, shapes, n, aliases={i: i for i in range(n)})


_HBM = pl.BlockSpec(memory_space=pltpu.HBM)
_SEM = pl.BlockSpec(memory_space=pltpu.SEMAPHORE)
_EFFECT = pltpu.SideEffectType.DATAFLOW_SIDE_EFFECTING


def _split_start(name, srcs, lands, after, n_copies, make_copies):
    n, m = len(srcs), len(lands)
    arrays = [pltpu.with_memory_space_constraint(a, pltpu.HBM) for a in list(srcs) + list(lands)]

    def body(*refs):
        src_refs, land_refs = refs[:n], refs[n:n + m]
        send_sems, recv_sems = refs[n + m + 1], refs[n + m + 2]
        for cp in make_copies(src_refs, land_refs, send_sems, recv_sems):
            cp.start()
        refs[-1][...] = jnp.zeros_like(refs[-1])

    outs = pl.pallas_call(
        body, name=name,
        out_shape=(pltpu.SemaphoreType.DMA((n_copies,)), pltpu.SemaphoreType.DMA((n_copies,)),
                   *[pltpu.HBM(a.shape, a.dtype) for a in arrays], jax.ShapeDtypeStruct((8, LANES), F32)),
        in_specs=[_HBM] * (n + m) + [pl.BlockSpec(memory_space=pl.ANY)],
        out_specs=(_SEM, _SEM, *[_HBM] * (n + m), pl.BlockSpec(memory_space=pltpu.VMEM)),
        input_output_aliases={i: 2 + i for i in range(n + m)},
        compiler_params=pltpu.CompilerParams(has_side_effects=_EFFECT),
    )(*arrays, after)
    return (outs[0], outs[1], list(outs[2:2 + n]), list(outs[2 + n:2 + n + m])), outs[-1]


def _split_wait(name, state, after, make_copies):
    send_sems, recv_sems, srcs, lands = state
    n, m = len(srcs), len(lands)

    def body(*refs):
        src_refs, land_refs = refs[:n], refs[n:n + m]
        for cp in make_copies(src_refs, land_refs, refs[n + m], refs[n + m + 1]):
            cp.wait_send()
            cp.wait_recv()

    outs = pl.pallas_call(
        body, name=name, out_shape=tuple(pltpu.HBM(a.shape, a.dtype) for a in srcs + lands),
        in_specs=[_HBM] * (n + m) + [_SEM, _SEM, pl.BlockSpec(memory_space=pl.ANY)], out_specs=tuple([_HBM] * (n + m)),
        input_output_aliases={i: i for i in range(n + m)},
        compiler_params=pltpu.CompilerParams(has_side_effects=_EFFECT),
    )(*srcs, *lands, send_sems, recv_sems, after)
    return list(outs[:n]), list(outs[n:])


def _scatter_copies(q_refs, land_refs, send_sems, recv_sems):
    x, y, c = _coords()
    cps = []
    for w, (q, land) in enumerate(zip(q_refs, land_refs)):
        for k, (px, py) in enumerate([(1 - x, y), (x, 1 - y), (1 - x, 1 - y)]):
            cps.append(pltpu.make_async_remote_copy(src_ref=q.at[2 * px + py], dst_ref=land.at[k],
                                                    send_sem=send_sems.at[3 * w + k], recv_sem=recv_sems.at[3 * w + k],
                                                    device_id=(px, py, c), device_id_type=MESH))
    return cps


def _chip_scatter_start(qs, after, name):
    lands = [lax.empty((3,) + q.shape[1:], q.dtype) for q in qs]
    return _split_start(name, qs, lands, after, 3 * len(qs), _scatter_copies)


def _chip_scatter_wait(state, after, name):
    return _split_wait(name, state, after, _scatter_copies)


def _gather_copies(src_refs, buf_refs, send_sems, recv_sems):
    x, y, c = _coords()
    j = 2 * x + y
    cps = []
    for w, buf in enumerate(buf_refs):
        mine, _ = _half_rows(buf.shape, c)
        for k, (px, py) in enumerate([(1 - x, y), (x, 1 - y), (1 - x, 1 - y)]):
            cps.append(pltpu.make_async_remote_copy(src_ref=buf.at[j, mine], dst_ref=buf.at[j, mine],
                                                    send_sem=send_sems.at[3 * w + k], recv_sem=recv_sems.at[3 * w + k],
                                                    device_id=(px, py, c), device_id_type=MESH))
    return cps


def _gather_wait_copies(src_refs, buf_refs, send_sems, recv_sems):
    x, y, c = _coords()
    j = 2 * x + y
    cps = []
    for w, buf in enumerate(buf_refs):
        mine, _ = _half_rows(buf.shape, c)
        for k, (px, py) in enumerate([(1 - x, y), (x, 1 - y), (1 - x, 1 - y)]):
            cps.append(pltpu.make_async_remote_copy(src_ref=buf.at[j, mine], dst_ref=buf.at[2 * px + py, mine],
                                                    send_sem=send_sems.at[3 * w + k], recv_sem=recv_sems.at[3 * w + k],
                                                    device_id=(px, py, c), device_id_type=MESH))
    return cps


def _pair_forward(bufs, name):
    n = len(bufs)

    def body(*refs):
        ins, outs, send_sems, recv_sems = refs[:n], refs[n:2 * n], refs[2 * n], refs[2 * n + 1]
        x, y, c = _coords()
        chips = [(1 - x, y), (x, 1 - y), (1 - x, 1 - y)]
        cps = []
        for w in range(n):
            mine, _ = _half_rows(outs[w].shape, c)
            for k, (px, py) in enumerate(chips):
                cp = pltpu.make_async_remote_copy(src_ref=ins[w].at[2 * px + py, mine],
                                                  dst_ref=outs[w].at[2 * px + py, mine],
                                                  send_sem=send_sems.at[3 * w + k], recv_sem=recv_sems.at[3 * w + k],
                                                  device_id=(x, y, 1 - c), device_id_type=MESH)
                cp.start()
                cps.append(cp)
        for w in range(n):
            _, sib = _half_rows(outs[w].shape, c)
            for k, (px, py) in enumerate(chips):
                pltpu.make_async_remote_copy(src_ref=ins[w].at[2 * px + py, sib], dst_ref=outs[w].at[2 * px + py, sib],
                                             send_sem=send_sems.at[3 * w + k], recv_sem=recv_sems.at[3 * w + k],
                                             device_id=(x, y, 1 - c), device_id_type=MESH).wait_recv()
        for cp in cps:
            cp.wait_send()

    shapes = [jax.ShapeDtypeStruct(b.shape, b.dtype) for b in bufs]
    return _hbm_call(body, name, bufs, shapes, 3 * n, aliases={i: i for i in range(n)})


def _slot_rows(h, cs):
    return _div_tile(h, max(16, (1 << 19) // cs), 16)


def _cast_into_slot(w, slot, name):
    r, cs = w.shape
    br = _slot_rows(r, cs)
    return _ew_slot(lambda a: a, name, (r // br,), slot, [w], [pl.BlockSpec((br, cs), lambda i, s: (i, 0))],
                    (4, r, cs), BF16, pl.BlockSpec((None, br, cs), lambda i, s: (s[0], i, 0)))


def _pair_add(p, rb, core, name):
    n, r, cs = p.shape
    h = r // 2
    br = _slot_rows(h, cs)
    nb = h // br
    return _ew_slot(lambda a, b: a + b, name, (n, nb), core, [p, rb],
                    [pl.BlockSpec((None, br, cs), lambda s, i, c: (s, c[0] * nb + i, 0)),
                     pl.BlockSpec((None, br, cs), lambda s, i, c: (s, i, 0))],
                    (n, h, cs), BF16, pl.BlockSpec((None, br, cs), lambda s, i, c: (s, i, 0)))


def _chip_sum(q, r3, slots, name):
    _, h, cs = q.shape
    br = _slot_rows(h, cs)

    def fn(a, b):
        acc = a.astype(F32)
        for k in range(3):
            acc = acc + b[k].astype(F32)
        return acc

    return _ew_slot(fn, name, (h // br,), slots, [q, r3],
                    [pl.BlockSpec((None, br, cs), lambda i, s: (s[0], i, 0)),
                     pl.BlockSpec((3, br, cs), lambda i, s: (0, i, 0))],
                    (2, h, cs), F32, pl.BlockSpec((None, br, cs), lambda i, s: (s[1], i, 0)))


def _sum_slots(r, name):
    n, h, w = r.shape
    br = _div_tile(h, 2048, 16)

    def fn(blk):
        acc = blk[0].astype(F32)
        for s in range(1, n):
            acc = acc + blk[s].astype(F32)
        return (acc,)

    return _ew(fn, name, (h // br,), [r], [pl.BlockSpec((n, br, w), lambda i: (0, i, 0))],
               [((h, w), F32, pl.BlockSpec((br, w), lambda i: (i, 0)))])[0]


def _pack_small(arrs):
    flat = jnp.concatenate([a.reshape(-1).astype(F32) for a in arrs])
    n = flat.shape[0]
    rows = -(-n // LANES)
    rows = -(-rows // 8) * 8
    return jnp.pad(flat, (0, rows * LANES - n)).reshape(rows, LANES)


def _unpack_small(p, shapes):
    lead = p.shape[:-2]
    flat = p.reshape(lead + (-1,))
    out, off = [], 0
    for sh in shapes:
        n = 1
        for d in sh:
            n *= d
        out.append(flat[..., off:off + n].reshape(lead + tuple(sh)))
        off += n
    return out


def kernel(x, c, w_ada, b_ada, norm1_g, w_in, dn_conv_w, dn_a_log, dn_dt_bias, dn_norm_g, dn_w_o, cf_conv_w, cf_ln_g, cf_ln_b, cf_w_o, w_out, norm2_g, ffn_w_up, ffn_conv_w, ffn_w_down, final_norm_g, loss_target, m_w_ada, m_b_ada, m_norm1_g, m_w_in, m_dn_conv_w, m_dn_a_log, m_dn_dt_bias, m_dn_norm_g, m_dn_w_o, m_cf_conv_w, m_cf_ln_g, m_cf_ln_b, m_cf_w_o, m_w_out, m_norm2_g, m_ffn_w_up, m_ffn_conv_w, m_ffn_w_down, m_final_norm_g, v_w_ada, v_b_ada, v_norm1_g, v_w_in, v_dn_conv_w, v_dn_a_log, v_dn_dt_bias, v_dn_norm_g, v_dn_w_o, v_cf_conv_w, v_cf_ln_g, v_cf_ln_b, v_cf_w_o, v_w_out, v_norm2_g, v_ffn_w_up, v_ffn_conv_w, v_ffn_w_down, v_final_norm_g):
    xi, yi, ci = _coords()
    chip = 2 * xi + yi
    me = 4 * xi + 2 * yi + ci
    core = jnp.reshape(ci, (1,)).astype(jnp.int32)

    S, D = x.shape[1], x.shape[2]
    NH = dn_a_log.shape[1]
    DH = dn_norm_g.shape[1]
    DNW = NH * DH
    CFW = cf_ln_g.shape[1]
    FFN = ffn_w_down.shape[1] * 4
    KDN, KCF, KFF = dn_conv_w.shape[1], cf_conv_w.shape[1], ffn_conv_w.shape[1]
    NIN = w_in.shape[2] * 4
    assert NIN == 4 * DNW + 2 * NH + 2 * CFW + 2 * D and DH == LANES and 2 * NH <= LANES
    x2, tgt = x[0], loss_target[0]

    sm_shapes = [(D,), (KDN, 3 * DNW // 4), (KCF, CFW // 4), (KFF, FFN // 4)]
    g1 = _all_gather_small(_pack_small([c[0], dn_conv_w[0], cf_conv_w[0], ffn_conv_w[0]]))
    c_all, dcw_s, ccw_s, fcw_s = _unpack_small(g1, sm_shapes)

    def chips_cols(t):
        t = t[0::2]
        return jnp.transpose(t, (1, 0, 2)).reshape(t.shape[1], -1)

    dn_cw, cf_cw, ff_cw = chips_cols(dcw_s), chips_cols(ccw_s), chips_cols(fcw_s)

    slot_chip = jnp.reshape(chip, (1,)).astype(jnp.int32)
    big = [w_in[0], dn_w_o[0], cf_w_o[0], w_out[0], ffn_w_up[0], ffn_w_down[0]]
    names_big = ["w_in", "dn_w_o", "cf_w_o", "w_out", "ffn_w_up", "ffn_w_down"]
    in_state, in_token = _split_start("ag_in_start", [], [_cast_into_slot(big[0], slot_chip, "cast_w_in")], c, 3,
                                      _gather_copies)
    bufs = [_cast_into_slot(w, slot_chip, "cast_" + nm) for w, nm in zip(big[1:], names_big[1:])]
    mid_state, mid_token = _split_start("ag_mid_start", [], bufs[:3], in_token, 9, _gather_copies)
    ag_state, ag_token = _split_start("ag_late_start", [], bufs[3:], mid_token, 6, _gather_copies)

    cb_cf = _div_tile(CFW, 256, LANES)
    o_b = 4 * DNW
    o_glu = o_b + 2 * NH
    o_ga = o_glu + 2 * CFW
    NA = NIN - 2 * NH
    a_z, a_ga, a_gb, a_glu, a_ba = 3 * DNW, 4 * DNW, 4 * DNW + D, 4 * DNW + 2 * D, NA

    segs = [(0, 0, o_b), (o_b, NA, 2 * NH), (o_ga, a_ga, 2 * D)]
    for t in range(CFW // cb_cf):
        segs += [(o_glu + t * cb_cf, a_glu + 2 * t * cb_cf, cb_cf),
                 (o_glu + CFW + t * cb_cf, a_glu + (2 * t + 1) * cb_cf, cb_cf)]
    CS = NIN // 4

    def shard_slices(lo, n):
        out = []
        while n > 0:
            j, off = lo // CS, lo % CS
            m = min(n, CS - off)
            out.append(w_in_g[j][:, off:off + m])
            lo, n = lo + m, n - m
        return out

    CA = w_ada.shape[2]
    b_sh = lax.dynamic_slice(b_ada, (0, chip * CA), (1, CA))
    tn_a = _div_tile(CA, 512, LANES)

    def mod_fn(cc, w, b):
        return (_mm(_silu(cc), w, NN) + b,)

    mod_sh = _ew(mod_fn, "ada_mod", (CA // tn_a,), [c_all, w_ada[0], b_sh],
                 [_full((8, D)), pl.BlockSpec((D, tn_a), lambda j: (0, j)), pl.BlockSpec((1, tn_a), lambda j: (0, j))],
                 [((8, CA), F32, pl.BlockSpec((8, tn_a), lambda j: (0, j)))])[0]
    g2 = _all_gather_small(_pack_small([mod_sh]))
    mod_all = _unpack_small(g2, [(8, CA)])[0][0::2]
    mod_all = jnp.transpose(mod_all, (1, 0, 2)).reshape(8, 4 * CA)
    mod_me = lax.dynamic_slice(mod_all, (me, 0), (1, 6 * D))
    sh1, sc1, gt1, sh2, sc2, gt2 = [mod_me[:, i * D:(i + 1) * D] for i in range(6)]

    bs = _div_tile(S, 128, 8)
    nb = S // bs
    vecD = _full((1, D))
    rowD = _row(bs, D)

    hn1 = _ew(lambda a, g, sc, sh: (_f_normmod(a, g, sc, sh),), "norm1_fwd", (nb,),
              [x2, norm1_g, sc1, sh1], [rowD, vecD, vecD, vecD], [((S, D), BF16, rowD)],
              after=[ag_token, m_w_in[0], v_w_in[0]])[0]
    _, in_landed = _split_wait("ag_in_wait", in_state, hn1, _gather_wait_copies)
    (w_in_g,) = _pair_forward(in_landed, "ag_in_pair")
    w_aug = jnp.concatenate([p for o, a, n in sorted(segs, key=lambda s: s[1]) for p in shard_slices(o, n)]
                            + [jnp.zeros((D, LANES - 2 * NH), BF16)], axis=1)
    proj = _matmul(hn1, w_aug, "nn", F32, "mm_in")

    def dn_post(j, cv):
        s = _silu(cv)
        nrm = s * lax.rsqrt(jnp.sum(s * s, axis=-1, keepdims=True) + EPS)
        fq = (j < NH).astype(F32)
        fk = (j < 2 * NH).astype(F32)
        scale = fq * (DH ** -0.5) + (1.0 - fq)
        return fk * (nrm * scale) + (1.0 - fk) * s

    def ident(a):
        return a

    def colS(w, off=0):
        return pl.BlockSpec((S, w), lambda j, off=off: (0, j + off))

    def wS(kw, w):
        return pl.BlockSpec((kw, w), lambda j: (0, j))

    qkv_spec = pl.BlockSpec((None, None, S, DH), lambda j: (j // NH, j % NH, 0, 0))
    dn_args = dict(kw=KDN, ncol=3 * NH, ins=[proj], in_specs=[colS(DH)], extras=[], extra_specs=[],
                   w=dn_cw, w_spec=wS(KDN, DH), pre=ident, post=dn_post)
    qkvn = _conv_fwd_call("dn_conv_fwd", out_shape=jax.ShapeDtypeStruct((3, NH, S, DH), F32), out_spec=qkv_spec,
                          **dn_args)

    alp = jnp.pad(dn_a_log, ((0, 0), (NH, LANES - 2 * NH)))
    dtb = jnp.pad(dn_dt_bias, ((0, 0), (NH, LANES - 2 * NH)))
    vecL = _full((1, LANES))
    ba_spec = _row(bs, LANES, a_ba // LANES)
    rowL = _row(bs, LANES)
    gate_fn = functools.partial(_f_dn_gate, NH)
    def gate_fwd(a, p, q):
        val = gate_fn(a, p, q)
        lane = lax.broadcasted_iota(jnp.int32, val.shape, 1)

        def spread(col):
            return jnp.sum(jnp.where(lane == col, val, 0.0), axis=-1, keepdims=True) + jnp.zeros(val.shape, F32)

        return tuple(spread(h) for h in range(NH)), tuple(spread(NH + h) for h in range(NH))

    hrow = pl.BlockSpec((NH, bs, DH), lambda i: (0, i, 0))
    bb_b, gb_b = _ew(gate_fwd, "dn_gate_fwd", (nb,), [proj, alp, dtb], [ba_spec, vecL, vecL],
                     [((NH, S, DH), F32, hrow), ((NH, S, DH), F32, hrow)])
    o_dn, states = _dn_fwd_call(qkvn, gb_b, bb_b)

    bsh = _div_tile(S, 512, 8)
    nbh = S // bsh
    o_spec = pl.BlockSpec((None, bsh, DH), lambda i, h: (h, i, 0))
    z_spec = pl.BlockSpec((bsh, DH), lambda i, h: (i, a_z // DH + h))
    oh_spec = pl.BlockSpec((bsh, DH), lambda i, h: (i, h))
    ng_spec = pl.BlockSpec((1, DH), lambda i, h: (0, 0))
    on = _ew(lambda o, z, g: (_f_dn_post(o, z, g),), "dn_post_fwd", (nbh, NH), [o_dn, proj, dn_norm_g],
             [o_spec, z_spec, ng_spec], [((S, DNW), BF16, oh_spec)])[0]
    _, landed = _split_wait("ag_mid_wait", mid_state, on, _gather_wait_copies)
    w_do_f, w_co_f, w_out_g = _pair_forward(landed, "ag_mid_pair")
    w_out_f = w_out_g.reshape(-1, w_out_g.shape[2])
    br_a = _matmul(on, w_do_f, "nn", F32, "mm_dn_o")

    def glu_pre(val, gl):
        return val * _sigmoid(gl)

    def glu_spec(t):
        return pl.BlockSpec((S, cb_cf), lambda j, t=t: (0, a_glu // cb_cf + 2 * j + t))

    cf_args = dict(kw=KCF, ncol=CFW // cb_cf, ins=[proj, proj], in_specs=[glu_spec(0), glu_spec(1)],
                   extras=[], extra_specs=[], w=cf_cw, w_spec=wS(KCF, cb_cf), pre=glu_pre, post=lambda j, cv: cv)
    uc = _conv_fwd_call("cf_conv_fwd", out_shape=jax.ShapeDtypeStruct((S, CFW), F32), out_spec=colS(cb_cf), **cf_args)
    rowC = _row(bs, CFW)
    vecC = _full((1, CFW))
    ub = _ew(lambda u, g, b: (_f_cf_ln(u, g, b),), "cf_ln_fwd", (nb,), [uc, cf_ln_g, cf_ln_b], [rowC, vecC, vecC],
             [((S, CFW), BF16, rowC)])[0]
    br_b = _matmul(ub, w_co_f, "nn", F32, "mm_cf_o")

    ga_spec, gb_spec = _row(bs, D, a_ga // D), _row(bs, D, a_gb // D)
    merged = _ew(lambda a, b, ga, gb: (_f_merge(a, b, ga, gb),), "merge_fwd", (nb,), [br_a, br_b, proj, proj],
                 [rowD, rowD, ga_spec, gb_spec], [((S, D), BF16, rowD)])[0]
    mix = _matmul(merged, w_out_f, "nn", F32, "mm_out")

    x1, hn2 = _ew(_f_res_normmod, "norm2_fwd", (nb,), [x2, mix, gt1, norm2_g, sc2, sh2],
                  [rowD, rowD, vecD, vecD, vecD, vecD], [((S, D), F32, rowD), ((S, D), BF16, rowD)])
    _, landed = _split_wait("ag_late_wait", ag_state, hn2, _gather_wait_copies)
    w_up_f, w_dn_g = _pair_forward(landed, "ag_late_pair")
    w_dn_f = w_dn_g.reshape(-1, w_dn_g.shape[2])
    up_all = _matmul(hn2, w_up_f, "nn", F32, "mm_up")

    cb_ff = _div_tile(FFN, 256, LANES)
    ff_args = dict(kw=KFF, ncol=FFN // cb_ff, ins=[up_all], in_specs=[colS(cb_ff)], extras=[up_all],
                   extra_specs=[colS(cb_ff, FFN // cb_ff)], w=ff_cw, w_spec=wS(KFF, cb_ff),
                   pre=ident, post=lambda j, cv, up: _silu(cv) * up)
    hff = _conv_fwd_call("ffn_conv_fwd", out_shape=jax.ShapeDtypeStruct((S, FFN), BF16), out_spec=colS(cb_ff), **ff_args)
    ffo = _matmul(hff, w_dn_f, "nn", F32, "mm_down")

    gf2 = final_norm_g.reshape(1, D)

    def loss_bwd(a, f, gt, gf, t):
        val, vjp = jax.vjp(_f_loss, a, f, gt, gf, t)
        da, df, dgt, dgf, _ = vjp(jnp.ones((), F32))
        return da, df, dgt, dgf, jnp.zeros((1, LANES), F32) + val

    dx1_l, dffo, dgt2, dgf, loss_v = _ew(
        loss_bwd, "loss_bwd", (nb,), [x1, ffo, gt2, gf2, tgt], [rowD, rowD, vecD, vecD, rowD],
        [((S, D), F32, rowD), ((S, D), BF16, rowD), ((1, D), F32, vecD), ((1, D), F32, vecD),
         ((1, LANES), F32, vecL)], acc=(2, 3, 4))

    dhff = _matmul(dffo, w_dn_f, "nt", F32, "mm_down_dx")
    g_w_dn = _matmul(hff, dffo, "tn", F32, "mm_down_dw")

    slots = jnp.stack([chip, ci]).astype(jnp.int32)
    rs_groups = []

    def rs_pair_begin(parts, tag):
        lands = [lax.empty((4, p.shape[1] // 2, p.shape[2]), F32) for p in parts]
        return _split_start("rs_pair_start_" + tag, parts, lands, core, len(parts), _pair_copies)

    def rs_chips_begin(pair_state, nms, tag, after):
        parts, rbs = _split_wait("rs_pair_wait_" + tag, pair_state, after, _pair_copies)
        q16 = [_pair_add(p, rb, core, "rs_pair_add_" + nm) for p, rb, nm in zip(parts, rbs, nms)]
        state, token = _chip_scatter_start(q16, core, "rs_chips_start_" + tag)
        rs_groups.append((state, nms, tag))
        return token

    pair_a, ptok_a = rs_pair_begin([g_w_dn.reshape(4, FFN // 4, D)], "a")

    d_upall, g_ffcw = _conv_bwd_call(
        "ffn_conv_bwd", dout=dhff, dout_spec=colS(cb_ff),
        dio_shapes=[jax.ShapeDtypeStruct((2, S, FFN), BF16)],
        dio_specs=[pl.BlockSpec((2, S, cb_ff), lambda j: (0, 0, j))], dio_pack=lambda dg, du: [(dg, du)],
        dw_shape=jax.ShapeDtypeStruct((KFF, FFN), F32), dw_spec=wS(KFF, cb_ff), after=[ptok_a], **ff_args)
    tok_a = rs_chips_begin(pair_a, ["ffn_w_down"], "a", d_upall)
    dhn2 = _matmul(d_upall, w_up_f, "nt", F32, "mm_up_dx", after=[tok_a])
    g_w_up = _matmul(hn2, d_upall, "tn", F32, "mm_up_dw", out_groups=4)
    pair_b, ptok_b = rs_pair_begin([g_w_up], "b")

    def res2_bwd(a, mx, gt, g, sc, sh, dx1, dhn):
        _, vjp = jax.vjp(_f_res_normmod, a, mx, gt, g, sc, sh)
        return vjp((dx1, dhn))

    dx_r, dmix, dgt1, dg2, dsc2, dsh2 = _ew(
        res2_bwd, "norm2_bwd", (nb,), [x2, mix, gt1, norm2_g, sc2, sh2, dx1_l, dhn2],
        [rowD, rowD, vecD, vecD, vecD, vecD, rowD, rowD],
        [((S, D), F32, rowD), ((S, D), BF16, rowD)] + [((1, D), F32, vecD)] * 4, acc=(2, 3, 4, 5), after=[ptok_b])

    dmerged = _matmul(dmix, w_out_f, "nt", F32, "mm_out_dx")
    g_w_out = _matmul(merged, dmix, "tn", F32, "mm_out_dw")

    def merge_bwd(a, b, ga, gb, dm):
        _, vjp = jax.vjp(_f_merge, a, b, ga, gb)
        da, db, dga, dgb = vjp(dm)
        return da, db, jnp.concatenate([dga, dgb], axis=1)

    d_bra, d_brb, dproj = _ew(merge_bwd, "merge_bwd", (nb,), [br_a, br_b, proj, proj, dmerged],
                              [rowD, rowD, ga_spec, gb_spec, rowD],
                              [((S, D), BF16, rowD), ((S, D), BF16, rowD),
                               ((S, NA + LANES), BF16, _row(bs, 2 * D, a_ga // (2 * D)))])

    tok_b = rs_chips_begin(pair_b, ["ffn_w_up"], "b", d_brb)
    d_on = _matmul(d_bra, w_do_f, "nt", F32, "mm_dn_o_dx", after=[tok_b])
    g_w_do = _matmul(on, d_bra, "tn", F32, "mm_dn_o_dw", out_groups=4)
    d_ub = _matmul(d_brb, w_co_f, "nt", F32, "mm_cf_o_dx")
    g_w_co = _matmul(ub, d_brb, "tn", F32, "mm_cf_o_dw", out_groups=4)
    pair_c, ptok_c = rs_pair_begin([g_w_do, g_w_co, g_w_out.reshape(4, D // 4, D)], "c")

    def cf_ln_bwd(u, g, b, du):
        _, vjp = jax.vjp(_f_cf_ln, u, g, b)
        return vjp(du)

    d_uc, g_cflg, g_cflb = _ew(cf_ln_bwd, "cf_ln_bwd", (nb,), [uc, cf_ln_g, cf_ln_b, d_ub], [rowC, vecC, vecC, rowC],
                               [((S, CFW), F32, rowC), ((1, CFW), F32, vecC), ((1, CFW), F32, vecC)], acc=(1, 2),
                               after=[ptok_c])
    dproj_sds = jax.ShapeDtypeStruct((S, NA + LANES), BF16)
    dproj, g_cfcw = _conv_bwd_call(
        "cf_conv_bwd", dout=d_uc, dout_spec=colS(cb_cf), dio_shapes=[dproj_sds],
        dio_specs=[colS(2 * cb_cf, a_glu // (2 * cb_cf))], dio_pack=lambda dv, dg: [jnp.concatenate([dv, dg], axis=1)],
        dw_shape=jax.ShapeDtypeStruct((KCF, CFW), F32), dw_spec=wS(KCF, cb_cf), alias=dproj, **cf_args)

    def dn_post_bwd(o, z, g, d):
        _, vjp = jax.vjp(_f_dn_post, o, z, g)
        return vjp(d)

    tok_c = rs_chips_begin(pair_c, ["dn_w_o", "cf_w_o", "w_out"], "c", g_cfcw)
    d_o, dproj, g_dnng = _ew(dn_post_bwd, "dn_post_bwd", (nbh, NH), [o_dn, proj, dn_norm_g, d_on],
                             [o_spec, z_spec, ng_spec, oh_spec],
                             [((NH, S, DH), F32, o_spec), ((S, NA + LANES), BF16, z_spec), ((1, DH), F32, ng_spec)],
                             acc=(2,), alias=(dproj, 1), after=[tok_c])

    dqkvn, dgb_b, dbb_b = _dn_bwd_call(qkvn, gb_b, bb_b, states, d_o)
    dproj, g_dncw = _conv_bwd_call(
        "dn_conv_bwd", dout=dqkvn, dout_spec=qkv_spec, dio_shapes=[dproj_sds], dio_specs=[colS(DH)],
        dio_pack=lambda d: [d], dw_shape=jax.ShapeDtypeStruct((KDN, 3 * DNW), F32), dw_spec=wS(KDN, DH),
        alias=dproj, **dn_args)

    def gate_bwd(a, p, q, db, dg):
        lane = lax.broadcasted_iota(jnp.int32, a.shape, 1)
        d = jnp.zeros(a.shape, F32)
        for h in range(NH):
            d = d + jnp.where(lane == h, db[h], 0.0) + jnp.where(lane == NH + h, dg[h], 0.0)
        _, vjp = jax.vjp(gate_fn, a, p, q)
        return vjp(d)

    dproj, g_alp, g_dtb = _ew(gate_bwd, "dn_gate_bwd", (nb,), [proj, alp, dtb, dbb_b, dgb_b],
                              [ba_spec, vecL, vecL, hrow, hrow],
                              [((S, NA + LANES), BF16, ba_spec), ((1, LANES), F32, vecL), ((1, LANES), F32, vecL)],
                              acc=(1, 2), alias=(dproj, 0))

    g_w_aug = _matmul(hn1, dproj, "tn", F32, "mm_in_dw")
    def aug_slices(lo, n):
        out = []
        for o, a, m in sorted(segs):
            s, e = max(lo, o), min(lo + n, o + m)
            if s < e:
                out.append(g_w_aug[:, a + s - o:a + e - o])
        return out

    g_w_in = jnp.stack([jnp.concatenate(aug_slices(j * CS, CS), axis=1) for j in range(4)])
    pair_d, ptok_d = rs_pair_begin([g_w_in], "d")
    dhn1 = _matmul(dproj, w_aug, "nt", F32, "mm_in_dx", after=[ptok_d])
    tok_d = rs_chips_begin(pair_d, ["w_in"], "d", dhn1)

    def norm1_bwd(a, g, sc, sh, dhn, dxr):
        _, vjp = jax.vjp(_f_normmod, a, g, sc, sh)
        da, dg, dsc, dsh = vjp(dhn)
        return da + dxr, dg, dsc, dsh

    grad_x, dg1, dsc1, dsh1 = _ew(norm1_bwd, "norm1_bwd", (nb,), [x2, norm1_g, sc1, sh1, dhn1, dx_r],
                                  [rowD, vecD, vecD, vecD, rowD, rowD],
                                  [((S, D), F32, rowD)] + [((1, D), F32, vecD)] * 3, acc=(1, 2, 3), after=[tok_d])

    names = ["w_ada", "b_ada", "norm1_g", "w_in", "dn_conv_w", "dn_a_log", "dn_dt_bias", "dn_norm_g", "dn_w_o",
             "cf_conv_w", "cf_ln_g", "cf_ln_b", "cf_w_o", "w_out", "norm2_g", "ffn_w_up", "ffn_conv_w", "ffn_w_down",
             "final_norm_g"]
    wts = dict(zip(names, [w_ada, b_ada, norm1_g, w_in, dn_conv_w, dn_a_log, dn_dt_bias, dn_norm_g, dn_w_o, cf_conv_w,
                           cf_ln_g, cf_ln_b, cf_w_o, w_out, norm2_g, ffn_w_up, ffn_conv_w, ffn_w_down, final_norm_g]))
    ms = dict(zip(names, [m_w_ada, m_b_ada, m_norm1_g, m_w_in, m_dn_conv_w, m_dn_a_log, m_dn_dt_bias, m_dn_norm_g,
                          m_dn_w_o, m_cf_conv_w, m_cf_ln_g, m_cf_ln_b, m_cf_w_o, m_w_out, m_norm2_g, m_ffn_w_up,
                          m_ffn_conv_w, m_ffn_w_down, m_final_norm_g]))
    vs = dict(zip(names, [v_w_ada, v_b_ada, v_norm1_g, v_w_in, v_dn_conv_w, v_dn_a_log, v_dn_dt_bias, v_dn_norm_g,
                          v_dn_w_o, v_cf_conv_w, v_cf_ln_g, v_cf_ln_b, v_cf_w_o, v_w_out, v_norm2_g, v_ffn_w_up,
                          v_ffn_conv_w, v_ffn_w_down, v_final_norm_g]))
    grads, delta, new_m, new_v = {}, {}, {}, {}

    def adam_large(n, g):
        grads[n] = g.reshape(wts[n].shape)
        d_, m_, v_ = _adamw(wts[n][0], grads[n][0], ms[n][0], vs[n][0], "adamw_" + n)
        delta[n], new_m[n], new_v[n] = d_[None], m_[None], v_[None]

    def rs_finish(groups, after, tag):
        hs, nms_all = [], []
        for state, nms, t in groups:
            q16, r3s = _chip_scatter_wait(state, after, "rs_chips_wait_" + t)
            for q, r3, nm in zip(q16, r3s, nms):
                hs.append(_chip_sum(q, r3, slots, "rs_chip_sum_" + nm))
                nms_all.append(nm)
        for nm, g in zip(nms_all, _pair_gather(hs, "rs_pair_gather_" + tag)):
            adam_large(nm, g)

    rs_finish(rs_groups[:3], grad_x, "abc")

    dmod = jnp.concatenate([dsh1, dsc1, dgt1, dsh2, dsc2, dgt2], axis=1)
    sm2 = [dmod, dg1, dg2, dgf, g_alp, g_dtb, g_dnng, g_cflg, g_cflb, g_dncw, g_cfcw, g_ffcw]
    sm2_shapes = [tuple(a.shape) for a in sm2]
    g3 = _all_gather_small(_pack_small(sm2))
    ssum = _sum_slots(g3, "small_sum")
    dmod_all = _unpack_small(g3, sm2_shapes[:1])[0].reshape(8, 6 * D)
    (g_b_ada, gs_n1, gs_n2, gs_fn, gs_alp, gs_dtb, gs_dnng, gs_cflg, gs_cflb, gs_dncw, gs_cfcw,
     gs_ffcw) = _unpack_small(ssum, sm2_shapes)
    gs_alog, gs_dtbias = gs_alp[:, NH:2 * NH], gs_dtb[:, NH:2 * NH]
    gs_dncw = lax.dynamic_slice(gs_dncw, (0, chip * (3 * DNW // 4)), (KDN, 3 * DNW // 4))
    gs_cfcw = lax.dynamic_slice(gs_cfcw, (0, chip * (CFW // 4)), (KCF, CFW // 4))
    gs_ffcw = lax.dynamic_slice(gs_ffcw, (0, chip * (FFN // 4)), (KFF, FFN // 4))

    dmod_sh = lax.dynamic_slice(dmod_all, (0, chip * CA), (8, CA))

    def wada_fn(cc, dm):
        return (_mm(_silu(cc), dm, TN),)

    g_w_ada = _ew(wada_fn, "ada_dw", (CA // tn_a,), [c_all, dmod_sh],
                  [_full((8, D)), pl.BlockSpec((8, tn_a), lambda j: (0, j))],
                  [((D, CA), F32, pl.BlockSpec((D, tn_a), lambda j: (0, j)))])[0]

    loss = lax.psum(loss_v[0, 0], ("x", "y", "c"))

    adam_large("w_ada", g_w_ada)
    small_grads = {"b_ada": g_b_ada, "norm1_g": gs_n1, "dn_conv_w": gs_dncw, "dn_a_log": gs_alog,
                   "dn_dt_bias": gs_dtbias, "dn_norm_g": gs_dnng, "cf_conv_w": gs_cfcw, "cf_ln_g": gs_cflg,
                   "cf_ln_b": gs_cflb, "norm2_g": gs_n2, "ffn_conv_w": gs_ffcw, "final_norm_g": gs_fn}
    small = [n for n in names if n in small_grads]
    for n in small:
        grads[n] = small_grads[n].reshape(wts[n].shape)
    sm_sh = [tuple(wts[n].shape) for n in small]
    d_, m_, v_ = _adamw(_pack_small([wts[n] for n in small]), _pack_small([grads[n] for n in small]),
                        _pack_small([ms[n] for n in small]), _pack_small([vs[n] for n in small]), "adamw_small")
    for n, a, b_, c_ in zip(small, _unpack_small(d_, sm_sh), _unpack_small(m_, sm_sh), _unpack_small(v_, sm_sh)):
        delta[n], new_m[n], new_v[n] = a, b_, c_
    rs_finish(rs_groups[3:], delta["w_ada"], "d")

    return (loss, grad_x[None], *[grads[n] for n in names], *[delta[n] for n in names],
            *[new_m[n] for n in names], *[new_v[n] for n in names])
```

```python
import functools

import jax
import jax.numpy as jnp
from jax import lax
from jax.experimental import pallas as pl
from jax.experimental.pallas import tpu as pltpu

F32 = jnp.float32
BF16 = jnp.bfloat16
EPS = 1e-6
LANES = 128
VMEM_LIMIT = 48 * 1024 * 1024
DN_CHUNK = 128
ADAM_LR, ADAM_B1, ADAM_B2, ADAM_EPS, ADAM_WD, ADAM_STEP = 0.001, 0.9, 0.999, 1e-08, 0.01, 10
MESH = pl.DeviceIdType.MESH

NN = (((1,), (0,)), ((), ()))
NT = (((1,), (1,)), ((), ()))
TN = (((0,), (0,)), ((), ()))
_DIMS = {"nn": NN, "nt": NT, "tn": TN}


def _mm(a, b, dims):
    return lax.dot_general(a.astype(BF16), b.astype(BF16), dims, preferred_element_type=F32)


def _mmx(a, b, dims):
    return lax.dot_general(a, b, dims, precision=lax.Precision.HIGH, preferred_element_type=F32)


def _div_tile(n, target, mult):
    best = None
    t = mult
    while t <= min(n, target):
        if n % t == 0:
            best = t
        t += mult
    return best if best is not None else n


def _params(sem=None):
    kw = dict(vmem_limit_bytes=VMEM_LIMIT)
    if sem is not None:
        kw["dimension_semantics"] = sem
    return pltpu.CompilerParams(**kw)


def _sigmoid(x):
    return jax.nn.sigmoid(x)


def _silu(x):
    return x * jax.nn.sigmoid(x)


def _softplus(x):
    return jnp.maximum(x, 0.0) + jnp.log(1.0 + jnp.exp(-jnp.abs(x)))


def _view(arr):
    if arr.ndim == 3:
        return arr.shape[1], arr.shape[0] * arr.shape[2], arr.shape[0]
    return arr.shape[0], arr.shape[1], 1


def _tile_spec(groups, cols, tr, tc, rsel, csel):
    if groups > 1:
        per = cols // groups // tc
        return pl.BlockSpec((None, tr, tc), lambda i, j, k: (csel(i, j, k) // per, rsel(i, j, k), csel(i, j, k) % per))
    return pl.BlockSpec((tr, tc), lambda i, j, k: (rsel(i, j, k), csel(i, j, k)))


def _matmul(a, b, mode, out_dtype, name, out_groups=1, after=()):
    ar, ac, ag = _view(a)
    br, bc, bg = _view(b)
    if mode == "nn":
        M, K, N = ar, ac, bc
        kdiv, mdiv, ndiv = ac // ag, M, min(bc // bg, N // out_groups)
    elif mode == "nt":
        M, K, N = ar, ac, br
        kdiv, mdiv, ndiv = min(ac // ag, bc // bg), M, N // out_groups
    else:
        K, M, N = ar, ac, bc
        kdiv, mdiv, ndiv = K, ac // ag, min(bc // bg, N // out_groups)
    tm = _div_tile(mdiv, 1024, LANES)
    tn = _div_tile(ndiv, 1536, LANES)
    tk = _div_tile(kdiv, 2048, LANES)
    nk = K // tk
    dims = _DIMS[mode]
    si, sj, sk = (lambda i, j, k: i), (lambda i, j, k: j), (lambda i, j, k: k)
    a_spec = {"nn": _tile_spec(ag, ac, tm, tk, si, sk), "nt": _tile_spec(ag, ac, tm, tk, si, sk),
              "tn": _tile_spec(ag, ac, tk, tm, sk, si)}[mode]
    b_spec = {"nn": _tile_spec(bg, bc, tk, tn, sk, sj), "nt": _tile_spec(bg, bc, tn, tk, sj, sk),
              "tn": _tile_spec(bg, bc, tk, tn, sk, sj)}[mode]
    out_shape = (M, N) if out_groups == 1 else (out_groups, M, N // out_groups)

    n_after = len(after)

    def body(*refs):
        a_ref, b_ref, o_ref = refs[0], refs[1], refs[2 + n_after]
        if nk == 1:
            o_ref[...] = lax.dot_general(a_ref[...], b_ref[...], dims, preferred_element_type=F32).astype(o_ref.dtype)
            return
        acc_ref = refs[3 + n_after]
        k = pl.program_id(2)

        @pl.when(k == 0)
        def _():
            acc_ref[...] = jnp.zeros_like(acc_ref)

        acc_ref[...] += lax.dot_general(a_ref[...], b_ref[...], dims, preferred_element_type=F32)

        @pl.when(k == nk - 1)
        def _():
            o_ref[...] = acc_ref[...].astype(o_ref.dtype)

    return pl.pallas_call(
        body, name=name, grid=(M // tm, N // tn, nk),
        in_specs=[a_spec, b_spec] + [pl.BlockSpec(memory_space=pl.ANY)] * n_after,
        out_specs=_tile_spec(out_groups, N, tm, tn, si, sj),
        out_shape=jax.ShapeDtypeStruct(out_shape, out_dtype),
        scratch_shapes=[pltpu.VMEM((tm, tn), F32)] if nk > 1 else [],
        compiler_params=_params(("parallel", "parallel", "arbitrary")),
    )(a, b, *after)


def _ew(fn, name, grid, ins, in_specs, outs, acc=(), alias=None, after=()):
    n_in = len(ins)
    n_ax = len(grid)
    extra, aliases = list(after), {}
    if alias is not None:
        extra, aliases = extra + [alias[0]], {n_in + len(after): alias[1]}
    extra_specs = [pl.BlockSpec(memory_space=pl.ANY)] * len(extra)

    def body(*refs):
        in_refs, out_refs = refs[:n_in], refs[n_in + len(extra):]
        ids = [pl.program_id(a) for a in range(n_ax)]
        res = fn(*[r[...] for r in in_refs])
        first = ids[0] == 0
        for t in ids[1:]:
            first = jnp.logical_and(first, t == 0)
        for idx, (r, val) in enumerate(zip(out_refs, res)):
            if idx in acc:
                @pl.when(first)
                def _(r=r, val=val):
                    r[...] = val.astype(r.dtype)

                @pl.when(jnp.logical_not(first))
                def _(r=r, val=val):
                    r[...] += val.astype(r.dtype)
            elif isinstance(val, tuple):
                for t, part in enumerate(val):
                    r[t] = part.astype(r.dtype)
            else:
                r[...] = val.astype(r.dtype)

    return pl.pallas_call(
        body, name=name, grid=grid, in_specs=list(in_specs) + extra_specs,
        out_specs=[o[2] for o in outs],
        out_shape=[jax.ShapeDtypeStruct(o[0], o[1]) for o in outs],
        input_output_aliases=aliases,
        compiler_params=_params(("arbitrary",) * n_ax),
    )(*ins, *extra)


def _ew_slot(fn, name, grid, slots, ins, in_specs, out_shape, out_dtype, out_spec):
    def body(s_ref, *refs):
        refs[-1][...] = fn(*[r[...] for r in refs[:-1]]).astype(refs[-1].dtype)

    return pl.pallas_call(
        body, name=name,
        grid_spec=pltpu.PrefetchScalarGridSpec(num_scalar_prefetch=1, grid=grid, in_specs=list(in_specs),
                                               out_specs=out_spec),
        out_shape=jax.ShapeDtypeStruct(out_shape, out_dtype),
        compiler_params=_params(("arbitrary",) * len(grid)),
    )(slots, *ins)


def _row(bs, w, col=0):
    return pl.BlockSpec((bs, w), lambda i, col=col: (i, col))


def _full(shape):
    nd = len(shape)
    return pl.BlockSpec(tuple(shape), lambda *_: (0,) * nd)


def _rms(x, g):
    return x * lax.rsqrt(jnp.mean(x * x, axis=-1, keepdims=True) + EPS) * g


def _f_normmod(x, g, sc, sh):
    return _rms(x, g) * (1.0 + sc) + sh


def _f_res_normmod(x, mix, gt, g, sc, sh):
    x1 = x + gt * mix
    return x1, _f_normmod(x1, g, sc, sh)


def _f_loss(x1, f, gt, gf, tgt):
    y = _rms(x1 + gt * f, gf)
    return 0.5 * jnp.sum(jnp.mean(jnp.square(y - tgt), axis=-1))


def _f_dn_gate(nh, ba, alp, dtb):
    lane = lax.broadcasted_iota(jnp.int32, ba.shape, 1)
    m = (lane < nh).astype(F32)
    beta = _sigmoid(ba)
    g = -jnp.exp(alp) * _softplus(ba + dtb)
    return m * beta + (1.0 - m) * g


def _f_dn_post(o, z, g):
    return o * lax.rsqrt(jnp.mean(o * o, axis=-1, keepdims=True) + EPS) * g * _silu(z)


def _f_cf_ln(u, g, b):
    mu = jnp.mean(u, axis=-1, keepdims=True)
    xc = u - mu
    y = xc * lax.rsqrt(jnp.mean(xc * xc, axis=-1, keepdims=True) + EPS)
    return _silu(y * g + b)


def _f_merge(a, b, ga, gb):
    return _sigmoid(ga) * a + _sigmoid(gb) * b


def _shift_down(u, d, rows):
    if d == 0:
        return u
    return jnp.where(rows >= d, pltpu.roll(u, d, 0), 0.0)


def _shift_up(u, d, rows):
    if d == 0:
        return u
    s = u.shape[0]
    return jnp.where(rows < s - d, pltpu.roll(u, s - d, 0), 0.0)


def _conv(u, w_ref, kw, rows):
    acc = None
    for k in range(kw):
        t = w_ref[k:k + 1, :] * _shift_down(u, kw - 1 - k, rows)
        acc = t if acc is None else acc + t
    return acc


def _conv_t(dc, w_ref, kw, rows):
    acc = None
    for k in range(kw):
        t = w_ref[k:k + 1, :] * _shift_up(dc, kw - 1 - k, rows)
        acc = t if acc is None else acc + t
    return acc


def _conv_fwd_call(name, kw, ncol, ins, in_specs, extras, extra_specs, w, w_spec, pre, post, out_shape, out_spec):
    n_in, n_ex = len(ins), len(extras)

    def body(*refs):
        in_refs, ex_refs = refs[:n_in], refs[n_in:n_in + n_ex]
        w_ref, out_ref = refs[n_in + n_ex], refs[n_in + n_ex + 1]
        j = pl.program_id(0)
        u = pre(*[r[...] for r in in_refs])
        rows = lax.broadcasted_iota(jnp.int32, u.shape, 0)
        cv = _conv(u, w_ref, kw, rows)
        out_ref[...] = post(j, cv, *[r[...] for r in ex_refs]).astype(out_ref.dtype)

    return pl.pallas_call(
        body, name=name, grid=(ncol,), in_specs=list(in_specs) + list(extra_specs) + [w_spec],
        out_specs=out_spec, out_shape=out_shape, compiler_params=_params(("arbitrary",)),
    )(*ins, *extras, w)


def _conv_bwd_call(name, kw, ncol, ins, in_specs, extras, extra_specs, w, w_spec, pre, post, dout, dout_spec,
                   dio_shapes, dio_specs, dio_pack, dw_shape, dw_spec, alias=None, after=()):
    n_in, n_ex, n_io = len(ins), len(extras), len(dio_shapes)
    al, aliases = list(after), {}
    if alias is not None:
        al, aliases = al + [alias], {n_in + n_ex + 2 + len(after): 0}
    al_specs = [pl.BlockSpec(memory_space=pl.ANY)] * len(al)

    def body(*refs):
        in_refs, ex_refs = refs[:n_in], refs[n_in:n_in + n_ex]
        w_ref, dout_ref = refs[n_in + n_ex], refs[n_in + n_ex + 1]
        outs = refs[n_in + n_ex + 2 + len(al):]
        dio_refs, dw_ref = outs[:n_io], outs[n_io]
        j = pl.program_id(0)
        u, pre_vjp = jax.vjp(pre, *[r[...] for r in in_refs])
        rows = lax.broadcasted_iota(jnp.int32, u.shape, 0)
        cv = _conv(u, w_ref, kw, rows)
        _, post_vjp = jax.vjp(lambda cc, *ex: post(j, cc, *ex), cv, *[r[...] for r in ex_refs])
        g = post_vjp(dout_ref[...].astype(F32))
        dc = g[0]
        for k in range(kw):
            dw_ref[k:k + 1, :] = jnp.sum(dc * _shift_down(u, kw - 1 - k, rows), axis=0, keepdims=True)
        du = _conv_t(dc, w_ref, kw, rows)
        for r, val in zip(dio_refs, dio_pack(*pre_vjp(du), *g[1:])):
            if isinstance(val, tuple):
                for t, part in enumerate(val):
                    r[t] = part.astype(r.dtype)
            else:
                r[...] = val.astype(r.dtype)

    return pl.pallas_call(
        body, name=name, grid=(ncol,),
        in_specs=list(in_specs) + list(extra_specs) + [w_spec, dout_spec] + al_specs,
        out_specs=list(dio_specs) + [dw_spec],
        out_shape=list(dio_shapes) + [dw_shape],
        input_output_aliases=aliases,
        compiler_params=_params(("arbitrary",)),
    )(*ins, *extras, w, dout, *al)


def _tri_inverse(a):
    c = a.shape[0]
    ii = lax.broadcasted_iota(jnp.int32, (c, c), 0)
    jj = lax.broadcasted_iota(jnp.int32, (c, c), 1)
    eye = (ii == jj).astype(F32)

    def same_block(bits):
        return jnp.right_shift(ii, bits) == jnp.right_shift(jj, bits)

    d = jnp.where(same_block(3), a, 0.0)
    d2 = _mmx(d, d, NN)
    yield
    t = (eye - d) + _mmx(eye - d, d2, NN)
    d4 = _mmx(d2, d2, NN)
    yield
    t = t + _mmx(t, d4, NN)
    yield
    bits = 3
    while (1 << bits) < c:
        low = jnp.where(jnp.logical_and(same_block(bits + 1), jnp.logical_not(same_block(bits))), a, 0.0)
        tl = _mmx(t, low, NN)
        yield
        t = t - _mmx(tl, t, NN)
        yield
        bits += 1
    return t


def _lockstep(gens):
    out = [None] * len(gens)
    live = list(range(len(gens)))
    while live:
        nxt = []
        for i in live:
            try:
                next(gens[i])
                nxt.append(i)
            except StopIteration as e:
                out[i] = e.value
        live = nxt
    return out


def _dn_common(q, k, v, gb, bb):
    c = q.shape[0]
    ii = lax.broadcasted_iota(jnp.int32, (c, c), 0)
    jj = lax.broadcasted_iota(jnp.int32, (c, c), 1)
    causal = jj <= ii
    strict = jj < ii
    low = causal.astype(F32)
    ones = jnp.ones((c, LANES), F32)
    gc = _mmx(low, gb, NN)
    kb = k * bb
    vb = v * bb
    kk = _mm(kb, k, NT)
    qk = _mm(q, k, NT)
    yield
    diff = (_mmx(gc, ones, NT) - _mmx(ones, gc, NT)) * (1.0 / LANES)
    gl = jnp.sum(gb, axis=0, keepdims=True)
    eg = jnp.exp(gc)
    egm = jnp.exp(gl - gc)
    egl = jnp.exp(gl)
    kbg = kb * eg
    yield
    decay = jnp.where(causal, jnp.exp(jnp.where(causal, diff, 0.0)), 0.0)
    t = yield from _tri_inverse(jnp.where(strict, kk * decay, 0.0))
    attn = qk * decay
    return dict(causal=causal, strict=strict, low=low, ones=ones, decay=decay, eg=eg, egm=egm, egl=egl,
                kb=kb, vb=vb, kbg=kbg, kk=kk, t=t, qk=qk, attn=attn, qd=q * eg, kd=k * egm)


def _dn_fwd_stages(q, k, v, gb, bb, s):
    m = yield from _dn_common(q, k, v, gb, bb)
    r = m["vb"] - _mm(m["kbg"], s, NN)
    qs = _mm(m["qd"], s, NN)
    yield
    u = _mmx(m["t"], r, NN)
    yield
    o = qs + _mm(m["attn"], u, NN)
    s2 = s * m["egl"] + _mm(m["kd"], u, TN)
    return o, s2


def _dn_bwd_stages(q, k, v, gb, bb, s, do, dsp):
    m = yield from _dn_common(q, k, v, gb, bb)
    c = q.shape[0]
    t, decay, eg, egm, egl = m["t"], m["decay"], m["eg"], m["egm"], m["egl"]
    r = m["vb"] - _mm(m["kbg"], s, NN)
    du = _mm(m["attn"], do, TN) + _mm(m["kd"], dsp, NN)
    dqd = _mm(do, s, NT)
    ds = dsp * egl + _mm(m["qd"], do, TN)
    degl = jnp.sum(jnp.sum(dsp * s, axis=1, keepdims=True), axis=0, keepdims=True)
    yield
    u = _mmx(t, r, NN)
    dr = _mmx(t, du, TN)
    yield
    dattn = jnp.where(m["causal"], _mm(do, u, NT), 0.0)
    dkd = _mm(u, dsp, NT)
    da = jnp.where(m["strict"], -_mm(dr, u, NT), 0.0)
    dkbg = -_mm(dr, s, NT)
    ds = ds - _mm(m["kbg"], dr, TN)
    yield
    dkk = da * decay
    dqk = dattn * decay
    ddiff = (da * m["kk"] + dattn * m["qk"]) * decay
    dgc = _mmx(ddiff, m["ones"], NN) - _mmx(ddiff, m["ones"], TN)
    dkb = _mm(dkk, k, NN) + dkbg * eg
    dk = _mm(dkk, m["kb"], TN) + _mm(dqk, q, TN) + dkd * egm + dkb * bb
    dq = _mm(dqk, k, NN) + dqd * eg
    yield
    dgc = dgc + jnp.sum(dqd * q + dkbg * m["kb"], axis=-1, keepdims=True) * eg
    tt = jnp.sum(dkd * k, axis=-1, keepdims=True) * egm
    dgc = dgc - tt
    dgl = jnp.sum(tt, axis=0, keepdims=True) + degl * egl
    dbb = jnp.sum(dkb * k + dr * v, axis=-1, keepdims=True) + jnp.zeros((c, LANES), F32)
    dv = dr * bb
    dgb = _mmx(m["low"], dgc, TN) + dgl
    return dq, dk, dv, dgb, dbb, ds


def _dn_fwd_chunk(q, k, v, gb, bb, s):
    return _lockstep([_dn_fwd_stages(q, k, v, gb, bb, s)])[0]


def _dn_bwd_chunk(q, k, v, gb, bb, s, do, dsp):
    return _lockstep([_dn_bwd_stages(q, k, v, gb, bb, s, do, dsp)])[0]


def _dn_fwd_call(qkvn, gb, bb):
    _, nh, s, dh = qkvn.shape
    c = min(DN_CHUNK, s)
    n = s // c
    hb = nh

    def body(q_ref, k_ref, v_ref, g_ref, b_ref, o_ref, st_ref, s_ref):
        @pl.when(pl.program_id(1) == 0)
        def _():
            s_ref[...] = jnp.zeros_like(s_ref)

        st_ref[...] = s_ref[...]
        heads = [_dn_fwd_stages(q_ref[h], k_ref[h], v_ref[h], g_ref[h], b_ref[h], s_ref[h]) for h in range(hb)]
        for h, (o, s2) in enumerate(_lockstep(heads)):
            o_ref[h] = o
            s_ref[h] = s2

    def qspec(t):
        return pl.BlockSpec((None, hb, c, dh), lambda i, j, t=t: (t, i, j, 0))

    hs = pl.BlockSpec((hb, c, dh), lambda i, j: (i, j, 0))
    return pl.pallas_call(
        body, name="dn_fwd", grid=(nh // hb, n),
        in_specs=[qspec(0), qspec(1), qspec(2), hs, hs],
        out_specs=[hs, pl.BlockSpec((hb, None, dh, dh), lambda i, j: (i, j, 0, 0))],
        out_shape=[jax.ShapeDtypeStruct((nh, s, dh), F32), jax.ShapeDtypeStruct((nh, n, dh, dh), F32)],
        scratch_shapes=[pltpu.VMEM((hb, dh, dh), F32)],
        compiler_params=_params(("arbitrary", "arbitrary")),
    )(qkvn, qkvn, qkvn, gb, bb)


def _dn_bwd_call(qkvn, gb, bb, states, do):
    _, nh, s, dh = qkvn.shape
    c = min(DN_CHUNK, s)
    n = s // c
    hb = nh

    def body(q_ref, k_ref, v_ref, g_ref, b_ref, st_ref, do_ref, dqkv_ref, dg_ref, db_ref, ds_ref):
        @pl.when(pl.program_id(1) == 0)
        def _():
            ds_ref[...] = jnp.zeros_like(ds_ref)

        heads = [_dn_bwd_stages(q_ref[h], k_ref[h], v_ref[h], g_ref[h], b_ref[h], st_ref[h], do_ref[h], ds_ref[h])
                 for h in range(hb)]
        for h, (dq, dk, dv, dg, db, ds) in enumerate(_lockstep(heads)):
            dqkv_ref[0, h] = dq
            dqkv_ref[1, h] = dk
            dqkv_ref[2, h] = dv
            dg_ref[h] = dg
            db_ref[h] = db
            ds_ref[h] = ds

    def qspec(t):
        return pl.BlockSpec((None, hb, c, dh), lambda i, j, t=t: (t, i, n - 1 - j, 0))

    hs = pl.BlockSpec((hb, c, dh), lambda i, j: (i, n - 1 - j, 0))
    sh = jax.ShapeDtypeStruct((nh, s, dh), F32)
    return pl.pallas_call(
        body, name="dn_bwd", grid=(nh // hb, n),
        in_specs=[qspec(0), qspec(1), qspec(2), hs, hs,
                  pl.BlockSpec((hb, None, dh, dh), lambda i, j: (i, n - 1 - j, 0, 0)), hs],
        out_specs=[pl.BlockSpec((3, hb, c, dh), lambda i, j: (0, i, n - 1 - j, 0)), hs, hs],
        out_shape=[jax.ShapeDtypeStruct((3, nh, s, dh), F32), sh, sh],
        scratch_shapes=[pltpu.VMEM((hb, dh, dh), F32)],
        compiler_params=_params(("arbitrary", "arbitrary")),
    )(qkvn, qkvn, qkvn, gb, bb, states, do)


def _adamw(w, g, m, v, name):
    r, c = w.shape
    br = _div_tile(r, max(8, (1 << 18) // max(c, 1)), 8)

    def fn(w, g, m, v):
        m = ADAM_B1 * m + (1.0 - ADAM_B1) * g
        v = ADAM_B2 * v + (1.0 - ADAM_B2) * jnp.square(g)
        m_hat = m / (1.0 - ADAM_B1 ** ADAM_STEP)
        v_hat = v / (1.0 - ADAM_B2 ** ADAM_STEP)
        delta = -ADAM_LR * (m_hat / (jnp.sqrt(v_hat) + ADAM_EPS) + ADAM_WD * w)
        return delta, m, v

    spec = pl.BlockSpec((br, c), lambda i: (i, 0))
    return _ew(fn, name, (r // br,), [w, g, m, v], [spec] * 4, [((r, c), F32, spec)] * 3)


def _coords():
    return lax.axis_index("x"), lax.axis_index("y"), lax.axis_index("c")


def _all_gather_small(v):
    r, w = v.shape

    def body(v_ref, out_ref, send_sems, recv_sems, local_sem):
        x, y, c = _coords()
        me = 4 * x + 2 * y + c
        mine = pltpu.make_async_copy(v_ref, out_ref.at[me], local_sem)
        mine.start()
        peers = []
        for k in range(1, 8):
            px = 1 - x if k & 4 else x
            py = 1 - y if k & 2 else y
            pc = 1 - c if k & 1 else c
            peers.append((px, py, pc))
        sends = []
        for k, peer in enumerate(peers):
            cp = pltpu.make_async_remote_copy(src_ref=v_ref, dst_ref=out_ref.at[me], send_sem=send_sems.at[k],
                                              recv_sem=recv_sems.at[k], device_id=peer, device_id_type=MESH)
            cp.start()
            sends.append(cp)
        for k, (px, py, pc) in enumerate(peers):
            pltpu.make_async_remote_copy(src_ref=v_ref, dst_ref=out_ref.at[4 * px + 2 * py + pc],
                                         send_sem=send_sems.at[k], recv_sem=recv_sems.at[k],
                                         device_id=(px, py, pc), device_id_type=MESH).wait_recv()
        for cp in sends:
            cp.wait_send()
        mine.wait()

    return pl.pallas_call(
        body, name="ag_small", out_shape=jax.ShapeDtypeStruct((8, r, w), v.dtype),
        in_specs=[pl.BlockSpec(memory_space=pltpu.VMEM)], out_specs=pl.BlockSpec(memory_space=pltpu.VMEM),
        scratch_shapes=[pltpu.SemaphoreType.DMA((7,)), pltpu.SemaphoreType.DMA((7,)), pltpu.SemaphoreType.DMA],
        compiler_params=pltpu.CompilerParams(vmem_limit_bytes=VMEM_LIMIT),
    )(v)


def _hbm_call(body, name, arrays, out_shapes, n_sems, aliases=None):
    hbm = pl.BlockSpec(memory_space=pltpu.HBM)
    return pl.pallas_call(
        body, name=name, out_shape=list(out_shapes), in_specs=[hbm] * len(arrays), out_specs=[hbm] * len(out_shapes),
        input_output_aliases=aliases or {},
        scratch_shapes=[pltpu.SemaphoreType.DMA((n_sems,)), pltpu.SemaphoreType.DMA((n_sems,))],
    )(*arrays)


def _half_rows(ref_shape, c):
    h = ref_shape[1] // 2
    return pl.ds(pl.multiple_of(c * h, 16), h), pl.ds(pl.multiple_of((1 - c) * h, 16), h)


def _pair_copies(p_refs, land_refs, send_sems, recv_sems):
    x, y, c = _coords()
    cps = []
    for w, (p, land) in enumerate(zip(p_refs, land_refs)):
        _, other = _half_rows(p.shape, c)
        cps.append(pltpu.make_async_remote_copy(src_ref=p.at[:, other], dst_ref=land, send_sem=send_sems.at[w],
                                                recv_sem=recv_sems.at[w], device_id=(x, y, 1 - c),
                                                device_id_type=MESH))
    return cps


def _pair_gather(gs, name):
    n = len(gs)

    def body(*refs):
        ins, outs, send_sems, recv_sems = refs[:n], refs[n:2 * n], refs[2 * n], refs[2 * n + 1]
        x, y, c = _coords()
        cps = []
        for w in range(n):
            cp = pltpu.make_async_remote_copy(src_ref=ins[w].at[c], dst_ref=outs[w].at[c], send_sem=send_sems.at[w],
                                              recv_sem=recv_sems.at[w], device_id=(x, y, 1 - c), device_id_type=MESH)
            cp.start()
            cps.append(cp)
        for w, cp in enumerate(cps):
            pltpu.make_async_remote_copy(src_ref=ins[w].at[c], dst_ref=outs[w].at[1 - c], send_sem=send_sems.at[w],
                                         recv_sem=recv_sems.at[w], device_id=(x, y, 1 - c),
                                         device_id_type=MESH).wait_recv()
            cp.wait_send()

    shapes = [jax.ShapeDtypeStruct(g.shape, g.dtype) for g in gs]
    return _hbm_call(body, name, gs, shapes, n, aliases={i: i for i in range(n)})


_HBM = pl.BlockSpec(memory_space=pltpu.HBM)
_SEM = pl.BlockSpec(memory_space=pltpu.SEMAPHORE)
_EFFECT = pltpu.SideEffectType.DATAFLOW_SIDE_EFFECTING


def _split_start(name, srcs, lands, after, n_copies, make_copies):
    n, m = len(srcs), len(lands)
    arrays = [pltpu.with_memory_space_constraint(a, pltpu.HBM) for a in list(srcs) + list(lands)]

    def body(*refs):
        src_refs, land_refs = refs[:n], refs[n:n + m]
        send_sems, recv_sems = refs[n + m + 1], refs[n + m + 2]
        for cp in make_copies(src_refs, land_refs, send_sems, recv_sems):
            cp.start()
        refs[-1][...] = jnp.zeros_like(refs[-1])

    outs = pl.pallas_call(
        body, name=name,
        out_shape=(pltpu.SemaphoreType.DMA((n_copies,)), pltpu.SemaphoreType.DMA((n_copies,)),
                   *[pltpu.HBM(a.shape, a.dtype) for a in arrays], jax.ShapeDtypeStruct((8, LANES), F32)),
        in_specs=[_HBM] * (n + m) + [pl.BlockSpec(memory_space=pl.ANY)],
        out_specs=(_SEM, _SEM, *[_HBM] * (n + m), pl.BlockSpec(memory_space=pltpu.VMEM)),
        input_output_aliases={i: 2 + i for i in range(n + m)},
        compiler_params=pltpu.CompilerParams(has_side_effects=_EFFECT),
    )(*arrays, after)
    return (outs[0], outs[1], list(outs[2:2 + n]), list(outs[2 + n:2 + n + m])), outs[-1]


def _split_wait(name, state, after, make_copies):
    send_sems, recv_sems, srcs, lands = state
    n, m = len(srcs), len(lands)

    def body(*refs):
        src_refs, land_refs = refs[:n], refs[n:n + m]
        for cp in make_copies(src_refs, land_refs, refs[n + m], refs[n + m + 1]):
            cp.wait_send()
            cp.wait_recv()

    outs = pl.pallas_call(
        body, name=name, out_shape=tuple(pltpu.HBM(a.shape, a.dtype) for a in srcs + lands),
        in_specs=[_HBM] * (n + m) + [_SEM, _SEM, pl.BlockSpec(memory_space=pl.ANY)], out_specs=tuple([_HBM] * (n + m)),
        input_output_aliases={i: i for i in range(n + m)},
        compiler_params=pltpu.CompilerParams(has_side_effects=_EFFECT),
    )(*srcs, *lands, send_sems, recv_sems, after)
    return list(outs[:n]), list(outs[n:])


def _scatter_copies(q_refs, land_refs, send_sems, recv_sems):
    x, y, c = _coords()
    cps = []
    for w, (q, land) in enumerate(zip(q_refs, land_refs)):
        for k, (px, py) in enumerate([(1 - x, y), (x, 1 - y), (1 - x, 1 - y)]):
            cps.append(pltpu.make_async_remote_copy(src_ref=q.at[2 * px + py], dst_ref=land.at[k],
                                                    send_sem=send_sems.at[3 * w + k], recv_sem=recv_sems.at[3 * w + k],
                                                    device_id=(px, py, c), device_id_type=MESH))
    return cps


def _chip_scatter_start(qs, after, name):
    lands = [lax.empty((3,) + q.shape[1:], q.dtype) for q in qs]
    return _split_start(name, qs, lands, after, 3 * len(qs), _scatter_copies)


def _chip_scatter_wait(state, after, name):
    return _split_wait(name, state, after, _scatter_copies)


def _gather_copies(src_refs, buf_refs, send_sems, recv_sems):
    x, y, c = _coords()
    j = 2 * x + y
    cps = []
    for w, buf in enumerate(buf_refs):
        mine, _ = _half_rows(buf.shape, c)
        for k, (px, py) in enumerate([(1 - x, y), (x, 1 - y), (1 - x, 1 - y)]):
            cps.append(pltpu.make_async_remote_copy(src_ref=buf.at[j, mine], dst_ref=buf.at[j, mine],
                                                    send_sem=send_sems.at[3 * w + k], recv_sem=recv_sems.at[3 * w + k],
                                                    device_id=(px, py, c), device_id_type=MESH))
    return cps


def _gather_wait_copies(src_refs, buf_refs, send_sems, recv_sems):
    x, y, c = _coords()
    j = 2 * x + y
    cps = []
    for w, buf in enumerate(buf_refs):
        mine, _ = _half_rows(buf.shape, c)
        for k, (px, py) in enumerate([(1 - x, y), (x, 1 - y), (1 - x, 1 - y)]):
            cps.append(pltpu.make_async_remote_copy(src_ref=buf.at[j, mine], dst_ref=buf.at[2 * px + py, mine],
                                                    send_sem=send_sems.at[3 * w + k], recv_sem=recv_sems.at[3 * w + k],
                                                    device_id=(px, py, c), device_id_type=MESH))
    return cps


def _pair_forward(bufs, name):
    n = len(bufs)

    def body(*refs):
        ins, outs, send_sems, recv_sems = refs[:n], refs[n:2 * n], refs[2 * n], refs[2 * n + 1]
        x, y, c = _coords()
        chips = [(1 - x, y), (x, 1 - y), (1 - x, 1 - y)]
        cps = []
        for w in range(n):
            mine, _ = _half_rows(outs[w].shape, c)
            for k, (px, py) in enumerate(chips):
                cp = pltpu.make_async_remote_copy(src_ref=ins[w].at[2 * px + py, mine],
                                                  dst_ref=outs[w].at[2 * px + py, mine],
                                                  send_sem=send_sems.at[3 * w + k], recv_sem=recv_sems.at[3 * w + k],
                                                  device_id=(x, y, 1 - c), device_id_type=MESH)
                cp.start()
                cps.append(cp)
        for w in range(n):
            _, sib = _half_rows(outs[w].shape, c)
            for k, (px, py) in enumerate(chips):
                pltpu.make_async_remote_copy(src_ref=ins[w].at[2 * px + py, sib], dst_ref=outs[w].at[2 * px + py, sib],
                                             send_sem=send_sems.at[3 * w + k], recv_sem=recv_sems.at[3 * w + k],
                                             device_id=(x, y, 1 - c), device_id_type=MESH).wait_recv()
        for cp in cps:
            cp.wait_send()

    shapes = [jax.ShapeDtypeStruct(b.shape, b.dtype) for b in bufs]
    return _hbm_call(body, name, bufs, shapes, 3 * n, aliases={i: i for i in range(n)})


def _slot_rows(h, cs):
    return _div_tile(h, max(16, (1 << 19) // cs), 16)


def _cast_into_slot(w, slot, name):
    r, cs = w.shape
    br = _slot_rows(r, cs)
    return _ew_slot(lambda a: a, name, (r // br,), slot, [w], [pl.BlockSpec((br, cs), lambda i, s: (i, 0))],
                    (4, r, cs), BF16, pl.BlockSpec((None, br, cs), lambda i, s: (s[0], i, 0)))


def _pair_add(p, rb, core, name):
    n, r, cs = p.shape
    h = r // 2
    br = _slot_rows(h, cs)
    nb = h // br
    return _ew_slot(lambda a, b: a + b, name, (n, nb), core, [p, rb],
                    [pl.BlockSpec((None, br, cs), lambda s, i, c: (s, c[0] * nb + i, 0)),
                     pl.BlockSpec((None, br, cs), lambda s, i, c: (s, i, 0))],
                    (n, h, cs), BF16, pl.BlockSpec((None, br, cs), lambda s, i, c: (s, i, 0)))


def _chip_sum(q, r3, slots, name):
    _, h, cs = q.shape
    br = _slot_rows(h, cs)

    def fn(a, b):
        acc = a.astype(F32)
        for k in range(3):
            acc = acc + b[k].astype(F32)
        return acc

    return _ew_slot(fn, name, (h // br,), slots, [q, r3],
                    [pl.BlockSpec((None, br, cs), lambda i, s: (s[0], i, 0)),
                     pl.BlockSpec((3, br, cs), lambda i, s: (0, i, 0))],
                    (2, h, cs), F32, pl.BlockSpec((None, br, cs), lambda i, s: (s[1], i, 0)))


def _sum_slots(r, name):
    n, h, w = r.shape
    br = _div_tile(h, 2048, 16)

    def fn(blk):
        acc = blk[0].astype(F32)
        for s in range(1, n):
            acc = acc + blk[s].astype(F32)
        return (acc,)

    return _ew(fn, name, (h // br,), [r], [pl.BlockSpec((n, br, w), lambda i: (0, i, 0))],
               [((h, w), F32, pl.BlockSpec((br, w), lambda i: (i, 0)))])[0]


def _pack_small(arrs):
    flat = jnp.concatenate([a.reshape(-1).astype(F32) for a in arrs])
    n = flat.shape[0]
    rows = -(-n // LANES)
    rows = -(-rows // 8) * 8
    return jnp.pad(flat, (0, rows * LANES - n)).reshape(rows, LANES)


def _unpack_small(p, shapes):
    lead = p.shape[:-2]
    flat = p.reshape(lead + (-1,))
    out, off = [], 0
    for sh in shapes:
        n = 1
        for d in sh:
            n *= d
        out.append(flat[..., off:off + n].reshape(lead + tuple(sh)))
        off += n
    return out


def kernel(x, c, w_ada, b_ada, norm1_g, w_in, dn_conv_w, dn_a_log, dn_dt_bias, dn_norm_g, dn_w_o, cf_conv_w, cf_ln_g, cf_ln_b, cf_w_o, w_out, norm2_g, ffn_w_up, ffn_conv_w, ffn_w_down, final_norm_g, loss_target, m_w_ada, m_b_ada, m_norm1_g, m_w_in, m_dn_conv_w, m_dn_a_log, m_dn_dt_bias, m_dn_norm_g, m_dn_w_o, m_cf_conv_w, m_cf_ln_g, m_cf_ln_b, m_cf_w_o, m_w_out, m_norm2_g, m_ffn_w_up, m_ffn_conv_w, m_ffn_w_down, m_final_norm_g, v_w_ada, v_b_ada, v_norm1_g, v_w_in, v_dn_conv_w, v_dn_a_log, v_dn_dt_bias, v_dn_norm_g, v_dn_w_o, v_cf_conv_w, v_cf_ln_g, v_cf_ln_b, v_cf_w_o, v_w_out, v_norm2_g, v_ffn_w_up, v_ffn_conv_w, v_ffn_w_down, v_final_norm_g):
    xi, yi, ci = _coords()
    chip = 2 * xi + yi
    me = 4 * xi + 2 * yi + ci
    core = jnp.reshape(ci, (1,)).astype(jnp.int32)

    S, D = x.shape[1], x.shape[2]
    NH = dn_a_log.shape[1]
    DH = dn_norm_g.shape[1]
    DNW = NH * DH
    CFW = cf_ln_g.shape[1]
    FFN = ffn_w_down.shape[1] * 4
    KDN, KCF, KFF = dn_conv_w.shape[1], cf_conv_w.shape[1], ffn_conv_w.shape[1]
    NIN = w_in.shape[2] * 4
    assert NIN == 4 * DNW + 2 * NH + 2 * CFW + 2 * D and DH == LANES and 2 * NH <= LANES
    x2, tgt = x[0], loss_target[0]

    sm_shapes = [(D,), (KDN, 3 * DNW // 4), (KCF, CFW // 4), (KFF, FFN // 4)]
    g1 = _all_gather_small(_pack_small([c[0], dn_conv_w[0], cf_conv_w[0], ffn_conv_w[0]]))
    c_all, dcw_s, ccw_s, fcw_s = _unpack_small(g1, sm_shapes)

    def chips_cols(t):
        t = t[0::2]
        return jnp.transpose(t, (1, 0, 2)).reshape(t.shape[1], -1)

    dn_cw, cf_cw, ff_cw = chips_cols(dcw_s), chips_cols(ccw_s), chips_cols(fcw_s)

    slot_chip = jnp.reshape(chip, (1,)).astype(jnp.int32)
    big = [w_in[0], dn_w_o[0], cf_w_o[0], w_out[0], ffn_w_up[0], ffn_w_down[0]]
    names_big = ["w_in", "dn_w_o", "cf_w_o", "w_out", "ffn_w_up", "ffn_w_down"]
    in_state, in_token = _split_start("ag_in_start", [], [_cast_into_slot(big[0], slot_chip, "cast_w_in")], c, 3,
                                      _gather_copies)
    bufs = [_cast_into_slot(w, slot_chip, "cast_" + nm) for w, nm in zip(big[1:], names_big[1:])]
    mid_state, mid_token = _split_start("ag_mid_start", [], bufs[:3], in_token, 9, _gather_copies)
    ag_state, ag_token = _split_start("ag_late_start", [], bufs[3:], mid_token, 6, _gather_copies)

    cb_cf = _div_tile(CFW, 256, LANES)
    o_b = 4 * DNW
    o_glu = o_b + 2 * NH
    o_ga = o_glu + 2 * CFW
    NA = NIN - 2 * NH
    a_z, a_ga, a_gb, a_glu, a_ba = 3 * DNW, 4 * DNW, 4 * DNW + D, 4 * DNW + 2 * D, NA

    segs = [(0, 0, o_b), (o_b, NA, 2 * NH), (o_ga, a_ga, 2 * D)]
    for t in range(CFW // cb_cf):
        segs += [(o_glu + t * cb_cf, a_glu + 2 * t * cb_cf, cb_cf),
                 (o_glu + CFW + t * cb_cf, a_glu + (2 * t + 1) * cb_cf, cb_cf)]
    CS = NIN // 4

    def shard_slices(lo, n):
        out = []
        while n > 0:
            j, off = lo // CS, lo % CS
            m = min(n, CS - off)
            out.append(w_in_g[j][:, off:off + m])
            lo, n = lo + m, n - m
        return out

    CA = w_ada.shape[2]
    b_sh = lax.dynamic_slice(b_ada, (0, chip * CA), (1, CA))
    tn_a = _div_tile(CA, 512, LANES)

    def mod_fn(cc, w, b):
        return (_mm(_silu(cc), w, NN) + b,)

    mod_sh = _ew(mod_fn, "ada_mod", (CA // tn_a,), [c_all, w_ada[0], b_sh],
                 [_full((8, D)), pl.BlockSpec((D, tn_a), lambda j: (0, j)), pl.BlockSpec((1, tn_a), lambda j: (0, j))],
                 [((8, CA), F32, pl.BlockSpec((8, tn_a), lambda j: (0, j)))])[0]
    g2 = _all_gather_small(_pack_small([mod_sh]))
    mod_all = _unpack_small(g2, [(8, CA)])[0][0::2]
    mod_all = jnp.transpose(mod_all, (1, 0, 2)).reshape(8, 4 * CA)
    mod_me = lax.dynamic_slice(mod_all, (me, 0), (1, 6 * D))
    sh1, sc1, gt1, sh2, sc2, gt2 = [mod_me[:, i * D:(i + 1) * D] for i in range(6)]

    bs = _div_tile(S, 128, 8)
    nb = S // bs
    vecD = _full((1, D))
    rowD = _row(bs, D)

    hn1 = _ew(lambda a, g, sc, sh: (_f_normmod(a, g, sc, sh),), "norm1_fwd", (nb,),
              [x2, norm1_g, sc1, sh1], [rowD, vecD, vecD, vecD], [((S, D), BF16, rowD)],
              after=[ag_token, m_w_in[0], v_w_in[0]])[0]
    _, in_landed = _split_wait("ag_in_wait", in_state, hn1, _gather_wait_copies)
    (w_in_g,) = _pair_forward(in_landed, "ag_in_pair")
    w_aug = jnp.concatenate([p for o, a, n in sorted(segs, key=lambda s: s[1]) for p in shard_slices(o, n)]
                            + [jnp.zeros((D, LANES - 2 * NH), BF16)], axis=1)
    proj = _matmul(hn1, w_aug, "nn", F32, "mm_in")

    def dn_post(j, cv):
        s = _silu(cv)
        nrm = s * lax.rsqrt(jnp.sum(s * s, axis=-1, keepdims=True) + EPS)
        fq = (j < NH).astype(F32)
        fk = (j < 2 * NH).astype(F32)
        scale = fq * (DH ** -0.5) + (1.0 - fq)
        return fk * (nrm * scale) + (1.0 - fk) * s

    def ident(a):
        return a

    def colS(w, off=0):
        return pl.BlockSpec((S, w), lambda j, off=off: (0, j + off))

    def wS(kw, w):
        return pl.BlockSpec((kw, w), lambda j: (0, j))

    qkv_spec = pl.BlockSpec((None, None, S, DH), lambda j: (j // NH, j % NH, 0, 0))
    dn_args = dict(kw=KDN, ncol=3 * NH, ins=[proj], in_specs=[colS(DH)], extras=[], extra_specs=[],
                   w=dn_cw, w_spec=wS(KDN, DH), pre=ident, post=dn_post)
    qkvn = _conv_fwd_call("dn_conv_fwd", out_shape=jax.ShapeDtypeStruct((3, NH, S, DH), F32), out_spec=qkv_spec,
                          **dn_args)

    alp = jnp.pad(dn_a_log, ((0, 0), (NH, LANES - 2 * NH)))
    dtb = jnp.pad(dn_dt_bias, ((0, 0), (NH, LANES - 2 * NH)))
    vecL = _full((1, LANES))
    ba_spec = _row(bs, LANES, a_ba // LANES)
    rowL = _row(bs, LANES)
    gate_fn = functools.partial(_f_dn_gate, NH)
    def gate_fwd(a, p, q):
        val = gate_fn(a, p, q)
        lane = lax.broadcasted_iota(jnp.int32, val.shape, 1)

        def spread(col):
            return jnp.sum(jnp.where(lane == col, val, 0.0), axis=-1, keepdims=True) + jnp.zeros(val.shape, F32)

        return tuple(spread(h) for h in range(NH)), tuple(spread(NH + h) for h in range(NH))

    hrow = pl.BlockSpec((NH, bs, DH), lambda i: (0, i, 0))
    bb_b, gb_b = _ew(gate_fwd, "dn_gate_fwd", (nb,), [proj, alp, dtb], [ba_spec, vecL, vecL],
                     [((NH, S, DH), F32, hrow), ((NH, S, DH), F32, hrow)])
    o_dn, states = _dn_fwd_call(qkvn, gb_b, bb_b)

    bsh = _div_tile(S, 512, 8)
    nbh = S // bsh
    o_spec = pl.BlockSpec((None, bsh, DH), lambda i, h: (h, i, 0))
    z_spec = pl.BlockSpec((bsh, DH), lambda i, h: (i, a_z // DH + h))
    oh_spec = pl.BlockSpec((bsh, DH), lambda i, h: (i, h))
    ng_spec = pl.BlockSpec((1, DH), lambda i, h: (0, 0))
    on = _ew(lambda o, z, g: (_f_dn_post(o, z, g),), "dn_post_fwd", (nbh, NH), [o_dn, proj, dn_norm_g],
             [o_spec, z_spec, ng_spec], [((S, DNW), BF16, oh_spec)])[0]
    _, landed = _split_wait("ag_mid_wait", mid_state, on, _gather_wait_copies)
    w_do_f, w_co_f, w_out_g = _pair_forward(landed, "ag_mid_pair")
    w_out_f = w_out_g.reshape(-1, w_out_g.shape[2])
    br_a = _matmul(on, w_do_f, "nn", F32, "mm_dn_o")

    def glu_pre(val, gl):
        return val * _sigmoid(gl)

    def glu_spec(t):
        return pl.BlockSpec((S, cb_cf), lambda j, t=t: (0, a_glu // cb_cf + 2 * j + t))

    cf_args = dict(kw=KCF, ncol=CFW // cb_cf, ins=[proj, proj], in_specs=[glu_spec(0), glu_spec(1)],
                   extras=[], extra_specs=[], w=cf_cw, w_spec=wS(KCF, cb_cf), pre=glu_pre, post=lambda j, cv: cv)
    uc = _conv_fwd_call("cf_conv_fwd", out_shape=jax.ShapeDtypeStruct((S, CFW), F32), out_spec=colS(cb_cf), **cf_args)
    rowC = _row(bs, CFW)
    vecC = _full((1, CFW))
    ub = _ew(lambda u, g, b: (_f_cf_ln(u, g, b),), "cf_ln_fwd", (nb,), [uc, cf_ln_g, cf_ln_b], [rowC, vecC, vecC],
             [((S, CFW), BF16, rowC)])[0]
    br_b = _matmul(ub, w_co_f, "nn", F32, "mm_cf_o")

    ga_spec, gb_spec = _row(bs, D, a_ga // D), _row(bs, D, a_gb // D)
    merged = _ew(lambda a, b, ga, gb: (_f_merge(a, b, ga, gb),), "merge_fwd", (nb,), [br_a, br_b, proj, proj],
                 [rowD, rowD, ga_spec, gb_spec], [((S, D), BF16, rowD)])[0]
    mix = _matmul(merged, w_out_f, "nn", F32, "mm_out")

    x1, hn2 = _ew(_f_res_normmod, "norm2_fwd", (nb,), [x2, mix, gt1, norm2_g, sc2, sh2],
                  [rowD, rowD, vecD, vecD, vecD, vecD], [((S, D), F32, rowD), ((S, D), BF16, rowD)])
    _, landed = _split_wait("ag_late_wait", ag_state, hn2, _gather_wait_copies)
    w_up_f, w_dn_g = _pair_forward(landed, "ag_late_pair")
    w_dn_f = w_dn_g.reshape(-1, w_dn_g.shape[2])
    up_all = _matmul(hn2, w_up_f, "nn", F32, "mm_up")

    cb_ff = _div_tile(FFN, 256, LANES)
    ff_args = dict(kw=KFF, ncol=FFN // cb_ff, ins=[up_all], in_specs=[colS(cb_ff)], extras=[up_all],
                   extra_specs=[colS(cb_ff, FFN // cb_ff)], w=ff_cw, w_spec=wS(KFF, cb_ff),
                   pre=ident, post=lambda j, cv, up: _silu(cv) * up)
    hff = _conv_fwd_call("ffn_conv_fwd", out_shape=jax.ShapeDtypeStruct((S, FFN), BF16), out_spec=colS(cb_ff), **ff_args)
    ffo = _matmul(hff, w_dn_f, "nn", F32, "mm_down")

    gf2 = final_norm_g.reshape(1, D)

    def loss_bwd(a, f, gt, gf, t):
        val, vjp = jax.vjp(_f_loss, a, f, gt, gf, t)
        da, df, dgt, dgf, _ = vjp(jnp.ones((), F32))
        return da, df, dgt, dgf, jnp.zeros((1, LANES), F32) + val

    dx1_l, dffo, dgt2, dgf, loss_v = _ew(
        loss_bwd, "loss_bwd", (nb,), [x1, ffo, gt2, gf2, tgt], [rowD, rowD, vecD, vecD, rowD],
        [((S, D), F32, rowD), ((S, D), BF16, rowD), ((1, D), F32, vecD), ((1, D), F32, vecD),
         ((1, LANES), F32, vecL)], acc=(2, 3, 4))

    dhff = _matmul(dffo, w_dn_f, "nt", F32, "mm_down_dx")
    g_w_dn = _matmul(hff, dffo, "tn", F32, "mm_down_dw")

    slots = jnp.stack([chip, ci]).astype(jnp.int32)
    rs_groups = []

    def rs_pair_begin(parts, tag):
        lands = [lax.empty((4, p.shape[1] // 2, p.shape[2]), F32) for p in parts]
        return _split_start("rs_pair_start_" + tag, parts, lands, core, len(parts), _pair_copies)

    def rs_chips_begin(pair_state, nms, tag, after):
        parts, rbs = _split_wait("rs_pair_wait_" + tag, pair_state, after, _pair_copies)
        q16 = [_pair_add(p, rb, core, "rs_pair_add_" + nm) for p, rb, nm in zip(parts, rbs, nms)]
        state, token = _chip_scatter_start(q16, core, "rs_chips_start_" + tag)
        rs_groups.append((state, nms, tag))
        return token

    pair_a, ptok_a = rs_pair_begin([g_w_dn.reshape(4, FFN // 4, D)], "a")

    d_upall, g_ffcw = _conv_bwd_call(
        "ffn_conv_bwd", dout=dhff, dout_spec=colS(cb_ff),
        dio_shapes=[jax.ShapeDtypeStruct((2, S, FFN), BF16)],
        dio_specs=[pl.BlockSpec((2, S, cb_ff), lambda j: (0, 0, j))], dio_pack=lambda dg, du: [(dg, du)],
        dw_shape=jax.ShapeDtypeStruct((KFF, FFN), F32), dw_spec=wS(KFF, cb_ff), after=[ptok_a], **ff_args)
    tok_a = rs_chips_begin(pair_a, ["ffn_w_down"], "a", d_upall)
    dhn2 = _matmul(d_upall, w_up_f, "nt", F32, "mm_up_dx", after=[tok_a])
    g_w_up = _matmul(hn2, d_upall, "tn", F32, "mm_up_dw", out_groups=4)
    pair_b, ptok_b = rs_pair_begin([g_w_up], "b")

    def res2_bwd(a, mx, gt, g, sc, sh, dx1, dhn):
        _, vjp = jax.vjp(_f_res_normmod, a, mx, gt, g, sc, sh)
        return vjp((dx1, dhn))

    dx_r, dmix, dgt1, dg2, dsc2, dsh2 = _ew(
        res2_bwd, "norm2_bwd", (nb,), [x2, mix, gt1, norm2_g, sc2, sh2, dx1_l, dhn2],
        [rowD, rowD, vecD, vecD, vecD, vecD, rowD, rowD],
        [((S, D), F32, rowD), ((S, D), BF16, rowD)] + [((1, D), F32, vecD)] * 4, acc=(2, 3, 4, 5), after=[ptok_b])

    dmerged = _matmul(dmix, w_out_f, "nt", F32, "mm_out_dx")
    g_w_out = _matmul(merged, dmix, "tn", F32, "mm_out_dw")

    def merge_bwd(a, b, ga, gb, dm):
        _, vjp = jax.vjp(_f_merge, a, b, ga, gb)
        da, db, dga, dgb = vjp(dm)
        return da, db, jnp.concatenate([dga, dgb], axis=1)

    d_bra, d_brb, dproj = _ew(merge_bwd, "merge_bwd", (nb,), [br_a, br_b, proj, proj, dmerged],
                              [rowD, rowD, ga_spec, gb_spec, rowD],
                              [((S, D), BF16, rowD), ((S, D), BF16, rowD),
                               ((S, NA + LANES), BF16, _row(bs, 2 * D, a_ga // (2 * D)))])

    tok_b = rs_chips_begin(pair_b, ["ffn_w_up"], "b", d_brb)
    d_on = _matmul(d_bra, w_do_f, "nt", F32, "mm_dn_o_dx", after=[tok_b])
    g_w_do = _matmul(on, d_bra, "tn", F32, "mm_dn_o_dw", out_groups=4)
    d_ub = _matmul(d_brb, w_co_f, "nt", F32, "mm_cf_o_dx")
    g_w_co = _matmul(ub, d_brb, "tn", F32, "mm_cf_o_dw", out_groups=4)
    pair_c, ptok_c = rs_pair_begin([g_w_do, g_w_co, g_w_out.reshape(4, D // 4, D)], "c")

    def cf_ln_bwd(u, g, b, du):
        _, vjp = jax.vjp(_f_cf_ln, u, g, b)
        return vjp(du)

    d_uc, g_cflg, g_cflb = _ew(cf_ln_bwd, "cf_ln_bwd", (nb,), [uc, cf_ln_g, cf_ln_b, d_ub], [rowC, vecC, vecC, rowC],
                               [((S, CFW), F32, rowC), ((1, CFW), F32, vecC), ((1, CFW), F32, vecC)], acc=(1, 2),
                               after=[ptok_c])
    dproj_sds = jax.ShapeDtypeStruct((S, NA + LANES), BF16)
    dproj, g_cfcw = _conv_bwd_call(
        "cf_conv_bwd", dout=d_uc, dout_spec=colS(cb_cf), dio_shapes=[dproj_sds],
        dio_specs=[colS(2 * cb_cf, a_glu // (2 * cb_cf))], dio_pack=lambda dv, dg: [jnp.concatenate([dv, dg], axis=1)],
        dw_shape=jax.ShapeDtypeStruct((KCF, CFW), F32), dw_spec=wS(KCF, cb_cf), alias=dproj, **cf_args)

    def dn_post_bwd(o, z, g, d):
        _, vjp = jax.vjp(_f_dn_post, o, z, g)
        return vjp(d)

    tok_c = rs_chips_begin(pair_c, ["dn_w_o", "cf_w_o", "w_out"], "c", g_cfcw)
    d_o, dproj, g_dnng = _ew(dn_post_bwd, "dn_post_bwd", (nbh, NH), [o_dn, proj, dn_norm_g, d_on],
                             [o_spec, z_spec, ng_spec, oh_spec],
                             [((NH, S, DH), F32, o_spec), ((S, NA + LANES), BF16, z_spec), ((1, DH), F32, ng_spec)],
                             acc=(2,), alias=(dproj, 1), after=[tok_c])

    dqkvn, dgb_b, dbb_b = _dn_bwd_call(qkvn, gb_b, bb_b, states, d_o)
    dproj, g_dncw = _conv_bwd_call(
        "dn_conv_bwd", dout=dqkvn, dout_spec=qkv_spec, dio_shapes=[dproj_sds], dio_specs=[colS(DH)],
        dio_pack=lambda d: [d], dw_shape=jax.ShapeDtypeStruct((KDN, 3 * DNW), F32), dw_spec=wS(KDN, DH),
        alias=dproj, **dn_args)

    def gate_bwd(a, p, q, db, dg):
        lane = lax.broadcasted_iota(jnp.int32, a.shape, 1)
        d = jnp.zeros(a.shape, F32)
        for h in range(NH):
            d = d + jnp.where(lane == h, db[h], 0.0) + jnp.where(lane == NH + h, dg[h], 0.0)
        _, vjp = jax.vjp(gate_fn, a, p, q)
        return vjp(d)

    dproj, g_alp, g_dtb = _ew(gate_bwd, "dn_gate_bwd", (nb,), [proj, alp, dtb, dbb_b, dgb_b],
                              [ba_spec, vecL, vecL, hrow, hrow],
                              [((S, NA + LANES), BF16, ba_spec), ((1, LANES), F32, vecL), ((1, LANES), F32, vecL)],
                              acc=(1, 2), alias=(dproj, 0))

    g_w_aug = _matmul(hn1, dproj, "tn", F32, "mm_in_dw")
    def aug_slices(lo, n):
        out = []
        for o, a, m in sorted(segs):
            s, e = max(lo, o), min(lo + n, o + m)
            if s < e:
                out.append(g_w_aug[:, a + s - o:a + e - o])
        return out

    g_w_in = jnp.stack([jnp.concatenate(aug_slices(j * CS, CS), axis=1) for j in range(4)])
    pair_d, ptok_d = rs_pair_begin([g_w_in], "d")
    dhn1 = _matmul(dproj, w_aug, "nt", F32, "mm_in_dx", after=[ptok_d])

    def norm1_bwd(a, g, sc, sh, dhn, dxr):
        _, vjp = jax.vjp(_f_normmod, a, g, sc, sh)
        da, dg, dsc, dsh = vjp(dhn)
        return da + dxr, dg, dsc, dsh

    grad_x, dg1, dsc1, dsh1 = _ew(norm1_bwd, "norm1_bwd", (nb,), [x2, norm1_g, sc1, sh1, dhn1, dx_r],
                                  [rowD, vecD, vecD, vecD, rowD, rowD],
                                  [((S, D), F32, rowD)] + [((1, D), F32, vecD)] * 3, acc=(1, 2, 3))

    dmod = jnp.concatenate([dsh1, dsc1, dgt1, dsh2, dsc2, dgt2], axis=1)
    sm2 = [dmod, dg1, dg2, dgf, g_alp, g_dtb, g_dnng, g_cflg, g_cflb, g_dncw, g_cfcw, g_ffcw]
    sm2_shapes = [tuple(a.shape) for a in sm2]
    g3 = _all_gather_small(_pack_small(sm2))
    tok_d = rs_chips_begin(pair_d, ["w_in"], "d", g3)

    names = ["w_ada", "b_ada", "norm1_g", "w_in", "dn_conv_w", "dn_a_log", "dn_dt_bias", "dn_norm_g", "dn_w_o",
             "cf_conv_w", "cf_ln_g", "cf_ln_b", "cf_w_o", "w_out", "norm2_g", "ffn_w_up", "ffn_conv_w", "ffn_w_down",
             "final_norm_g"]
    wts = dict(zip(names, [w_ada, b_ada, norm1_g, w_in, dn_conv_w, dn_a_log, dn_dt_bias, dn_norm_g, dn_w_o, cf_conv_w,
                           cf_ln_g, cf_ln_b, cf_w_o, w_out, norm2_g, ffn_w_up, ffn_conv_w, ffn_w_down, final_norm_g]))
    ms = dict(zip(names, [m_w_ada, m_b_ada, m_norm1_g, m_w_in, m_dn_conv_w, m_dn_a_log, m_dn_dt_bias, m_dn_norm_g,
                          m_dn_w_o, m_cf_conv_w, m_cf_ln_g, m_cf_ln_b, m_cf_w_o, m_w_out, m_norm2_g, m_ffn_w_up,
                          m_ffn_conv_w, m_ffn_w_down, m_final_norm_g]))
    vs = dict(zip(names, [v_w_ada, v_b_ada, v_norm1_g, v_w_in, v_dn_conv_w, v_dn_a_log, v_dn_dt_bias, v_dn_norm_g,
                          v_dn_w_o, v_cf_conv_w, v_cf_ln_g, v_cf_ln_b, v_cf_w_o, v_w_out, v_norm2_g, v_ffn_w_up,
                          v_ffn_conv_w, v_ffn_w_down, v_final_norm_g]))
    grads, delta, new_m, new_v = {}, {}, {}, {}

    def adam_large(n, g):
        grads[n] = g.reshape(wts[n].shape)
        d_, m_, v_ = _adamw(wts[n][0], grads[n][0], ms[n][0], vs[n][0], "adamw_" + n)
        delta[n], new_m[n], new_v[n] = d_[None], m_[None], v_[None]

    def rs_finish(groups, after, tag):
        hs, nms_all = [], []
        for state, nms, t in groups:
            q16, r3s = _chip_scatter_wait(state, after, "rs_chips_wait_" + t)
            for q, r3, nm in zip(q16, r3s, nms):
                hs.append(_chip_sum(q, r3, slots, "rs_chip_sum_" + nm))
                nms_all.append(nm)
        for nm, g in zip(nms_all, _pair_gather(hs, "rs_pair_gather_" + tag)):
            adam_large(nm, g)

    rs_finish(rs_groups[:3], tok_d, "abc")

    ssum = _sum_slots(g3, "small_sum")
    dmod_all = _unpack_small(g3, sm2_shapes[:1])[0].reshape(8, 6 * D)
    (g_b_ada, gs_n1, gs_n2, gs_fn, gs_alp, gs_dtb, gs_dnng, gs_cflg, gs_cflb, gs_dncw, gs_cfcw,
     gs_ffcw) = _unpack_small(ssum, sm2_shapes)
    gs_alog, gs_dtbias = gs_alp[:, NH:2 * NH], gs_dtb[:, NH:2 * NH]
    gs_dncw = lax.dynamic_slice(gs_dncw, (0, chip * (3 * DNW // 4)), (KDN, 3 * DNW // 4))
    gs_cfcw = lax.dynamic_slice(gs_cfcw, (0, chip * (CFW // 4)), (KCF, CFW // 4))
    gs_ffcw = lax.dynamic_slice(gs_ffcw, (0, chip * (FFN // 4)), (KFF, FFN // 4))

    dmod_sh = lax.dynamic_slice(dmod_all, (0, chip * CA), (8, CA))

    def wada_fn(cc, dm):
        return (_mm(_silu(cc), dm, TN),)

    g_w_ada = _ew(wada_fn, "ada_dw", (CA // tn_a,), [c_all, dmod_sh],
                  [_full((8, D)), pl.BlockSpec((8, tn_a), lambda j: (0, j))],
                  [((D, CA), F32, pl.BlockSpec((D, tn_a), lambda j: (0, j)))])[0]

    loss = lax.psum(loss_v[0, 0], ("x", "y", "c"))

    adam_large("w_ada", g_w_ada)
    small_grads = {"b_ada": g_b_ada, "norm1_g": gs_n1, "dn_conv_w": gs_dncw, "dn_a_log": gs_alog,
                   "dn_dt_bias": gs_dtbias, "dn_norm_g": gs_dnng, "cf_conv_w": gs_cfcw, "cf_ln_g": gs_cflg,
                   "cf_ln_b": gs_cflb, "norm2_g": gs_n2, "ffn_conv_w": gs_ffcw, "final_norm_g": gs_fn}
    small = [n for n in names if n in small_grads]
    for n in small:
        grads[n] = small_grads[n].reshape(wts[n].shape)
    sm_sh = [tuple(wts[n].shape) for n in small]
    d_, m_, v_ = _adamw(_pack_small([wts[n] for n in small]), _pack_small([grads[n] for n in small]),
                        _pack_small([ms[n] for n in small]), _pack_small([vs[n] for n in small]), "adamw_small")
    for n, a, b_, c_ in zip(small, _unpack_small(d_, sm_sh), _unpack_small(m_, sm_sh), _unpack_small(v_, sm_sh)):
        delta[n], new_m[n], new_v[n] = a, b_, c_
    rs_finish(rs_groups[3:], delta["w_ada"], "d")

    return (loss, grad_x[None], *[grads[n] for n in names], *[delta[n] for n in names],
            *[new_m[n] for n in names], *[new_v[n] for n in names])
```

```python
import functools

import jax
import jax.numpy as jnp
from jax import lax
from jax.experimental import pallas as pl
from jax.experimental.pallas import tpu as pltpu

F32 = jnp.float32
BF16 = jnp.bfloat16
EPS = 1e-6
LANES = 128
VMEM_LIMIT = 48 * 1024 * 1024
DN_CHUNK = 128
ADAM_LR, ADAM_B1, ADAM_B2, ADAM_EPS, ADAM_WD, ADAM_STEP = 0.001, 0.9, 0.999, 1e-08, 0.01, 10
MESH = pl.DeviceIdType.MESH

NN = (((1,), (0,)), ((), ()))
NT = (((1,), (1,)), ((), ()))
TN = (((0,), (0,)), ((), ()))
_DIMS = {"nn": NN, "nt": NT, "tn": TN}


def _mm(a, b, dims):
    return lax.dot_general(a.astype(BF16), b.astype(BF16), dims, preferred_element_type=F32)


def _mmx(a, b, dims):
    return lax.dot_general(a, b, dims, precision=lax.Precision.HIGH, preferred_element_type=F32)


def _div_tile(n, target, mult):
    best = None
    t = mult
    while t <= min(n, target):
        if n % t == 0:
            best = t
        t += mult
    return best if best is not None else n


def _params(sem=None):
    kw = dict(vmem_limit_bytes=VMEM_LIMIT)
    if sem is not None:
        kw["dimension_semantics"] = sem
    return pltpu.CompilerParams(**kw)


def _sigmoid(x):
    return jax.nn.sigmoid(x)


def _silu(x):
    return x * jax.nn.sigmoid(x)


def _softplus(x):
    return jnp.maximum(x, 0.0) + jnp.log(1.0 + jnp.exp(-jnp.abs(x)))


def _view(arr):
    if arr.ndim == 3:
        return arr.shape[1], arr.shape[0] * arr.shape[2], arr.shape[0]
    return arr.shape[0], arr.shape[1], 1


def _tile_spec(groups, cols, tr, tc, rsel, csel):
    if groups > 1:
        per = cols // groups // tc
        return pl.BlockSpec((None, tr, tc), lambda i, j, k: (csel(i, j, k) // per, rsel(i, j, k), csel(i, j, k) % per))
    return pl.BlockSpec((tr, tc), lambda i, j, k: (rsel(i, j, k), csel(i, j, k)))


def _matmul(a, b, mode, out_dtype, name, out_groups=1, after=()):
    ar, ac, ag = _view(a)
    br, bc, bg = _view(b)
    if mode == "nn":
        M, K, N = ar, ac, bc
        kdiv, mdiv, ndiv = ac // ag, M, min(bc // bg, N // out_groups)
    elif mode == "nt":
        M, K, N = ar, ac, br
        kdiv, mdiv, ndiv = min(ac // ag, bc // bg), M, N // out_groups
    else:
        K, M, N = ar, ac, bc
        kdiv, mdiv, ndiv = K, ac // ag, min(bc // bg, N // out_groups)
    tm = _div_tile(mdiv, 1024, LANES)
    tn = _div_tile(ndiv, 1536, LANES)
    tk = _div_tile(kdiv, 2048, LANES)
    nk = K // tk
    dims = _DIMS[mode]
    si, sj, sk = (lambda i, j, k: i), (lambda i, j, k: j), (lambda i, j, k: k)
    a_spec = {"nn": _tile_spec(ag, ac, tm, tk, si, sk), "nt": _tile_spec(ag, ac, tm, tk, si, sk),
              "tn": _tile_spec(ag, ac, tk, tm, sk, si)}[mode]
    b_spec = {"nn": _tile_spec(bg, bc, tk, tn, sk, sj), "nt": _tile_spec(bg, bc, tn, tk, sj, sk),
              "tn": _tile_spec(bg, bc, tk, tn, sk, sj)}[mode]
    out_shape = (M, N) if out_groups == 1 else (out_groups, M, N // out_groups)

    n_after = len(after)

    def body(*refs):
        a_ref, b_ref, o_ref = refs[0], refs[1], refs[2 + n_after]
        if nk == 1:
            o_ref[...] = lax.dot_general(a_ref[...], b_ref[...], dims, preferred_element_type=F32).astype(o_ref.dtype)
            return
        acc_ref = refs[3 + n_after]
        k = pl.program_id(2)

        @pl.when(k == 0)
        def _():
            acc_ref[...] = jnp.zeros_like(acc_ref)

        acc_ref[...] += lax.dot_general(a_ref[...], b_ref[...], dims, preferred_element_type=F32)

        @pl.when(k == nk - 1)
        def _():
            o_ref[...] = acc_ref[...].astype(o_ref.dtype)

    return pl.pallas_call(
        body, name=name, grid=(M // tm, N // tn, nk),
        in_specs=[a_spec, b_spec] + [pl.BlockSpec(memory_space=pl.ANY)] * n_after,
        out_specs=_tile_spec(out_groups, N, tm, tn, si, sj),
        out_shape=jax.ShapeDtypeStruct(out_shape, out_dtype),
        scratch_shapes=[pltpu.VMEM((tm, tn), F32)] if nk > 1 else [],
        compiler_params=_params(("parallel", "parallel", "arbitrary")),
    )(a, b, *after)


def _ew(fn, name, grid, ins, in_specs, outs, acc=(), alias=None, after=()):
    n_in = len(ins)
    n_ax = len(grid)
    extra, aliases = list(after), {}
    if alias is not None:
        extra, aliases = extra + [alias[0]], {n_in + len(after): alias[1]}
    extra_specs = [pl.BlockSpec(memory_space=pl.ANY)] * len(extra)

    def body(*refs):
        in_refs, out_refs = refs[:n_in], refs[n_in + len(extra):]
        ids = [pl.program_id(a) for a in range(n_ax)]
        res = fn(*[r[...] for r in in_refs])
        first = ids[0] == 0
        for t in ids[1:]:
            first = jnp.logical_and(first, t == 0)
        for idx, (r, val) in enumerate(zip(out_refs, res)):
            if idx in acc:
                @pl.when(first)
                def _(r=r, val=val):
                    r[...] = val.astype(r.dtype)

                @pl.when(jnp.logical_not(first))
                def _(r=r, val=val):
                    r[...] += val.astype(r.dtype)
            elif isinstance(val, tuple):
                for t, part in enumerate(val):
                    r[t] = part.astype(r.dtype)
            else:
                r[...] = val.astype(r.dtype)

    return pl.pallas_call(
        body, name=name, grid=grid, in_specs=list(in_specs) + extra_specs,
        out_specs=[o[2] for o in outs],
        out_shape=[jax.ShapeDtypeStruct(o[0], o[1]) for o in outs],
        input_output_aliases=aliases,
        compiler_params=_params(("arbitrary",) * n_ax),
    )(*ins, *extra)


def _ew_slot(fn, name, grid, slots, ins, in_specs, out_shape, out_dtype, out_spec):
    def body(s_ref, *refs):
        refs[-1][...] = fn(*[r[...] for r in refs[:-1]]).astype(refs[-1].dtype)

    return pl.pallas_call(
        body, name=name,
        grid_spec=pltpu.PrefetchScalarGridSpec(num_scalar_prefetch=1, grid=grid, in_specs=list(in_specs),
                                               out_specs=out_spec),
        out_shape=jax.ShapeDtypeStruct(out_shape, out_dtype),
        compiler_params=_params(("arbitrary",) * len(grid)),
    )(slots, *ins)


def _row(bs, w, col=0):
    return pl.BlockSpec((bs, w), lambda i, col=col: (i, col))


def _full(shape):
    nd = len(shape)
    return pl.BlockSpec(tuple(shape), lambda *_: (0,) * nd)


def _rms(x, g):
    return x * lax.rsqrt(jnp.mean(x * x, axis=-1, keepdims=True) + EPS) * g


def _f_normmod(x, g, sc, sh):
    return _rms(x, g) * (1.0 + sc) + sh


def _f_res_normmod(x, mix, gt, g, sc, sh):
    x1 = x + gt * mix
    return x1, _f_normmod(x1, g, sc, sh)


def _f_loss(x1, f, gt, gf, tgt):
    y = _rms(x1 + gt * f, gf)
    return 0.5 * jnp.sum(jnp.mean(jnp.square(y - tgt), axis=-1))


def _f_dn_gate(nh, ba, alp, dtb):
    lane = lax.broadcasted_iota(jnp.int32, ba.shape, 1)
    m = (lane < nh).astype(F32)
    beta = _sigmoid(ba)
    g = -jnp.exp(alp) * _softplus(ba + dtb)
    return m * beta + (1.0 - m) * g


def _f_dn_post(o, z, g):
    return o * lax.rsqrt(jnp.mean(o * o, axis=-1, keepdims=True) + EPS) * g * _silu(z)


def _f_cf_ln(u, g, b):
    mu = jnp.mean(u, axis=-1, keepdims=True)
    xc = u - mu
    y = xc * lax.rsqrt(jnp.mean(xc * xc, axis=-1, keepdims=True) + EPS)
    return _silu(y * g + b)


def _f_merge(a, b, ga, gb):
    return _sigmoid(ga) * a + _sigmoid(gb) * b


def _shift_down(u, d, rows):
    if d == 0:
        return u
    return jnp.where(rows >= d, pltpu.roll(u, d, 0), 0.0)


def _shift_up(u, d, rows):
    if d == 0:
        return u
    s = u.shape[0]
    return jnp.where(rows < s - d, pltpu.roll(u, s - d, 0), 0.0)


def _conv(u, w_ref, kw, rows):
    acc = None
    for k in range(kw):
        t = w_ref[k:k + 1, :] * _shift_down(u, kw - 1 - k, rows)
        acc = t if acc is None else acc + t
    return acc


def _conv_t(dc, w_ref, kw, rows):
    acc = None
    for k in range(kw):
        t = w_ref[k:k + 1, :] * _shift_up(dc, kw - 1 - k, rows)
        acc = t if acc is None else acc + t
    return acc


def _conv_fwd_call(name, kw, ncol, ins, in_specs, extras, extra_specs, w, w_spec, pre, post, out_shape, out_spec):
    n_in, n_ex = len(ins), len(extras)

    def body(*refs):
        in_refs, ex_refs = refs[:n_in], refs[n_in:n_in + n_ex]
        w_ref, out_ref = refs[n_in + n_ex], refs[n_in + n_ex + 1]
        j = pl.program_id(0)
        u = pre(*[r[...] for r in in_refs])
        rows = lax.broadcasted_iota(jnp.int32, u.shape, 0)
        cv = _conv(u, w_ref, kw, rows)
        out_ref[...] = post(j, cv, *[r[...] for r in ex_refs]).astype(out_ref.dtype)

    return pl.pallas_call(
        body, name=name, grid=(ncol,), in_specs=list(in_specs) + list(extra_specs) + [w_spec],
        out_specs=out_spec, out_shape=out_shape, compiler_params=_params(("arbitrary",)),
    )(*ins, *extras, w)


def _conv_bwd_call(name, kw, ncol, ins, in_specs, extras, extra_specs, w, w_spec, pre, post, dout, dout_spec,
                   dio_shapes, dio_specs, dio_pack, dw_shape, dw_spec, alias=None, after=()):
    n_in, n_ex, n_io = len(ins), len(extras), len(dio_shapes)
    al, aliases = list(after), {}
    if alias is not None:
        al, aliases = al + [alias], {n_in + n_ex + 2 + len(after): 0}
    al_specs = [pl.BlockSpec(memory_space=pl.ANY)] * len(al)

    def body(*refs):
        in_refs, ex_refs = refs[:n_in], refs[n_in:n_in + n_ex]
        w_ref, dout_ref = refs[n_in + n_ex], refs[n_in + n_ex + 1]
        outs = refs[n_in + n_ex + 2 + len(al):]
        dio_refs, dw_ref = outs[:n_io], outs[n_io]
        j = pl.program_id(0)
        u, pre_vjp = jax.vjp(pre, *[r[...] for r in in_refs])
        rows = lax.broadcasted_iota(jnp.int32, u.shape, 0)
        cv = _conv(u, w_ref, kw, rows)
        _, post_vjp = jax.vjp(lambda cc, *ex: post(j, cc, *ex), cv, *[r[...] for r in ex_refs])
        g = post_vjp(dout_ref[...].astype(F32))
        dc = g[0]
        for k in range(kw):
            dw_ref[k:k + 1, :] = jnp.sum(dc * _shift_down(u, kw - 1 - k, rows), axis=0, keepdims=True)
        du = _conv_t(dc, w_ref, kw, rows)
        for r, val in zip(dio_refs, dio_pack(*pre_vjp(du), *g[1:])):
            if isinstance(val, tuple):
                for t, part in enumerate(val):
                    r[t] = part.astype(r.dtype)
            else:
                r[...] = val.astype(r.dtype)

    return pl.pallas_call(
        body, name=name, grid=(ncol,),
        in_specs=list(in_specs) + list(extra_specs) + [w_spec, dout_spec] + al_specs,
        out_specs=list(dio_specs) + [dw_spec],
        out_shape=list(dio_shapes) + [dw_shape],
        input_output_aliases=aliases,
        compiler_params=_params(("arbitrary",)),
    )(*ins, *extras, w, dout, *al)


def _tri_inverse(a):
    c = a.shape[0]
    ii = lax.broadcasted_iota(jnp.int32, (c, c), 0)
    jj = lax.broadcasted_iota(jnp.int32, (c, c), 1)
    eye = (ii == jj).astype(F32)

    def same_block(bits):
        return jnp.right_shift(ii, bits) == jnp.right_shift(jj, bits)

    d = jnp.where(same_block(3), a, 0.0)
    d2 = _mmx(d, d, NN)
    yield
    t = (eye - d) + _mmx(eye - d, d2, NN)
    d4 = _mmx(d2, d2, NN)
    yield
    t = t + _mmx(t, d4, NN)
    yield
    bits = 3
    while (1 << bits) < c:
        low = jnp.where(jnp.logical_and(same_block(bits + 1), jnp.logical_not(same_block(bits))), a, 0.0)
        tl = _mmx(t, low, NN)
        yield
        t = t - _mmx(tl, t, NN)
        yield
        bits += 1
    return t


def _lockstep(gens):
    out = [None] * len(gens)
    live = list(range(len(gens)))
    while live:
        nxt = []
        for i in live:
            try:
                next(gens[i])
                nxt.append(i)
            except StopIteration as e:
                out[i] = e.value
        live = nxt
    return out


def _dn_common(q, k, v, gb, bb):
    c = q.shape[0]
    ii = lax.broadcasted_iota(jnp.int32, (c, c), 0)
    jj = lax.broadcasted_iota(jnp.int32, (c, c), 1)
    causal = jj <= ii
    strict = jj < ii
    low = causal.astype(F32)
    ones = jnp.ones((c, LANES), F32)
    gc = _mmx(low, gb, NN)
    kb = k * bb
    vb = v * bb
    kk = _mm(kb, k, NT)
    qk = _mm(q, k, NT)
    yield
    diff = (_mmx(gc, ones, NT) - _mmx(ones, gc, NT)) * (1.0 / LANES)
    gl = jnp.sum(gb, axis=0, keepdims=True)
    eg = jnp.exp(gc)
    egm = jnp.exp(gl - gc)
    egl = jnp.exp(gl)
    kbg = kb * eg
    yield
    decay = jnp.where(causal, jnp.exp(jnp.where(causal, diff, 0.0)), 0.0)
    t = yield from _tri_inverse(jnp.where(strict, kk * decay, 0.0))
    attn = qk * decay
    return dict(causal=causal, strict=strict, low=low, ones=ones, decay=decay, eg=eg, egm=egm, egl=egl,
                kb=kb, vb=vb, kbg=kbg, kk=kk, t=t, qk=qk, attn=attn, qd=q * eg, kd=k * egm)


def _dn_fwd_stages(q, k, v, gb, bb, s):
    m = yield from _dn_common(q, k, v, gb, bb)
    r = m["vb"] - _mm(m["kbg"], s, NN)
    qs = _mm(m["qd"], s, NN)
    yield
    u = _mmx(m["t"], r, NN)
    yield
    o = qs + _mm(m["attn"], u, NN)
    s2 = s * m["egl"] + _mm(m["kd"], u, TN)
    return o, s2


def _dn_bwd_stages(q, k, v, gb, bb, s, do, dsp):
    m = yield from _dn_common(q, k, v, gb, bb)
    c = q.shape[0]
    t, decay, eg, egm, egl = m["t"], m["decay"], m["eg"], m["egm"], m["egl"]
    r = m["vb"] - _mm(m["kbg"], s, NN)
    du = _mm(m["attn"], do, TN) + _mm(m["kd"], dsp, NN)
    dqd = _mm(do, s, NT)
    ds = dsp * egl + _mm(m["qd"], do, TN)
    degl = jnp.sum(jnp.sum(dsp * s, axis=1, keepdims=True), axis=0, keepdims=True)
    yield
    u = _mmx(t, r, NN)
    dr = _mmx(t, du, TN)
    yield
    dattn = jnp.where(m["causal"], _mm(do, u, NT), 0.0)
    dkd = _mm(u, dsp, NT)
    da = jnp.where(m["strict"], -_mm(dr, u, NT), 0.0)
    dkbg = -_mm(dr, s, NT)
    ds = ds - _mm(m["kbg"], dr, TN)
    yield
    dkk = da * decay
    dqk = dattn * decay
    ddiff = (da * m["kk"] + dattn * m["qk"]) * decay
    dgc = _mmx(ddiff, m["ones"], NN) - _mmx(ddiff, m["ones"], TN)
    dkb = _mm(dkk, k, NN) + dkbg * eg
    dk = _mm(dkk, m["kb"], TN) + _mm(dqk, q, TN) + dkd * egm + dkb * bb
    dq = _mm(dqk, k, NN) + dqd * eg
    yield
    dgc = dgc + jnp.sum(dqd * q + dkbg * m["kb"], axis=-1, keepdims=True) * eg
    tt = jnp.sum(dkd * k, axis=-1, keepdims=True) * egm
    dgc = dgc - tt
    dgl = jnp.sum(tt, axis=0, keepdims=True) + degl * egl
    dbb = jnp.sum(dkb * k + dr * v, axis=-1, keepdims=True) + jnp.zeros((c, LANES), F32)
    dv = dr * bb
    dgb = _mmx(m["low"], dgc, TN) + dgl
    return dq, dk, dv, dgb, dbb, ds


def _dn_fwd_chunk(q, k, v, gb, bb, s):
    return _lockstep([_dn_fwd_stages(q, k, v, gb, bb, s)])[0]


def _dn_bwd_chunk(q, k, v, gb, bb, s, do, dsp):
    return _lockstep([_dn_bwd_stages(q, k, v, gb, bb, s, do, dsp)])[0]


def _dn_fwd_call(qkvn, gb, bb):
    _, nh, s, dh = qkvn.shape
    c = min(DN_CHUNK, s)
    n = s // c
    hb = nh

    def body(q_ref, k_ref, v_ref, g_ref, b_ref, o_ref, st_ref, s_ref):
        @pl.when(pl.program_id(1) == 0)
        def _():
            s_ref[...] = jnp.zeros_like(s_ref)

        st_ref[...] = s_ref[...]
        heads = [_dn_fwd_stages(q_ref[h], k_ref[h], v_ref[h], g_ref[h], b_ref[h], s_ref[h]) for h in range(hb)]
        for h, (o, s2) in enumerate(_lockstep(heads)):
            o_ref[h] = o
            s_ref[h] = s2

    def qspec(t):
        return pl.BlockSpec((None, hb, c, dh), lambda i, j, t=t: (t, i, j, 0))

    hs = pl.BlockSpec((hb, c, dh), lambda i, j: (i, j, 0))
    return pl.pallas_call(
        body, name="dn_fwd", grid=(nh // hb, n),
        in_specs=[qspec(0), qspec(1), qspec(2), hs, hs],
        out_specs=[hs, pl.BlockSpec((hb, None, dh, dh), lambda i, j: (i, j, 0, 0))],
        out_shape=[jax.ShapeDtypeStruct((nh, s, dh), F32), jax.ShapeDtypeStruct((nh, n, dh, dh), F32)],
        scratch_shapes=[pltpu.VMEM((hb, dh, dh), F32)],
        compiler_params=_params(("arbitrary", "arbitrary")),
    )(qkvn, qkvn, qkvn, gb, bb)


def _dn_bwd_call(qkvn, gb, bb, states, do):
    _, nh, s, dh = qkvn.shape
    c = min(DN_CHUNK, s)
    n = s // c
    hb = nh

    def body(q_ref, k_ref, v_ref, g_ref, b_ref, st_ref, do_ref, dqkv_ref, dg_ref, db_ref, ds_ref):
        @pl.when(pl.program_id(1) == 0)
        def _():
            ds_ref[...] = jnp.zeros_like(ds_ref)

        heads = [_dn_bwd_stages(q_ref[h], k_ref[h], v_ref[h], g_ref[h], b_ref[h], st_ref[h], do_ref[h], ds_ref[h])
                 for h in range(hb)]
        for h, (dq, dk, dv, dg, db, ds) in enumerate(_lockstep(heads)):
            dqkv_ref[0, h] = dq
            dqkv_ref[1, h] = dk
            dqkv_ref[2, h] = dv
            dg_ref[h] = dg
            db_ref[h] = db
            ds_ref[h] = ds

    def qspec(t):
        return pl.BlockSpec((None, hb, c, dh), lambda i, j, t=t: (t, i, n - 1 - j, 0))

    hs = pl.BlockSpec((hb, c, dh), lambda i, j: (i, n - 1 - j, 0))
    sh = jax.ShapeDtypeStruct((nh, s, dh), F32)
    return pl.pallas_call(
        body, name="dn_bwd", grid=(nh // hb, n),
        in_specs=[qspec(0), qspec(1), qspec(2), hs, hs,
                  pl.BlockSpec((hb, None, dh, dh), lambda i, j: (i, n - 1 - j, 0, 0)), hs],
        out_specs=[pl.BlockSpec((3, hb, c, dh), lambda i, j: (0, i, n - 1 - j, 0)), hs, hs],
        out_shape=[jax.ShapeDtypeStruct((3, nh, s, dh), F32), sh, sh],
        scratch_shapes=[pltpu.VMEM((hb, dh, dh), F32)],
        compiler_params=_params(("arbitrary", "arbitrary")),
    )(qkvn, qkvn, qkvn, gb, bb, states, do)


def _adamw(w, g, m, v, name):
    r, c = w.shape
    br = _div_tile(r, max(8, (1 << 18) // max(c, 1)), 8)

    def fn(w, g, m, v):
        m = ADAM_B1 * m + (1.0 - ADAM_B1) * g
        v = ADAM_B2 * v + (1.0 - ADAM_B2) * jnp.square(g)
        m_hat = m / (1.0 - ADAM_B1 ** ADAM_STEP)
        v_hat = v / (1.0 - ADAM_B2 ** ADAM_STEP)
        delta = -ADAM_LR * (m_hat / (jnp.sqrt(v_hat) + ADAM_EPS) + ADAM_WD * w)
        return delta, m, v

    spec = pl.BlockSpec((br, c), lambda i: (i, 0))
    return _ew(fn, name, (r // br,), [w, g, m, v], [spec] * 4, [((r, c), F32, spec)] * 3)


def _coords():
    return lax.axis_index("x"), lax.axis_index("y"), lax.axis_index("c")


def _all_gather_small(v):
    r, w = v.shape

    def body(v_ref, out_ref, send_sems, recv_sems, local_sem):
        x, y, c = _coords()
        me = 4 * x + 2 * y + c
        mine = pltpu.make_async_copy(v_ref, out_ref.at[me], local_sem)
        mine.start()
        peers = []
        for k in range(1, 8):
            px = 1 - x if k & 4 else x
            py = 1 - y if k & 2 else y
            pc = 1 - c if k & 1 else c
            peers.append((px, py, pc))
        sends = []
        for k, peer in enumerate(peers):
            cp = pltpu.make_async_remote_copy(src_ref=v_ref, dst_ref=out_ref.at[me], send_sem=send_sems.at[k],
                                              recv_sem=recv_sems.at[k], device_id=peer, device_id_type=MESH)
            cp.start()
            sends.append(cp)
        for k, (px, py, pc) in enumerate(peers):
            pltpu.make_async_remote_copy(src_ref=v_ref, dst_ref=out_ref.at[4 * px + 2 * py + pc],
                                         send_sem=send_sems.at[k], recv_sem=recv_sems.at[k],
                                         device_id=(px, py, pc), device_id_type=MESH).wait_recv()
        for cp in sends:
            cp.wait_send()
        mine.wait()

    return pl.pallas_call(
        body, name="ag_small", out_shape=jax.ShapeDtypeStruct((8, r, w), v.dtype),
        in_specs=[pl.BlockSpec(memory_space=pltpu.VMEM)], out_specs=pl.BlockSpec(memory_space=pltpu.VMEM),
        scratch_shapes=[pltpu.SemaphoreType.DMA((7,)), pltpu.SemaphoreType.DMA((7,)), pltpu.SemaphoreType.DMA],
        compiler_params=pltpu.CompilerParams(vmem_limit_bytes=VMEM_LIMIT),
    )(v)


def _hbm_call(body, name, arrays, out_shapes, n_sems, aliases=None):
    hbm = pl.BlockSpec(memory_space=pltpu.HBM)
    return pl.pallas_call(
        body, name=name, out_shape=list(out_shapes), in_specs=[hbm] * len(arrays), out_specs=[hbm] * len(out_shapes),
        input_output_aliases=aliases or {},
        scratch_shapes=[pltpu.SemaphoreType.DMA((n_sems,)), pltpu.SemaphoreType.DMA((n_sems,))],
    )(*arrays)


def _half_rows(ref_shape, c):
    h = ref_shape[1] // 2
    return pl.ds(pl.multiple_of(c * h, 16), h), pl.ds(pl.multiple_of((1 - c) * h, 16), h)


def _pair_copies(p_refs, land_refs, send_sems, recv_sems):
    x, y, c = _coords()
    cps = []
    for w, (p, land) in enumerate(zip(p_refs, land_refs)):
        _, other = _half_rows(p.shape, c)
        cps.append(pltpu.make_async_remote_copy(src_ref=p.at[:, other], dst_ref=land, send_sem=send_sems.at[w],
                                                recv_sem=recv_sems.at[w], device_id=(x, y, 1 - c),
                                                device_id_type=MESH))
    return cps


def _pair_gather(gs, name):
    n = len(gs)

    def body(*refs):
        ins, outs, send_sems, recv_sems = refs[:n], refs[n:2 * n], refs[2 * n], refs[2 * n + 1]
        x, y, c = _coords()
        cps = []
        for w in range(n):
            cp = pltpu.make_async_remote_copy(src_ref=ins[w].at[c], dst_ref=outs[w].at[c], send_sem=send_sems.at[w],
                                              recv_sem=recv_sems.at[w], device_id=(x, y, 1 - c), device_id_type=MESH)
            cp.start()
            cps.append(cp)
        for w, cp in enumerate(cps):
            pltpu.make_async_remote_copy(src_ref=ins[w].at[c], dst_ref=outs[w].at[1 - c], send_sem=send_sems.at[w],
                                         recv_sem=recv_sems.at[w], device_id=(x, y, 1 - c),
                                         device_id_type=MESH).wait_recv()
            cp.wait_send()

    shapes = [jax.ShapeDtypeStruct(g.shape, g.dtype) for g in gs]
    return _hbm_call(body, name, gs, shapes, n, aliases={i: i for i in range(n)})


_HBM = pl.BlockSpec(memory_space=pltpu.HBM)
_SEM = pl.BlockSpec(memory_space=pltpu.SEMAPHORE)
_EFFECT = pltpu.SideEffectType.DATAFLOW_SIDE_EFFECTING


def _split_start(name, srcs, lands, after, n_copies, make_copies):
    n, m = len(srcs), len(lands)
    arrays = [pltpu.with_memory_space_constraint(a, pltpu.HBM) for a in list(srcs) + list(lands)]

    def body(*refs):
        src_refs, land_refs = refs[:n], refs[n:n + m]
        send_sems, recv_sems = refs[n + m + 1], refs[n + m + 2]
        for cp in make_copies(src_refs, land_refs, send_sems, recv_sems):
            cp.start()
        refs[-1][...] = jnp.zeros_like(refs[-1])

    outs = pl.pallas_call(
        body, name=name,
        out_shape=(pltpu.SemaphoreType.DMA((n_copies,)), pltpu.SemaphoreType.DMA((n_copies,)),
                   *[pltpu.HBM(a.shape, a.dtype) for a in arrays], jax.ShapeDtypeStruct((8, LANES), F32)),
        in_specs=[_HBM] * (n + m) + [pl.BlockSpec(memory_space=pl.ANY)],
        out_specs=(_SEM, _SEM, *[_HBM] * (n + m), pl.BlockSpec(memory_space=pltpu.VMEM)),
        input_output_aliases={i: 2 + i for i in range(n + m)},
        compiler_params=pltpu.CompilerParams(has_side_effects=_EFFECT),
    )(*arrays, after)
    return (outs[0], outs[1], list(outs[2:2 + n]), list(outs[2 + n:2 + n + m])), outs[-1]


def _split_wait(name, state, after, make_copies):
    send_sems, recv_sems, srcs, lands = state
    n, m = len(srcs), len(lands)

    def body(*refs):
        src_refs, land_refs = refs[:n], refs[n:n + m]
        for cp in make_copies(src_refs, land_refs, refs[n + m], refs[n + m + 1]):
            cp.wait_send()
            cp.wait_recv()

    outs = pl.pallas_call(
        body, name=name, out_shape=tuple(pltpu.HBM(a.shape, a.dtype) for a in srcs + lands),
        in_specs=[_HBM] * (n + m) + [_SEM, _SEM, pl.BlockSpec(memory_space=pl.ANY)], out_specs=tuple([_HBM] * (n + m)),
        input_output_aliases={i: i for i in range(n + m)},
        compiler_params=pltpu.CompilerParams(has_side_effects=_EFFECT),
    )(*srcs, *lands, send_sems, recv_sems, after)
    return list(outs[:n]), list(outs[n:])


def _scatter_copies(q_refs, land_refs, send_sems, recv_sems):
    x, y, c = _coords()
    cps = []
    for w, (q, land) in enumerate(zip(q_refs, land_refs)):
        for k, (px, py) in enumerate([(1 - x, y), (x, 1 - y), (1 - x, 1 - y)]):
            cps.append(pltpu.make_async_remote_copy(src_ref=q.at[2 * px + py], dst_ref=land.at[k],
                                                    send_sem=send_sems.at[3 * w + k], recv_sem=recv_sems.at[3 * w + k],
                                                    device_id=(px, py, c), device_id_type=MESH))
    return cps


def _chip_scatter_start(qs, after, name):
    lands = [lax.empty((3,) + q.shape[1:], q.dtype) for q in qs]
    return _split_start(name, qs, lands, after, 3 * len(qs), _scatter_copies)


def _chip_scatter_wait(state, after, name):
    return _split_wait(name, state, after, _scatter_copies)


def _gather_copies(src_refs, buf_refs, send_sems, recv_sems):
    x, y, c = _coords()
    j = 2 * x + y
    cps = []
    for w, buf in enumerate(buf_refs):
        mine, _ = _half_rows(buf.shape, c)
        for k, (px, py) in enumerate([(1 - x, y), (x, 1 - y), (1 - x, 1 - y)]):
            cps.append(pltpu.make_async_remote_copy(src_ref=buf.at[j, mine], dst_ref=buf.at[j, mine],
                                                    send_sem=send_sems.at[3 * w + k], recv_sem=recv_sems.at[3 * w + k],
                                                    device_id=(px, py, c), device_id_type=MESH))
    return cps


def _gather_wait_copies(src_refs, buf_refs, send_sems, recv_sems):
    x, y, c = _coords()
    j = 2 * x + y
    cps = []
    for w, buf in enumerate(buf_refs):
        mine, _ = _half_rows(buf.shape, c)
        for k, (px, py) in enumerate([(1 - x, y), (x, 1 - y), (1 - x, 1 - y)]):
            cps.append(pltpu.make_async_remote_copy(src_ref=buf.at[j, mine], dst_ref=buf.at[2 * px + py, mine],
                                                    send_sem=send_sems.at[3 * w + k], recv_sem=recv_sems.at[3 * w + k],
                                                    device_id=(px, py, c), device_id_type=MESH))
    return cps


def _pair_forward(bufs, name):
    n = len(bufs)

    def body(*refs):
        ins, outs, send_sems, recv_sems = refs[:n], refs[n:2 * n], refs[2 * n], refs[2 * n + 1]
        x, y, c = _coords()
        chips = [(1 - x, y), (x, 1 - y), (1 - x, 1 - y)]
        cps = []
        for w in range(n):
            mine, _ = _half_rows(outs[w].shape, c)
            for k, (px, py) in enumerate(chips):
                cp = pltpu.make_async_remote_copy(src_ref=ins[w].at[2 * px + py, mine],
                                                  dst_ref=outs[w].at[2 * px + py, mine],
                                                  send_sem=send_sems.at[3 * w + k], recv_sem=recv_sems.at[3 * w + k],
                                                  device_id=(x, y, 1 - c), device_id_type=MESH)
                cp.start()
                cps.append(cp)
        for w in range(n):
            _, sib = _half_rows(outs[w].shape, c)
            for k, (px, py) in enumerate(chips):
                pltpu.make_async_remote_copy(src_ref=ins[w].at[2 * px + py, sib], dst_ref=outs[w].at[2 * px + py, sib],
                                             send_sem=send_sems.at[3 * w + k], recv_sem=recv_sems.at[3 * w + k],
                                             device_id=(x, y, 1 - c), device_id_type=MESH).wait_recv()
        for cp in cps:
            cp.wait_send()

    shapes = [jax.ShapeDtypeStruct(b.shape, b.dtype) for b in bufs]
    return _hbm_call(body, name, bufs, shapes, 3 * n, aliases={i: i for i in range(n)})


def _slot_rows(h, cs):
    return _div_tile(h, max(16, (1 << 19) // cs), 16)


def _cast_into_slot(w, slot, name):
    r, cs = w.shape
    br = _slot_rows(r, cs)
    return _ew_slot(lambda a: a, name, (r // br,), slot, [w], [pl.BlockSpec((br, cs), lambda i, s: (i, 0))],
                    (4, r, cs), BF16, pl.BlockSpec((None, br, cs), lambda i, s: (s[0], i, 0)))


def _pair_add(p, rb, core, name):
    n, r, cs = p.shape
    h = r // 2
    br = _slot_rows(h, cs)
    nb = h // br
    return _ew_slot(lambda a, b: a + b, name, (n, nb), core, [p, rb],
                    [pl.BlockSpec((None, br, cs), lambda s, i, c: (s, c[0] * nb + i, 0)),
                     pl.BlockSpec((None, br, cs), lambda s, i, c: (s, i, 0))],
                    (n, h, cs), BF16, pl.BlockSpec((None, br, cs), lambda s, i, c: (s, i, 0)))


def _chip_sum(q, r3, slots, name):
    _, h, cs = q.shape
    br = _slot_rows(h, cs)

    def fn(a, b):
        acc = a.astype(F32)
        for k in range(3):
            acc = acc + b[k].astype(F32)
        return acc

    return _ew_slot(fn, name, (h // br,), slots, [q, r3],
                    [pl.BlockSpec((None, br, cs), lambda i, s: (s[0], i, 0)),
                     pl.BlockSpec((3, br, cs), lambda i, s: (0, i, 0))],
                    (2, h, cs), F32, pl.BlockSpec((None, br, cs), lambda i, s: (s[1], i, 0)))


def _sum_slots(r, name):
    n, h, w = r.shape
    br = _div_tile(h, 2048, 16)

    def fn(blk):
        acc = blk[0].astype(F32)
        for s in range(1, n):
            acc = acc + blk[s].astype(F32)
        return (acc,)

    return _ew(fn, name, (h // br,), [r], [pl.BlockSpec((n, br, w), lambda i: (0, i, 0))],
               [((h, w), F32, pl.BlockSpec((br, w), lambda i: (i, 0)))])[0]


def _pack_small(arrs):
    flat = jnp.concatenate([a.reshape(-1).astype(F32) for a in arrs])
    n = flat.shape[0]
    rows = -(-n // LANES)
    rows = -(-rows // 8) * 8
    return jnp.pad(flat, (0, rows * LANES - n)).reshape(rows, LANES)


def _unpack_small(p, shapes):
    lead = p.shape[:-2]
    flat = p.reshape(lead + (-1,))
    out, off = [], 0
    for sh in shapes:
        n = 1
        for d in sh:
            n *= d
        out.append(flat[..., off:off + n].reshape(lead + tuple(sh)))
        off += n
    return out


def kernel(x, c, w_ada, b_ada, norm1_g, w_in, dn_conv_w, dn_a_log, dn_dt_bias, dn_norm_g, dn_w_o, cf_conv_w, cf_ln_g, cf_ln_b, cf_w_o, w_out, norm2_g, ffn_w_up, ffn_conv_w, ffn_w_down, final_norm_g, loss_target, m_w_ada, m_b_ada, m_norm1_g, m_w_in, m_dn_conv_w, m_dn_a_log, m_dn_dt_bias, m_dn_norm_g, m_dn_w_o, m_cf_conv_w, m_cf_ln_g, m_cf_ln_b, m_cf_w_o, m_w_out, m_norm2_g, m_ffn_w_up, m_ffn_conv_w, m_ffn_w_down, m_final_norm_g, v_w_ada, v_b_ada, v_norm1_g, v_w_in, v_dn_conv_w, v_dn_a_log, v_dn_dt_bias, v_dn_norm_g, v_dn_w_o, v_cf_conv_w, v_cf_ln_g, v_cf_ln_b, v_cf_w_o, v_w_out, v_norm2_g, v_ffn_w_up, v_ffn_conv_w, v_ffn_w_down, v_final_norm_g):
    xi, yi, ci = _coords()
    chip = 2 * xi + yi
    me = 4 * xi + 2 * yi + ci
    core = jnp.reshape(ci, (1,)).astype(jnp.int32)

    S, D = x.shape[1], x.shape[2]
    NH = dn_a_log.shape[1]
    DH = dn_norm_g.shape[1]
    DNW = NH * DH
    CFW = cf_ln_g.shape[1]
    FFN = ffn_w_down.shape[1] * 4
    KDN, KCF, KFF = dn_conv_w.shape[1], cf_conv_w.shape[1], ffn_conv_w.shape[1]
    NIN = w_in.shape[2] * 4
    assert NIN == 4 * DNW + 2 * NH + 2 * CFW + 2 * D and DH == LANES and 2 * NH <= LANES
    x2, tgt = x[0], loss_target[0]

    sm_shapes = [(D,), (KDN, 3 * DNW // 4), (KCF, CFW // 4), (KFF, FFN // 4)]
    g1 = _all_gather_small(_pack_small([c[0], dn_conv_w[0], cf_conv_w[0], ffn_conv_w[0]]))
    c_all, dcw_s, ccw_s, fcw_s = _unpack_small(g1, sm_shapes)

    def chips_cols(t):
        t = t[0::2]
        return jnp.transpose(t, (1, 0, 2)).reshape(t.shape[1], -1)

    dn_cw, cf_cw, ff_cw = chips_cols(dcw_s), chips_cols(ccw_s), chips_cols(fcw_s)

    slot_chip = jnp.reshape(chip, (1,)).astype(jnp.int32)
    big = [w_in[0], dn_w_o[0], cf_w_o[0], w_out[0], ffn_w_up[0], ffn_w_down[0]]
    names_big = ["w_in", "dn_w_o", "cf_w_o", "w_out", "ffn_w_up", "ffn_w_down"]
    in_state, in_token = _split_start("ag_in_start", [], [_cast_into_slot(big[0], slot_chip, "cast_w_in")], c, 3,
                                      _gather_copies)
    bufs = [_cast_into_slot(w, slot_chip, "cast_" + nm) for w, nm in zip(big[1:], names_big[1:])]
    mid_state, mid_token = _split_start("ag_mid_start", [], bufs[:3], in_token, 9, _gather_copies)
    ag_state, ag_token = _split_start("ag_late_start", [], bufs[3:], mid_token, 6, _gather_copies)

    cb_cf = _div_tile(CFW, 256, LANES)
    o_b = 4 * DNW
    o_glu = o_b + 2 * NH
    o_ga = o_glu + 2 * CFW
    NA = NIN - 2 * NH
    a_z, a_ga, a_gb, a_glu, a_ba = 3 * DNW, 4 * DNW, 4 * DNW + D, 4 * DNW + 2 * D, NA

    segs = [(0, 0, o_b), (o_b, NA, 2 * NH), (o_ga, a_ga, 2 * D)]
    for t in range(CFW // cb_cf):
        segs += [(o_glu + t * cb_cf, a_glu + 2 * t * cb_cf, cb_cf),
                 (o_glu + CFW + t * cb_cf, a_glu + (2 * t + 1) * cb_cf, cb_cf)]
    CS = NIN // 4

    def shard_slices(lo, n):
        out = []
        while n > 0:
            j, off = lo // CS, lo % CS
            m = min(n, CS - off)
            out.append(w_in_g[j][:, off:off + m])
            lo, n = lo + m, n - m
        return out

    CA = w_ada.shape[2]
    b_sh = lax.dynamic_slice(b_ada, (0, chip * CA), (1, CA))
    tn_a = _div_tile(CA, 512, LANES)

    def mod_fn(cc, w, b):
        return (_mm(_silu(cc), w, NN) + b,)

    mod_sh = _ew(mod_fn, "ada_mod", (CA // tn_a,), [c_all, w_ada[0], b_sh],
                 [_full((8, D)), pl.BlockSpec((D, tn_a), lambda j: (0, j)), pl.BlockSpec((1, tn_a), lambda j: (0, j))],
                 [((8, CA), F32, pl.BlockSpec((8, tn_a), lambda j: (0, j)))])[0]
    g2 = _all_gather_small(_pack_small([mod_sh]))
    mod_all = _unpack_small(g2, [(8, CA)])[0][0::2]
    mod_all = jnp.transpose(mod_all, (1, 0, 2)).reshape(8, 4 * CA)
    mod_me = lax.dynamic_slice(mod_all, (me, 0), (1, 6 * D))
    sh1, sc1, gt1, sh2, sc2, gt2 = [mod_me[:, i * D:(i + 1) * D] for i in range(6)]

    bs = _div_tile(S, 128, 8)
    nb = S // bs
    vecD = _full((1, D))
    rowD = _row(bs, D)

    hn1 = _ew(lambda a, g, sc, sh: (_f_normmod(a, g, sc, sh),), "norm1_fwd", (nb,),
              [x2, norm1_g, sc1, sh1], [rowD, vecD, vecD, vecD], [((S, D), BF16, rowD)],
              after=[ag_token, m_w_in[0], v_w_in[0]])[0]
    _, in_landed = _split_wait("ag_in_wait", in_state, hn1, _gather_wait_copies)
    (w_in_g,) = _pair_forward(in_landed, "ag_in_pair")
    w_aug = jnp.concatenate([p for o, a, n in sorted(segs, key=lambda s: s[1]) for p in shard_slices(o, n)]
                            + [jnp.zeros((D, LANES - 2 * NH), BF16)], axis=1)
    proj = _matmul(hn1, w_aug, "nn", F32, "mm_in")

    def dn_post(j, cv):
        s = _silu(cv)
        nrm = s * lax.rsqrt(jnp.sum(s * s, axis=-1, keepdims=True) + EPS)
        fq = (j < NH).astype(F32)
        fk = (j < 2 * NH).astype(F32)
        scale = fq * (DH ** -0.5) + (1.0 - fq)
        return fk * (nrm * scale) + (1.0 - fk) * s

    def ident(a):
        return a

    def colS(w, off=0):
        return pl.BlockSpec((S, w), lambda j, off=off: (0, j + off))

    def wS(kw, w):
        return pl.BlockSpec((kw, w), lambda j: (0, j))

    qkv_spec = pl.BlockSpec((None, None, S, DH), lambda j: (j // NH, j % NH, 0, 0))
    dn_args = dict(kw=KDN, ncol=3 * NH, ins=[proj], in_specs=[colS(DH)], extras=[], extra_specs=[],
                   w=dn_cw, w_spec=wS(KDN, DH), pre=ident, post=dn_post)
    qkvn = _conv_fwd_call("dn_conv_fwd", out_shape=jax.ShapeDtypeStruct((3, NH, S, DH), F32), out_spec=qkv_spec,
                          **dn_args)

    alp = jnp.pad(dn_a_log, ((0, 0), (NH, LANES - 2 * NH)))
    dtb = jnp.pad(dn_dt_bias, ((0, 0), (NH, LANES - 2 * NH)))
    vecL = _full((1, LANES))
    ba_spec = _row(bs, LANES, a_ba // LANES)
    rowL = _row(bs, LANES)
    gate_fn = functools.partial(_f_dn_gate, NH)
    def gate_fwd(a, p, q):
        val = gate_fn(a, p, q)
        lane = lax.broadcasted_iota(jnp.int32, val.shape, 1)

        def spread(col):
            return jnp.sum(jnp.where(lane == col, val, 0.0), axis=-1, keepdims=True) + jnp.zeros(val.shape, F32)

        return tuple(spread(h) for h in range(NH)), tuple(spread(NH + h) for h in range(NH))

    hrow = pl.BlockSpec((NH, bs, DH), lambda i: (0, i, 0))
    bb_b, gb_b = _ew(gate_fwd, "dn_gate_fwd", (nb,), [proj, alp, dtb], [ba_spec, vecL, vecL],
                     [((NH, S, DH), F32, hrow), ((NH, S, DH), F32, hrow)])
    o_dn, states = _dn_fwd_call(qkvn, gb_b, bb_b)

    bsh = _div_tile(S, 512, 8)
    nbh = S // bsh
    o_spec = pl.BlockSpec((None, bsh, DH), lambda i, h: (h, i, 0))
    z_spec = pl.BlockSpec((bsh, DH), lambda i, h: (i, a_z // DH + h))
    oh_spec = pl.BlockSpec((bsh, DH), lambda i, h: (i, h))
    ng_spec = pl.BlockSpec((1, DH), lambda i, h: (0, 0))
    on = _ew(lambda o, z, g: (_f_dn_post(o, z, g),), "dn_post_fwd", (nbh, NH), [o_dn, proj, dn_norm_g],
             [o_spec, z_spec, ng_spec], [((S, DNW), BF16, oh_spec)])[0]
    _, landed = _split_wait("ag_mid_wait", mid_state, on, _gather_wait_copies)
    w_do_f, w_co_f, w_out_g = _pair_forward(landed, "ag_mid_pair")
    w_out_f = w_out_g.reshape(-1, w_out_g.shape[2])
    br_a = _matmul(on, w_do_f, "nn", F32, "mm_dn_o")

    def glu_pre(val, gl):
        return val * _sigmoid(gl)

    def glu_spec(t):
        return pl.BlockSpec((S, cb_cf), lambda j, t=t: (0, a_glu // cb_cf + 2 * j + t))

    cf_args = dict(kw=KCF, ncol=CFW // cb_cf, ins=[proj, proj], in_specs=[glu_spec(0), glu_spec(1)],
                   extras=[], extra_specs=[], w=cf_cw, w_spec=wS(KCF, cb_cf), pre=glu_pre, post=lambda j, cv: cv)
    uc = _conv_fwd_call("cf_conv_fwd", out_shape=jax.ShapeDtypeStruct((S, CFW), F32), out_spec=colS(cb_cf), **cf_args)
    rowC = _row(bs, CFW)
    vecC = _full((1, CFW))
    ub = _ew(lambda u, g, b: (_f_cf_ln(u, g, b),), "cf_ln_fwd", (nb,), [uc, cf_ln_g, cf_ln_b], [rowC, vecC, vecC],
             [((S, CFW), BF16, rowC)])[0]
    br_b = _matmul(ub, w_co_f, "nn", F32, "mm_cf_o")

    ga_spec, gb_spec = _row(bs, D, a_ga // D), _row(bs, D, a_gb // D)
    merged = _ew(lambda a, b, ga, gb: (_f_merge(a, b, ga, gb),), "merge_fwd", (nb,), [br_a, br_b, proj, proj],
                 [rowD, rowD, ga_spec, gb_spec], [((S, D), BF16, rowD)])[0]
    mix = _matmul(merged, w_out_f, "nn", F32, "mm_out")

    x1, hn2 = _ew(_f_res_normmod, "norm2_fwd", (nb,), [x2, mix, gt1, norm2_g, sc2, sh2],
                  [rowD, rowD, vecD, vecD, vecD, vecD], [((S, D), F32, rowD), ((S, D), BF16, rowD)])
    _, landed = _split_wait("ag_late_wait", ag_state, hn2, _gather_wait_copies)
    w_up_f, w_dn_g = _pair_forward(landed, "ag_late_pair")
    w_dn_f = w_dn_g.reshape(-1, w_dn_g.shape[2])
    up_all = _matmul(hn2, w_up_f, "nn", F32, "mm_up")

    cb_ff = _div_tile(FFN, 256, LANES)
    ff_args = dict(kw=KFF, ncol=FFN // cb_ff, ins=[up_all], in_specs=[colS(cb_ff)], extras=[up_all],
                   extra_specs=[colS(cb_ff, FFN // cb_ff)], w=ff_cw, w_spec=wS(KFF, cb_ff),
                   pre=ident, post=lambda j, cv, up: _silu(cv) * up)
    hff = _conv_fwd_call("ffn_conv_fwd", out_shape=jax.ShapeDtypeStruct((S, FFN), BF16), out_spec=colS(cb_ff), **ff_args)
    ffo = _matmul(hff, w_dn_f, "nn", F32, "mm_down")

    gf2 = final_norm_g.reshape(1, D)

    def loss_bwd(a, f, gt, gf, t):
        val, vjp = jax.vjp(_f_loss, a, f, gt, gf, t)
        da, df, dgt, dgf, _ = vjp(jnp.ones((), F32))
        return da, df, dgt, dgf, jnp.zeros((1, LANES), F32) + val

    dx1_l, dffo, dgt2, dgf, loss_v = _ew(
        loss_bwd, "loss_bwd", (nb,), [x1, ffo, gt2, gf2, tgt], [rowD, rowD, vecD, vecD, rowD],
        [((S, D), F32, rowD), ((S, D), BF16, rowD), ((1, D), F32, vecD), ((1, D), F32, vecD),
         ((1, LANES), F32, vecL)], acc=(2, 3, 4))

    dhff = _matmul(dffo, w_dn_f, "nt", F32, "mm_down_dx")
    g_w_dn = _matmul(hff, dffo, "tn", F32, "mm_down_dw")

    slots = jnp.stack([chip, ci]).astype(jnp.int32)
    rs_groups = []

    def rs_pair_begin(parts, tag):
        lands = [lax.empty((4, p.shape[1] // 2, p.shape[2]), F32) for p in parts]
        return _split_start("rs_pair_start_" + tag, parts, lands, core, len(parts), _pair_copies)

    def rs_chips_begin(pair_state, nms, tag, after):
        parts, rbs = _split_wait("rs_pair_wait_" + tag, pair_state, after, _pair_copies)
        q16 = [_pair_add(p, rb, core, "rs_pair_add_" + nm) for p, rb, nm in zip(parts, rbs, nms)]
        state, token = _chip_scatter_start(q16, core, "rs_chips_start_" + tag)
        rs_groups.append((state, nms, tag))
        return token

    pair_a, ptok_a = rs_pair_begin([g_w_dn.reshape(4, FFN // 4, D)], "a")

    d_upall, g_ffcw = _conv_bwd_call(
        "ffn_conv_bwd", dout=dhff, dout_spec=colS(cb_ff),
        dio_shapes=[jax.ShapeDtypeStruct((2, S, FFN), BF16)],
        dio_specs=[pl.BlockSpec((2, S, cb_ff), lambda j: (0, 0, j))], dio_pack=lambda dg, du: [(dg, du)],
        dw_shape=jax.ShapeDtypeStruct((KFF, FFN), F32), dw_spec=wS(KFF, cb_ff), after=[ptok_a], **ff_args)
    tok_a = rs_chips_begin(pair_a, ["ffn_w_down"], "a", d_upall)
    dhn2 = _matmul(d_upall, w_up_f, "nt", F32, "mm_up_dx", after=[tok_a])
    g_w_up = _matmul(hn2, d_upall, "tn", F32, "mm_up_dw", out_groups=4)
    pair_b, ptok_b = rs_pair_begin([g_w_up], "b")

    def res2_bwd(a, mx, gt, g, sc, sh, dx1, dhn):
        _, vjp = jax.vjp(_f_res_normmod, a, mx, gt, g, sc, sh)
        return vjp((dx1, dhn))

    dx_r, dmix, dgt1, dg2, dsc2, dsh2 = _ew(
        res2_bwd, "norm2_bwd", (nb,), [x2, mix, gt1, norm2_g, sc2, sh2, dx1_l, dhn2],
        [rowD, rowD, vecD, vecD, vecD, vecD, rowD, rowD],
        [((S, D), F32, rowD), ((S, D), BF16, rowD)] + [((1, D), F32, vecD)] * 4, acc=(2, 3, 4, 5), after=[ptok_b])

    dmerged = _matmul(dmix, w_out_f, "nt", F32, "mm_out_dx")
    g_w_out = _matmul(merged, dmix, "tn", F32, "mm_out_dw")

    def merge_bwd(a, b, ga, gb, dm):
        _, vjp = jax.vjp(_f_merge, a, b, ga, gb)
        da, db, dga, dgb = vjp(dm)
        return da, db, jnp.concatenate([dga, dgb], axis=1)

    d_bra, d_brb, dproj = _ew(merge_bwd, "merge_bwd", (nb,), [br_a, br_b, proj, proj, dmerged],
                              [rowD, rowD, ga_spec, gb_spec, rowD],
                              [((S, D), BF16, rowD), ((S, D), BF16, rowD),
                               ((S, NA + LANES), BF16, _row(bs, 2 * D, a_ga // (2 * D)))])

    tok_b = rs_chips_begin(pair_b, ["ffn_w_up"], "b", d_brb)
    d_on = _matmul(d_bra, w_do_f, "nt", F32, "mm_dn_o_dx", after=[tok_b])
    g_w_do = _matmul(on, d_bra, "tn", F32, "mm_dn_o_dw", out_groups=4)
    d_ub = _matmul(d_brb, w_co_f, "nt", F32, "mm_cf_o_dx")
    g_w_co = _matmul(ub, d_brb, "tn", F32, "mm_cf_o_dw", out_groups=4)
    pair_c, ptok_c = rs_pair_begin([g_w_do, g_w_co, g_w_out.reshape(4, D // 4, D)], "c")

    def cf_ln_bwd(u, g, b, du):
        _, vjp = jax.vjp(_f_cf_ln, u, g, b)
        return vjp(du)

    d_uc, g_cflg, g_cflb = _ew(cf_ln_bwd, "cf_ln_bwd", (nb,), [uc, cf_ln_g, cf_ln_b, d_ub], [rowC, vecC, vecC, rowC],
                               [((S, CFW), F32, rowC), ((1, CFW), F32, vecC), ((1, CFW), F32, vecC)], acc=(1, 2),
                               after=[ptok_c])
    dproj_sds = jax.ShapeDtypeStruct((S, NA + LANES), BF16)
    dproj, g_cfcw = _conv_bwd_call(
        "cf_conv_bwd", dout=d_uc, dout_spec=colS(cb_cf), dio_shapes=[dproj_sds],
        dio_specs=[colS(2 * cb_cf, a_glu // (2 * cb_cf))], dio_pack=lambda dv, dg: [jnp.concatenate([dv, dg], axis=1)],
        dw_shape=jax.ShapeDtypeStruct((KCF, CFW), F32), dw_spec=wS(KCF, cb_cf), alias=dproj, **cf_args)

    def dn_post_bwd(o, z, g, d):
        _, vjp = jax.vjp(_f_dn_post, o, z, g)
        return vjp(d)

    tok_c = rs_chips_begin(pair_c, ["dn_w_o", "cf_w_o", "w_out"], "c", g_cfcw)
    d_o, dproj, g_dnng = _ew(dn_post_bwd, "dn_post_bwd", (nbh, NH), [o_dn, proj, dn_norm_g, d_on],
                             [o_spec, z_spec, ng_spec, oh_spec],
                             [((NH, S, DH), F32, o_spec), ((S, NA + LANES), BF16, z_spec), ((1, DH), F32, ng_spec)],
                             acc=(2,), alias=(dproj, 1), after=[tok_c])

    dqkvn, dgb_b, dbb_b = _dn_bwd_call(qkvn, gb_b, bb_b, states, d_o)
    dproj, g_dncw = _conv_bwd_call(
        "dn_conv_bwd", dout=dqkvn, dout_spec=qkv_spec, dio_shapes=[dproj_sds], dio_specs=[colS(DH)],
        dio_pack=lambda d: [d], dw_shape=jax.ShapeDtypeStruct((KDN, 3 * DNW), F32), dw_spec=wS(KDN, DH),
        alias=dproj, **dn_args)

    def gate_bwd(a, p, q, db, dg):
        lane = lax.broadcasted_iota(jnp.int32, a.shape, 1)
        d = jnp.zeros(a.shape, F32)
        for h in range(NH):
            d = d + jnp.where(lane == h, db[h], 0.0) + jnp.where(lane == NH + h, dg[h], 0.0)
        _, vjp = jax.vjp(gate_fn, a, p, q)
        return vjp(d)

    dproj, g_alp, g_dtb = _ew(gate_bwd, "dn_gate_bwd", (nb,), [proj, alp, dtb, dbb_b, dgb_b],
                              [ba_spec, vecL, vecL, hrow, hrow],
                              [((S, NA + LANES), BF16, ba_spec), ((1, LANES), F32, vecL), ((1, LANES), F32, vecL)],
                              acc=(1, 2), alias=(dproj, 0))

    g_w_aug = _matmul(hn1, dproj, "tn", F32, "mm_in_dw")
    def aug_slices(lo, n):
        out = []
        for o, a, m in sorted(segs):
            s, e = max(lo, o), min(lo + n, o + m)
            if s < e:
                out.append(g_w_aug[:, a + s - o:a + e - o])
        return out

    g_w_in = jnp.stack([jnp.concatenate(aug_slices(j * CS, CS), axis=1) for j in range(4)])
    pair_d, ptok_d = rs_pair_begin([g_w_in], "d")
    dhn1 = _matmul(dproj, w_aug, "nt", F32, "mm_in_dx", after=[ptok_d])

    def norm1_bwd(a, g, sc, sh, dhn, dxr):
        _, vjp = jax.vjp(_f_normmod, a, g, sc, sh)
        da, dg, dsc, dsh = vjp(dhn)
        return da + dxr, dg, dsc, dsh

    grad_x, dg1, dsc1, dsh1 = _ew(norm1_bwd, "norm1_bwd", (nb,), [x2, norm1_g, sc1, sh1, dhn1, dx_r],
                                  [rowD, vecD, vecD, vecD, rowD, rowD],
                                  [((S, D), F32, rowD)] + [((1, D), F32, vecD)] * 3, acc=(1, 2, 3))

    dmod = jnp.concatenate([dsh1, dsc1, dgt1, dsh2, dsc2, dgt2], axis=1)
    sm2 = [dmod, dg1, dg2, dgf, g_alp, g_dtb, g_dnng, g_cflg, g_cflb, g_dncw, g_cfcw, g_ffcw]
    sm2_shapes = [tuple(a.shape) for a in sm2]
    g3 = _all_gather_small(_pack_small(sm2))
    tok_d = rs_chips_begin(pair_d, ["w_in"], "d", g3)

    names = ["w_ada", "b_ada", "norm1_g", "w_in", "dn_conv_w", "dn_a_log", "dn_dt_bias", "dn_norm_g", "dn_w_o",
             "cf_conv_w", "cf_ln_g", "cf_ln_b", "cf_w_o", "w_out", "norm2_g", "ffn_w_up", "ffn_conv_w", "ffn_w_down",
             "final_norm_g"]
    wts = dict(zip(names, [w_ada, b_ada, norm1_g, w_in, dn_conv_w, dn_a_log, dn_dt_bias, dn_norm_g, dn_w_o, cf_conv_w,
                           cf_ln_g, cf_ln_b, cf_w_o, w_out, norm2_g, ffn_w_up, ffn_conv_w, ffn_w_down, final_norm_g]))
    ms = dict(zip(names, [m_w_ada, m_b_ada, m_norm1_g, m_w_in, m_dn_conv_w, m_dn_a_log, m_dn_dt_bias, m_dn_norm_g,
                          m_dn_w_o, m_cf_conv_w, m_cf_ln_g, m_cf_ln_b, m_cf_w_o, m_w_out, m_norm2_g, m_ffn_w_up,
                          m_ffn_conv_w, m_ffn_w_down, m_final_norm_g]))
    vs = dict(zip(names, [v_w_ada, v_b_ada, v_norm1_g, v_w_in, v_dn_conv_w, v_dn_a_log, v_dn_dt_bias, v_dn_norm_g,
                          v_dn_w_o, v_cf_conv_w, v_cf_ln_g, v_cf_ln_b, v_cf_w_o, v_w_out, v_norm2_g, v_ffn_w_up,
                          v_ffn_conv_w, v_ffn_w_down, v_final_norm_g]))
    grads, delta, new_m, new_v = {}, {}, {}, {}

    def adam_large(n, g):
        grads[n] = g.reshape(wts[n].shape)
        d_, m_, v_ = _adamw(wts[n][0], grads[n][0], ms[n][0], vs[n][0], "adamw_" + n)
        delta[n], new_m[n], new_v[n] = d_[None], m_[None], v_[None]

    def rs_finish(groups, after, tag):
        hs, nms_all = [], []
        for state, nms, t in groups:
            q16, r3s = _chip_scatter_wait(state, after, "rs_chips_wait_" + t)
            for q, r3, nm in zip(q16, r3s, nms):
                hs.append(_chip_sum(q, r3, slots, "rs_chip_sum_" + nm))
                nms_all.append(nm)
        for nm, g in zip(nms_all, _pair_gather(hs, "rs_pair_gather_" + tag)):
            adam_large(nm, g)

    rs_finish(rs_groups[:3], tok_d, "abc")

    ssum = _sum_slots(g3, "small_sum")
    dmod_all = _unpack_small(g3, sm2_shapes[:1])[0].reshape(8, 6 * D)
    (g_b_ada, gs_n1, gs_n2, gs_fn, gs_alp, gs_dtb, gs_dnng, gs_cflg, gs_cflb, gs_dncw, gs_cfcw,
     gs_ffcw) = _unpack_small(ssum, sm2_shapes)
    gs_alog, gs_dtbias = gs_alp[:, NH:2 * NH], gs_dtb[:, NH:2 * NH]
    gs_dncw = lax.dynamic_slice(gs_dncw, (0, chip * (3 * DNW // 4)), (KDN, 3 * DNW // 4))
    gs_cfcw = lax.dynamic_slice(gs_cfcw, (0, chip * (CFW // 4)), (KCF, CFW // 4))
    gs_ffcw = lax.dynamic_slice(gs_ffcw, (0, chip * (FFN // 4)), (KFF, FFN // 4))

    dmod_sh = lax.dynamic_slice(dmod_all, (0, chip * CA), (8, CA))

    def wada_fn(cc, dm):
        return (_mm(_silu(cc), dm, TN),)

    g_w_ada = _ew(wada_fn, "ada_dw", (CA // tn_a,), [c_all, dmod_sh],
                  [_full((8, D)), pl.BlockSpec((8, tn_a), lambda j: (0, j))],
                  [((D, CA), F32, pl.BlockSpec((D, tn_a), lambda j: (0, j)))])[0]

    loss = lax.psum(loss_v[0, 0], ("x", "y", "c"))

    adam_large("w_ada", g_w_ada)
    small_grads = {"b_ada": g_b_ada, "norm1_g": gs_n1, "dn_conv_w": gs_dncw, "dn_a_log": gs_alog,
                   "dn_dt_bias": gs_dtbias, "dn_norm_g": gs_dnng, "cf_conv_w": gs_cfcw, "cf_ln_g": gs_cflg,
                   "cf_ln_b": gs_cflb, "norm2_g": gs_n2, "ffn_conv_w": gs_ffcw, "final_norm_g": gs_fn}
    small = [n for n in names if n in small_grads]
    for n in small:
        grads[n] = small_grads[n].reshape(wts[n].shape)
    sm_sh = [tuple(wts[n].shape) for n in small]
    d_, m_, v_ = _adamw(_pack_small([wts[n] for n in small]), _pack_small([grads[n] for n in small]),
                        _pack_small([ms[n] for n in small]), _pack_small([vs[n] for n in small]), "adamw_small")
    for n, a, b_, c_ in zip(small, _unpack_small(d_, sm_sh), _unpack_small(m_, sm_sh), _unpack_small(v_, sm_sh)):
        delta[n], new_m[n], new_v[n] = a, b_, c_
    behind = sum(delta[n][0, :8, :LANES] for n in ["w_ada", "ffn_w_up", "ffn_w_down", "w_out", "dn_w_o", "cf_w_o"])
    rs_finish(rs_groups[3:], behind, "d")

    return (loss, grad_x[None], *[grads[n] for n in names], *[delta[n] for n in names],
            *[new_m[n] for n in names], *[new_v[n] for n in names])
```

```python
import functools

import jax
import jax.numpy as jnp
from jax import lax
from jax.experimental import pallas as pl
from jax.experimental.pallas import tpu as pltpu

F32 = jnp.float32
BF16 = jnp.bfloat16
EPS = 1e-6
LANES = 128
VMEM_LIMIT = 48 * 1024 * 1024
DN_CHUNK = 128
ADAM_LR, ADAM_B1, ADAM_B2, ADAM_EPS, ADAM_WD, ADAM_STEP = 0.001, 0.9, 0.999, 1e-08, 0.01, 10
MESH = pl.DeviceIdType.MESH

NN = (((1,), (0,)), ((), ()))
NT = (((1,), (1,)), ((), ()))
TN = (((0,), (0,)), ((), ()))
_DIMS = {"nn": NN, "nt": NT, "tn": TN}


def _mm(a, b, dims):
    return lax.dot_general(a.astype(BF16), b.astype(BF16), dims, preferred_element_type=F32)


def _mmx(a, b, dims):
    return lax.dot_general(a, b, dims, precision=lax.Precision.HIGH, preferred_element_type=F32)


def _div_tile(n, target, mult):
    best = None
    t = mult
    while t <= min(n, target):
        if n % t == 0:
            best = t
        t += mult
    return best if best is not None else n


def _params(sem=None):
    kw = dict(vmem_limit_bytes=VMEM_LIMIT)
    if sem is not None:
        kw["dimension_semantics"] = sem
    return pltpu.CompilerParams(**kw)


def _sigmoid(x):
    return jax.nn.sigmoid(x)


def _silu(x):
    return x * jax.nn.sigmoid(x)


def _softplus(x):
    return jnp.maximum(x, 0.0) + jnp.log(1.0 + jnp.exp(-jnp.abs(x)))


def _view(arr):
    if arr.ndim == 3:
        return arr.shape[1], arr.shape[0] * arr.shape[2], arr.shape[0]
    return arr.shape[0], arr.shape[1], 1


def _tile_spec(groups, cols, tr, tc, rsel, csel):
    if groups > 1:
        per = cols // groups // tc
        return pl.BlockSpec((None, tr, tc), lambda i, j, k: (csel(i, j, k) // per, rsel(i, j, k), csel(i, j, k) % per))
    return pl.BlockSpec((tr, tc), lambda i, j, k: (rsel(i, j, k), csel(i, j, k)))


def _matmul(a, b, mode, out_dtype, name, out_groups=1, after=()):
    ar, ac, ag = _view(a)
    br, bc, bg = _view(b)
    if mode == "nn":
        M, K, N = ar, ac, bc
        kdiv, mdiv, ndiv = ac // ag, M, min(bc // bg, N // out_groups)
    elif mode == "nt":
        M, K, N = ar, ac, br
        kdiv, mdiv, ndiv = min(ac // ag, bc // bg), M, N // out_groups
    else:
        K, M, N = ar, ac, bc
        kdiv, mdiv, ndiv = K, ac // ag, min(bc // bg, N // out_groups)
    tm = _div_tile(mdiv, 1024, LANES)
    tn = _div_tile(ndiv, 1536, LANES)
    tk = _div_tile(kdiv, 2048, LANES)
    nk = K // tk
    dims = _DIMS[mode]
    si, sj, sk = (lambda i, j, k: i), (lambda i, j, k: j), (lambda i, j, k: k)
    a_spec = {"nn": _tile_spec(ag, ac, tm, tk, si, sk), "nt": _tile_spec(ag, ac, tm, tk, si, sk),
              "tn": _tile_spec(ag, ac, tk, tm, sk, si)}[mode]
    b_spec = {"nn": _tile_spec(bg, bc, tk, tn, sk, sj), "nt": _tile_spec(bg, bc, tn, tk, sj, sk),
              "tn": _tile_spec(bg, bc, tk, tn, sk, sj)}[mode]
    out_shape = (M, N) if out_groups == 1 else (out_groups, M, N // out_groups)

    n_after = len(after)

    def body(*refs):
        a_ref, b_ref, o_ref = refs[0], refs[1], refs[2 + n_after]
        if nk == 1:
            o_ref[...] = lax.dot_general(a_ref[...], b_ref[...], dims, preferred_element_type=F32).astype(o_ref.dtype)
            return
        acc_ref = refs[3 + n_after]
        k = pl.program_id(2)

        @pl.when(k == 0)
        def _():
            acc_ref[...] = jnp.zeros_like(acc_ref)

        acc_ref[...] += lax.dot_general(a_ref[...], b_ref[...], dims, preferred_element_type=F32)

        @pl.when(k == nk - 1)
        def _():
            o_ref[...] = acc_ref[...].astype(o_ref.dtype)

    return pl.pallas_call(
        body, name=name, grid=(M // tm, N // tn, nk),
        in_specs=[a_spec, b_spec] + [pl.BlockSpec(memory_space=pl.ANY)] * n_after,
        out_specs=_tile_spec(out_groups, N, tm, tn, si, sj),
        out_shape=jax.ShapeDtypeStruct(out_shape, out_dtype),
        scratch_shapes=[pltpu.VMEM((tm, tn), F32)] if nk > 1 else [],
        compiler_params=_params(("parallel", "parallel", "arbitrary")),
    )(a, b, *after)


def _ew(fn, name, grid, ins, in_specs, outs, acc=(), alias=None, after=()):
    n_in = len(ins)
    n_ax = len(grid)
    extra, aliases = list(after), {}
    if alias is not None:
        extra, aliases = extra + [alias[0]], {n_in + len(after): alias[1]}
    extra_specs = [pl.BlockSpec(memory_space=pl.ANY)] * len(extra)

    def body(*refs):
        in_refs, out_refs = refs[:n_in], refs[n_in + len(extra):]
        ids = [pl.program_id(a) for a in range(n_ax)]
        res = fn(*[r[...] for r in in_refs])
        first = ids[0] == 0
        for t in ids[1:]:
            first = jnp.logical_and(first, t == 0)
        for idx, (r, val) in enumerate(zip(out_refs, res)):
            if idx in acc:
                @pl.when(first)
                def _(r=r, val=val):
                    r[...] = val.astype(r.dtype)

                @pl.when(jnp.logical_not(first))
                def _(r=r, val=val):
                    r[...] += val.astype(r.dtype)
            elif isinstance(val, tuple):
                for t, part in enumerate(val):
                    r[t] = part.astype(r.dtype)
            else:
                r[...] = val.astype(r.dtype)

    return pl.pallas_call(
        body, name=name, grid=grid, in_specs=list(in_specs) + extra_specs,
        out_specs=[o[2] for o in outs],
        out_shape=[jax.ShapeDtypeStruct(o[0], o[1]) for o in outs],
        input_output_aliases=aliases,
        compiler_params=_params(("arbitrary",) * n_ax),
    )(*ins, *extra)


def _ew_slot(fn, name, grid, slots, ins, in_specs, out_shape, out_dtype, out_spec, after=()):
    n_in = len(ins)

    def body(s_ref, *refs):
        refs[-1][...] = fn(*[r[...] for r in refs[:n_in]]).astype(refs[-1].dtype)

    return pl.pallas_call(
        body, name=name,
        grid_spec=pltpu.PrefetchScalarGridSpec(
            num_scalar_prefetch=1, grid=grid,
            in_specs=list(in_specs) + [pl.BlockSpec(memory_space=pl.ANY)] * len(after), out_specs=out_spec),
        out_shape=jax.ShapeDtypeStruct(out_shape, out_dtype),
        compiler_params=_params(("arbitrary",) * len(grid)),
    )(slots, *ins, *after)


def _row(bs, w, col=0):
    return pl.BlockSpec((bs, w), lambda i, col=col: (i, col))


def _full(shape):
    nd = len(shape)
    return pl.BlockSpec(tuple(shape), lambda *_: (0,) * nd)


def _rms(x, g):
    return x * lax.rsqrt(jnp.mean(x * x, axis=-1, keepdims=True) + EPS) * g


def _f_normmod(x, g, sc, sh):
    return _rms(x, g) * (1.0 + sc) + sh


def _f_res_normmod(x, mix, gt, g, sc, sh):
    x1 = x + gt * mix
    return x1, _f_normmod(x1, g, sc, sh)


def _f_loss(x1, f, gt, gf, tgt):
    y = _rms(x1 + gt * f, gf)
    return 0.5 * jnp.sum(jnp.mean(jnp.square(y - tgt), axis=-1))


def _f_dn_gate(nh, ba, alp, dtb):
    lane = lax.broadcasted_iota(jnp.int32, ba.shape, 1)
    m = (lane < nh).astype(F32)
    beta = _sigmoid(ba)
    g = -jnp.exp(alp) * _softplus(ba + dtb)
    return m * beta + (1.0 - m) * g


def _f_dn_post(o, z, g):
    return o * lax.rsqrt(jnp.mean(o * o, axis=-1, keepdims=True) + EPS) * g * _silu(z)


def _f_cf_ln(u, g, b):
    mu = jnp.mean(u, axis=-1, keepdims=True)
    xc = u - mu
    y = xc * lax.rsqrt(jnp.mean(xc * xc, axis=-1, keepdims=True) + EPS)
    return _silu(y * g + b)


def _f_merge(a, b, ga, gb):
    return _sigmoid(ga) * a + _sigmoid(gb) * b


def _shift_down(u, d, rows):
    if d == 0:
        return u
    return jnp.where(rows >= d, pltpu.roll(u, d, 0), 0.0)


def _shift_up(u, d, rows):
    if d == 0:
        return u
    s = u.shape[0]
    return jnp.where(rows < s - d, pltpu.roll(u, s - d, 0), 0.0)


def _conv(u, w_ref, kw, rows):
    acc = None
    for k in range(kw):
        t = w_ref[k:k + 1, :] * _shift_down(u, kw - 1 - k, rows)
        acc = t if acc is None else acc + t
    return acc


def _conv_t(dc, w_ref, kw, rows):
    acc = None
    for k in range(kw):
        t = w_ref[k:k + 1, :] * _shift_up(dc, kw - 1 - k, rows)
        acc = t if acc is None else acc + t
    return acc


def _conv_fwd_call(name, kw, ncol, ins, in_specs, extras, extra_specs, w, w_spec, pre, post, out_shape, out_spec):
    n_in, n_ex = len(ins), len(extras)

    def body(*refs):
        in_refs, ex_refs = refs[:n_in], refs[n_in:n_in + n_ex]
        w_ref, out_ref = refs[n_in + n_ex], refs[n_in + n_ex + 1]
        j = pl.program_id(0)
        u = pre(*[r[...] for r in in_refs])
        rows = lax.broadcasted_iota(jnp.int32, u.shape, 0)
        cv = _conv(u, w_ref, kw, rows)
        out_ref[...] = post(j, cv, *[r[...] for r in ex_refs]).astype(out_ref.dtype)

    return pl.pallas_call(
        body, name=name, grid=(ncol,), in_specs=list(in_specs) + list(extra_specs) + [w_spec],
        out_specs=out_spec, out_shape=out_shape, compiler_params=_params(("arbitrary",)),
    )(*ins, *extras, w)


def _conv_bwd_call(name, kw, ncol, ins, in_specs, extras, extra_specs, w, w_spec, pre, post, dout, dout_spec,
                   dio_shapes, dio_specs, dio_pack, dw_shape, dw_spec, alias=None, after=()):
    n_in, n_ex, n_io = len(ins), len(extras), len(dio_shapes)
    al, aliases = list(after), {}
    if alias is not None:
        al, aliases = al + [alias], {n_in + n_ex + 2 + len(after): 0}
    al_specs = [pl.BlockSpec(memory_space=pl.ANY)] * len(al)

    def body(*refs):
        in_refs, ex_refs = refs[:n_in], refs[n_in:n_in + n_ex]
        w_ref, dout_ref = refs[n_in + n_ex], refs[n_in + n_ex + 1]
        outs = refs[n_in + n_ex + 2 + len(al):]
        dio_refs, dw_ref = outs[:n_io], outs[n_io]
        j = pl.program_id(0)
        u, pre_vjp = jax.vjp(pre, *[r[...] for r in in_refs])
        rows = lax.broadcasted_iota(jnp.int32, u.shape, 0)
        cv = _conv(u, w_ref, kw, rows)
        _, post_vjp = jax.vjp(lambda cc, *ex: post(j, cc, *ex), cv, *[r[...] for r in ex_refs])
        g = post_vjp(dout_ref[...].astype(F32))
        dc = g[0]
        for k in range(kw):
            dw_ref[k:k + 1, :] = jnp.sum(dc * _shift_down(u, kw - 1 - k, rows), axis=0, keepdims=True)
        du = _conv_t(dc, w_ref, kw, rows)
        for r, val in zip(dio_refs, dio_pack(*pre_vjp(du), *g[1:])):
            if isinstance(val, tuple):
                for t, part in enumerate(val):
                    r[t] = part.astype(r.dtype)
            else:
                r[...] = val.astype(r.dtype)

    return pl.pallas_call(
        body, name=name, grid=(ncol,),
        in_specs=list(in_specs) + list(extra_specs) + [w_spec, dout_spec] + al_specs,
        out_specs=list(dio_specs) + [dw_spec],
        out_shape=list(dio_shapes) + [dw_shape],
        input_output_aliases=aliases,
        compiler_params=_params(("arbitrary",)),
    )(*ins, *extras, w, dout, *al)


def _tri_inverse(a):
    c = a.shape[0]
    ii = lax.broadcasted_iota(jnp.int32, (c, c), 0)
    jj = lax.broadcasted_iota(jnp.int32, (c, c), 1)
    eye = (ii == jj).astype(F32)

    def same_block(bits):
        return jnp.right_shift(ii, bits) == jnp.right_shift(jj, bits)

    d = jnp.where(same_block(3), a, 0.0)
    d2 = _mmx(d, d, NN)
    yield
    t = (eye - d) + _mmx(eye - d, d2, NN)
    d4 = _mmx(d2, d2, NN)
    yield
    t = t + _mmx(t, d4, NN)
    yield
    bits = 3
    while (1 << bits) < c:
        low = jnp.where(jnp.logical_and(same_block(bits + 1), jnp.logical_not(same_block(bits))), a, 0.0)
        tl = _mmx(t, low, NN)
        yield
        t = t - _mmx(tl, t, NN)
        yield
        bits += 1
    return t


def _lockstep(gens):
    out = [None] * len(gens)
    live = list(range(len(gens)))
    while live:
        nxt = []
        for i in live:
            try:
                next(gens[i])
                nxt.append(i)
            except StopIteration as e:
                out[i] = e.value
        live = nxt
    return out


def _dn_common(q, k, v, gb, bb):
    c = q.shape[0]
    ii = lax.broadcasted_iota(jnp.int32, (c, c), 0)
    jj = lax.broadcasted_iota(jnp.int32, (c, c), 1)
    causal = jj <= ii
    strict = jj < ii
    low = causal.astype(F32)
    ones = jnp.ones((c, LANES), F32)
    gc = _mmx(low, gb, NN)
    kb = k * bb
    vb = v * bb
    kk = _mm(kb, k, NT)
    qk = _mm(q, k, NT)
    yield
    diff = (_mmx(gc, ones, NT) - _mmx(ones, gc, NT)) * (1.0 / LANES)
    gl = jnp.sum(gb, axis=0, keepdims=True)
    eg = jnp.exp(gc)
    egm = jnp.exp(gl - gc)
    egl = jnp.exp(gl)
    kbg = kb * eg
    yield
    decay = jnp.where(causal, jnp.exp(jnp.where(causal, diff, 0.0)), 0.0)
    t = yield from _tri_inverse(jnp.where(strict, kk * decay, 0.0))
    attn = qk * decay
    return dict(causal=causal, strict=strict, low=low, ones=ones, decay=decay, eg=eg, egm=egm, egl=egl,
                kb=kb, vb=vb, kbg=kbg, kk=kk, t=t, qk=qk, attn=attn, qd=q * eg, kd=k * egm)


def _dn_fwd_stages(q, k, v, gb, bb, s):
    m = yield from _dn_common(q, k, v, gb, bb)
    r = m["vb"] - _mm(m["kbg"], s, NN)
    qs = _mm(m["qd"], s, NN)
    yield
    u = _mmx(m["t"], r, NN)
    yield
    o = qs + _mm(m["attn"], u, NN)
    s2 = s * m["egl"] + _mm(m["kd"], u, TN)
    return o, s2


def _dn_bwd_stages(q, k, v, gb, bb, s, do, dsp):
    m = yield from _dn_common(q, k, v, gb, bb)
    c = q.shape[0]
    t, decay, eg, egm, egl = m["t"], m["decay"], m["eg"], m["egm"], m["egl"]
    r = m["vb"] - _mm(m["kbg"], s, NN)
    du = _mm(m["attn"], do, TN) + _mm(m["kd"], dsp, NN)
    dqd = _mm(do, s, NT)
    ds = dsp * egl + _mm(m["qd"], do, TN)
    degl = jnp.sum(jnp.sum(dsp * s, axis=1, keepdims=True), axis=0, keepdims=True)
    yield
    u = _mmx(t, r, NN)
    dr = _mmx(t, du, TN)
    yield
    dattn = jnp.where(m["causal"], _mm(do, u, NT), 0.0)
    dkd = _mm(u, dsp, NT)
    da = jnp.where(m["strict"], -_mm(dr, u, NT), 0.0)
    dkbg = -_mm(dr, s, NT)
    ds = ds - _mm(m["kbg"], dr, TN)
    yield
    dkk = da * decay
    dqk = dattn * decay
    ddiff = (da * m["kk"] + dattn * m["qk"]) * decay
    dgc = _mmx(ddiff, m["ones"], NN) - _mmx(ddiff, m["ones"], TN)
    dkb = _mm(dkk, k, NN) + dkbg * eg
    dk = _mm(dkk, m["kb"], TN) + _mm(dqk, q, TN) + dkd * egm + dkb * bb
    dq = _mm(dqk, k, NN) + dqd * eg
    yield
    dgc = dgc + jnp.sum(dqd * q + dkbg * m["kb"], axis=-1, keepdims=True) * eg
    tt = jnp.sum(dkd * k, axis=-1, keepdims=True) * egm
    dgc = dgc - tt
    dgl = jnp.sum(tt, axis=0, keepdims=True) + degl * egl
    dbb = jnp.sum(dkb * k + dr * v, axis=-1, keepdims=True) + jnp.zeros((c, LANES), F32)
    dv = dr * bb
    dgb = _mmx(m["low"], dgc, TN) + dgl
    return dq, dk, dv, dgb, dbb, ds


def _dn_fwd_chunk(q, k, v, gb, bb, s):
    return _lockstep([_dn_fwd_stages(q, k, v, gb, bb, s)])[0]


def _dn_bwd_chunk(q, k, v, gb, bb, s, do, dsp):
    return _lockstep([_dn_bwd_stages(q, k, v, gb, bb, s, do, dsp)])[0]


def _dn_fwd_call(qkvn, gb, bb):
    _, nh, s, dh = qkvn.shape
    c = min(DN_CHUNK, s)
    n = s // c
    hb = nh

    def body(q_ref, k_ref, v_ref, g_ref, b_ref, o_ref, st_ref, s_ref):
        @pl.when(pl.program_id(1) == 0)
        def _():
            s_ref[...] = jnp.zeros_like(s_ref)

        st_ref[...] = s_ref[...]
        heads = [_dn_fwd_stages(q_ref[h], k_ref[h], v_ref[h], g_ref[h], b_ref[h], s_ref[h]) for h in range(hb)]
        for h, (o, s2) in enumerate(_lockstep(heads)):
            o_ref[h] = o
            s_ref[h] = s2

    def qspec(t):
        return pl.BlockSpec((None, hb, c, dh), lambda i, j, t=t: (t, i, j, 0))

    hs = pl.BlockSpec((hb, c, dh), lambda i, j: (i, j, 0))
    return pl.pallas_call(
        body, name="dn_fwd", grid=(nh // hb, n),
        in_specs=[qspec(0), qspec(1), qspec(2), hs, hs],
        out_specs=[hs, pl.BlockSpec((hb, None, dh, dh), lambda i, j: (i, j, 0, 0))],
        out_shape=[jax.ShapeDtypeStruct((nh, s, dh), F32), jax.ShapeDtypeStruct((nh, n, dh, dh), F32)],
        scratch_shapes=[pltpu.VMEM((hb, dh, dh), F32)],
        compiler_params=_params(("arbitrary", "arbitrary")),
    )(qkvn, qkvn, qkvn, gb, bb)


def _dn_bwd_call(qkvn, gb, bb, states, do):
    _, nh, s, dh = qkvn.shape
    c = min(DN_CHUNK, s)
    n = s // c
    hb = nh

    def body(q_ref, k_ref, v_ref, g_ref, b_ref, st_ref, do_ref, dqkv_ref, dg_ref, db_ref, ds_ref):
        @pl.when(pl.program_id(1) == 0)
        def _():
            ds_ref[...] = jnp.zeros_like(ds_ref)

        heads = [_dn_bwd_stages(q_ref[h], k_ref[h], v_ref[h], g_ref[h], b_ref[h], st_ref[h], do_ref[h], ds_ref[h])
                 for h in range(hb)]
        for h, (dq, dk, dv, dg, db, ds) in enumerate(_lockstep(heads)):
            dqkv_ref[0, h] = dq
            dqkv_ref[1, h] = dk
            dqkv_ref[2, h] = dv
            dg_ref[h] = dg
            db_ref[h] = db
            ds_ref[h] = ds

    def qspec(t):
        return pl.BlockSpec((None, hb, c, dh), lambda i, j, t=t: (t, i, n - 1 - j, 0))

    hs = pl.BlockSpec((hb, c, dh), lambda i, j: (i, n - 1 - j, 0))
    sh = jax.ShapeDtypeStruct((nh, s, dh), F32)
    return pl.pallas_call(
        body, name="dn_bwd", grid=(nh // hb, n),
        in_specs=[qspec(0), qspec(1), qspec(2), hs, hs,
                  pl.BlockSpec((hb, None, dh, dh), lambda i, j: (i, n - 1 - j, 0, 0)), hs],
        out_specs=[pl.BlockSpec((3, hb, c, dh), lambda i, j: (0, i, n - 1 - j, 0)), hs, hs],
        out_shape=[jax.ShapeDtypeStruct((3, nh, s, dh), F32), sh, sh],
        scratch_shapes=[pltpu.VMEM((hb, dh, dh), F32)],
        compiler_params=_params(("arbitrary", "arbitrary")),
    )(qkvn, qkvn, qkvn, gb, bb, states, do)


def _adamw(w, g, m, v, name):
    r, c = w.shape
    br = _div_tile(r, max(8, (1 << 18) // max(c, 1)), 8)

    def fn(w, g, m, v):
        m = ADAM_B1 * m + (1.0 - ADAM_B1) * g
        v = ADAM_B2 * v + (1.0 - ADAM_B2) * jnp.square(g)
        m_hat = m / (1.0 - ADAM_B1 ** ADAM_STEP)
        v_hat = v / (1.0 - ADAM_B2 ** ADAM_STEP)
        delta = -ADAM_LR * (m_hat / (jnp.sqrt(v_hat) + ADAM_EPS) + ADAM_WD * w)
        return delta, m, v

    spec = pl.BlockSpec((br, c), lambda i: (i, 0))
    return _ew(fn, name, (r // br,), [w, g, m, v], [spec] * 4, [((r, c), F32, spec)] * 3)


def _coords():
    return lax.axis_index("x"), lax.axis_index("y"), lax.axis_index("c")


def _all_gather_small(v, after=()):
    r, w = v.shape
    n_after = len(after)

    def body(v_ref, *rest):
        out_ref, send_sems, recv_sems, local_sem = rest[n_after:]
        x, y, c = _coords()
        me = 4 * x + 2 * y + c
        mine = pltpu.make_async_copy(v_ref, out_ref.at[me], local_sem)
        mine.start()
        peers = []
        for k in range(1, 8):
            px = 1 - x if k & 4 else x
            py = 1 - y if k & 2 else y
            pc = 1 - c if k & 1 else c
            peers.append((px, py, pc))
        sends = []
        for k, peer in enumerate(peers):
            cp = pltpu.make_async_remote_copy(src_ref=v_ref, dst_ref=out_ref.at[me], send_sem=send_sems.at[k],
                                              recv_sem=recv_sems.at[k], device_id=peer, device_id_type=MESH)
            cp.start()
            sends.append(cp)
        for k, (px, py, pc) in enumerate(peers):
            pltpu.make_async_remote_copy(src_ref=v_ref, dst_ref=out_ref.at[4 * px + 2 * py + pc],
                                         send_sem=send_sems.at[k], recv_sem=recv_sems.at[k],
                                         device_id=(px, py, pc), device_id_type=MESH).wait_recv()
        for cp in sends:
            cp.wait_send()
        mine.wait()

    return pl.pallas_call(
        body, name="ag_small", out_shape=jax.ShapeDtypeStruct((8, r, w), v.dtype),
        in_specs=[pl.BlockSpec(memory_space=pltpu.VMEM)] + [pl.BlockSpec(memory_space=pl.ANY)] * n_after,
        out_specs=pl.BlockSpec(memory_space=pltpu.VMEM),
        scratch_shapes=[pltpu.SemaphoreType.DMA((7,)), pltpu.SemaphoreType.DMA((7,)), pltpu.SemaphoreType.DMA],
        compiler_params=pltpu.CompilerParams(vmem_limit_bytes=VMEM_LIMIT),
    )(v, *after)


def _hbm_call(body, name, arrays, out_shapes, n_sems, aliases=None):
    hbm = pl.BlockSpec(memory_space=pltpu.HBM)
    return pl.pallas_call(
        body, name=name, out_shape=list(out_shapes), in_specs=[hbm] * len(arrays), out_specs=[hbm] * len(out_shapes),
        input_output_aliases=aliases or {},
        scratch_shapes=[pltpu.SemaphoreType.DMA((n_sems,)), pltpu.SemaphoreType.DMA((n_sems,))],
    )(*arrays)


def _half_rows(ref_shape, c):
    h = ref_shape[1] // 2
    return pl.ds(pl.multiple_of(c * h, 16), h), pl.ds(pl.multiple_of((1 - c) * h, 16), h)


def _pair_copies(p_refs, land_refs, send_sems, recv_sems):
    x, y, c = _coords()
    cps = []
    for w, (p, land) in enumerate(zip(p_refs, land_refs)):
        _, other = _half_rows(p.shape, c)
        cps.append(pltpu.make_async_remote_copy(src_ref=p.at[:, other], dst_ref=land, send_sem=send_sems.at[w],
                                                recv_sem=recv_sems.at[w], device_id=(x, y, 1 - c),
                                                device_id_type=MESH))
    return cps


def _pair_gather(gs, name):
    n = len(gs)

    def body(*refs):
        ins, outs, send_sems, recv_sems = refs[:n], refs[n:2 * n], refs[2 * n], refs[2 * n + 1]
        x, y, c = _coords()
        cps = []
        for w in range(n):
            cp = pltpu.make_async_remote_copy(src_ref=ins[w].at[c], dst_ref=outs[w].at[c], send_sem=send_sems.at[w],
                                              recv_sem=recv_sems.at[w], device_id=(x, y, 1 - c), device_id_type=MESH)
            cp.start()
            cps.append(cp)
        for w, cp in enumerate(cps):
            pltpu.make_async_remote_copy(src_ref=ins[w].at[c], dst_ref=outs[w].at[1 - c], send_sem=send_sems.at[w],
                                         recv_sem=recv_sems.at[w], device_id=(x, y, 1 - c),
                                         device_id_type=MESH).wait_recv()
            cp.wait_send()

    shapes = [jax.ShapeDtypeStruct(g.shape, g.dtype) for g in gs]
    return _hbm_call(body, name, gs, shapes, n, aliases={i: i for i in range(n)})


_HBM = pl.BlockSpec(memory_space=pltpu.HBM)
_SEM = pl.BlockSpec(memory_space=pltpu.SEMAPHORE)
_EFFECT = pltpu.SideEffectType.DATAFLOW_SIDE_EFFECTING


def _split_start(name, srcs, lands, after, n_copies, make_copies):
    n, m = len(srcs), len(lands)
    arrays = [pltpu.with_memory_space_constraint(a, pltpu.HBM) for a in list(srcs) + list(lands)]

    def body(*refs):
        src_refs, land_refs = refs[:n], refs[n:n + m]
        send_sems, recv_sems = refs[n + m + 1], refs[n + m + 2]
        for cp in make_copies(src_refs, land_refs, send_sems, recv_sems):
            cp.start()
        refs[-1][...] = jnp.zeros_like(refs[-1])

    outs = pl.pallas_call(
        body, name=name,
        out_shape=(pltpu.SemaphoreType.DMA((n_copies,)), pltpu.SemaphoreType.DMA((n_copies,)),
                   *[pltpu.HBM(a.shape, a.dtype) for a in arrays], jax.ShapeDtypeStruct((8, LANES), F32)),
        in_specs=[_HBM] * (n + m) + [pl.BlockSpec(memory_space=pl.ANY)],
        out_specs=(_SEM, _SEM, *[_HBM] * (n + m), pl.BlockSpec(memory_space=pltpu.VMEM)),
        input_output_aliases={i: 2 + i for i in range(n + m)},
        compiler_params=pltpu.CompilerParams(has_side_effects=_EFFECT),
    )(*arrays, after)
    return (outs[0], outs[1], list(outs[2:2 + n]), list(outs[2 + n:2 + n + m])), outs[-1]


def _split_wait(name, state, after, make_copies):
    send_sems, recv_sems, srcs, lands = state
    n, m = len(srcs), len(lands)

    def body(*refs):
        src_refs, land_refs = refs[:n], refs[n:n + m]
        for cp in make_copies(src_refs, land_refs, refs[n + m], refs[n + m + 1]):
            cp.wait_send()
            cp.wait_recv()

    outs = pl.pallas_call(
        body, name=name, out_shape=tuple(pltpu.HBM(a.shape, a.dtype) for a in srcs + lands),
        in_specs=[_HBM] * (n + m) + [_SEM, _SEM, pl.BlockSpec(memory_space=pl.ANY)], out_specs=tuple([_HBM] * (n + m)),
        input_output_aliases={i: i for i in range(n + m)},
        compiler_params=pltpu.CompilerParams(has_side_effects=_EFFECT),
    )(*srcs, *lands, send_sems, recv_sems, after)
    return list(outs[:n]), list(outs[n:])


def _scatter_copies(q_refs, land_refs, send_sems, recv_sems):
    x, y, c = _coords()
    cps = []
    for w, (q, land) in enumerate(zip(q_refs, land_refs)):
        for k, (px, py) in enumerate([(1 - x, y), (x, 1 - y), (1 - x, 1 - y)]):
            cps.append(pltpu.make_async_remote_copy(src_ref=q.at[2 * px + py], dst_ref=land.at[k],
                                                    send_sem=send_sems.at[3 * w + k], recv_sem=recv_sems.at[3 * w + k],
                                                    device_id=(px, py, c), device_id_type=MESH))
    return cps


def _chip_scatter_start(qs, after, name):
    lands = [lax.empty((3,) + q.shape[1:], q.dtype) for q in qs]
    return _split_start(name, qs, lands, after, 3 * len(qs), _scatter_copies)


def _chip_scatter_wait(state, after, name):
    return _split_wait(name, state, after, _scatter_copies)


def _gather_copies(src_refs, buf_refs, send_sems, recv_sems):
    x, y, c = _coords()
    j = 2 * x + y
    cps = []
    for w, buf in enumerate(buf_refs):
        mine, _ = _half_rows(buf.shape, c)
        for k, (px, py) in enumerate([(1 - x, y), (x, 1 - y), (1 - x, 1 - y)]):
            cps.append(pltpu.make_async_remote_copy(src_ref=buf.at[j, mine], dst_ref=buf.at[j, mine],
                                                    send_sem=send_sems.at[3 * w + k], recv_sem=recv_sems.at[3 * w + k],
                                                    device_id=(px, py, c), device_id_type=MESH))
    return cps


def _gather_wait_copies(src_refs, buf_refs, send_sems, recv_sems):
    x, y, c = _coords()
    j = 2 * x + y
    cps = []
    for w, buf in enumerate(buf_refs):
        mine, _ = _half_rows(buf.shape, c)
        for k, (px, py) in enumerate([(1 - x, y), (x, 1 - y), (1 - x, 1 - y)]):
            cps.append(pltpu.make_async_remote_copy(src_ref=buf.at[j, mine], dst_ref=buf.at[2 * px + py, mine],
                                                    send_sem=send_sems.at[3 * w + k], recv_sem=recv_sems.at[3 * w + k],
                                                    device_id=(px, py, c), device_id_type=MESH))
    return cps


def _pair_forward(bufs, name):
    n = len(bufs)

    def body(*refs):
        ins, outs, send_sems, recv_sems = refs[:n], refs[n:2 * n], refs[2 * n], refs[2 * n + 1]
        x, y, c = _coords()
        chips = [(1 - x, y), (x, 1 - y), (1 - x, 1 - y)]
        cps = []
        for w in range(n):
            mine, _ = _half_rows(outs[w].shape, c)
            for k, (px, py) in enumerate(chips):
                cp = pltpu.make_async_remote_copy(src_ref=ins[w].at[2 * px + py, mine],
                                                  dst_ref=outs[w].at[2 * px + py, mine],
                                                  send_sem=send_sems.at[3 * w + k], recv_sem=recv_sems.at[3 * w + k],
                                                  device_id=(x, y, 1 - c), device_id_type=MESH)
                cp.start()
                cps.append(cp)
        for w in range(n):
            _, sib = _half_rows(outs[w].shape, c)
            for k, (px, py) in enumerate(chips):
                pltpu.make_async_remote_copy(src_ref=ins[w].at[2 * px + py, sib], dst_ref=outs[w].at[2 * px + py, sib],
                                             send_sem=send_sems.at[3 * w + k], recv_sem=recv_sems.at[3 * w + k],
                                             device_id=(x, y, 1 - c), device_id_type=MESH).wait_recv()
        for cp in cps:
            cp.wait_send()

    shapes = [jax.ShapeDtypeStruct(b.shape, b.dtype) for b in bufs]
    return _hbm_call(body, name, bufs, shapes, 3 * n, aliases={i: i for i in range(n)})


def _slot_rows(h, cs):
    return _div_tile(h, max(16, (1 << 19) // cs), 16)


def _cast_into_slot(w, slot, name, after=()):
    r, cs = w.shape
    br = _slot_rows(r, cs)
    return _ew_slot(lambda a: a, name, (r // br,), slot, [w], [pl.BlockSpec((br, cs), lambda i, s: (i, 0))],
                    (4, r, cs), BF16, pl.BlockSpec((None, br, cs), lambda i, s: (s[0], i, 0)), after=after)


def _pair_add(p, rb, core, name):
    n, r, cs = p.shape
    h = r // 2
    br = _slot_rows(h, cs)
    nb = h // br
    return _ew_slot(lambda a, b: a + b, name, (n, nb), core, [p, rb],
                    [pl.BlockSpec((None, br, cs), lambda s, i, c: (s, c[0] * nb + i, 0)),
                     pl.BlockSpec((None, br, cs), lambda s, i, c: (s, i, 0))],
                    (n, h, cs), BF16, pl.BlockSpec((None, br, cs), lambda s, i, c: (s, i, 0)))


def _chip_sum(q, r3, slots, name):
    _, h, cs = q.shape
    br = _slot_rows(h, cs)

    def fn(a, b):
        acc = a.astype(F32)
        for k in range(3):
            acc = acc + b[k].astype(F32)
        return acc

    return _ew_slot(fn, name, (h // br,), slots, [q, r3],
                    [pl.BlockSpec((None, br, cs), lambda i, s: (s[0], i, 0)),
                     pl.BlockSpec((3, br, cs), lambda i, s: (0, i, 0))],
                    (2, h, cs), F32, pl.BlockSpec((None, br, cs), lambda i, s: (s[1], i, 0)))


def _sum_slots(r, name):
    n, h, w = r.shape
    br = _div_tile(h, 2048, 16)

    def fn(blk):
        acc = blk[0].astype(F32)
        for s in range(1, n):
            acc = acc + blk[s].astype(F32)
        return (acc,)

    return _ew(fn, name, (h // br,), [r], [pl.BlockSpec((n, br, w), lambda i: (0, i, 0))],
               [((h, w), F32, pl.BlockSpec((br, w), lambda i: (i, 0)))])[0]


def _pack_small(arrs):
    flat = jnp.concatenate([a.reshape(-1).astype(F32) for a in arrs])
    n = flat.shape[0]
    rows = -(-n // LANES)
    rows = -(-rows // 8) * 8
    return jnp.pad(flat, (0, rows * LANES - n)).reshape(rows, LANES)


def _unpack_small(p, shapes):
    lead = p.shape[:-2]
    flat = p.reshape(lead + (-1,))
    out, off = [], 0
    for sh in shapes:
        n = 1
        for d in sh:
            n *= d
        out.append(flat[..., off:off + n].reshape(lead + tuple(sh)))
        off += n
    return out


def kernel(x, c, w_ada, b_ada, norm1_g, w_in, dn_conv_w, dn_a_log, dn_dt_bias, dn_norm_g, dn_w_o, cf_conv_w, cf_ln_g, cf_ln_b, cf_w_o, w_out, norm2_g, ffn_w_up, ffn_conv_w, ffn_w_down, final_norm_g, loss_target, m_w_ada, m_b_ada, m_norm1_g, m_w_in, m_dn_conv_w, m_dn_a_log, m_dn_dt_bias, m_dn_norm_g, m_dn_w_o, m_cf_conv_w, m_cf_ln_g, m_cf_ln_b, m_cf_w_o, m_w_out, m_norm2_g, m_ffn_w_up, m_ffn_conv_w, m_ffn_w_down, m_final_norm_g, v_w_ada, v_b_ada, v_norm1_g, v_w_in, v_dn_conv_w, v_dn_a_log, v_dn_dt_bias, v_dn_norm_g, v_dn_w_o, v_cf_conv_w, v_cf_ln_g, v_cf_ln_b, v_cf_w_o, v_w_out, v_norm2_g, v_ffn_w_up, v_ffn_conv_w, v_ffn_w_down, v_final_norm_g):
    xi, yi, ci = _coords()
    chip = 2 * xi + yi
    me = 4 * xi + 2 * yi + ci
    core = jnp.reshape(ci, (1,)).astype(jnp.int32)

    S, D = x.shape[1], x.shape[2]
    NH = dn_a_log.shape[1]
    DH = dn_norm_g.shape[1]
    DNW = NH * DH
    CFW = cf_ln_g.shape[1]
    FFN = ffn_w_down.shape[1] * 4
    KDN, KCF, KFF = dn_conv_w.shape[1], cf_conv_w.shape[1], ffn_conv_w.shape[1]
    NIN = w_in.shape[2] * 4
    assert NIN == 4 * DNW + 2 * NH + 2 * CFW + 2 * D and DH == LANES and 2 * NH <= LANES
    x2, tgt = x[0], loss_target[0]

    slot_chip = jnp.reshape(chip, (1,)).astype(jnp.int32)
    big = [w_in[0], dn_w_o[0], cf_w_o[0], w_out[0], ffn_w_up[0], ffn_w_down[0]]
    names_big = ["w_in", "dn_w_o", "cf_w_o", "w_out", "ffn_w_up", "ffn_w_down"]
    in_state, in_token = _split_start("ag_in_start", [], [_cast_into_slot(big[0], slot_chip, "cast_w_in")], c, 3,
                                      _gather_copies)

    sm_shapes = [(D,), (KDN, 3 * DNW // 4), (KCF, CFW // 4), (KFF, FFN // 4)]
    g1 = _all_gather_small(_pack_small([c[0], dn_conv_w[0], cf_conv_w[0], ffn_conv_w[0]]), after=[in_token])
    c_all, dcw_s, ccw_s, fcw_s = _unpack_small(g1, sm_shapes)

    def chips_cols(t):
        t = t[0::2]
        return jnp.transpose(t, (1, 0, 2)).reshape(t.shape[1], -1)

    dn_cw, cf_cw, ff_cw = chips_cols(dcw_s), chips_cols(ccw_s), chips_cols(fcw_s)

    bufs = [_cast_into_slot(w, slot_chip, "cast_" + nm, after=[in_token]) for w, nm in zip(big[1:], names_big[1:])]
    mid_state, mid_token = _split_start("ag_mid_start", [], bufs[:3], in_token, 9, _gather_copies)
    ag_state, ag_token = _split_start("ag_late_start", [], bufs[3:], mid_token, 6, _gather_copies)

    cb_cf = _div_tile(CFW, 256, LANES)
    o_b = 4 * DNW
    o_glu = o_b + 2 * NH
    o_ga = o_glu + 2 * CFW
    NA = NIN - 2 * NH
    a_z, a_ga, a_gb, a_glu, a_ba = 3 * DNW, 4 * DNW, 4 * DNW + D, 4 * DNW + 2 * D, NA

    segs = [(0, 0, o_b), (o_b, NA, 2 * NH), (o_ga, a_ga, 2 * D)]
    for t in range(CFW // cb_cf):
        segs += [(o_glu + t * cb_cf, a_glu + 2 * t * cb_cf, cb_cf),
                 (o_glu + CFW + t * cb_cf, a_glu + (2 * t + 1) * cb_cf, cb_cf)]
    CS = NIN // 4

    def shard_slices(lo, n):
        out = []
        while n > 0:
            j, off = lo // CS, lo % CS
            m = min(n, CS - off)
            out.append(w_in_g[j][:, off:off + m])
            lo, n = lo + m, n - m
        return out

    CA = w_ada.shape[2]
    b_sh = lax.dynamic_slice(b_ada, (0, chip * CA), (1, CA))
    tn_a = _div_tile(CA, 512, LANES)

    def mod_fn(cc, w, b):
        return (_mm(_silu(cc), w, NN) + b,)

    mod_sh = _ew(mod_fn, "ada_mod", (CA // tn_a,), [c_all, w_ada[0], b_sh],
                 [_full((8, D)), pl.BlockSpec((D, tn_a), lambda j: (0, j)), pl.BlockSpec((1, tn_a), lambda j: (0, j))],
                 [((8, CA), F32, pl.BlockSpec((8, tn_a), lambda j: (0, j)))])[0]
    g2 = _all_gather_small(_pack_small([mod_sh]))
    mod_all = _unpack_small(g2, [(8, CA)])[0][0::2]
    mod_all = jnp.transpose(mod_all, (1, 0, 2)).reshape(8, 4 * CA)
    mod_me = lax.dynamic_slice(mod_all, (me, 0), (1, 6 * D))
    sh1, sc1, gt1, sh2, sc2, gt2 = [mod_me[:, i * D:(i + 1) * D] for i in range(6)]

    bs = _div_tile(S, 128, 8)
    nb = S // bs
    vecD = _full((1, D))
    rowD = _row(bs, D)

    hn1 = _ew(lambda a, g, sc, sh: (_f_normmod(a, g, sc, sh),), "norm1_fwd", (nb,),
              [x2, norm1_g, sc1, sh1], [rowD, vecD, vecD, vecD], [((S, D), BF16, rowD)],
              after=[ag_token, m_w_in[0], v_w_in[0]])[0]
    _, in_landed = _split_wait("ag_in_wait", in_state, hn1, _gather_wait_copies)
    (w_in_g,) = _pair_forward(in_landed, "ag_in_pair")
    w_aug = jnp.concatenate([p for o, a, n in sorted(segs, key=lambda s: s[1]) for p in shard_slices(o, n)]
                            + [jnp.zeros((D, LANES - 2 * NH), BF16)], axis=1)
    proj = _matmul(hn1, w_aug, "nn", F32, "mm_in")

    def dn_post(j, cv):
        s = _silu(cv)
        nrm = s * lax.rsqrt(jnp.sum(s * s, axis=-1, keepdims=True) + EPS)
        fq = (j < NH).astype(F32)
        fk = (j < 2 * NH).astype(F32)
        scale = fq * (DH ** -0.5) + (1.0 - fq)
        return fk * (nrm * scale) + (1.0 - fk) * s

    def ident(a):
        return a

    def colS(w, off=0):
        return pl.BlockSpec((S, w), lambda j, off=off: (0, j + off))

    def wS(kw, w):
        return pl.BlockSpec((kw, w), lambda j: (0, j))

    qkv_spec = pl.BlockSpec((None, None, S, DH), lambda j: (j // NH, j % NH, 0, 0))
    dn_args = dict(kw=KDN, ncol=3 * NH, ins=[proj], in_specs=[colS(DH)], extras=[], extra_specs=[],
                   w=dn_cw, w_spec=wS(KDN, DH), pre=ident, post=dn_post)
    qkvn = _conv_fwd_call("dn_conv_fwd", out_shape=jax.ShapeDtypeStruct((3, NH, S, DH), F32), out_spec=qkv_spec,
                          **dn_args)

    alp = jnp.pad(dn_a_log, ((0, 0), (NH, LANES - 2 * NH)))
    dtb = jnp.pad(dn_dt_bias, ((0, 0), (NH, LANES - 2 * NH)))
    vecL = _full((1, LANES))
    ba_spec = _row(bs, LANES, a_ba // LANES)
    rowL = _row(bs, LANES)
    gate_fn = functools.partial(_f_dn_gate, NH)
    def gate_fwd(a, p, q):
        val = gate_fn(a, p, q)
        lane = lax.broadcasted_iota(jnp.int32, val.shape, 1)

        def spread(col):
            return jnp.sum(jnp.where(lane == col, val, 0.0), axis=-1, keepdims=True) + jnp.zeros(val.shape, F32)

        return tuple(spread(h) for h in range(NH)), tuple(spread(NH + h) for h in range(NH))

    hrow = pl.BlockSpec((NH, bs, DH), lambda i: (0, i, 0))
    bb_b, gb_b = _ew(gate_fwd, "dn_gate_fwd", (nb,), [proj, alp, dtb], [ba_spec, vecL, vecL],
                     [((NH, S, DH), F32, hrow), ((NH, S, DH), F32, hrow)])
    o_dn, states = _dn_fwd_call(qkvn, gb_b, bb_b)

    bsh = _div_tile(S, 512, 8)
    nbh = S // bsh
    o_spec = pl.BlockSpec((None, bsh, DH), lambda i, h: (h, i, 0))
    z_spec = pl.BlockSpec((bsh, DH), lambda i, h: (i, a_z // DH + h))
    oh_spec = pl.BlockSpec((bsh, DH), lambda i, h: (i, h))
    ng_spec = pl.BlockSpec((1, DH), lambda i, h: (0, 0))
    on = _ew(lambda o, z, g: (_f_dn_post(o, z, g),), "dn_post_fwd", (nbh, NH), [o_dn, proj, dn_norm_g],
             [o_spec, z_spec, ng_spec], [((S, DNW), BF16, oh_spec)])[0]
    _, landed = _split_wait("ag_mid_wait", mid_state, on, _gather_wait_copies)
    w_do_f, w_co_f, w_out_g = _pair_forward(landed, "ag_mid_pair")
    w_out_f = w_out_g.reshape(-1, w_out_g.shape[2])
    br_a = _matmul(on, w_do_f, "nn", F32, "mm_dn_o")

    def glu_pre(val, gl):
        return val * _sigmoid(gl)

    def glu_spec(t):
        return pl.BlockSpec((S, cb_cf), lambda j, t=t: (0, a_glu // cb_cf + 2 * j + t))

    cf_args = dict(kw=KCF, ncol=CFW // cb_cf, ins=[proj, proj], in_specs=[glu_spec(0), glu_spec(1)],
                   extras=[], extra_specs=[], w=cf_cw, w_spec=wS(KCF, cb_cf), pre=glu_pre, post=lambda j, cv: cv)
    uc = _conv_fwd_call("cf_conv_fwd", out_shape=jax.ShapeDtypeStruct((S, CFW), F32), out_spec=colS(cb_cf), **cf_args)
    rowC = _row(bs, CFW)
    vecC = _full((1, CFW))
    ub = _ew(lambda u, g, b: (_f_cf_ln(u, g, b),), "cf_ln_fwd", (nb,), [uc, cf_ln_g, cf_ln_b], [rowC, vecC, vecC],
             [((S, CFW), BF16, rowC)])[0]
    br_b = _matmul(ub, w_co_f, "nn", F32, "mm_cf_o")

    ga_spec, gb_spec = _row(bs, D, a_ga // D), _row(bs, D, a_gb // D)
    merged = _ew(lambda a, b, ga, gb: (_f_merge(a, b, ga, gb),), "merge_fwd", (nb,), [br_a, br_b, proj, proj],
                 [rowD, rowD, ga_spec, gb_spec], [((S, D), BF16, rowD)])[0]
    mix = _matmul(merged, w_out_f, "nn", F32, "mm_out")

    x1, hn2 = _ew(_f_res_normmod, "norm2_fwd", (nb,), [x2, mix, gt1, norm2_g, sc2, sh2],
                  [rowD, rowD, vecD, vecD, vecD, vecD], [((S, D), F32, rowD), ((S, D), BF16, rowD)])
    _, landed = _split_wait("ag_late_wait", ag_state, hn2, _gather_wait_copies)
    w_up_f, w_dn_g = _pair_forward(landed, "ag_late_pair")
    w_dn_f = w_dn_g.reshape(-1, w_dn_g.shape[2])
    up_all = _matmul(hn2, w_up_f, "nn", F32, "mm_up")

    cb_ff = _div_tile(FFN, 256, LANES)
    ff_args = dict(kw=KFF, ncol=FFN // cb_ff, ins=[up_all], in_specs=[colS(cb_ff)], extras=[up_all],
                   extra_specs=[colS(cb_ff, FFN // cb_ff)], w=ff_cw, w_spec=wS(KFF, cb_ff),
                   pre=ident, post=lambda j, cv, up: _silu(cv) * up)
    hff = _conv_fwd_call("ffn_conv_fwd", out_shape=jax.ShapeDtypeStruct((S, FFN), BF16), out_spec=colS(cb_ff), **ff_args)
    ffo = _matmul(hff, w_dn_f, "nn", F32, "mm_down")

    gf2 = final_norm_g.reshape(1, D)

    def loss_bwd(a, f, gt, gf, t):
        val, vjp = jax.vjp(_f_loss, a, f, gt, gf, t)
        da, df, dgt, dgf, _ = vjp(jnp.ones((), F32))
        return da, df, dgt, dgf, jnp.zeros((1, LANES), F32) + val

    dx1_l, dffo, dgt2, dgf, loss_v = _ew(
        loss_bwd, "loss_bwd", (nb,), [x1, ffo, gt2, gf2, tgt], [rowD, rowD, vecD, vecD, rowD],
        [((S, D), F32, rowD), ((S, D), BF16, rowD), ((1, D), F32, vecD), ((1, D), F32, vecD),
         ((1, LANES), F32, vecL)], acc=(2, 3, 4))

    dhff = _matmul(dffo, w_dn_f, "nt", F32, "mm_down_dx")
    g_w_dn = _matmul(hff, dffo, "tn", F32, "mm_down_dw")

    slots = jnp.stack([chip, ci]).astype(jnp.int32)
    rs_groups = []

    def rs_pair_begin(parts, tag):
        lands = [lax.empty((4, p.shape[1] // 2, p.shape[2]), F32) for p in parts]
        return _split_start("rs_pair_start_" + tag, parts, lands, core, len(parts), _pair_copies)

    def rs_chips_begin(pair_state, nms, tag, after):
        parts, rbs = _split_wait("rs_pair_wait_" + tag, pair_state, after, _pair_copies)
        q16 = [_pair_add(p, rb, core, "rs_pair_add_" + nm) for p, rb, nm in zip(parts, rbs, nms)]
        state, token = _chip_scatter_start(q16, core, "rs_chips_start_" + tag)
        rs_groups.append((state, nms, tag))
        return token

    pair_a, ptok_a = rs_pair_begin([g_w_dn.reshape(4, FFN // 4, D)], "a")

    d_upall, g_ffcw = _conv_bwd_call(
        "ffn_conv_bwd", dout=dhff, dout_spec=colS(cb_ff),
        dio_shapes=[jax.ShapeDtypeStruct((2, S, FFN), BF16)],
        dio_specs=[pl.BlockSpec((2, S, cb_ff), lambda j: (0, 0, j))], dio_pack=lambda dg, du: [(dg, du)],
        dw_shape=jax.ShapeDtypeStruct((KFF, FFN), F32), dw_spec=wS(KFF, cb_ff), after=[ptok_a], **ff_args)
    tok_a = rs_chips_begin(pair_a, ["ffn_w_down"], "a", d_upall)
    dhn2 = _matmul(d_upall, w_up_f, "nt", F32, "mm_up_dx", after=[tok_a])
    g_w_up = _matmul(hn2, d_upall, "tn", F32, "mm_up_dw", out_groups=4)
    pair_b, ptok_b = rs_pair_begin([g_w_up], "b")

    def res2_bwd(a, mx, gt, g, sc, sh, dx1, dhn):
        _, vjp = jax.vjp(_f_res_normmod, a, mx, gt, g, sc, sh)
        return vjp((dx1, dhn))

    dx_r, dmix, dgt1, dg2, dsc2, dsh2 = _ew(
        res2_bwd, "norm2_bwd", (nb,), [x2, mix, gt1, norm2_g, sc2, sh2, dx1_l, dhn2],
        [rowD, rowD, vecD, vecD, vecD, vecD, rowD, rowD],
        [((S, D), F32, rowD), ((S, D), BF16, rowD)] + [((1, D), F32, vecD)] * 4, acc=(2, 3, 4, 5), after=[ptok_b])

    dmerged = _matmul(dmix, w_out_f, "nt", F32, "mm_out_dx")
    g_w_out = _matmul(merged, dmix, "tn", F32, "mm_out_dw")

    def merge_bwd(a, b, ga, gb, dm):
        _, vjp = jax.vjp(_f_merge, a, b, ga, gb)
        da, db, dga, dgb = vjp(dm)
        return da, db, jnp.concatenate([dga, dgb], axis=1)

    d_bra, d_brb, dproj = _ew(merge_bwd, "merge_bwd", (nb,), [br_a, br_b, proj, proj, dmerged],
                              [rowD, rowD, ga_spec, gb_spec, rowD],
                              [((S, D), BF16, rowD), ((S, D), BF16, rowD),
                               ((S, NA + LANES), BF16, _row(bs, 2 * D, a_ga // (2 * D)))])

    tok_b = rs_chips_begin(pair_b, ["ffn_w_up"], "b", d_brb)
    d_on = _matmul(d_bra, w_do_f, "nt", F32, "mm_dn_o_dx", after=[tok_b])
    g_w_do = _matmul(on, d_bra, "tn", F32, "mm_dn_o_dw", out_groups=4)
    d_ub = _matmul(d_brb, w_co_f, "nt", F32, "mm_cf_o_dx")
    g_w_co = _matmul(ub, d_brb, "tn", F32, "mm_cf_o_dw", out_groups=4)
    pair_c, ptok_c = rs_pair_begin([g_w_do, g_w_co, g_w_out.reshape(4, D // 4, D)], "c")

    def cf_ln_bwd(u, g, b, du):
        _, vjp = jax.vjp(_f_cf_ln, u, g, b)
        return vjp(du)

    d_uc, g_cflg, g_cflb = _ew(cf_ln_bwd, "cf_ln_bwd", (nb,), [uc, cf_ln_g, cf_ln_b, d_ub], [rowC, vecC, vecC, rowC],
                               [((S, CFW), F32, rowC), ((1, CFW), F32, vecC), ((1, CFW), F32, vecC)], acc=(1, 2),
                               after=[ptok_c])
    dproj_sds = jax.ShapeDtypeStruct((S, NA + LANES), BF16)
    dproj, g_cfcw = _conv_bwd_call(
        "cf_conv_bwd", dout=d_uc, dout_spec=colS(cb_cf), dio_shapes=[dproj_sds],
        dio_specs=[colS(2 * cb_cf, a_glu // (2 * cb_cf))], dio_pack=lambda dv, dg: [jnp.concatenate([dv, dg], axis=1)],
        dw_shape=jax.ShapeDtypeStruct((KCF, CFW), F32), dw_spec=wS(KCF, cb_cf), alias=dproj, **cf_args)

    def dn_post_bwd(o, z, g, d):
        _, vjp = jax.vjp(_f_dn_post, o, z, g)
        return vjp(d)

    tok_c = rs_chips_begin(pair_c, ["dn_w_o", "cf_w_o", "w_out"], "c", g_cfcw)
    d_o, dproj, g_dnng = _ew(dn_post_bwd, "dn_post_bwd", (nbh, NH), [o_dn, proj, dn_norm_g, d_on],
                             [o_spec, z_spec, ng_spec, oh_spec],
                             [((NH, S, DH), F32, o_spec), ((S, NA + LANES), BF16, z_spec), ((1, DH), F32, ng_spec)],
                             acc=(2,), alias=(dproj, 1), after=[tok_c])

    dqkvn, dgb_b, dbb_b = _dn_bwd_call(qkvn, gb_b, bb_b, states, d_o)
    dproj, g_dncw = _conv_bwd_call(
        "dn_conv_bwd", dout=dqkvn, dout_spec=qkv_spec, dio_shapes=[dproj_sds], dio_specs=[colS(DH)],
        dio_pack=lambda d: [d], dw_shape=jax.ShapeDtypeStruct((KDN, 3 * DNW), F32), dw_spec=wS(KDN, DH),
        alias=dproj, **dn_args)

    def gate_bwd(a, p, q, db, dg):
        lane = lax.broadcasted_iota(jnp.int32, a.shape, 1)
        d = jnp.zeros(a.shape, F32)
        for h in range(NH):
            d = d + jnp.where(lane == h, db[h], 0.0) + jnp.where(lane == NH + h, dg[h], 0.0)
        _, vjp = jax.vjp(gate_fn, a, p, q)
        return vjp(d)

    dproj, g_alp, g_dtb = _ew(gate_bwd, "dn_gate_bwd", (nb,), [proj, alp, dtb, dbb_b, dgb_b],
                              [ba_spec, vecL, vecL, hrow, hrow],
                              [((S, NA + LANES), BF16, ba_spec), ((1, LANES), F32, vecL), ((1, LANES), F32, vecL)],
                              acc=(1, 2), alias=(dproj, 0))

    g_w_aug = _matmul(hn1, dproj, "tn", F32, "mm_in_dw")
    def aug_slices(lo, n):
        out = []
        for o, a, m in sorted(segs):
            s, e = max(lo, o), min(lo + n, o + m)
            if s < e:
                out.append(g_w_aug[:, a + s - o:a + e - o])
        return out

    g_w_in = jnp.stack([jnp.concatenate(aug_slices(j * CS, CS), axis=1) for j in range(4)])
    pair_d, ptok_d = rs_pair_begin([g_w_in], "d")
    dhn1 = _matmul(dproj, w_aug, "nt", F32, "mm_in_dx", after=[ptok_d])

    def norm1_bwd(a, g, sc, sh, dhn, dxr):
        _, vjp = jax.vjp(_f_normmod, a, g, sc, sh)
        da, dg, dsc, dsh = vjp(dhn)
        return da + dxr, dg, dsc, dsh

    grad_x, dg1, dsc1, dsh1 = _ew(norm1_bwd, "norm1_bwd", (nb,), [x2, norm1_g, sc1, sh1, dhn1, dx_r],
                                  [rowD, vecD, vecD, vecD, rowD, rowD],
                                  [((S, D), F32, rowD)] + [((1, D), F32, vecD)] * 3, acc=(1, 2, 3))

    dmod = jnp.concatenate([dsh1, dsc1, dgt1, dsh2, dsc2, dgt2], axis=1)
    sm2 = [dmod, dg1, dg2, dgf, g_alp, g_dtb, g_dnng, g_cflg, g_cflb, g_dncw, g_cfcw, g_ffcw]
    sm2_shapes = [tuple(a.shape) for a in sm2]
    g3 = _all_gather_small(_pack_small(sm2))
    tok_d = rs_chips_begin(pair_d, ["w_in"], "d", g3)

    names = ["w_ada", "b_ada", "norm1_g", "w_in", "dn_conv_w", "dn_a_log", "dn_dt_bias", "dn_norm_g", "dn_w_o",
             "cf_conv_w", "cf_ln_g", "cf_ln_b", "cf_w_o", "w_out", "norm2_g", "ffn_w_up", "ffn_conv_w", "ffn_w_down",
             "final_norm_g"]
    wts = dict(zip(names, [w_ada, b_ada, norm1_g, w_in, dn_conv_w, dn_a_log, dn_dt_bias, dn_norm_g, dn_w_o, cf_conv_w,
                           cf_ln_g, cf_ln_b, cf_w_o, w_out, norm2_g, ffn_w_up, ffn_conv_w, ffn_w_down, final_norm_g]))
    ms = dict(zip(names, [m_w_ada, m_b_ada, m_norm1_g, m_w_in, m_dn_conv_w, m_dn_a_log, m_dn_dt_bias, m_dn_norm_g,
                          m_dn_w_o, m_cf_conv_w, m_cf_ln_g, m_cf_ln_b, m_cf_w_o, m_w_out, m_norm2_g, m_ffn_w_up,
                          m_ffn_conv_w, m_ffn_w_down, m_final_norm_g]))
    vs = dict(zip(names, [v_w_ada, v_b_ada, v_norm1_g, v_w_in, v_dn_conv_w, v_dn_a_log, v_dn_dt_bias, v_dn_norm_g,
                          v_dn_w_o, v_cf_conv_w, v_cf_ln_g, v_cf_ln_b, v_cf_w_o, v_w_out, v_norm2_g, v_ffn_w_up,
                          v_ffn_conv_w, v_ffn_w_down, v_final_norm_g]))
    grads, delta, new_m, new_v = {}, {}, {}, {}

    def adam_large(n, g):
        grads[n] = g.reshape(wts[n].shape)
        d_, m_, v_ = _adamw(wts[n][0], grads[n][0], ms[n][0], vs[n][0], "adamw_" + n)
        delta[n], new_m[n], new_v[n] = d_[None], m_[None], v_[None]

    def rs_finish(groups, after, tag):
        hs, nms_all = [], []
        for state, nms, t in groups:
            q16, r3s = _chip_scatter_wait(state, after, "rs_chips_wait_" + t)
            for q, r3, nm in zip(q16, r3s, nms):
                hs.append(_chip_sum(q, r3, slots, "rs_chip_sum_" + nm))
                nms_all.append(nm)
        for nm, g in zip(nms_all, _pair_gather(hs, "rs_pair_gather_" + tag)):
            adam_large(nm, g)

    rs_finish(rs_groups[:3], tok_d, "abc")

    ssum = _sum_slots(g3, "small_sum")
    dmod_all = _unpack_small(g3, sm2_shapes[:1])[0].reshape(8, 6 * D)
    (g_b_ada, gs_n1, gs_n2, gs_fn, gs_alp, gs_dtb, gs_dnng, gs_cflg, gs_cflb, gs_dncw, gs_cfcw,
     gs_ffcw) = _unpack_small(ssum, sm2_shapes)
    gs_alog, gs_dtbias = gs_alp[:, NH:2 * NH], gs_dtb[:, NH:2 * NH]
    gs_dncw = lax.dynamic_slice(gs_dncw, (0, chip * (3 * DNW // 4)), (KDN, 3 * DNW // 4))
    gs_cfcw = lax.dynamic_slice(gs_cfcw, (0, chip * (CFW // 4)), (KCF, CFW // 4))
    gs_ffcw = lax.dynamic_slice(gs_ffcw, (0, chip * (FFN // 4)), (KFF, FFN // 4))

    dmod_sh = lax.dynamic_slice(dmod_all, (0, chip * CA), (8, CA))

    def wada_fn(cc, dm):
        return (_mm(_silu(cc), dm, TN),)

    g_w_ada = _ew(wada_fn, "ada_dw", (CA // tn_a,), [c_all, dmod_sh],
                  [_full((8, D)), pl.BlockSpec((8, tn_a), lambda j: (0, j))],
                  [((D, CA), F32, pl.BlockSpec((D, tn_a), lambda j: (0, j)))])[0]

    loss = lax.psum(loss_v[0, 0], ("x", "y", "c"))

    adam_large("w_ada", g_w_ada)
    small_grads = {"b_ada": g_b_ada, "norm1_g": gs_n1, "dn_conv_w": gs_dncw, "dn_a_log": gs_alog,
                   "dn_dt_bias": gs_dtbias, "dn_norm_g": gs_dnng, "cf_conv_w": gs_cfcw, "cf_ln_g": gs_cflg,
                   "cf_ln_b": gs_cflb, "norm2_g": gs_n2, "ffn_conv_w": gs_ffcw, "final_norm_g": gs_fn}
    small = [n for n in names if n in small_grads]
    for n in small:
        grads[n] = small_grads[n].reshape(wts[n].shape)
    sm_sh = [tuple(wts[n].shape) for n in small]
    d_, m_, v_ = _adamw(_pack_small([wts[n] for n in small]), _pack_small([grads[n] for n in small]),
                        _pack_small([ms[n] for n in small]), _pack_small([vs[n] for n in small]), "adamw_small")
    for n, a, b_, c_ in zip(small, _unpack_small(d_, sm_sh), _unpack_small(m_, sm_sh), _unpack_small(v_, sm_sh)):
        delta[n], new_m[n], new_v[n] = a, b_, c_
    behind = sum(delta[n][0, :8, :8] for n in ["w_ada", "ffn_w_up", "ffn_w_down", "w_out", "dn_w_o", "cf_w_o"])
    rs_finish(rs_groups[3:], behind, "d")

    return (loss, grad_x[None], *[grads[n] for n in names], *[delta[n] for n in names],
            *[new_m[n] for n in names], *[new_v[n] for n in names])
```

```python
import functools

import jax
import jax.numpy as jnp
from jax import lax
from jax.experimental import pallas as pl
from jax.experimental.pallas import tpu as pltpu

F32 = jnp.float32
BF16 = jnp.bfloat16
EPS = 1e-6
LANES = 128
VMEM_LIMIT = 48 * 1024 * 1024
DN_CHUNK = 128
ADAM_LR, ADAM_B1, ADAM_B2, ADAM_EPS, ADAM_WD, ADAM_STEP = 0.001, 0.9, 0.999, 1e-08, 0.01, 10
MESH = pl.DeviceIdType.MESH

NN = (((1,), (0,)), ((), ()))
NT = (((1,), (1,)), ((), ()))
TN = (((0,), (0,)), ((), ()))
_DIMS = {"nn": NN, "nt": NT, "tn": TN}


def _mm(a, b, dims):
    return lax.dot_general(a.astype(BF16), b.astype(BF16), dims, preferred_element_type=F32)


def _mmx(a, b, dims):
    return lax.dot_general(a, b, dims, precision=lax.Precision.HIGH, preferred_element_type=F32)


def _div_tile(n, target, mult):
    best = None
    t = mult
    while t <= min(n, target):
        if n % t == 0:
            best = t
        t += mult
    return best if best is not None else n


def _params(sem=None):
    kw = dict(vmem_limit_bytes=VMEM_LIMIT)
    if sem is not None:
        kw["dimension_semantics"] = sem
    return pltpu.CompilerParams(**kw)


def _sigmoid(x):
    return jax.nn.sigmoid(x)


def _silu(x):
    return x * jax.nn.sigmoid(x)


def _softplus(x):
    return jnp.maximum(x, 0.0) + jnp.log(1.0 + jnp.exp(-jnp.abs(x)))


def _view(arr):
    if arr.ndim == 3:
        return arr.shape[1], arr.shape[0] * arr.shape[2], arr.shape[0]
    return arr.shape[0], arr.shape[1], 1


def _tile_spec(groups, cols, tr, tc, rsel, csel):
    if groups > 1:
        per = cols // groups // tc
        return pl.BlockSpec((None, tr, tc), lambda i, j, k: (csel(i, j, k) // per, rsel(i, j, k), csel(i, j, k) % per))
    return pl.BlockSpec((tr, tc), lambda i, j, k: (rsel(i, j, k), csel(i, j, k)))


def _matmul(a, b, mode, out_dtype, name, out_groups=1, after=()):
    ar, ac, ag = _view(a)
    br, bc, bg = _view(b)
    if mode == "nn":
        M, K, N = ar, ac, bc
        kdiv, mdiv, ndiv = ac // ag, M, min(bc // bg, N // out_groups)
    elif mode == "nt":
        M, K, N = ar, ac, br
        kdiv, mdiv, ndiv = min(ac // ag, bc // bg), M, N // out_groups
    else:
        K, M, N = ar, ac, bc
        kdiv, mdiv, ndiv = K, ac // ag, min(bc // bg, N // out_groups)
    tm = _div_tile(mdiv, 1024, LANES)
    tn = _div_tile(ndiv, 1536, LANES)
    tk = _div_tile(kdiv, 2048, LANES)
    nk = K // tk
    dims = _DIMS[mode]
    si, sj, sk = (lambda i, j, k: i), (lambda i, j, k: j), (lambda i, j, k: k)
    a_spec = {"nn": _tile_spec(ag, ac, tm, tk, si, sk), "nt": _tile_spec(ag, ac, tm, tk, si, sk),
              "tn": _tile_spec(ag, ac, tk, tm, sk, si)}[mode]
    b_spec = {"nn": _tile_spec(bg, bc, tk, tn, sk, sj), "nt": _tile_spec(bg, bc, tn, tk, sj, sk),
              "tn": _tile_spec(bg, bc, tk, tn, sk, sj)}[mode]
    out_shape = (M, N) if out_groups == 1 else (out_groups, M, N // out_groups)

    n_after = len(after)

    def body(*refs):
        a_ref, b_ref, o_ref = refs[0], refs[1], refs[2 + n_after]
        if nk == 1:
            o_ref[...] = lax.dot_general(a_ref[...], b_ref[...], dims, preferred_element_type=F32).astype(o_ref.dtype)
            return
        acc_ref = refs[3 + n_after]
        k = pl.program_id(2)

        @pl.when(k == 0)
        def _():
            acc_ref[...] = jnp.zeros_like(acc_ref)

        acc_ref[...] += lax.dot_general(a_ref[...], b_ref[...], dims, preferred_element_type=F32)

        @pl.when(k == nk - 1)
        def _():
            o_ref[...] = acc_ref[...].astype(o_ref.dtype)

    return pl.pallas_call(
        body, name=name, grid=(M // tm, N // tn, nk),
        in_specs=[a_spec, b_spec] + [pl.BlockSpec(memory_space=pl.ANY)] * n_after,
        out_specs=_tile_spec(out_groups, N, tm, tn, si, sj),
        out_shape=jax.ShapeDtypeStruct(out_shape, out_dtype),
        scratch_shapes=[pltpu.VMEM((tm, tn), F32)] if nk > 1 else [],
        compiler_params=_params(("parallel", "parallel", "arbitrary")),
    )(a, b, *after)


def _ew(fn, name, grid, ins, in_specs, outs, acc=(), alias=None, after=()):
    n_in = len(ins)
    n_ax = len(grid)
    extra, aliases = list(after), {}
    if alias is not None:
        extra, aliases = extra + [alias[0]], {n_in + len(after): alias[1]}
    extra_specs = [pl.BlockSpec(memory_space=pl.ANY)] * len(extra)

    def body(*refs):
        in_refs, out_refs = refs[:n_in], refs[n_in + len(extra):]
        ids = [pl.program_id(a) for a in range(n_ax)]
        res = fn(*[r[...] for r in in_refs])
        first = ids[0] == 0
        for t in ids[1:]:
            first = jnp.logical_and(first, t == 0)
        for idx, (r, val) in enumerate(zip(out_refs, res)):
            if idx in acc:
                @pl.when(first)
                def _(r=r, val=val):
                    r[...] = val.astype(r.dtype)

                @pl.when(jnp.logical_not(first))
                def _(r=r, val=val):
                    r[...] += val.astype(r.dtype)
            elif isinstance(val, tuple):
                for t, part in enumerate(val):
                    r[t] = part.astype(r.dtype)
            else:
                r[...] = val.astype(r.dtype)

    return pl.pallas_call(
        body, name=name, grid=grid, in_specs=list(in_specs) + extra_specs,
        out_specs=[o[2] for o in outs],
        out_shape=[jax.ShapeDtypeStruct(o[0], o[1]) for o in outs],
        input_output_aliases=aliases,
        compiler_params=_params(("arbitrary",) * n_ax),
    )(*ins, *extra)


def _ew_slot(fn, name, grid, slots, ins, in_specs, out_shape, out_dtype, out_spec, after=()):
    n_in = len(ins)

    def body(s_ref, *refs):
        refs[-1][...] = fn(*[r[...] for r in refs[:n_in]]).astype(refs[-1].dtype)

    return pl.pallas_call(
        body, name=name,
        grid_spec=pltpu.PrefetchScalarGridSpec(
            num_scalar_prefetch=1, grid=grid,
            in_specs=list(in_specs) + [pl.BlockSpec(memory_space=pl.ANY)] * len(after), out_specs=out_spec),
        out_shape=jax.ShapeDtypeStruct(out_shape, out_dtype),
        compiler_params=_params(("arbitrary",) * len(grid)),
    )(slots, *ins, *after)


def _row(bs, w, col=0):
    return pl.BlockSpec((bs, w), lambda i, col=col: (i, col))


def _full(shape):
    nd = len(shape)
    return pl.BlockSpec(tuple(shape), lambda *_: (0,) * nd)


def _rms(x, g):
    return x * lax.rsqrt(jnp.mean(x * x, axis=-1, keepdims=True) + EPS) * g


def _f_normmod(x, g, sc, sh):
    return _rms(x, g) * (1.0 + sc) + sh


def _f_res_normmod(x, mix, gt, g, sc, sh):
    x1 = x + gt * mix
    return x1, _f_normmod(x1, g, sc, sh)


def _f_loss(x1, f, gt, gf, tgt):
    y = _rms(x1 + gt * f, gf)
    return 0.5 * jnp.sum(jnp.mean(jnp.square(y - tgt), axis=-1))


def _f_dn_gate(nh, ba, alp, dtb):
    lane = lax.broadcasted_iota(jnp.int32, ba.shape, 1)
    m = (lane < nh).astype(F32)
    beta = _sigmoid(ba)
    g = -jnp.exp(alp) * _softplus(ba + dtb)
    return m * beta + (1.0 - m) * g


def _f_dn_post(o, z, g):
    return o * lax.rsqrt(jnp.mean(o * o, axis=-1, keepdims=True) + EPS) * g * _silu(z)


def _f_cf_ln(u, g, b):
    mu = jnp.mean(u, axis=-1, keepdims=True)
    xc = u - mu
    y = xc * lax.rsqrt(jnp.mean(xc * xc, axis=-1, keepdims=True) + EPS)
    return _silu(y * g + b)


def _f_merge(a, b, ga, gb):
    return _sigmoid(ga) * a + _sigmoid(gb) * b


def _shift_down(u, d, rows):
    if d == 0:
        return u
    return jnp.where(rows >= d, pltpu.roll(u, d, 0), 0.0)


def _shift_up(u, d, rows):
    if d == 0:
        return u
    s = u.shape[0]
    return jnp.where(rows < s - d, pltpu.roll(u, s - d, 0), 0.0)


def _conv(u, w_ref, kw, rows):
    acc = None
    for k in range(kw):
        t = w_ref[k:k + 1, :] * _shift_down(u, kw - 1 - k, rows)
        acc = t if acc is None else acc + t
    return acc


def _conv_t(dc, w_ref, kw, rows):
    acc = None
    for k in range(kw):
        t = w_ref[k:k + 1, :] * _shift_up(dc, kw - 1 - k, rows)
        acc = t if acc is None else acc + t
    return acc


def _conv_fwd_call(name, kw, ncol, ins, in_specs, extras, extra_specs, w, w_spec, pre, post, out_shape, out_spec):
    n_in, n_ex = len(ins), len(extras)

    def body(*refs):
        in_refs, ex_refs = refs[:n_in], refs[n_in:n_in + n_ex]
        w_ref, out_ref = refs[n_in + n_ex], refs[n_in + n_ex + 1]
        j = pl.program_id(0)
        u = pre(*[r[...] for r in in_refs])
        rows = lax.broadcasted_iota(jnp.int32, u.shape, 0)
        cv = _conv(u, w_ref, kw, rows)
        out_ref[...] = post(j, cv, *[r[...] for r in ex_refs]).astype(out_ref.dtype)

    return pl.pallas_call(
        body, name=name, grid=(ncol,), in_specs=list(in_specs) + list(extra_specs) + [w_spec],
        out_specs=out_spec, out_shape=out_shape, compiler_params=_params(("arbitrary",)),
    )(*ins, *extras, w)


def _conv_bwd_call(name, kw, ncol, ins, in_specs, extras, extra_specs, w, w_spec, pre, post, dout, dout_spec,
                   dio_shapes, dio_specs, dio_pack, dw_shape, dw_spec, alias=None, after=()):
    n_in, n_ex, n_io = len(ins), len(extras), len(dio_shapes)
    al, aliases = list(after), {}
    if alias is not None:
        al, aliases = al + [alias], {n_in + n_ex + 2 + len(after): 0}
    al_specs = [pl.BlockSpec(memory_space=pl.ANY)] * len(al)

    def body(*refs):
        in_refs, ex_refs = refs[:n_in], refs[n_in:n_in + n_ex]
        w_ref, dout_ref = refs[n_in + n_ex], refs[n_in + n_ex + 1]
        outs = refs[n_in + n_ex + 2 + len(al):]
        dio_refs, dw_ref = outs[:n_io], outs[n_io]
        j = pl.program_id(0)
        u, pre_vjp = jax.vjp(pre, *[r[...] for r in in_refs])
        rows = lax.broadcasted_iota(jnp.int32, u.shape, 0)
        cv = _conv(u, w_ref, kw, rows)
        _, post_vjp = jax.vjp(lambda cc, *ex: post(j, cc, *ex), cv, *[r[...] for r in ex_refs])
        g = post_vjp(dout_ref[...].astype(F32))
        dc = g[0]
        for k in range(kw):
            dw_ref[k:k + 1, :] = jnp.sum(dc * _shift_down(u, kw - 1 - k, rows), axis=0, keepdims=True)
        du = _conv_t(dc, w_ref, kw, rows)
        for r, val in zip(dio_refs, dio_pack(*pre_vjp(du), *g[1:])):
            if isinstance(val, tuple):
                for t, part in enumerate(val):
                    r[t] = part.astype(r.dtype)
            else:
                r[...] = val.astype(r.dtype)

    return pl.pallas_call(
        body, name=name, grid=(ncol,),
        in_specs=list(in_specs) + list(extra_specs) + [w_spec, dout_spec] + al_specs,
        out_specs=list(dio_specs) + [dw_spec],
        out_shape=list(dio_shapes) + [dw_shape],
        input_output_aliases=aliases,
        compiler_params=_params(("arbitrary",)),
    )(*ins, *extras, w, dout, *al)


def _tri_inverse(a):
    c = a.shape[0]
    ii = lax.broadcasted_iota(jnp.int32, (c, c), 0)
    jj = lax.broadcasted_iota(jnp.int32, (c, c), 1)
    eye = (ii == jj).astype(F32)

    def same_block(bits):
        return jnp.right_shift(ii, bits) == jnp.right_shift(jj, bits)

    d = jnp.where(same_block(3), a, 0.0)
    d2 = _mmx(d, d, NN)
    yield
    t = (eye - d) + _mmx(eye - d, d2, NN)
    d4 = _mmx(d2, d2, NN)
    yield
    t = t + _mmx(t, d4, NN)
    yield
    bits = 3
    while (1 << bits) < c:
        low = jnp.where(jnp.logical_and(same_block(bits + 1), jnp.logical_not(same_block(bits))), a, 0.0)
        tl = _mmx(t, low, NN)
        yield
        t = t - _mmx(tl, t, NN)
        yield
        bits += 1
    return t


def _lockstep(gens):
    out = [None] * len(gens)
    live = list(range(len(gens)))
    while live:
        nxt = []
        for i in live:
            try:
                next(gens[i])
                nxt.append(i)
            except StopIteration as e:
                out[i] = e.value
        live = nxt
    return out


def _dn_common(q, k, v, gb, bb):
    c = q.shape[0]
    ii = lax.broadcasted_iota(jnp.int32, (c, c), 0)
    jj = lax.broadcasted_iota(jnp.int32, (c, c), 1)
    causal = jj <= ii
    strict = jj < ii
    low = causal.astype(F32)
    ones = jnp.ones((c, LANES), F32)
    gc = _mmx(low, gb, NN)
    kb = k * bb
    vb = v * bb
    kk = _mm(kb, k, NT)
    qk = _mm(q, k, NT)
    yield
    diff = (_mmx(gc, ones, NT) - _mmx(ones, gc, NT)) * (1.0 / LANES)
    gl = jnp.sum(gb, axis=0, keepdims=True)
    eg = jnp.exp(gc)
    egm = jnp.exp(gl - gc)
    egl = jnp.exp(gl)
    kbg = kb * eg
    yield
    decay = jnp.where(causal, jnp.exp(jnp.where(causal, diff, 0.0)), 0.0)
    t = yield from _tri_inverse(jnp.where(strict, kk * decay, 0.0))
    attn = qk * decay
    return dict(causal=causal, strict=strict, low=low, ones=ones, decay=decay, eg=eg, egm=egm, egl=egl,
                kb=kb, vb=vb, kbg=kbg, kk=kk, t=t, qk=qk, attn=attn, qd=q * eg, kd=k * egm)


def _dn_fwd_stages(q, k, v, gb, bb, s):
    m = yield from _dn_common(q, k, v, gb, bb)
    r = m["vb"] - _mm(m["kbg"], s, NN)
    qs = _mm(m["qd"], s, NN)
    yield
    u = _mmx(m["t"], r, NN)
    yield
    o = qs + _mm(m["attn"], u, NN)
    s2 = s * m["egl"] + _mm(m["kd"], u, TN)
    return o, s2


def _dn_bwd_stages(q, k, v, gb, bb, s, do, dsp):
    m = yield from _dn_common(q, k, v, gb, bb)
    c = q.shape[0]
    t, decay, eg, egm, egl = m["t"], m["decay"], m["eg"], m["egm"], m["egl"]
    r = m["vb"] - _mm(m["kbg"], s, NN)
    du = _mm(m["attn"], do, TN) + _mm(m["kd"], dsp, NN)
    dqd = _mm(do, s, NT)
    ds = dsp * egl + _mm(m["qd"], do, TN)
    degl = jnp.sum(jnp.sum(dsp * s, axis=1, keepdims=True), axis=0, keepdims=True)
    yield
    u = _mmx(t, r, NN)
    dr = _mmx(t, du, TN)
    yield
    dattn = jnp.where(m["causal"], _mm(do, u, NT), 0.0)
    dkd = _mm(u, dsp, NT)
    da = jnp.where(m["strict"], -_mm(dr, u, NT), 0.0)
    dkbg = -_mm(dr, s, NT)
    ds = ds - _mm(m["kbg"], dr, TN)
    yield
    dkk = da * decay
    dqk = dattn * decay
    ddiff = (da * m["kk"] + dattn * m["qk"]) * decay
    dgc = _mmx(ddiff, m["ones"], NN) - _mmx(ddiff, m["ones"], TN)
    dkb = _mm(dkk, k, NN) + dkbg * eg
    dk = _mm(dkk, m["kb"], TN) + _mm(dqk, q, TN) + dkd * egm + dkb * bb
    dq = _mm(dqk, k, NN) + dqd * eg
    yield
    dgc = dgc + jnp.sum(dqd * q + dkbg * m["kb"], axis=-1, keepdims=True) * eg
    tt = jnp.sum(dkd * k, axis=-1, keepdims=True) * egm
    dgc = dgc - tt
    dgl = jnp.sum(tt, axis=0, keepdims=True) + degl * egl
    dbb = jnp.sum(dkb * k + dr * v, axis=-1, keepdims=True) + jnp.zeros((c, LANES), F32)
    dv = dr * bb
    dgb = _mmx(m["low"], dgc, TN) + dgl
    return dq, dk, dv, dgb, dbb, ds


def _dn_fwd_chunk(q, k, v, gb, bb, s):
    return _lockstep([_dn_fwd_stages(q, k, v, gb, bb, s)])[0]


def _dn_bwd_chunk(q, k, v, gb, bb, s, do, dsp):
    return _lockstep([_dn_bwd_stages(q, k, v, gb, bb, s, do, dsp)])[0]


def _dn_fwd_call(qkvn, gb, bb):
    _, nh, s, dh = qkvn.shape
    c = min(DN_CHUNK, s)
    n = s // c
    hb = nh

    def body(q_ref, k_ref, v_ref, g_ref, b_ref, o_ref, st_ref, s_ref):
        @pl.when(pl.program_id(1) == 0)
        def _():
            s_ref[...] = jnp.zeros_like(s_ref)

        st_ref[...] = s_ref[...]
        heads = [_dn_fwd_stages(q_ref[h], k_ref[h], v_ref[h], g_ref[h], b_ref[h], s_ref[h]) for h in range(hb)]
        for h, (o, s2) in enumerate(_lockstep(heads)):
            o_ref[h] = o
            s_ref[h] = s2

    def qspec(t):
        return pl.BlockSpec((None, hb, c, dh), lambda i, j, t=t: (t, i, j, 0))

    hs = pl.BlockSpec((hb, c, dh), lambda i, j: (i, j, 0))
    return pl.pallas_call(
        body, name="dn_fwd", grid=(nh // hb, n),
        in_specs=[qspec(0), qspec(1), qspec(2), hs, hs],
        out_specs=[hs, pl.BlockSpec((hb, None, dh, dh), lambda i, j: (i, j, 0, 0))],
        out_shape=[jax.ShapeDtypeStruct((nh, s, dh), F32), jax.ShapeDtypeStruct((nh, n, dh, dh), F32)],
        scratch_shapes=[pltpu.VMEM((hb, dh, dh), F32)],
        compiler_params=_params(("arbitrary", "arbitrary")),
    )(qkvn, qkvn, qkvn, gb, bb)


def _dn_bwd_call(qkvn, gb, bb, states, do):
    _, nh, s, dh = qkvn.shape
    c = min(DN_CHUNK, s)
    n = s // c
    hb = nh

    def body(q_ref, k_ref, v_ref, g_ref, b_ref, st_ref, do_ref, dqkv_ref, dg_ref, db_ref, ds_ref):
        @pl.when(pl.program_id(1) == 0)
        def _():
            ds_ref[...] = jnp.zeros_like(ds_ref)

        heads = [_dn_bwd_stages(q_ref[h], k_ref[h], v_ref[h], g_ref[h], b_ref[h], st_ref[h], do_ref[h], ds_ref[h])
                 for h in range(hb)]
        for h, (dq, dk, dv, dg, db, ds) in enumerate(_lockstep(heads)):
            dqkv_ref[0, h] = dq
            dqkv_ref[1, h] = dk
            dqkv_ref[2, h] = dv
            dg_ref[h] = dg
            db_ref[h] = db
            ds_ref[h] = ds

    def qspec(t):
        return pl.BlockSpec((None, hb, c, dh), lambda i, j, t=t: (t, i, n - 1 - j, 0))

    hs = pl.BlockSpec((hb, c, dh), lambda i, j: (i, n - 1 - j, 0))
    sh = jax.ShapeDtypeStruct((nh, s, dh), F32)
    return pl.pallas_call(
        body, name="dn_bwd", grid=(nh // hb, n),
        in_specs=[qspec(0), qspec(1), qspec(2), hs, hs,
                  pl.BlockSpec((hb, None, dh, dh), lambda i, j: (i, n - 1 - j, 0, 0)), hs],
        out_specs=[pl.BlockSpec((3, hb, c, dh), lambda i, j: (0, i, n - 1 - j, 0)), hs, hs],
        out_shape=[jax.ShapeDtypeStruct((3, nh, s, dh), F32), sh, sh],
        scratch_shapes=[pltpu.VMEM((hb, dh, dh), F32)],
        compiler_params=_params(("arbitrary", "arbitrary")),
    )(qkvn, qkvn, qkvn, gb, bb, states, do)


def _adamw(w, g, m, v, name):
    r, c = w.shape
    br = _div_tile(r, max(8, (1 << 18) // max(c, 1)), 8)

    def fn(w, g, m, v):
        m = ADAM_B1 * m + (1.0 - ADAM_B1) * g
        v = ADAM_B2 * v + (1.0 - ADAM_B2) * jnp.square(g)
        m_hat = m / (1.0 - ADAM_B1 ** ADAM_STEP)
        v_hat = v / (1.0 - ADAM_B2 ** ADAM_STEP)
        delta = -ADAM_LR * (m_hat / (jnp.sqrt(v_hat) + ADAM_EPS) + ADAM_WD * w)
        return delta, m, v

    spec = pl.BlockSpec((br, c), lambda i: (i, 0))
    return _ew(fn, name, (r // br,), [w, g, m, v], [spec] * 4, [((r, c), F32, spec)] * 3)


def _coords():
    return lax.axis_index("x"), lax.axis_index("y"), lax.axis_index("c")


def _all_gather_small(v, after=()):
    r, w = v.shape
    n_after = len(after)

    def body(v_ref, *rest):
        out_ref, send_sems, recv_sems, local_sem = rest[n_after:]
        x, y, c = _coords()
        me = 4 * x + 2 * y + c
        mine = pltpu.make_async_copy(v_ref, out_ref.at[me], local_sem)
        mine.start()
        peers = []
        for k in range(1, 8):
            px = 1 - x if k & 4 else x
            py = 1 - y if k & 2 else y
            pc = 1 - c if k & 1 else c
            peers.append((px, py, pc))
        sends = []
        for k, peer in enumerate(peers):
            cp = pltpu.make_async_remote_copy(src_ref=v_ref, dst_ref=out_ref.at[me], send_sem=send_sems.at[k],
                                              recv_sem=recv_sems.at[k], device_id=peer, device_id_type=MESH)
            cp.start()
            sends.append(cp)
        for k, (px, py, pc) in enumerate(peers):
            pltpu.make_async_remote_copy(src_ref=v_ref, dst_ref=out_ref.at[4 * px + 2 * py + pc],
                                         send_sem=send_sems.at[k], recv_sem=recv_sems.at[k],
                                         device_id=(px, py, pc), device_id_type=MESH).wait_recv()
        for cp in sends:
            cp.wait_send()
        mine.wait()

    return pl.pallas_call(
        body, name="ag_small", out_shape=jax.ShapeDtypeStruct((8, r, w), v.dtype),
        in_specs=[pl.BlockSpec(memory_space=pltpu.VMEM)] + [pl.BlockSpec(memory_space=pl.ANY)] * n_after,
        out_specs=pl.BlockSpec(memory_space=pltpu.VMEM),
        scratch_shapes=[pltpu.SemaphoreType.DMA((7,)), pltpu.SemaphoreType.DMA((7,)), pltpu.SemaphoreType.DMA],
        compiler_params=pltpu.CompilerParams(vmem_limit_bytes=VMEM_LIMIT),
    )(v, *after)


def _hbm_call(body, name, arrays, out_shapes, n_sems, aliases=None):
    hbm = pl.BlockSpec(memory_space=pltpu.HBM)
    return pl.pallas_call(
        body, name=name, out_shape=list(out_shapes), in_specs=[hbm] * len(arrays), out_specs=[hbm] * len(out_shapes),
        input_output_aliases=aliases or {},
        scratch_shapes=[pltpu.SemaphoreType.DMA((n_sems,)), pltpu.SemaphoreType.DMA((n_sems,))],
    )(*arrays)


def _half_rows(ref_shape, c):
    h = ref_shape[1] // 2
    return pl.ds(pl.multiple_of(c * h, 16), h), pl.ds(pl.multiple_of((1 - c) * h, 16), h)


def _pair_copies(p_refs, land_refs, send_sems, recv_sems):
    x, y, c = _coords()
    cps = []
    for w, (p, land) in enumerate(zip(p_refs, land_refs)):
        _, other = _half_rows(p.shape, c)
        cps.append(pltpu.make_async_remote_copy(src_ref=p.at[:, other], dst_ref=land, send_sem=send_sems.at[w],
                                                recv_sem=recv_sems.at[w], device_id=(x, y, 1 - c),
                                                device_id_type=MESH))
    return cps


def _pair_gather(gs, name):
    n = len(gs)

    def body(*refs):
        ins, outs, send_sems, recv_sems = refs[:n], refs[n:2 * n], refs[2 * n], refs[2 * n + 1]
        x, y, c = _coords()
        cps = []
        for w in range(n):
            cp = pltpu.make_async_remote_copy(src_ref=ins[w].at[c], dst_ref=outs[w].at[c], send_sem=send_sems.at[w],
                                              recv_sem=recv_sems.at[w], device_id=(x, y, 1 - c), device_id_type=MESH)
            cp.start()
            cps.append(cp)
        for w, cp in enumerate(cps):
            pltpu.make_async_remote_copy(src_ref=ins[w].at[c], dst_ref=outs[w].at[1 - c], send_sem=send_sems.at[w],
                                         recv_sem=recv_sems.at[w], device_id=(x, y, 1 - c),
                                         device_id_type=MESH).wait_recv()
            cp.wait_send()

    shapes = [jax.ShapeDtypeStruct(g.shape, g.dtype) for g in gs]
    return _hbm_call(body, name, gs, shapes, n, aliases={i: i for i in range(n)})


_HBM = pl.BlockSpec(memory_space=pltpu.HBM)
_SEM = pl.BlockSpec(memory_space=pltpu.SEMAPHORE)
_EFFECT = pltpu.SideEffectType.DATAFLOW_SIDE_EFFECTING


def _split_start(name, srcs, lands, after, n_copies, make_copies):
    n, m = len(srcs), len(lands)
    arrays = [pltpu.with_memory_space_constraint(a, pltpu.HBM) for a in list(srcs) + list(lands)]

    def body(*refs):
        src_refs, land_refs = refs[:n], refs[n:n + m]
        send_sems, recv_sems = refs[n + m + 1], refs[n + m + 2]
        for cp in make_copies(src_refs, land_refs, send_sems, recv_sems):
            cp.start()
        refs[-1][...] = jnp.zeros_like(refs[-1])

    outs = pl.pallas_call(
        body, name=name,
        out_shape=(pltpu.SemaphoreType.DMA((n_copies,)), pltpu.SemaphoreType.DMA((n_copies,)),
                   *[pltpu.HBM(a.shape, a.dtype) for a in arrays], jax.ShapeDtypeStruct((8, LANES), F32)),
        in_specs=[_HBM] * (n + m) + [pl.BlockSpec(memory_space=pl.ANY)],
        out_specs=(_SEM, _SEM, *[_HBM] * (n + m), pl.BlockSpec(memory_space=pltpu.VMEM)),
        input_output_aliases={i: 2 + i for i in range(n + m)},
        compiler_params=pltpu.CompilerParams(has_side_effects=_EFFECT),
    )(*arrays, after)
    return (outs[0], outs[1], list(outs[2:2 + n]), list(outs[2 + n:2 + n + m])), outs[-1]


def _split_wait(name, state, after, make_copies):
    send_sems, recv_sems, srcs, lands = state
    n, m = len(srcs), len(lands)

    def body(*refs):
        src_refs, land_refs = refs[:n], refs[n:n + m]
        for cp in make_copies(src_refs, land_refs, refs[n + m], refs[n + m + 1]):
            cp.wait_send()
            cp.wait_recv()

    outs = pl.pallas_call(
        body, name=name, out_shape=tuple(pltpu.HBM(a.shape, a.dtype) for a in srcs + lands),
        in_specs=[_HBM] * (n + m) + [_SEM, _SEM, pl.BlockSpec(memory_space=pl.ANY)], out_specs=tuple([_HBM] * (n + m)),
        input_output_aliases={i: i for i in range(n + m)},
        compiler_params=pltpu.CompilerParams(has_side_effects=_EFFECT),
    )(*srcs, *lands, send_sems, recv_sems, after)
    return list(outs[:n]), list(outs[n:])


def _scatter_copies(q_refs, land_refs, send_sems, recv_sems):
    x, y, c = _coords()
    cps = []
    for w, (q, land) in enumerate(zip(q_refs, land_refs)):
        for k, (px, py) in enumerate([(1 - x, y), (x, 1 - y), (1 - x, 1 - y)]):
            cps.append(pltpu.make_async_remote_copy(src_ref=q.at[2 * px + py], dst_ref=land.at[k],
                                                    send_sem=send_sems.at[3 * w + k], recv_sem=recv_sems.at[3 * w + k],
                                                    device_id=(px, py, c), device_id_type=MESH))
    return cps


def _chip_scatter_start(qs, after, name):
    lands = [lax.empty((3,) + q.shape[1:], q.dtype) for q in qs]
    return _split_start(name, qs, lands, after, 3 * len(qs), _scatter_copies)


def _chip_scatter_wait(state, after, name):
    return _split_wait(name, state, after, _scatter_copies)


def _gather_copies(src_refs, buf_refs, send_sems, recv_sems):
    x, y, c = _coords()
    j = 2 * x + y
    cps = []
    for w, buf in enumerate(buf_refs):
        mine, _ = _half_rows(buf.shape, c)
        for k, (px, py) in enumerate([(1 - x, y), (x, 1 - y), (1 - x, 1 - y)]):
            cps.append(pltpu.make_async_remote_copy(src_ref=buf.at[j, mine], dst_ref=buf.at[j, mine],
                                                    send_sem=send_sems.at[3 * w + k], recv_sem=recv_sems.at[3 * w + k],
                                                    device_id=(px, py, c), device_id_type=MESH))
    return cps


def _gather_wait_copies(src_refs, buf_refs, send_sems, recv_sems):
    x, y, c = _coords()
    j = 2 * x + y
    cps = []
    for w, buf in enumerate(buf_refs):
        mine, _ = _half_rows(buf.shape, c)
        for k, (px, py) in enumerate([(1 - x, y), (x, 1 - y), (1 - x, 1 - y)]):
            cps.append(pltpu.make_async_remote_copy(src_ref=buf.at[j, mine], dst_ref=buf.at[2 * px + py, mine],
                                                    send_sem=send_sems.at[3 * w + k], recv_sem=recv_sems.at[3 * w + k],
                                                    device_id=(px, py, c), device_id_type=MESH))
    return cps


def _pair_forward(bufs, name):
    n = len(bufs)

    def body(*refs):
        ins, outs, send_sems, recv_sems = refs[:n], refs[n:2 * n], refs[2 * n], refs[2 * n + 1]
        x, y, c = _coords()
        chips = [(1 - x, y), (x, 1 - y), (1 - x, 1 - y)]
        cps = []
        for w in range(n):
            mine, _ = _half_rows(outs[w].shape, c)
            for k, (px, py) in enumerate(chips):
                cp = pltpu.make_async_remote_copy(src_ref=ins[w].at[2 * px + py, mine],
                                                  dst_ref=outs[w].at[2 * px + py, mine],
                                                  send_sem=send_sems.at[3 * w + k], recv_sem=recv_sems.at[3 * w + k],
                                                  device_id=(x, y, 1 - c), device_id_type=MESH)
                cp.start()
                cps.append(cp)
        for w in range(n):
            _, sib = _half_rows(outs[w].shape, c)
            for k, (px, py) in enumerate(chips):
                pltpu.make_async_remote_copy(src_ref=ins[w].at[2 * px + py, sib], dst_ref=outs[w].at[2 * px + py, sib],
                                             send_sem=send_sems.at[3 * w + k], recv_sem=recv_sems.at[3 * w + k],
                                             device_id=(x, y, 1 - c), device_id_type=MESH).wait_recv()
        for cp in cps:
            cp.wait_send()

    shapes = [jax.ShapeDtypeStruct(b.shape, b.dtype) for b in bufs]
    return _hbm_call(body, name, bufs, shapes, 3 * n, aliases={i: i for i in range(n)})


def _slot_rows(h, cs):
    return _div_tile(h, max(16, (1 << 19) // cs), 16)


def _cast_into_slot(w, slot, name, after=()):
    r, cs = w.shape
    br = _slot_rows(r, cs)
    return _ew_slot(lambda a: a, name, (r // br,), slot, [w], [pl.BlockSpec((br, cs), lambda i, s: (i, 0))],
                    (4, r, cs), BF16, pl.BlockSpec((None, br, cs), lambda i, s: (s[0], i, 0)), after=after)


def _pair_add(p, rb, core, name):
    n, r, cs = p.shape
    h = r // 2
    br = _slot_rows(h, cs)
    nb = h // br
    return _ew_slot(lambda a, b: a + b, name, (n, nb), core, [p, rb],
                    [pl.BlockSpec((None, br, cs), lambda s, i, c: (s, c[0] * nb + i, 0)),
                     pl.BlockSpec((None, br, cs), lambda s, i, c: (s, i, 0))],
                    (n, h, cs), BF16, pl.BlockSpec((None, br, cs), lambda s, i, c: (s, i, 0)))


def _chip_sum(q, r3, slots, name):
    _, h, cs = q.shape
    br = _slot_rows(h, cs)

    def fn(a, b):
        acc = a.astype(F32)
        for k in range(3):
            acc = acc + b[k].astype(F32)
        return acc

    return _ew_slot(fn, name, (h // br,), slots, [q, r3],
                    [pl.BlockSpec((None, br, cs), lambda i, s: (s[0], i, 0)),
                     pl.BlockSpec((3, br, cs), lambda i, s: (0, i, 0))],
                    (2, h, cs), F32, pl.BlockSpec((None, br, cs), lambda i, s: (s[1], i, 0)))


def _sum_slots(r, name):
    n, h, w = r.shape
    br = _div_tile(h, 2048, 16)

    def fn(blk):
        acc = blk[0].astype(F32)
        for s in range(1, n):
            acc = acc + blk[s].astype(F32)
        return (acc,)

    return _ew(fn, name, (h // br,), [r], [pl.BlockSpec((n, br, w), lambda i: (0, i, 0))],
               [((h, w), F32, pl.BlockSpec((br, w), lambda i: (i, 0)))])[0]


def _pack_small(arrs):
    flat = jnp.concatenate([a.reshape(-1).astype(F32) for a in arrs])
    n = flat.shape[0]
    rows = -(-n // LANES)
    rows = -(-rows // 8) * 8
    return jnp.pad(flat, (0, rows * LANES - n)).reshape(rows, LANES)


def _unpack_small(p, shapes):
    lead = p.shape[:-2]
    flat = p.reshape(lead + (-1,))
    out, off = [], 0
    for sh in shapes:
        n = 1
        for d in sh:
            n *= d
        out.append(flat[..., off:off + n].reshape(lead + tuple(sh)))
        off += n
    return out


def kernel(x, c, w_ada, b_ada, norm1_g, w_in, dn_conv_w, dn_a_log, dn_dt_bias, dn_norm_g, dn_w_o, cf_conv_w, cf_ln_g, cf_ln_b, cf_w_o, w_out, norm2_g, ffn_w_up, ffn_conv_w, ffn_w_down, final_norm_g, loss_target, m_w_ada, m_b_ada, m_norm1_g, m_w_in, m_dn_conv_w, m_dn_a_log, m_dn_dt_bias, m_dn_norm_g, m_dn_w_o, m_cf_conv_w, m_cf_ln_g, m_cf_ln_b, m_cf_w_o, m_w_out, m_norm2_g, m_ffn_w_up, m_ffn_conv_w, m_ffn_w_down, m_final_norm_g, v_w_ada, v_b_ada, v_norm1_g, v_w_in, v_dn_conv_w, v_dn_a_log, v_dn_dt_bias, v_dn_norm_g, v_dn_w_o, v_cf_conv_w, v_cf_ln_g, v_cf_ln_b, v_cf_w_o, v_w_out, v_norm2_g, v_ffn_w_up, v_ffn_conv_w, v_ffn_w_down, v_final_norm_g):
    xi, yi, ci = _coords()
    chip = 2 * xi + yi
    me = 4 * xi + 2 * yi + ci
    core = jnp.reshape(ci, (1,)).astype(jnp.int32)

    S, D = x.shape[1], x.shape[2]
    NH = dn_a_log.shape[1]
    DH = dn_norm_g.shape[1]
    DNW = NH * DH
    CFW = cf_ln_g.shape[1]
    FFN = ffn_w_down.shape[1] * 4
    KDN, KCF, KFF = dn_conv_w.shape[1], cf_conv_w.shape[1], ffn_conv_w.shape[1]
    NIN = w_in.shape[2] * 4
    assert NIN == 4 * DNW + 2 * NH + 2 * CFW + 2 * D and DH == LANES and 2 * NH <= LANES
    x2, tgt = x[0], loss_target[0]

    slot_chip = jnp.reshape(chip, (1,)).astype(jnp.int32)
    big = [w_in[0], dn_w_o[0], cf_w_o[0], w_out[0], ffn_w_up[0], ffn_w_down[0]]
    names_big = ["w_in", "dn_w_o", "cf_w_o", "w_out", "ffn_w_up", "ffn_w_down"]
    buf_in = _cast_into_slot(big[0], slot_chip, "cast_w_in")

    sm_shapes = [(D,), (KDN, 3 * DNW // 4), (KCF, CFW // 4), (KFF, FFN // 4)]
    g1 = _all_gather_small(_pack_small([c[0], dn_conv_w[0], cf_conv_w[0], ffn_conv_w[0]]))
    c_all, dcw_s, ccw_s, fcw_s = _unpack_small(g1, sm_shapes)

    def chips_cols(t):
        t = t[0::2]
        return jnp.transpose(t, (1, 0, 2)).reshape(t.shape[1], -1)

    dn_cw, cf_cw, ff_cw = chips_cols(dcw_s), chips_cols(ccw_s), chips_cols(fcw_s)


    cb_cf = _div_tile(CFW, 256, LANES)
    o_b = 4 * DNW
    o_glu = o_b + 2 * NH
    o_ga = o_glu + 2 * CFW
    NA = NIN - 2 * NH
    a_z, a_ga, a_gb, a_glu, a_ba = 3 * DNW, 4 * DNW, 4 * DNW + D, 4 * DNW + 2 * D, NA

    segs = [(0, 0, o_b), (o_b, NA, 2 * NH), (o_ga, a_ga, 2 * D)]
    for t in range(CFW // cb_cf):
        segs += [(o_glu + t * cb_cf, a_glu + 2 * t * cb_cf, cb_cf),
                 (o_glu + CFW + t * cb_cf, a_glu + (2 * t + 1) * cb_cf, cb_cf)]
    CS = NIN // 4

    def shard_slices(lo, n):
        out = []
        while n > 0:
            j, off = lo // CS, lo % CS
            m = min(n, CS - off)
            out.append(w_in_g[j][:, off:off + m])
            lo, n = lo + m, n - m
        return out

    CA = w_ada.shape[2]
    b_sh = lax.dynamic_slice(b_ada, (0, chip * CA), (1, CA))
    tn_a = _div_tile(CA, 512, LANES)

    def mod_fn(cc, w, b):
        return (_mm(_silu(cc), w, NN) + b,)

    mod_sh = _ew(mod_fn, "ada_mod", (CA // tn_a,), [c_all, w_ada[0], b_sh],
                 [_full((8, D)), pl.BlockSpec((D, tn_a), lambda j: (0, j)), pl.BlockSpec((1, tn_a), lambda j: (0, j))],
                 [((8, CA), F32, pl.BlockSpec((8, tn_a), lambda j: (0, j)))])[0]
    g2 = _all_gather_small(_pack_small([mod_sh]))
    in_state, in_token = _split_start("ag_in_start", [], [buf_in], g2, 3, _gather_copies)
    bufs = [_cast_into_slot(w, slot_chip, "cast_" + nm, after=[in_token]) for w, nm in zip(big[1:], names_big[1:])]
    mid_state, mid_token = _split_start("ag_mid_start", [], bufs[:3], in_token, 9, _gather_copies)
    ag_state, ag_token = _split_start("ag_late_start", [], bufs[3:], mid_token, 6, _gather_copies)
    mod_all = _unpack_small(g2, [(8, CA)])[0][0::2]
    mod_all = jnp.transpose(mod_all, (1, 0, 2)).reshape(8, 4 * CA)
    mod_me = lax.dynamic_slice(mod_all, (me, 0), (1, 6 * D))
    sh1, sc1, gt1, sh2, sc2, gt2 = [mod_me[:, i * D:(i + 1) * D] for i in range(6)]

    bs = _div_tile(S, 128, 8)
    nb = S // bs
    vecD = _full((1, D))
    rowD = _row(bs, D)

    hn1 = _ew(lambda a, g, sc, sh: (_f_normmod(a, g, sc, sh),), "norm1_fwd", (nb,),
              [x2, norm1_g, sc1, sh1], [rowD, vecD, vecD, vecD], [((S, D), BF16, rowD)],
              after=[ag_token, m_w_in[0], v_w_in[0]])[0]
    _, in_landed = _split_wait("ag_in_wait", in_state, hn1, _gather_wait_copies)
    (w_in_g,) = _pair_forward(in_landed, "ag_in_pair")
    w_aug = jnp.concatenate([p for o, a, n in sorted(segs, key=lambda s: s[1]) for p in shard_slices(o, n)]
                            + [jnp.zeros((D, LANES - 2 * NH), BF16)], axis=1)
    proj = _matmul(hn1, w_aug, "nn", F32, "mm_in")

    def dn_post(j, cv):
        s = _silu(cv)
        nrm = s * lax.rsqrt(jnp.sum(s * s, axis=-1, keepdims=True) + EPS)
        fq = (j < NH).astype(F32)
        fk = (j < 2 * NH).astype(F32)
        scale = fq * (DH ** -0.5) + (1.0 - fq)
        return fk * (nrm * scale) + (1.0 - fk) * s

    def ident(a):
        return a

    def colS(w, off=0):
        return pl.BlockSpec((S, w), lambda j, off=off: (0, j + off))

    def wS(kw, w):
        return pl.BlockSpec((kw, w), lambda j: (0, j))

    qkv_spec = pl.BlockSpec((None, None, S, DH), lambda j: (j // NH, j % NH, 0, 0))
    dn_args = dict(kw=KDN, ncol=3 * NH, ins=[proj], in_specs=[colS(DH)], extras=[], extra_specs=[],
                   w=dn_cw, w_spec=wS(KDN, DH), pre=ident, post=dn_post)
    qkvn = _conv_fwd_call("dn_conv_fwd", out_shape=jax.ShapeDtypeStruct((3, NH, S, DH), F32), out_spec=qkv_spec,
                          **dn_args)

    alp = jnp.pad(dn_a_log, ((0, 0), (NH, LANES - 2 * NH)))
    dtb = jnp.pad(dn_dt_bias, ((0, 0), (NH, LANES - 2 * NH)))
    vecL = _full((1, LANES))
    ba_spec = _row(bs, LANES, a_ba // LANES)
    rowL = _row(bs, LANES)
    gate_fn = functools.partial(_f_dn_gate, NH)
    def gate_fwd(a, p, q):
        val = gate_fn(a, p, q)
        lane = lax.broadcasted_iota(jnp.int32, val.shape, 1)

        def spread(col):
            return jnp.sum(jnp.where(lane == col, val, 0.0), axis=-1, keepdims=True) + jnp.zeros(val.shape, F32)

        return tuple(spread(h) for h in range(NH)), tuple(spread(NH + h) for h in range(NH))

    hrow = pl.BlockSpec((NH, bs, DH), lambda i: (0, i, 0))
    bb_b, gb_b = _ew(gate_fwd, "dn_gate_fwd", (nb,), [proj, alp, dtb], [ba_spec, vecL, vecL],
                     [((NH, S, DH), F32, hrow), ((NH, S, DH), F32, hrow)])
    o_dn, states = _dn_fwd_call(qkvn, gb_b, bb_b)

    bsh = _div_tile(S, 512, 8)
    nbh = S // bsh
    o_spec = pl.BlockSpec((None, bsh, DH), lambda i, h: (h, i, 0))
    z_spec = pl.BlockSpec((bsh, DH), lambda i, h: (i, a_z // DH + h))
    oh_spec = pl.BlockSpec((bsh, DH), lambda i, h: (i, h))
    ng_spec = pl.BlockSpec((1, DH), lambda i, h: (0, 0))
    on = _ew(lambda o, z, g: (_f_dn_post(o, z, g),), "dn_post_fwd", (nbh, NH), [o_dn, proj, dn_norm_g],
             [o_spec, z_spec, ng_spec], [((S, DNW), BF16, oh_spec)])[0]
    _, landed = _split_wait("ag_mid_wait", mid_state, on, _gather_wait_copies)
    w_do_f, w_co_f, w_out_g = _pair_forward(landed, "ag_mid_pair")
    w_out_f = w_out_g.reshape(-1, w_out_g.shape[2])
    br_a = _matmul(on, w_do_f, "nn", F32, "mm_dn_o")

    def glu_pre(val, gl):
        return val * _sigmoid(gl)

    def glu_spec(t):
        return pl.BlockSpec((S, cb_cf), lambda j, t=t: (0, a_glu // cb_cf + 2 * j + t))

    cf_args = dict(kw=KCF, ncol=CFW // cb_cf, ins=[proj, proj], in_specs=[glu_spec(0), glu_spec(1)],
                   extras=[], extra_specs=[], w=cf_cw, w_spec=wS(KCF, cb_cf), pre=glu_pre, post=lambda j, cv: cv)
    uc = _conv_fwd_call("cf_conv_fwd", out_shape=jax.ShapeDtypeStruct((S, CFW), F32), out_spec=colS(cb_cf), **cf_args)
    rowC = _row(bs, CFW)
    vecC = _full((1, CFW))
    ub = _ew(lambda u, g, b: (_f_cf_ln(u, g, b),), "cf_ln_fwd", (nb,), [uc, cf_ln_g, cf_ln_b], [rowC, vecC, vecC],
             [((S, CFW), BF16, rowC)])[0]
    br_b = _matmul(ub, w_co_f, "nn", F32, "mm_cf_o")

    ga_spec, gb_spec = _row(bs, D, a_ga // D), _row(bs, D, a_gb // D)
    merged = _ew(lambda a, b, ga, gb: (_f_merge(a, b, ga, gb),), "merge_fwd", (nb,), [br_a, br_b, proj, proj],
                 [rowD, rowD, ga_spec, gb_spec], [((S, D), BF16, rowD)])[0]
    mix = _matmul(merged, w_out_f, "nn", F32, "mm_out")

    x1, hn2 = _ew(_f_res_normmod, "norm2_fwd", (nb,), [x2, mix, gt1, norm2_g, sc2, sh2],
                  [rowD, rowD, vecD, vecD, vecD, vecD], [((S, D), F32, rowD), ((S, D), BF16, rowD)])
    _, landed = _split_wait("ag_late_wait", ag_state, hn2, _gather_wait_copies)
    w_up_f, w_dn_g = _pair_forward(landed, "ag_late_pair")
    w_dn_f = w_dn_g.reshape(-1, w_dn_g.shape[2])
    up_all = _matmul(hn2, w_up_f, "nn", F32, "mm_up")

    cb_ff = _div_tile(FFN, 256, LANES)
    ff_args = dict(kw=KFF, ncol=FFN // cb_ff, ins=[up_all], in_specs=[colS(cb_ff)], extras=[up_all],
                   extra_specs=[colS(cb_ff, FFN // cb_ff)], w=ff_cw, w_spec=wS(KFF, cb_ff),
                   pre=ident, post=lambda j, cv, up: _silu(cv) * up)
    hff = _conv_fwd_call("ffn_conv_fwd", out_shape=jax.ShapeDtypeStruct((S, FFN), BF16), out_spec=colS(cb_ff), **ff_args)
    ffo = _matmul(hff, w_dn_f, "nn", F32, "mm_down")

    gf2 = final_norm_g.reshape(1, D)

    def loss_bwd(a, f, gt, gf, t):
        val, vjp = jax.vjp(_f_loss, a, f, gt, gf, t)
        da, df, dgt, dgf, _ = vjp(jnp.ones((), F32))
        return da, df, dgt, dgf, jnp.zeros((1, LANES), F32) + val

    dx1_l, dffo, dgt2, dgf, loss_v = _ew(
        loss_bwd, "loss_bwd", (nb,), [x1, ffo, gt2, gf2, tgt], [rowD, rowD, vecD, vecD, rowD],
        [((S, D), F32, rowD), ((S, D), BF16, rowD), ((1, D), F32, vecD), ((1, D), F32, vecD),
         ((1, LANES), F32, vecL)], acc=(2, 3, 4))

    dhff = _matmul(dffo, w_dn_f, "nt", F32, "mm_down_dx")
    g_w_dn = _matmul(hff, dffo, "tn", F32, "mm_down_dw")

    slots = jnp.stack([chip, ci]).astype(jnp.int32)
    rs_groups = []

    def rs_pair_begin(parts, tag):
        lands = [lax.empty((4, p.shape[1] // 2, p.shape[2]), F32) for p in parts]
        return _split_start("rs_pair_start_" + tag, parts, lands, core, len(parts), _pair_copies)

    def rs_chips_begin(pair_state, nms, tag, after):
        parts, rbs = _split_wait("rs_pair_wait_" + tag, pair_state, after, _pair_copies)
        q16 = [_pair_add(p, rb, core, "rs_pair_add_" + nm) for p, rb, nm in zip(parts, rbs, nms)]
        state, token = _chip_scatter_start(q16, core, "rs_chips_start_" + tag)
        rs_groups.append((state, nms, tag))
        return token

    pair_a, ptok_a = rs_pair_begin([g_w_dn.reshape(4, FFN // 4, D)], "a")

    d_upall, g_ffcw = _conv_bwd_call(
        "ffn_conv_bwd", dout=dhff, dout_spec=colS(cb_ff),
        dio_shapes=[jax.ShapeDtypeStruct((2, S, FFN), BF16)],
        dio_specs=[pl.BlockSpec((2, S, cb_ff), lambda j: (0, 0, j))], dio_pack=lambda dg, du: [(dg, du)],
        dw_shape=jax.ShapeDtypeStruct((KFF, FFN), F32), dw_spec=wS(KFF, cb_ff), after=[ptok_a], **ff_args)
    tok_a = rs_chips_begin(pair_a, ["ffn_w_down"], "a", d_upall)
    dhn2 = _matmul(d_upall, w_up_f, "nt", F32, "mm_up_dx", after=[tok_a])
    g_w_up = _matmul(hn2, d_upall, "tn", F32, "mm_up_dw", out_groups=4)
    pair_b, ptok_b = rs_pair_begin([g_w_up], "b")

    def res2_bwd(a, mx, gt, g, sc, sh, dx1, dhn):
        _, vjp = jax.vjp(_f_res_normmod, a, mx, gt, g, sc, sh)
        return vjp((dx1, dhn))

    dx_r, dmix, dgt1, dg2, dsc2, dsh2 = _ew(
        res2_bwd, "norm2_bwd", (nb,), [x2, mix, gt1, norm2_g, sc2, sh2, dx1_l, dhn2],
        [rowD, rowD, vecD, vecD, vecD, vecD, rowD, rowD],
        [((S, D), F32, rowD), ((S, D), BF16, rowD)] + [((1, D), F32, vecD)] * 4, acc=(2, 3, 4, 5), after=[ptok_b])

    dmerged = _matmul(dmix, w_out_f, "nt", F32, "mm_out_dx")
    g_w_out = _matmul(merged, dmix, "tn", F32, "mm_out_dw")

    def merge_bwd(a, b, ga, gb, dm):
        _, vjp = jax.vjp(_f_merge, a, b, ga, gb)
        da, db, dga, dgb = vjp(dm)
        return da, db, jnp.concatenate([dga, dgb], axis=1)

    d_bra, d_brb, dproj = _ew(merge_bwd, "merge_bwd", (nb,), [br_a, br_b, proj, proj, dmerged],
                              [rowD, rowD, ga_spec, gb_spec, rowD],
                              [((S, D), BF16, rowD), ((S, D), BF16, rowD),
                               ((S, NA + LANES), BF16, _row(bs, 2 * D, a_ga // (2 * D)))])

    tok_b = rs_chips_begin(pair_b, ["ffn_w_up"], "b", d_brb)
    d_on = _matmul(d_bra, w_do_f, "nt", F32, "mm_dn_o_dx", after=[tok_b])
    g_w_do = _matmul(on, d_bra, "tn", F32, "mm_dn_o_dw", out_groups=4)
    d_ub = _matmul(d_brb, w_co_f, "nt", F32, "mm_cf_o_dx")
    g_w_co = _matmul(ub, d_brb, "tn", F32, "mm_cf_o_dw", out_groups=4)
    pair_c, ptok_c = rs_pair_begin([g_w_do, g_w_co, g_w_out.reshape(4, D // 4, D)], "c")

    def cf_ln_bwd(u, g, b, du):
        _, vjp = jax.vjp(_f_cf_ln, u, g, b)
        return vjp(du)

    d_uc, g_cflg, g_cflb = _ew(cf_ln_bwd, "cf_ln_bwd", (nb,), [uc, cf_ln_g, cf_ln_b, d_ub], [rowC, vecC, vecC, rowC],
                               [((S, CFW), F32, rowC), ((1, CFW), F32, vecC), ((1, CFW), F32, vecC)], acc=(1, 2),
                               after=[ptok_c])
    dproj_sds = jax.ShapeDtypeStruct((S, NA + LANES), BF16)
    dproj, g_cfcw = _conv_bwd_call(
        "cf_conv_bwd", dout=d_uc, dout_spec=colS(cb_cf), dio_shapes=[dproj_sds],
        dio_specs=[colS(2 * cb_cf, a_glu // (2 * cb_cf))], dio_pack=lambda dv, dg: [jnp.concatenate([dv, dg], axis=1)],
        dw_shape=jax.ShapeDtypeStruct((KCF, CFW), F32), dw_spec=wS(KCF, cb_cf), alias=dproj, **cf_args)

    def dn_post_bwd(o, z, g, d):
        _, vjp = jax.vjp(_f_dn_post, o, z, g)
        return vjp(d)

    tok_c = rs_chips_begin(pair_c, ["dn_w_o", "cf_w_o", "w_out"], "c", g_cfcw)
    d_o, dproj, g_dnng = _ew(dn_post_bwd, "dn_post_bwd", (nbh, NH), [o_dn, proj, dn_norm_g, d_on],
                             [o_spec, z_spec, ng_spec, oh_spec],
                             [((NH, S, DH), F32, o_spec), ((S, NA + LANES), BF16, z_spec), ((1, DH), F32, ng_spec)],
                             acc=(2,), alias=(dproj, 1), after=[tok_c])

    dqkvn, dgb_b, dbb_b = _dn_bwd_call(qkvn, gb_b, bb_b, states, d_o)
    dproj, g_dncw = _conv_bwd_call(
        "dn_conv_bwd", dout=dqkvn, dout_spec=qkv_spec, dio_shapes=[dproj_sds], dio_specs=[colS(DH)],
        dio_pack=lambda d: [d], dw_shape=jax.ShapeDtypeStruct((KDN, 3 * DNW), F32), dw_spec=wS(KDN, DH),
        alias=dproj, **dn_args)

    def gate_bwd(a, p, q, db, dg):
        lane = lax.broadcasted_iota(jnp.int32, a.shape, 1)
        d = jnp.zeros(a.shape, F32)
        for h in range(NH):
            d = d + jnp.where(lane == h, db[h], 0.0) + jnp.where(lane == NH + h, dg[h], 0.0)
        _, vjp = jax.vjp(gate_fn, a, p, q)
        return vjp(d)

    dproj, g_alp, g_dtb = _ew(gate_bwd, "dn_gate_bwd", (nb,), [proj, alp, dtb, dbb_b, dgb_b],
                              [ba_spec, vecL, vecL, hrow, hrow],
                              [((S, NA + LANES), BF16, ba_spec), ((1, LANES), F32, vecL), ((1, LANES), F32, vecL)],
                              acc=(1, 2), alias=(dproj, 0))

    g_w_aug = _matmul(hn1, dproj, "tn", F32, "mm_in_dw")
    def aug_slices(lo, n):
        out = []
        for o, a, m in sorted(segs):
            s, e = max(lo, o), min(lo + n, o + m)
            if s < e:
                out.append(g_w_aug[:, a + s - o:a + e - o])
        return out

    g_w_in = jnp.stack([jnp.concatenate(aug_slices(j * CS, CS), axis=1) for j in range(4)])
    pair_d, ptok_d = rs_pair_begin([g_w_in], "d")
    dhn1 = _matmul(dproj, w_aug, "nt", F32, "mm_in_dx", after=[ptok_d])

    def norm1_bwd(a, g, sc, sh, dhn, dxr):
        _, vjp = jax.vjp(_f_normmod, a, g, sc, sh)
        da, dg, dsc, dsh = vjp(dhn)
        return da + dxr, dg, dsc, dsh

    grad_x, dg1, dsc1, dsh1 = _ew(norm1_bwd, "norm1_bwd", (nb,), [x2, norm1_g, sc1, sh1, dhn1, dx_r],
                                  [rowD, vecD, vecD, vecD, rowD, rowD],
                                  [((S, D), F32, rowD)] + [((1, D), F32, vecD)] * 3, acc=(1, 2, 3))

    dmod = jnp.concatenate([dsh1, dsc1, dgt1, dsh2, dsc2, dgt2], axis=1)
    sm2 = [dmod, dg1, dg2, dgf, g_alp, g_dtb, g_dnng, g_cflg, g_cflb, g_dncw, g_cfcw, g_ffcw]
    sm2_shapes = [tuple(a.shape) for a in sm2]
    g3 = _all_gather_small(_pack_small(sm2))
    tok_d = rs_chips_begin(pair_d, ["w_in"], "d", g3)

    names = ["w_ada", "b_ada", "norm1_g", "w_in", "dn_conv_w", "dn_a_log", "dn_dt_bias", "dn_norm_g", "dn_w_o",
             "cf_conv_w", "cf_ln_g", "cf_ln_b", "cf_w_o", "w_out", "norm2_g", "ffn_w_up", "ffn_conv_w", "ffn_w_down",
             "final_norm_g"]
    wts = dict(zip(names, [w_ada, b_ada, norm1_g, w_in, dn_conv_w, dn_a_log, dn_dt_bias, dn_norm_g, dn_w_o, cf_conv_w,
                           cf_ln_g, cf_ln_b, cf_w_o, w_out, norm2_g, ffn_w_up, ffn_conv_w, ffn_w_down, final_norm_g]))
    ms = dict(zip(names, [m_w_ada, m_b_ada, m_norm1_g, m_w_in, m_dn_conv_w, m_dn_a_log, m_dn_dt_bias, m_dn_norm_g,
                          m_dn_w_o, m_cf_conv_w, m_cf_ln_g, m_cf_ln_b, m_cf_w_o, m_w_out, m_norm2_g, m_ffn_w_up,
                          m_ffn_conv_w, m_ffn_w_down, m_final_norm_g]))
    vs = dict(zip(names, [v_w_ada, v_b_ada, v_norm1_g, v_w_in, v_dn_conv_w, v_dn_a_log, v_dn_dt_bias, v_dn_norm_g,
                          v_dn_w_o, v_cf_conv_w, v_cf_ln_g, v_cf_ln_b, v_cf_w_o, v_w_out, v_norm2_g, v_ffn_w_up,
                          v_ffn_conv_w, v_ffn_w_down, v_final_norm_g]))
    grads, delta, new_m, new_v = {}, {}, {}, {}

    def adam_large(n, g):
        grads[n] = g.reshape(wts[n].shape)
        d_, m_, v_ = _adamw(wts[n][0], grads[n][0], ms[n][0], vs[n][0], "adamw_" + n)
        delta[n], new_m[n], new_v[n] = d_[None], m_[None], v_[None]

    def rs_finish(groups, after, tag):
        hs, nms_all = [], []
        for state, nms, t in groups:
            q16, r3s = _chip_scatter_wait(state, after, "rs_chips_wait_" + t)
            for q, r3, nm in zip(q16, r3s, nms):
                hs.append(_chip_sum(q, r3, slots, "rs_chip_sum_" + nm))
                nms_all.append(nm)
        for nm, g in zip(nms_all, _pair_gather(hs, "rs_pair_gather_" + tag)):
            adam_large(nm, g)

    rs_finish(rs_groups[:3], tok_d, "abc")

    ssum = _sum_slots(g3, "small_sum")
    dmod_all = _unpack_small(g3, sm2_shapes[:1])[0].reshape(8, 6 * D)
    (g_b_ada, gs_n1, gs_n2, gs_fn, gs_alp, gs_dtb, gs_dnng, gs_cflg, gs_cflb, gs_dncw, gs_cfcw,
     gs_ffcw) = _unpack_small(ssum, sm2_shapes)
    gs_alog, gs_dtbias = gs_alp[:, NH:2 * NH], gs_dtb[:, NH:2 * NH]
    gs_dncw = lax.dynamic_slice(gs_dncw, (0, chip * (3 * DNW // 4)), (KDN, 3 * DNW // 4))
    gs_cfcw = lax.dynamic_slice(gs_cfcw, (0, chip * (CFW // 4)), (KCF, CFW // 4))
    gs_ffcw = lax.dynamic_slice(gs_ffcw, (0, chip * (FFN // 4)), (KFF, FFN // 4))

    dmod_sh = lax.dynamic_slice(dmod_all, (0, chip * CA), (8, CA))

    def wada_fn(cc, dm):
        return (_mm(_silu(cc), dm, TN),)

    g_w_ada = _ew(wada_fn, "ada_dw", (CA // tn_a,), [c_all, dmod_sh],
                  [_full((8, D)), pl.BlockSpec((8, tn_a), lambda j: (0, j))],
                  [((D, CA), F32, pl.BlockSpec((D, tn_a), lambda j: (0, j)))])[0]

    loss = lax.psum(loss_v[0, 0], ("x", "y", "c"))

    adam_large("w_ada", g_w_ada)
    small_grads = {"b_ada": g_b_ada, "norm1_g": gs_n1, "dn_conv_w": gs_dncw, "dn_a_log": gs_alog,
                   "dn_dt_bias": gs_dtbias, "dn_norm_g": gs_dnng, "cf_conv_w": gs_cfcw, "cf_ln_g": gs_cflg,
                   "cf_ln_b": gs_cflb, "norm2_g": gs_n2, "ffn_conv_w": gs_ffcw, "final_norm_g": gs_fn}
    small = [n for n in names if n in small_grads]
    for n in small:
        grads[n] = small_grads[n].reshape(wts[n].shape)
    sm_sh = [tuple(wts[n].shape) for n in small]
    d_, m_, v_ = _adamw(_pack_small([wts[n] for n in small]), _pack_small([grads[n] for n in small]),
                        _pack_small([ms[n] for n in small]), _pack_small([vs[n] for n in small]), "adamw_small")
    for n, a, b_, c_ in zip(small, _unpack_small(d_, sm_sh), _unpack_small(m_, sm_sh), _unpack_small(v_, sm_sh)):
        delta[n], new_m[n], new_v[n] = a, b_, c_
    behind = sum(delta[n][0, :8, :8] for n in ["w_ada", "ffn_w_up", "ffn_w_down", "w_out", "dn_w_o", "cf_w_o"])
    rs_finish(rs_groups[3:], behind, "d")

    return (loss, grad_x[None], *[grads[n] for n in names], *[delta[n] for n in names],
            *[new_m[n] for n in names], *[new_v[n] for n in names])
```

```python
import functools

import jax
import jax.numpy as jnp
from jax import lax
from jax.experimental import pallas as pl
from jax.experimental.pallas import tpu as pltpu

F32 = jnp.float32
BF16 = jnp.bfloat16
EPS = 1e-6
LANES = 128
VMEM_LIMIT = 48 * 1024 * 1024
DN_CHUNK = 128
ADAM_LR, ADAM_B1, ADAM_B2, ADAM_EPS, ADAM_WD, ADAM_STEP = 0.001, 0.9, 0.999, 1e-08, 0.01, 10
MESH = pl.DeviceIdType.MESH

NN = (((1,), (0,)), ((), ()))
NT = (((1,), (1,)), ((), ()))
TN = (((0,), (0,)), ((), ()))
_DIMS = {"nn": NN, "nt": NT, "tn": TN}


def _mm(a, b, dims):
    return lax.dot_general(a.astype(BF16), b.astype(BF16), dims, preferred_element_type=F32)


def _mmx(a, b, dims):
    return lax.dot_general(a, b, dims, precision=lax.Precision.HIGH, preferred_element_type=F32)


def _div_tile(n, target, mult):
    best = None
    t = mult
    while t <= min(n, target):
        if n % t == 0:
            best = t
        t += mult
    return best if best is not None else n


def _params(sem=None):
    kw = dict(vmem_limit_bytes=VMEM_LIMIT)
    if sem is not None:
        kw["dimension_semantics"] = sem
    return pltpu.CompilerParams(**kw)


def _sigmoid(x):
    return jax.nn.sigmoid(x)


def _silu(x):
    return x * jax.nn.sigmoid(x)


def _softplus(x):
    return jnp.maximum(x, 0.0) + jnp.log(1.0 + jnp.exp(-jnp.abs(x)))


def _view(arr):
    if arr.ndim == 3:
        return arr.shape[1], arr.shape[0] * arr.shape[2], arr.shape[0]
    return arr.shape[0], arr.shape[1], 1


def _tile_spec(groups, cols, tr, tc, rsel, csel):
    if groups > 1:
        per = cols // groups // tc
        return pl.BlockSpec((None, tr, tc), lambda i, j, k: (csel(i, j, k) // per, rsel(i, j, k), csel(i, j, k) % per))
    return pl.BlockSpec((tr, tc), lambda i, j, k: (rsel(i, j, k), csel(i, j, k)))


def _matmul(a, b, mode, out_dtype, name, out_groups=1, after=()):
    ar, ac, ag = _view(a)
    br, bc, bg = _view(b)
    if mode == "nn":
        M, K, N = ar, ac, bc
        kdiv, mdiv, ndiv = ac // ag, M, min(bc // bg, N // out_groups)
    elif mode == "nt":
        M, K, N = ar, ac, br
        kdiv, mdiv, ndiv = min(ac // ag, bc // bg), M, N // out_groups
    else:
        K, M, N = ar, ac, bc
        kdiv, mdiv, ndiv = K, ac // ag, min(bc // bg, N // out_groups)
    tm = _div_tile(mdiv, 1024, LANES)
    tn = _div_tile(ndiv, 1536, LANES)
    tk = _div_tile(kdiv, 2048, LANES)
    nk = K // tk
    dims = _DIMS[mode]
    si, sj, sk = (lambda i, j, k: i), (lambda i, j, k: j), (lambda i, j, k: k)
    a_spec = {"nn": _tile_spec(ag, ac, tm, tk, si, sk), "nt": _tile_spec(ag, ac, tm, tk, si, sk),
              "tn": _tile_spec(ag, ac, tk, tm, sk, si)}[mode]
    b_spec = {"nn": _tile_spec(bg, bc, tk, tn, sk, sj), "nt": _tile_spec(bg, bc, tn, tk, sj, sk),
              "tn": _tile_spec(bg, bc, tk, tn, sk, sj)}[mode]
    out_shape = (M, N) if out_groups == 1 else (out_groups, M, N // out_groups)

    n_after = len(after)

    def body(*refs):
        a_ref, b_ref, o_ref = refs[0], refs[1], refs[2 + n_after]
        if nk == 1:
            o_ref[...] = lax.dot_general(a_ref[...], b_ref[...], dims, preferred_element_type=F32).astype(o_ref.dtype)
            return
        acc_ref = refs[3 + n_after]
        k = pl.program_id(2)

        @pl.when(k == 0)
        def _():
            acc_ref[...] = jnp.zeros_like(acc_ref)

        acc_ref[...] += lax.dot_general(a_ref[...], b_ref[...], dims, preferred_element_type=F32)

        @pl.when(k == nk - 1)
        def _():
            o_ref[...] = acc_ref[...].astype(o_ref.dtype)

    return pl.pallas_call(
        body, name=name, grid=(M // tm, N // tn, nk),
        in_specs=[a_spec, b_spec] + [pl.BlockSpec(memory_space=pl.ANY)] * n_after,
        out_specs=_tile_spec(out_groups, N, tm, tn, si, sj),
        out_shape=jax.ShapeDtypeStruct(out_shape, out_dtype),
        scratch_shapes=[pltpu.VMEM((tm, tn), F32)] if nk > 1 else [],
        compiler_params=_params(("parallel", "parallel", "arbitrary")),
    )(a, b, *after)


def _ew(fn, name, grid, ins, in_specs, outs, acc=(), alias=None, after=()):
    n_in = len(ins)
    n_ax = len(grid)
    extra, aliases = list(after), {}
    if alias is not None:
        extra, aliases = extra + [alias[0]], {n_in + len(after): alias[1]}
    extra_specs = [pl.BlockSpec(memory_space=pl.ANY)] * len(extra)

    def body(*refs):
        in_refs, out_refs = refs[:n_in], refs[n_in + len(extra):]
        ids = [pl.program_id(a) for a in range(n_ax)]
        res = fn(*[r[...] for r in in_refs])
        first = ids[0] == 0
        for t in ids[1:]:
            first = jnp.logical_and(first, t == 0)
        for idx, (r, val) in enumerate(zip(out_refs, res)):
            if idx in acc:
                @pl.when(first)
                def _(r=r, val=val):
                    r[...] = val.astype(r.dtype)

                @pl.when(jnp.logical_not(first))
                def _(r=r, val=val):
                    r[...] += val.astype(r.dtype)
            elif isinstance(val, tuple):
                for t, part in enumerate(val):
                    r[t] = part.astype(r.dtype)
            else:
                r[...] = val.astype(r.dtype)

    return pl.pallas_call(
        body, name=name, grid=grid, in_specs=list(in_specs) + extra_specs,
        out_specs=[o[2] for o in outs],
        out_shape=[jax.ShapeDtypeStruct(o[0], o[1]) for o in outs],
        input_output_aliases=aliases,
        compiler_params=_params(("arbitrary",) * n_ax),
    )(*ins, *extra)


def _ew_slot(fn, name, grid, slots, ins, in_specs, out_shape, out_dtype, out_spec, after=()):
    n_in = len(ins)

    def body(s_ref, *refs):
        refs[-1][...] = fn(*[r[...] for r in refs[:n_in]]).astype(refs[-1].dtype)

    return pl.pallas_call(
        body, name=name,
        grid_spec=pltpu.PrefetchScalarGridSpec(
            num_scalar_prefetch=1, grid=grid,
            in_specs=list(in_specs) + [pl.BlockSpec(memory_space=pl.ANY)] * len(after), out_specs=out_spec),
        out_shape=jax.ShapeDtypeStruct(out_shape, out_dtype),
        compiler_params=_params(("arbitrary",) * len(grid)),
    )(slots, *ins, *after)


def _row(bs, w, col=0):
    return pl.BlockSpec((bs, w), lambda i, col=col: (i, col))


def _full(shape):
    nd = len(shape)
    return pl.BlockSpec(tuple(shape), lambda *_: (0,) * nd)


def _rms(x, g):
    return x * lax.rsqrt(jnp.mean(x * x, axis=-1, keepdims=True) + EPS) * g


def _f_normmod(x, g, sc, sh):
    return _rms(x, g) * (1.0 + sc) + sh


def _f_res_normmod(x, mix, gt, g, sc, sh):
    x1 = x + gt * mix
    return x1, _f_normmod(x1, g, sc, sh)


def _f_loss(x1, f, gt, gf, tgt):
    y = _rms(x1 + gt * f, gf)
    return 0.5 * jnp.sum(jnp.mean(jnp.square(y - tgt), axis=-1))


def _f_dn_gate(nh, ba, alp, dtb):
    lane = lax.broadcasted_iota(jnp.int32, ba.shape, 1)
    m = (lane < nh).astype(F32)
    beta = _sigmoid(ba)
    g = -jnp.exp(alp) * _softplus(ba + dtb)
    return m * beta + (1.0 - m) * g


def _f_dn_post(o, z, g):
    return o * lax.rsqrt(jnp.mean(o * o, axis=-1, keepdims=True) + EPS) * g * _silu(z)


def _f_cf_ln(u, g, b):
    mu = jnp.mean(u, axis=-1, keepdims=True)
    xc = u - mu
    y = xc * lax.rsqrt(jnp.mean(xc * xc, axis=-1, keepdims=True) + EPS)
    return _silu(y * g + b)


def _f_merge(a, b, ga, gb):
    return _sigmoid(ga) * a + _sigmoid(gb) * b


def _shift_down(u, d, rows):
    if d == 0:
        return u
    return jnp.where(rows >= d, pltpu.roll(u, d, 0), 0.0)


def _shift_up(u, d, rows):
    if d == 0:
        return u
    s = u.shape[0]
    return jnp.where(rows < s - d, pltpu.roll(u, s - d, 0), 0.0)


def _conv(u, w_ref, kw, rows):
    acc = None
    for k in range(kw):
        t = w_ref[k:k + 1, :] * _shift_down(u, kw - 1 - k, rows)
        acc = t if acc is None else acc + t
    return acc


def _conv_t(dc, w_ref, kw, rows):
    acc = None
    for k in range(kw):
        t = w_ref[k:k + 1, :] * _shift_up(dc, kw - 1 - k, rows)
        acc = t if acc is None else acc + t
    return acc


def _conv_fwd_call(name, kw, ncol, ins, in_specs, extras, extra_specs, w, w_spec, pre, post, out_shape, out_spec):
    n_in, n_ex = len(ins), len(extras)

    def body(*refs):
        in_refs, ex_refs = refs[:n_in], refs[n_in:n_in + n_ex]
        w_ref, out_ref = refs[n_in + n_ex], refs[n_in + n_ex + 1]
        j = pl.program_id(0)
        u = pre(*[r[...] for r in in_refs])
        rows = lax.broadcasted_iota(jnp.int32, u.shape, 0)
        cv = _conv(u, w_ref, kw, rows)
        out_ref[...] = post(j, cv, *[r[...] for r in ex_refs]).astype(out_ref.dtype)

    return pl.pallas_call(
        body, name=name, grid=(ncol,), in_specs=list(in_specs) + list(extra_specs) + [w_spec],
        out_specs=out_spec, out_shape=out_shape, compiler_params=_params(("arbitrary",)),
    )(*ins, *extras, w)


def _conv_bwd_call(name, kw, ncol, ins, in_specs, extras, extra_specs, w, w_spec, pre, post, dout, dout_spec,
                   dio_shapes, dio_specs, dio_pack, dw_shape, dw_spec, alias=None, after=()):
    n_in, n_ex, n_io = len(ins), len(extras), len(dio_shapes)
    al, aliases = list(after), {}
    if alias is not None:
        al, aliases = al + [alias], {n_in + n_ex + 2 + len(after): 0}
    al_specs = [pl.BlockSpec(memory_space=pl.ANY)] * len(al)

    def body(*refs):
        in_refs, ex_refs = refs[:n_in], refs[n_in:n_in + n_ex]
        w_ref, dout_ref = refs[n_in + n_ex], refs[n_in + n_ex + 1]
        outs = refs[n_in + n_ex + 2 + len(al):]
        dio_refs, dw_ref = outs[:n_io], outs[n_io]
        j = pl.program_id(0)
        u, pre_vjp = jax.vjp(pre, *[r[...] for r in in_refs])
        rows = lax.broadcasted_iota(jnp.int32, u.shape, 0)
        cv = _conv(u, w_ref, kw, rows)
        _, post_vjp = jax.vjp(lambda cc, *ex: post(j, cc, *ex), cv, *[r[...] for r in ex_refs])
        g = post_vjp(dout_ref[...].astype(F32))
        dc = g[0]
        for k in range(kw):
            dw_ref[k:k + 1, :] = jnp.sum(dc * _shift_down(u, kw - 1 - k, rows), axis=0, keepdims=True)
        du = _conv_t(dc, w_ref, kw, rows)
        for r, val in zip(dio_refs, dio_pack(*pre_vjp(du), *g[1:])):
            if isinstance(val, tuple):
                for t, part in enumerate(val):
                    r[t] = part.astype(r.dtype)
            else:
                r[...] = val.astype(r.dtype)

    return pl.pallas_call(
        body, name=name, grid=(ncol,),
        in_specs=list(in_specs) + list(extra_specs) + [w_spec, dout_spec] + al_specs,
        out_specs=list(dio_specs) + [dw_spec],
        out_shape=list(dio_shapes) + [dw_shape],
        input_output_aliases=aliases,
        compiler_params=_params(("arbitrary",)),
    )(*ins, *extras, w, dout, *al)


def _tri_inverse(a):
    c = a.shape[0]
    ii = lax.broadcasted_iota(jnp.int32, (c, c), 0)
    jj = lax.broadcasted_iota(jnp.int32, (c, c), 1)
    eye = (ii == jj).astype(F32)

    def same_block(bits):
        return jnp.right_shift(ii, bits) == jnp.right_shift(jj, bits)

    d = jnp.where(same_block(3), a, 0.0)
    d2 = _mmx(d, d, NN)
    yield
    t = (eye - d) + _mmx(eye - d, d2, NN)
    d4 = _mmx(d2, d2, NN)
    yield
    t = t + _mmx(t, d4, NN)
    yield
    bits = 3
    while (1 << bits) < c:
        low = jnp.where(jnp.logical_and(same_block(bits + 1), jnp.logical_not(same_block(bits))), a, 0.0)
        tl = _mmx(t, low, NN)
        yield
        t = t - _mmx(tl, t, NN)
        yield
        bits += 1
    return t


def _lockstep(gens):
    out = [None] * len(gens)
    live = list(range(len(gens)))
    while live:
        nxt = []
        for i in live:
            try:
                next(gens[i])
                nxt.append(i)
            except StopIteration as e:
                out[i] = e.value
        live = nxt
    return out


def _dn_common(q, k, v, gb, bb):
    c = q.shape[0]
    ii = lax.broadcasted_iota(jnp.int32, (c, c), 0)
    jj = lax.broadcasted_iota(jnp.int32, (c, c), 1)
    causal = jj <= ii
    strict = jj < ii
    low = causal.astype(F32)
    ones = jnp.ones((c, LANES), F32)
    gc = _mmx(low, gb, NN)
    kb = k * bb
    vb = v * bb
    kk = _mm(kb, k, NT)
    qk = _mm(q, k, NT)
    yield
    diff = (_mmx(gc, ones, NT) - _mmx(ones, gc, NT)) * (1.0 / LANES)
    gl = jnp.sum(gb, axis=0, keepdims=True)
    eg = jnp.exp(gc)
    egm = jnp.exp(gl - gc)
    egl = jnp.exp(gl)
    kbg = kb * eg
    yield
    decay = jnp.where(causal, jnp.exp(jnp.where(causal, diff, 0.0)), 0.0)
    t = yield from _tri_inverse(jnp.where(strict, kk * decay, 0.0))
    attn = qk * decay
    return dict(causal=causal, strict=strict, low=low, ones=ones, decay=decay, eg=eg, egm=egm, egl=egl,
                kb=kb, vb=vb, kbg=kbg, kk=kk, t=t, qk=qk, attn=attn, qd=q * eg, kd=k * egm)


def _dn_fwd_stages(q, k, v, gb, bb, s):
    m = yield from _dn_common(q, k, v, gb, bb)
    r = m["vb"] - _mm(m["kbg"], s, NN)
    qs = _mm(m["qd"], s, NN)
    yield
    u = _mmx(m["t"], r, NN)
    yield
    o = qs + _mm(m["attn"], u, NN)
    s2 = s * m["egl"] + _mm(m["kd"], u, TN)
    return o, s2


def _dn_bwd_stages(q, k, v, gb, bb, s, do, dsp):
    m = yield from _dn_common(q, k, v, gb, bb)
    c = q.shape[0]
    t, decay, eg, egm, egl = m["t"], m["decay"], m["eg"], m["egm"], m["egl"]
    r = m["vb"] - _mm(m["kbg"], s, NN)
    du = _mm(m["attn"], do, TN) + _mm(m["kd"], dsp, NN)
    dqd = _mm(do, s, NT)
    ds = dsp * egl + _mm(m["qd"], do, TN)
    degl = jnp.sum(jnp.sum(dsp * s, axis=1, keepdims=True), axis=0, keepdims=True)
    yield
    u = _mmx(t, r, NN)
    dr = _mmx(t, du, TN)
    yield
    dattn = jnp.where(m["causal"], _mm(do, u, NT), 0.0)
    dkd = _mm(u, dsp, NT)
    da = jnp.where(m["strict"], -_mm(dr, u, NT), 0.0)
    dkbg = -_mm(dr, s, NT)
    ds = ds - _mm(m["kbg"], dr, TN)
    yield
    dkk = da * decay
    dqk = dattn * decay
    ddiff = (da * m["kk"] + dattn * m["qk"]) * decay
    dgc = _mmx(ddiff, m["ones"], NN) - _mmx(ddiff, m["ones"], TN)
    dkb = _mm(dkk, k, NN) + dkbg * eg
    dk = _mm(dkk, m["kb"], TN) + _mm(dqk, q, TN) + dkd * egm + dkb * bb
    dq = _mm(dqk, k, NN) + dqd * eg
    yield
    dgc = dgc + jnp.sum(dqd * q + dkbg * m["kb"], axis=-1, keepdims=True) * eg
    tt = jnp.sum(dkd * k, axis=-1, keepdims=True) * egm
    dgc = dgc - tt
    dgl = jnp.sum(tt, axis=0, keepdims=True) + degl * egl
    dbb = jnp.sum(dkb * k + dr * v, axis=-1, keepdims=True) + jnp.zeros((c, LANES), F32)
    dv = dr * bb
    dgb = _mmx(m["low"], dgc, TN) + dgl
    return dq, dk, dv, dgb, dbb, ds


def _dn_fwd_chunk(q, k, v, gb, bb, s):
    return _lockstep([_dn_fwd_stages(q, k, v, gb, bb, s)])[0]


def _dn_bwd_chunk(q, k, v, gb, bb, s, do, dsp):
    return _lockstep([_dn_bwd_stages(q, k, v, gb, bb, s, do, dsp)])[0]


def _dn_fwd_call(qkvn, gb, bb):
    _, nh, s, dh = qkvn.shape
    c = min(DN_CHUNK, s)
    n = s // c
    hb = nh

    def body(q_ref, k_ref, v_ref, g_ref, b_ref, o_ref, st_ref, s_ref):
        @pl.when(pl.program_id(1) == 0)
        def _():
            s_ref[...] = jnp.zeros_like(s_ref)

        st_ref[...] = s_ref[...]
        heads = [_dn_fwd_stages(q_ref[h], k_ref[h], v_ref[h], g_ref[h], b_ref[h], s_ref[h]) for h in range(hb)]
        for h, (o, s2) in enumerate(_lockstep(heads)):
            o_ref[h] = o
            s_ref[h] = s2

    def qspec(t):
        return pl.BlockSpec((None, hb, c, dh), lambda i, j, t=t: (t, i, j, 0))

    hs = pl.BlockSpec((hb, c, dh), lambda i, j: (i, j, 0))
    return pl.pallas_call(
        body, name="dn_fwd", grid=(nh // hb, n),
        in_specs=[qspec(0), qspec(1), qspec(2), hs, hs],
        out_specs=[hs, pl.BlockSpec((hb, None, dh, dh), lambda i, j: (i, j, 0, 0))],
        out_shape=[jax.ShapeDtypeStruct((nh, s, dh), F32), jax.ShapeDtypeStruct((nh, n, dh, dh), F32)],
        scratch_shapes=[pltpu.VMEM((hb, dh, dh), F32)],
        compiler_params=_params(("arbitrary", "arbitrary")),
    )(qkvn, qkvn, qkvn, gb, bb)


def _dn_bwd_call(qkvn, gb, bb, states, do):
    _, nh, s, dh = qkvn.shape
    c = min(DN_CHUNK, s)
    n = s // c
    hb = nh

    def body(q_ref, k_ref, v_ref, g_ref, b_ref, st_ref, do_ref, dqkv_ref, dg_ref, db_ref, ds_ref):
        @pl.when(pl.program_id(1) == 0)
        def _():
            ds_ref[...] = jnp.zeros_like(ds_ref)

        heads = [_dn_bwd_stages(q_ref[h], k_ref[h], v_ref[h], g_ref[h], b_ref[h], st_ref[h], do_ref[h], ds_ref[h])
                 for h in range(hb)]
        for h, (dq, dk, dv, dg, db, ds) in enumerate(_lockstep(heads)):
            dqkv_ref[0, h] = dq
            dqkv_ref[1, h] = dk
            dqkv_ref[2, h] = dv
            dg_ref[h] = dg
            db_ref[h] = db
            ds_ref[h] = ds

    def qspec(t):
        return pl.BlockSpec((None, hb, c, dh), lambda i, j, t=t: (t, i, n - 1 - j, 0))

    hs = pl.BlockSpec((hb, c, dh), lambda i, j: (i, n - 1 - j, 0))
    sh = jax.ShapeDtypeStruct((nh, s, dh), F32)
    return pl.pallas_call(
        body, name="dn_bwd", grid=(nh // hb, n),
        in_specs=[qspec(0), qspec(1), qspec(2), hs, hs,
                  pl.BlockSpec((hb, None, dh, dh), lambda i, j: (i, n - 1 - j, 0, 0)), hs],
        out_specs=[pl.BlockSpec((3, hb, c, dh), lambda i, j: (0, i, n - 1 - j, 0)), hs, hs],
        out_shape=[jax.ShapeDtypeStruct((3, nh, s, dh), F32), sh, sh],
        scratch_shapes=[pltpu.VMEM((hb, dh, dh), F32)],
        compiler_params=_params(("arbitrary", "arbitrary")),
    )(qkvn, qkvn, qkvn, gb, bb, states, do)


def _adamw(w, g, m, v, name):
    r, c = w.shape
    br = _div_tile(r, max(8, (1 << 18) // max(c, 1)), 8)

    def fn(w, g, m, v):
        m = ADAM_B1 * m + (1.0 - ADAM_B1) * g
        v = ADAM_B2 * v + (1.0 - ADAM_B2) * jnp.square(g)
        m_hat = m / (1.0 - ADAM_B1 ** ADAM_STEP)
        v_hat = v / (1.0 - ADAM_B2 ** ADAM_STEP)
        delta = -ADAM_LR * (m_hat / (jnp.sqrt(v_hat) + ADAM_EPS) + ADAM_WD * w)
        return delta, m, v

    spec = pl.BlockSpec((br, c), lambda i: (i, 0))
    return _ew(fn, name, (r // br,), [w, g, m, v], [spec] * 4, [((r, c), F32, spec)] * 3)


def _coords():
    return lax.axis_index("x"), lax.axis_index("y"), lax.axis_index("c")


def _all_gather_small(v, after=()):
    r, w = v.shape
    n_after = len(after)

    def body(v_ref, *rest):
        out_ref, send_sems, recv_sems, local_sem = rest[n_after:]
        x, y, c = _coords()
        me = 4 * x + 2 * y + c
        mine = pltpu.make_async_copy(v_ref, out_ref.at[me], local_sem)
        mine.start()
        peers = []
        for k in range(1, 8):
            px = 1 - x if k & 4 else x
            py = 1 - y if k & 2 else y
            pc = 1 - c if k & 1 else c
            peers.append((px, py, pc))
        sends = []
        for k, peer in enumerate(peers):
            cp = pltpu.make_async_remote_copy(src_ref=v_ref, dst_ref=out_ref.at[me], send_sem=send_sems.at[k],
                                              recv_sem=recv_sems.at[k], device_id=peer, device_id_type=MESH)
            cp.start()
            sends.append(cp)
        for k, (px, py, pc) in enumerate(peers):
            pltpu.make_async_remote_copy(src_ref=v_ref, dst_ref=out_ref.at[4 * px + 2 * py + pc],
                                         send_sem=send_sems.at[k], recv_sem=recv_sems.at[k],
                                         device_id=(px, py, pc), device_id_type=MESH).wait_recv()
        for cp in sends:
            cp.wait_send()
        mine.wait()

    return pl.pallas_call(
        body, name="ag_small", out_shape=jax.ShapeDtypeStruct((8, r, w), v.dtype),
        in_specs=[pl.BlockSpec(memory_space=pltpu.VMEM)] + [pl.BlockSpec(memory_space=pl.ANY)] * n_after,
        out_specs=pl.BlockSpec(memory_space=pltpu.VMEM),
        scratch_shapes=[pltpu.SemaphoreType.DMA((7,)), pltpu.SemaphoreType.DMA((7,)), pltpu.SemaphoreType.DMA],
        compiler_params=pltpu.CompilerParams(vmem_limit_bytes=VMEM_LIMIT),
    )(v, *after)


def _hbm_call(body, name, arrays, out_shapes, n_sems, aliases=None):
    hbm = pl.BlockSpec(memory_space=pltpu.HBM)
    return pl.pallas_call(
        body, name=name, out_shape=list(out_shapes), in_specs=[hbm] * len(arrays), out_specs=[hbm] * len(out_shapes),
        input_output_aliases=aliases or {},
        scratch_shapes=[pltpu.SemaphoreType.DMA((n_sems,)), pltpu.SemaphoreType.DMA((n_sems,))],
    )(*arrays)


def _half_rows(ref_shape, c):
    h = ref_shape[1] // 2
    return pl.ds(pl.multiple_of(c * h, 16), h), pl.ds(pl.multiple_of((1 - c) * h, 16), h)


def _pair_copies(p_refs, land_refs, send_sems, recv_sems):
    x, y, c = _coords()
    cps = []
    for w, (p, land) in enumerate(zip(p_refs, land_refs)):
        _, other = _half_rows(p.shape, c)
        cps.append(pltpu.make_async_remote_copy(src_ref=p.at[:, other], dst_ref=land, send_sem=send_sems.at[w],
                                                recv_sem=recv_sems.at[w], device_id=(x, y, 1 - c),
                                                device_id_type=MESH))
    return cps


def _pair_gather(gs, name):
    n = len(gs)

    def body(*refs):
        ins, outs, send_sems, recv_sems = refs[:n], refs[n:2 * n], refs[2 * n], refs[2 * n + 1]
        x, y, c = _coords()
        cps = []
        for w in range(n):
            cp = pltpu.make_async_remote_copy(src_ref=ins[w].at[c], dst_ref=outs[w].at[c], send_sem=send_sems.at[w],
                                              recv_sem=recv_sems.at[w], device_id=(x, y, 1 - c), device_id_type=MESH)
            cp.start()
            cps.append(cp)
        for w, cp in enumerate(cps):
            pltpu.make_async_remote_copy(src_ref=ins[w].at[c], dst_ref=outs[w].at[1 - c], send_sem=send_sems.at[w],
                                         recv_sem=recv_sems.at[w], device_id=(x, y, 1 - c),
                                         device_id_type=MESH).wait_recv()
            cp.wait_send()

    shapes = [jax.ShapeDtypeStruct(g.shape, g.dtype) for g in gs]
    return _hbm_call(body, name, gs, shapes, n, aliases={i: i for i in range(n)})


_HBM = pl.BlockSpec(memory_space=pltpu.HBM)
_SEM = pl.BlockSpec(memory_space=pltpu.SEMAPHORE)
_EFFECT = pltpu.SideEffectType.DATAFLOW_SIDE_EFFECTING


def _split_start(name, srcs, lands, after, n_copies, make_copies):
    n, m = len(srcs), len(lands)
    arrays = [pltpu.with_memory_space_constraint(a, pltpu.HBM) for a in list(srcs) + list(lands)]

    def body(*refs):
        src_refs, land_refs = refs[:n], refs[n:n + m]
        send_sems, recv_sems = refs[n + m + 1], refs[n + m + 2]
        for cp in make_copies(src_refs, land_refs, send_sems, recv_sems):
            cp.start()
        refs[-1][...] = jnp.zeros_like(refs[-1])

    outs = pl.pallas_call(
        body, name=name,
        out_shape=(pltpu.SemaphoreType.DMA((n_copies,)), pltpu.SemaphoreType.DMA((n_copies,)),
                   *[pltpu.HBM(a.shape, a.dtype) for a in arrays], jax.ShapeDtypeStruct((8, LANES), F32)),
        in_specs=[_HBM] * (n + m) + [pl.BlockSpec(memory_space=pl.ANY)],
        out_specs=(_SEM, _SEM, *[_HBM] * (n + m), pl.BlockSpec(memory_space=pltpu.VMEM)),
        input_output_aliases={i: 2 + i for i in range(n + m)},
        compiler_params=pltpu.CompilerParams(has_side_effects=_EFFECT),
    )(*arrays, after)
    return (outs[0], outs[1], list(outs[2:2 + n]), list(outs[2 + n:2 + n + m])), outs[-1]


def _split_wait(name, state, after, make_copies):
    send_sems, recv_sems, srcs, lands = state
    n, m = len(srcs), len(lands)

    def body(*refs):
        src_refs, land_refs = refs[:n], refs[n:n + m]
        for cp in make_copies(src_refs, land_refs, refs[n + m], refs[n + m + 1]):
            cp.wait_send()
            cp.wait_recv()

    outs = pl.pallas_call(
        body, name=name, out_shape=tuple(pltpu.HBM(a.shape, a.dtype) for a in srcs + lands),
        in_specs=[_HBM] * (n + m) + [_SEM, _SEM, pl.BlockSpec(memory_space=pl.ANY)], out_specs=tuple([_HBM] * (n + m)),
        input_output_aliases={i: i for i in range(n + m)},
        compiler_params=pltpu.CompilerParams(has_side_effects=_EFFECT),
    )(*srcs, *lands, send_sems, recv_sems, after)
    return list(outs[:n]), list(outs[n:])


def _scatter_copies(q_refs, land_refs, send_sems, recv_sems):
    x, y, c = _coords()
    cps = []
    for w, (q, land) in enumerate(zip(q_refs, land_refs)):
        for k, (px, py) in enumerate([(1 - x, y), (x, 1 - y), (1 - x, 1 - y)]):
            cps.append(pltpu.make_async_remote_copy(src_ref=q.at[2 * px + py], dst_ref=land.at[k],
                                                    send_sem=send_sems.at[3 * w + k], recv_sem=recv_sems.at[3 * w + k],
                                                    device_id=(px, py, c), device_id_type=MESH))
    return cps


def _chip_scatter_start(qs, after, name):
    lands = [lax.empty((3,) + q.shape[1:], q.dtype) for q in qs]
    return _split_start(name, qs, lands, after, 3 * len(qs), _scatter_copies)


def _chip_scatter_wait(state, after, name):
    return _split_wait(name, state, after, _scatter_copies)


def _gather_copies(src_refs, buf_refs, send_sems, recv_sems):
    x, y, c = _coords()
    j = 2 * x + y
    cps = []
    for w, buf in enumerate(buf_refs):
        mine, _ = _half_rows(buf.shape, c)
        for k, (px, py) in enumerate([(1 - x, y), (x, 1 - y), (1 - x, 1 - y)]):
            cps.append(pltpu.make_async_remote_copy(src_ref=buf.at[j, mine], dst_ref=buf.at[j, mine],
                                                    send_sem=send_sems.at[3 * w + k], recv_sem=recv_sems.at[3 * w + k],
                                                    device_id=(px, py, c), device_id_type=MESH))
    return cps


def _gather_wait_copies(src_refs, buf_refs, send_sems, recv_sems):
    x, y, c = _coords()
    j = 2 * x + y
    cps = []
    for w, buf in enumerate(buf_refs):
        mine, _ = _half_rows(buf.shape, c)
        for k, (px, py) in enumerate([(1 - x, y), (x, 1 - y), (1 - x, 1 - y)]):
            cps.append(pltpu.make_async_remote_copy(src_ref=buf.at[j, mine], dst_ref=buf.at[2 * px + py, mine],
                                                    send_sem=send_sems.at[3 * w + k], recv_sem=recv_sems.at[3 * w + k],
                                                    device_id=(px, py, c), device_id_type=MESH))
    return cps


def _pair_forward(bufs, name):
    n = len(bufs)

    def body(*refs):
        ins, outs, send_sems, recv_sems = refs[:n], refs[n:2 * n], refs[2 * n], refs[2 * n + 1]
        x, y, c = _coords()
        chips = [(1 - x, y), (x, 1 - y), (1 - x, 1 - y)]
        cps = []
        for w in range(n):
            mine, _ = _half_rows(outs[w].shape, c)
            for k, (px, py) in enumerate(chips):
                cp = pltpu.make_async_remote_copy(src_ref=ins[w].at[2 * px + py, mine],
                                                  dst_ref=outs[w].at[2 * px + py, mine],
                                                  send_sem=send_sems.at[3 * w + k], recv_sem=recv_sems.at[3 * w + k],
                                                  device_id=(x, y, 1 - c), device_id_type=MESH)
                cp.start()
                cps.append(cp)
        for w in range(n):
            _, sib = _half_rows(outs[w].shape, c)
            for k, (px, py) in enumerate(chips):
                pltpu.make_async_remote_copy(src_ref=ins[w].at[2 * px + py, sib], dst_ref=outs[w].at[2 * px + py, sib],
                                             send_sem=send_sems.at[3 * w + k], recv_sem=recv_sems.at[3 * w + k],
                                             device_id=(x, y, 1 - c), device_id_type=MESH).wait_recv()
        for cp in cps:
            cp.wait_send()

    shapes = [jax.ShapeDtypeStruct(b.shape, b.dtype) for b in bufs]
    return _hbm_call(body, name, bufs, shapes, 3 * n, aliases={i: i for i in range(n)})


def _slot_rows(h, cs):
    return _div_tile(h, max(16, (1 << 19) // cs), 16)


def _cast_into_slot(w, slot, name, after=()):
    r, cs = w.shape
    br = _slot_rows(r, cs)
    return _ew_slot(lambda a: a, name, (r // br,), slot, [w], [pl.BlockSpec((br, cs), lambda i, s: (i, 0))],
                    (4, r, cs), BF16, pl.BlockSpec((None, br, cs), lambda i, s: (s[0], i, 0)), after=after)


def _pair_add(p, rb, core, name):
    n, r, cs = p.shape
    h = r // 2
    br = _slot_rows(h, cs)
    nb = h // br
    return _ew_slot(lambda a, b: a + b, name, (n, nb), core, [p, rb],
                    [pl.BlockSpec((None, br, cs), lambda s, i, c: (s, c[0] * nb + i, 0)),
                     pl.BlockSpec((None, br, cs), lambda s, i, c: (s, i, 0))],
                    (n, h, cs), BF16, pl.BlockSpec((None, br, cs), lambda s, i, c: (s, i, 0)))


def _chip_sum(q, r3, slots, name):
    _, h, cs = q.shape
    br = _slot_rows(h, cs)

    def fn(a, b):
        acc = a.astype(F32)
        for k in range(3):
            acc = acc + b[k].astype(F32)
        return acc

    return _ew_slot(fn, name, (h // br,), slots, [q, r3],
                    [pl.BlockSpec((None, br, cs), lambda i, s: (s[0], i, 0)),
                     pl.BlockSpec((3, br, cs), lambda i, s: (0, i, 0))],
                    (2, h, cs), F32, pl.BlockSpec((None, br, cs), lambda i, s: (s[1], i, 0)))


def _sum_slots(r, name):
    n, h, w = r.shape
    br = _div_tile(h, 2048, 16)

    def fn(blk):
        acc = blk[0].astype(F32)
        for s in range(1, n):
            acc = acc + blk[s].astype(F32)
        return (acc,)

    return _ew(fn, name, (h // br,), [r], [pl.BlockSpec((n, br, w), lambda i: (0, i, 0))],
               [((h, w), F32, pl.BlockSpec((br, w), lambda i: (i, 0)))])[0]


def _pack_small(arrs):
    flat = jnp.concatenate([a.reshape(-1).astype(F32) for a in arrs])
    n = flat.shape[0]
    rows = -(-n // LANES)
    rows = -(-rows // 8) * 8
    return jnp.pad(flat, (0, rows * LANES - n)).reshape(rows, LANES)


def _unpack_small(p, shapes):
    lead = p.shape[:-2]
    flat = p.reshape(lead + (-1,))
    out, off = [], 0
    for sh in shapes:
        n = 1
        for d in sh:
            n *= d
        out.append(flat[..., off:off + n].reshape(lead + tuple(sh)))
        off += n
    return out


def kernel(x, c, w_ada, b_ada, norm1_g, w_in, dn_conv_w, dn_a_log, dn_dt_bias, dn_norm_g, dn_w_o, cf_conv_w, cf_ln_g, cf_ln_b, cf_w_o, w_out, norm2_g, ffn_w_up, ffn_conv_w, ffn_w_down, final_norm_g, loss_target, m_w_ada, m_b_ada, m_norm1_g, m_w_in, m_dn_conv_w, m_dn_a_log, m_dn_dt_bias, m_dn_norm_g, m_dn_w_o, m_cf_conv_w, m_cf_ln_g, m_cf_ln_b, m_cf_w_o, m_w_out, m_norm2_g, m_ffn_w_up, m_ffn_conv_w, m_ffn_w_down, m_final_norm_g, v_w_ada, v_b_ada, v_norm1_g, v_w_in, v_dn_conv_w, v_dn_a_log, v_dn_dt_bias, v_dn_norm_g, v_dn_w_o, v_cf_conv_w, v_cf_ln_g, v_cf_ln_b, v_cf_w_o, v_w_out, v_norm2_g, v_ffn_w_up, v_ffn_conv_w, v_ffn_w_down, v_final_norm_g):
    xi, yi, ci = _coords()
    chip = 2 * xi + yi
    me = 4 * xi + 2 * yi + ci
    core = jnp.reshape(ci, (1,)).astype(jnp.int32)

    S, D = x.shape[1], x.shape[2]
    NH = dn_a_log.shape[1]
    DH = dn_norm_g.shape[1]
    DNW = NH * DH
    CFW = cf_ln_g.shape[1]
    FFN = ffn_w_down.shape[1] * 4
    KDN, KCF, KFF = dn_conv_w.shape[1], cf_conv_w.shape[1], ffn_conv_w.shape[1]
    NIN = w_in.shape[2] * 4
    assert NIN == 4 * DNW + 2 * NH + 2 * CFW + 2 * D and DH == LANES and 2 * NH <= LANES
    x2, tgt = x[0], loss_target[0]

    slot_chip = jnp.reshape(chip, (1,)).astype(jnp.int32)
    big = [w_in[0], dn_w_o[0], cf_w_o[0], w_out[0], ffn_w_up[0], ffn_w_down[0]]
    names_big = ["w_in", "dn_w_o", "cf_w_o", "w_out", "ffn_w_up", "ffn_w_down"]
    buf_in = _cast_into_slot(big[0], slot_chip, "cast_w_in")

    sm_shapes = [(D,), (KDN, 3 * DNW // 4), (KCF, CFW // 4), (KFF, FFN // 4)]
    g1 = _all_gather_small(_pack_small([c[0], dn_conv_w[0], cf_conv_w[0], ffn_conv_w[0]]))
    c_all, dcw_s, ccw_s, fcw_s = _unpack_small(g1, sm_shapes)

    def chips_cols(t):
        t = t[0::2]
        return jnp.transpose(t, (1, 0, 2)).reshape(t.shape[1], -1)

    dn_cw, cf_cw, ff_cw = chips_cols(dcw_s), chips_cols(ccw_s), chips_cols(fcw_s)


    cb_cf = _div_tile(CFW, 256, LANES)
    o_b = 4 * DNW
    o_glu = o_b + 2 * NH
    o_ga = o_glu + 2 * CFW
    NA = NIN - 2 * NH
    a_z, a_ga, a_gb, a_glu, a_ba = 3 * DNW, 4 * DNW, 4 * DNW + D, 4 * DNW + 2 * D, NA

    segs = [(0, 0, o_b), (o_b, NA, 2 * NH), (o_ga, a_ga, 2 * D)]
    for t in range(CFW // cb_cf):
        segs += [(o_glu + t * cb_cf, a_glu + 2 * t * cb_cf, cb_cf),
                 (o_glu + CFW + t * cb_cf, a_glu + (2 * t + 1) * cb_cf, cb_cf)]
    CS = NIN // 4

    def shard_slices(lo, n):
        out = []
        while n > 0:
            j, off = lo // CS, lo % CS
            m = min(n, CS - off)
            out.append(w_in_rows[j * D:(j + 1) * D, off:off + m])
            lo, n = lo + m, n - m
        return out

    CA = w_ada.shape[2]
    b_sh = lax.dynamic_slice(b_ada, (0, chip * CA), (1, CA))
    tn_a = _div_tile(CA, 512, LANES)

    def mod_fn(cc, w, b):
        return (_mm(_silu(cc), w, NN) + b,)

    mod_sh = _ew(mod_fn, "ada_mod", (CA // tn_a,), [c_all, w_ada[0], b_sh],
                 [_full((8, D)), pl.BlockSpec((D, tn_a), lambda j: (0, j)), pl.BlockSpec((1, tn_a), lambda j: (0, j))],
                 [((8, CA), F32, pl.BlockSpec((8, tn_a), lambda j: (0, j)))])[0]
    g2 = _all_gather_small(_pack_small([mod_sh]))
    in_state, in_token = _split_start("ag_in_start", [], [buf_in], g2, 3, _gather_copies)
    bufs = [_cast_into_slot(w, slot_chip, "cast_" + nm, after=[in_token]) for w, nm in zip(big[1:], names_big[1:])]
    mid_state, mid_token = _split_start("ag_mid_start", [], bufs[:3], in_token, 9, _gather_copies)
    ag_state, ag_token = _split_start("ag_late_start", [], bufs[3:], mid_token, 6, _gather_copies)
    mod_all = _unpack_small(g2, [(8, CA)])[0][0::2]
    mod_all = jnp.transpose(mod_all, (1, 0, 2)).reshape(8, 4 * CA)
    mod_me = lax.dynamic_slice(mod_all, (me, 0), (1, 6 * D))
    sh1, sc1, gt1, sh2, sc2, gt2 = [mod_me[:, i * D:(i + 1) * D] for i in range(6)]

    bs = _div_tile(S, 128, 8)
    nb = S // bs
    vecD = _full((1, D))
    rowD = _row(bs, D)

    hn1 = _ew(lambda a, g, sc, sh: (_f_normmod(a, g, sc, sh),), "norm1_fwd", (nb,),
              [x2, norm1_g, sc1, sh1], [rowD, vecD, vecD, vecD], [((S, D), BF16, rowD)],
              after=[ag_token, m_w_in[0], v_w_in[0]])[0]
    _, in_landed = _split_wait("ag_in_wait", in_state, hn1, _gather_wait_copies)
    (w_in_g,) = _pair_forward(in_landed, "ag_in_pair")
    w_in_rows = w_in_g.reshape(4 * D, CS)
    w_aug = jnp.concatenate([p for o, a, n in sorted(segs, key=lambda s: s[1]) for p in shard_slices(o, n)]
                            + [jnp.zeros((D, LANES - 2 * NH), BF16)], axis=1)
    proj = _matmul(hn1, w_aug, "nn", F32, "mm_in")

    def dn_post(j, cv):
        s = _silu(cv)
        nrm = s * lax.rsqrt(jnp.sum(s * s, axis=-1, keepdims=True) + EPS)
        fq = (j < NH).astype(F32)
        fk = (j < 2 * NH).astype(F32)
        scale = fq * (DH ** -0.5) + (1.0 - fq)
        return fk * (nrm * scale) + (1.0 - fk) * s

    def ident(a):
        return a

    def colS(w, off=0):
        return pl.BlockSpec((S, w), lambda j, off=off: (0, j + off))

    def wS(kw, w):
        return pl.BlockSpec((kw, w), lambda j: (0, j))

    qkv_spec = pl.BlockSpec((None, None, S, DH), lambda j: (j // NH, j % NH, 0, 0))
    dn_args = dict(kw=KDN, ncol=3 * NH, ins=[proj], in_specs=[colS(DH)], extras=[], extra_specs=[],
                   w=dn_cw, w_spec=wS(KDN, DH), pre=ident, post=dn_post)
    qkvn = _conv_fwd_call("dn_conv_fwd", out_shape=jax.ShapeDtypeStruct((3, NH, S, DH), F32), out_spec=qkv_spec,
                          **dn_args)

    alp = jnp.pad(dn_a_log, ((0, 0), (NH, LANES - 2 * NH)))
    dtb = jnp.pad(dn_dt_bias, ((0, 0), (NH, LANES - 2 * NH)))
    vecL = _full((1, LANES))
    ba_spec = _row(bs, LANES, a_ba // LANES)
    rowL = _row(bs, LANES)
    gate_fn = functools.partial(_f_dn_gate, NH)
    def gate_fwd(a, p, q):
        val = gate_fn(a, p, q)
        lane = lax.broadcasted_iota(jnp.int32, val.shape, 1)

        def spread(col):
            return jnp.sum(jnp.where(lane == col, val, 0.0), axis=-1, keepdims=True) + jnp.zeros(val.shape, F32)

        return tuple(spread(h) for h in range(NH)), tuple(spread(NH + h) for h in range(NH))

    hrow = pl.BlockSpec((NH, bs, DH), lambda i: (0, i, 0))
    bb_b, gb_b = _ew(gate_fwd, "dn_gate_fwd", (nb,), [proj, alp, dtb], [ba_spec, vecL, vecL],
                     [((NH, S, DH), F32, hrow), ((NH, S, DH), F32, hrow)])
    o_dn, states = _dn_fwd_call(qkvn, gb_b, bb_b)

    bsh = _div_tile(S, 512, 8)
    nbh = S // bsh
    o_spec = pl.BlockSpec((None, bsh, DH), lambda i, h: (h, i, 0))
    z_spec = pl.BlockSpec((bsh, DH), lambda i, h: (i, a_z // DH + h))
    oh_spec = pl.BlockSpec((bsh, DH), lambda i, h: (i, h))
    ng_spec = pl.BlockSpec((1, DH), lambda i, h: (0, 0))
    on = _ew(lambda o, z, g: (_f_dn_post(o, z, g),), "dn_post_fwd", (nbh, NH), [o_dn, proj, dn_norm_g],
             [o_spec, z_spec, ng_spec], [((S, DNW), BF16, oh_spec)])[0]
    _, landed = _split_wait("ag_mid_wait", mid_state, on, _gather_wait_copies)
    w_do_f, w_co_f, w_out_g = _pair_forward(landed, "ag_mid_pair")
    w_out_f = w_out_g.reshape(-1, w_out_g.shape[2])
    br_a = _matmul(on, w_do_f, "nn", F32, "mm_dn_o")

    def glu_pre(val, gl):
        return val * _sigmoid(gl)

    def glu_spec(t):
        return pl.BlockSpec((S, cb_cf), lambda j, t=t: (0, a_glu // cb_cf + 2 * j + t))

    cf_args = dict(kw=KCF, ncol=CFW // cb_cf, ins=[proj, proj], in_specs=[glu_spec(0), glu_spec(1)],
                   extras=[], extra_specs=[], w=cf_cw, w_spec=wS(KCF, cb_cf), pre=glu_pre, post=lambda j, cv: cv)
    uc = _conv_fwd_call("cf_conv_fwd", out_shape=jax.ShapeDtypeStruct((S, CFW), F32), out_spec=colS(cb_cf), **cf_args)
    rowC = _row(bs, CFW)
    vecC = _full((1, CFW))
    ub = _ew(lambda u, g, b: (_f_cf_ln(u, g, b),), "cf_ln_fwd", (nb,), [uc, cf_ln_g, cf_ln_b], [rowC, vecC, vecC],
             [((S, CFW), BF16, rowC)])[0]
    br_b = _matmul(ub, w_co_f, "nn", F32, "mm_cf_o")

    ga_spec, gb_spec = _row(bs, D, a_ga // D), _row(bs, D, a_gb // D)
    merged = _ew(lambda a, b, ga, gb: (_f_merge(a, b, ga, gb),), "merge_fwd", (nb,), [br_a, br_b, proj, proj],
                 [rowD, rowD, ga_spec, gb_spec], [((S, D), BF16, rowD)])[0]
    mix = _matmul(merged, w_out_f, "nn", F32, "mm_out")

    x1, hn2 = _ew(_f_res_normmod, "norm2_fwd", (nb,), [x2, mix, gt1, norm2_g, sc2, sh2],
                  [rowD, rowD, vecD, vecD, vecD, vecD], [((S, D), F32, rowD), ((S, D), BF16, rowD)])
    _, landed = _split_wait("ag_late_wait", ag_state, hn2, _gather_wait_copies)
    w_up_f, w_dn_g = _pair_forward(landed, "ag_late_pair")
    w_dn_f = w_dn_g.reshape(-1, w_dn_g.shape[2])
    up_all = _matmul(hn2, w_up_f, "nn", F32, "mm_up")

    cb_ff = _div_tile(FFN, 256, LANES)
    ff_args = dict(kw=KFF, ncol=FFN // cb_ff, ins=[up_all], in_specs=[colS(cb_ff)], extras=[up_all],
                   extra_specs=[colS(cb_ff, FFN // cb_ff)], w=ff_cw, w_spec=wS(KFF, cb_ff),
                   pre=ident, post=lambda j, cv, up: _silu(cv) * up)
    hff = _conv_fwd_call("ffn_conv_fwd", out_shape=jax.ShapeDtypeStruct((S, FFN), BF16), out_spec=colS(cb_ff), **ff_args)
    ffo = _matmul(hff, w_dn_f, "nn", F32, "mm_down")

    gf2 = final_norm_g.reshape(1, D)

    def loss_bwd(a, f, gt, gf, t):
        val, vjp = jax.vjp(_f_loss, a, f, gt, gf, t)
        da, df, dgt, dgf, _ = vjp(jnp.ones((), F32))
        return da, df, dgt, dgf, jnp.zeros((1, LANES), F32) + val

    dx1_l, dffo, dgt2, dgf, loss_v = _ew(
        loss_bwd, "loss_bwd", (nb,), [x1, ffo, gt2, gf2, tgt], [rowD, rowD, vecD, vecD, rowD],
        [((S, D), F32, rowD), ((S, D), BF16, rowD), ((1, D), F32, vecD), ((1, D), F32, vecD),
         ((1, LANES), F32, vecL)], acc=(2, 3, 4))

    dhff = _matmul(dffo, w_dn_f, "nt", F32, "mm_down_dx")
    g_w_dn = _matmul(hff, dffo, "tn", F32, "mm_down_dw")

    slots = jnp.stack([chip, ci]).astype(jnp.int32)
    rs_groups = []

    def rs_pair_begin(parts, tag):
        lands = [lax.empty((4, p.shape[1] // 2, p.shape[2]), F32) for p in parts]
        return _split_start("rs_pair_start_" + tag, parts, lands, core, len(parts), _pair_copies)

    def rs_chips_begin(pair_state, nms, tag, after):
        parts, rbs = _split_wait("rs_pair_wait_" + tag, pair_state, after, _pair_copies)
        q16 = [_pair_add(p, rb, core, "rs_pair_add_" + nm) for p, rb, nm in zip(parts, rbs, nms)]
        state, token = _chip_scatter_start(q16, core, "rs_chips_start_" + tag)
        rs_groups.append((state, nms, tag))
        return token

    pair_a, ptok_a = rs_pair_begin([g_w_dn.reshape(4, FFN // 4, D)], "a")

    d_upall, g_ffcw = _conv_bwd_call(
        "ffn_conv_bwd", dout=dhff, dout_spec=colS(cb_ff),
        dio_shapes=[jax.ShapeDtypeStruct((2, S, FFN), BF16)],
        dio_specs=[pl.BlockSpec((2, S, cb_ff), lambda j: (0, 0, j))], dio_pack=lambda dg, du: [(dg, du)],
        dw_shape=jax.ShapeDtypeStruct((KFF, FFN), F32), dw_spec=wS(KFF, cb_ff), after=[ptok_a], **ff_args)
    tok_a = rs_chips_begin(pair_a, ["ffn_w_down"], "a", d_upall)
    dhn2 = _matmul(d_upall, w_up_f, "nt", F32, "mm_up_dx", after=[tok_a])
    g_w_up = _matmul(hn2, d_upall, "tn", F32, "mm_up_dw", out_groups=4)
    pair_b, ptok_b = rs_pair_begin([g_w_up], "b")

    def res2_bwd(a, mx, gt, g, sc, sh, dx1, dhn):
        _, vjp = jax.vjp(_f_res_normmod, a, mx, gt, g, sc, sh)
        return vjp((dx1, dhn))

    dx_r, dmix, dgt1, dg2, dsc2, dsh2 = _ew(
        res2_bwd, "norm2_bwd", (nb,), [x2, mix, gt1, norm2_g, sc2, sh2, dx1_l, dhn2],
        [rowD, rowD, vecD, vecD, vecD, vecD, rowD, rowD],
        [((S, D), F32, rowD), ((S, D), BF16, rowD)] + [((1, D), F32, vecD)] * 4, acc=(2, 3, 4, 5), after=[ptok_b])

    dmerged = _matmul(dmix, w_out_f, "nt", F32, "mm_out_dx")
    g_w_out = _matmul(merged, dmix, "tn", F32, "mm_out_dw")

    def merge_bwd(a, b, ga, gb, dm):
        _, vjp = jax.vjp(_f_merge, a, b, ga, gb)
        da, db, dga, dgb = vjp(dm)
        return da, db, jnp.concatenate([dga, dgb], axis=1)

    d_bra, d_brb, dproj = _ew(merge_bwd, "merge_bwd", (nb,), [br_a, br_b, proj, proj, dmerged],
                              [rowD, rowD, ga_spec, gb_spec, rowD],
                              [((S, D), BF16, rowD), ((S, D), BF16, rowD),
                               ((S, NA + LANES), BF16, _row(bs, 2 * D, a_ga // (2 * D)))])

    tok_b = rs_chips_begin(pair_b, ["ffn_w_up"], "b", d_brb)
    d_on = _matmul(d_bra, w_do_f, "nt", F32, "mm_dn_o_dx", after=[tok_b])
    g_w_do = _matmul(on, d_bra, "tn", F32, "mm_dn_o_dw", out_groups=4)
    d_ub = _matmul(d_brb, w_co_f, "nt", F32, "mm_cf_o_dx")
    g_w_co = _matmul(ub, d_brb, "tn", F32, "mm_cf_o_dw", out_groups=4)
    pair_c, ptok_c = rs_pair_begin([g_w_do, g_w_co, g_w_out.reshape(4, D // 4, D)], "c")

    def cf_ln_bwd(u, g, b, du):
        _, vjp = jax.vjp(_f_cf_ln, u, g, b)
        return vjp(du)

    d_uc, g_cflg, g_cflb = _ew(cf_ln_bwd, "cf_ln_bwd", (nb,), [uc, cf_ln_g, cf_ln_b, d_ub], [rowC, vecC, vecC, rowC],
                               [((S, CFW), F32, rowC), ((1, CFW), F32, vecC), ((1, CFW), F32, vecC)], acc=(1, 2),
                               after=[ptok_c])
    dproj_sds = jax.ShapeDtypeStruct((S, NA + LANES), BF16)
    dproj, g_cfcw = _conv_bwd_call(
        "cf_conv_bwd", dout=d_uc, dout_spec=colS(cb_cf), dio_shapes=[dproj_sds],
        dio_specs=[colS(2 * cb_cf, a_glu // (2 * cb_cf))], dio_pack=lambda dv, dg: [jnp.concatenate([dv, dg], axis=1)],
        dw_shape=jax.ShapeDtypeStruct((KCF, CFW), F32), dw_spec=wS(KCF, cb_cf), alias=dproj, **cf_args)

    def dn_post_bwd(o, z, g, d):
        _, vjp = jax.vjp(_f_dn_post, o, z, g)
        return vjp(d)

    tok_c = rs_chips_begin(pair_c, ["dn_w_o", "cf_w_o", "w_out"], "c", g_cfcw)
    d_o, dproj, g_dnng = _ew(dn_post_bwd, "dn_post_bwd", (nbh, NH), [o_dn, proj, dn_norm_g, d_on],
                             [o_spec, z_spec, ng_spec, oh_spec],
                             [((NH, S, DH), F32, o_spec), ((S, NA + LANES), BF16, z_spec), ((1, DH), F32, ng_spec)],
                             acc=(2,), alias=(dproj, 1), after=[tok_c])

    dqkvn, dgb_b, dbb_b = _dn_bwd_call(qkvn, gb_b, bb_b, states, d_o)
    dproj, g_dncw = _conv_bwd_call(
        "dn_conv_bwd", dout=dqkvn, dout_spec=qkv_spec, dio_shapes=[dproj_sds], dio_specs=[colS(DH)],
        dio_pack=lambda d: [d], dw_shape=jax.ShapeDtypeStruct((KDN, 3 * DNW), F32), dw_spec=wS(KDN, DH),
        alias=dproj, **dn_args)

    def gate_bwd(a, p, q, db, dg):
        lane = lax.broadcasted_iota(jnp.int32, a.shape, 1)
        d = jnp.zeros(a.shape, F32)
        for h in range(NH):
            d = d + jnp.where(lane == h, db[h], 0.0) + jnp.where(lane == NH + h, dg[h], 0.0)
        _, vjp = jax.vjp(gate_fn, a, p, q)
        return vjp(d)

    dproj, g_alp, g_dtb = _ew(gate_bwd, "dn_gate_bwd", (nb,), [proj, alp, dtb, dbb_b, dgb_b],
                              [ba_spec, vecL, vecL, hrow, hrow],
                              [((S, NA + LANES), BF16, ba_spec), ((1, LANES), F32, vecL), ((1, LANES), F32, vecL)],
                              acc=(1, 2), alias=(dproj, 0))

    g_w_aug = _matmul(hn1, dproj, "tn", F32, "mm_in_dw")
    def aug_slices(lo, n):
        out = []
        for o, a, m in sorted(segs):
            s, e = max(lo, o), min(lo + n, o + m)
            if s < e:
                out.append(g_w_aug[:, a + s - o:a + e - o])
        return out

    g_w_in = jnp.concatenate([jnp.concatenate(aug_slices(j * CS, CS), axis=1) for j in range(4)],
                             axis=0).reshape(4, D, CS)
    pair_d, ptok_d = rs_pair_begin([g_w_in], "d")
    dhn1 = _matmul(dproj, w_aug, "nt", F32, "mm_in_dx", after=[ptok_d])

    def norm1_bwd(a, g, sc, sh, dhn, dxr):
        _, vjp = jax.vjp(_f_normmod, a, g, sc, sh)
        da, dg, dsc, dsh = vjp(dhn)
        return da + dxr, dg, dsc, dsh

    grad_x, dg1, dsc1, dsh1 = _ew(norm1_bwd, "norm1_bwd", (nb,), [x2, norm1_g, sc1, sh1, dhn1, dx_r],
                                  [rowD, vecD, vecD, vecD, rowD, rowD],
                                  [((S, D), F32, rowD)] + [((1, D), F32, vecD)] * 3, acc=(1, 2, 3))

    dmod = jnp.concatenate([dsh1, dsc1, dgt1, dsh2, dsc2, dgt2], axis=1)
    sm2 = [dmod, dg1, dg2, dgf, g_alp, g_dtb, g_dnng, g_cflg, g_cflb, g_dncw, g_cfcw, g_ffcw]
    sm2_shapes = [tuple(a.shape) for a in sm2]
    g3 = _all_gather_small(_pack_small(sm2))
    tok_d = rs_chips_begin(pair_d, ["w_in"], "d", g3)

    names = ["w_ada", "b_ada", "norm1_g", "w_in", "dn_conv_w", "dn_a_log", "dn_dt_bias", "dn_norm_g", "dn_w_o",
             "cf_conv_w", "cf_ln_g", "cf_ln_b", "cf_w_o", "w_out", "norm2_g", "ffn_w_up", "ffn_conv_w", "ffn_w_down",
             "final_norm_g"]
    wts = dict(zip(names, [w_ada, b_ada, norm1_g, w_in, dn_conv_w, dn_a_log, dn_dt_bias, dn_norm_g, dn_w_o, cf_conv_w,
                           cf_ln_g, cf_ln_b, cf_w_o, w_out, norm2_g, ffn_w_up, ffn_conv_w, ffn_w_down, final_norm_g]))
    ms = dict(zip(names, [m_w_ada, m_b_ada, m_norm1_g, m_w_in, m_dn_conv_w, m_dn_a_log, m_dn_dt_bias, m_dn_norm_g,
                          m_dn_w_o, m_cf_conv_w, m_cf_ln_g, m_cf_ln_b, m_cf_w_o, m_w_out, m_norm2_g, m_ffn_w_up,
                          m_ffn_conv_w, m_ffn_w_down, m_final_norm_g]))
    vs = dict(zip(names, [v_w_ada, v_b_ada, v_norm1_g, v_w_in, v_dn_conv_w, v_dn_a_log, v_dn_dt_bias, v_dn_norm_g,
                          v_dn_w_o, v_cf_conv_w, v_cf_ln_g, v_cf_ln_b, v_cf_w_o, v_w_out, v_norm2_g, v_ffn_w_up,
                          v_ffn_conv_w, v_ffn_w_down, v_final_norm_g]))
    grads, delta, new_m, new_v = {}, {}, {}, {}

    def adam_large(n, g):
        grads[n] = g.reshape(wts[n].shape)
        d_, m_, v_ = _adamw(wts[n][0], grads[n][0], ms[n][0], vs[n][0], "adamw_" + n)
        delta[n], new_m[n], new_v[n] = d_[None], m_[None], v_[None]

    def rs_finish(groups, after, tag):
        hs, nms_all = [], []
        for state, nms, t in groups:
            q16, r3s = _chip_scatter_wait(state, after, "rs_chips_wait_" + t)
            for q, r3, nm in zip(q16, r3s, nms):
                hs.append(_chip_sum(q, r3, slots, "rs_chip_sum_" + nm))
                nms_all.append(nm)
        for nm, g in zip(nms_all, _pair_gather(hs, "rs_pair_gather_" + tag)):
            adam_large(nm, g)

    rs_finish(rs_groups[:3], tok_d, "abc")

    ssum = _sum_slots(g3, "small_sum")
    dmod_all = _unpack_small(g3, sm2_shapes[:1])[0].reshape(8, 6 * D)
    (g_b_ada, gs_n1, gs_n2, gs_fn, gs_alp, gs_dtb, gs_dnng, gs_cflg, gs_cflb, gs_dncw, gs_cfcw,
     gs_ffcw) = _unpack_small(ssum, sm2_shapes)
    gs_alog, gs_dtbias = gs_alp[:, NH:2 * NH], gs_dtb[:, NH:2 * NH]
    gs_dncw = lax.dynamic_slice(gs_dncw, (0, chip * (3 * DNW // 4)), (KDN, 3 * DNW // 4))
    gs_cfcw = lax.dynamic_slice(gs_cfcw, (0, chip * (CFW // 4)), (KCF, CFW // 4))
    gs_ffcw = lax.dynamic_slice(gs_ffcw, (0, chip * (FFN // 4)), (KFF, FFN // 4))

    dmod_sh = lax.dynamic_slice(dmod_all, (0, chip * CA), (8, CA))

    def wada_fn(cc, dm):
        return (_mm(_silu(cc), dm, TN),)

    g_w_ada = _ew(wada_fn, "ada_dw", (CA // tn_a,), [c_all, dmod_sh],
                  [_full((8, D)), pl.BlockSpec((8, tn_a), lambda j: (0, j))],
                  [((D, CA), F32, pl.BlockSpec((D, tn_a), lambda j: (0, j)))])[0]

    loss = lax.psum(loss_v[0, 0], ("x", "y", "c"))

    adam_large("w_ada", g_w_ada)
    small_grads = {"b_ada": g_b_ada, "norm1_g": gs_n1, "dn_conv_w": gs_dncw, "dn_a_log": gs_alog,
                   "dn_dt_bias": gs_dtbias, "dn_norm_g": gs_dnng, "cf_conv_w": gs_cfcw, "cf_ln_g": gs_cflg,
                   "cf_ln_b": gs_cflb, "norm2_g": gs_n2, "ffn_conv_w": gs_ffcw, "final_norm_g": gs_fn}
    small = [n for n in names if n in small_grads]
    for n in small:
        grads[n] = small_grads[n].reshape(wts[n].shape)
    sm_sh = [tuple(wts[n].shape) for n in small]
    d_, m_, v_ = _adamw(_pack_small([wts[n] for n in small]), _pack_small([grads[n] for n in small]),
                        _pack_small([ms[n] for n in small]), _pack_small([vs[n] for n in small]), "adamw_small")
    for n, a, b_, c_ in zip(small, _unpack_small(d_, sm_sh), _unpack_small(m_, sm_sh), _unpack_small(v_, sm_sh)):
        delta[n], new_m[n], new_v[n] = a, b_, c_
    behind = sum(delta[n][0, :8, :8] for n in ["w_ada", "ffn_w_up", "ffn_w_down", "w_out", "dn_w_o", "cf_w_o"])
    rs_finish(rs_groups[3:], behind, "d")

    return (loss, grad_x[None], *[grads[n] for n in names], *[delta[n] for n in names],
            *[new_m[n] for n in names], *[new_v[n] for n in names])
```

```python
import functools

import jax
import jax.numpy as jnp
from jax import lax
from jax.experimental import pallas as pl
from jax.experimental.pallas import tpu as pltpu

F32 = jnp.float32
BF16 = jnp.bfloat16
EPS = 1e-6
LANES = 128
VMEM_LIMIT = 48 * 1024 * 1024
DN_CHUNK = 128
ADAM_LR, ADAM_B1, ADAM_B2, ADAM_EPS, ADAM_WD, ADAM_STEP = 0.001, 0.9, 0.999, 1e-08, 0.01, 10
MESH = pl.DeviceIdType.MESH

NN = (((1,), (0,)), ((), ()))
NT = (((1,), (1,)), ((), ()))
TN = (((0,), (0,)), ((), ()))
_DIMS = {"nn": NN, "nt": NT, "tn": TN}


def _mm(a, b, dims):
    return lax.dot_general(a.astype(BF16), b.astype(BF16), dims, preferred_element_type=F32)


def _mmx(a, b, dims):
    return lax.dot_general(a, b, dims, precision=lax.Precision.HIGH, preferred_element_type=F32)


def _div_tile(n, target, mult):
    best = None
    t = mult
    while t <= min(n, target):
        if n % t == 0:
            best = t
        t += mult
    return best if best is not None else n


def _params(sem=None):
    kw = dict(vmem_limit_bytes=VMEM_LIMIT)
    if sem is not None:
        kw["dimension_semantics"] = sem
    return pltpu.CompilerParams(**kw)


def _sigmoid(x):
    return jax.nn.sigmoid(x)


def _silu(x):
    return x * jax.nn.sigmoid(x)


def _softplus(x):
    return jnp.maximum(x, 0.0) + jnp.log(1.0 + jnp.exp(-jnp.abs(x)))


def _view(arr):
    if arr.ndim == 3:
        return arr.shape[1], arr.shape[0] * arr.shape[2], arr.shape[0]
    return arr.shape[0], arr.shape[1], 1


def _tile_spec(groups, cols, tr, tc, rsel, csel):
    if groups > 1:
        per = cols // groups // tc
        return pl.BlockSpec((None, tr, tc), lambda i, j, k: (csel(i, j, k) // per, rsel(i, j, k), csel(i, j, k) % per))
    return pl.BlockSpec((tr, tc), lambda i, j, k: (rsel(i, j, k), csel(i, j, k)))


def _matmul(a, b, mode, out_dtype, name, out_groups=1, after=()):
    ar, ac, ag = _view(a)
    br, bc, bg = _view(b)
    if mode == "nn":
        M, K, N = ar, ac, bc
        kdiv, mdiv, ndiv = ac // ag, M, min(bc // bg, N // out_groups)
    elif mode == "nt":
        M, K, N = ar, ac, br
        kdiv, mdiv, ndiv = min(ac // ag, bc // bg), M, N // out_groups
    else:
        K, M, N = ar, ac, bc
        kdiv, mdiv, ndiv = K, ac // ag, min(bc // bg, N // out_groups)
    tm = _div_tile(mdiv, 1024, LANES)
    tn = _div_tile(ndiv, 2048 if mode == "nt" else 1536, LANES)
    tk = _div_tile(kdiv, 2048, LANES)
    nk = K // tk
    dims = _DIMS[mode]
    si, sj, sk = (lambda i, j, k: i), (lambda i, j, k: j), (lambda i, j, k: k)
    a_spec = {"nn": _tile_spec(ag, ac, tm, tk, si, sk), "nt": _tile_spec(ag, ac, tm, tk, si, sk),
              "tn": _tile_spec(ag, ac, tk, tm, sk, si)}[mode]
    b_spec = {"nn": _tile_spec(bg, bc, tk, tn, sk, sj), "nt": _tile_spec(bg, bc, tn, tk, sj, sk),
              "tn": _tile_spec(bg, bc, tk, tn, sk, sj)}[mode]
    out_shape = (M, N) if out_groups == 1 else (out_groups, M, N // out_groups)

    n_after = len(after)

    def body(*refs):
        a_ref, b_ref, o_ref = refs[0], refs[1], refs[2 + n_after]
        if nk == 1:
            o_ref[...] = lax.dot_general(a_ref[...], b_ref[...], dims, preferred_element_type=F32).astype(o_ref.dtype)
            return
        acc_ref = refs[3 + n_after]
        k = pl.program_id(2)

        @pl.when(k == 0)
        def _():
            acc_ref[...] = jnp.zeros_like(acc_ref)

        acc_ref[...] += lax.dot_general(a_ref[...], b_ref[...], dims, preferred_element_type=F32)

        @pl.when(k == nk - 1)
        def _():
            o_ref[...] = acc_ref[...].astype(o_ref.dtype)

    return pl.pallas_call(
        body, name=name, grid=(M // tm, N // tn, nk),
        in_specs=[a_spec, b_spec] + [pl.BlockSpec(memory_space=pl.ANY)] * n_after,
        out_specs=_tile_spec(out_groups, N, tm, tn, si, sj),
        out_shape=jax.ShapeDtypeStruct(out_shape, out_dtype),
        scratch_shapes=[pltpu.VMEM((tm, tn), F32)] if nk > 1 else [],
        compiler_params=_params(("parallel", "parallel", "arbitrary")),
    )(a, b, *after)


def _ew(fn, name, grid, ins, in_specs, outs, acc=(), alias=None, after=()):
    n_in = len(ins)
    n_ax = len(grid)
    extra, aliases = list(after), {}
    if alias is not None:
        extra, aliases = extra + [alias[0]], {n_in + len(after): alias[1]}
    extra_specs = [pl.BlockSpec(memory_space=pl.ANY)] * len(extra)

    def body(*refs):
        in_refs, out_refs = refs[:n_in], refs[n_in + len(extra):]
        ids = [pl.program_id(a) for a in range(n_ax)]
        res = fn(*[r[...] for r in in_refs])
        first = ids[0] == 0
        for t in ids[1:]:
            first = jnp.logical_and(first, t == 0)
        for idx, (r, val) in enumerate(zip(out_refs, res)):
            if idx in acc:
                @pl.when(first)
                def _(r=r, val=val):
                    r[...] = val.astype(r.dtype)

                @pl.when(jnp.logical_not(first))
                def _(r=r, val=val):
                    r[...] += val.astype(r.dtype)
            elif isinstance(val, tuple):
                for t, part in enumerate(val):
                    r[t] = part.astype(r.dtype)
            else:
                r[...] = val.astype(r.dtype)

    return pl.pallas_call(
        body, name=name, grid=grid, in_specs=list(in_specs) + extra_specs,
        out_specs=[o[2] for o in outs],
        out_shape=[jax.ShapeDtypeStruct(o[0], o[1]) for o in outs],
        input_output_aliases=aliases,
        compiler_params=_params(("arbitrary",) * n_ax),
    )(*ins, *extra)


def _ew_slot(fn, name, grid, slots, ins, in_specs, out_shape, out_dtype, out_spec, after=()):
    n_in = len(ins)

    def body(s_ref, *refs):
        refs[-1][...] = fn(*[r[...] for r in refs[:n_in]]).astype(refs[-1].dtype)

    return pl.pallas_call(
        body, name=name,
        grid_spec=pltpu.PrefetchScalarGridSpec(
            num_scalar_prefetch=1, grid=grid,
            in_specs=list(in_specs) + [pl.BlockSpec(memory_space=pl.ANY)] * len(after), out_specs=out_spec),
        out_shape=jax.ShapeDtypeStruct(out_shape, out_dtype),
        compiler_params=_params(("arbitrary",) * len(grid)),
    )(slots, *ins, *after)


def _row(bs, w, col=0):
    return pl.BlockSpec((bs, w), lambda i, col=col: (i, col))


def _full(shape):
    nd = len(shape)
    return pl.BlockSpec(tuple(shape), lambda *_: (0,) * nd)


def _rms(x, g):
    return x * lax.rsqrt(jnp.mean(x * x, axis=-1, keepdims=True) + EPS) * g


def _f_normmod(x, g, sc, sh):
    return _rms(x, g) * (1.0 + sc) + sh


def _f_res_normmod(x, mix, gt, g, sc, sh):
    x1 = x + gt * mix
    return x1, _f_normmod(x1, g, sc, sh)


def _f_loss(x1, f, gt, gf, tgt):
    y = _rms(x1 + gt * f, gf)
    return 0.5 * jnp.sum(jnp.mean(jnp.square(y - tgt), axis=-1))


def _f_dn_gate(nh, ba, alp, dtb):
    lane = lax.broadcasted_iota(jnp.int32, ba.shape, 1)
    m = (lane < nh).astype(F32)
    beta = _sigmoid(ba)
    g = -jnp.exp(alp) * _softplus(ba + dtb)
    return m * beta + (1.0 - m) * g


def _f_dn_post(o, z, g):
    return o * lax.rsqrt(jnp.mean(o * o, axis=-1, keepdims=True) + EPS) * g * _silu(z)


def _f_cf_ln(u, g, b):
    mu = jnp.mean(u, axis=-1, keepdims=True)
    xc = u - mu
    y = xc * lax.rsqrt(jnp.mean(xc * xc, axis=-1, keepdims=True) + EPS)
    return _silu(y * g + b)


def _f_merge(a, b, ga, gb):
    return _sigmoid(ga) * a + _sigmoid(gb) * b


def _shift_down(u, d, rows):
    if d == 0:
        return u
    return jnp.where(rows >= d, pltpu.roll(u, d, 0), 0.0)


def _shift_up(u, d, rows):
    if d == 0:
        return u
    s = u.shape[0]
    return jnp.where(rows < s - d, pltpu.roll(u, s - d, 0), 0.0)


def _conv(u, w_ref, kw, rows):
    acc = None
    for k in range(kw):
        t = w_ref[k:k + 1, :] * _shift_down(u, kw - 1 - k, rows)
        acc = t if acc is None else acc + t
    return acc


def _conv_t(dc, w_ref, kw, rows):
    acc = None
    for k in range(kw):
        t = w_ref[k:k + 1, :] * _shift_up(dc, kw - 1 - k, rows)
        acc = t if acc is None else acc + t
    return acc


def _conv_fwd_call(name, kw, ncol, ins, in_specs, extras, extra_specs, w, w_spec, pre, post, out_shape, out_spec):
    n_in, n_ex = len(ins), len(extras)

    def body(*refs):
        in_refs, ex_refs = refs[:n_in], refs[n_in:n_in + n_ex]
        w_ref, out_ref = refs[n_in + n_ex], refs[n_in + n_ex + 1]
        j = pl.program_id(0)
        u = pre(*[r[...] for r in in_refs])
        rows = lax.broadcasted_iota(jnp.int32, u.shape, 0)
        cv = _conv(u, w_ref, kw, rows)
        out_ref[...] = post(j, cv, *[r[...] for r in ex_refs]).astype(out_ref.dtype)

    return pl.pallas_call(
        body, name=name, grid=(ncol,), in_specs=list(in_specs) + list(extra_specs) + [w_spec],
        out_specs=out_spec, out_shape=out_shape, compiler_params=_params(("arbitrary",)),
    )(*ins, *extras, w)


def _conv_bwd_call(name, kw, ncol, ins, in_specs, extras, extra_specs, w, w_spec, pre, post, dout, dout_spec,
                   dio_shapes, dio_specs, dio_pack, dw_shape, dw_spec, alias=None, after=()):
    n_in, n_ex, n_io = len(ins), len(extras), len(dio_shapes)
    al, aliases = list(after), {}
    if alias is not None:
        al, aliases = al + [alias], {n_in + n_ex + 2 + len(after): 0}
    al_specs = [pl.BlockSpec(memory_space=pl.ANY)] * len(al)

    def body(*refs):
        in_refs, ex_refs = refs[:n_in], refs[n_in:n_in + n_ex]
        w_ref, dout_ref = refs[n_in + n_ex], refs[n_in + n_ex + 1]
        outs = refs[n_in + n_ex + 2 + len(al):]
        dio_refs, dw_ref = outs[:n_io], outs[n_io]
        j = pl.program_id(0)
        u, pre_vjp = jax.vjp(pre, *[r[...] for r in in_refs])
        rows = lax.broadcasted_iota(jnp.int32, u.shape, 0)
        cv = _conv(u, w_ref, kw, rows)
        _, post_vjp = jax.vjp(lambda cc, *ex: post(j, cc, *ex), cv, *[r[...] for r in ex_refs])
        g = post_vjp(dout_ref[...].astype(F32))
        dc = g[0]
        for k in range(kw):
            dw_ref[k:k + 1, :] = jnp.sum(dc * _shift_down(u, kw - 1 - k, rows), axis=0, keepdims=True)
        du = _conv_t(dc, w_ref, kw, rows)
        for r, val in zip(dio_refs, dio_pack(*pre_vjp(du), *g[1:])):
            if isinstance(val, tuple):
                for t, part in enumerate(val):
                    r[t] = part.astype(r.dtype)
            else:
                r[...] = val.astype(r.dtype)

    return pl.pallas_call(
        body, name=name, grid=(ncol,),
        in_specs=list(in_specs) + list(extra_specs) + [w_spec, dout_spec] + al_specs,
        out_specs=list(dio_specs) + [dw_spec],
        out_shape=list(dio_shapes) + [dw_shape],
        input_output_aliases=aliases,
        compiler_params=_params(("arbitrary",)),
    )(*ins, *extras, w, dout, *al)


def _tri_inverse(a):
    c = a.shape[0]
    ii = lax.broadcasted_iota(jnp.int32, (c, c), 0)
    jj = lax.broadcasted_iota(jnp.int32, (c, c), 1)
    eye = (ii == jj).astype(F32)

    def same_block(bits):
        return jnp.right_shift(ii, bits) == jnp.right_shift(jj, bits)

    d = jnp.where(same_block(3), a, 0.0)
    d2 = _mmx(d, d, NN)
    yield
    t = (eye - d) + _mmx(eye - d, d2, NN)
    d4 = _mmx(d2, d2, NN)
    yield
    t = t + _mmx(t, d4, NN)
    yield
    bits = 3
    while (1 << bits) < c:
        low = jnp.where(jnp.logical_and(same_block(bits + 1), jnp.logical_not(same_block(bits))), a, 0.0)
        tl = _mmx(t, low, NN)
        yield
        t = t - _mmx(tl, t, NN)
        yield
        bits += 1
    return t


def _lockstep(gens):
    out = [None] * len(gens)
    live = list(range(len(gens)))
    while live:
        nxt = []
        for i in live:
            try:
                next(gens[i])
                nxt.append(i)
            except StopIteration as e:
                out[i] = e.value
        live = nxt
    return out


def _dn_common(q, k, v, gb, bb):
    c = q.shape[0]
    ii = lax.broadcasted_iota(jnp.int32, (c, c), 0)
    jj = lax.broadcasted_iota(jnp.int32, (c, c), 1)
    causal = jj <= ii
    strict = jj < ii
    low = causal.astype(F32)
    ones = jnp.ones((c, LANES), F32)
    gc = _mmx(low, gb, NN)
    kb = k * bb
    vb = v * bb
    kk = _mm(kb, k, NT)
    qk = _mm(q, k, NT)
    yield
    diff = (_mmx(gc, ones, NT) - _mmx(ones, gc, NT)) * (1.0 / LANES)
    gl = jnp.sum(gb, axis=0, keepdims=True)
    eg = jnp.exp(gc)
    egm = jnp.exp(gl - gc)
    egl = jnp.exp(gl)
    kbg = kb * eg
    yield
    decay = jnp.where(causal, jnp.exp(jnp.where(causal, diff, 0.0)), 0.0)
    t = yield from _tri_inverse(jnp.where(strict, kk * decay, 0.0))
    attn = qk * decay
    return dict(causal=causal, strict=strict, low=low, ones=ones, decay=decay, eg=eg, egm=egm, egl=egl,
                kb=kb, vb=vb, kbg=kbg, kk=kk, t=t, qk=qk, attn=attn, qd=q * eg, kd=k * egm)


def _dn_fwd_stages(q, k, v, gb, bb, s):
    m = yield from _dn_common(q, k, v, gb, bb)
    r = m["vb"] - _mm(m["kbg"], s, NN)
    qs = _mm(m["qd"], s, NN)
    yield
    u = _mmx(m["t"], r, NN)
    yield
    o = qs + _mm(m["attn"], u, NN)
    s2 = s * m["egl"] + _mm(m["kd"], u, TN)
    return o, s2


def _dn_bwd_stages(q, k, v, gb, bb, s, do, dsp):
    m = yield from _dn_common(q, k, v, gb, bb)
    c = q.shape[0]
    t, decay, eg, egm, egl = m["t"], m["decay"], m["eg"], m["egm"], m["egl"]
    r = m["vb"] - _mm(m["kbg"], s, NN)
    du = _mm(m["attn"], do, TN) + _mm(m["kd"], dsp, NN)
    dqd = _mm(do, s, NT)
    ds = dsp * egl + _mm(m["qd"], do, TN)
    degl = jnp.sum(jnp.sum(dsp * s, axis=1, keepdims=True), axis=0, keepdims=True)
    yield
    u = _mmx(t, r, NN)
    dr = _mmx(t, du, TN)
    yield
    dattn = jnp.where(m["causal"], _mm(do, u, NT), 0.0)
    dkd = _mm(u, dsp, NT)
    da = jnp.where(m["strict"], -_mm(dr, u, NT), 0.0)
    dkbg = -_mm(dr, s, NT)
    ds = ds - _mm(m["kbg"], dr, TN)
    yield
    dkk = da * decay
    dqk = dattn * decay
    ddiff = (da * m["kk"] + dattn * m["qk"]) * decay
    dgc = _mmx(ddiff, m["ones"], NN) - _mmx(ddiff, m["ones"], TN)
    dkb = _mm(dkk, k, NN) + dkbg * eg
    dk = _mm(dkk, m["kb"], TN) + _mm(dqk, q, TN) + dkd * egm + dkb * bb
    dq = _mm(dqk, k, NN) + dqd * eg
    yield
    dgc = dgc + jnp.sum(dqd * q + dkbg * m["kb"], axis=-1, keepdims=True) * eg
    tt = jnp.sum(dkd * k, axis=-1, keepdims=True) * egm
    dgc = dgc - tt
    dgl = jnp.sum(tt, axis=0, keepdims=True) + degl * egl
    dbb = jnp.sum(dkb * k + dr * v, axis=-1, keepdims=True) + jnp.zeros((c, LANES), F32)
    dv = dr * bb
    dgb = _mmx(m["low"], dgc, TN) + dgl
    return dq, dk, dv, dgb, dbb, ds


def _dn_fwd_chunk(q, k, v, gb, bb, s):
    return _lockstep([_dn_fwd_stages(q, k, v, gb, bb, s)])[0]


def _dn_bwd_chunk(q, k, v, gb, bb, s, do, dsp):
    return _lockstep([_dn_bwd_stages(q, k, v, gb, bb, s, do, dsp)])[0]


def _dn_fwd_call(qkvn, gb, bb):
    _, nh, s, dh = qkvn.shape
    c = min(DN_CHUNK, s)
    n = s // c
    hb = nh

    def body(q_ref, k_ref, v_ref, g_ref, b_ref, o_ref, st_ref, s_ref):
        @pl.when(pl.program_id(1) == 0)
        def _():
            s_ref[...] = jnp.zeros_like(s_ref)

        st_ref[...] = s_ref[...]
        heads = [_dn_fwd_stages(q_ref[h], k_ref[h], v_ref[h], g_ref[h], b_ref[h], s_ref[h]) for h in range(hb)]
        for h, (o, s2) in enumerate(_lockstep(heads)):
            o_ref[h] = o
            s_ref[h] = s2

    def qspec(t):
        return pl.BlockSpec((None, hb, c, dh), lambda i, j, t=t: (t, i, j, 0))

    hs = pl.BlockSpec((hb, c, dh), lambda i, j: (i, j, 0))
    return pl.pallas_call(
        body, name="dn_fwd", grid=(nh // hb, n),
        in_specs=[qspec(0), qspec(1), qspec(2), hs, hs],
        out_specs=[hs, pl.BlockSpec((hb, None, dh, dh), lambda i, j: (i, j, 0, 0))],
        out_shape=[jax.ShapeDtypeStruct((nh, s, dh), F32), jax.ShapeDtypeStruct((nh, n, dh, dh), F32)],
        scratch_shapes=[pltpu.VMEM((hb, dh, dh), F32)],
        compiler_params=_params(("arbitrary", "arbitrary")),
    )(qkvn, qkvn, qkvn, gb, bb)


def _dn_bwd_call(qkvn, gb, bb, states, do):
    _, nh, s, dh = qkvn.shape
    c = min(DN_CHUNK, s)
    n = s // c
    hb = nh

    def body(q_ref, k_ref, v_ref, g_ref, b_ref, st_ref, do_ref, dqkv_ref, dg_ref, db_ref, ds_ref):
        @pl.when(pl.program_id(1) == 0)
        def _():
            ds_ref[...] = jnp.zeros_like(ds_ref)

        heads = [_dn_bwd_stages(q_ref[h], k_ref[h], v_ref[h], g_ref[h], b_ref[h], st_ref[h], do_ref[h], ds_ref[h])
                 for h in range(hb)]
        for h, (dq, dk, dv, dg, db, ds) in enumerate(_lockstep(heads)):
            dqkv_ref[0, h] = dq
            dqkv_ref[1, h] = dk
            dqkv_ref[2, h] = dv
            dg_ref[h] = dg
            db_ref[h] = db
            ds_ref[h] = ds

    def qspec(t):
        return pl.BlockSpec((None, hb, c, dh), lambda i, j, t=t: (t, i, n - 1 - j, 0))

    hs = pl.BlockSpec((hb, c, dh), lambda i, j: (i, n - 1 - j, 0))
    sh = jax.ShapeDtypeStruct((nh, s, dh), F32)
    return pl.pallas_call(
        body, name="dn_bwd", grid=(nh // hb, n),
        in_specs=[qspec(0), qspec(1), qspec(2), hs, hs,
                  pl.BlockSpec((hb, None, dh, dh), lambda i, j: (i, n - 1 - j, 0, 0)), hs],
        out_specs=[pl.BlockSpec((3, hb, c, dh), lambda i, j: (0, i, n - 1 - j, 0)), hs, hs],
        out_shape=[jax.ShapeDtypeStruct((3, nh, s, dh), F32), sh, sh],
        scratch_shapes=[pltpu.VMEM((hb, dh, dh), F32)],
        compiler_params=_params(("arbitrary", "arbitrary")),
    )(qkvn, qkvn, qkvn, gb, bb, states, do)


def _adamw(w, g, m, v, name):
    r, c = w.shape
    br = _div_tile(r, max(8, (1 << 18) // max(c, 1)), 8)

    def fn(w, g, m, v):
        m = ADAM_B1 * m + (1.0 - ADAM_B1) * g
        v = ADAM_B2 * v + (1.0 - ADAM_B2) * jnp.square(g)
        m_hat = m / (1.0 - ADAM_B1 ** ADAM_STEP)
        v_hat = v / (1.0 - ADAM_B2 ** ADAM_STEP)
        delta = -ADAM_LR * (m_hat / (jnp.sqrt(v_hat) + ADAM_EPS) + ADAM_WD * w)
        return delta, m, v

    spec = pl.BlockSpec((br, c), lambda i: (i, 0))
    return _ew(fn, name, (r // br,), [w, g, m, v], [spec] * 4, [((r, c), F32, spec)] * 3)


def _coords():
    return lax.axis_index("x"), lax.axis_index("y"), lax.axis_index("c")


def _all_gather_small(v, after=()):
    r, w = v.shape
    n_after = len(after)

    def body(v_ref, *rest):
        out_ref, send_sems, recv_sems, local_sem = rest[n_after:]
        x, y, c = _coords()
        me = 4 * x + 2 * y + c
        mine = pltpu.make_async_copy(v_ref, out_ref.at[me], local_sem)
        mine.start()
        peers = []
        for k in range(1, 8):
            px = 1 - x if k & 4 else x
            py = 1 - y if k & 2 else y
            pc = 1 - c if k & 1 else c
            peers.append((px, py, pc))
        sends = []
        for k, peer in enumerate(peers):
            cp = pltpu.make_async_remote_copy(src_ref=v_ref, dst_ref=out_ref.at[me], send_sem=send_sems.at[k],
                                              recv_sem=recv_sems.at[k], device_id=peer, device_id_type=MESH)
            cp.start()
            sends.append(cp)
        for k, (px, py, pc) in enumerate(peers):
            pltpu.make_async_remote_copy(src_ref=v_ref, dst_ref=out_ref.at[4 * px + 2 * py + pc],
                                         send_sem=send_sems.at[k], recv_sem=recv_sems.at[k],
                                         device_id=(px, py, pc), device_id_type=MESH).wait_recv()
        for cp in sends:
            cp.wait_send()
        mine.wait()

    return pl.pallas_call(
        body, name="ag_small", out_shape=jax.ShapeDtypeStruct((8, r, w), v.dtype),
        in_specs=[pl.BlockSpec(memory_space=pltpu.VMEM)] + [pl.BlockSpec(memory_space=pl.ANY)] * n_after,
        out_specs=pl.BlockSpec(memory_space=pltpu.VMEM),
        scratch_shapes=[pltpu.SemaphoreType.DMA((7,)), pltpu.SemaphoreType.DMA((7,)), pltpu.SemaphoreType.DMA],
        compiler_params=pltpu.CompilerParams(vmem_limit_bytes=VMEM_LIMIT),
    )(v, *after)


def _hbm_call(body, name, arrays, out_shapes, n_sems, aliases=None):
    hbm = pl.BlockSpec(memory_space=pltpu.HBM)
    return pl.pallas_call(
        body, name=name, out_shape=list(out_shapes), in_specs=[hbm] * len(arrays), out_specs=[hbm] * len(out_shapes),
        input_output_aliases=aliases or {},
        scratch_shapes=[pltpu.SemaphoreType.DMA((n_sems,)), pltpu.SemaphoreType.DMA((n_sems,))],
    )(*arrays)


def _half_rows(ref_shape, c):
    h = ref_shape[1] // 2
    return pl.ds(pl.multiple_of(c * h, 16), h), pl.ds(pl.multiple_of((1 - c) * h, 16), h)


def _pair_copies(p_refs, land_refs, send_sems, recv_sems):
    x, y, c = _coords()
    cps = []
    for w, (p, land) in enumerate(zip(p_refs, land_refs)):
        _, other = _half_rows(p.shape, c)
        cps.append(pltpu.make_async_remote_copy(src_ref=p.at[:, other], dst_ref=land, send_sem=send_sems.at[w],
                                                recv_sem=recv_sems.at[w], device_id=(x, y, 1 - c),
                                                device_id_type=MESH))
    return cps


def _pair_gather(gs, name):
    n = len(gs)

    def body(*refs):
        ins, outs, send_sems, recv_sems = refs[:n], refs[n:2 * n], refs[2 * n], refs[2 * n + 1]
        x, y, c = _coords()
        cps = []
        for w in range(n):
            cp = pltpu.make_async_remote_copy(src_ref=ins[w].at[c], dst_ref=outs[w].at[c], send_sem=send_sems.at[w],
                                              recv_sem=recv_sems.at[w], device_id=(x, y, 1 - c), device_id_type=MESH)
            cp.start()
            cps.append(cp)
        for w, cp in enumerate(cps):
            pltpu.make_async_remote_copy(src_ref=ins[w].at[c], dst_ref=outs[w].at[1 - c], send_sem=send_sems.at[w],
                                         recv_sem=recv_sems.at[w], device_id=(x, y, 1 - c),
                                         device_id_type=MESH).wait_recv()
            cp.wait_send()

    shapes = [jax.ShapeDtypeStruct(g.shape, g.dtype) for g in gs]
    return _hbm_call(body, name, gs, shapes, n, aliases={i: i for i in range(n)})


_HBM = pl.BlockSpec(memory_space=pltpu.HBM)
_SEM = pl.BlockSpec(memory_space=pltpu.SEMAPHORE)
_EFFECT = pltpu.SideEffectType.DATAFLOW_SIDE_EFFECTING


def _split_start(name, srcs, lands, after, n_copies, make_copies):
    n, m = len(srcs), len(lands)
    arrays = [pltpu.with_memory_space_constraint(a, pltpu.HBM) for a in list(srcs) + list(lands)]

    def body(*refs):
        src_refs, land_refs = refs[:n], refs[n:n + m]
        send_sems, recv_sems = refs[n + m + 1], refs[n + m + 2]
        for cp in make_copies(src_refs, land_refs, send_sems, recv_sems):
            cp.start()
        refs[-1][...] = jnp.zeros_like(refs[-1])

    outs = pl.pallas_call(
        body, name=name,
        out_shape=(pltpu.SemaphoreType.DMA((n_copies,)), pltpu.SemaphoreType.DMA((n_copies,)),
                   *[pltpu.HBM(a.shape, a.dtype) for a in arrays], jax.ShapeDtypeStruct((8, LANES), F32)),
        in_specs=[_HBM] * (n + m) + [pl.BlockSpec(memory_space=pl.ANY)],
        out_specs=(_SEM, _SEM, *[_HBM] * (n + m), pl.BlockSpec(memory_space=pltpu.VMEM)),
        input_output_aliases={i: 2 + i for i in range(n + m)},
        compiler_params=pltpu.CompilerParams(has_side_effects=_EFFECT),
    )(*arrays, after)
    return (outs[0], outs[1], list(outs[2:2 + n]), list(outs[2 + n:2 + n + m])), outs[-1]


def _split_wait(name, state, after, make_copies):
    send_sems, recv_sems, srcs, lands = state
    n, m = len(srcs), len(lands)

    def body(*refs):
        src_refs, land_refs = refs[:n], refs[n:n + m]
        for cp in make_copies(src_refs, land_refs, refs[n + m], refs[n + m + 1]):
            cp.wait_send()
            cp.wait_recv()

    outs = pl.pallas_call(
        body, name=name, out_shape=tuple(pltpu.HBM(a.shape, a.dtype) for a in srcs + lands),
        in_specs=[_HBM] * (n + m) + [_SEM, _SEM, pl.BlockSpec(memory_space=pl.ANY)], out_specs=tuple([_HBM] * (n + m)),
        input_output_aliases={i: i for i in range(n + m)},
        compiler_params=pltpu.CompilerParams(has_side_effects=_EFFECT),
    )(*srcs, *lands, send_sems, recv_sems, after)
    return list(outs[:n]), list(outs[n:])


def _scatter_copies(q_refs, land_refs, send_sems, recv_sems):
    x, y, c = _coords()
    cps = []
    for w, (q, land) in enumerate(zip(q_refs, land_refs)):
        for k, (px, py) in enumerate([(1 - x, y), (x, 1 - y), (1 - x, 1 - y)]):
            cps.append(pltpu.make_async_remote_copy(src_ref=q.at[2 * px + py], dst_ref=land.at[k],
                                                    send_sem=send_sems.at[3 * w + k], recv_sem=recv_sems.at[3 * w + k],
                                                    device_id=(px, py, c), device_id_type=MESH))
    return cps


def _chip_scatter_start(qs, after, name):
    lands = [lax.empty((3,) + q.shape[1:], q.dtype) for q in qs]
    return _split_start(name, qs, lands, after, 3 * len(qs), _scatter_copies)


def _chip_scatter_wait(state, after, name):
    return _split_wait(name, state, after, _scatter_copies)


def _gather_copies(src_refs, buf_refs, send_sems, recv_sems):
    x, y, c = _coords()
    j = 2 * x + y
    cps = []
    for w, buf in enumerate(buf_refs):
        mine, _ = _half_rows(buf.shape, c)
        for k, (px, py) in enumerate([(1 - x, y), (x, 1 - y), (1 - x, 1 - y)]):
            cps.append(pltpu.make_async_remote_copy(src_ref=buf.at[j, mine], dst_ref=buf.at[j, mine],
                                                    send_sem=send_sems.at[3 * w + k], recv_sem=recv_sems.at[3 * w + k],
                                                    device_id=(px, py, c), device_id_type=MESH))
    return cps


def _gather_wait_copies(src_refs, buf_refs, send_sems, recv_sems):
    x, y, c = _coords()
    j = 2 * x + y
    cps = []
    for w, buf in enumerate(buf_refs):
        mine, _ = _half_rows(buf.shape, c)
        for k, (px, py) in enumerate([(1 - x, y), (x, 1 - y), (1 - x, 1 - y)]):
            cps.append(pltpu.make_async_remote_copy(src_ref=buf.at[j, mine], dst_ref=buf.at[2 * px + py, mine],
                                                    send_sem=send_sems.at[3 * w + k], recv_sem=recv_sems.at[3 * w + k],
                                                    device_id=(px, py, c), device_id_type=MESH))
    return cps


def _pair_forward(bufs, name):
    n = len(bufs)

    def body(*refs):
        ins, outs, send_sems, recv_sems = refs[:n], refs[n:2 * n], refs[2 * n], refs[2 * n + 1]
        x, y, c = _coords()
        chips = [(1 - x, y), (x, 1 - y), (1 - x, 1 - y)]
        cps = []
        for w in range(n):
            mine, _ = _half_rows(outs[w].shape, c)
            for k, (px, py) in enumerate(chips):
                cp = pltpu.make_async_remote_copy(src_ref=ins[w].at[2 * px + py, mine],
                                                  dst_ref=outs[w].at[2 * px + py, mine],
                                                  send_sem=send_sems.at[3 * w + k], recv_sem=recv_sems.at[3 * w + k],
                                                  device_id=(x, y, 1 - c), device_id_type=MESH)
                cp.start()
                cps.append(cp)
        for w in range(n):
            _, sib = _half_rows(outs[w].shape, c)
            for k, (px, py) in enumerate(chips):
                pltpu.make_async_remote_copy(src_ref=ins[w].at[2 * px + py, sib], dst_ref=outs[w].at[2 * px + py, sib],
                                             send_sem=send_sems.at[3 * w + k], recv_sem=recv_sems.at[3 * w + k],
                                             device_id=(x, y, 1 - c), device_id_type=MESH).wait_recv()
        for cp in cps:
            cp.wait_send()

    shapes = [jax.ShapeDtypeStruct(b.shape, b.dtype) for b in bufs]
    return _hbm_call(body, name, bufs, shapes, 3 * n, aliases={i: i for i in range(n)})


def _slot_rows(h, cs):
    return _div_tile(h, max(16, (1 << 19) // cs), 16)


def _cast_into_slot(w, slot, name, after=()):
    r, cs = w.shape
    br = _slot_rows(r, cs)
    return _ew_slot(lambda a: a, name, (r // br,), slot, [w], [pl.BlockSpec((br, cs), lambda i, s: (i, 0))],
                    (4, r, cs), BF16, pl.BlockSpec((None, br, cs), lambda i, s: (s[0], i, 0)), after=after)


def _pair_add(p, rb, core, name):
    n, r, cs = p.shape
    h = r // 2
    br = _slot_rows(h, cs)
    nb = h // br
    return _ew_slot(lambda a, b: a + b, name, (n, nb), core, [p, rb],
                    [pl.BlockSpec((None, br, cs), lambda s, i, c: (s, c[0] * nb + i, 0)),
                     pl.BlockSpec((None, br, cs), lambda s, i, c: (s, i, 0))],
                    (n, h, cs), BF16, pl.BlockSpec((None, br, cs), lambda s, i, c: (s, i, 0)))


def _chip_sum(q, r3, slots, name):
    _, h, cs = q.shape
    br = _slot_rows(h, cs)

    def fn(a, b):
        acc = a.astype(F32)
        for k in range(3):
            acc = acc + b[k].astype(F32)
        return acc

    return _ew_slot(fn, name, (h // br,), slots, [q, r3],
                    [pl.BlockSpec((None, br, cs), lambda i, s: (s[0], i, 0)),
                     pl.BlockSpec((3, br, cs), lambda i, s: (0, i, 0))],
                    (2, h, cs), F32, pl.BlockSpec((None, br, cs), lambda i, s: (s[1], i, 0)))


def _sum_slots(r, name):
    n, h, w = r.shape
    br = _div_tile(h, 2048, 16)

    def fn(blk):
        acc = blk[0].astype(F32)
        for s in range(1, n):
            acc = acc + blk[s].astype(F32)
        return (acc,)

    return _ew(fn, name, (h // br,), [r], [pl.BlockSpec((n, br, w), lambda i: (0, i, 0))],
               [((h, w), F32, pl.BlockSpec((br, w), lambda i: (i, 0)))])[0]


def _pack_small(arrs):
    flat = jnp.concatenate([a.reshape(-1).astype(F32) for a in arrs])
    n = flat.shape[0]
    rows = -(-n // LANES)
    rows = -(-rows // 8) * 8
    return jnp.pad(flat, (0, rows * LANES - n)).reshape(rows, LANES)


def _unpack_small(p, shapes):
    lead = p.shape[:-2]
    flat = p.reshape(lead + (-1,))
    out, off = [], 0
    for sh in shapes:
        n = 1
        for d in sh:
            n *= d
        out.append(flat[..., off:off + n].reshape(lead + tuple(sh)))
        off += n
    return out


def kernel(x, c, w_ada, b_ada, norm1_g, w_in, dn_conv_w, dn_a_log, dn_dt_bias, dn_norm_g, dn_w_o, cf_conv_w, cf_ln_g, cf_ln_b, cf_w_o, w_out, norm2_g, ffn_w_up, ffn_conv_w, ffn_w_down, final_norm_g, loss_target, m_w_ada, m_b_ada, m_norm1_g, m_w_in, m_dn_conv_w, m_dn_a_log, m_dn_dt_bias, m_dn_norm_g, m_dn_w_o, m_cf_conv_w, m_cf_ln_g, m_cf_ln_b, m_cf_w_o, m_w_out, m_norm2_g, m_ffn_w_up, m_ffn_conv_w, m_ffn_w_down, m_final_norm_g, v_w_ada, v_b_ada, v_norm1_g, v_w_in, v_dn_conv_w, v_dn_a_log, v_dn_dt_bias, v_dn_norm_g, v_dn_w_o, v_cf_conv_w, v_cf_ln_g, v_cf_ln_b, v_cf_w_o, v_w_out, v_norm2_g, v_ffn_w_up, v_ffn_conv_w, v_ffn_w_down, v_final_norm_g):
    xi, yi, ci = _coords()
    chip = 2 * xi + yi
    me = 4 * xi + 2 * yi + ci
    core = jnp.reshape(ci, (1,)).astype(jnp.int32)

    S, D = x.shape[1], x.shape[2]
    NH = dn_a_log.shape[1]
    DH = dn_norm_g.shape[1]
    DNW = NH * DH
    CFW = cf_ln_g.shape[1]
    FFN = ffn_w_down.shape[1] * 4
    KDN, KCF, KFF = dn_conv_w.shape[1], cf_conv_w.shape[1], ffn_conv_w.shape[1]
    NIN = w_in.shape[2] * 4
    assert NIN == 4 * DNW + 2 * NH + 2 * CFW + 2 * D and DH == LANES and 2 * NH <= LANES
    x2, tgt = x[0], loss_target[0]

    slot_chip = jnp.reshape(chip, (1,)).astype(jnp.int32)
    big = [w_in[0], dn_w_o[0], cf_w_o[0], w_out[0], ffn_w_up[0], ffn_w_down[0]]
    names_big = ["w_in", "dn_w_o", "cf_w_o", "w_out", "ffn_w_up", "ffn_w_down"]
    buf_in = _cast_into_slot(big[0], slot_chip, "cast_w_in")

    sm_shapes = [(D,), (KDN, 3 * DNW // 4), (KCF, CFW // 4), (KFF, FFN // 4)]
    g1 = _all_gather_small(_pack_small([c[0], dn_conv_w[0], cf_conv_w[0], ffn_conv_w[0]]))
    c_all, dcw_s, ccw_s, fcw_s = _unpack_small(g1, sm_shapes)

    def chips_cols(t):
        t = t[0::2]
        return jnp.transpose(t, (1, 0, 2)).reshape(t.shape[1], -1)

    dn_cw, cf_cw, ff_cw = chips_cols(dcw_s), chips_cols(ccw_s), chips_cols(fcw_s)


    cb_cf = _div_tile(CFW, 256, LANES)
    o_b = 4 * DNW
    o_glu = o_b + 2 * NH
    o_ga = o_glu + 2 * CFW
    NA = NIN - 2 * NH
    a_z, a_ga, a_gb, a_glu, a_ba = 3 * DNW, 4 * DNW, 4 * DNW + D, 4 * DNW + 2 * D, NA

    segs = [(0, 0, o_b), (o_b, NA, 2 * NH), (o_ga, a_ga, 2 * D)]
    for t in range(CFW // cb_cf):
        segs += [(o_glu + t * cb_cf, a_glu + 2 * t * cb_cf, cb_cf),
                 (o_glu + CFW + t * cb_cf, a_glu + (2 * t + 1) * cb_cf, cb_cf)]
    CS = NIN // 4

    def shard_slices(lo, n):
        out = []
        while n > 0:
            j, off = lo // CS, lo % CS
            m = min(n, CS - off)
            out.append(w_in_g[j][:, off:off + m])
            lo, n = lo + m, n - m
        return out

    CA = w_ada.shape[2]
    b_sh = lax.dynamic_slice(b_ada, (0, chip * CA), (1, CA))
    tn_a = _div_tile(CA, 512, LANES)

    def mod_fn(cc, w, b):
        return (_mm(_silu(cc), w, NN) + b,)

    mod_sh = _ew(mod_fn, "ada_mod", (CA // tn_a,), [c_all, w_ada[0], b_sh],
                 [_full((8, D)), pl.BlockSpec((D, tn_a), lambda j: (0, j)), pl.BlockSpec((1, tn_a), lambda j: (0, j))],
                 [((8, CA), F32, pl.BlockSpec((8, tn_a), lambda j: (0, j)))])[0]
    g2 = _all_gather_small(_pack_small([mod_sh]))
    in_state, in_token = _split_start("ag_in_start", [], [buf_in], g2, 3, _gather_copies)
    bufs = [_cast_into_slot(w, slot_chip, "cast_" + nm, after=[in_token]) for w, nm in zip(big[1:], names_big[1:])]
    mid_state, mid_token = _split_start("ag_mid_start", [], bufs[:3], in_token, 9, _gather_copies)
    ag_state, ag_token = _split_start("ag_late_start", [], bufs[3:], mid_token, 6, _gather_copies)
    mod_all = _unpack_small(g2, [(8, CA)])[0][0::2]
    mod_all = jnp.transpose(mod_all, (1, 0, 2)).reshape(8, 4 * CA)
    mod_me = lax.dynamic_slice(mod_all, (me, 0), (1, 6 * D))
    sh1, sc1, gt1, sh2, sc2, gt2 = [mod_me[:, i * D:(i + 1) * D] for i in range(6)]

    bs = _div_tile(S, 128, 8)
    nb = S // bs
    vecD = _full((1, D))
    rowD = _row(bs, D)

    hn1 = _ew(lambda a, g, sc, sh: (_f_normmod(a, g, sc, sh),), "norm1_fwd", (nb,),
              [x2, norm1_g, sc1, sh1], [rowD, vecD, vecD, vecD], [((S, D), BF16, rowD)],
              after=[ag_token, m_w_in[0], v_w_in[0]])[0]
    _, in_landed = _split_wait("ag_in_wait", in_state, hn1, _gather_wait_copies)
    (w_in_g,) = _pair_forward(in_landed, "ag_in_pair")
    w_aug = jnp.concatenate([p for o, a, n in sorted(segs, key=lambda s: s[1]) for p in shard_slices(o, n)]
                            + [jnp.zeros((D, LANES - 2 * NH), BF16)], axis=1)
    proj = _matmul(hn1, w_aug, "nn", F32, "mm_in")

    def dn_post(j, cv):
        s = _silu(cv)
        nrm = s * lax.rsqrt(jnp.sum(s * s, axis=-1, keepdims=True) + EPS)
        fq = (j < NH).astype(F32)
        fk = (j < 2 * NH).astype(F32)
        scale = fq * (DH ** -0.5) + (1.0 - fq)
        return fk * (nrm * scale) + (1.0 - fk) * s

    def ident(a):
        return a

    def colS(w, off=0):
        return pl.BlockSpec((S, w), lambda j, off=off: (0, j + off))

    def wS(kw, w):
        return pl.BlockSpec((kw, w), lambda j: (0, j))

    qkv_spec = pl.BlockSpec((None, None, S, DH), lambda j: (j // NH, j % NH, 0, 0))
    dn_args = dict(kw=KDN, ncol=3 * NH, ins=[proj], in_specs=[colS(DH)], extras=[], extra_specs=[],
                   w=dn_cw, w_spec=wS(KDN, DH), pre=ident, post=dn_post)
    qkvn = _conv_fwd_call("dn_conv_fwd", out_shape=jax.ShapeDtypeStruct((3, NH, S, DH), F32), out_spec=qkv_spec,
                          **dn_args)

    alp = jnp.pad(dn_a_log, ((0, 0), (NH, LANES - 2 * NH)))
    dtb = jnp.pad(dn_dt_bias, ((0, 0), (NH, LANES - 2 * NH)))
    vecL = _full((1, LANES))
    ba_spec = _row(bs, LANES, a_ba // LANES)
    rowL = _row(bs, LANES)
    gate_fn = functools.partial(_f_dn_gate, NH)
    def gate_fwd(a, p, q):
        val = gate_fn(a, p, q)
        lane = lax.broadcasted_iota(jnp.int32, val.shape, 1)

        def spread(col):
            return jnp.sum(jnp.where(lane == col, val, 0.0), axis=-1, keepdims=True) + jnp.zeros(val.shape, F32)

        return tuple(spread(h) for h in range(NH)), tuple(spread(NH + h) for h in range(NH))

    hrow = pl.BlockSpec((NH, bs, DH), lambda i: (0, i, 0))
    bb_b, gb_b = _ew(gate_fwd, "dn_gate_fwd", (nb,), [proj, alp, dtb], [ba_spec, vecL, vecL],
                     [((NH, S, DH), F32, hrow), ((NH, S, DH), F32, hrow)])
    o_dn, states = _dn_fwd_call(qkvn, gb_b, bb_b)

    bsh = _div_tile(S, 512, 8)
    nbh = S // bsh
    o_spec = pl.BlockSpec((None, bsh, DH), lambda i, h: (h, i, 0))
    z_spec = pl.BlockSpec((bsh, DH), lambda i, h: (i, a_z // DH + h))
    oh_spec = pl.BlockSpec((bsh, DH), lambda i, h: (i, h))
    ng_spec = pl.BlockSpec((1, DH), lambda i, h: (0, 0))
    on = _ew(lambda o, z, g: (_f_dn_post(o, z, g),), "dn_post_fwd", (nbh, NH), [o_dn, proj, dn_norm_g],
             [o_spec, z_spec, ng_spec], [((S, DNW), BF16, oh_spec)])[0]
    _, landed = _split_wait("ag_mid_wait", mid_state, on, _gather_wait_copies)
    w_do_f, w_co_f, w_out_g = _pair_forward(landed, "ag_mid_pair")
    w_out_f = w_out_g.reshape(-1, w_out_g.shape[2])
    br_a = _matmul(on, w_do_f, "nn", F32, "mm_dn_o")

    def glu_pre(val, gl):
        return val * _sigmoid(gl)

    def glu_spec(t):
        return pl.BlockSpec((S, cb_cf), lambda j, t=t: (0, a_glu // cb_cf + 2 * j + t))

    cf_args = dict(kw=KCF, ncol=CFW // cb_cf, ins=[proj, proj], in_specs=[glu_spec(0), glu_spec(1)],
                   extras=[], extra_specs=[], w=cf_cw, w_spec=wS(KCF, cb_cf), pre=glu_pre, post=lambda j, cv: cv)
    uc = _conv_fwd_call("cf_conv_fwd", out_shape=jax.ShapeDtypeStruct((S, CFW), F32), out_spec=colS(cb_cf), **cf_args)
    rowC = _row(bs, CFW)
    vecC = _full((1, CFW))
    ub = _ew(lambda u, g, b: (_f_cf_ln(u, g, b),), "cf_ln_fwd", (nb,), [uc, cf_ln_g, cf_ln_b], [rowC, vecC, vecC],
             [((S, CFW), BF16, rowC)])[0]
    br_b = _matmul(ub, w_co_f, "nn", F32, "mm_cf_o")

    ga_spec, gb_spec = _row(bs, D, a_ga // D), _row(bs, D, a_gb // D)
    merged = _ew(lambda a, b, ga, gb: (_f_merge(a, b, ga, gb),), "merge_fwd", (nb,), [br_a, br_b, proj, proj],
                 [rowD, rowD, ga_spec, gb_spec], [((S, D), BF16, rowD)])[0]
    mix = _matmul(merged, w_out_f, "nn", F32, "mm_out")

    x1, hn2 = _ew(_f_res_normmod, "norm2_fwd", (nb,), [x2, mix, gt1, norm2_g, sc2, sh2],
                  [rowD, rowD, vecD, vecD, vecD, vecD], [((S, D), F32, rowD), ((S, D), BF16, rowD)])
    _, landed = _split_wait("ag_late_wait", ag_state, hn2, _gather_wait_copies)
    w_up_f, w_dn_g = _pair_forward(landed, "ag_late_pair")
    w_dn_f = w_dn_g.reshape(-1, w_dn_g.shape[2])
    up_all = _matmul(hn2, w_up_f, "nn", F32, "mm_up")

    cb_ff = _div_tile(FFN, 256, LANES)
    ff_args = dict(kw=KFF, ncol=FFN // cb_ff, ins=[up_all], in_specs=[colS(cb_ff)], extras=[up_all],
                   extra_specs=[colS(cb_ff, FFN // cb_ff)], w=ff_cw, w_spec=wS(KFF, cb_ff),
                   pre=ident, post=lambda j, cv, up: _silu(cv) * up)
    hff = _conv_fwd_call("ffn_conv_fwd", out_shape=jax.ShapeDtypeStruct((S, FFN), BF16), out_spec=colS(cb_ff), **ff_args)
    ffo = _matmul(hff, w_dn_f, "nn", F32, "mm_down")

    gf2 = final_norm_g.reshape(1, D)

    def loss_bwd(a, f, gt, gf, t):
        val, vjp = jax.vjp(_f_loss, a, f, gt, gf, t)
        da, df, dgt, dgf, _ = vjp(jnp.ones((), F32))
        return da, df, dgt, dgf, jnp.zeros((1, LANES), F32) + val

    dx1_l, dffo, dgt2, dgf, loss_v = _ew(
        loss_bwd, "loss_bwd", (nb,), [x1, ffo, gt2, gf2, tgt], [rowD, rowD, vecD, vecD, rowD],
        [((S, D), F32, rowD), ((S, D), BF16, rowD), ((1, D), F32, vecD), ((1, D), F32, vecD),
         ((1, LANES), F32, vecL)], acc=(2, 3, 4))

    dhff = _matmul(dffo, w_dn_f, "nt", F32, "mm_down_dx")
    g_w_dn = _matmul(hff, dffo, "tn", F32, "mm_down_dw")

    slots = jnp.stack([chip, ci]).astype(jnp.int32)
    rs_groups = []

    def rs_pair_begin(parts, tag):
        lands = [lax.empty((4, p.shape[1] // 2, p.shape[2]), F32) for p in parts]
        return _split_start("rs_pair_start_" + tag, parts, lands, core, len(parts), _pair_copies)

    def rs_chips_begin(pair_state, nms, tag, after):
        parts, rbs = _split_wait("rs_pair_wait_" + tag, pair_state, after, _pair_copies)
        q16 = [_pair_add(p, rb, core, "rs_pair_add_" + nm) for p, rb, nm in zip(parts, rbs, nms)]
        state, token = _chip_scatter_start(q16, core, "rs_chips_start_" + tag)
        rs_groups.append((state, nms, tag))
        return token

    pair_a, ptok_a = rs_pair_begin([g_w_dn.reshape(4, FFN // 4, D)], "a")

    d_upall, g_ffcw = _conv_bwd_call(
        "ffn_conv_bwd", dout=dhff, dout_spec=colS(cb_ff),
        dio_shapes=[jax.ShapeDtypeStruct((2, S, FFN), BF16)],
        dio_specs=[pl.BlockSpec((2, S, cb_ff), lambda j: (0, 0, j))], dio_pack=lambda dg, du: [(dg, du)],
        dw_shape=jax.ShapeDtypeStruct((KFF, FFN), F32), dw_spec=wS(KFF, cb_ff), after=[ptok_a], **ff_args)
    tok_a = rs_chips_begin(pair_a, ["ffn_w_down"], "a", d_upall)
    dhn2 = _matmul(d_upall, w_up_f, "nt", F32, "mm_up_dx", after=[tok_a])
    g_w_up = _matmul(hn2, d_upall, "tn", F32, "mm_up_dw", out_groups=4)
    pair_b, ptok_b = rs_pair_begin([g_w_up], "b")

    def res2_bwd(a, mx, gt, g, sc, sh, dx1, dhn):
        _, vjp = jax.vjp(_f_res_normmod, a, mx, gt, g, sc, sh)
        return vjp((dx1, dhn))

    dx_r, dmix, dgt1, dg2, dsc2, dsh2 = _ew(
        res2_bwd, "norm2_bwd", (nb,), [x2, mix, gt1, norm2_g, sc2, sh2, dx1_l, dhn2],
        [rowD, rowD, vecD, vecD, vecD, vecD, rowD, rowD],
        [((S, D), F32, rowD), ((S, D), BF16, rowD)] + [((1, D), F32, vecD)] * 4, acc=(2, 3, 4, 5), after=[ptok_b])

    dmerged = _matmul(dmix, w_out_f, "nt", F32, "mm_out_dx")
    g_w_out = _matmul(merged, dmix, "tn", F32, "mm_out_dw")

    def merge_bwd(a, b, ga, gb, dm):
        _, vjp = jax.vjp(_f_merge, a, b, ga, gb)
        da, db, dga, dgb = vjp(dm)
        return da, db, jnp.concatenate([dga, dgb], axis=1)

    d_bra, d_brb, dproj = _ew(merge_bwd, "merge_bwd", (nb,), [br_a, br_b, proj, proj, dmerged],
                              [rowD, rowD, ga_spec, gb_spec, rowD],
                              [((S, D), BF16, rowD), ((S, D), BF16, rowD),
                               ((S, NA + LANES), BF16, _row(bs, 2 * D, a_ga // (2 * D)))])

    tok_b = rs_chips_begin(pair_b, ["ffn_w_up"], "b", d_brb)
    d_on = _matmul(d_bra, w_do_f, "nt", F32, "mm_dn_o_dx", after=[tok_b])
    g_w_do = _matmul(on, d_bra, "tn", F32, "mm_dn_o_dw", out_groups=4)
    d_ub = _matmul(d_brb, w_co_f, "nt", F32, "mm_cf_o_dx")
    g_w_co = _matmul(ub, d_brb, "tn", F32, "mm_cf_o_dw", out_groups=4)
    pair_c, ptok_c = rs_pair_begin([g_w_do, g_w_co, g_w_out.reshape(4, D // 4, D)], "c")

    def cf_ln_bwd(u, g, b, du):
        _, vjp = jax.vjp(_f_cf_ln, u, g, b)
        return vjp(du)

    d_uc, g_cflg, g_cflb = _ew(cf_ln_bwd, "cf_ln_bwd", (nb,), [uc, cf_ln_g, cf_ln_b, d_ub], [rowC, vecC, vecC, rowC],
                               [((S, CFW), F32, rowC), ((1, CFW), F32, vecC), ((1, CFW), F32, vecC)], acc=(1, 2),
                               after=[ptok_c])
    dproj_sds = jax.ShapeDtypeStruct((S, NA + LANES), BF16)
    dproj, g_cfcw = _conv_bwd_call(
        "cf_conv_bwd", dout=d_uc, dout_spec=colS(cb_cf), dio_shapes=[dproj_sds],
        dio_specs=[colS(2 * cb_cf, a_glu // (2 * cb_cf))], dio_pack=lambda dv, dg: [jnp.concatenate([dv, dg], axis=1)],
        dw_shape=jax.ShapeDtypeStruct((KCF, CFW), F32), dw_spec=wS(KCF, cb_cf), alias=dproj, **cf_args)

    def dn_post_bwd(o, z, g, d):
        _, vjp = jax.vjp(_f_dn_post, o, z, g)
        return vjp(d)

    tok_c = rs_chips_begin(pair_c, ["dn_w_o", "cf_w_o", "w_out"], "c", g_cfcw)
    d_o, dproj, g_dnng = _ew(dn_post_bwd, "dn_post_bwd", (nbh, NH), [o_dn, proj, dn_norm_g, d_on],
                             [o_spec, z_spec, ng_spec, oh_spec],
                             [((NH, S, DH), F32, o_spec), ((S, NA + LANES), BF16, z_spec), ((1, DH), F32, ng_spec)],
                             acc=(2,), alias=(dproj, 1), after=[tok_c])

    dqkvn, dgb_b, dbb_b = _dn_bwd_call(qkvn, gb_b, bb_b, states, d_o)
    dproj, g_dncw = _conv_bwd_call(
        "dn_conv_bwd", dout=dqkvn, dout_spec=qkv_spec, dio_shapes=[dproj_sds], dio_specs=[colS(DH)],
        dio_pack=lambda d: [d], dw_shape=jax.ShapeDtypeStruct((KDN, 3 * DNW), F32), dw_spec=wS(KDN, DH),
        alias=dproj, **dn_args)

    def gate_bwd(a, p, q, db, dg):
        lane = lax.broadcasted_iota(jnp.int32, a.shape, 1)
        d = jnp.zeros(a.shape, F32)
        for h in range(NH):
            d = d + jnp.where(lane == h, db[h], 0.0) + jnp.where(lane == NH + h, dg[h], 0.0)
        _, vjp = jax.vjp(gate_fn, a, p, q)
        return vjp(d)

    dproj, g_alp, g_dtb = _ew(gate_bwd, "dn_gate_bwd", (nb,), [proj, alp, dtb, dbb_b, dgb_b],
                              [ba_spec, vecL, vecL, hrow, hrow],
                              [((S, NA + LANES), BF16, ba_spec), ((1, LANES), F32, vecL), ((1, LANES), F32, vecL)],
                              acc=(1, 2), alias=(dproj, 0))

    g_w_aug = _matmul(hn1, dproj, "tn", F32, "mm_in_dw")
    def aug_slices(lo, n):
        out = []
        for o, a, m in sorted(segs):
            s, e = max(lo, o), min(lo + n, o + m)
            if s < e:
                out.append(g_w_aug[:, a + s - o:a + e - o])
        return out

    g_w_in = jnp.stack([jnp.concatenate(aug_slices(j * CS, CS), axis=1) for j in range(4)])
    pair_d, ptok_d = rs_pair_begin([g_w_in], "d")
    dhn1 = _matmul(dproj, w_aug, "nt", F32, "mm_in_dx", after=[ptok_d])

    def norm1_bwd(a, g, sc, sh, dhn, dxr):
        _, vjp = jax.vjp(_f_normmod, a, g, sc, sh)
        da, dg, dsc, dsh = vjp(dhn)
        return da + dxr, dg, dsc, dsh

    grad_x, dg1, dsc1, dsh1 = _ew(norm1_bwd, "norm1_bwd", (nb,), [x2, norm1_g, sc1, sh1, dhn1, dx_r],
                                  [rowD, vecD, vecD, vecD, rowD, rowD],
                                  [((S, D), F32, rowD)] + [((1, D), F32, vecD)] * 3, acc=(1, 2, 3))

    dmod = jnp.concatenate([dsh1, dsc1, dgt1, dsh2, dsc2, dgt2], axis=1)
    sm2 = [dmod, dg1, dg2, dgf, g_alp, g_dtb, g_dnng, g_cflg, g_cflb, g_dncw, g_cfcw, g_ffcw]
    sm2_shapes = [tuple(a.shape) for a in sm2]
    g3 = _all_gather_small(_pack_small(sm2))
    tok_d = rs_chips_begin(pair_d, ["w_in"], "d", g3)

    names = ["w_ada", "b_ada", "norm1_g", "w_in", "dn_conv_w", "dn_a_log", "dn_dt_bias", "dn_norm_g", "dn_w_o",
             "cf_conv_w", "cf_ln_g", "cf_ln_b", "cf_w_o", "w_out", "norm2_g", "ffn_w_up", "ffn_conv_w", "ffn_w_down",
             "final_norm_g"]
    wts = dict(zip(names, [w_ada, b_ada, norm1_g, w_in, dn_conv_w, dn_a_log, dn_dt_bias, dn_norm_g, dn_w_o, cf_conv_w,
                           cf_ln_g, cf_ln_b, cf_w_o, w_out, norm2_g, ffn_w_up, ffn_conv_w, ffn_w_down, final_norm_g]))
    ms = dict(zip(names, [m_w_ada, m_b_ada, m_norm1_g, m_w_in, m_dn_conv_w, m_dn_a_log, m_dn_dt_bias, m_dn_norm_g,
                          m_dn_w_o, m_cf_conv_w, m_cf_ln_g, m_cf_ln_b, m_cf_w_o, m_w_out, m_norm2_g, m_ffn_w_up,
                          m_ffn_conv_w, m_ffn_w_down, m_final_norm_g]))
    vs = dict(zip(names, [v_w_ada, v_b_ada, v_norm1_g, v_w_in, v_dn_conv_w, v_dn_a_log, v_dn_dt_bias, v_dn_norm_g,
                          v_dn_w_o, v_cf_conv_w, v_cf_ln_g, v_cf_ln_b, v_cf_w_o, v_w_out, v_norm2_g, v_ffn_w_up,
                          v_ffn_conv_w, v_ffn_w_down, v_final_norm_g]))
    grads, delta, new_m, new_v = {}, {}, {}, {}

    def adam_large(n, g):
        grads[n] = g.reshape(wts[n].shape)
        d_, m_, v_ = _adamw(wts[n][0], grads[n][0], ms[n][0], vs[n][0], "adamw_" + n)
        delta[n], new_m[n], new_v[n] = d_[None], m_[None], v_[None]

    def rs_finish(groups, after, tag):
        hs, nms_all = [], []
        for state, nms, t in groups:
            q16, r3s = _chip_scatter_wait(state, after, "rs_chips_wait_" + t)
            for q, r3, nm in zip(q16, r3s, nms):
                hs.append(_chip_sum(q, r3, slots, "rs_chip_sum_" + nm))
                nms_all.append(nm)
        for nm, g in zip(nms_all, _pair_gather(hs, "rs_pair_gather_" + tag)):
            adam_large(nm, g)

    rs_finish(rs_groups[:3], tok_d, "abc")

    ssum = _sum_slots(g3, "small_sum")
    dmod_all = _unpack_small(g3, sm2_shapes[:1])[0].reshape(8, 6 * D)
    (g_b_ada, gs_n1, gs_n2, gs_fn, gs_alp, gs_dtb, gs_dnng, gs_cflg, gs_cflb, gs_dncw, gs_cfcw,
     gs_ffcw) = _unpack_small(ssum, sm2_shapes)
    gs_alog, gs_dtbias = gs_alp[:, NH:2 * NH], gs_dtb[:, NH:2 * NH]
    gs_dncw = lax.dynamic_slice(gs_dncw, (0, chip * (3 * DNW // 4)), (KDN, 3 * DNW // 4))
    gs_cfcw = lax.dynamic_slice(gs_cfcw, (0, chip * (CFW // 4)), (KCF, CFW // 4))
    gs_ffcw = lax.dynamic_slice(gs_ffcw, (0, chip * (FFN // 4)), (KFF, FFN // 4))

    dmod_sh = lax.dynamic_slice(dmod_all, (0, chip * CA), (8, CA))

    def wada_fn(cc, dm):
        return (_mm(_silu(cc), dm, TN),)

    g_w_ada = _ew(wada_fn, "ada_dw", (CA // tn_a,), [c_all, dmod_sh],
                  [_full((8, D)), pl.BlockSpec((8, tn_a), lambda j: (0, j))],
                  [((D, CA), F32, pl.BlockSpec((D, tn_a), lambda j: (0, j)))])[0]

    loss = lax.psum(loss_v[0, 0], ("x", "y", "c"))

    adam_large("w_ada", g_w_ada)
    small_grads = {"b_ada": g_b_ada, "norm1_g": gs_n1, "dn_conv_w": gs_dncw, "dn_a_log": gs_alog,
                   "dn_dt_bias": gs_dtbias, "dn_norm_g": gs_dnng, "cf_conv_w": gs_cfcw, "cf_ln_g": gs_cflg,
                   "cf_ln_b": gs_cflb, "norm2_g": gs_n2, "ffn_conv_w": gs_ffcw, "final_norm_g": gs_fn}
    small = [n for n in names if n in small_grads]
    for n in small:
        grads[n] = small_grads[n].reshape(wts[n].shape)
    sm_sh = [tuple(wts[n].shape) for n in small]
    d_, m_, v_ = _adamw(_pack_small([wts[n] for n in small]), _pack_small([grads[n] for n in small]),
                        _pack_small([ms[n] for n in small]), _pack_small([vs[n] for n in small]), "adamw_small")
    for n, a, b_, c_ in zip(small, _unpack_small(d_, sm_sh), _unpack_small(m_, sm_sh), _unpack_small(v_, sm_sh)):
        delta[n], new_m[n], new_v[n] = a, b_, c_
    behind = sum(delta[n][0, :8, :8] for n in ["w_ada", "ffn_w_up", "ffn_w_down", "w_out", "dn_w_o", "cf_w_o"])
    rs_finish(rs_groups[3:], behind, "d")

    return (loss, grad_x[None], *[grads[n] for n in names], *[delta[n] for n in names],
            *[new_m[n] for n in names], *[new_v[n] for n in names])
```

```python
import functools

import jax
import jax.numpy as jnp
from jax import lax
from jax.experimental import pallas as pl
from jax.experimental.pallas import tpu as pltpu

F32 = jnp.float32
BF16 = jnp.bfloat16
EPS = 1e-6
LANES = 128
VMEM_LIMIT = 48 * 1024 * 1024
DN_CHUNK = 128
ADAM_LR, ADAM_B1, ADAM_B2, ADAM_EPS, ADAM_WD, ADAM_STEP = 0.001, 0.9, 0.999, 1e-08, 0.01, 10
MESH = pl.DeviceIdType.MESH

NN = (((1,), (0,)), ((), ()))
NT = (((1,), (1,)), ((), ()))
TN = (((0,), (0,)), ((), ()))
_DIMS = {"nn": NN, "nt": NT, "tn": TN}


def _mm(a, b, dims):
    return lax.dot_general(a.astype(BF16), b.astype(BF16), dims, preferred_element_type=F32)


def _mmx(a, b, dims):
    return lax.dot_general(a, b, dims, precision=lax.Precision.HIGH, preferred_element_type=F32)


def _div_tile(n, target, mult):
    best = None
    t = mult
    while t <= min(n, target):
        if n % t == 0:
            best = t
        t += mult
    return best if best is not None else n


def _params(sem=None):
    kw = dict(vmem_limit_bytes=VMEM_LIMIT)
    if sem is not None:
        kw["dimension_semantics"] = sem
    return pltpu.CompilerParams(**kw)


def _sigmoid(x):
    return jax.nn.sigmoid(x)


def _silu(x):
    return x * jax.nn.sigmoid(x)


def _softplus(x):
    return jnp.maximum(x, 0.0) + jnp.log(1.0 + jnp.exp(-jnp.abs(x)))


def _view(arr):
    if arr.ndim == 3:
        return arr.shape[1], arr.shape[0] * arr.shape[2], arr.shape[0]
    return arr.shape[0], arr.shape[1], 1


def _tile_spec(groups, cols, tr, tc, rsel, csel):
    if groups > 1:
        per = cols // groups // tc
        return pl.BlockSpec((None, tr, tc), lambda i, j, k: (csel(i, j, k) // per, rsel(i, j, k), csel(i, j, k) % per))
    return pl.BlockSpec((tr, tc), lambda i, j, k: (rsel(i, j, k), csel(i, j, k)))


def _matmul(a, b, mode, out_dtype, name, out_groups=1, after=()):
    ar, ac, ag = _view(a)
    br, bc, bg = _view(b)
    if mode == "nn":
        M, K, N = ar, ac, bc
        kdiv, mdiv, ndiv = ac // ag, M, min(bc // bg, N // out_groups)
    elif mode == "nt":
        M, K, N = ar, ac, br
        kdiv, mdiv, ndiv = min(ac // ag, bc // bg), M, N // out_groups
    else:
        K, M, N = ar, ac, bc
        kdiv, mdiv, ndiv = K, ac // ag, min(bc // bg, N // out_groups)
    tm = _div_tile(mdiv, 1024, LANES)
    tn = _div_tile(ndiv, 2048 if mode == "nt" else 1536, LANES)
    tk = _div_tile(kdiv, 2048, LANES)
    nk = K // tk
    dims = _DIMS[mode]
    si, sj, sk = (lambda i, j, k: i), (lambda i, j, k: j), (lambda i, j, k: k)
    a_spec = {"nn": _tile_spec(ag, ac, tm, tk, si, sk), "nt": _tile_spec(ag, ac, tm, tk, si, sk),
              "tn": _tile_spec(ag, ac, tk, tm, sk, si)}[mode]
    b_spec = {"nn": _tile_spec(bg, bc, tk, tn, sk, sj), "nt": _tile_spec(bg, bc, tn, tk, sj, sk),
              "tn": _tile_spec(bg, bc, tk, tn, sk, sj)}[mode]
    out_shape = (M, N) if out_groups == 1 else (out_groups, M, N // out_groups)

    n_after = len(after)

    def body(*refs):
        a_ref, b_ref, o_ref = refs[0], refs[1], refs[2 + n_after]
        if nk == 1:
            o_ref[...] = lax.dot_general(a_ref[...], b_ref[...], dims, preferred_element_type=F32).astype(o_ref.dtype)
            return
        acc_ref = refs[3 + n_after]
        k = pl.program_id(2)

        @pl.when(k == 0)
        def _():
            acc_ref[...] = jnp.zeros_like(acc_ref)

        acc_ref[...] += lax.dot_general(a_ref[...], b_ref[...], dims, preferred_element_type=F32)

        @pl.when(k == nk - 1)
        def _():
            o_ref[...] = acc_ref[...].astype(o_ref.dtype)

    return pl.pallas_call(
        body, name=name, grid=(M // tm, N // tn, nk),
        in_specs=[a_spec, b_spec] + [pl.BlockSpec(memory_space=pl.ANY)] * n_after,
        out_specs=_tile_spec(out_groups, N, tm, tn, si, sj),
        out_shape=jax.ShapeDtypeStruct(out_shape, out_dtype),
        scratch_shapes=[pltpu.VMEM((tm, tn), F32)] if nk > 1 else [],
        compiler_params=_params(("parallel", "parallel", "arbitrary")),
    )(a, b, *after)


def _ew(fn, name, grid, ins, in_specs, outs, acc=(), alias=None, after=()):
    n_in = len(ins)
    n_ax = len(grid)
    extra, aliases = list(after), {}
    if alias is not None:
        extra, aliases = extra + [alias[0]], {n_in + len(after): alias[1]}
    extra_specs = [pl.BlockSpec(memory_space=pl.ANY)] * len(extra)

    def body(*refs):
        in_refs, out_refs = refs[:n_in], refs[n_in + len(extra):]
        ids = [pl.program_id(a) for a in range(n_ax)]
        res = fn(*[r[...] for r in in_refs])
        first = ids[0] == 0
        for t in ids[1:]:
            first = jnp.logical_and(first, t == 0)
        for idx, (r, val) in enumerate(zip(out_refs, res)):
            if idx in acc:
                @pl.when(first)
                def _(r=r, val=val):
                    r[...] = val.astype(r.dtype)

                @pl.when(jnp.logical_not(first))
                def _(r=r, val=val):
                    r[...] += val.astype(r.dtype)
            elif isinstance(val, tuple):
                for t, part in enumerate(val):
                    r[t] = part.astype(r.dtype)
            else:
                r[...] = val.astype(r.dtype)

    return pl.pallas_call(
        body, name=name, grid=grid, in_specs=list(in_specs) + extra_specs,
        out_specs=[o[2] for o in outs],
        out_shape=[jax.ShapeDtypeStruct(o[0], o[1]) for o in outs],
        input_output_aliases=aliases,
        compiler_params=_params(("arbitrary",) * n_ax),
    )(*ins, *extra)


def _ew_slot(fn, name, grid, slots, ins, in_specs, out_shape, out_dtype, out_spec, after=()):
    n_in = len(ins)

    def body(s_ref, *refs):
        refs[-1][...] = fn(*[r[...] for r in refs[:n_in]]).astype(refs[-1].dtype)

    return pl.pallas_call(
        body, name=name,
        grid_spec=pltpu.PrefetchScalarGridSpec(
            num_scalar_prefetch=1, grid=grid,
            in_specs=list(in_specs) + [pl.BlockSpec(memory_space=pl.ANY)] * len(after), out_specs=out_spec),
        out_shape=jax.ShapeDtypeStruct(out_shape, out_dtype),
        compiler_params=_params(("arbitrary",) * len(grid)),
    )(slots, *ins, *after)


def _row(bs, w, col=0):
    return pl.BlockSpec((bs, w), lambda i, col=col: (i, col))


def _full(shape):
    nd = len(shape)
    return pl.BlockSpec(tuple(shape), lambda *_: (0,) * nd)


def _rms(x, g):
    return x * lax.rsqrt(jnp.mean(x * x, axis=-1, keepdims=True) + EPS) * g


def _f_normmod(x, g, sc, sh):
    return _rms(x, g) * (1.0 + sc) + sh


def _f_res_normmod(x, mix, gt, g, sc, sh):
    x1 = x + gt * mix
    return x1, _f_normmod(x1, g, sc, sh)


def _f_loss(x1, f, gt, gf, tgt):
    y = _rms(x1 + gt * f, gf)
    return 0.5 * jnp.sum(jnp.mean(jnp.square(y - tgt), axis=-1))


def _f_dn_gate(nh, ba, alp, dtb):
    lane = lax.broadcasted_iota(jnp.int32, ba.shape, 1)
    m = (lane < nh).astype(F32)
    beta = _sigmoid(ba)
    g = -jnp.exp(alp) * _softplus(ba + dtb)
    return m * beta + (1.0 - m) * g


def _f_dn_post(o, z, g):
    return o * lax.rsqrt(jnp.mean(o * o, axis=-1, keepdims=True) + EPS) * g * _silu(z)


def _f_cf_ln(u, g, b):
    mu = jnp.mean(u, axis=-1, keepdims=True)
    xc = u - mu
    y = xc * lax.rsqrt(jnp.mean(xc * xc, axis=-1, keepdims=True) + EPS)
    return _silu(y * g + b)


def _f_merge(a, b, ga, gb):
    return _sigmoid(ga) * a + _sigmoid(gb) * b


def _shift_down(u, d, rows):
    if d == 0:
        return u
    return jnp.where(rows >= d, pltpu.roll(u, d, 0), 0.0)


def _shift_up(u, d, rows):
    if d == 0:
        return u
    s = u.shape[0]
    return jnp.where(rows < s - d, pltpu.roll(u, s - d, 0), 0.0)


def _conv(u, w_ref, kw, rows):
    acc = None
    for k in range(kw):
        t = w_ref[k:k + 1, :] * _shift_down(u, kw - 1 - k, rows)
        acc = t if acc is None else acc + t
    return acc


def _conv_t(dc, w_ref, kw, rows):
    acc = None
    for k in range(kw):
        t = w_ref[k:k + 1, :] * _shift_up(dc, kw - 1 - k, rows)
        acc = t if acc is None else acc + t
    return acc


def _conv_fwd_call(name, kw, ncol, ins, in_specs, extras, extra_specs, w, w_spec, pre, post, out_shape, out_spec):
    n_in, n_ex = len(ins), len(extras)

    def body(*refs):
        in_refs, ex_refs = refs[:n_in], refs[n_in:n_in + n_ex]
        w_ref, out_ref = refs[n_in + n_ex], refs[n_in + n_ex + 1]
        j = pl.program_id(0)
        u = pre(*[r[...] for r in in_refs])
        rows = lax.broadcasted_iota(jnp.int32, u.shape, 0)
        cv = _conv(u, w_ref, kw, rows)
        out_ref[...] = post(j, cv, *[r[...] for r in ex_refs]).astype(out_ref.dtype)

    return pl.pallas_call(
        body, name=name, grid=(ncol,), in_specs=list(in_specs) + list(extra_specs) + [w_spec],
        out_specs=out_spec, out_shape=out_shape, compiler_params=_params(("arbitrary",)),
    )(*ins, *extras, w)


def _conv_bwd_call(name, kw, ncol, ins, in_specs, extras, extra_specs, w, w_spec, pre, post, dout, dout_spec,
                   dio_shapes, dio_specs, dio_pack, dw_shape, dw_spec, alias=None, after=()):
    n_in, n_ex, n_io = len(ins), len(extras), len(dio_shapes)
    al, aliases = list(after), {}
    if alias is not None:
        al, aliases = al + [alias], {n_in + n_ex + 2 + len(after): 0}
    al_specs = [pl.BlockSpec(memory_space=pl.ANY)] * len(al)

    def body(*refs):
        in_refs, ex_refs = refs[:n_in], refs[n_in:n_in + n_ex]
        w_ref, dout_ref = refs[n_in + n_ex], refs[n_in + n_ex + 1]
        outs = refs[n_in + n_ex + 2 + len(al):]
        dio_refs, dw_ref = outs[:n_io], outs[n_io]
        j = pl.program_id(0)
        u, pre_vjp = jax.vjp(pre, *[r[...] for r in in_refs])
        rows = lax.broadcasted_iota(jnp.int32, u.shape, 0)
        cv = _conv(u, w_ref, kw, rows)
        _, post_vjp = jax.vjp(lambda cc, *ex: post(j, cc, *ex), cv, *[r[...] for r in ex_refs])
        g = post_vjp(dout_ref[...].astype(F32))
        dc = g[0]
        for k in range(kw):
            dw_ref[k:k + 1, :] = jnp.sum(dc * _shift_down(u, kw - 1 - k, rows), axis=0, keepdims=True)
        du = _conv_t(dc, w_ref, kw, rows)
        for r, val in zip(dio_refs, dio_pack(*pre_vjp(du), *g[1:])):
            if isinstance(val, tuple):
                for t, part in enumerate(val):
                    r[t] = part.astype(r.dtype)
            else:
                r[...] = val.astype(r.dtype)

    return pl.pallas_call(
        body, name=name, grid=(ncol,),
        in_specs=list(in_specs) + list(extra_specs) + [w_spec, dout_spec] + al_specs,
        out_specs=list(dio_specs) + [dw_spec],
        out_shape=list(dio_shapes) + [dw_shape],
        input_output_aliases=aliases,
        compiler_params=_params(("arbitrary",)),
    )(*ins, *extras, w, dout, *al)


def _tri_inverse(a):
    c = a.shape[0]
    ii = lax.broadcasted_iota(jnp.int32, (c, c), 0)
    jj = lax.broadcasted_iota(jnp.int32, (c, c), 1)
    eye = (ii == jj).astype(F32)

    def same_block(bits):
        return jnp.right_shift(ii, bits) == jnp.right_shift(jj, bits)

    d = jnp.where(same_block(3), a, 0.0)
    d2 = _mmx(d, d, NN)
    yield
    t = (eye - d) + _mmx(eye - d, d2, NN)
    d4 = _mmx(d2, d2, NN)
    yield
    t = t + _mmx(t, d4, NN)
    yield
    bits = 3
    while (1 << bits) < c:
        low = jnp.where(jnp.logical_and(same_block(bits + 1), jnp.logical_not(same_block(bits))), a, 0.0)
        tl = _mmx(t, low, NN)
        yield
        t = t - _mmx(tl, t, NN)
        yield
        bits += 1
    return t


def _lockstep(gens):
    out = [None] * len(gens)
    live = list(range(len(gens)))
    while live:
        nxt = []
        for i in live:
            try:
                next(gens[i])
                nxt.append(i)
            except StopIteration as e:
                out[i] = e.value
        live = nxt
    return out


def _dn_common(q, k, v, gb, bb):
    c = q.shape[0]
    ii = lax.broadcasted_iota(jnp.int32, (c, c), 0)
    jj = lax.broadcasted_iota(jnp.int32, (c, c), 1)
    causal = jj <= ii
    strict = jj < ii
    low = causal.astype(F32)
    ones = jnp.ones((c, LANES), F32)
    gc = _mmx(low, gb, NN)
    kb = k * bb
    vb = v * bb
    kk = _mm(kb, k, NT)
    qk = _mm(q, k, NT)
    yield
    diff = (_mmx(gc, ones, NT) - _mmx(ones, gc, NT)) * (1.0 / LANES)
    gl = jnp.sum(gb, axis=0, keepdims=True)
    eg = jnp.exp(gc)
    egm = jnp.exp(gl - gc)
    egl = jnp.exp(gl)
    kbg = kb * eg
    yield
    decay = jnp.where(causal, jnp.exp(jnp.where(causal, diff, 0.0)), 0.0)
    t = yield from _tri_inverse(jnp.where(strict, kk * decay, 0.0))
    attn = qk * decay
    return dict(causal=causal, strict=strict, low=low, ones=ones, decay=decay, eg=eg, egm=egm, egl=egl,
                kb=kb, vb=vb, kbg=kbg, kk=kk, t=t, qk=qk, attn=attn, qd=q * eg, kd=k * egm)


def _dn_fwd_stages(q, k, v, gb, bb, s):
    m = yield from _dn_common(q, k, v, gb, bb)
    r = m["vb"] - _mm(m["kbg"], s, NN)
    qs = _mm(m["qd"], s, NN)
    yield
    u = _mmx(m["t"], r, NN)
    yield
    o = qs + _mm(m["attn"], u, NN)
    s2 = s * m["egl"] + _mm(m["kd"], u, TN)
    return o, s2


def _dn_bwd_stages(q, k, v, gb, bb, s, do, dsp):
    m = yield from _dn_common(q, k, v, gb, bb)
    c = q.shape[0]
    t, decay, eg, egm, egl = m["t"], m["decay"], m["eg"], m["egm"], m["egl"]
    r = m["vb"] - _mm(m["kbg"], s, NN)
    du = _mm(m["attn"], do, TN) + _mm(m["kd"], dsp, NN)
    dqd = _mm(do, s, NT)
    ds = dsp * egl + _mm(m["qd"], do, TN)
    degl = jnp.sum(jnp.sum(dsp * s, axis=1, keepdims=True), axis=0, keepdims=True)
    yield
    u = _mmx(t, r, NN)
    dr = _mmx(t, du, TN)
    yield
    dattn = jnp.where(m["causal"], _mm(do, u, NT), 0.0)
    dkd = _mm(u, dsp, NT)
    da = jnp.where(m["strict"], -_mm(dr, u, NT), 0.0)
    dkbg = -_mm(dr, s, NT)
    ds = ds - _mm(m["kbg"], dr, TN)
    yield
    dkk = da * decay
    dqk = dattn * decay
    ddiff = (da * m["kk"] + dattn * m["qk"]) * decay
    dgc = _mmx(ddiff, m["ones"], NN) - _mmx(ddiff, m["ones"], TN)
    dkb = _mm(dkk, k, NN) + dkbg * eg
    dk = _mm(dkk, m["kb"], TN) + _mm(dqk, q, TN) + dkd * egm + dkb * bb
    dq = _mm(dqk, k, NN) + dqd * eg
    yield
    dgc = dgc + jnp.sum(dqd * q + dkbg * m["kb"], axis=-1, keepdims=True) * eg
    tt = jnp.sum(dkd * k, axis=-1, keepdims=True) * egm
    dgc = dgc - tt
    dgl = jnp.sum(tt, axis=0, keepdims=True) + degl * egl
    dbb = jnp.sum(dkb * k + dr * v, axis=-1, keepdims=True) + jnp.zeros((c, LANES), F32)
    dv = dr * bb
    dgb = _mmx(m["low"], dgc, TN) + dgl
    return dq, dk, dv, dgb, dbb, ds


def _dn_fwd_chunk(q, k, v, gb, bb, s):
    return _lockstep([_dn_fwd_stages(q, k, v, gb, bb, s)])[0]


def _dn_bwd_chunk(q, k, v, gb, bb, s, do, dsp):
    return _lockstep([_dn_bwd_stages(q, k, v, gb, bb, s, do, dsp)])[0]


def _dn_fwd_call(qkvn, gb, bb):
    _, nh, s, dh = qkvn.shape
    c = min(DN_CHUNK, s)
    n = s // c
    hb = nh

    def body(q_ref, k_ref, v_ref, g_ref, b_ref, o_ref, st_ref, s_ref):
        @pl.when(pl.program_id(1) == 0)
        def _():
            s_ref[...] = jnp.zeros_like(s_ref)

        st_ref[...] = s_ref[...]
        heads = [_dn_fwd_stages(q_ref[h], k_ref[h], v_ref[h], g_ref[h], b_ref[h], s_ref[h]) for h in range(hb)]
        for h, (o, s2) in enumerate(_lockstep(heads)):
            o_ref[h] = o
            s_ref[h] = s2

    def qspec(t):
        return pl.BlockSpec((None, hb, c, dh), lambda i, j, t=t: (t, i, j, 0))

    hs = pl.BlockSpec((hb, c, dh), lambda i, j: (i, j, 0))
    return pl.pallas_call(
        body, name="dn_fwd", grid=(nh // hb, n),
        in_specs=[qspec(0), qspec(1), qspec(2), hs, hs],
        out_specs=[hs, pl.BlockSpec((hb, None, dh, dh), lambda i, j: (i, j, 0, 0))],
        out_shape=[jax.ShapeDtypeStruct((nh, s, dh), F32), jax.ShapeDtypeStruct((nh, n, dh, dh), F32)],
        scratch_shapes=[pltpu.VMEM((hb, dh, dh), F32)],
        compiler_params=_params(("arbitrary", "arbitrary")),
    )(qkvn, qkvn, qkvn, gb, bb)


def _dn_bwd_call(qkvn, gb, bb, states, do):
    _, nh, s, dh = qkvn.shape
    c = min(DN_CHUNK, s)
    n = s // c
    hb = nh

    def body(q_ref, k_ref, v_ref, g_ref, b_ref, st_ref, do_ref, dqkv_ref, dg_ref, db_ref, ds_ref):
        @pl.when(pl.program_id(1) == 0)
        def _():
            ds_ref[...] = jnp.zeros_like(ds_ref)

        heads = [_dn_bwd_stages(q_ref[h], k_ref[h], v_ref[h], g_ref[h], b_ref[h], st_ref[h], do_ref[h], ds_ref[h])
                 for h in range(hb)]
        for h, (dq, dk, dv, dg, db, ds) in enumerate(_lockstep(heads)):
            dqkv_ref[0, h] = dq
            dqkv_ref[1, h] = dk
            dqkv_ref[2, h] = dv
            dg_ref[h] = dg
            db_ref[h] = db
            ds_ref[h] = ds

    def qspec(t):
        return pl.BlockSpec((None, hb, c, dh), lambda i, j, t=t: (t, i, n - 1 - j, 0))

    hs = pl.BlockSpec((hb, c, dh), lambda i, j: (i, n - 1 - j, 0))
    sh = jax.ShapeDtypeStruct((nh, s, dh), F32)
    return pl.pallas_call(
        body, name="dn_bwd", grid=(nh // hb, n),
        in_specs=[qspec(0), qspec(1), qspec(2), hs, hs,
                  pl.BlockSpec((hb, None, dh, dh), lambda i, j: (i, n - 1 - j, 0, 0)), hs],
        out_specs=[pl.BlockSpec((3, hb, c, dh), lambda i, j: (0, i, n - 1 - j, 0)), hs, hs],
        out_shape=[jax.ShapeDtypeStruct((3, nh, s, dh), F32), sh, sh],
        scratch_shapes=[pltpu.VMEM((hb, dh, dh), F32)],
        compiler_params=_params(("arbitrary", "arbitrary")),
    )(qkvn, qkvn, qkvn, gb, bb, states, do)


def _adamw(w, g, m, v, name, emit_grad=False):
    r, c = w.shape
    br = _div_tile(r, max(8, (1 << 18) // max(c, 1)), 8)

    def fn(w, g, m, v):
        g_in = g
        m = ADAM_B1 * m + (1.0 - ADAM_B1) * g
        v = ADAM_B2 * v + (1.0 - ADAM_B2) * jnp.square(g)
        m_hat = m / (1.0 - ADAM_B1 ** ADAM_STEP)
        v_hat = v / (1.0 - ADAM_B2 ** ADAM_STEP)
        delta = -ADAM_LR * (m_hat / (jnp.sqrt(v_hat) + ADAM_EPS) + ADAM_WD * w)
        return (delta, m, v, g_in) if emit_grad else (delta, m, v)

    spec = pl.BlockSpec((br, c), lambda i: (i, 0))
    return _ew(fn, name, (r // br,), [w, g, m, v], [spec] * 4, [((r, c), F32, spec)] * (4 if emit_grad else 3))


def _coords():
    return lax.axis_index("x"), lax.axis_index("y"), lax.axis_index("c")


def _all_gather_small(v, after=()):
    r, w = v.shape
    n_after = len(after)

    def body(v_ref, *rest):
        out_ref, send_sems, recv_sems, local_sem = rest[n_after:]
        x, y, c = _coords()
        me = 4 * x + 2 * y + c
        mine = pltpu.make_async_copy(v_ref, out_ref.at[me], local_sem)
        mine.start()
        peers = []
        for k in range(1, 8):
            px = 1 - x if k & 4 else x
            py = 1 - y if k & 2 else y
            pc = 1 - c if k & 1 else c
            peers.append((px, py, pc))
        sends = []
        for k, peer in enumerate(peers):
            cp = pltpu.make_async_remote_copy(src_ref=v_ref, dst_ref=out_ref.at[me], send_sem=send_sems.at[k],
                                              recv_sem=recv_sems.at[k], device_id=peer, device_id_type=MESH)
            cp.start()
            sends.append(cp)
        for k, (px, py, pc) in enumerate(peers):
            pltpu.make_async_remote_copy(src_ref=v_ref, dst_ref=out_ref.at[4 * px + 2 * py + pc],
                                         send_sem=send_sems.at[k], recv_sem=recv_sems.at[k],
                                         device_id=(px, py, pc), device_id_type=MESH).wait_recv()
        for cp in sends:
            cp.wait_send()
        mine.wait()

    return pl.pallas_call(
        body, name="ag_small", out_shape=jax.ShapeDtypeStruct((8, r, w), v.dtype),
        in_specs=[pl.BlockSpec(memory_space=pltpu.VMEM)] + [pl.BlockSpec(memory_space=pl.ANY)] * n_after,
        out_specs=pl.BlockSpec(memory_space=pltpu.VMEM),
        scratch_shapes=[pltpu.SemaphoreType.DMA((7,)), pltpu.SemaphoreType.DMA((7,)), pltpu.SemaphoreType.DMA],
        compiler_params=pltpu.CompilerParams(vmem_limit_bytes=VMEM_LIMIT),
    )(v, *after)


def _hbm_call(body, name, arrays, out_shapes, n_sems, aliases=None):
    hbm = pl.BlockSpec(memory_space=pltpu.HBM)
    return pl.pallas_call(
        body, name=name, out_shape=list(out_shapes), in_specs=[hbm] * len(arrays), out_specs=[hbm] * len(out_shapes),
        input_output_aliases=aliases or {},
        scratch_shapes=[pltpu.SemaphoreType.DMA((n_sems,)), pltpu.SemaphoreType.DMA((n_sems,))],
    )(*arrays)


def _half_rows(ref_shape, c):
    h = ref_shape[1] // 2
    return pl.ds(pl.multiple_of(c * h, 16), h), pl.ds(pl.multiple_of((1 - c) * h, 16), h)


def _pair_copies(p_refs, land_refs, send_sems, recv_sems):
    x, y, c = _coords()
    cps = []
    for w, (p, land) in enumerate(zip(p_refs, land_refs)):
        _, other = _half_rows(p.shape, c)
        cps.append(pltpu.make_async_remote_copy(src_ref=p.at[:, other], dst_ref=land, send_sem=send_sems.at[w],
                                                recv_sem=recv_sems.at[w], device_id=(x, y, 1 - c),
                                                device_id_type=MESH))
    return cps


def _pair_gather(gs, name):
    n = len(gs)

    def body(*refs):
        ins, outs, send_sems, recv_sems = refs[:n], refs[n:2 * n], refs[2 * n], refs[2 * n + 1]
        x, y, c = _coords()
        cps = []
        for w in range(n):
            cp = pltpu.make_async_remote_copy(src_ref=ins[w].at[c], dst_ref=outs[w].at[c], send_sem=send_sems.at[w],
                                              recv_sem=recv_sems.at[w], device_id=(x, y, 1 - c), device_id_type=MESH)
            cp.start()
            cps.append(cp)
        for w, cp in enumerate(cps):
            pltpu.make_async_remote_copy(src_ref=ins[w].at[c], dst_ref=outs[w].at[1 - c], send_sem=send_sems.at[w],
                                         recv_sem=recv_sems.at[w], device_id=(x, y, 1 - c),
                                         device_id_type=MESH).wait_recv()
            cp.wait_send()

    shapes = [jax.ShapeDtypeStruct(g.shape, g.dtype) for g in gs]
    return _hbm_call(body, name, gs, shapes, n, aliases={i: i for i in range(n)})


_HBM = pl.BlockSpec(memory_space=pltpu.HBM)
_SEM = pl.BlockSpec(memory_space=pltpu.SEMAPHORE)
_EFFECT = pltpu.SideEffectType.DATAFLOW_SIDE_EFFECTING


def _split_start(name, srcs, lands, after, n_copies, make_copies):
    n, m = len(srcs), len(lands)
    arrays = [pltpu.with_memory_space_constraint(a, pltpu.HBM) for a in list(srcs) + list(lands)]

    def body(*refs):
        src_refs, land_refs = refs[:n], refs[n:n + m]
        send_sems, recv_sems = refs[n + m + 1], refs[n + m + 2]
        for cp in make_copies(src_refs, land_refs, send_sems, recv_sems):
            cp.start()
        refs[-1][...] = jnp.zeros_like(refs[-1])

    outs = pl.pallas_call(
        body, name=name,
        out_shape=(pltpu.SemaphoreType.DMA((n_copies,)), pltpu.SemaphoreType.DMA((n_copies,)),
                   *[pltpu.HBM(a.shape, a.dtype) for a in arrays], jax.ShapeDtypeStruct((8, LANES), F32)),
        in_specs=[_HBM] * (n + m) + [pl.BlockSpec(memory_space=pl.ANY)],
        out_specs=(_SEM, _SEM, *[_HBM] * (n + m), pl.BlockSpec(memory_space=pltpu.VMEM)),
        input_output_aliases={i: 2 + i for i in range(n + m)},
        compiler_params=pltpu.CompilerParams(has_side_effects=_EFFECT),
    )(*arrays, after)
    return (outs[0], outs[1], list(outs[2:2 + n]), list(outs[2 + n:2 + n + m])), outs[-1]


def _split_wait(name, state, after, make_copies):
    send_sems, recv_sems, srcs, lands = state
    n, m = len(srcs), len(lands)

    def body(*refs):
        src_refs, land_refs = refs[:n], refs[n:n + m]
        for cp in make_copies(src_refs, land_refs, refs[n + m], refs[n + m + 1]):
            cp.wait_send()
            cp.wait_recv()

    outs = pl.pallas_call(
        body, name=name, out_shape=tuple(pltpu.HBM(a.shape, a.dtype) for a in srcs + lands),
        in_specs=[_HBM] * (n + m) + [_SEM, _SEM, pl.BlockSpec(memory_space=pl.ANY)], out_specs=tuple([_HBM] * (n + m)),
        input_output_aliases={i: i for i in range(n + m)},
        compiler_params=pltpu.CompilerParams(has_side_effects=_EFFECT),
    )(*srcs, *lands, send_sems, recv_sems, after)
    return list(outs[:n]), list(outs[n:])


def _scatter_copies(q_refs, land_refs, send_sems, recv_sems):
    x, y, c = _coords()
    cps = []
    for w, (q, land) in enumerate(zip(q_refs, land_refs)):
        for k, (px, py) in enumerate([(1 - x, y), (x, 1 - y), (1 - x, 1 - y)]):
            cps.append(pltpu.make_async_remote_copy(src_ref=q.at[2 * px + py], dst_ref=land.at[k],
                                                    send_sem=send_sems.at[3 * w + k], recv_sem=recv_sems.at[3 * w + k],
                                                    device_id=(px, py, c), device_id_type=MESH))
    return cps


def _chip_scatter_start(qs, after, name):
    lands = [lax.empty((3,) + q.shape[1:], q.dtype) for q in qs]
    return _split_start(name, qs, lands, after, 3 * len(qs), _scatter_copies)


def _chip_scatter_wait(state, after, name):
    return _split_wait(name, state, after, _scatter_copies)


def _gather_copies(src_refs, buf_refs, send_sems, recv_sems):
    x, y, c = _coords()
    j = 2 * x + y
    cps = []
    for w, buf in enumerate(buf_refs):
        mine, _ = _half_rows(buf.shape, c)
        for k, (px, py) in enumerate([(1 - x, y), (x, 1 - y), (1 - x, 1 - y)]):
            cps.append(pltpu.make_async_remote_copy(src_ref=buf.at[j, mine], dst_ref=buf.at[j, mine],
                                                    send_sem=send_sems.at[3 * w + k], recv_sem=recv_sems.at[3 * w + k],
                                                    device_id=(px, py, c), device_id_type=MESH))
    return cps


def _gather_wait_copies(src_refs, buf_refs, send_sems, recv_sems):
    x, y, c = _coords()
    j = 2 * x + y
    cps = []
    for w, buf in enumerate(buf_refs):
        mine, _ = _half_rows(buf.shape, c)
        for k, (px, py) in enumerate([(1 - x, y), (x, 1 - y), (1 - x, 1 - y)]):
            cps.append(pltpu.make_async_remote_copy(src_ref=buf.at[j, mine], dst_ref=buf.at[2 * px + py, mine],
                                                    send_sem=send_sems.at[3 * w + k], recv_sem=recv_sems.at[3 * w + k],
                                                    device_id=(px, py, c), device_id_type=MESH))
    return cps


def _pair_forward(bufs, name):
    n = len(bufs)

    def body(*refs):
        ins, outs, send_sems, recv_sems = refs[:n], refs[n:2 * n], refs[2 * n], refs[2 * n + 1]
        x, y, c = _coords()
        chips = [(1 - x, y), (x, 1 - y), (1 - x, 1 - y)]
        cps = []
        for w in range(n):
            mine, _ = _half_rows(outs[w].shape, c)
            for k, (px, py) in enumerate(chips):
                cp = pltpu.make_async_remote_copy(src_ref=ins[w].at[2 * px + py, mine],
                                                  dst_ref=outs[w].at[2 * px + py, mine],
                                                  send_sem=send_sems.at[3 * w + k], recv_sem=recv_sems.at[3 * w + k],
                                                  device_id=(x, y, 1 - c), device_id_type=MESH)
                cp.start()
                cps.append(cp)
        for w in range(n):
            _, sib = _half_rows(outs[w].shape, c)
            for k, (px, py) in enumerate(chips):
                pltpu.make_async_remote_copy(src_ref=ins[w].at[2 * px + py, sib], dst_ref=outs[w].at[2 * px + py, sib],
                                             send_sem=send_sems.at[3 * w + k], recv_sem=recv_sems.at[3 * w + k],
                                             device_id=(x, y, 1 - c), device_id_type=MESH).wait_recv()
        for cp in cps:
            cp.wait_send()

    shapes = [jax.ShapeDtypeStruct(b.shape, b.dtype) for b in bufs]
    return _hbm_call(body, name, bufs, shapes, 3 * n, aliases={i: i for i in range(n)})


def _slot_rows(h, cs):
    return _div_tile(h, max(16, (1 << 19) // cs), 16)


def _cast_into_slot(w, slot, name, after=()):
    r, cs = w.shape
    br = _slot_rows(r, cs)
    return _ew_slot(lambda a: a, name, (r // br,), slot, [w], [pl.BlockSpec((br, cs), lambda i, s: (i, 0))],
                    (4, r, cs), BF16, pl.BlockSpec((None, br, cs), lambda i, s: (s[0], i, 0)), after=after)


def _pair_add(p, rb, core, name):
    n, r, cs = p.shape
    h = r // 2
    br = _slot_rows(h, cs)
    nb = h // br
    return _ew_slot(lambda a, b: a + b, name, (n, nb), core, [p, rb],
                    [pl.BlockSpec((None, br, cs), lambda s, i, c: (s, c[0] * nb + i, 0)),
                     pl.BlockSpec((None, br, cs), lambda s, i, c: (s, i, 0))],
                    (n, h, cs), BF16, pl.BlockSpec((None, br, cs), lambda s, i, c: (s, i, 0)))


def _chip_sum(q, r3, slots, name):
    _, h, cs = q.shape
    br = _slot_rows(h, cs)

    def fn(a, b):
        acc = a.astype(F32)
        for k in range(3):
            acc = acc + b[k].astype(F32)
        return acc

    return _ew_slot(fn, name, (h // br,), slots, [q, r3],
                    [pl.BlockSpec((None, br, cs), lambda i, s: (s[0], i, 0)),
                     pl.BlockSpec((3, br, cs), lambda i, s: (0, i, 0))],
                    (2, h, cs), F32, pl.BlockSpec((None, br, cs), lambda i, s: (s[1], i, 0)))


def _sum_slots(r, name):
    n, h, w = r.shape
    br = _div_tile(h, 2048, 16)

    def fn(blk):
        acc = blk[0].astype(F32)
        for s in range(1, n):
            acc = acc + blk[s].astype(F32)
        return (acc,)

    return _ew(fn, name, (h // br,), [r], [pl.BlockSpec((n, br, w), lambda i: (0, i, 0))],
               [((h, w), F32, pl.BlockSpec((br, w), lambda i: (i, 0)))])[0]


def _pack_small(arrs):
    flat = jnp.concatenate([a.reshape(-1).astype(F32) for a in arrs])
    n = flat.shape[0]
    rows = -(-n // LANES)
    rows = -(-rows // 8) * 8
    return jnp.pad(flat, (0, rows * LANES - n)).reshape(rows, LANES)


def _unpack_small(p, shapes):
    lead = p.shape[:-2]
    flat = p.reshape(lead + (-1,))
    out, off = [], 0
    for sh in shapes:
        n = 1
        for d in sh:
            n *= d
        out.append(flat[..., off:off + n].reshape(lead + tuple(sh)))
        off += n
    return out


def kernel(x, c, w_ada, b_ada, norm1_g, w_in, dn_conv_w, dn_a_log, dn_dt_bias, dn_norm_g, dn_w_o, cf_conv_w, cf_ln_g, cf_ln_b, cf_w_o, w_out, norm2_g, ffn_w_up, ffn_conv_w, ffn_w_down, final_norm_g, loss_target, m_w_ada, m_b_ada, m_norm1_g, m_w_in, m_dn_conv_w, m_dn_a_log, m_dn_dt_bias, m_dn_norm_g, m_dn_w_o, m_cf_conv_w, m_cf_ln_g, m_cf_ln_b, m_cf_w_o, m_w_out, m_norm2_g, m_ffn_w_up, m_ffn_conv_w, m_ffn_w_down, m_final_norm_g, v_w_ada, v_b_ada, v_norm1_g, v_w_in, v_dn_conv_w, v_dn_a_log, v_dn_dt_bias, v_dn_norm_g, v_dn_w_o, v_cf_conv_w, v_cf_ln_g, v_cf_ln_b, v_cf_w_o, v_w_out, v_norm2_g, v_ffn_w_up, v_ffn_conv_w, v_ffn_w_down, v_final_norm_g):
    xi, yi, ci = _coords()
    chip = 2 * xi + yi
    me = 4 * xi + 2 * yi + ci
    core = jnp.reshape(ci, (1,)).astype(jnp.int32)

    S, D = x.shape[1], x.shape[2]
    NH = dn_a_log.shape[1]
    DH = dn_norm_g.shape[1]
    DNW = NH * DH
    CFW = cf_ln_g.shape[1]
    FFN = ffn_w_down.shape[1] * 4
    KDN, KCF, KFF = dn_conv_w.shape[1], cf_conv_w.shape[1], ffn_conv_w.shape[1]
    NIN = w_in.shape[2] * 4
    assert NIN == 4 * DNW + 2 * NH + 2 * CFW + 2 * D and DH == LANES and 2 * NH <= LANES
    x2, tgt = x[0], loss_target[0]

    slot_chip = jnp.reshape(chip, (1,)).astype(jnp.int32)
    big = [w_in[0], dn_w_o[0], cf_w_o[0], w_out[0], ffn_w_up[0], ffn_w_down[0]]
    names_big = ["w_in", "dn_w_o", "cf_w_o", "w_out", "ffn_w_up", "ffn_w_down"]
    buf_in = _cast_into_slot(big[0], slot_chip, "cast_w_in")

    sm_shapes = [(D,), (KDN, 3 * DNW // 4), (KCF, CFW // 4), (KFF, FFN // 4)]
    g1 = _all_gather_small(_pack_small([c[0], dn_conv_w[0], cf_conv_w[0], ffn_conv_w[0]]))
    c_all, dcw_s, ccw_s, fcw_s = _unpack_small(g1, sm_shapes)

    def chips_cols(t):
        t = t[0::2]
        return jnp.transpose(t, (1, 0, 2)).reshape(t.shape[1], -1)

    dn_cw, cf_cw, ff_cw = chips_cols(dcw_s), chips_cols(ccw_s), chips_cols(fcw_s)


    cb_cf = _div_tile(CFW, 256, LANES)
    o_b = 4 * DNW
    o_glu = o_b + 2 * NH
    o_ga = o_glu + 2 * CFW
    NA = NIN - 2 * NH
    a_z, a_ga, a_gb, a_glu, a_ba = 3 * DNW, 4 * DNW, 4 * DNW + D, 4 * DNW + 2 * D, NA

    segs = [(0, 0, o_b), (o_b, NA, 2 * NH), (o_ga, a_ga, 2 * D)]
    for t in range(CFW // cb_cf):
        segs += [(o_glu + t * cb_cf, a_glu + 2 * t * cb_cf, cb_cf),
                 (o_glu + CFW + t * cb_cf, a_glu + (2 * t + 1) * cb_cf, cb_cf)]
    CS = NIN // 4

    def shard_slices(lo, n):
        out = []
        while n > 0:
            j, off = lo // CS, lo % CS
            m = min(n, CS - off)
            out.append(w_in_g[j][:, off:off + m])
            lo, n = lo + m, n - m
        return out

    CA = w_ada.shape[2]
    b_sh = lax.dynamic_slice(b_ada, (0, chip * CA), (1, CA))
    tn_a = _div_tile(CA, 512, LANES)

    def mod_fn(cc, w, b):
        return (_mm(_silu(cc), w, NN) + b,)

    mod_sh = _ew(mod_fn, "ada_mod", (CA // tn_a,), [c_all, w_ada[0], b_sh],
                 [_full((8, D)), pl.BlockSpec((D, tn_a), lambda j: (0, j)), pl.BlockSpec((1, tn_a), lambda j: (0, j))],
                 [((8, CA), F32, pl.BlockSpec((8, tn_a), lambda j: (0, j)))])[0]
    g2 = _all_gather_small(_pack_small([mod_sh]))
    in_state, in_token = _split_start("ag_in_start", [], [buf_in], g2, 3, _gather_copies)
    bufs = [_cast_into_slot(w, slot_chip, "cast_" + nm, after=[in_token]) for w, nm in zip(big[1:], names_big[1:])]
    mid_state, mid_token = _split_start("ag_mid_start", [], bufs[:3], in_token, 9, _gather_copies)
    ag_state, ag_token = _split_start("ag_late_start", [], bufs[3:], mid_token, 6, _gather_copies)
    mod_all = _unpack_small(g2, [(8, CA)])[0][0::2]
    mod_all = jnp.transpose(mod_all, (1, 0, 2)).reshape(8, 4 * CA)
    mod_me = lax.dynamic_slice(mod_all, (me, 0), (1, 6 * D))
    sh1, sc1, gt1, sh2, sc2, gt2 = [mod_me[:, i * D:(i + 1) * D] for i in range(6)]

    bs = _div_tile(S, 128, 8)
    nb = S // bs
    vecD = _full((1, D))
    rowD = _row(bs, D)

    hn1 = _ew(lambda a, g, sc, sh: (_f_normmod(a, g, sc, sh),), "norm1_fwd", (nb,),
              [x2, norm1_g, sc1, sh1], [rowD, vecD, vecD, vecD], [((S, D), BF16, rowD)],
              after=[ag_token, m_w_in[0], v_w_in[0]])[0]
    _, in_landed = _split_wait("ag_in_wait", in_state, hn1, _gather_wait_copies)
    (w_in_g,) = _pair_forward(in_landed, "ag_in_pair")
    w_aug = jnp.concatenate([p for o, a, n in sorted(segs, key=lambda s: s[1]) for p in shard_slices(o, n)]
                            + [jnp.zeros((D, LANES - 2 * NH), BF16)], axis=1)
    proj = _matmul(hn1, w_aug, "nn", F32, "mm_in")

    def dn_post(j, cv):
        s = _silu(cv)
        nrm = s * lax.rsqrt(jnp.sum(s * s, axis=-1, keepdims=True) + EPS)
        fq = (j < NH).astype(F32)
        fk = (j < 2 * NH).astype(F32)
        scale = fq * (DH ** -0.5) + (1.0 - fq)
        return fk * (nrm * scale) + (1.0 - fk) * s

    def ident(a):
        return a

    def colS(w, off=0):
        return pl.BlockSpec((S, w), lambda j, off=off: (0, j + off))

    def wS(kw, w):
        return pl.BlockSpec((kw, w), lambda j: (0, j))

    qkv_spec = pl.BlockSpec((None, None, S, DH), lambda j: (j // NH, j % NH, 0, 0))
    dn_args = dict(kw=KDN, ncol=3 * NH, ins=[proj], in_specs=[colS(DH)], extras=[], extra_specs=[],
                   w=dn_cw, w_spec=wS(KDN, DH), pre=ident, post=dn_post)
    qkvn = _conv_fwd_call("dn_conv_fwd", out_shape=jax.ShapeDtypeStruct((3, NH, S, DH), F32), out_spec=qkv_spec,
                          **dn_args)

    alp = jnp.pad(dn_a_log, ((0, 0), (NH, LANES - 2 * NH)))
    dtb = jnp.pad(dn_dt_bias, ((0, 0), (NH, LANES - 2 * NH)))
    vecL = _full((1, LANES))
    ba_spec = _row(bs, LANES, a_ba // LANES)
    rowL = _row(bs, LANES)
    gate_fn = functools.partial(_f_dn_gate, NH)
    def gate_fwd(a, p, q):
        val = gate_fn(a, p, q)
        lane = lax.broadcasted_iota(jnp.int32, val.shape, 1)

        def spread(col):
            return jnp.sum(jnp.where(lane == col, val, 0.0), axis=-1, keepdims=True) + jnp.zeros(val.shape, F32)

        return tuple(spread(h) for h in range(NH)), tuple(spread(NH + h) for h in range(NH))

    hrow = pl.BlockSpec((NH, bs, DH), lambda i: (0, i, 0))
    bb_b, gb_b = _ew(gate_fwd, "dn_gate_fwd", (nb,), [proj, alp, dtb], [ba_spec, vecL, vecL],
                     [((NH, S, DH), F32, hrow), ((NH, S, DH), F32, hrow)])
    o_dn, states = _dn_fwd_call(qkvn, gb_b, bb_b)

    bsh = _div_tile(S, 512, 8)
    nbh = S // bsh
    o_spec = pl.BlockSpec((None, bsh, DH), lambda i, h: (h, i, 0))
    z_spec = pl.BlockSpec((bsh, DH), lambda i, h: (i, a_z // DH + h))
    oh_spec = pl.BlockSpec((bsh, DH), lambda i, h: (i, h))
    ng_spec = pl.BlockSpec((1, DH), lambda i, h: (0, 0))
    on = _ew(lambda o, z, g: (_f_dn_post(o, z, g),), "dn_post_fwd", (nbh, NH), [o_dn, proj, dn_norm_g],
             [o_spec, z_spec, ng_spec], [((S, DNW), BF16, oh_spec)])[0]
    _, landed = _split_wait("ag_mid_wait", mid_state, on, _gather_wait_copies)
    w_do_f, w_co_f, w_out_g = _pair_forward(landed, "ag_mid_pair")
    w_out_f = w_out_g.reshape(-1, w_out_g.shape[2])
    br_a = _matmul(on, w_do_f, "nn", F32, "mm_dn_o")

    def glu_pre(val, gl):
        return val * _sigmoid(gl)

    def glu_spec(t):
        return pl.BlockSpec((S, cb_cf), lambda j, t=t: (0, a_glu // cb_cf + 2 * j + t))

    cf_args = dict(kw=KCF, ncol=CFW // cb_cf, ins=[proj, proj], in_specs=[glu_spec(0), glu_spec(1)],
                   extras=[], extra_specs=[], w=cf_cw, w_spec=wS(KCF, cb_cf), pre=glu_pre, post=lambda j, cv: cv)
    uc = _conv_fwd_call("cf_conv_fwd", out_shape=jax.ShapeDtypeStruct((S, CFW), F32), out_spec=colS(cb_cf), **cf_args)
    rowC = _row(bs, CFW)
    vecC = _full((1, CFW))
    ub = _ew(lambda u, g, b: (_f_cf_ln(u, g, b),), "cf_ln_fwd", (nb,), [uc, cf_ln_g, cf_ln_b], [rowC, vecC, vecC],
             [((S, CFW), BF16, rowC)])[0]
    br_b = _matmul(ub, w_co_f, "nn", F32, "mm_cf_o")

    ga_spec, gb_spec = _row(bs, D, a_ga // D), _row(bs, D, a_gb // D)
    merged = _ew(lambda a, b, ga, gb: (_f_merge(a, b, ga, gb),), "merge_fwd", (nb,), [br_a, br_b, proj, proj],
                 [rowD, rowD, ga_spec, gb_spec], [((S, D), BF16, rowD)])[0]
    mix = _matmul(merged, w_out_f, "nn", F32, "mm_out")

    x1, hn2 = _ew(_f_res_normmod, "norm2_fwd", (nb,), [x2, mix, gt1, norm2_g, sc2, sh2],
                  [rowD, rowD, vecD, vecD, vecD, vecD], [((S, D), F32, rowD), ((S, D), BF16, rowD)])
    _, landed = _split_wait("ag_late_wait", ag_state, hn2, _gather_wait_copies)
    w_up_f, w_dn_g = _pair_forward(landed, "ag_late_pair")
    w_dn_f = w_dn_g.reshape(-1, w_dn_g.shape[2])
    up_all = _matmul(hn2, w_up_f, "nn", F32, "mm_up")

    cb_ff = _div_tile(FFN, 256, LANES)
    ff_args = dict(kw=KFF, ncol=FFN // cb_ff, ins=[up_all], in_specs=[colS(cb_ff)], extras=[up_all],
                   extra_specs=[colS(cb_ff, FFN // cb_ff)], w=ff_cw, w_spec=wS(KFF, cb_ff),
                   pre=ident, post=lambda j, cv, up: _silu(cv) * up)
    hff = _conv_fwd_call("ffn_conv_fwd", out_shape=jax.ShapeDtypeStruct((S, FFN), BF16), out_spec=colS(cb_ff), **ff_args)
    ffo = _matmul(hff, w_dn_f, "nn", F32, "mm_down")

    gf2 = final_norm_g.reshape(1, D)

    def loss_bwd(a, f, gt, gf, t):
        val, vjp = jax.vjp(_f_loss, a, f, gt, gf, t)
        da, df, dgt, dgf, _ = vjp(jnp.ones((), F32))
        return da, df, dgt, dgf, jnp.zeros((1, LANES), F32) + val

    dx1_l, dffo, dgt2, dgf, loss_v = _ew(
        loss_bwd, "loss_bwd", (nb,), [x1, ffo, gt2, gf2, tgt], [rowD, rowD, vecD, vecD, rowD],
        [((S, D), F32, rowD), ((S, D), BF16, rowD), ((1, D), F32, vecD), ((1, D), F32, vecD),
         ((1, LANES), F32, vecL)], acc=(2, 3, 4))

    dhff = _matmul(dffo, w_dn_f, "nt", F32, "mm_down_dx")
    g_w_dn = _matmul(hff, dffo, "tn", F32, "mm_down_dw")

    slots = jnp.stack([chip, ci]).astype(jnp.int32)
    rs_groups = []

    def rs_pair_begin(parts, tag):
        lands = [lax.empty((4, p.shape[1] // 2, p.shape[2]), F32) for p in parts]
        return _split_start("rs_pair_start_" + tag, parts, lands, core, len(parts), _pair_copies)

    def rs_chips_begin(pair_state, nms, tag, after):
        parts, rbs = _split_wait("rs_pair_wait_" + tag, pair_state, after, _pair_copies)
        q16 = [_pair_add(p, rb, core, "rs_pair_add_" + nm) for p, rb, nm in zip(parts, rbs, nms)]
        state, token = _chip_scatter_start(q16, core, "rs_chips_start_" + tag)
        rs_groups.append((state, nms, tag))
        return token

    pair_a, ptok_a = rs_pair_begin([g_w_dn.reshape(4, FFN // 4, D)], "a")

    d_upall, g_ffcw = _conv_bwd_call(
        "ffn_conv_bwd", dout=dhff, dout_spec=colS(cb_ff),
        dio_shapes=[jax.ShapeDtypeStruct((2, S, FFN), BF16)],
        dio_specs=[pl.BlockSpec((2, S, cb_ff), lambda j: (0, 0, j))], dio_pack=lambda dg, du: [(dg, du)],
        dw_shape=jax.ShapeDtypeStruct((KFF, FFN), F32), dw_spec=wS(KFF, cb_ff), after=[ptok_a], **ff_args)
    tok_a = rs_chips_begin(pair_a, ["ffn_w_down"], "a", d_upall)
    dhn2 = _matmul(d_upall, w_up_f, "nt", F32, "mm_up_dx", after=[tok_a])
    g_w_up = _matmul(hn2, d_upall, "tn", F32, "mm_up_dw", out_groups=4)
    pair_b, ptok_b = rs_pair_begin([g_w_up], "b")

    def res2_bwd(a, mx, gt, g, sc, sh, dx1, dhn):
        _, vjp = jax.vjp(_f_res_normmod, a, mx, gt, g, sc, sh)
        return vjp((dx1, dhn))

    dx_r, dmix, dgt1, dg2, dsc2, dsh2 = _ew(
        res2_bwd, "norm2_bwd", (nb,), [x2, mix, gt1, norm2_g, sc2, sh2, dx1_l, dhn2],
        [rowD, rowD, vecD, vecD, vecD, vecD, rowD, rowD],
        [((S, D), F32, rowD), ((S, D), BF16, rowD)] + [((1, D), F32, vecD)] * 4, acc=(2, 3, 4, 5), after=[ptok_b])

    dmerged = _matmul(dmix, w_out_f, "nt", F32, "mm_out_dx")
    g_w_out = _matmul(merged, dmix, "tn", F32, "mm_out_dw")

    def merge_bwd(a, b, ga, gb, dm):
        _, vjp = jax.vjp(_f_merge, a, b, ga, gb)
        da, db, dga, dgb = vjp(dm)
        return da, db, jnp.concatenate([dga, dgb], axis=1)

    d_bra, d_brb, dproj = _ew(merge_bwd, "merge_bwd", (nb,), [br_a, br_b, proj, proj, dmerged],
                              [rowD, rowD, ga_spec, gb_spec, rowD],
                              [((S, D), BF16, rowD), ((S, D), BF16, rowD),
                               ((S, NA + LANES), BF16, _row(bs, 2 * D, a_ga // (2 * D)))])

    tok_b = rs_chips_begin(pair_b, ["ffn_w_up"], "b", d_brb)
    d_on = _matmul(d_bra, w_do_f, "nt", F32, "mm_dn_o_dx", after=[tok_b])
    g_w_do = _matmul(on, d_bra, "tn", F32, "mm_dn_o_dw", out_groups=4)
    d_ub = _matmul(d_brb, w_co_f, "nt", F32, "mm_cf_o_dx")
    g_w_co = _matmul(ub, d_brb, "tn", F32, "mm_cf_o_dw", out_groups=4)
    pair_c, ptok_c = rs_pair_begin([g_w_do, g_w_co, g_w_out.reshape(4, D // 4, D)], "c")

    def cf_ln_bwd(u, g, b, du):
        _, vjp = jax.vjp(_f_cf_ln, u, g, b)
        return vjp(du)

    d_uc, g_cflg, g_cflb = _ew(cf_ln_bwd, "cf_ln_bwd", (nb,), [uc, cf_ln_g, cf_ln_b, d_ub], [rowC, vecC, vecC, rowC],
                               [((S, CFW), F32, rowC), ((1, CFW), F32, vecC), ((1, CFW), F32, vecC)], acc=(1, 2),
                               after=[ptok_c])
    dproj_sds = jax.ShapeDtypeStruct((S, NA + LANES), BF16)
    dproj, g_cfcw = _conv_bwd_call(
        "cf_conv_bwd", dout=d_uc, dout_spec=colS(cb_cf), dio_shapes=[dproj_sds],
        dio_specs=[colS(2 * cb_cf, a_glu // (2 * cb_cf))], dio_pack=lambda dv, dg: [jnp.concatenate([dv, dg], axis=1)],
        dw_shape=jax.ShapeDtypeStruct((KCF, CFW), F32), dw_spec=wS(KCF, cb_cf), alias=dproj, **cf_args)

    def dn_post_bwd(o, z, g, d):
        _, vjp = jax.vjp(_f_dn_post, o, z, g)
        return vjp(d)

    tok_c = rs_chips_begin(pair_c, ["dn_w_o", "cf_w_o", "w_out"], "c", g_cfcw)
    d_o, dproj, g_dnng = _ew(dn_post_bwd, "dn_post_bwd", (nbh, NH), [o_dn, proj, dn_norm_g, d_on],
                             [o_spec, z_spec, ng_spec, oh_spec],
                             [((NH, S, DH), F32, o_spec), ((S, NA + LANES), BF16, z_spec), ((1, DH), F32, ng_spec)],
                             acc=(2,), alias=(dproj, 1), after=[tok_c])

    dqkvn, dgb_b, dbb_b = _dn_bwd_call(qkvn, gb_b, bb_b, states, d_o)
    dproj, g_dncw = _conv_bwd_call(
        "dn_conv_bwd", dout=dqkvn, dout_spec=qkv_spec, dio_shapes=[dproj_sds], dio_specs=[colS(DH)],
        dio_pack=lambda d: [d], dw_shape=jax.ShapeDtypeStruct((KDN, 3 * DNW), F32), dw_spec=wS(KDN, DH),
        alias=dproj, **dn_args)

    def gate_bwd(a, p, q, db, dg):
        lane = lax.broadcasted_iota(jnp.int32, a.shape, 1)
        d = jnp.zeros(a.shape, F32)
        for h in range(NH):
            d = d + jnp.where(lane == h, db[h], 0.0) + jnp.where(lane == NH + h, dg[h], 0.0)
        _, vjp = jax.vjp(gate_fn, a, p, q)
        return vjp(d)

    dproj, g_alp, g_dtb = _ew(gate_bwd, "dn_gate_bwd", (nb,), [proj, alp, dtb, dbb_b, dgb_b],
                              [ba_spec, vecL, vecL, hrow, hrow],
                              [((S, NA + LANES), BF16, ba_spec), ((1, LANES), F32, vecL), ((1, LANES), F32, vecL)],
                              acc=(1, 2), alias=(dproj, 0))

    g_w_aug = _matmul(hn1, dproj, "tn", F32, "mm_in_dw")
    def aug_slices(lo, n):
        out = []
        for o, a, m in sorted(segs):
            s, e = max(lo, o), min(lo + n, o + m)
            if s < e:
                out.append(g_w_aug[:, a + s - o:a + e - o])
        return out

    g_w_in = jnp.stack([jnp.concatenate(aug_slices(j * CS, CS), axis=1) for j in range(4)])
    pair_d, ptok_d = rs_pair_begin([g_w_in], "d")
    dhn1 = _matmul(dproj, w_aug, "nt", F32, "mm_in_dx", after=[ptok_d])

    def norm1_bwd(a, g, sc, sh, dhn, dxr):
        _, vjp = jax.vjp(_f_normmod, a, g, sc, sh)
        da, dg, dsc, dsh = vjp(dhn)
        return da + dxr, dg, dsc, dsh

    grad_x, dg1, dsc1, dsh1 = _ew(norm1_bwd, "norm1_bwd", (nb,), [x2, norm1_g, sc1, sh1, dhn1, dx_r],
                                  [rowD, vecD, vecD, vecD, rowD, rowD],
                                  [((S, D), F32, rowD)] + [((1, D), F32, vecD)] * 3, acc=(1, 2, 3))

    dmod = jnp.concatenate([dsh1, dsc1, dgt1, dsh2, dsc2, dgt2], axis=1)
    sm2 = [dmod, dg1, dg2, dgf, g_alp, g_dtb, g_dnng, g_cflg, g_cflb, g_dncw, g_cfcw, g_ffcw]
    sm2_shapes = [tuple(a.shape) for a in sm2]
    g3 = _all_gather_small(_pack_small(sm2))
    tok_d = rs_chips_begin(pair_d, ["w_in"], "d", g3)

    names = ["w_ada", "b_ada", "norm1_g", "w_in", "dn_conv_w", "dn_a_log", "dn_dt_bias", "dn_norm_g", "dn_w_o",
             "cf_conv_w", "cf_ln_g", "cf_ln_b", "cf_w_o", "w_out", "norm2_g", "ffn_w_up", "ffn_conv_w", "ffn_w_down",
             "final_norm_g"]
    wts = dict(zip(names, [w_ada, b_ada, norm1_g, w_in, dn_conv_w, dn_a_log, dn_dt_bias, dn_norm_g, dn_w_o, cf_conv_w,
                           cf_ln_g, cf_ln_b, cf_w_o, w_out, norm2_g, ffn_w_up, ffn_conv_w, ffn_w_down, final_norm_g]))
    ms = dict(zip(names, [m_w_ada, m_b_ada, m_norm1_g, m_w_in, m_dn_conv_w, m_dn_a_log, m_dn_dt_bias, m_dn_norm_g,
                          m_dn_w_o, m_cf_conv_w, m_cf_ln_g, m_cf_ln_b, m_cf_w_o, m_w_out, m_norm2_g, m_ffn_w_up,
                          m_ffn_conv_w, m_ffn_w_down, m_final_norm_g]))
    vs = dict(zip(names, [v_w_ada, v_b_ada, v_norm1_g, v_w_in, v_dn_conv_w, v_dn_a_log, v_dn_dt_bias, v_dn_norm_g,
                          v_dn_w_o, v_cf_conv_w, v_cf_ln_g, v_cf_ln_b, v_cf_w_o, v_w_out, v_norm2_g, v_ffn_w_up,
                          v_ffn_conv_w, v_ffn_w_down, v_final_norm_g]))
    grads, delta, new_m, new_v = {}, {}, {}, {}

    def adam_large(n, g):
        g = g.reshape(wts[n].shape[1:])
        if n == "w_ada":
            d_, m_, v_ = _adamw(wts[n][0], g, ms[n][0], vs[n][0], "adamw_" + n)
        else:
            d_, m_, v_, g = _adamw(wts[n][0], g, ms[n][0], vs[n][0], "adamw_" + n, emit_grad=True)
        grads[n], delta[n], new_m[n], new_v[n] = g[None], d_[None], m_[None], v_[None]

    def rs_finish(groups, after, tag):
        hs, nms_all = [], []
        for state, nms, t in groups:
            q16, r3s = _chip_scatter_wait(state, after, "rs_chips_wait_" + t)
            for q, r3, nm in zip(q16, r3s, nms):
                hs.append(_chip_sum(q, r3, slots, "rs_chip_sum_" + nm))
                nms_all.append(nm)
        for nm, g in zip(nms_all, _pair_gather(hs, "rs_pair_gather_" + tag)):
            adam_large(nm, g)

    rs_finish(rs_groups[:3], tok_d, "abc")

    ssum = _sum_slots(g3, "small_sum")
    dmod_all = _unpack_small(g3, sm2_shapes[:1])[0].reshape(8, 6 * D)
    (g_b_ada, gs_n1, gs_n2, gs_fn, gs_alp, gs_dtb, gs_dnng, gs_cflg, gs_cflb, gs_dncw, gs_cfcw,
     gs_ffcw) = _unpack_small(ssum, sm2_shapes)
    gs_alog, gs_dtbias = gs_alp[:, NH:2 * NH], gs_dtb[:, NH:2 * NH]
    gs_dncw = lax.dynamic_slice(gs_dncw, (0, chip * (3 * DNW // 4)), (KDN, 3 * DNW // 4))
    gs_cfcw = lax.dynamic_slice(gs_cfcw, (0, chip * (CFW // 4)), (KCF, CFW // 4))
    gs_ffcw = lax.dynamic_slice(gs_ffcw, (0, chip * (FFN // 4)), (KFF, FFN // 4))

    dmod_sh = lax.dynamic_slice(dmod_all, (0, chip * CA), (8, CA))

    def wada_fn(cc, dm):
        return (_mm(_silu(cc), dm, TN),)

    g_w_ada = _ew(wada_fn, "ada_dw", (CA // tn_a,), [c_all, dmod_sh],
                  [_full((8, D)), pl.BlockSpec((8, tn_a), lambda j: (0, j))],
                  [((D, CA), F32, pl.BlockSpec((D, tn_a), lambda j: (0, j)))])[0]

    loss = lax.psum(loss_v[0, 0], ("x", "y", "c"))

    adam_large("w_ada", g_w_ada)
    small_grads = {"b_ada": g_b_ada, "norm1_g": gs_n1, "dn_conv_w": gs_dncw, "dn_a_log": gs_alog,
                   "dn_dt_bias": gs_dtbias, "dn_norm_g": gs_dnng, "cf_conv_w": gs_cfcw, "cf_ln_g": gs_cflg,
                   "cf_ln_b": gs_cflb, "norm2_g": gs_n2, "ffn_conv_w": gs_ffcw, "final_norm_g": gs_fn}
    small = [n for n in names if n in small_grads]
    for n in small:
        grads[n] = small_grads[n].reshape(wts[n].shape)
    sm_sh = [tuple(wts[n].shape) for n in small]
    d_, m_, v_ = _adamw(_pack_small([wts[n] for n in small]), _pack_small([grads[n] for n in small]),
                        _pack_small([ms[n] for n in small]), _pack_small([vs[n] for n in small]), "adamw_small")
    for n, a, b_, c_ in zip(small, _unpack_small(d_, sm_sh), _unpack_small(m_, sm_sh), _unpack_small(v_, sm_sh)):
        delta[n], new_m[n], new_v[n] = a, b_, c_
    behind = sum(delta[n][0, :8, :8] for n in ["w_ada", "ffn_w_up", "ffn_w_down", "w_out", "dn_w_o", "cf_w_o"])
    rs_finish(rs_groups[3:], behind, "d")

    return (loss, grad_x[None], *[grads[n] for n in names], *[delta[n] for n in names],
            *[new_m[n] for n in names], *[new_v[n] for n in names])
```
